```python
import math
import jax, jax.numpy as jnp
from jax import lax
import numpy as np


D_MODEL = 2048
BATCH = 8
SEQ = 8192
DEPTH = 2

EPS = 1e-6
ROPE_THETA = 10000.0

SSD_HEADS = 16
SSD_HEAD_DIM = 64
SSD_WIDTH = SSD_HEADS * SSD_HEAD_DIM
SSD_GROUPS = 2
SSD_STATE = 128
SSD_CONV = 4
SSD_CHUNK = 128
SSD_CONV_CH = SSD_WIDTH + 2 * SSD_GROUPS * SSD_STATE

ATT_HEADS = 8
ATT_HEAD_DIM = 64
ATT_WIDTH = ATT_HEADS * ATT_HEAD_DIM
DILATED_PAIRS = ((128, 1), (512, 4), (2048, 16))
ATT_BLOCK = 128

RET_HEADS = 4
RET_QK_DIM = 64
RET_V_DIM = 128
RET_QK_WIDTH = RET_HEADS * RET_QK_DIM
RET_V_WIDTH = RET_HEADS * RET_V_DIM
RET_CHUNK = 128

MIX_WIDTH = SSD_WIDTH + ATT_WIDTH + RET_V_WIDTH
IN_SPLITS = (SSD_WIDTH, SSD_CONV_CH, SSD_HEADS,
             ATT_WIDTH, ATT_WIDTH, ATT_WIDTH,
             RET_QK_WIDTH, RET_QK_WIDTH, RET_V_WIDTH, RET_V_WIDTH)
IN_WIDTH = sum(IN_SPLITS)
D_FF = ((8 * D_MODEL // 3 + 255) // 256) * 256

kernel_name = 'hybrid_ssd_dilated_retention_block'


def rms_norm(x, g):
    xf = x.astype(jnp.float32)
    y = xf * lax.rsqrt(jnp.mean(xf * xf, axis=-1, keepdims=True) + EPS)
    return (y * g.astype(jnp.float32)).astype(x.dtype)


def grouped_rms(x, groups, gain):
    b, s, w = x.shape
    xg = x.astype(jnp.float32).reshape(b, s, groups, w // groups)
    xg = xg * lax.rsqrt(jnp.mean(xg * xg, axis=-1, keepdims=True) + EPS)
    return xg.reshape(b, s, w) * gain.astype(jnp.float32)


def rope_tables(seq, dim):
    pos = jnp.arange(seq, dtype=jnp.float32)
    inv = ROPE_THETA ** (-jnp.arange(0, dim, 2, dtype=jnp.float32) / dim)
    ang = pos[:, None] * inv[None, :]
    return jnp.cos(ang), jnp.sin(ang)


def apply_rope(t, cos, sin):
    half = t.shape[-1] // 2
    t1, t2 = t[..., :half], t[..., half:]
    c = cos[None, :, None, :]
    s = sin[None, :, None, :]
    return jnp.concatenate([t1 * c - t2 * s, t1 * s + t2 * c], axis=-1)


def causal_dwconv(u, w, b):
    k = w.shape[0]
    s = u.shape[1]
    up = jnp.pad(u, ((0, 0), (k - 1, 0), (0, 0)))
    return b + sum(up[:, i:i + s] * w[i] for i in range(k))


def ssd_chunked(x, dt, a_neg, bm, cm):
    bsz, s, h, p = x.shape
    g, n = bm.shape[2], bm.shape[3]
    j = h // g
    l = SSD_CHUNK
    c = s // l
    xg = (x * dt[..., None]).reshape(bsz, c, l, g, j, p)
    da = (dt * a_neg).reshape(bsz, c, l, g, j).transpose(0, 1, 3, 4, 2)
    acum = jnp.cumsum(da, axis=-1)
    bc = bm.reshape(bsz, c, l, g, n)
    cc = cm.reshape(bsz, c, l, g, n)
    causal = jnp.tril(jnp.ones((l, l), dtype=bool))
    seg = acum[..., :, None] - acum[..., None, :]
    decay_in = jnp.exp(jnp.where(causal, seg, -jnp.inf))
    cb = jnp.einsum('bclgn,bcsgn->bcgls', cc, bc)
    y_diag = jnp.einsum('bcgjls,bcsgjp->bclgjp', cb[:, :, :, None] * decay_in, xg)
    decay_to_end = jnp.exp(acum[..., -1:] - acum)
    xw = xg * decay_to_end.transpose(0, 1, 4, 2, 3)[..., None]
    chunk_states = jnp.einsum('bclgn,bclgjp->bcgjpn', bc, xw)
    chunk_decay = jnp.exp(acum[..., -1])

    def step(state, inp):
        st_c, dec_c = inp
        return state * dec_c[..., None, None] + st_c, state

    init = jnp.zeros((bsz, g, j, p, n), jnp.float32)
    _, prev = lax.scan(step, init, (chunk_states.transpose(1, 0, 2, 3, 4, 5),
                                    chunk_decay.transpose(1, 0, 2, 3)))
    prev = prev.transpose(1, 0, 2, 3, 4, 5)
    y_off = jnp.einsum('bclgn,bcgjpn->bclgjp', cc, prev) * \
        jnp.exp(acum).transpose(0, 1, 4, 2, 3)[..., None]
    return (y_diag + y_off).reshape(bsz, s, h, p)


def dilated_branch(q, k, v, window, dilation):
    bsz, s, h, hd = q.shape
    steps = window // dilation
    blk = ATT_BLOCK
    L = s // dilation
    nb = -(-L // blk)
    pad = nb * blk - L

    def to_sub(t):
        t = t.reshape(bsz, L, dilation, h, hd).transpose(0, 2, 3, 1, 4)
        return jnp.pad(t, ((0, 0), (0, 0), (0, 0), (0, pad), (0, 0)))

    def key_blocks(t):
        t = jnp.pad(to_sub(t), ((0, 0), (0, 0), (0, 0), (blk, 0), (0, 0)))
        t = t.reshape(bsz, dilation, h, nb + 1, blk, hd)
        return jnp.concatenate([t[:, :, :, :-1], t[:, :, :, 1:]], axis=4)

    qb = to_sub(q).reshape(bsz, dilation, h, nb, blk, hd)
    kb = key_blocks(k)
    vb = key_blocks(v)
    sc = jnp.einsum('brhnqd,brhnkd->brhnqk', qb, kb).astype(jnp.float32)
    iq = jnp.arange(blk)[:, None]
    ik = jnp.arange(2 * blk)[None, :]
    dist = blk + iq - ik
    kpos = (jnp.arange(nb) * blk - blk)[:, None, None] + ik[None]
    valid = (dist >= 0) & (dist <= steps) & (kpos >= 0)
    sc = jnp.where(valid, sc, -jnp.inf)
    mx = jnp.max(sc, axis=-1, keepdims=True)
    pr = jnp.exp(sc - mx)
    den = jnp.sum(pr, axis=-1, keepdims=True)
    out = jnp.einsum('brhnqk,brhnkd->brhnqd', pr, vb.astype(jnp.float32)) / den
    lse = (mx + jnp.log(den))[..., 0]

    def from_sub(t):
        t = t.reshape(bsz, dilation, h, nb * blk, *t.shape[5:])[:, :, :, :L]
        t = jnp.moveaxis(t, 3, 1)
        return t.reshape(bsz, s, h, *t.shape[4:])

    return from_sub(out), from_sub(lse)


def retention_chunked(q, k, v):
    bsz, s, h, dk = q.shape
    dv = v.shape[-1]
    l = RET_CHUNK
    c = s // l
    log_gamma = jnp.log1p(-jnp.exp2(-5.0 - jnp.arange(h, dtype=jnp.float32)))
    idx = jnp.arange(l, dtype=jnp.float32)
    rel = idx[:, None] - idx[None, :]
    decay_in = jnp.where(rel >= 0,
                         jnp.exp(jnp.maximum(rel, 0.0)[None] * log_gamma[:, None, None]),
                         0.0)
    qc = q.reshape(bsz, c, l, h, dk)
    kc = k.reshape(bsz, c, l, h, dk)
    vc = v.reshape(bsz, c, l, h, dv)
    scores = jnp.einsum('bclhd,bcshd->bchls', qc, kc) * decay_in
    inner = jnp.einsum('bchls,bcshe->bclhe', scores, vc)
    k_to_end = jnp.exp((l - 1 - idx)[:, None] * log_gamma[None, :])
    kv = jnp.einsum('bclhd,bclhe->bchde', kc * k_to_end[:, :, None], vc)
    chunk_decay = jnp.exp(l * log_gamma)

    def step(state, kv_c):
        return state * chunk_decay[:, None, None] + kv_c, state

    init = jnp.zeros((bsz, h, dk, dv), jnp.float32)
    _, prev = lax.scan(step, init, kv.transpose(1, 0, 2, 3, 4))
    prev = prev.transpose(1, 0, 2, 3, 4)
    q_from_start = jnp.exp((idx + 1.0)[:, None] * log_gamma[None, :])
    cross = jnp.einsum('bclhd,bchde->bclhe', qc * q_from_start[:, :, None], prev)
    return (inner + cross).reshape(bsz, s, h, dv)


def hybrid_mixer(hn, w_in, conv_w, conv_b, dt_bias, a_log, d_skip, ssd_norm,
                 q_norm, k_norm, ret_norm, w_out, cos, sin):
    bsz, s, _ = hn.shape
    f32 = jnp.float32
    proj = (hn @ w_in).astype(f32)
    cuts = list(np.cumsum(IN_SPLITS)[:-1])
    z, xbc, dt_raw, aq, ak, av, rq, rk, rv, rg = jnp.split(proj, cuts, axis=-1)

    xbc = jax.nn.silu(causal_dwconv(xbc, conv_w.astype(f32), conv_b.astype(f32)))
    xs, bm, cm = jnp.split(xbc, [SSD_WIDTH, SSD_WIDTH + SSD_GROUPS * SSD_STATE], axis=-1)
    xs = xs.reshape(bsz, s, SSD_HEADS, SSD_HEAD_DIM)
    bm = bm.reshape(bsz, s, SSD_GROUPS, SSD_STATE)
    cm = cm.reshape(bsz, s, SSD_GROUPS, SSD_STATE)
    dt = jax.nn.softplus(dt_raw + dt_bias.astype(f32))
    a_neg = -jnp.exp(a_log.astype(f32))
    y_ssd = ssd_chunked(xs, dt, a_neg, bm, cm) + xs * d_skip.astype(f32)[:, None]
    y_ssd = grouped_rms(y_ssd.reshape(bsz, s, SSD_WIDTH) * jax.nn.silu(z), SSD_GROUPS, ssd_norm)

    aq = apply_rope(rms_norm(aq.reshape(bsz, s, ATT_HEADS, ATT_HEAD_DIM), q_norm), cos, sin)
    aq = aq * (ATT_HEAD_DIM ** -0.5)
    ak = apply_rope(rms_norm(ak.reshape(bsz, s, ATT_HEADS, ATT_HEAD_DIM), k_norm), cos, sin)
    av = av.reshape(bsz, s, ATT_HEADS, ATT_HEAD_DIM)
    branches = [dilated_branch(aq, ak, av, w, d) for (w, d) in DILATED_PAIRS]
    outs = jnp.stack([o for o, _ in branches])
    wts = jax.nn.softmax(jnp.stack([lse for _, lse in branches]), axis=0)
    y_att = jnp.sum(wts[..., None] * outs, axis=0).reshape(bsz, s, ATT_WIDTH)

    rq = apply_rope(rq.reshape(bsz, s, RET_HEADS, RET_QK_DIM), cos, sin)
    rk = apply_rope(rk.reshape(bsz, s, RET_HEADS, RET_QK_DIM), cos, sin) * (RET_QK_DIM ** -0.5)
    rv = rv.reshape(bsz, s, RET_HEADS, RET_V_DIM)
    y_ret = retention_chunked(rq, rk, rv).reshape(bsz, s, RET_V_WIDTH)
    y_ret = grouped_rms(y_ret, RET_HEADS, ret_norm) * jax.nn.silu(rg)

    y = jnp.concatenate([y_ssd, y_att, y_ret], axis=-1).astype(hn.dtype)
    return y @ w_out


def swiglu(hn, w_gate, w_up, w_down):
    return (jax.nn.silu(hn @ w_gate) * (hn @ w_up)) @ w_down


def _fwd_setup_inputs(seed: int = 0) -> dict:
    key = jax.random.key(seed)
    ks = jax.random.split(key, 17)
    f32 = jnp.float32

    def nrm(k, shape, scale):
        return jax.random.normal(k, shape, f32) * scale

    x = nrm(ks[0], (BATCH, SEQ, D_MODEL), 1.0)
    ln_mix = 1.0 + nrm(ks[1], (DEPTH, D_MODEL), 0.02)
    w_in = nrm(ks[2], (DEPTH, D_MODEL, IN_WIDTH), D_MODEL ** -0.5)
    conv_w = nrm(ks[3], (DEPTH, SSD_CONV, SSD_CONV_CH), SSD_CONV ** -0.5)
    conv_b = nrm(ks[4], (DEPTH, SSD_CONV_CH), 0.01)
    dt0 = jnp.exp(jax.random.uniform(ks[5], (DEPTH, SSD_HEADS), f32,
                                     math.log(1e-3), math.log(1e-1)))
    dt_bias = dt0 + jnp.log(-jnp.expm1(-dt0))
    a_log = jnp.log(jax.random.uniform(ks[6], (DEPTH, SSD_HEADS), f32, 1.0, 16.0))
    d_skip = 1.0 + nrm(ks[7], (DEPTH, SSD_HEADS), 0.1)
    ssd_norm = 1.0 + nrm(ks[8], (DEPTH, SSD_WIDTH), 0.02)
    q_norm = 1.0 + nrm(ks[9], (DEPTH, ATT_HEAD_DIM), 0.02)
    k_norm = 1.0 + nrm(ks[10], (DEPTH, ATT_HEAD_DIM), 0.02)
    ret_norm = 1.0 + nrm(ks[11], (DEPTH, RET_V_WIDTH), 0.02)
    w_out = nrm(ks[12], (DEPTH, MIX_WIDTH, D_MODEL), MIX_WIDTH ** -0.5)
    ln_ffn = 1.0 + nrm(ks[13], (DEPTH, D_MODEL), 0.02)
    w_gate = nrm(ks[14], (DEPTH, D_MODEL, D_FF), D_MODEL ** -0.5)
    w_up = nrm(ks[15], (DEPTH, D_MODEL, D_FF), D_MODEL ** -0.5)
    w_down = nrm(ks[16], (DEPTH, D_FF, D_MODEL), D_FF ** -0.5)
    return {'x': x, 'ln_mix': ln_mix, 'w_in': w_in, 'conv_w': conv_w, 'conv_b': conv_b,
            'dt_bias': dt_bias, 'a_log': a_log, 'd_skip': d_skip, 'ssd_norm': ssd_norm,
            'q_norm': q_norm, 'k_norm': k_norm, 'ret_norm': ret_norm, 'w_out': w_out,
            'ln_ffn': ln_ffn, 'w_gate': w_gate, 'w_up': w_up, 'w_down': w_down}


def _fwd_reference(x, ln_mix, w_in, conv_w, conv_b, dt_bias, a_log, d_skip, ssd_norm,
              q_norm, k_norm, ret_norm, w_out, ln_ffn, w_gate, w_up, w_down):
    cos, sin = rope_tables(x.shape[1], ATT_HEAD_DIM)
    for i in range(DEPTH):
        x = x + hybrid_mixer(rms_norm(x, ln_mix[i]), w_in[i], conv_w[i], conv_b[i],
                             dt_bias[i], a_log[i], d_skip[i], ssd_norm[i],
                             q_norm[i], k_norm[i], ret_norm[i], w_out[i], cos, sin)
        x = x + swiglu(rms_norm(x, ln_ffn[i]), w_gate[i], w_up[i], w_down[i])
    return x


import jax as _jax
import jax.numpy as _jnp

TWIN_FORMAT = 'train_step'
FWD_PARAMS = ['x', 'ln_mix', 'w_in', 'conv_w', 'conv_b', 'dt_bias', 'a_log', 'd_skip', 'ssd_norm', 'q_norm', 'k_norm', 'ret_norm', 'w_out', 'ln_ffn', 'w_gate', 'w_up', 'w_down']
TWIN_WEIGHTS = ['ln_mix', 'w_in', 'conv_w', 'conv_b', 'dt_bias', 'a_log', 'd_skip', 'ssd_norm', 'q_norm', 'k_norm', 'ret_norm', 'w_out', 'ln_ffn', 'w_gate', 'w_up', 'w_down']
TWIN_DIFF_INPUT = 'x'
TWIN_INPUTS = ['x', 'ln_mix', 'w_in', 'conv_w', 'conv_b', 'dt_bias', 'a_log', 'd_skip', 'ssd_norm', 'q_norm', 'k_norm', 'ret_norm', 'w_out', 'ln_ffn', 'w_gate', 'w_up', 'w_down', 'loss_target', 'm_ln_mix', 'm_w_in', 'm_conv_w', 'm_conv_b', 'm_dt_bias', 'm_a_log', 'm_d_skip', 'm_ssd_norm', 'm_q_norm', 'm_k_norm', 'm_ret_norm', 'm_w_out', 'm_ln_ffn', 'm_w_gate', 'm_w_up', 'm_w_down', 'v_ln_mix', 'v_w_in', 'v_conv_w', 'v_conv_b', 'v_dt_bias', 'v_a_log', 'v_d_skip', 'v_ssd_norm', 'v_q_norm', 'v_k_norm', 'v_ret_norm', 'v_w_out', 'v_ln_ffn', 'v_w_gate', 'v_w_up', 'v_w_down']
TWIN_OUTPUTS = ['loss', 'grad_x', 'grad_ln_mix', 'grad_w_in', 'grad_conv_w', 'grad_conv_b', 'grad_dt_bias', 'grad_a_log', 'grad_d_skip', 'grad_ssd_norm', 'grad_q_norm', 'grad_k_norm', 'grad_ret_norm', 'grad_w_out', 'grad_ln_ffn', 'grad_w_gate', 'grad_w_up', 'grad_w_down', 'delta_ln_mix', 'delta_w_in', 'delta_conv_w', 'delta_conv_b', 'delta_dt_bias', 'delta_a_log', 'delta_d_skip', 'delta_ssd_norm', 'delta_q_norm', 'delta_k_norm', 'delta_ret_norm', 'delta_w_out', 'delta_ln_ffn', 'delta_w_gate', 'delta_w_up', 'delta_w_down', 'new_m_ln_mix', 'new_m_w_in', 'new_m_conv_w', 'new_m_conv_b', 'new_m_dt_bias', 'new_m_a_log', 'new_m_d_skip', 'new_m_ssd_norm', 'new_m_q_norm', 'new_m_k_norm', 'new_m_ret_norm', 'new_m_w_out', 'new_m_ln_ffn', 'new_m_w_gate', 'new_m_w_up', 'new_m_w_down', 'new_v_ln_mix', 'new_v_w_in', 'new_v_conv_w', 'new_v_conv_b', 'new_v_dt_bias', 'new_v_a_log', 'new_v_d_skip', 'new_v_ssd_norm', 'new_v_q_norm', 'new_v_k_norm', 'new_v_ret_norm', 'new_v_w_out', 'new_v_ln_ffn', 'new_v_w_gate', 'new_v_w_up', 'new_v_w_down']
TWIN_LEAF_KINDS = {'loss': 'loss', 'grad_x': 'grad_x', 'grad_ln_mix': 'grad_w', 'grad_w_in': 'grad_w', 'grad_conv_w': 'grad_w', 'grad_conv_b': 'grad_w', 'grad_dt_bias': 'grad_w', 'grad_a_log': 'grad_w', 'grad_d_skip': 'grad_w', 'grad_ssd_norm': 'grad_w', 'grad_q_norm': 'grad_w', 'grad_k_norm': 'grad_w', 'grad_ret_norm': 'grad_w', 'grad_w_out': 'grad_w', 'grad_ln_ffn': 'grad_w', 'grad_w_gate': 'grad_w', 'grad_w_up': 'grad_w', 'grad_w_down': 'grad_w', 'delta_ln_mix': 'delta_w', 'delta_w_in': 'delta_w', 'delta_conv_w': 'delta_w', 'delta_conv_b': 'delta_w', 'delta_dt_bias': 'delta_w', 'delta_a_log': 'delta_w', 'delta_d_skip': 'delta_w', 'delta_ssd_norm': 'delta_w', 'delta_q_norm': 'delta_w', 'delta_k_norm': 'delta_w', 'delta_ret_norm': 'delta_w', 'delta_w_out': 'delta_w', 'delta_ln_ffn': 'delta_w', 'delta_w_gate': 'delta_w', 'delta_w_up': 'delta_w', 'delta_w_down': 'delta_w', 'new_m_ln_mix': 'new_m', 'new_m_w_in': 'new_m', 'new_m_conv_w': 'new_m', 'new_m_conv_b': 'new_m', 'new_m_dt_bias': 'new_m', 'new_m_a_log': 'new_m', 'new_m_d_skip': 'new_m', 'new_m_ssd_norm': 'new_m', 'new_m_q_norm': 'new_m', 'new_m_k_norm': 'new_m', 'new_m_ret_norm': 'new_m', 'new_m_w_out': 'new_m', 'new_m_ln_ffn': 'new_m', 'new_m_w_gate': 'new_m', 'new_m_w_up': 'new_m', 'new_m_w_down': 'new_m', 'new_v_ln_mix': 'new_v', 'new_v_w_in': 'new_v', 'new_v_conv_w': 'new_v', 'new_v_conv_b': 'new_v', 'new_v_dt_bias': 'new_v', 'new_v_a_log': 'new_v', 'new_v_d_skip': 'new_v', 'new_v_ssd_norm': 'new_v', 'new_v_q_norm': 'new_v', 'new_v_k_norm': 'new_v', 'new_v_ret_norm': 'new_v', 'new_v_w_out': 'new_v', 'new_v_ln_ffn': 'new_v', 'new_v_w_gate': 'new_v', 'new_v_w_up': 'new_v', 'new_v_w_down': 'new_v'}


def _forward(args):
    return _fwd_reference(*[args[k] for k in FWD_PARAMS])


def _output_shape():
    def fwd():
        inp = _fwd_setup_inputs(0)
        return _fwd_reference(*[inp[k] for k in FWD_PARAMS])
    out = _jax.eval_shape(fwd)
    return out.shape, out.dtype

N_MICROBATCH = 1
ADAM_LR = 0.001
ADAM_B1 = 0.9
ADAM_B2 = 0.999
ADAM_EPS = 1e-08
ADAM_WD = 0.01
ADAM_STEP = 10
PER_EXAMPLE_BATCH_AXIS = {'x': 0, 'loss_target': 0}
SHARED_INPUTS = []
_WEIGHT_DTYPES = {'ln_mix': _jnp.float32, 'w_in': _jnp.float32, 'conv_w': _jnp.float32, 'conv_b': _jnp.float32, 'dt_bias': _jnp.float32, 'a_log': _jnp.float32, 'd_skip': _jnp.float32, 'ssd_norm': _jnp.float32, 'q_norm': _jnp.float32, 'k_norm': _jnp.float32, 'ret_norm': _jnp.float32, 'w_out': _jnp.float32, 'ln_ffn': _jnp.float32, 'w_gate': _jnp.float32, 'w_up': _jnp.float32, 'w_down': _jnp.float32}
MOMENT_SCALE = {'ln_mix': 2.915486e+00, 'w_in': 3.631148e-01, 'conv_w': 1.030987e+00, 'conv_b': 3.307137e+00, 'dt_bias': 1.811971e+00, 'a_log': 6.284011e+00, 'd_skip': 8.015293e+00, 'ssd_norm': 4.322921e+01, 'q_norm': 1.540534e+00, 'k_norm': 1.542914e+00, 'ret_norm': 1.142460e+01, 'w_out': 1.169177e+00, 'ln_ffn': 2.472840e+01, 'w_gate': 2.946435e-01, 'w_up': 1.921169e-01, 'w_down': 3.005627e-01}


def _to_microbatches(a, axis):
    t = _jnp.moveaxis(a, axis, 0)
    t = t.reshape((N_MICROBATCH, t.shape[0] // N_MICROBATCH) + t.shape[1:])
    return _jnp.moveaxis(t, 1, axis + 1)


def setup_inputs(seed: int = 0) -> dict:
    inp = _fwd_setup_inputs(seed)
    key = _jax.random.fold_in(_jax.random.key(seed), 7919)
    shape, _ = _output_shape()
    out = dict(inp)
    out["loss_target"] = _jax.random.normal(_jax.random.fold_in(key, 0), shape, _jnp.float32)
    for i, name in enumerate(TWIN_WEIGHTS):
        w = inp[name].astype(_jnp.float32)
        if MOMENT_SCALE is None:
            s = _jnp.sqrt(_jnp.mean(_jnp.square(w)) + 1e-30)
        else:
            s = MOMENT_SCALE[name]
        km, kv = _jax.random.split(_jax.random.fold_in(key, i + 1))
        out[name] = w
        out["m_" + name] = s * _jax.random.normal(km, w.shape, _jnp.float32)
        out["v_" + name] = (s * s) * _jax.random.uniform(kv, w.shape, _jnp.float32, 0.5, 1.5)
    if N_MICROBATCH > 1:
        for name, axis in PER_EXAMPLE_BATCH_AXIS.items():
            out[name] = _to_microbatches(out[name], axis)
    return {'x': out['x'], 'ln_mix': out['ln_mix'], 'w_in': out['w_in'], 'conv_w': out['conv_w'], 'conv_b': out['conv_b'], 'dt_bias': out['dt_bias'], 'a_log': out['a_log'], 'd_skip': out['d_skip'], 'ssd_norm': out['ssd_norm'], 'q_norm': out['q_norm'], 'k_norm': out['k_norm'], 'ret_norm': out['ret_norm'], 'w_out': out['w_out'], 'ln_ffn': out['ln_ffn'], 'w_gate': out['w_gate'], 'w_up': out['w_up'], 'w_down': out['w_down'], 'loss_target': out['loss_target'], 'm_ln_mix': out['m_ln_mix'], 'm_w_in': out['m_w_in'], 'm_conv_w': out['m_conv_w'], 'm_conv_b': out['m_conv_b'], 'm_dt_bias': out['m_dt_bias'], 'm_a_log': out['m_a_log'], 'm_d_skip': out['m_d_skip'], 'm_ssd_norm': out['m_ssd_norm'], 'm_q_norm': out['m_q_norm'], 'm_k_norm': out['m_k_norm'], 'm_ret_norm': out['m_ret_norm'], 'm_w_out': out['m_w_out'], 'm_ln_ffn': out['m_ln_ffn'], 'm_w_gate': out['m_w_gate'], 'm_w_up': out['m_w_up'], 'm_w_down': out['m_w_down'], 'v_ln_mix': out['v_ln_mix'], 'v_w_in': out['v_w_in'], 'v_conv_w': out['v_conv_w'], 'v_conv_b': out['v_conv_b'], 'v_dt_bias': out['v_dt_bias'], 'v_a_log': out['v_a_log'], 'v_d_skip': out['v_d_skip'], 'v_ssd_norm': out['v_ssd_norm'], 'v_q_norm': out['v_q_norm'], 'v_k_norm': out['v_k_norm'], 'v_ret_norm': out['v_ret_norm'], 'v_w_out': out['v_w_out'], 'v_ln_ffn': out['v_ln_ffn'], 'v_w_gate': out['v_w_gate'], 'v_w_up': out['v_w_up'], 'v_w_down': out['v_w_down']}


def _loss(weights, diff, rest, loss_target):
    with _jax.named_scope("forward"):
        args = {**rest, TWIN_DIFF_INPUT: diff, **{k: w.astype(_WEIGHT_DTYPES[k]) for k, w in weights.items()}}
        y = _forward(args)
    with _jax.named_scope("loss_head"):
        err = _jnp.square(y.astype(_jnp.float32) - loss_target)
        return 0.5 * _jnp.sum(_jnp.mean(err, axis=-1)) if err.ndim else 0.5 * err


def _adamw(w, g, m, v):
    m = ADAM_B1 * m + (1.0 - ADAM_B1) * g
    v = ADAM_B2 * v + (1.0 - ADAM_B2) * _jnp.square(g)
    m_hat = m / (1.0 - ADAM_B1 ** ADAM_STEP)
    v_hat = v / (1.0 - ADAM_B2 ** ADAM_STEP)
    delta = -ADAM_LR * (m_hat / (_jnp.sqrt(v_hat) + ADAM_EPS) + ADAM_WD * w)
    return delta, m, v


def reference(x, ln_mix, w_in, conv_w, conv_b, dt_bias, a_log, d_skip, ssd_norm, q_norm, k_norm, ret_norm, w_out, ln_ffn, w_gate, w_up, w_down, loss_target, m_ln_mix, m_w_in, m_conv_w, m_conv_b, m_dt_bias, m_a_log, m_d_skip, m_ssd_norm, m_q_norm, m_k_norm, m_ret_norm, m_w_out, m_ln_ffn, m_w_gate, m_w_up, m_w_down, v_ln_mix, v_w_in, v_conv_w, v_conv_b, v_dt_bias, v_a_log, v_d_skip, v_ssd_norm, v_q_norm, v_k_norm, v_ret_norm, v_w_out, v_ln_ffn, v_w_gate, v_w_up, v_w_down):
    given = dict(x=x, ln_mix=ln_mix, w_in=w_in, conv_w=conv_w, conv_b=conv_b, dt_bias=dt_bias, a_log=a_log, d_skip=d_skip, ssd_norm=ssd_norm, q_norm=q_norm, k_norm=k_norm, ret_norm=ret_norm, w_out=w_out, ln_ffn=ln_ffn, w_gate=w_gate, w_up=w_up, w_down=w_down, loss_target=loss_target, m_ln_mix=m_ln_mix, m_w_in=m_w_in, m_conv_w=m_conv_w, m_conv_b=m_conv_b, m_dt_bias=m_dt_bias, m_a_log=m_a_log, m_d_skip=m_d_skip, m_ssd_norm=m_ssd_norm, m_q_norm=m_q_norm, m_k_norm=m_k_norm, m_ret_norm=m_ret_norm, m_w_out=m_w_out, m_ln_ffn=m_ln_ffn, m_w_gate=m_w_gate, m_w_up=m_w_up, m_w_down=m_w_down, v_ln_mix=v_ln_mix, v_w_in=v_w_in, v_conv_w=v_conv_w, v_conv_b=v_conv_b, v_dt_bias=v_dt_bias, v_a_log=v_a_log, v_d_skip=v_d_skip, v_ssd_norm=v_ssd_norm, v_q_norm=v_q_norm, v_k_norm=v_k_norm, v_ret_norm=v_ret_norm, v_w_out=v_w_out, v_ln_ffn=v_ln_ffn, v_w_gate=v_w_gate, v_w_up=v_w_up, v_w_down=v_w_down)
    weights = {n: given[n] for n in TWIN_WEIGHTS}
    shared = {n: given[n] for n in SHARED_INPUTS}
    per_example = {n: given[n] for n in ['x']}
    grad_fn = _jax.value_and_grad(_loss, argnums=(0, 1))

    def one_microbatch(ex, loss_target):
        ex = dict(ex)
        diff = ex.pop(TWIN_DIFF_INPUT)
        return grad_fn(weights, diff, {**shared, **ex}, loss_target)

    if N_MICROBATCH == 1:
        loss, (grad_w, grad_x) = one_microbatch(per_example, given["loss_target"])
    else:
        def body(carry, xs):
            loss_sum, grad_sum = carry
            l_k, (gw_k, gx_k) = one_microbatch(xs[0], xs[1])
            with _jax.named_scope("update"):
                return (loss_sum + l_k, _jax.tree.map(_jnp.add, grad_sum, gw_k)), gx_k

        init = (_jnp.zeros((), _jnp.float32), _jax.tree.map(_jnp.zeros_like, weights))
        (loss, grad_w), grad_x = _jax.lax.scan(body, init, (per_example, given["loss_target"]))
    with _jax.named_scope("update"):
        delta_w, new_m, new_v = {}, {}, {}
        for n in TWIN_WEIGHTS:
            delta_w[n], new_m[n], new_v[n] = _adamw(weights[n], grad_w[n], given["m_" + n], given["v_" + n])
    return (loss, grad_x, *[grad_w[n] for n in TWIN_WEIGHTS], *[delta_w[n] for n in TWIN_WEIGHTS],
            *[new_m[n] for n in TWIN_WEIGHTS], *[new_v[n] for n in TWIN_WEIGHTS])
```

```python
import functools
import math

import jax
import jax.numpy as jnp
import numpy as np
from jax import lax
from jax.experimental import pallas as pl
from jax.experimental.pallas import tpu as pltpu

F32 = jnp.float32
BF16 = jnp.bfloat16

N_DEV = 8
MESH_AXES = ("x", "y", "c")
D_MODEL = 2048
DEPTH = 2
EPS = 1e-6
ROPE_THETA = 10000.0
SSD_HEADS = 16
SSD_HEAD_DIM = 64
SSD_WIDTH = 1024
SSD_GROUPS = 2
SSD_STATE = 128
SSD_CONV = 4
SSD_CONV_CH = 1536
ATT_HEADS = 8
ATT_HEAD_DIM = 64
ATT_WIDTH = 512
DILATED_PAIRS = ((128, 1), (512, 4), (2048, 16))
RET_HEADS = 4
RET_QK_DIM = 64
RET_V_DIM = 128
RET_QK_WIDTH = 256
RET_V_WIDTH = 512
CHUNK = 128
ATT_SPAN = 2048
ATT_STRIP = ATT_SPAN + CHUNK
IN_WIDTH = 5648
IN_PAD = 5760
DT_COL = 5632
D_FF = 5632
ADAM_LR = 0.001
ADAM_B1 = 0.9
ADAM_B2 = 0.999
ADAM_EPS = 1e-08
ADAM_WD = 0.01
ADAM_STEP = 10
NEG = -1e30
VMEM_LIMIT_V7X = 60 * 1024 * 1024

NN = (((1,), (0,)), ((), ()))
NT = (((1,), (1,)), ((), ()))
TN = (((0,), (0,)), ((), ()))


def _bdot(a, b, dims):
    return lax.dot_general(a.astype(BF16), b.astype(BF16), dims, preferred_element_type=F32)


def _xdot(a, b, dims):
    return lax.dot_general(a, b, dims, precision=lax.Precision.HIGHEST, preferred_element_type=F32)


def _params(*sem):
    return pltpu.CompilerParams(dimension_semantics=sem, vmem_limit_bytes=VMEM_LIMIT_V7X)


def _sigmoid(v):
    return 1.0 / (1.0 + jnp.exp(-v))


def _silu_grad(v, s):
    return s * (1.0 + v * (1.0 - s))


def _col_of_row(row, eye):
    return jnp.sum(jnp.where(eye, row, 0.0), axis=1, keepdims=True)


def _rmsnorm_fwd(x, g, name):
    S, D = x.shape
    tr = min(512, S)

    def body(x_ref, g_ref, o_ref):
        xv = x_ref[...]
        r = lax.rsqrt(jnp.mean(xv * xv, axis=-1, keepdims=True) + EPS)
        o_ref[...] = (xv * r * g_ref[...]).astype(o_ref.dtype)

    return pl.pallas_call(
        body, grid=(S // tr,),
        in_specs=[pl.BlockSpec((tr, D), lambda i: (i, 0)), pl.BlockSpec((1, D), lambda i: (0, 0))],
        out_specs=pl.BlockSpec((tr, D), lambda i: (i, 0)),
        out_shape=jax.ShapeDtypeStruct((S, D), BF16), name=name, compiler_params=_params("parallel"),
    )(x, g)


def _rmsnorm_bwd(x, dy, g, dres, name):
    S, D = x.shape
    tr = min(512, S)

    def body(x_ref, dy_ref, g_ref, dres_ref, dx_ref, dg_ref):
        i = pl.program_id(0)
        xv = x_ref[...]
        r = lax.rsqrt(jnp.mean(xv * xv, axis=-1, keepdims=True) + EPS)
        n = xv * r
        dy = dy_ref[...]
        dn = dy * g_ref[...]
        dx_ref[...] = dres_ref[...] + r * (dn - n * jnp.mean(dn * n, axis=-1, keepdims=True))
        part = jnp.sum(dy * n, axis=0, keepdims=True)

        @pl.when(i == 0)
        def _():
            dg_ref[...] = part

        @pl.when(i > 0)
        def _():
            dg_ref[...] += part

    row = pl.BlockSpec((tr, D), lambda i: (i, 0))
    vec = pl.BlockSpec((1, D), lambda i: (0, 0))
    return pl.pallas_call(
        body, grid=(S // tr,), in_specs=[row, row, vec, row], out_specs=[row, vec],
        out_shape=[jax.ShapeDtypeStruct((S, D), F32), jax.ShapeDtypeStruct((1, D), F32)],
        name=name, compiler_params=_params("arbitrary"),
    )(x, dy, g, dres)


def _loss_grad(y, tgt, name):
    S, D = y.shape
    tr = min(512, S)

    def body(y_ref, t_ref, dy_ref, l_ref):
        i = pl.program_id(0)
        err = y_ref[...] - t_ref[...]
        dy_ref[...] = err * (1.0 / D)
        part = jnp.sum(jnp.sum(err * err, axis=1, keepdims=True), axis=0, keepdims=True) * (0.5 / D)

        @pl.when(i == 0)
        def _():
            l_ref[...] = jnp.zeros_like(l_ref)

        l_ref[...] += part

    row = pl.BlockSpec((tr, D), lambda i: (i, 0))
    return pl.pallas_call(
        body, grid=(S // tr,), in_specs=[row, row],
        out_specs=[row, pl.BlockSpec((8, 128), lambda i: (0, 0))],
        out_shape=[jax.ShapeDtypeStruct((S, D), F32), jax.ShapeDtypeStruct((8, 128), F32)],
        name=name, compiler_params=_params("arbitrary"),
    )(y, tgt)


def _pick(n, cands):
    for c in cands:
        if n % c == 0:
            return c
    return n


def _mm(a, b, mode, name, out_dtype=F32, residual=None, tm=None, tn=None, tk=None):
    if mode == "nn":
        (M, K), (_, N) = a.shape, b.shape
    elif mode == "nt":
        (M, K), (N, _) = a.shape, b.shape
    else:
        (K, M), (_, N) = a.shape, b.shape
    tm = min(tm, M) if tm else _pick(M, (1024, 512, 256, 128))
    tn = min(tn, N) if tn else _pick(N, (1024, 1152, 1408, 512, 256, 128))
    tk = min(tk, K) if tk else _pick(K, (2048, 1920, 1408, 1024, 512, 256, 128))
    assert M % tm == 0 and N % tn == 0 and K % tk == 0, (name, M, N, K, tm, tn, tk)
    nk = K // tk
    a_spec = pl.BlockSpec((tk, tm), lambda i, j, k: (k, i)) if mode == "tn" else pl.BlockSpec((tm, tk), lambda i, j, k: (i, k))
    b_spec = pl.BlockSpec((tn, tk), lambda i, j, k: (j, k)) if mode == "nt" else pl.BlockSpec((tk, tn), lambda i, j, k: (k, j))
    o_spec = pl.BlockSpec((tm, tn), lambda i, j, k: (i, j))
    dims = {"nn": NN, "nt": NT, "tn": TN}[mode]
    has_res = residual is not None

    def body(*refs):
        a_ref, b_ref = refs[0], refs[1]
        r_ref = refs[2] if has_res else None
        o_ref = refs[3] if has_res else refs[2]
        p = _bdot(a_ref[...], b_ref[...], dims)

        def finish(acc):
            if has_res:
                acc = acc + r_ref[...]
            o_ref[...] = acc.astype(o_ref.dtype)

        if nk == 1:
            finish(p)
        else:
            acc_ref = refs[-1]
            k = pl.program_id(2)

            @pl.when(k == 0)
            def _():
                acc_ref[...] = p

            @pl.when(k > 0)
            def _():
                acc_ref[...] += p

            @pl.when(k == nk - 1)
            def _():
                finish(acc_ref[...])

    ins = [a, b] + ([residual] if has_res else [])
    in_specs = [a_spec, b_spec] + ([o_spec] if has_res else [])
    scratch = [pltpu.VMEM((tm, tn), F32)] if nk > 1 else []
    return pl.pallas_call(
        body, grid=(M // tm, N // tn, nk), in_specs=in_specs, out_specs=o_spec,
        out_shape=jax.ShapeDtypeStruct((M, N), out_dtype), scratch_shapes=scratch, name=name,
        compiler_params=_params("parallel", "parallel", "arbitrary"),
    )(*ins)


def _swiglu_fwd(hn, wg, wu, name):
    S, K = hn.shape
    F = wg.shape[1]
    tm = _pick(S, (1024, 512))
    tn = _pick(F, (512, 256, 128))

    def body(a_ref, wg_ref, wu_ref, g_ref, u_ref, act_ref):
        a = a_ref[...]
        g = _bdot(a, wg_ref[...], NN)
        u = _bdot(a, wu_ref[...], NN)
        g_ref[...] = g.astype(BF16)
        u_ref[...] = u.astype(BF16)
        act_ref[...] = (g * _sigmoid(g) * u).astype(BF16)

    w_spec = pl.BlockSpec((K, tn), lambda i, j: (0, j))
    o_spec = pl.BlockSpec((tm, tn), lambda i, j: (i, j))
    sh = jax.ShapeDtypeStruct((S, F), BF16)
    return pl.pallas_call(
        body, grid=(S // tm, F // tn), in_specs=[pl.BlockSpec((tm, K), lambda i, j: (i, 0)), w_spec, w_spec],
        out_specs=[o_spec, o_spec, o_spec], out_shape=[sh, sh, sh], name=name,
        compiler_params=_params("parallel", "parallel"),
    )(hn, wg, wu)


def _swiglu_bwd(dx, wd, g, u, name):
    S, K = dx.shape
    F = wd.shape[0]
    tm = _pick(S, (1024, 512))
    tn = _pick(F, (512, 256, 128))

    def body(dx_ref, wd_ref, g_ref, u_ref, dg_ref, du_ref):
        da = _bdot(dx_ref[...], wd_ref[...], NT)
        gv = g_ref[...].astype(F32)
        uv = u_ref[...].astype(F32)
        s = _sigmoid(gv)
        dg_ref[...] = (da * uv * _silu_grad(gv, s)).astype(BF16)
        du_ref[...] = (da * gv * s).astype(BF16)

    o_spec = pl.BlockSpec((tm, tn), lambda i, j: (i, j))
    sh = jax.ShapeDtypeStruct((S, F), BF16)
    return pl.pallas_call(
        body, grid=(S // tm, F // tn),
        in_specs=[pl.BlockSpec((tm, K), lambda i, j: (i, 0)), pl.BlockSpec((tn, K), lambda i, j: (j, 0)), o_spec, o_spec],
        out_specs=[o_spec, o_spec], out_shape=[sh, sh], name=name, compiler_params=_params("parallel", "parallel"),
    )(dx, wd, g, u)


def _mm_nt2(a1, b1, a2, b2, name):
    M, K = a1.shape
    N = b1.shape[0]
    tm = _pick(M, (1024, 512))
    tn = _pick(N, (1024, 512))
    tk = _pick(K, (1408, 1024, 512, 256, 128))
    nk = K // tk

    def body(a1_ref, b1_ref, a2_ref, b2_ref, o_ref, acc_ref):
        k = pl.program_id(2)
        p = _bdot(a1_ref[...], b1_ref[...], NT) + _bdot(a2_ref[...], b2_ref[...], NT)

        @pl.when(k == 0)
        def _():
            acc_ref[...] = p

        @pl.when(k > 0)
        def _():
            acc_ref[...] += p

        @pl.when(k == nk - 1)
        def _():
            o_ref[...] = acc_ref[...]

    a_spec = pl.BlockSpec((tm, tk), lambda i, j, k: (i, k))
    b_spec = pl.BlockSpec((tn, tk), lambda i, j, k: (j, k))
    return pl.pallas_call(
        body, grid=(M // tm, N // tn, nk), in_specs=[a_spec, b_spec, a_spec, b_spec],
        out_specs=pl.BlockSpec((tm, tn), lambda i, j, k: (i, j)), out_shape=jax.ShapeDtypeStruct((M, N), F32),
        scratch_shapes=[pltpu.VMEM((tm, tn), F32)], name=name,
        compiler_params=_params("parallel", "parallel", "arbitrary"),
    )(a1, b1, a2, b2)


XBC_BLK0 = SSD_WIDTH // 128


def _conv_fwd(proj, w, b, name):
    S = proj.shape[0]
    T = min(512, S)

    def body(x_ref, w_ref, b_ref, o_ref, xp_ref):
        xp_ref[pl.ds(0, 8), :] = jnp.zeros((8, 128), F32)
        xp_ref[pl.ds(8, S), :] = x_ref[...]
        wv = w_ref[...]
        bv = b_ref[...]

        def step(c, carry):
            base = pl.multiple_of(c * T, T)
            acc = wv[0:1] * xp_ref[pl.ds(base + 5, T), :]
            for i in range(1, SSD_CONV):
                acc = acc + wv[i:i + 1] * xp_ref[pl.ds(base + 5 + i, T), :]
            acc = bv + acc
            o_ref[pl.ds(base, T), :] = acc * _sigmoid(acc)
            return carry

        lax.fori_loop(0, S // T, step, 0)

    return pl.pallas_call(
        body, grid=(SSD_CONV_CH // 128,),
        in_specs=[pl.BlockSpec((S, 128), lambda j: (0, XBC_BLK0 + j)), pl.BlockSpec((SSD_CONV, 128), lambda j: (0, j)),
                  pl.BlockSpec((1, 128), lambda j: (0, j))],
        out_specs=pl.BlockSpec((S, 128), lambda j: (0, j)),
        out_shape=jax.ShapeDtypeStruct((S, SSD_CONV_CH), F32),
        scratch_shapes=[pltpu.VMEM((S + 8, 128), F32)], name=name, compiler_params=_params("parallel"),
    )(proj, w, b)


def _conv_bwd(dact, proj, w, b, name):
    S = proj.shape[0]
    T = min(512, S)

    def body(da_ref, x_ref, w_ref, b_ref, dx_ref, dw_ref, db_ref, xp_ref, dcp_ref):
        xp_ref[pl.ds(0, 8), :] = jnp.zeros((8, 128), F32)
        xp_ref[pl.ds(8, S), :] = x_ref[...]
        dcp_ref[pl.ds(S, 8), :] = jnp.zeros((8, 128), F32)
        wv = w_ref[...]
        bv = b_ref[...]

        def step1(c, carry):
            base = pl.multiple_of(c * T, T)
            xs = [xp_ref[pl.ds(base + 5 + i, T), :] for i in range(SSD_CONV)]
            acc = wv[0:1] * xs[0]
            for i in range(1, SSD_CONV):
                acc = acc + wv[i:i + 1] * xs[i]
            acc = bv + acc
            s = _sigmoid(acc)
            dc = da_ref[pl.ds(base, T), :] * _silu_grad(acc, s)
            dcp_ref[pl.ds(base, T), :] = dc
            new = tuple(carry[i] + jnp.sum(xs[i] * dc, axis=0, keepdims=True) for i in range(SSD_CONV))
            return new + (carry[SSD_CONV] + jnp.sum(dc, axis=0, keepdims=True),)

        z = jnp.zeros((1, 128), F32)
        res = lax.fori_loop(0, S // T, step1, (z,) * (SSD_CONV + 1))
        for i in range(SSD_CONV):
            dw_ref[pl.ds(i, 1), :] = res[i]
        db_ref[...] = res[SSD_CONV]

        def step2(c, carry):
            base = pl.multiple_of(c * T, T)
            acc = wv[0:1] * dcp_ref[pl.ds(base + 3, T), :]
            for i in range(1, SSD_CONV):
                acc = acc + wv[i:i + 1] * dcp_ref[pl.ds(base + 3 - i, T), :]
            dx_ref[pl.ds(base, T), :] = acc.astype(dx_ref.dtype)
            return carry

        lax.fori_loop(0, S // T, step2, 0)

    col = pl.BlockSpec((S, 128), lambda j: (0, j))
    return pl.pallas_call(
        body, grid=(SSD_CONV_CH // 128,),
        in_specs=[col, pl.BlockSpec((S, 128), lambda j: (0, XBC_BLK0 + j)), pl.BlockSpec((SSD_CONV, 128), lambda j: (0, j)),
                  pl.BlockSpec((1, 128), lambda j: (0, j))],
        out_specs=[col, pl.BlockSpec((SSD_CONV, 128), lambda j: (0, j)), pl.BlockSpec((1, 128), lambda j: (0, j))],
        out_shape=[jax.ShapeDtypeStruct((S, SSD_CONV_CH), BF16), jax.ShapeDtypeStruct((SSD_CONV, SSD_CONV_CH), F32),
                   jax.ShapeDtypeStruct((1, SSD_CONV_CH), F32)],
        scratch_shapes=[pltpu.VMEM((S + 8, 128), F32), pltpu.VMEM((S + 8, 128), F32)], name=name,
        compiler_params=_params("parallel"),
    )(dact, proj, w, b)


HPG = SSD_HEADS // SSD_GROUPS
GW = HPG * SSD_HEAD_DIM


def _ssd_chunk_terms(dtr, bias, alog, tril, triu):
    pre = dtr + bias
    dt = jnp.maximum(pre, 0.0) + jnp.log(1.0 + jnp.exp(-jnp.abs(pre)))
    a_neg = -jnp.exp(alog)
    a = dt * a_neg
    acum = _xdot(tril, a, NN)
    acum_t = _xdot(a, triu, TN)
    return pre, dt, a_neg, acum, acum_t


def _ssd_fwd(xbc, proj, dtr, dt_bias, a_log, d_rep, gain, name):
    S = xbc.shape[0]
    L = CHUNK
    T = min(512, S)
    CPS = T // L
    NC = S // L

    def body(x_ref, b_ref, c_ref, z_ref, dtr_ref, bias_ref, alog_ref, d_ref, gain_ref, y_ref, yraw_ref, st_ref, state):
        i = pl.program_id(1)

        @pl.when(i == 0)
        def _():
            state[...] = jnp.zeros_like(state)

        row = lax.broadcasted_iota(jnp.int32, (L, L), 0)
        col = lax.broadcasted_iota(jnp.int32, (L, L), 1)
        causal = row >= col
        tril = causal.astype(F32)
        triu = (row <= col).astype(F32)
        for c in range(CPS):
            rows = pl.ds(c * L, L)
            xv = x_ref[rows, :]
            bm = b_ref[rows, :]
            cm = c_ref[rows, :]
            _, dt, _, acum, acum_t = _ssd_chunk_terms(dtr_ref[rows, :], bias_ref[...], alog_ref[...], tril, triu)
            gmat = _bdot(cm, bm, NT)
            for j in range(HPG):
                ac = acum[:, j:j + 1]
                lam = jnp.exp(jnp.where(causal, ac - acum_t[j:j + 1, :], NEG))
                xd = xv[:, j * 64:(j + 1) * 64] * dt[:, j:j + 1]
                sp = state[j]
                st_ref[c, j] = sp
                ac_last = ac[L - 1:L, :]
                y = _bdot(gmat * lam, xd, NN) + _bdot(cm, sp, NT) * jnp.exp(ac)
                yraw_ref[rows, j * 64:(j + 1) * 64] = y
                state[j] = jnp.exp(ac_last) * sp + _bdot(xd * jnp.exp(ac_last - ac), bm, TN)
            zz = z_ref[rows, :]
            u = (yraw_ref[rows, :] + xv * d_ref[...]) * (zz * _sigmoid(zz))
            r = lax.rsqrt(jnp.mean(u * u, axis=-1, keepdims=True) + EPS)
            y_ref[rows, :] = (u * r * gain_ref[...]).astype(y_ref.dtype)

    vec8 = pl.BlockSpec((None, 1, HPG), lambda g, i: (g, 0, 0))
    return pl.pallas_call(
        body, grid=(SSD_GROUPS, S // T),
        in_specs=[pl.BlockSpec((T, GW), lambda g, i: (i, g)),
                  pl.BlockSpec((T, SSD_STATE), lambda g, i: (i, SSD_WIDTH // SSD_STATE + g)),
                  pl.BlockSpec((T, SSD_STATE), lambda g, i: (i, SSD_WIDTH // SSD_STATE + SSD_GROUPS + g)),
                  pl.BlockSpec((T, GW), lambda g, i: (i, g)),
                  pl.BlockSpec((None, T, HPG), lambda g, i: (g, i, 0)),
                  vec8, vec8,
                  pl.BlockSpec((1, GW), lambda g, i: (0, g)), pl.BlockSpec((1, GW), lambda g, i: (0, g))],
        out_specs=[pl.BlockSpec((T, GW), lambda g, i: (i, g)), pl.BlockSpec((T, GW), lambda g, i: (i, g)),
                   pl.BlockSpec((CPS, HPG, SSD_HEAD_DIM, SSD_STATE), lambda g, i: (i, g, 0, 0))],
        out_shape=[jax.ShapeDtypeStruct((S, SSD_WIDTH), BF16), jax.ShapeDtypeStruct((S, SSD_WIDTH), F32),
                   jax.ShapeDtypeStruct((NC, SSD_HEADS, SSD_HEAD_DIM, SSD_STATE), F32)],
        scratch_shapes=[pltpu.VMEM((HPG, SSD_HEAD_DIM, SSD_STATE), F32)], name=name,
        compiler_params=_params("arbitrary", "arbitrary"),
    )(xbc, xbc, xbc, proj, dtr, dt_bias, a_log, d_rep, gain)


def _ssd_bwd(dy, yraw, xbc, proj, dtr, dt_bias, a_log, d_rep, gain, states, name):
    S = xbc.shape[0]
    L = CHUNK
    T = min(512, S)
    CPS = T // L
    NI = S // T

    def body(dy_ref, yraw_ref, x_ref, b_ref, c_ref, z_ref, dtr_ref, bias_ref, alog_ref, d_ref, gain_ref, st_ref,
             dz_ref, dx_ref, db_ref, dc_ref, ddtr_ref, dbias_ref, dalog_ref, dd_ref, dgain_ref, dstate, dyh_ref, dxs_ref):
        i = pl.program_id(1)

        @pl.when(i == 0)
        def _():
            dstate[...] = jnp.zeros_like(dstate)
            dbias_ref[...] = jnp.zeros_like(dbias_ref)
            dalog_ref[...] = jnp.zeros_like(dalog_ref)
            dd_ref[...] = jnp.zeros_like(dd_ref)
            dgain_ref[...] = jnp.zeros_like(dgain_ref)

        row = lax.broadcasted_iota(jnp.int32, (L, L), 0)
        col = lax.broadcasted_iota(jnp.int32, (L, L), 1)
        causal = row >= col
        eye = row == col
        tril = causal.astype(F32)
        triu = (row <= col).astype(F32)
        lane8 = lax.broadcasted_iota(jnp.int32, (1, HPG), 1)
        last_row = (lax.broadcasted_iota(jnp.int32, (L, 1), 0) == L - 1).astype(F32)
        for c in reversed(range(CPS)):
            rows = pl.ds(c * L, L)
            xv = x_ref[rows, :]
            bm = b_ref[rows, :]
            cm = c_ref[rows, :]
            zz = z_ref[rows, :]
            dvec = d_ref[...]
            gn = gain_ref[...]
            sz = _sigmoid(zz)
            silu_z = zz * sz
            v = yraw_ref[rows, :] + xv * dvec
            u = v * silu_z
            r = lax.rsqrt(jnp.mean(u * u, axis=-1, keepdims=True) + EPS)
            n = u * r
            do = dy_ref[rows, :]
            dgain_ref[...] += jnp.sum(do * n, axis=0, keepdims=True)
            dn = do * gn
            du = r * (dn - n * jnp.mean(dn * n, axis=-1, keepdims=True))
            dz_ref[rows, :] = (du * v * _silu_grad(zz, sz)).astype(dz_ref.dtype)
            dv = du * silu_z
            dyh_ref[...] = dv
            dxs_ref[...] = dv * dvec
            e = dv * xv
            pre, dt, a_neg, acum, acum_t = _ssd_chunk_terms(dtr_ref[rows, :], bias_ref[...], alog_ref[...], tril, triu)
            gmat = _bdot(cm, bm, NT)
            dgmat = jnp.zeros((L, L), F32)
            dbm = jnp.zeros((L, SSD_STATE), F32)
            dcm = jnp.zeros((L, SSD_STATE), F32)
            dac8 = jnp.zeros((L, HPG), F32)
            ddt8 = jnp.zeros((L, HPG), F32)
            dd8 = jnp.zeros((1, HPG), F32)
            for j in range(HPG):
                hs = slice(j * 64, (j + 1) * 64)
                onehot = (lane8 == j).astype(F32)
                ac = acum[:, j:j + 1]
                dtj = dt[:, j:j + 1]
                lam = jnp.exp(jnp.where(causal, ac - acum_t[j:j + 1, :], NEG))
                mh = gmat * lam
                xj = xv[:, hs]
                xd = xj * dtj
                sp = st_ref[c, j]
                ds = dstate[j]
                ea = jnp.exp(ac)
                ac_last = ac[L - 1:L, :]
                ea_last = jnp.exp(ac_last)
                w = jnp.exp(ac_last - ac)
                dyj = dyh_ref[:, hs]
                dye = dyj * ea
                yoff = _bdot(cm, sp, NT) * ea
                bds = _bdot(bm, ds, NT)
                dxd = _bdot(mh, dyj, TN) + bds * w
                dm = _bdot(dyj, xd, NT)
                dgmat = dgmat + dm * lam
                q = dm * mh
                dw = jnp.sum(xd * bds, axis=1, keepdims=True)
                dac = (jnp.sum(q, axis=1, keepdims=True) - _col_of_row(jnp.sum(q, axis=0, keepdims=True), eye)
                       + jnp.sum(dyj * yoff, axis=1, keepdims=True) - dw * w)
                tail = jnp.sum(dw * w, axis=0, keepdims=True) + ea_last * jnp.sum(
                    jnp.sum(ds * sp, axis=1, keepdims=True), axis=0, keepdims=True)
                dac = dac + last_row * tail
                dcm = dcm + _bdot(dye, sp, NN)
                dbm = dbm + _bdot(xd * w, ds, NN)
                dstate[j] = ea_last * ds + _bdot(dye, cm, TN)
                dxs_ref[:, hs] += dxd * dtj
                ddt8 = ddt8 + jnp.sum(dxd * xj, axis=1, keepdims=True) * onehot
                dac8 = dac8 + dac * onehot
                dd8 = dd8 + jnp.sum(jnp.sum(e[:, hs], axis=1, keepdims=True), axis=0, keepdims=True) * onehot
            dx_ref[rows, :] = dxs_ref[...]
            dc_ref[rows, :] = dcm + _bdot(dgmat, bm, NN)
            db_ref[rows, :] = dbm + _bdot(dgmat, cm, TN)
            da8 = _xdot(triu, dac8, NN)
            ddt8 = ddt8 + da8 * a_neg
            dalog_ref[...] += jnp.sum(da8 * dt, axis=0, keepdims=True) * a_neg
            dpre = ddt8 * _sigmoid(pre)
            ddtr_ref[rows, :] = dpre
            dbias_ref[...] += jnp.sum(dpre, axis=0, keepdims=True)
            dd_ref[...] += dd8

    rev = lambda i: NI - 1 - i
    vec8 = pl.BlockSpec((None, 1, HPG), lambda g, i: (g, 0, 0))
    grp = pl.BlockSpec((T, GW), lambda g, i: (rev(i), g))
    bspec = pl.BlockSpec((T, SSD_STATE), lambda g, i: (rev(i), SSD_WIDTH // SSD_STATE + g))
    cspec = pl.BlockSpec((T, SSD_STATE), lambda g, i: (rev(i), SSD_WIDTH // SSD_STATE + SSD_GROUPS + g))
    gvec = pl.BlockSpec((1, GW), lambda g, i: (0, g))
    st_spec = pl.BlockSpec((CPS, HPG, SSD_HEAD_DIM, SSD_STATE), lambda g, i: (rev(i), g, 0, 0))
    small = jax.ShapeDtypeStruct((SSD_GROUPS, 1, HPG), F32)
    return pl.pallas_call(
        body, grid=(SSD_GROUPS, NI),
        in_specs=[grp, grp, grp, bspec, cspec, grp, pl.BlockSpec((None, T, HPG), lambda g, i: (g, rev(i), 0)),
                  vec8, vec8, gvec, gvec, st_spec],
        out_specs=[grp, grp, pl.BlockSpec((T, SSD_STATE), lambda g, i: (rev(i), g)),
                   pl.BlockSpec((T, SSD_STATE), lambda g, i: (rev(i), g)),
                   pl.BlockSpec((None, T, HPG), lambda g, i: (g, rev(i), 0)), vec8, vec8, vec8, gvec],
        out_shape=[jax.ShapeDtypeStruct((S, SSD_WIDTH), BF16), jax.ShapeDtypeStruct((S, SSD_WIDTH), F32),
                   jax.ShapeDtypeStruct((S, SSD_GROUPS * SSD_STATE), F32), jax.ShapeDtypeStruct((S, SSD_GROUPS * SSD_STATE), F32),
                   jax.ShapeDtypeStruct((SSD_GROUPS, S, HPG), F32), small, small, small,
                   jax.ShapeDtypeStruct((1, SSD_WIDTH), F32)],
        scratch_shapes=[pltpu.VMEM((HPG, SSD_HEAD_DIM, SSD_STATE), F32), pltpu.VMEM((L, GW), F32), pltpu.VMEM((L, GW), F32)],
        name=name, compiler_params=_params("arbitrary", "arbitrary"),
    )(dy, yraw, xbc, xbc, xbc, proj, dtr, dt_bias, a_log, d_rep, gain, states)


def _swap_halves(t):
    w = t.shape[1]
    lane = lax.broadcasted_iota(jnp.int32, t.shape, 1)
    return jnp.where((lane % 64) < 32, pltpu.roll(t, w - 32, axis=1), pltpu.roll(t, 32, axis=1))


def _widen(tab, w):
    return tab if w == 128 else jnp.concatenate([tab] * (w // 128), axis=1)


def _rope(t, cos, sin_signed):
    return t * cos + _swap_halves(t) * sin_signed


def _rope_t(d, cos, sin_signed):
    return d * cos - _swap_halves(d) * sin_signed


def _group_sum64(v, bd):
    hi = v.astype(BF16)
    lo = (v - hi.astype(F32)).astype(BF16)
    return (lax.dot_general(hi, bd, NN, preferred_element_type=F32)
            + lax.dot_general(lo, bd, NN, preferred_element_type=F32))


AQ_BLK = 2560 // ATT_WIDTH


def _att_prep_fwd(proj, qg, kg, cos, sin, bd, name):
    S = proj.shape[0]
    T = min(512, S)

    def body(q_ref, k_ref, v_ref, qg_ref, kg_ref, cos_ref, sin_ref, bd_ref, qo_ref, ko_ref, vo_ref):
        cw = _widen(cos_ref[...], ATT_WIDTH)
        sw = _widen(sin_ref[...], ATT_WIDTH)
        bdv = bd_ref[...]

        def norm_rope(t, gain):
            ss = _group_sum64(t * t, bdv)
            return _rope(t * lax.rsqrt(ss * (1.0 / ATT_HEAD_DIM) + EPS) * gain, cw, sw)

        qo_ref[...] = (norm_rope(q_ref[...], qg_ref[...]) * (ATT_HEAD_DIM ** -0.5)).astype(BF16)
        kt = norm_rope(k_ref[...], kg_ref[...]).astype(BF16)
        vt = v_ref[...].astype(BF16)
        for h in range(ATT_HEADS):
            ko_ref[h] = kt[:, h * 64:(h + 1) * 64]
            vo_ref[h] = vt[:, h * 64:(h + 1) * 64]

    vec = pl.BlockSpec((1, ATT_WIDTH), lambda i: (0, 0))
    tab = pl.BlockSpec((T, 128), lambda i: (i, 0))
    hm = pl.BlockSpec((ATT_HEADS, T, ATT_HEAD_DIM), lambda i: (0, i, 0))
    return pl.pallas_call(
        body, grid=(S // T,),
        in_specs=[pl.BlockSpec((T, ATT_WIDTH), lambda i: (i, AQ_BLK)), pl.BlockSpec((T, ATT_WIDTH), lambda i: (i, AQ_BLK + 1)),
                  pl.BlockSpec((T, ATT_WIDTH), lambda i: (i, AQ_BLK + 2)), vec, vec, tab, tab,
                  pl.BlockSpec((ATT_WIDTH, ATT_WIDTH), lambda i: (0, 0))],
        out_specs=[pl.BlockSpec((T, ATT_WIDTH), lambda i: (i, 0)), hm, hm],
        out_shape=[jax.ShapeDtypeStruct((S, ATT_WIDTH), BF16), jax.ShapeDtypeStruct((ATT_HEADS, S, ATT_HEAD_DIM), BF16),
                   jax.ShapeDtypeStruct((ATT_HEADS, S, ATT_HEAD_DIM), BF16)],
        name=name, compiler_params=_params("parallel"),
    )(proj, proj, proj, qg, kg, cos, sin, bd)


def _att_prep_bwd(proj, dq, dk, qg, kg, cos, sin, bd, name):
    S = proj.shape[0]
    T = min(512, S)
    NI = S // T

    def body(q_ref, k_ref, dq_ref, dk_ref, qg_ref, kg_ref, cos_ref, sin_ref, bd_ref, dqo_ref, dko_ref, dqg_ref, dkg_ref, acc_ref):
        i = pl.program_id(0)

        @pl.when(i == 0)
        def _():
            acc_ref[...] = jnp.zeros_like(acc_ref)

        cw = _widen(cos_ref[...], ATT_WIDTH)
        sw = _widen(sin_ref[...], ATT_WIDTH)
        bdv = bd_ref[...]

        def one(t, d_rot, gain, scale, slot):
            ss = _group_sum64(t * t, bdv)
            r = lax.rsqrt(ss * (1.0 / ATT_HEAD_DIM) + EPS)
            n = t * r
            d_ng = _rope_t(d_rot * scale, cw, sw)
            acc_ref[pl.ds(slot, 1), :] += jnp.sum(d_ng * n, axis=0, keepdims=True)
            dn = d_ng * gain
            return r * (dn - n * (_group_sum64(dn * n, bdv) * (1.0 / ATT_HEAD_DIM)))

        dqo_ref[...] = one(q_ref[...], dq_ref[...], qg_ref[...], ATT_HEAD_DIM ** -0.5, 0).astype(BF16)
        dko_ref[...] = one(k_ref[...], dk_ref[...], kg_ref[...], 1.0, 1).astype(BF16)

        @pl.when(i == NI - 1)
        def _():
            a = acc_ref[...]
            f = a[:, 0:64]
            for h in range(1, ATT_HEADS):
                f = f + a[:, h * 64:(h + 1) * 64]
            dqg_ref[...] = f[0:1]
            dkg_ref[...] = f[1:2]

    vec = pl.BlockSpec((1, ATT_WIDTH), lambda i: (0, 0))
    tab = pl.BlockSpec((T, 128), lambda i: (i, 0))
    row = pl.BlockSpec((T, ATT_WIDTH), lambda i: (i, 0))
    g64 = pl.BlockSpec((1, ATT_HEAD_DIM), lambda i: (0, 0))
    return pl.pallas_call(
        body, grid=(NI,),
        in_specs=[pl.BlockSpec((T, ATT_WIDTH), lambda i: (i, AQ_BLK)), pl.BlockSpec((T, ATT_WIDTH), lambda i: (i, AQ_BLK + 1)),
                  row, row, vec, vec, tab, tab, pl.BlockSpec((ATT_WIDTH, ATT_WIDTH), lambda i: (0, 0))],
        out_specs=[row, row, g64, g64],
        out_shape=[jax.ShapeDtypeStruct((S, ATT_WIDTH), BF16), jax.ShapeDtypeStruct((S, ATT_WIDTH), BF16),
                   jax.ShapeDtypeStruct((1, ATT_HEAD_DIM), F32), jax.ShapeDtypeStruct((1, ATT_HEAD_DIM), F32)],
        scratch_shapes=[pltpu.VMEM((8, ATT_WIDTH), F32)], name=name, compiler_params=_params("arbitrary"),
    )(proj, proj, dq, dk, qg, kg, cos, sin, bd)


def _att_bias():
    qpos = np.arange(CHUNK)[:, None] + ATT_SPAN
    kpos = np.arange(ATT_STRIP)[None, :]
    rel = qpos - kpos
    mult = np.zeros((CHUNK, ATT_STRIP), np.float64)
    for window, dil in DILATED_PAIRS:
        mult += (rel >= 0) & (rel % dil == 0) & (rel // dil <= window // dil)
    return np.where(mult > 0, np.log(np.maximum(mult, 1.0)), NEG).astype(np.float32)


def _att_scores(q, ks, bias, i):
    s = _bdot(q, ks, NT) + bias
    kcol = lax.broadcasted_iota(jnp.int32, (1, ATT_STRIP), 1) + i * CHUNK
    return jnp.where(kcol >= ATT_SPAN, s, NEG)


def _att_fwd(q, kp, vp, bias, name):
    S = q.shape[0]
    SP = kp.shape[1]

    def body(q_ref, k_ref, v_ref, bias_ref, o_ref):
        i = pl.program_id(1)
        start = pl.multiple_of(i * CHUNK, CHUNK)
        qv = q_ref[...]
        for hh in range(2):
            s = _att_scores(qv[:, hh * 64:(hh + 1) * 64], k_ref[hh, pl.ds(start, ATT_STRIP), :], bias_ref[...], i)
            m = jnp.max(s, axis=-1, keepdims=True)
            p = jnp.exp(s - m)
            den = jnp.sum(p, axis=-1, keepdims=True)
            o = _bdot(p, v_ref[hh, pl.ds(start, ATT_STRIP), :], NN) / den
            o_ref[:, hh * 64:(hh + 1) * 64] = o.astype(o_ref.dtype)

    kv = pl.BlockSpec((2, SP, ATT_HEAD_DIM), lambda hp, i: (hp, 0, 0))
    return pl.pallas_call(
        body, grid=(ATT_HEADS // 2, S // CHUNK),
        in_specs=[pl.BlockSpec((CHUNK, 128), lambda hp, i: (i, hp)), kv, kv,
                  pl.BlockSpec((CHUNK, ATT_STRIP), lambda hp, i: (0, 0))],
        out_specs=pl.BlockSpec((CHUNK, 128), lambda hp, i: (i, hp)),
        out_shape=jax.ShapeDtypeStruct((S, ATT_WIDTH), BF16), name=name, compiler_params=_params("parallel", "arbitrary"),
    )(q, kp, vp, bias)


def _att_bwd(q, kp, vp, bias, dy, name):
    S = q.shape[0]
    SP = kp.shape[1]

    def body(q_ref, k_ref, v_ref, bias_ref, do_ref, dq_ref, dk_ref, dv_ref):
        h = pl.program_id(0)
        i = pl.program_id(1)

        @pl.when(i == 0)
        def _():
            dk_ref[...] = jnp.zeros_like(dk_ref)
            dv_ref[...] = jnp.zeros_like(dv_ref)

        start = pl.multiple_of(i * CHUNK, CHUNK)
        strip = pl.ds(start, ATT_STRIP)
        odd = (h % 2) == 1
        qv = q_ref[...]
        dov = do_ref[...]
        qh = jnp.where(odd, qv[:, 64:128], qv[:, 0:64])
        doh = jnp.where(odd, dov[:, 64:128], dov[:, 0:64])
        ks = k_ref[strip, :]
        vs = v_ref[strip, :]
        s = _att_scores(qh, ks, bias_ref[...], i)
        m = jnp.max(s, axis=-1, keepdims=True)
        p = jnp.exp(s - m)
        p = p / jnp.sum(p, axis=-1, keepdims=True)
        dp = _bdot(doh, vs, NT)
        dsc = p * (dp - jnp.sum(p * dp, axis=-1, keepdims=True))
        dq_ref[...] = _bdot(dsc, ks, NN)
        dv_ref[strip, :] += _bdot(p, doh, TN)
        dk_ref[strip, :] += _bdot(dsc, qh, TN)

    kv = pl.BlockSpec((None, SP, ATT_HEAD_DIM), lambda h, i: (h, 0, 0))
    return pl.pallas_call(
        body, grid=(ATT_HEADS, S // CHUNK),
        in_specs=[pl.BlockSpec((CHUNK, 128), lambda h, i: (i, h // 2)), kv, kv,
                  pl.BlockSpec((CHUNK, ATT_STRIP), lambda h, i: (0, 0)),
                  pl.BlockSpec((CHUNK, 128), lambda h, i: (i, SSD_WIDTH // 128 + h // 2))],
        out_specs=[pl.BlockSpec((None, CHUNK, ATT_HEAD_DIM), lambda h, i: (h, i, 0)), kv, kv],
        out_shape=[jax.ShapeDtypeStruct((ATT_HEADS, S, ATT_HEAD_DIM), F32), jax.ShapeDtypeStruct((ATT_HEADS, SP, ATT_HEAD_DIM), F32),
                   jax.ShapeDtypeStruct((ATT_HEADS, SP, ATT_HEAD_DIM), F32)],
        name=name, compiler_params=_params("parallel", "arbitrary"),
    )(q, kp, vp, bias, dy)


RQ_BLK = 4096 // RET_QK_WIDTH
RV_BLK = 4608 // RET_V_WIDTH
RET_LOG_GAMMA = tuple(math.log1p(-2.0 ** (-5.0 - h)) for h in range(RET_HEADS))


def _ret_decays(h):
    L = CHUNK
    lg = RET_LOG_GAMMA[h]
    row = lax.broadcasted_iota(jnp.int32, (L, L), 0)
    col = lax.broadcasted_iota(jnp.int32, (L, L), 1)
    rel = (row - col).astype(F32)
    dm = jnp.where(rel >= 0, jnp.exp(jnp.maximum(rel, 0.0) * lg), 0.0)
    idx = lax.broadcasted_iota(jnp.int32, (L, 1), 0).astype(F32)
    kte = jnp.exp((L - 1 - idx) * lg)
    qfs = jnp.exp((idx + 1.0) * lg)
    return dm, kte, qfs, math.exp(L * lg)


def _ret_fwd(proj, cos, sin, gain, name):
    S = proj.shape[0]
    L = CHUNK
    T = min(512, S)
    CPS = T // L
    NC = S // L

    def body(q_ref, k_ref, v_ref, g_ref, cos_ref, sin_ref, gain_ref, y_ref, o_ref, st_ref, state):
        i = pl.program_id(0)

        @pl.when(i == 0)
        def _():
            state[...] = jnp.zeros_like(state)

        dec = [_ret_decays(h) for h in range(RET_HEADS)]
        for c in range(CPS):
            rows = pl.ds(c * L, L)
            cw = _widen(cos_ref[rows, :], RET_QK_WIDTH)
            sw = _widen(sin_ref[rows, :], RET_QK_WIDTH)
            qv = _rope(q_ref[rows, :], cw, sw)
            kv = _rope(k_ref[rows, :], cw, sw) * (RET_QK_DIM ** -0.5)
            for h in range(RET_HEADS):
                dm, kte, qfs, cd = dec[h]
                qh = qv[:, h * 64:(h + 1) * 64]
                kh = kv[:, h * 64:(h + 1) * 64]
                vs = slice(h * RET_V_DIM, (h + 1) * RET_V_DIM)
                vh = v_ref[rows, vs]
                sp = state[h]
                st_ref[c, h] = sp
                o = _bdot(_bdot(qh, kh, NT) * dm, vh, NN) + _bdot(qh * qfs, sp, NN)
                state[h] = cd * sp + _bdot(kh * kte, vh, TN)
                o_ref[rows, vs] = o
                gh = g_ref[rows, vs]
                r = lax.rsqrt(jnp.mean(o * o, axis=-1, keepdims=True) + EPS)
                y_ref[rows, vs] = (o * r * gain_ref[:, vs] * (gh * _sigmoid(gh))).astype(y_ref.dtype)

    tab = pl.BlockSpec((T, 128), lambda i: (i, 0))
    wide = pl.BlockSpec((T, RET_V_WIDTH), lambda i: (i, 0))
    return pl.pallas_call(
        body, grid=(S // T,),
        in_specs=[pl.BlockSpec((T, RET_QK_WIDTH), lambda i: (i, RQ_BLK)), pl.BlockSpec((T, RET_QK_WIDTH), lambda i: (i, RQ_BLK + 1)),
                  pl.BlockSpec((T, RET_V_WIDTH), lambda i: (i, RV_BLK)), pl.BlockSpec((T, RET_V_WIDTH), lambda i: (i, RV_BLK + 1)),
                  tab, tab, pl.BlockSpec((1, RET_V_WIDTH), lambda i: (0, 0))],
        out_specs=[wide, wide, pl.BlockSpec((CPS, RET_HEADS, RET_QK_DIM, RET_V_DIM), lambda i: (i, 0, 0, 0))],
        out_shape=[jax.ShapeDtypeStruct((S, RET_V_WIDTH), BF16), jax.ShapeDtypeStruct((S, RET_V_WIDTH), F32),
                   jax.ShapeDtypeStruct((NC, RET_HEADS, RET_QK_DIM, RET_V_DIM), F32)],
        scratch_shapes=[pltpu.VMEM((RET_HEADS, RET_QK_DIM, RET_V_DIM), F32)], name=name,
        compiler_params=_params("arbitrary"),
    )(proj, proj, proj, proj, cos, sin, gain)


def _ret_bwd(dy, oraw, proj, cos, sin, gain, states, name):
    S = proj.shape[0]
    L = CHUNK
    T = min(512, S)
    CPS = T // L
    NI = S // T

    def body(dy_ref, o_ref, q_ref, k_ref, v_ref, g_ref, cos_ref, sin_ref, gain_ref, st_ref,
             dq_ref, dk_ref, dv_ref, dg_ref, dgain_ref, dstate, dqs, dks):
        i = pl.program_id(0)

        @pl.when(i == 0)
        def _():
            dstate[...] = jnp.zeros_like(dstate)
            dgain_ref[...] = jnp.zeros_like(dgain_ref)

        dec = [_ret_decays(h) for h in range(RET_HEADS)]
        for c in reversed(range(CPS)):
            rows = pl.ds(c * L, L)
            cw = _widen(cos_ref[rows, :], RET_QK_WIDTH)
            sw = _widen(sin_ref[rows, :], RET_QK_WIDTH)
            qv = _rope(q_ref[rows, :], cw, sw)
            kv = _rope(k_ref[rows, :], cw, sw) * (RET_QK_DIM ** -0.5)
            for h in range(RET_HEADS):
                dm, kte, qfs, cd = dec[h]
                qs = slice(h * 64, (h + 1) * 64)
                vs = slice(h * RET_V_DIM, (h + 1) * RET_V_DIM)
                qh = qv[:, qs]
                kh = kv[:, qs]
                vh = v_ref[rows, vs]
                gh = g_ref[rows, vs]
                gn = gain_ref[:, vs]
                o = o_ref[rows, vs]
                dyh = dy_ref[rows, vs]
                sg = _sigmoid(gh)
                silu_g = gh * sg
                r = lax.rsqrt(jnp.mean(o * o, axis=-1, keepdims=True) + EPS)
                n = o * r
                dgain_ref[:, vs] += jnp.sum(dyh * n * silu_g, axis=0, keepdims=True)
                dg_ref[rows, vs] = (dyh * n * gn * _silu_grad(gh, sg)).astype(dg_ref.dtype)
                dn = dyh * gn * silu_g
                do = r * (dn - n * jnp.mean(dn * n, axis=-1, keepdims=True))
                sp = st_ref[c, h]
                ds = dstate[h]
                sc = _bdot(qh, kh, NT) * dm
                dsc = _bdot(do, vh, NT) * dm
                dv_ref[rows, vs] = (_bdot(sc, do, TN) + _bdot(kh * kte, ds, NN)).astype(dv_ref.dtype)
                dqs[:, qs] = _bdot(dsc, kh, NN) + _bdot(do, sp, NT) * qfs
                dks[:, qs] = _bdot(dsc, qh, TN) + _bdot(vh, ds, NT) * kte
                dstate[h] = cd * ds + _bdot(qh * qfs, do, TN)
            dq_ref[rows, :] = _rope_t(dqs[...], cw, sw).astype(dq_ref.dtype)
            dk_ref[rows, :] = _rope_t(dks[...] * (RET_QK_DIM ** -0.5), cw, sw).astype(dk_ref.dtype)

    rev = lambda i: NI - 1 - i
    tab = pl.BlockSpec((T, 128), lambda i: (rev(i), 0))
    wide = pl.BlockSpec((T, RET_V_WIDTH), lambda i: (rev(i), 0))
    narrow = pl.BlockSpec((T, RET_QK_WIDTH), lambda i: (rev(i), 0))
    gvec = pl.BlockSpec((1, RET_V_WIDTH), lambda i: (0, 0))
    return pl.pallas_call(
        body, grid=(NI,),
        in_specs=[pl.BlockSpec((T, RET_V_WIDTH), lambda i: (rev(i), (SSD_WIDTH + ATT_WIDTH) // RET_V_WIDTH)), wide,
                  pl.BlockSpec((T, RET_QK_WIDTH), lambda i: (rev(i), RQ_BLK)), pl.BlockSpec((T, RET_QK_WIDTH), lambda i: (rev(i), RQ_BLK + 1)),
                  pl.BlockSpec((T, RET_V_WIDTH), lambda i: (rev(i), RV_BLK)), pl.BlockSpec((T, RET_V_WIDTH), lambda i: (rev(i), RV_BLK + 1)),
                  tab, tab, gvec,
                  pl.BlockSpec((CPS, RET_HEADS, RET_QK_DIM, RET_V_DIM), lambda i: (rev(i), 0, 0, 0))],
        out_specs=[narrow, narrow, wide, wide, gvec],
        out_shape=[jax.ShapeDtypeStruct((S, RET_QK_WIDTH), BF16), jax.ShapeDtypeStruct((S, RET_QK_WIDTH), BF16),
                   jax.ShapeDtypeStruct((S, RET_V_WIDTH), BF16), jax.ShapeDtypeStruct((S, RET_V_WIDTH), BF16),
                   jax.ShapeDtypeStruct((1, RET_V_WIDTH), F32)],
        scratch_shapes=[pltpu.VMEM((RET_HEADS, RET_QK_DIM, RET_V_DIM), F32), pltpu.VMEM((L, RET_QK_WIDTH), F32),
                        pltpu.VMEM((L, RET_QK_WIDTH), F32)],
        name=name, compiler_params=_params("arbitrary"),
    )(dy, oraw, proj, proj, proj, proj, cos, sin, gain, states)


def _adamw(gblocks, w, m, v, name):
    nb, R, C = gblocks.shape
    tr = _pick(R, tuple(t for t in (512, 256, 128, 64, 32, 16, 8) if t * C <= 256 * 1024))
    c1 = 1.0 - ADAM_B1 ** ADAM_STEP
    c2 = 1.0 - ADAM_B2 ** ADAM_STEP

    def body(g_ref, w_ref, m_ref, v_ref, go_ref, d_ref, mo_ref, vo_ref):
        g = g_ref[0].astype(F32)
        for k in range(1, nb):
            g = g + g_ref[k].astype(F32)
        mn = ADAM_B1 * m_ref[...] + (1.0 - ADAM_B1) * g
        vn = ADAM_B2 * v_ref[...] + (1.0 - ADAM_B2) * (g * g)
        go_ref[...] = g
        mo_ref[...] = mn
        vo_ref[...] = vn
        d_ref[...] = -ADAM_LR * ((mn / c1) / (jnp.sqrt(vn / c2) + ADAM_EPS) + ADAM_WD * w_ref[...])

    row = pl.BlockSpec((tr, C), lambda i: (i, 0))
    sh = jax.ShapeDtypeStruct((R, C), F32)
    return pl.pallas_call(
        body, grid=(R // tr,), in_specs=[pl.BlockSpec((nb, tr, C), lambda i: (0, i, 0)), row, row, row],
        out_specs=[row, row, row, row], out_shape=[sh, sh, sh, sh], name=name, compiler_params=_params("parallel"),
    )(gblocks, w, m, v)


def _peers():
    x, y, c = lax.axis_index("x"), lax.axis_index("y"), lax.axis_index("c")
    flips = ((0, 0, 1), (1, 0, 0), (0, 1, 0), (1, 1, 0), (1, 0, 1), (0, 1, 1), (1, 1, 1))
    me = 4 * x + 2 * y + c
    peers = [(x ^ fx, y ^ fy, c ^ fc) for fx, fy, fc in flips]
    return me, peers


def _exchange(arrs, scatter, name):
    n = len(arrs)
    npeer = N_DEV - 1

    def body(*refs):
        ins, outs = refs[:n], refs[n:2 * n]
        send_sems, recv_sems, local_sems = refs[2 * n:]
        me, peers = _peers()
        copies = []
        for a in range(n):
            src_own = ins[a].at[me] if scatter else ins[a]
            own = pltpu.make_async_copy(src_own, outs[a].at[me], local_sems.at[a])
            own.start()
            copies.append(own)
            for k, peer in enumerate(peers):
                src = ins[a].at[4 * peer[0] + 2 * peer[1] + peer[2]] if scatter else ins[a]
                cp = pltpu.make_async_remote_copy(
                    src_ref=src, dst_ref=outs[a].at[me], send_sem=send_sems.at[a * npeer + k],
                    recv_sem=recv_sems.at[a * npeer + k], device_id=peer, device_id_type=pl.DeviceIdType.MESH)
                cp.start()
                copies.append(cp)
        for cp in copies:
            cp.wait()

    out_shape = [jax.ShapeDtypeStruct(((N_DEV,) + a.shape[1:]) if scatter else ((N_DEV,) + a.shape), a.dtype) for a in arrs]
    anyspec = pl.BlockSpec(memory_space=pl.ANY)
    return pl.pallas_call(
        body, in_specs=[anyspec] * n, out_specs=[anyspec] * n, out_shape=out_shape,
        scratch_shapes=[pltpu.SemaphoreType.DMA((n * npeer,)), pltpu.SemaphoreType.DMA((n * npeer,)),
                        pltpu.SemaphoreType.DMA((n,))],
        name=name,
    )(*arrs)


def _tables(S):
    pos = jnp.arange(S, dtype=F32)
    inv = ROPE_THETA ** (-jnp.arange(0, ATT_HEAD_DIM, 2, dtype=F32) / ATT_HEAD_DIM)
    ang = pos[:, None] * inv[None, :]
    cos, sin = jnp.cos(ang), jnp.sin(ang)
    cos128 = jnp.tile(cos, (1, 4))
    sin128 = jnp.tile(jnp.concatenate([-sin, sin], axis=1), (1, 2))
    lane = np.arange(ATT_WIDTH)
    bd = jnp.asarray((lane[:, None] // 64 == lane[None, :] // 64).astype(np.float32), dtype=BF16)
    return cos128, sin128, bd, jnp.asarray(_att_bias())


def _permute_in(w):
    pad = jnp.zeros((w.shape[0], IN_PAD - IN_WIDTH), w.dtype)
    return jnp.concatenate([w[:, :2560], w[:, 2576:], w[:, 2560:2576], pad], axis=1)


def _unpermute_in(g):
    return jnp.concatenate([g[:, :2560], g[:, DT_COL:DT_COL + SSD_HEADS], g[:, 2560:DT_COL]], axis=1)


def _layer_fwd(l, x, p, tabs):
    cos, sin, bd, bias = tabs
    S = x.shape[0]
    row = lambda v: v.reshape(1, -1)
    hn = _rmsnorm_fwd(x, row(p["ln_mix"]), f"norm_mix_fwd{l}")
    proj = _mm(hn, p["w_in"], "nn", f"in_proj{l}", tn=1152)
    xbc = _conv_fwd(proj, p["conv_w"], row(p["conv_b"]), f"conv_fwd{l}")
    dtr = proj[:, DT_COL:DT_COL + SSD_HEADS].reshape(S, SSD_GROUPS, HPG).transpose(1, 0, 2)
    grp = lambda v: v.reshape(SSD_GROUPS, 1, HPG)
    d_rep = row(jnp.repeat(p["d_skip"], SSD_HEAD_DIM))
    y_ssd, yraw, ssd_st = _ssd_fwd(xbc, proj, dtr, grp(p["dt_bias"]), grp(p["a_log"]), d_rep, row(p["ssd_norm"]), f"ssd_fwd{l}")
    qg = row(jnp.tile(p["q_norm"], ATT_HEADS))
    kg = row(jnp.tile(p["k_norm"], ATT_HEADS))
    aq, ak, av = _att_prep_fwd(proj, qg, kg, cos, sin, bd, f"att_prep_fwd{l}")
    padk = lambda t: jnp.pad(t, ((0, 0), (ATT_SPAN, 0), (0, 0)))
    akp, avp = padk(ak), padk(av)
    y_att = _att_fwd(aq, akp, avp, bias, f"att_fwd{l}")
    y_ret, oraw, ret_st = _ret_fwd(proj, cos, sin, row(p["ret_norm"]), f"ret_fwd{l}")
    y = jnp.concatenate([y_ssd, y_att, y_ret], axis=1)
    x1 = _mm(y, p["w_out"], "nn", f"out_proj{l}", residual=x)
    hn2 = _rmsnorm_fwd(x1, row(p["ln_ffn"]), f"norm_ffn_fwd{l}")
    g, u, act = _swiglu_fwd(hn2, p["w_gate"], p["w_up"], f"swiglu_fwd{l}")
    x2 = _mm(act, p["w_down"], "nn", f"down_proj{l}", residual=x1, tk=1408)
    saved = dict(x=x, hn=hn, proj=proj, xbc=xbc, dtr=dtr, yraw=yraw, ssd_st=ssd_st, aq=aq, akp=akp, avp=avp,
                 oraw=oraw, ret_st=ret_st, y=y, x1=x1, hn2=hn2, g=g, u=u, act=act, d_rep=d_rep, qg=qg, kg=kg)
    return x2, saved


def _layer_bwd(l, dx2, p, sv, tabs):
    cos, sin, bd, bias = tabs
    S = dx2.shape[0]
    row = lambda v: v.reshape(1, -1)
    grp = lambda v: v.reshape(SSD_GROUPS, 1, HPG)
    gr = {}
    dg, du = _swiglu_bwd(dx2, p["w_down"], sv["g"], sv["u"], f"swiglu_bwd{l}")
    gr["w_down"] = _mm(sv["act"], dx2, "tn", f"down_wgrad{l}", tm=1408, tn=1024, tk=1024)
    dhn2 = _mm_nt2(dg, p["w_gate"], du, p["w_up"], f"ffn_dgrad{l}")
    gr["w_gate"] = _mm(sv["hn2"], dg, "tn", f"gate_wgrad{l}", tm=1024, tn=1408, tk=1024)
    gr["w_up"] = _mm(sv["hn2"], du, "tn", f"up_wgrad{l}", tm=1024, tn=1408, tk=1024)
    dx1, dln_ffn = _rmsnorm_bwd(sv["x1"], dhn2, row(p["ln_ffn"]), dx2, f"norm_ffn_bwd{l}")
    gr["ln_ffn"] = dln_ffn[0]
    dy = _mm(dx1, p["w_out"], "nt", f"out_dgrad{l}")
    gr["w_out"] = _mm(sv["y"], dx1, "tn", f"out_wgrad{l}", tm=1024, tn=1024, tk=1024)
    dz, dxs, dbm, dcm, ddtr, dbias, dalog, dd, dssd_gain = _ssd_bwd(
        dy, sv["yraw"], sv["xbc"], sv["proj"], sv["dtr"], grp(p["dt_bias"]), grp(p["a_log"]), sv["d_rep"],
        row(p["ssd_norm"]), sv["ssd_st"], f"ssd_bwd{l}")
    gr["dt_bias"], gr["a_log"], gr["d_skip"] = dbias.reshape(-1), dalog.reshape(-1), dd.reshape(-1)
    gr["ssd_norm"] = dssd_gain[0]
    dxbc_act = jnp.concatenate([dxs, dbm, dcm], axis=1)
    dxbc, dconv_w, dconv_b = _conv_bwd(dxbc_act, sv["proj"], p["conv_w"], row(p["conv_b"]), f"conv_bwd{l}")
    gr["conv_w"], gr["conv_b"] = dconv_w, dconv_b[0]
    dq_h, dk_h, dv_h = _att_bwd(sv["aq"], sv["akp"], sv["avp"], bias, dy, f"att_bwd{l}")
    flat = lambda t: t.transpose(1, 0, 2).reshape(S, ATT_WIDTH)
    daq, dak, dqg, dkg = _att_prep_bwd(sv["proj"], flat(dq_h), flat(dk_h[:, ATT_SPAN:]), sv["qg"], sv["kg"], cos, sin, bd,
                                       f"att_prep_bwd{l}")
    dav = flat(dv_h[:, ATT_SPAN:]).astype(BF16)
    gr["q_norm"], gr["k_norm"] = dqg[0], dkg[0]
    drq, drk, drv, drg, dret_gain = _ret_bwd(dy, sv["oraw"], sv["proj"], cos, sin, row(p["ret_norm"]), sv["ret_st"], f"ret_bwd{l}")
    gr["ret_norm"] = dret_gain[0]
    ddt_cols = ddtr.transpose(1, 0, 2).reshape(S, SSD_HEADS).astype(BF16)
    dproj = jnp.concatenate([dz, dxbc, daq, dak, dav, drq, drk, drv, drg, ddt_cols,
                             jnp.zeros((S, IN_PAD - IN_WIDTH), BF16)], axis=1)
    dhn = _mm(dproj, p["w_in"], "nt", f"in_dgrad{l}", tk=1920)
    gr["w_in"] = _mm(sv["hn"], dproj, "tn", f"in_wgrad{l}", tm=1024, tn=1152, tk=1024)
    dx0, dln_mix = _rmsnorm_bwd(sv["x"], dhn, row(p["ln_mix"]), dx1, f"norm_mix_bwd{l}")
    gr["ln_mix"] = dln_mix[0]
    return dx0, gr


def _local_step(x, tgt, layers):
    tabs = _tables(x.shape[0])
    saved = []
    h = x
    for l, p in enumerate(layers):
        h, sv = _layer_fwd(l, h, p, tabs)
        saved.append(sv)
    dh, lacc = _loss_grad(h, tgt, "loss_grad")
    grads = [None] * len(layers)
    for l in reversed(range(len(layers))):
        dh, grads[l] = _layer_bwd(l, dh, layers[l], saved[l], tabs)
    return lacc[0, 0], dh, grads


BIG = ("w_in", "w_out", "w_gate", "w_up", "w_down")
SMALL = ("ln_mix", "conv_b", "dt_bias", "a_log", "d_skip", "ssd_norm", "q_norm", "k_norm", "ret_norm", "ln_ffn")
ORDER = ("ln_mix", "w_in", "conv_w", "conv_b", "dt_bias", "a_log", "d_skip", "ssd_norm", "q_norm", "k_norm", "ret_norm",
         "w_out", "ln_ffn", "w_gate", "w_up", "w_down")


def _full_weights(gath, small):
    layers = []
    for l in range(DEPTH):
        p = {k: small[k][l] for k in SMALL}
        cols = lambda t: t[:, l].transpose(1, 0, 2).reshape(t.shape[2], -1)
        rows = lambda t: t[:, l].reshape(-1, t.shape[3])
        p["w_in"] = _permute_in(cols(gath["w_in"]))
        p["w_gate"] = cols(gath["w_gate"])
        p["w_up"] = cols(gath["w_up"])
        p["w_out"] = rows(gath["w_out"])
        p["w_down"] = rows(gath["w_down"])
        p["conv_w"] = cols(gath["conv_w"])
        layers.append(p)
    return layers


def _shard_blocks(grads):
    out = {}
    for k in BIG:
        per_layer = []
        for l in range(DEPTH):
            g = grads[l][k]
            if k == "w_in":
                g = _unpermute_in(g)
            if k in ("w_in", "w_gate", "w_up"):
                blk = g.reshape(g.shape[0], N_DEV, -1).transpose(1, 0, 2)
            else:
                blk = g.reshape(N_DEV, -1, g.shape[1])
            per_layer.append(blk)
        out[k] = jnp.concatenate(per_layer, axis=1).astype(BF16)
    return out


def kernel(x, ln_mix, w_in, conv_w, conv_b, dt_bias, a_log, d_skip, ssd_norm, q_norm, k_norm, ret_norm, w_out, ln_ffn, w_gate, w_up, w_down, loss_target, m_ln_mix, m_w_in, m_conv_w, m_conv_b, m_dt_bias, m_a_log, m_d_skip, m_ssd_norm, m_q_norm, m_k_norm, m_ret_norm, m_w_out, m_ln_ffn, m_w_gate, m_w_up, m_w_down, v_ln_mix, v_w_in, v_conv_w, v_conv_b, v_dt_bias, v_a_log, v_d_skip, v_ssd_norm, v_q_norm, v_k_norm, v_ret_norm, v_w_out, v_ln_ffn, v_w_gate, v_w_up, v_w_down):
    w = dict(ln_mix=ln_mix, w_in=w_in, conv_w=conv_w, conv_b=conv_b, dt_bias=dt_bias, a_log=a_log, d_skip=d_skip,
             ssd_norm=ssd_norm, q_norm=q_norm, k_norm=k_norm, ret_norm=ret_norm, w_out=w_out, ln_ffn=ln_ffn,
             w_gate=w_gate, w_up=w_up, w_down=w_down)
    m = dict(ln_mix=m_ln_mix, w_in=m_w_in, conv_w=m_conv_w, conv_b=m_conv_b, dt_bias=m_dt_bias, a_log=m_a_log,
             d_skip=m_d_skip, ssd_norm=m_ssd_norm, q_norm=m_q_norm, k_norm=m_k_norm, ret_norm=m_ret_norm, w_out=m_w_out,
             ln_ffn=m_ln_ffn, w_gate=m_w_gate, w_up=m_w_up, w_down=m_w_down)
    v = dict(ln_mix=v_ln_mix, w_in=v_w_in, conv_w=v_conv_w, conv_b=v_conv_b, dt_bias=v_dt_bias, a_log=v_a_log,
             d_skip=v_d_skip, ssd_norm=v_ssd_norm, q_norm=v_q_norm, k_norm=v_k_norm, ret_norm=v_ret_norm, w_out=v_w_out,
             ln_ffn=v_ln_ffn, w_gate=v_w_gate, w_up=v_w_up, w_down=v_w_down)
    me = 4 * lax.axis_index("x") + 2 * lax.axis_index("y") + lax.axis_index("c")

    gathered = _exchange([w[k].astype(BF16) for k in BIG] + [conv_w], False, "gather_weights")
    gath = dict(zip(BIG + ("conv_w",), gathered))
    layers = _full_weights(gath, w)
    loss_part, gx, grads = _local_step(x[0], loss_target[0], layers)
    loss = lax.psum(loss_part, MESH_AXES)

    out = {}
    blocks = _shard_blocks(grads)
    recv = _exchange([blocks[k] for k in BIG], True, "scatter_grads")
    for k, r in zip(BIG, recv):
        shp = w[k].shape
        flat = lambda t: t.reshape(-1, shp[-1])
        res = _adamw(r, flat(w[k]), flat(m[k]), flat(v[k]), f"adamw_{k}")
        out[k] = [t.reshape(shp) for t in res]
    names = SMALL + ("conv_w",)
    sizes = [int(np.prod(grads[0][k].shape)) for k in names]
    packed = jnp.concatenate([jnp.stack([grads[l][k] for l in range(DEPTH)]).reshape(-1) for k in names])
    n_small = packed.shape[0]
    rows_small = -(-n_small // 1024) * 8
    pad = lambda t, fill: jnp.concatenate([t, jnp.full((rows_small * 128 - n_small,), fill, F32)]).reshape(rows_small, 128)
    parts = _exchange([pad(packed, 0.0)], False, "gather_small_grads")[0]
    n_rep = DEPTH * sum(sizes[:-1])
    pack_rep = lambda d, fill: pad(jnp.concatenate([d[k].reshape(-1) for k in SMALL]
                                                   + [jnp.full((n_small - n_rep,), fill, F32)]), fill)
    res = _adamw(parts, pack_rep(w, 1.0), pack_rep(m, 1.0), pack_rep(v, 1.0), "adamw_small")
    res = [t.reshape(-1) for t in res]
    off = 0
    for k, sz in zip(SMALL, sizes[:-1]):
        out[k] = [t[off:off + DEPTH * sz].reshape(w[k].shape) for t in res]
        off += DEPTH * sz
    gconv = res[0][off:off + DEPTH * sizes[-1]].reshape(DEPTH, SSD_CONV, SSD_CONV_CH)
    gconv = lax.dynamic_slice_in_dim(gconv, me * conv_w.shape[2], conv_w.shape[2], axis=2)
    flat = lambda t: t.reshape(8, -1)
    resc = _adamw(flat(gconv)[None], flat(conv_w), flat(m_conv_w), flat(v_conv_w), "adamw_conv_w")
    out["conv_w"] = [t.reshape(conv_w.shape) for t in resc]

    return (loss, gx[None], *[out[k][0] for k in ORDER], *[out[k][1] for k in ORDER],
            *[out[k][2] for k in ORDER], *[out[k][3] for k in ORDER])
```

```python
import functools
import math

import jax
import jax.numpy as jnp
import numpy as np
from jax import lax
from jax.experimental import pallas as pl
from jax.experimental.pallas import tpu as pltpu

F32 = jnp.float32
BF16 = jnp.bfloat16

N_DEV = 8
MESH_AXES = ("x", "y", "c")
D_MODEL = 2048
DEPTH = 2
EPS = 1e-6
ROPE_THETA = 10000.0
SSD_HEADS = 16
SSD_HEAD_DIM = 64
SSD_WIDTH = 1024
SSD_GROUPS = 2
SSD_STATE = 128
SSD_CONV = 4
SSD_CONV_CH = 1536
ATT_HEADS = 8
ATT_HEAD_DIM = 64
ATT_WIDTH = 512
DILATED_PAIRS = ((128, 1), (512, 4), (2048, 16))
RET_HEADS = 4
RET_QK_DIM = 64
RET_V_DIM = 128
RET_QK_WIDTH = 256
RET_V_WIDTH = 512
CHUNK = 128
ATT_SPAN = 2048
ATT_STRIP = ATT_SPAN + CHUNK
IN_WIDTH = 5648
IN_PAD = 5760
DT_COL = 5632
D_FF = 5632
ADAM_LR = 0.001
ADAM_B1 = 0.9
ADAM_B2 = 0.999
ADAM_EPS = 1e-08
ADAM_WD = 0.01
ADAM_STEP = 10
NEG = -1e30
VMEM_LIMIT_V7X = 60 * 1024 * 1024

NN = (((1,), (0,)), ((), ()))
NT = (((1,), (1,)), ((), ()))
TN = (((0,), (0,)), ((), ()))


def _bdot(a, b, dims):
    return lax.dot_general(a.astype(BF16), b.astype(BF16), dims, preferred_element_type=F32)


def _xdot(a, b, dims):
    return lax.dot_general(a, b, dims, precision=lax.Precision.HIGHEST, preferred_element_type=F32)


def _params(*sem):
    return pltpu.CompilerParams(dimension_semantics=sem, vmem_limit_bytes=VMEM_LIMIT_V7X)


def _sigmoid(v):
    return 1.0 / (1.0 + jnp.exp(-v))


def _silu_grad(v, s):
    return s * (1.0 + v * (1.0 - s))


def _col_of_row(row, eye):
    return jnp.sum(jnp.where(eye, row, 0.0), axis=1, keepdims=True)


def _rmsnorm_fwd(x, g, name):
    S, D = x.shape
    tr = min(512, S)

    def body(x_ref, g_ref, o_ref):
        xv = x_ref[...]
        r = lax.rsqrt(jnp.mean(xv * xv, axis=-1, keepdims=True) + EPS)
        o_ref[...] = (xv * r * g_ref[...]).astype(o_ref.dtype)

    return pl.pallas_call(
        body, grid=(S // tr,),
        in_specs=[pl.BlockSpec((tr, D), lambda i: (i, 0)), pl.BlockSpec((1, D), lambda i: (0, 0))],
        out_specs=pl.BlockSpec((tr, D), lambda i: (i, 0)),
        out_shape=jax.ShapeDtypeStruct((S, D), BF16), name=name, compiler_params=_params("parallel"),
    )(x, g)


def _rmsnorm_bwd(x, dy, g, dres, name):
    S, D = x.shape
    tr = min(512, S)

    def body(x_ref, dy_ref, g_ref, dres_ref, dx_ref, dg_ref):
        i = pl.program_id(0)
        xv = x_ref[...]
        r = lax.rsqrt(jnp.mean(xv * xv, axis=-1, keepdims=True) + EPS)
        n = xv * r
        dy = dy_ref[...]
        dn = dy * g_ref[...]
        dx_ref[...] = dres_ref[...] + r * (dn - n * jnp.mean(dn * n, axis=-1, keepdims=True))
        part = jnp.sum(dy * n, axis=0, keepdims=True)

        @pl.when(i == 0)
        def _():
            dg_ref[...] = part

        @pl.when(i > 0)
        def _():
            dg_ref[...] += part

    row = pl.BlockSpec((tr, D), lambda i: (i, 0))
    vec = pl.BlockSpec((1, D), lambda i: (0, 0))
    return pl.pallas_call(
        body, grid=(S // tr,), in_specs=[row, row, vec, row], out_specs=[row, vec],
        out_shape=[jax.ShapeDtypeStruct((S, D), F32), jax.ShapeDtypeStruct((1, D), F32)],
        name=name, compiler_params=_params("arbitrary"),
    )(x, dy, g, dres)


def _loss_grad(y, tgt, name):
    S, D = y.shape
    tr = min(512, S)

    def body(y_ref, t_ref, dy_ref, l_ref):
        i = pl.program_id(0)
        err = y_ref[...] - t_ref[...]
        dy_ref[...] = err * (1.0 / D)
        part = jnp.sum(jnp.sum(err * err, axis=1, keepdims=True), axis=0, keepdims=True) * (0.5 / D)

        @pl.when(i == 0)
        def _():
            l_ref[...] = jnp.zeros_like(l_ref)

        l_ref[...] += part

    row = pl.BlockSpec((tr, D), lambda i: (i, 0))
    return pl.pallas_call(
        body, grid=(S // tr,), in_specs=[row, row],
        out_specs=[row, pl.BlockSpec((8, 128), lambda i: (0, 0))],
        out_shape=[jax.ShapeDtypeStruct((S, D), F32), jax.ShapeDtypeStruct((8, 128), F32)],
        name=name, compiler_params=_params("arbitrary"),
    )(y, tgt)


def _pick(n, cands):
    for c in cands:
        if n % c == 0:
            return c
    return n


def _mm(a, b, mode, name, out_dtype=F32, residual=None, tm=None, tn=None, tk=None):
    if mode == "nn":
        (M, K), (_, N) = a.shape, b.shape
    elif mode == "nt":
        (M, K), (N, _) = a.shape, b.shape
    else:
        (K, M), (_, N) = a.shape, b.shape
    tm = min(tm, M) if tm else _pick(M, (1024, 512, 256, 128))
    tn = min(tn, N) if tn else _pick(N, (1024, 1152, 1408, 512, 256, 128))
    tk = min(tk, K) if tk else _pick(K, (2048, 1920, 1408, 1024, 512, 256, 128))
    assert M % tm == 0 and N % tn == 0 and K % tk == 0, (name, M, N, K, tm, tn, tk)
    nk = K // tk
    a_spec = pl.BlockSpec((tk, tm), lambda i, j, k: (k, i)) if mode == "tn" else pl.BlockSpec((tm, tk), lambda i, j, k: (i, k))
    b_spec = pl.BlockSpec((tn, tk), lambda i, j, k: (j, k)) if mode == "nt" else pl.BlockSpec((tk, tn), lambda i, j, k: (k, j))
    o_spec = pl.BlockSpec((tm, tn), lambda i, j, k: (i, j))
    dims = {"nn": NN, "nt": NT, "tn": TN}[mode]
    has_res = residual is not None

    def body(*refs):
        a_ref, b_ref = refs[0], refs[1]
        r_ref = refs[2] if has_res else None
        o_ref = refs[3] if has_res else refs[2]
        p = _bdot(a_ref[...], b_ref[...], dims)

        def finish(acc):
            if has_res:
                acc = acc + r_ref[...]
            o_ref[...] = acc.astype(o_ref.dtype)

        if nk == 1:
            finish(p)
        else:
            acc_ref = refs[-1]
            k = pl.program_id(2)

            @pl.when(k == 0)
            def _():
                acc_ref[...] = p

            @pl.when(k > 0)
            def _():
                acc_ref[...] += p

            @pl.when(k == nk - 1)
            def _():
                finish(acc_ref[...])

    ins = [a, b] + ([residual] if has_res else [])
    in_specs = [a_spec, b_spec] + ([o_spec] if has_res else [])
    scratch = [pltpu.VMEM((tm, tn), F32)] if nk > 1 else []
    return pl.pallas_call(
        body, grid=(M // tm, N // tn, nk), in_specs=in_specs, out_specs=o_spec,
        out_shape=jax.ShapeDtypeStruct((M, N), out_dtype), scratch_shapes=scratch, name=name,
        compiler_params=_params("parallel", "parallel", "arbitrary"),
    )(*ins)


def _swiglu_fwd(hn, wg, wu, name):
    S, K = hn.shape
    F = wg.shape[1]
    tm = _pick(S, (1024, 512))
    tn = _pick(F, (512, 256, 128))

    def body(a_ref, wg_ref, wu_ref, g_ref, u_ref, act_ref):
        a = a_ref[...]
        g = _bdot(a, wg_ref[...], NN)
        u = _bdot(a, wu_ref[...], NN)
        g_ref[...] = g.astype(BF16)
        u_ref[...] = u.astype(BF16)
        act_ref[...] = (g * _sigmoid(g) * u).astype(BF16)

    w_spec = pl.BlockSpec((K, tn), lambda i, j: (0, j))
    o_spec = pl.BlockSpec((tm, tn), lambda i, j: (i, j))
    sh = jax.ShapeDtypeStruct((S, F), BF16)
    return pl.pallas_call(
        body, grid=(S // tm, F // tn), in_specs=[pl.BlockSpec((tm, K), lambda i, j: (i, 0)), w_spec, w_spec],
        out_specs=[o_spec, o_spec, o_spec], out_shape=[sh, sh, sh], name=name,
        compiler_params=_params("parallel", "parallel"),
    )(hn, wg, wu)


def _swiglu_bwd(dx, wd, g, u, name):
    S, K = dx.shape
    F = wd.shape[0]
    tm = _pick(S, (1024, 512))
    tn = _pick(F, (512, 256, 128))

    def body(dx_ref, wd_ref, g_ref, u_ref, dg_ref, du_ref):
        da = _bdot(dx_ref[...], wd_ref[...], NT)
        gv = g_ref[...].astype(F32)
        uv = u_ref[...].astype(F32)
        s = _sigmoid(gv)
        dg_ref[...] = (da * uv * _silu_grad(gv, s)).astype(BF16)
        du_ref[...] = (da * gv * s).astype(BF16)

    o_spec = pl.BlockSpec((tm, tn), lambda i, j: (i, j))
    sh = jax.ShapeDtypeStruct((S, F), BF16)
    return pl.pallas_call(
        body, grid=(S // tm, F // tn),
        in_specs=[pl.BlockSpec((tm, K), lambda i, j: (i, 0)), pl.BlockSpec((tn, K), lambda i, j: (j, 0)), o_spec, o_spec],
        out_specs=[o_spec, o_spec], out_shape=[sh, sh], name=name, compiler_params=_params("parallel", "parallel"),
    )(dx, wd, g, u)


def _mm_nt2(a1, b1, a2, b2, name):
    M, K = a1.shape
    N = b1.shape[0]
    tm = _pick(M, (1024, 512))
    tn = _pick(N, (1024, 512))
    tk = _pick(K, (1408, 1024, 512, 256, 128))
    nk = K // tk

    def body(a1_ref, b1_ref, a2_ref, b2_ref, o_ref, acc_ref):
        k = pl.program_id(2)
        p = _bdot(a1_ref[...], b1_ref[...], NT) + _bdot(a2_ref[...], b2_ref[...], NT)

        @pl.when(k == 0)
        def _():
            acc_ref[...] = p

        @pl.when(k > 0)
        def _():
            acc_ref[...] += p

        @pl.when(k == nk - 1)
        def _():
            o_ref[...] = acc_ref[...]

    a_spec = pl.BlockSpec((tm, tk), lambda i, j, k: (i, k))
    b_spec = pl.BlockSpec((tn, tk), lambda i, j, k: (j, k))
    return pl.pallas_call(
        body, grid=(M // tm, N // tn, nk), in_specs=[a_spec, b_spec, a_spec, b_spec],
        out_specs=pl.BlockSpec((tm, tn), lambda i, j, k: (i, j)), out_shape=jax.ShapeDtypeStruct((M, N), F32),
        scratch_shapes=[pltpu.VMEM((tm, tn), F32)], name=name,
        compiler_params=_params("parallel", "parallel", "arbitrary"),
    )(a1, b1, a2, b2)


XBC_BLK0 = SSD_WIDTH // 128


def _conv_fwd(proj, w, b, name):
    S = proj.shape[0]
    T = min(512, S)

    def body(x_ref, w_ref, b_ref, o_ref, xp_ref):
        xp_ref[pl.ds(0, 8), :] = jnp.zeros((8, 128), F32)
        xp_ref[pl.ds(8, S), :] = x_ref[...]
        wv = w_ref[...]
        bv = b_ref[...]

        def step(c, carry):
            base = pl.multiple_of(c * T, T)
            acc = wv[0:1] * xp_ref[pl.ds(base + 5, T), :]
            for i in range(1, SSD_CONV):
                acc = acc + wv[i:i + 1] * xp_ref[pl.ds(base + 5 + i, T), :]
            acc = bv + acc
            o_ref[pl.ds(base, T), :] = acc * _sigmoid(acc)
            return carry

        lax.fori_loop(0, S // T, step, 0)

    return pl.pallas_call(
        body, grid=(SSD_CONV_CH // 128,),
        in_specs=[pl.BlockSpec((S, 128), lambda j: (0, XBC_BLK0 + j)), pl.BlockSpec((SSD_CONV, 128), lambda j: (0, j)),
                  pl.BlockSpec((1, 128), lambda j: (0, j))],
        out_specs=pl.BlockSpec((S, 128), lambda j: (0, j)),
        out_shape=jax.ShapeDtypeStruct((S, SSD_CONV_CH), F32),
        scratch_shapes=[pltpu.VMEM((S + 8, 128), F32)], name=name, compiler_params=_params("parallel"),
    )(proj, w, b)


def _conv_bwd(dact, proj, w, b, name):
    S = proj.shape[0]
    T = min(512, S)

    def body(da_ref, x_ref, w_ref, b_ref, dx_ref, dw_ref, db_ref, xp_ref, dcp_ref):
        xp_ref[pl.ds(0, 8), :] = jnp.zeros((8, 128), F32)
        xp_ref[pl.ds(8, S), :] = x_ref[...]
        dcp_ref[pl.ds(S, 8), :] = jnp.zeros((8, 128), F32)
        wv = w_ref[...]
        bv = b_ref[...]

        def step1(c, carry):
            base = pl.multiple_of(c * T, T)
            xs = [xp_ref[pl.ds(base + 5 + i, T), :] for i in range(SSD_CONV)]
            acc = wv[0:1] * xs[0]
            for i in range(1, SSD_CONV):
                acc = acc + wv[i:i + 1] * xs[i]
            acc = bv + acc
            s = _sigmoid(acc)
            dc = da_ref[pl.ds(base, T), :] * _silu_grad(acc, s)
            dcp_ref[pl.ds(base, T), :] = dc
            new = tuple(carry[i] + jnp.sum(xs[i] * dc, axis=0, keepdims=True) for i in range(SSD_CONV))
            return new + (carry[SSD_CONV] + jnp.sum(dc, axis=0, keepdims=True),)

        z = jnp.zeros((1, 128), F32)
        res = lax.fori_loop(0, S // T, step1, (z,) * (SSD_CONV + 1))
        for i in range(SSD_CONV):
            dw_ref[pl.ds(i, 1), :] = res[i]
        db_ref[...] = res[SSD_CONV]

        def step2(c, carry):
            base = pl.multiple_of(c * T, T)
            acc = wv[0:1] * dcp_ref[pl.ds(base + 3, T), :]
            for i in range(1, SSD_CONV):
                acc = acc + wv[i:i + 1] * dcp_ref[pl.ds(base + 3 - i, T), :]
            dx_ref[pl.ds(base, T), :] = acc.astype(dx_ref.dtype)
            return carry

        lax.fori_loop(0, S // T, step2, 0)

    col = pl.BlockSpec((S, 128), lambda j: (0, j))
    return pl.pallas_call(
        body, grid=(SSD_CONV_CH // 128,),
        in_specs=[col, pl.BlockSpec((S, 128), lambda j: (0, XBC_BLK0 + j)), pl.BlockSpec((SSD_CONV, 128), lambda j: (0, j)),
                  pl.BlockSpec((1, 128), lambda j: (0, j))],
        out_specs=[col, pl.BlockSpec((SSD_CONV, 128), lambda j: (0, j)), pl.BlockSpec((1, 128), lambda j: (0, j))],
        out_shape=[jax.ShapeDtypeStruct((S, SSD_CONV_CH), BF16), jax.ShapeDtypeStruct((SSD_CONV, SSD_CONV_CH), F32),
                   jax.ShapeDtypeStruct((1, SSD_CONV_CH), F32)],
        scratch_shapes=[pltpu.VMEM((S + 8, 128), F32), pltpu.VMEM((S + 8, 128), F32)], name=name,
        compiler_params=_params("parallel"),
    )(dact, proj, w, b)


HPG = SSD_HEADS // SSD_GROUPS
GW = HPG * SSD_HEAD_DIM


def _ssd_chunk_terms(dtr, bias, alog, tril, triu):
    pre = dtr + bias
    dt = jnp.maximum(pre, 0.0) + jnp.log(1.0 + jnp.exp(-jnp.abs(pre)))
    a_neg = -jnp.exp(alog)
    a = dt * a_neg
    acum = _xdot(tril, a, NN)
    acum_t = _xdot(a, triu, TN)
    return pre, dt, a_neg, acum, acum_t


def _ssd_fwd(xbc, proj, dtr, dt_bias, a_log, d_rep, gain, name):
    S = xbc.shape[0]
    L = CHUNK
    T = min(512, S)
    CPS = T // L
    NC = S // L

    def body(x_ref, b_ref, c_ref, z_ref, dtr_ref, bias_ref, alog_ref, d_ref, gain_ref, y_ref, yraw_ref, st_ref, state):
        i = pl.program_id(1)

        @pl.when(i == 0)
        def _():
            state[...] = jnp.zeros_like(state)

        row = lax.broadcasted_iota(jnp.int32, (L, L), 0)
        col = lax.broadcasted_iota(jnp.int32, (L, L), 1)
        causal = row >= col
        tril = causal.astype(F32)
        triu = (row <= col).astype(F32)
        for c in range(CPS):
            rows = pl.ds(c * L, L)
            xv = x_ref[rows, :]
            bm = b_ref[rows, :]
            cm = c_ref[rows, :]
            _, dt, _, acum, acum_t = _ssd_chunk_terms(dtr_ref[rows, :], bias_ref[...], alog_ref[...], tril, triu)
            gmat = _bdot(cm, bm, NT)
            for j in range(HPG):
                ac = acum[:, j:j + 1]
                lam = jnp.exp(jnp.where(causal, ac - acum_t[j:j + 1, :], NEG))
                xd = xv[:, j * 64:(j + 1) * 64] * dt[:, j:j + 1]
                sp = state[j]
                st_ref[c, j] = sp
                ac_last = ac[L - 1:L, :]
                y = _bdot(gmat * lam, xd, NN) + _bdot(cm, sp, NT) * jnp.exp(ac)
                yraw_ref[rows, j * 64:(j + 1) * 64] = y
                state[j] = jnp.exp(ac_last) * sp + _bdot(xd * jnp.exp(ac_last - ac), bm, TN)
            zz = z_ref[rows, :]
            u = (yraw_ref[rows, :] + xv * d_ref[...]) * (zz * _sigmoid(zz))
            r = lax.rsqrt(jnp.mean(u * u, axis=-1, keepdims=True) + EPS)
            y_ref[rows, :] = (u * r * gain_ref[...]).astype(y_ref.dtype)

    vec8 = pl.BlockSpec((None, 1, HPG), lambda g, i: (g, 0, 0))
    return pl.pallas_call(
        body, grid=(SSD_GROUPS, S // T),
        in_specs=[pl.BlockSpec((T, GW), lambda g, i: (i, g)),
                  pl.BlockSpec((T, SSD_STATE), lambda g, i: (i, SSD_WIDTH // SSD_STATE + g)),
                  pl.BlockSpec((T, SSD_STATE), lambda g, i: (i, SSD_WIDTH // SSD_STATE + SSD_GROUPS + g)),
                  pl.BlockSpec((T, GW), lambda g, i: (i, g)),
                  pl.BlockSpec((None, T, HPG), lambda g, i: (g, i, 0)),
                  vec8, vec8,
                  pl.BlockSpec((1, GW), lambda g, i: (0, g)), pl.BlockSpec((1, GW), lambda g, i: (0, g))],
        out_specs=[pl.BlockSpec((T, GW), lambda g, i: (i, g)), pl.BlockSpec((T, GW), lambda g, i: (i, g)),
                   pl.BlockSpec((CPS, HPG, SSD_HEAD_DIM, SSD_STATE), lambda g, i: (i, g, 0, 0))],
        out_shape=[jax.ShapeDtypeStruct((S, SSD_WIDTH), BF16), jax.ShapeDtypeStruct((S, SSD_WIDTH), F32),
                   jax.ShapeDtypeStruct((NC, SSD_HEADS, SSD_HEAD_DIM, SSD_STATE), F32)],
        scratch_shapes=[pltpu.VMEM((HPG, SSD_HEAD_DIM, SSD_STATE), F32)], name=name,
        compiler_params=_params("arbitrary", "arbitrary"),
    )(xbc, xbc, xbc, proj, dtr, dt_bias, a_log, d_rep, gain)


def _ssd_bwd(dy, yraw, xbc, proj, dtr, dt_bias, a_log, d_rep, gain, states, name):
    S = xbc.shape[0]
    L = CHUNK
    T = min(512, S)
    CPS = T // L
    NI = S // T

    def body(dy_ref, yraw_ref, x_ref, b_ref, c_ref, z_ref, dtr_ref, bias_ref, alog_ref, d_ref, gain_ref, st_ref,
             dz_ref, dx_ref, db_ref, dc_ref, ddtr_ref, dbias_ref, dalog_ref, dd_ref, dgain_ref, dstate, dyh_ref, dxs_ref):
        i = pl.program_id(1)

        @pl.when(i == 0)
        def _():
            dstate[...] = jnp.zeros_like(dstate)
            dbias_ref[...] = jnp.zeros_like(dbias_ref)
            dalog_ref[...] = jnp.zeros_like(dalog_ref)
            dd_ref[...] = jnp.zeros_like(dd_ref)
            dgain_ref[...] = jnp.zeros_like(dgain_ref)

        row = lax.broadcasted_iota(jnp.int32, (L, L), 0)
        col = lax.broadcasted_iota(jnp.int32, (L, L), 1)
        causal = row >= col
        eye = row == col
        tril = causal.astype(F32)
        triu = (row <= col).astype(F32)
        lane8 = lax.broadcasted_iota(jnp.int32, (1, HPG), 1)
        last_row = (lax.broadcasted_iota(jnp.int32, (L, 1), 0) == L - 1).astype(F32)
        for c in reversed(range(CPS)):
            rows = pl.ds(c * L, L)
            xv = x_ref[rows, :]
            bm = b_ref[rows, :]
            cm = c_ref[rows, :]
            zz = z_ref[rows, :]
            dvec = d_ref[...]
            gn = gain_ref[...]
            sz = _sigmoid(zz)
            silu_z = zz * sz
            v = yraw_ref[rows, :] + xv * dvec
            u = v * silu_z
            r = lax.rsqrt(jnp.mean(u * u, axis=-1, keepdims=True) + EPS)
            n = u * r
            do = dy_ref[rows, :]
            dgain_ref[...] += jnp.sum(do * n, axis=0, keepdims=True)
            dn = do * gn
            du = r * (dn - n * jnp.mean(dn * n, axis=-1, keepdims=True))
            dz_ref[rows, :] = (du * v * _silu_grad(zz, sz)).astype(dz_ref.dtype)
            dv = du * silu_z
            dyh_ref[...] = dv
            dxs_ref[...] = dv * dvec
            e = dv * xv
            pre, dt, a_neg, acum, acum_t = _ssd_chunk_terms(dtr_ref[rows, :], bias_ref[...], alog_ref[...], tril, triu)
            gmat = _bdot(cm, bm, NT)
            dgmat = jnp.zeros((L, L), F32)
            dbm = jnp.zeros((L, SSD_STATE), F32)
            dcm = jnp.zeros((L, SSD_STATE), F32)
            dac8 = jnp.zeros((L, HPG), F32)
            ddt8 = jnp.zeros((L, HPG), F32)
            dd8 = jnp.zeros((1, HPG), F32)
            for j in range(HPG):
                hs = slice(j * 64, (j + 1) * 64)
                onehot = (lane8 == j).astype(F32)
                ac = acum[:, j:j + 1]
                dtj = dt[:, j:j + 1]
                lam = jnp.exp(jnp.where(causal, ac - acum_t[j:j + 1, :], NEG))
                mh = gmat * lam
                xj = xv[:, hs]
                xd = xj * dtj
                sp = st_ref[c, j]
                ds = dstate[j]
                ea = jnp.exp(ac)
                ac_last = ac[L - 1:L, :]
                ea_last = jnp.exp(ac_last)
                w = jnp.exp(ac_last - ac)
                dyj = dyh_ref[:, hs]
                dye = dyj * ea
                yoff = _bdot(cm, sp, NT) * ea
                bds = _bdot(bm, ds, NT)
                dxd = _bdot(mh, dyj, TN) + bds * w
                dm = _bdot(dyj, xd, NT)
                dgmat = dgmat + dm * lam
                q = dm * mh
                dw = jnp.sum(xd * bds, axis=1, keepdims=True)
                dac = (jnp.sum(q, axis=1, keepdims=True) - _col_of_row(jnp.sum(q, axis=0, keepdims=True), eye)
                       + jnp.sum(dyj * yoff, axis=1, keepdims=True) - dw * w)
                tail = jnp.sum(dw * w, axis=0, keepdims=True) + ea_last * jnp.sum(
                    jnp.sum(ds * sp, axis=1, keepdims=True), axis=0, keepdims=True)
                dac = dac + last_row * tail
                dcm = dcm + _bdot(dye, sp, NN)
                dbm = dbm + _bdot(xd * w, ds, NN)
                dstate[j] = ea_last * ds + _bdot(dye, cm, TN)
                dxs_ref[:, hs] += dxd * dtj
                ddt8 = ddt8 + jnp.sum(dxd * xj, axis=1, keepdims=True) * onehot
                dac8 = dac8 + dac * onehot
                dd8 = dd8 + jnp.sum(jnp.sum(e[:, hs], axis=1, keepdims=True), axis=0, keepdims=True) * onehot
            dx_ref[rows, :] = dxs_ref[...]
            dc_ref[rows, :] = dcm + _bdot(dgmat, bm, NN)
            db_ref[rows, :] = dbm + _bdot(dgmat, cm, TN)
            da8 = _xdot(triu, dac8, NN)
            ddt8 = ddt8 + da8 * a_neg
            dalog_ref[...] += jnp.sum(da8 * dt, axis=0, keepdims=True) * a_neg
            dpre = ddt8 * _sigmoid(pre)
            ddtr_ref[rows, :] = dpre
            dbias_ref[...] += jnp.sum(dpre, axis=0, keepdims=True)
            dd_ref[...] += dd8

    rev = lambda i: NI - 1 - i
    vec8 = pl.BlockSpec((None, 1, HPG), lambda g, i: (g, 0, 0))
    grp = pl.BlockSpec((T, GW), lambda g, i: (rev(i), g))
    bspec = pl.BlockSpec((T, SSD_STATE), lambda g, i: (rev(i), SSD_WIDTH // SSD_STATE + g))
    cspec = pl.BlockSpec((T, SSD_STATE), lambda g, i: (rev(i), SSD_WIDTH // SSD_STATE + SSD_GROUPS + g))
    gvec = pl.BlockSpec((1, GW), lambda g, i: (0, g))
    st_spec = pl.BlockSpec((CPS, HPG, SSD_HEAD_DIM, SSD_STATE), lambda g, i: (rev(i), g, 0, 0))
    small = jax.ShapeDtypeStruct((SSD_GROUPS, 1, HPG), F32)
    return pl.pallas_call(
        body, grid=(SSD_GROUPS, NI),
        in_specs=[grp, grp, grp, bspec, cspec, grp, pl.BlockSpec((None, T, HPG), lambda g, i: (g, rev(i), 0)),
                  vec8, vec8, gvec, gvec, st_spec],
        out_specs=[grp, grp, pl.BlockSpec((T, SSD_STATE), lambda g, i: (rev(i), g)),
                   pl.BlockSpec((T, SSD_STATE), lambda g, i: (rev(i), g)),
                   pl.BlockSpec((None, T, HPG), lambda g, i: (g, rev(i), 0)), vec8, vec8, vec8, gvec],
        out_shape=[jax.ShapeDtypeStruct((S, SSD_WIDTH), BF16), jax.ShapeDtypeStruct((S, SSD_WIDTH), F32),
                   jax.ShapeDtypeStruct((S, SSD_GROUPS * SSD_STATE), F32), jax.ShapeDtypeStruct((S, SSD_GROUPS * SSD_STATE), F32),
                   jax.ShapeDtypeStruct((SSD_GROUPS, S, HPG), F32), small, small, small,
                   jax.ShapeDtypeStruct((1, SSD_WIDTH), F32)],
        scratch_shapes=[pltpu.VMEM((HPG, SSD_HEAD_DIM, SSD_STATE), F32), pltpu.VMEM((L, GW), F32), pltpu.VMEM((L, GW), F32)],
        name=name, compiler_params=_params("arbitrary", "arbitrary"),
    )(dy, yraw, xbc, xbc, xbc, proj, dtr, dt_bias, a_log, d_rep, gain, states)


def _swap_halves(t):
    w = t.shape[1]
    lane = lax.broadcasted_iota(jnp.int32, t.shape, 1)
    return jnp.where((lane % 64) < 32, pltpu.roll(t, w - 32, axis=1), pltpu.roll(t, 32, axis=1))


def _widen(tab, w):
    return tab if w == 128 else jnp.concatenate([tab] * (w // 128), axis=1)


def _rope(t, cos, sin_signed):
    return t * cos + _swap_halves(t) * sin_signed


def _rope_t(d, cos, sin_signed):
    return d * cos - _swap_halves(d) * sin_signed


def _group_sum64(v, bd):
    hi = v.astype(BF16)
    lo = (v - hi.astype(F32)).astype(BF16)
    return (lax.dot_general(hi, bd, NN, preferred_element_type=F32)
            + lax.dot_general(lo, bd, NN, preferred_element_type=F32))


AQ_BLK = 2560 // ATT_WIDTH


def _att_prep_fwd(proj, qg, kg, cos, sin, bd, name):
    S = proj.shape[0]
    T = min(512, S)

    def body(q_ref, k_ref, v_ref, qg_ref, kg_ref, cos_ref, sin_ref, bd_ref, qo_ref, ko_ref, vo_ref):
        cw = _widen(cos_ref[...], ATT_WIDTH)
        sw = _widen(sin_ref[...], ATT_WIDTH)
        bdv = bd_ref[...]

        def norm_rope(t, gain):
            ss = _group_sum64(t * t, bdv)
            return _rope(t * lax.rsqrt(ss * (1.0 / ATT_HEAD_DIM) + EPS) * gain, cw, sw)

        qo_ref[...] = (norm_rope(q_ref[...], qg_ref[...]) * (ATT_HEAD_DIM ** -0.5)).astype(BF16)
        kt = norm_rope(k_ref[...], kg_ref[...]).astype(BF16)
        vt = v_ref[...].astype(BF16)
        for h in range(ATT_HEADS):
            ko_ref[h] = kt[:, h * 64:(h + 1) * 64]
            vo_ref[h] = vt[:, h * 64:(h + 1) * 64]

    vec = pl.BlockSpec((1, ATT_WIDTH), lambda i: (0, 0))
    tab = pl.BlockSpec((T, 128), lambda i: (i, 0))
    hm = pl.BlockSpec((ATT_HEADS, T, ATT_HEAD_DIM), lambda i: (0, i, 0))
    return pl.pallas_call(
        body, grid=(S // T,),
        in_specs=[pl.BlockSpec((T, ATT_WIDTH), lambda i: (i, AQ_BLK)), pl.BlockSpec((T, ATT_WIDTH), lambda i: (i, AQ_BLK + 1)),
                  pl.BlockSpec((T, ATT_WIDTH), lambda i: (i, AQ_BLK + 2)), vec, vec, tab, tab,
                  pl.BlockSpec((ATT_WIDTH, ATT_WIDTH), lambda i: (0, 0))],
        out_specs=[pl.BlockSpec((T, ATT_WIDTH), lambda i: (i, 0)), hm, hm],
        out_shape=[jax.ShapeDtypeStruct((S, ATT_WIDTH), BF16), jax.ShapeDtypeStruct((ATT_HEADS, S, ATT_HEAD_DIM), BF16),
                   jax.ShapeDtypeStruct((ATT_HEADS, S, ATT_HEAD_DIM), BF16)],
        name=name, compiler_params=_params("parallel"),
    )(proj, proj, proj, qg, kg, cos, sin, bd)


def _att_prep_bwd(proj, dq, dk, qg, kg, cos, sin, bd, name):
    S = proj.shape[0]
    T = min(512, S)
    NI = S // T

    def body(q_ref, k_ref, dq_ref, dk_ref, qg_ref, kg_ref, cos_ref, sin_ref, bd_ref, dqo_ref, dko_ref, dqg_ref, dkg_ref, acc_ref):
        i = pl.program_id(0)

        @pl.when(i == 0)
        def _():
            acc_ref[...] = jnp.zeros_like(acc_ref)

        cw = _widen(cos_ref[...], ATT_WIDTH)
        sw = _widen(sin_ref[...], ATT_WIDTH)
        bdv = bd_ref[...]

        def one(t, d_rot, gain, scale, slot):
            ss = _group_sum64(t * t, bdv)
            r = lax.rsqrt(ss * (1.0 / ATT_HEAD_DIM) + EPS)
            n = t * r
            d_ng = _rope_t(d_rot * scale, cw, sw)
            acc_ref[pl.ds(slot, 1), :] += jnp.sum(d_ng * n, axis=0, keepdims=True)
            dn = d_ng * gain
            return r * (dn - n * (_group_sum64(dn * n, bdv) * (1.0 / ATT_HEAD_DIM)))

        dqo_ref[...] = one(q_ref[...], dq_ref[...], qg_ref[...], ATT_HEAD_DIM ** -0.5, 0).astype(BF16)
        dko_ref[...] = one(k_ref[...], dk_ref[...], kg_ref[...], 1.0, 1).astype(BF16)

        @pl.when(i == NI - 1)
        def _():
            a = acc_ref[...]
            f = a[:, 0:64]
            for h in range(1, ATT_HEADS):
                f = f + a[:, h * 64:(h + 1) * 64]
            dqg_ref[...] = f[0:1]
            dkg_ref[...] = f[1:2]

    vec = pl.BlockSpec((1, ATT_WIDTH), lambda i: (0, 0))
    tab = pl.BlockSpec((T, 128), lambda i: (i, 0))
    row = pl.BlockSpec((T, ATT_WIDTH), lambda i: (i, 0))
    g64 = pl.BlockSpec((1, ATT_HEAD_DIM), lambda i: (0, 0))
    return pl.pallas_call(
        body, grid=(NI,),
        in_specs=[pl.BlockSpec((T, ATT_WIDTH), lambda i: (i, AQ_BLK)), pl.BlockSpec((T, ATT_WIDTH), lambda i: (i, AQ_BLK + 1)),
                  row, row, vec, vec, tab, tab, pl.BlockSpec((ATT_WIDTH, ATT_WIDTH), lambda i: (0, 0))],
        out_specs=[row, row, g64, g64],
        out_shape=[jax.ShapeDtypeStruct((S, ATT_WIDTH), BF16), jax.ShapeDtypeStruct((S, ATT_WIDTH), BF16),
                   jax.ShapeDtypeStruct((1, ATT_HEAD_DIM), F32), jax.ShapeDtypeStruct((1, ATT_HEAD_DIM), F32)],
        scratch_shapes=[pltpu.VMEM((8, ATT_WIDTH), F32)], name=name, compiler_params=_params("arbitrary"),
    )(proj, proj, dq, dk, qg, kg, cos, sin, bd)


def _att_bias():
    qpos = np.arange(CHUNK)[:, None] + ATT_SPAN
    kpos = np.arange(ATT_STRIP)[None, :]
    rel = qpos - kpos
    mult = np.zeros((CHUNK, ATT_STRIP), np.float64)
    for window, dil in DILATED_PAIRS:
        mult += (rel >= 0) & (rel % dil == 0) & (rel // dil <= window // dil)
    return np.where(mult > 0, np.log(np.maximum(mult, 1.0)), NEG).astype(np.float32)


def _att_scores(q, ks, bias, i):
    s = _bdot(q, ks, NT) + bias
    kcol = lax.broadcasted_iota(jnp.int32, (1, ATT_STRIP), 1) + i * CHUNK
    return jnp.where(kcol >= ATT_SPAN, s, NEG)


def _att_fwd(q, kp, vp, bias, name):
    S = q.shape[0]
    SP = kp.shape[1]

    def body(q_ref, k_ref, v_ref, bias_ref, o_ref):
        i = pl.program_id(1)
        start = pl.multiple_of(i * CHUNK, CHUNK)
        qv = q_ref[...]
        for hh in range(2):
            s = _att_scores(qv[:, hh * 64:(hh + 1) * 64], k_ref[hh, pl.ds(start, ATT_STRIP), :], bias_ref[...], i)
            m = jnp.max(s, axis=-1, keepdims=True)
            p = jnp.exp(s - m)
            den = jnp.sum(p, axis=-1, keepdims=True)
            o = _bdot(p, v_ref[hh, pl.ds(start, ATT_STRIP), :], NN) / den
            o_ref[:, hh * 64:(hh + 1) * 64] = o.astype(o_ref.dtype)

    kv = pl.BlockSpec((2, SP, ATT_HEAD_DIM), lambda hp, i: (hp, 0, 0))
    return pl.pallas_call(
        body, grid=(ATT_HEADS // 2, S // CHUNK),
        in_specs=[pl.BlockSpec((CHUNK, 128), lambda hp, i: (i, hp)), kv, kv,
                  pl.BlockSpec((CHUNK, ATT_STRIP), lambda hp, i: (0, 0))],
        out_specs=pl.BlockSpec((CHUNK, 128), lambda hp, i: (i, hp)),
        out_shape=jax.ShapeDtypeStruct((S, ATT_WIDTH), BF16), name=name, compiler_params=_params("parallel", "arbitrary"),
    )(q, kp, vp, bias)


def _att_bwd(q, kp, vp, bias, dy, name):
    S = q.shape[0]
    SP = kp.shape[1]

    def body(q_ref, k_ref, v_ref, bias_ref, do_ref, dq_ref, dk_ref, dv_ref):
        h = pl.program_id(0)
        i = pl.program_id(1)

        @pl.when(i == 0)
        def _():
            dk_ref[...] = jnp.zeros_like(dk_ref)
            dv_ref[...] = jnp.zeros_like(dv_ref)

        start = pl.multiple_of(i * CHUNK, CHUNK)
        strip = pl.ds(start, ATT_STRIP)
        odd = (h % 2) == 1
        qv = q_ref[...]
        dov = do_ref[...]
        qh = jnp.where(odd, qv[:, 64:128], qv[:, 0:64])
        doh = jnp.where(odd, dov[:, 64:128], dov[:, 0:64])
        ks = k_ref[strip, :]
        vs = v_ref[strip, :]
        s = _att_scores(qh, ks, bias_ref[...], i)
        m = jnp.max(s, axis=-1, keepdims=True)
        p = jnp.exp(s - m)
        p = p / jnp.sum(p, axis=-1, keepdims=True)
        dp = _bdot(doh, vs, NT)
        dsc = p * (dp - jnp.sum(p * dp, axis=-1, keepdims=True))
        dq_ref[...] = _bdot(dsc, ks, NN)
        dv_ref[strip, :] += _bdot(p, doh, TN)
        dk_ref[strip, :] += _bdot(dsc, qh, TN)

    kv = pl.BlockSpec((None, SP, ATT_HEAD_DIM), lambda h, i: (h, 0, 0))
    return pl.pallas_call(
        body, grid=(ATT_HEADS, S // CHUNK),
        in_specs=[pl.BlockSpec((CHUNK, 128), lambda h, i: (i, h // 2)), kv, kv,
                  pl.BlockSpec((CHUNK, ATT_STRIP), lambda h, i: (0, 0)),
                  pl.BlockSpec((CHUNK, 128), lambda h, i: (i, SSD_WIDTH // 128 + h // 2))],
        out_specs=[pl.BlockSpec((None, CHUNK, ATT_HEAD_DIM), lambda h, i: (h, i, 0)), kv, kv],
        out_shape=[jax.ShapeDtypeStruct((ATT_HEADS, S, ATT_HEAD_DIM), F32), jax.ShapeDtypeStruct((ATT_HEADS, SP, ATT_HEAD_DIM), F32),
                   jax.ShapeDtypeStruct((ATT_HEADS, SP, ATT_HEAD_DIM), F32)],
        name=name, compiler_params=_params("parallel", "arbitrary"),
    )(q, kp, vp, bias, dy)


RQ_BLK = 4096 // RET_QK_WIDTH
RV_BLK = 4608 // RET_V_WIDTH
RET_LOG_GAMMA = tuple(math.log1p(-2.0 ** (-5.0 - h)) for h in range(RET_HEADS))


def _ret_decays(h):
    L = CHUNK
    lg = RET_LOG_GAMMA[h]
    row = lax.broadcasted_iota(jnp.int32, (L, L), 0)
    col = lax.broadcasted_iota(jnp.int32, (L, L), 1)
    rel = (row - col).astype(F32)
    dm = jnp.where(rel >= 0, jnp.exp(jnp.maximum(rel, 0.0) * lg), 0.0)
    idx = lax.broadcasted_iota(jnp.int32, (L, 1), 0).astype(F32)
    kte = jnp.exp((L - 1 - idx) * lg)
    qfs = jnp.exp((idx + 1.0) * lg)
    return dm, kte, qfs, math.exp(L * lg)


def _ret_fwd(proj, cos, sin, gain, name):
    S = proj.shape[0]
    L = CHUNK
    T = min(512, S)
    CPS = T // L
    NC = S // L

    def body(q_ref, k_ref, v_ref, g_ref, cos_ref, sin_ref, gain_ref, y_ref, o_ref, st_ref, state):
        i = pl.program_id(0)

        @pl.when(i == 0)
        def _():
            state[...] = jnp.zeros_like(state)

        dec = [_ret_decays(h) for h in range(RET_HEADS)]
        for c in range(CPS):
            rows = pl.ds(c * L, L)
            cw = _widen(cos_ref[rows, :], RET_QK_WIDTH)
            sw = _widen(sin_ref[rows, :], RET_QK_WIDTH)
            qv = _rope(q_ref[rows, :], cw, sw)
            kv = _rope(k_ref[rows, :], cw, sw) * (RET_QK_DIM ** -0.5)
            for h in range(RET_HEADS):
                dm, kte, qfs, cd = dec[h]
                qh = qv[:, h * 64:(h + 1) * 64]
                kh = kv[:, h * 64:(h + 1) * 64]
                vs = slice(h * RET_V_DIM, (h + 1) * RET_V_DIM)
                vh = v_ref[rows, vs]
                sp = state[h]
                st_ref[c, h] = sp
                o = _bdot(_bdot(qh, kh, NT) * dm, vh, NN) + _bdot(qh * qfs, sp, NN)
                state[h] = cd * sp + _bdot(kh * kte, vh, TN)
                o_ref[rows, vs] = o
                gh = g_ref[rows, vs]
                r = lax.rsqrt(jnp.mean(o * o, axis=-1, keepdims=True) + EPS)
                y_ref[rows, vs] = (o * r * gain_ref[:, vs] * (gh * _sigmoid(gh))).astype(y_ref.dtype)

    tab = pl.BlockSpec((T, 128), lambda i: (i, 0))
    wide = pl.BlockSpec((T, RET_V_WIDTH), lambda i: (i, 0))
    return pl.pallas_call(
        body, grid=(S // T,),
        in_specs=[pl.BlockSpec((T, RET_QK_WIDTH), lambda i: (i, RQ_BLK)), pl.BlockSpec((T, RET_QK_WIDTH), lambda i: (i, RQ_BLK + 1)),
                  pl.BlockSpec((T, RET_V_WIDTH), lambda i: (i, RV_BLK)), pl.BlockSpec((T, RET_V_WIDTH), lambda i: (i, RV_BLK + 1)),
                  tab, tab, pl.BlockSpec((1, RET_V_WIDTH), lambda i: (0, 0))],
        out_specs=[wide, wide, pl.BlockSpec((CPS, RET_HEADS, RET_QK_DIM, RET_V_DIM), lambda i: (i, 0, 0, 0))],
        out_shape=[jax.ShapeDtypeStruct((S, RET_V_WIDTH), BF16), jax.ShapeDtypeStruct((S, RET_V_WIDTH), F32),
                   jax.ShapeDtypeStruct((NC, RET_HEADS, RET_QK_DIM, RET_V_DIM), F32)],
        scratch_shapes=[pltpu.VMEM((RET_HEADS, RET_QK_DIM, RET_V_DIM), F32)], name=name,
        compiler_params=_params("arbitrary"),
    )(proj, proj, proj, proj, cos, sin, gain)


def _ret_bwd(dy, oraw, proj, cos, sin, gain, states, name):
    S = proj.shape[0]
    L = CHUNK
    T = min(512, S)
    CPS = T // L
    NI = S // T

    def body(dy_ref, o_ref, q_ref, k_ref, v_ref, g_ref, cos_ref, sin_ref, gain_ref, st_ref,
             dq_ref, dk_ref, dv_ref, dg_ref, dgain_ref, dstate, dqs, dks):
        i = pl.program_id(0)

        @pl.when(i == 0)
        def _():
            dstate[...] = jnp.zeros_like(dstate)
            dgain_ref[...] = jnp.zeros_like(dgain_ref)

        dec = [_ret_decays(h) for h in range(RET_HEADS)]
        for c in reversed(range(CPS)):
            rows = pl.ds(c * L, L)
            cw = _widen(cos_ref[rows, :], RET_QK_WIDTH)
            sw = _widen(sin_ref[rows, :], RET_QK_WIDTH)
            qv = _rope(q_ref[rows, :], cw, sw)
            kv = _rope(k_ref[rows, :], cw, sw) * (RET_QK_DIM ** -0.5)
            for h in range(RET_HEADS):
                dm, kte, qfs, cd = dec[h]
                qs = slice(h * 64, (h + 1) * 64)
                vs = slice(h * RET_V_DIM, (h + 1) * RET_V_DIM)
                qh = qv[:, qs]
                kh = kv[:, qs]
                vh = v_ref[rows, vs]
                gh = g_ref[rows, vs]
                gn = gain_ref[:, vs]
                o = o_ref[rows, vs]
                dyh = dy_ref[rows, vs]
                sg = _sigmoid(gh)
                silu_g = gh * sg
                r = lax.rsqrt(jnp.mean(o * o, axis=-1, keepdims=True) + EPS)
                n = o * r
                dgain_ref[:, vs] += jnp.sum(dyh * n * silu_g, axis=0, keepdims=True)
                dg_ref[rows, vs] = (dyh * n * gn * _silu_grad(gh, sg)).astype(dg_ref.dtype)
                dn = dyh * gn * silu_g
                do = r * (dn - n * jnp.mean(dn * n, axis=-1, keepdims=True))
                sp = st_ref[c, h]
                ds = dstate[h]
                sc = _bdot(qh, kh, NT) * dm
                dsc = _bdot(do, vh, NT) * dm
                dv_ref[rows, vs] = (_bdot(sc, do, TN) + _bdot(kh * kte, ds, NN)).astype(dv_ref.dtype)
                dqs[:, qs] = _bdot(dsc, kh, NN) + _bdot(do, sp, NT) * qfs
                dks[:, qs] = _bdot(dsc, qh, TN) + _bdot(vh, ds, NT) * kte
                dstate[h] = cd * ds + _bdot(qh * qfs, do, TN)
            dq_ref[rows, :] = _rope_t(dqs[...], cw, sw).astype(dq_ref.dtype)
            dk_ref[rows, :] = _rope_t(dks[...] * (RET_QK_DIM ** -0.5), cw, sw).astype(dk_ref.dtype)

    rev = lambda i: NI - 1 - i
    tab = pl.BlockSpec((T, 128), lambda i: (rev(i), 0))
    wide = pl.BlockSpec((T, RET_V_WIDTH), lambda i: (rev(i), 0))
    narrow = pl.BlockSpec((T, RET_QK_WIDTH), lambda i: (rev(i), 0))
    gvec = pl.BlockSpec((1, RET_V_WIDTH), lambda i: (0, 0))
    return pl.pallas_call(
        body, grid=(NI,),
        in_specs=[pl.BlockSpec((T, RET_V_WIDTH), lambda i: (rev(i), (SSD_WIDTH + ATT_WIDTH) // RET_V_WIDTH)), wide,
                  pl.BlockSpec((T, RET_QK_WIDTH), lambda i: (rev(i), RQ_BLK)), pl.BlockSpec((T, RET_QK_WIDTH), lambda i: (rev(i), RQ_BLK + 1)),
                  pl.BlockSpec((T, RET_V_WIDTH), lambda i: (rev(i), RV_BLK)), pl.BlockSpec((T, RET_V_WIDTH), lambda i: (rev(i), RV_BLK + 1)),
                  tab, tab, gvec,
                  pl.BlockSpec((CPS, RET_HEADS, RET_QK_DIM, RET_V_DIM), lambda i: (rev(i), 0, 0, 0))],
        out_specs=[narrow, narrow, wide, wide, gvec],
        out_shape=[jax.ShapeDtypeStruct((S, RET_QK_WIDTH), BF16), jax.ShapeDtypeStruct((S, RET_QK_WIDTH), BF16),
                   jax.ShapeDtypeStruct((S, RET_V_WIDTH), BF16), jax.ShapeDtypeStruct((S, RET_V_WIDTH), BF16),
                   jax.ShapeDtypeStruct((1, RET_V_WIDTH), F32)],
        scratch_shapes=[pltpu.VMEM((RET_HEADS, RET_QK_DIM, RET_V_DIM), F32), pltpu.VMEM((L, RET_QK_WIDTH), F32),
                        pltpu.VMEM((L, RET_QK_WIDTH), F32)],
        name=name, compiler_params=_params("arbitrary"),
    )(dy, oraw, proj, proj, proj, proj, cos, sin, gain, states)


def _adamw_update(g_ref, nb, w_ref, m_ref, v_ref, go_ref, d_ref, mo_ref, vo_ref):
    g = g_ref[0].astype(F32)
    for k in range(1, nb):
        g = g + g_ref[k].astype(F32)
    mn = ADAM_B1 * m_ref[...] + (1.0 - ADAM_B1) * g
    vn = ADAM_B2 * v_ref[...] + (1.0 - ADAM_B2) * (g * g)
    go_ref[...] = g
    mo_ref[...] = mn
    vo_ref[...] = vn
    c1 = 1.0 - ADAM_B1 ** ADAM_STEP
    c2 = 1.0 - ADAM_B2 ** ADAM_STEP
    d_ref[...] = -ADAM_LR * ((mn / c1) / (jnp.sqrt(vn / c2) + ADAM_EPS) + ADAM_WD * w_ref[...])


def _adamw_rows(R, C):
    return _pick(R, tuple(t for t in (512, 256, 128, 64, 32, 16, 8) if t * C <= 256 * 1024))


def _adamw(gblocks, w, m, v, name):
    nb, R, C = gblocks.shape
    tr = _adamw_rows(R, C)

    def body(g_ref, *refs):
        _adamw_update(g_ref, nb, *refs)

    row = pl.BlockSpec((tr, C), lambda i: (i, 0))
    sh = jax.ShapeDtypeStruct((R, C), F32)
    return pl.pallas_call(
        body, grid=(R // tr,), in_specs=[pl.BlockSpec((nb, tr, C), lambda i: (0, i, 0)), row, row, row],
        out_specs=[row, row, row, row], out_shape=[sh, sh, sh, sh], name=name, compiler_params=_params("parallel"),
    )(gblocks, w, m, v)


def _adamw_layers(g0, g1, w, m, v, name):
    nb, R, C = g0.shape
    tr = _adamw_rows(R, C)

    def body(g0_ref, g1_ref, *refs):
        l = pl.program_id(0)

        @pl.when(l == 0)
        def _():
            _adamw_update(g0_ref, nb, *refs)

        @pl.when(l == 1)
        def _():
            _adamw_update(g1_ref, nb, *refs)

    row = pl.BlockSpec((None, tr, C), lambda l, i: (l, i, 0))
    sh = jax.ShapeDtypeStruct((DEPTH, R, C), F32)
    return pl.pallas_call(
        body, grid=(DEPTH, R // tr),
        in_specs=[pl.BlockSpec((nb, tr, C), lambda l, i: (0, i * (1 - l), 0)), pl.BlockSpec((nb, tr, C), lambda l, i: (0, i * l, 0)),
                  row, row, row],
        out_specs=[row, row, row, row], out_shape=[sh, sh, sh, sh], name=name, compiler_params=_params("arbitrary", "arbitrary"),
    )(g0, g1, w, m, v)


def _peers():
    x, y, c = lax.axis_index("x"), lax.axis_index("y"), lax.axis_index("c")
    flips = ((0, 0, 1), (1, 0, 0), (0, 1, 0), (1, 1, 0), (1, 0, 1), (0, 1, 1), (1, 1, 1))
    me = 4 * x + 2 * y + c
    peers = [(x ^ fx, y ^ fy, c ^ fc) for fx, fy, fc in flips]
    return me, peers


def _exchange(arrs, scatter, name):
    n = len(arrs)
    npeer = N_DEV - 1

    def body(*refs):
        ins, outs = refs[:n], refs[n:2 * n]
        send_sems, recv_sems, local_sems = refs[2 * n:]
        me, peers = _peers()
        copies = []
        for a in range(n):
            src_own = ins[a].at[me] if scatter else ins[a]
            own = pltpu.make_async_copy(src_own, outs[a].at[me], local_sems.at[a])
            own.start()
            copies.append(own)
            for k, peer in enumerate(peers):
                src = ins[a].at[4 * peer[0] + 2 * peer[1] + peer[2]] if scatter else ins[a]
                cp = pltpu.make_async_remote_copy(
                    src_ref=src, dst_ref=outs[a].at[me], send_sem=send_sems.at[a * npeer + k],
                    recv_sem=recv_sems.at[a * npeer + k], device_id=peer, device_id_type=pl.DeviceIdType.MESH)
                cp.start()
                copies.append(cp)
        for cp in copies:
            cp.wait()

    out_shape = [jax.ShapeDtypeStruct(((N_DEV,) + a.shape[1:]) if scatter else ((N_DEV,) + a.shape), a.dtype) for a in arrs]
    anyspec = pl.BlockSpec(memory_space=pl.ANY)
    return pl.pallas_call(
        body, in_specs=[anyspec] * n, out_specs=[anyspec] * n, out_shape=out_shape,
        scratch_shapes=[pltpu.SemaphoreType.DMA((n * npeer,)), pltpu.SemaphoreType.DMA((n * npeer,)),
                        pltpu.SemaphoreType.DMA((n,))],
        name=name,
    )(*arrs)


def _dev_index(peer):
    return 4 * peer[0] + 2 * peer[1] + peer[2]


def _push_copies(src_refs, land_refs, send_sems, recv_sems, scatter, as_receiver):
    me, peers = _peers()
    npeer = N_DEV - 1
    copies = []
    for a in range(len(src_refs)):
        for k, peer in enumerate(peers):
            src = src_refs[a].at[_dev_index(peer)] if scatter else src_refs[a]
            slot = _dev_index(peer) if as_receiver else me
            copies.append(pltpu.make_async_remote_copy(
                src_ref=src, dst_ref=land_refs[a].at[slot], send_sem=send_sems.at[a * npeer + k],
                recv_sem=recv_sems.at[a * npeer + k], device_id=peer, device_id_type=pl.DeviceIdType.MESH))
    return copies


def _push_start(srcs, lands, scatter, name):
    n = len(srcs)
    nsem = n * (N_DEV - 1)

    def body(*refs):
        for cp in _push_copies(refs[:n], refs[n:2 * n], refs[2 * n], refs[2 * n + 1], scatter, False):
            cp.start()
        token = refs[-1]
        token[...] = jnp.zeros_like(token)

    hbm = pl.BlockSpec(memory_space=pltpu.HBM)
    sem = pl.BlockSpec(memory_space=pltpu.SEMAPHORE)
    arrs = list(srcs) + list(lands)
    return pl.pallas_call(
        body, name=name,
        out_shape=(pltpu.SemaphoreType.DMA((nsem,)), pltpu.SemaphoreType.DMA((nsem,)),
                   *[pltpu.HBM(a.shape, a.dtype) for a in arrs], jax.ShapeDtypeStruct((8, 128), F32)),
        in_specs=[hbm] * (2 * n), out_specs=(sem, sem, *([hbm] * (2 * n)), pl.BlockSpec(memory_space=pltpu.VMEM)),
        input_output_aliases={i: 2 + i for i in range(2 * n)},
        compiler_params=pltpu.CompilerParams(has_side_effects=pltpu.SideEffectType.DATAFLOW_SIDE_EFFECTING),
    )(*[pltpu.with_memory_space_constraint(a, pltpu.HBM) for a in arrs])


def _push_wait(handle, after, scatter, name):
    send_sems, recv_sems, *thru, _ = handle
    n = len(thru) // 2

    def body(*refs):
        for cp in _push_copies(refs[:n], refs[n:2 * n], refs[2 * n], refs[2 * n + 1], scatter, True):
            cp.wait_send()
            cp.wait_recv()

    hbm = pl.BlockSpec(memory_space=pltpu.HBM)
    sem = pl.BlockSpec(memory_space=pltpu.SEMAPHORE)
    outs = pl.pallas_call(
        body, name=name, out_shape=tuple(pltpu.HBM(a.shape, a.dtype) for a in thru),
        in_specs=[hbm] * (2 * n) + [sem, sem, pl.BlockSpec(memory_space=pl.ANY)], out_specs=tuple([hbm] * (2 * n)),
        input_output_aliases={i: i for i in range(2 * n)},
        compiler_params=pltpu.CompilerParams(has_side_effects=pltpu.SideEffectType.DATAFLOW_SIDE_EFFECTING),
    )(*thru, send_sems, recv_sems, after)
    return list(outs[n:])


def _landing(own, me):
    return lax.dynamic_update_index_in_dim(lax.empty((N_DEV,) + own.shape, own.dtype), own, me, 0)


def _tables(S):
    pos = jnp.arange(S, dtype=F32)
    inv = ROPE_THETA ** (-jnp.arange(0, ATT_HEAD_DIM, 2, dtype=F32) / ATT_HEAD_DIM)
    ang = pos[:, None] * inv[None, :]
    cos, sin = jnp.cos(ang), jnp.sin(ang)
    cos128 = jnp.tile(cos, (1, 4))
    sin128 = jnp.tile(jnp.concatenate([-sin, sin], axis=1), (1, 2))
    lane = np.arange(ATT_WIDTH)
    bd = jnp.asarray((lane[:, None] // 64 == lane[None, :] // 64).astype(np.float32), dtype=BF16)
    return cos128, sin128, bd, jnp.asarray(_att_bias())


def _permute_in(w):
    pad = jnp.zeros((w.shape[0], IN_PAD - IN_WIDTH), w.dtype)
    return jnp.concatenate([w[:, :2560], w[:, 2576:], w[:, 2560:2576], pad], axis=1)


def _unpermute_in(g):
    return jnp.concatenate([g[:, :2560], g[:, DT_COL:DT_COL + SSD_HEADS], g[:, 2560:DT_COL]], axis=1)


def _layer_fwd(l, x, p, tabs, late=None):
    cos, sin, bd, bias = tabs
    S = x.shape[0]
    row = lambda v: v.reshape(1, -1)
    hn = _rmsnorm_fwd(x, row(p["ln_mix"]), f"norm_mix_fwd{l}")
    proj = _mm(hn, p["w_in"], "nn", f"in_proj{l}", tn=1152)
    xbc = _conv_fwd(proj, p["conv_w"], row(p["conv_b"]), f"conv_fwd{l}")
    dtr = proj[:, DT_COL:DT_COL + SSD_HEADS].reshape(S, SSD_GROUPS, HPG).transpose(1, 0, 2)
    grp = lambda v: v.reshape(SSD_GROUPS, 1, HPG)
    d_rep = row(jnp.repeat(p["d_skip"], SSD_HEAD_DIM))
    y_ssd, yraw, ssd_st = _ssd_fwd(xbc, proj, dtr, grp(p["dt_bias"]), grp(p["a_log"]), d_rep, row(p["ssd_norm"]), f"ssd_fwd{l}")
    qg = row(jnp.tile(p["q_norm"], ATT_HEADS))
    kg = row(jnp.tile(p["k_norm"], ATT_HEADS))
    aq, ak, av = _att_prep_fwd(proj, qg, kg, cos, sin, bd, f"att_prep_fwd{l}")
    padk = lambda t: jnp.pad(t, ((0, 0), (ATT_SPAN, 0), (0, 0)))
    akp, avp = padk(ak), padk(av)
    y_att = _att_fwd(aq, akp, avp, bias, f"att_fwd{l}")
    y_ret, oraw, ret_st = _ret_fwd(proj, cos, sin, row(p["ret_norm"]), f"ret_fwd{l}")
    y = jnp.concatenate([y_ssd, y_att, y_ret], axis=1)
    if late is not None:
        p.update(late(y))
    x1 = _mm(y, p["w_out"], "nn", f"out_proj{l}", residual=x)
    hn2 = _rmsnorm_fwd(x1, row(p["ln_ffn"]), f"norm_ffn_fwd{l}")
    g, u, act = _swiglu_fwd(hn2, p["w_gate"], p["w_up"], f"swiglu_fwd{l}")
    x2 = _mm(act, p["w_down"], "nn", f"down_proj{l}", residual=x1, tk=1408)
    saved = dict(x=x, hn=hn, proj=proj, xbc=xbc, dtr=dtr, yraw=yraw, ssd_st=ssd_st, aq=aq, akp=akp, avp=avp,
                 oraw=oraw, ret_st=ret_st, y=y, x1=x1, hn2=hn2, g=g, u=u, act=act, d_rep=d_rep, qg=qg, kg=kg)
    return x2, saved


def _layer_bwd(l, dx2, p, sv, tabs, on_ffn=None, on_all=None):
    cos, sin, bd, bias = tabs
    S = dx2.shape[0]
    row = lambda v: v.reshape(1, -1)
    grp = lambda v: v.reshape(SSD_GROUPS, 1, HPG)
    gr = {}
    dg, du = _swiglu_bwd(dx2, p["w_down"], sv["g"], sv["u"], f"swiglu_bwd{l}")
    gr["w_down"] = _mm(sv["act"], dx2, "tn", f"down_wgrad{l}", tm=1408, tn=1024, tk=1024)
    dhn2 = _mm_nt2(dg, p["w_gate"], du, p["w_up"], f"ffn_dgrad{l}")
    gr["w_gate"] = _mm(sv["hn2"], dg, "tn", f"gate_wgrad{l}", tm=1024, tn=1408, tk=1024)
    gr["w_up"] = _mm(sv["hn2"], du, "tn", f"up_wgrad{l}", tm=1024, tn=1408, tk=1024)
    ffn_gain = row(p["ln_ffn"]) + (on_ffn(gr) if on_ffn is not None else 0.0)
    dx1, dln_ffn = _rmsnorm_bwd(sv["x1"], dhn2, ffn_gain, dx2, f"norm_ffn_bwd{l}")
    gr["ln_ffn"] = dln_ffn[0]
    dy = _mm(dx1, p["w_out"], "nt", f"out_dgrad{l}")
    gr["w_out"] = _mm(sv["y"], dx1, "tn", f"out_wgrad{l}", tm=1024, tn=1024, tk=1024)
    dz, dxs, dbm, dcm, ddtr, dbias, dalog, dd, dssd_gain = _ssd_bwd(
        dy, sv["yraw"], sv["xbc"], sv["proj"], sv["dtr"], grp(p["dt_bias"]), grp(p["a_log"]), sv["d_rep"],
        row(p["ssd_norm"]), sv["ssd_st"], f"ssd_bwd{l}")
    gr["dt_bias"], gr["a_log"], gr["d_skip"] = dbias.reshape(-1), dalog.reshape(-1), dd.reshape(-1)
    gr["ssd_norm"] = dssd_gain[0]
    dxbc_act = jnp.concatenate([dxs, dbm, dcm], axis=1)
    dxbc, dconv_w, dconv_b = _conv_bwd(dxbc_act, sv["proj"], p["conv_w"], row(p["conv_b"]), f"conv_bwd{l}")
    gr["conv_w"], gr["conv_b"] = dconv_w, dconv_b[0]
    dq_h, dk_h, dv_h = _att_bwd(sv["aq"], sv["akp"], sv["avp"], bias, dy, f"att_bwd{l}")
    flat = lambda t: t.transpose(1, 0, 2).reshape(S, ATT_WIDTH)
    daq, dak, dqg, dkg = _att_prep_bwd(sv["proj"], flat(dq_h), flat(dk_h[:, ATT_SPAN:]), sv["qg"], sv["kg"], cos, sin, bd,
                                       f"att_prep_bwd{l}")
    dav = flat(dv_h[:, ATT_SPAN:]).astype(BF16)
    gr["q_norm"], gr["k_norm"] = dqg[0], dkg[0]
    drq, drk, drv, drg, dret_gain = _ret_bwd(dy, sv["oraw"], sv["proj"], cos, sin, row(p["ret_norm"]), sv["ret_st"], f"ret_bwd{l}")
    gr["ret_norm"] = dret_gain[0]
    ddt_cols = ddtr.transpose(1, 0, 2).reshape(S, SSD_HEADS).astype(BF16)
    dproj = jnp.concatenate([dz, dxbc, daq, dak, dav, drq, drk, drv, drg, ddt_cols,
                             jnp.zeros((S, IN_PAD - IN_WIDTH), BF16)], axis=1)
    dhn = _mm(dproj, p["w_in"], "nt", f"in_dgrad{l}", tk=1920)
    gr["w_in"] = _mm(sv["hn"], dproj, "tn", f"in_wgrad{l}", tm=1024, tn=1152, tk=1024)
    mix_gain = row(p["ln_mix"]) + (on_all(gr) if on_all is not None else 0.0)
    dx0, dln_mix = _rmsnorm_bwd(sv["x"], dhn, mix_gain, dx1, f"norm_mix_bwd{l}")
    gr["ln_mix"] = dln_mix[0]
    return dx0, gr


def _local_step(x, tgt, layers, late=None, on_ffn=None, on_all=None):
    n = len(layers)
    none = [None] * n
    late, on_ffn, on_all = late or none, on_ffn or none, on_all or none
    tabs = _tables(x.shape[0])
    saved, params = [], []
    h = x
    for l in range(n):
        p = dict(layers[l](h) if callable(layers[l]) else layers[l])
        h, sv = _layer_fwd(l, h, p, tabs, late[l])
        saved.append(sv)
        params.append(p)
    dh, lacc = _loss_grad(h, tgt, "loss_grad")
    grads = [None] * n
    for l in reversed(range(n)):
        dh, grads[l] = _layer_bwd(l, dh, params[l], saved[l], tabs, on_ffn[l], on_all[l])
    return lacc[0, 0], dh, grads


BIG = ("w_in", "w_out", "w_gate", "w_up", "w_down")
SMALL = ("ln_mix", "conv_b", "dt_bias", "a_log", "d_skip", "ssd_norm", "q_norm", "k_norm", "ret_norm", "ln_ffn")
ORDER = ("ln_mix", "w_in", "conv_w", "conv_b", "dt_bias", "a_log", "d_skip", "ssd_norm", "q_norm", "k_norm", "ret_norm",
         "w_out", "ln_ffn", "w_gate", "w_up", "w_down")


COL_SHARDED = ("w_in", "w_gate", "w_up", "conv_w")


def _full_weight(k, gathered):
    if k in COL_SHARDED:
        full = gathered.transpose(1, 0, 2).reshape(gathered.shape[1], -1)
        return _permute_in(full) if k == "w_in" else full
    return gathered.reshape(-1, gathered.shape[2])


def _shard_block(k, g):
    if k == "w_in":
        g = _unpermute_in(g)
    if k in COL_SHARDED:
        return g.reshape(g.shape[0], N_DEV, -1).transpose(1, 0, 2).astype(BF16)
    return g.reshape(N_DEV, -1, g.shape[1]).astype(BF16)


def kernel(x, ln_mix, w_in, conv_w, conv_b, dt_bias, a_log, d_skip, ssd_norm, q_norm, k_norm, ret_norm, w_out, ln_ffn, w_gate, w_up, w_down, loss_target, m_ln_mix, m_w_in, m_conv_w, m_conv_b, m_dt_bias, m_a_log, m_d_skip, m_ssd_norm, m_q_norm, m_k_norm, m_ret_norm, m_w_out, m_ln_ffn, m_w_gate, m_w_up, m_w_down, v_ln_mix, v_w_in, v_conv_w, v_conv_b, v_dt_bias, v_a_log, v_d_skip, v_ssd_norm, v_q_norm, v_k_norm, v_ret_norm, v_w_out, v_ln_ffn, v_w_gate, v_w_up, v_w_down):
    w = dict(ln_mix=ln_mix, w_in=w_in, conv_w=conv_w, conv_b=conv_b, dt_bias=dt_bias, a_log=a_log, d_skip=d_skip,
             ssd_norm=ssd_norm, q_norm=q_norm, k_norm=k_norm, ret_norm=ret_norm, w_out=w_out, ln_ffn=ln_ffn,
             w_gate=w_gate, w_up=w_up, w_down=w_down)
    m = dict(ln_mix=m_ln_mix, w_in=m_w_in, conv_w=m_conv_w, conv_b=m_conv_b, dt_bias=m_dt_bias, a_log=m_a_log,
             d_skip=m_d_skip, ssd_norm=m_ssd_norm, q_norm=m_q_norm, k_norm=m_k_norm, ret_norm=m_ret_norm, w_out=m_w_out,
             ln_ffn=m_ln_ffn, w_gate=m_w_gate, w_up=m_w_up, w_down=m_w_down)
    v = dict(ln_mix=v_ln_mix, w_in=v_w_in, conv_w=v_conv_w, conv_b=v_conv_b, dt_bias=v_dt_bias, a_log=v_a_log,
             d_skip=v_d_skip, ssd_norm=v_ssd_norm, q_norm=v_q_norm, k_norm=v_k_norm, ret_norm=v_ret_norm, w_out=v_w_out,
             ln_ffn=v_ln_ffn, w_gate=v_w_gate, w_up=v_w_up, w_down=v_w_down)
    me = 4 * lax.axis_index("x") + 2 * lax.axis_index("y") + lax.axis_index("c")

    waves = {"a": [("w_in", 0), ("conv_w", 0), ("conv_w", 1)],
             "b": [(k, 0) for k in ("w_out", "w_gate", "w_up", "w_down")],
             "c": [(k, 1) for k in BIG]}
    gather = {}
    for tag, items in waves.items():
        srcs = [w[k][l] if k == "conv_w" else w[k][l].astype(BF16) for k, l in items]
        gather[tag] = _push_start(srcs, [_landing(s, me) for s in srcs], False, f"gather_{tag}_start")
    started = gather["a"][-1][0, 0] + gather["b"][-1][0, 0] + gather["c"][-1][0, 0]
    full = {}

    def arrive(tag, after):
        for (k, l), g in zip(waves[tag], _push_wait(gather[tag], after, False, f"gather_{tag}_wait")):
            full[k, l] = _full_weight(k, g)

    def layer_weights(l, names):
        return {k: full[k, l] for k in names}

    def small_weights(l):
        return {k: w[k][l] for k in SMALL}

    def layer0(h):
        arrive("a", h)
        p = small_weights(0)
        p["ln_mix"] = p["ln_mix"] + started
        return {**p, **layer_weights(0, ("w_in", "conv_w"))}

    def late0(y):
        arrive("b", y)
        return layer_weights(0, waves_b_names)

    def layer1(h):
        arrive("c", h)
        return {**small_weights(1), **layer_weights(1, BIG + ("conv_w",))}

    waves_b_names = tuple(k for k, _ in waves["b"])

    groups = {"1": [(k, 1) for k in BIG], "0a": [(k, 0) for k in ("w_down", "w_gate", "w_up")],
              "0b": [(k, 0) for k in ("w_out", "w_in")]}
    scatter = {}

    def push_grads(tag, gr):
        blocks = [_shard_block(k, gr[k]) for k, _ in groups[tag]]
        lands = [_landing(lax.dynamic_index_in_dim(b, me, 0, keepdims=False), me) for b in blocks]
        scatter[tag] = _push_start(blocks, lands, True, f"scatter_{tag}_start")
        return scatter[tag][-1][0, 0]

    loss_part, gx, grads = _local_step(
        x[0], loss_target[0], [layer0, layer1], late=[late0, None],
        on_ffn=[functools.partial(push_grads, "0a"), None],
        on_all=[functools.partial(push_grads, "0b"), functools.partial(push_grads, "1")])
    loss = lax.psum(loss_part, MESH_AXES)

    out = {}
    recv = {}
    for tag, items in groups.items():
        for item, r in zip(items, _push_wait(scatter[tag], gx, True, f"scatter_{tag}_wait")):
            recv[item] = r
    for k in BIG:
        out[k] = _adamw_layers(recv[k, 0], recv[k, 1], w[k], m[k], v[k], f"adamw_{k}")
    names = SMALL + ("conv_w",)
    sizes = [int(np.prod(grads[0][k].shape)) for k in names]
    packed = jnp.concatenate([jnp.stack([grads[l][k] for l in range(DEPTH)]).reshape(-1) for k in names])
    n_small = packed.shape[0]
    rows_small = -(-n_small // 1024) * 8
    pad = lambda t, fill: jnp.concatenate([t, jnp.full((rows_small * 128 - n_small,), fill, F32)]).reshape(rows_small, 128)
    parts = _exchange([pad(packed, 0.0)], False, "gather_small_grads")[0]
    n_rep = DEPTH * sum(sizes[:-1])
    pack_rep = lambda d, fill: pad(jnp.concatenate([d[k].reshape(-1) for k in SMALL]
                                                   + [jnp.full((n_small - n_rep,), fill, F32)]), fill)
    res = _adamw(parts, pack_rep(w, 1.0), pack_rep(m, 1.0), pack_rep(v, 1.0), "adamw_small")
    res = [t.reshape(-1) for t in res]
    off = 0
    for k, sz in zip(SMALL, sizes[:-1]):
        out[k] = [t[off:off + DEPTH * sz].reshape(w[k].shape) for t in res]
        off += DEPTH * sz
    gconv = res[0][off:off + DEPTH * sizes[-1]].reshape(DEPTH, SSD_CONV, SSD_CONV_CH)
    gconv = lax.dynamic_slice_in_dim(gconv, me * conv_w.shape[2], conv_w.shape[2], axis=2)
    flat = lambda t: t.reshape(8, -1)
    resc = _adamw(flat(gconv)[None], flat(conv_w), flat(m_conv_w), flat(v_conv_w), "adamw_conv_w")
    out["conv_w"] = [t.reshape(conv_w.shape) for t in resc]

    return (loss, gx[None], *[out[k][0] for k in ORDER], *[out[k][1] for k in ORDER],
            *[out[k][2] for k in ORDER], *[out[k][3] for k in ORDER])
```

```python
import functools
import math

import jax
import jax.numpy as jnp
import numpy as np
from jax import lax
from jax.experimental import pallas as pl
from jax.experimental.pallas import tpu as pltpu

F32 = jnp.float32
BF16 = jnp.bfloat16

N_DEV = 8
MESH_AXES = ("x", "y", "c")
D_MODEL = 2048
DEPTH = 2
EPS = 1e-6
ROPE_THETA = 10000.0
SSD_HEADS = 16
SSD_HEAD_DIM = 64
SSD_WIDTH = 1024
SSD_GROUPS = 2
SSD_STATE = 128
SSD_CONV = 4
SSD_CONV_CH = 1536
ATT_HEADS = 8
ATT_HEAD_DIM = 64
ATT_WIDTH = 512
DILATED_PAIRS = ((128, 1), (512, 4), (2048, 16))
RET_HEADS = 4
RET_QK_DIM = 64
RET_V_DIM = 128
RET_QK_WIDTH = 256
RET_V_WIDTH = 512
CHUNK = 128
ATT_SPAN = 2048
ATT_STRIP = ATT_SPAN + CHUNK
IN_WIDTH = 5648
IN_PAD = 5760
DT_COL = 5632
D_FF = 5632
ADAM_LR = 0.001
ADAM_B1 = 0.9
ADAM_B2 = 0.999
ADAM_EPS = 1e-08
ADAM_WD = 0.01
ADAM_STEP = 10
NEG = -1e30
VMEM_LIMIT_V7X = 60 * 1024 * 1024

NN = (((1,), (0,)), ((), ()))
NT = (((1,), (1,)), ((), ()))
TN = (((0,), (0,)), ((), ()))


def _bdot(a, b, dims):
    return lax.dot_general(a.astype(BF16), b.astype(BF16), dims, preferred_element_type=F32)


def _xdot(a, b, dims):
    return lax.dot_general(a, b, dims, precision=lax.Precision.HIGHEST, preferred_element_type=F32)


def _params(*sem):
    return pltpu.CompilerParams(dimension_semantics=sem, vmem_limit_bytes=VMEM_LIMIT_V7X)


def _sigmoid(v):
    return 1.0 / (1.0 + jnp.exp(-v))


def _silu_grad(v, s):
    return s * (1.0 + v * (1.0 - s))


def _rmsnorm_fwd(x, g, name):
    S, D = x.shape
    tr = min(512, S)

    def body(x_ref, g_ref, o_ref):
        xv = x_ref[...]
        r = lax.rsqrt(jnp.mean(xv * xv, axis=-1, keepdims=True) + EPS)
        o_ref[...] = (xv * r * g_ref[...]).astype(o_ref.dtype)

    return pl.pallas_call(
        body, grid=(S // tr,),
        in_specs=[pl.BlockSpec((tr, D), lambda i: (i, 0)), pl.BlockSpec((1, D), lambda i: (0, 0))],
        out_specs=pl.BlockSpec((tr, D), lambda i: (i, 0)),
        out_shape=jax.ShapeDtypeStruct((S, D), BF16), name=name, compiler_params=_params("parallel"),
    )(x, g)


def _rmsnorm_bwd(x, dy, g, dres, name):
    S, D = x.shape
    tr = min(512, S)

    def body(x_ref, dy_ref, g_ref, dres_ref, dx_ref, dg_ref):
        i = pl.program_id(0)
        xv = x_ref[...]
        r = lax.rsqrt(jnp.mean(xv * xv, axis=-1, keepdims=True) + EPS)
        n = xv * r
        dy = dy_ref[...]
        dn = dy * g_ref[...]
        dx_ref[...] = dres_ref[...] + r * (dn - n * jnp.mean(dn * n, axis=-1, keepdims=True))
        part = jnp.sum(dy * n, axis=0, keepdims=True)

        @pl.when(i == 0)
        def _():
            dg_ref[...] = part

        @pl.when(i > 0)
        def _():
            dg_ref[...] += part

    row = pl.BlockSpec((tr, D), lambda i: (i, 0))
    vec = pl.BlockSpec((1, D), lambda i: (0, 0))
    return pl.pallas_call(
        body, grid=(S // tr,), in_specs=[row, row, vec, row], out_specs=[row, vec],
        out_shape=[jax.ShapeDtypeStruct((S, D), F32), jax.ShapeDtypeStruct((1, D), F32)],
        name=name, compiler_params=_params("arbitrary"),
    )(x, dy, g, dres)


def _loss_grad(y, tgt, name):
    S, D = y.shape
    tr = min(512, S)

    def body(y_ref, t_ref, dy_ref, l_ref):
        i = pl.program_id(0)
        err = y_ref[...] - t_ref[...]
        dy_ref[...] = err * (1.0 / D)
        part = jnp.sum(jnp.sum(err * err, axis=1, keepdims=True), axis=0, keepdims=True) * (0.5 / D)

        @pl.when(i == 0)
        def _():
            l_ref[...] = jnp.zeros_like(l_ref)

        l_ref[...] += part

    row = pl.BlockSpec((tr, D), lambda i: (i, 0))
    return pl.pallas_call(
        body, grid=(S // tr,), in_specs=[row, row],
        out_specs=[row, pl.BlockSpec((8, 128), lambda i: (0, 0))],
        out_shape=[jax.ShapeDtypeStruct((S, D), F32), jax.ShapeDtypeStruct((8, 128), F32)],
        name=name, compiler_params=_params("arbitrary"),
    )(y, tgt)


def _pick(n, cands):
    for c in cands:
        if n % c == 0:
            return c
    return n


def _mm(a, b, mode, name, out_dtype=F32, residual=None, tm=None, tn=None, tk=None, after=None):
    if mode == "nn":
        (M, K), (_, N) = a.shape, b.shape
    elif mode == "nt":
        (M, K), (N, _) = a.shape, b.shape
    else:
        (K, M), (_, N) = a.shape, b.shape
    tm = min(tm, M) if tm else _pick(M, (1024, 512, 256, 128))
    tn = min(tn, N) if tn else _pick(N, (1024, 1152, 1408, 512, 256, 128))
    tk = min(tk, K) if tk else _pick(K, (2048, 1920, 1408, 1024, 512, 256, 128))
    assert M % tm == 0 and N % tn == 0 and K % tk == 0, (name, M, N, K, tm, tn, tk)
    nk = K // tk
    a_spec = pl.BlockSpec((tk, tm), lambda i, j, k: (k, i)) if mode == "tn" else pl.BlockSpec((tm, tk), lambda i, j, k: (i, k))
    b_spec = pl.BlockSpec((tn, tk), lambda i, j, k: (j, k)) if mode == "nt" else pl.BlockSpec((tk, tn), lambda i, j, k: (k, j))
    o_spec = pl.BlockSpec((tm, tn), lambda i, j, k: (i, j))
    dims = {"nn": NN, "nt": NT, "tn": TN}[mode]
    has_res = residual is not None

    has_after = after is not None

    def body(*refs):
        a_ref, b_ref = refs[0], refs[1]
        r_ref = refs[2] if has_res else None
        o_ref = refs[2 + has_res + has_after]
        p = _bdot(a_ref[...], b_ref[...], dims)

        def finish(acc):
            if has_res:
                acc = acc + r_ref[...]
            o_ref[...] = acc.astype(o_ref.dtype)

        if nk == 1:
            finish(p)
        else:
            acc_ref = refs[-1]
            k = pl.program_id(2)

            @pl.when(k == 0)
            def _():
                acc_ref[...] = p

            @pl.when(k > 0)
            def _():
                acc_ref[...] += p

            @pl.when(k == nk - 1)
            def _():
                finish(acc_ref[...])

    ins = [a, b] + ([residual] if has_res else []) + ([after] if has_after else [])
    in_specs = [a_spec, b_spec] + ([o_spec] if has_res else []) + ([pl.BlockSpec(memory_space=pl.ANY)] if has_after else [])
    scratch = [pltpu.VMEM((tm, tn), F32)] if nk > 1 else []
    return pl.pallas_call(
        body, grid=(M // tm, N // tn, nk), in_specs=in_specs, out_specs=o_spec,
        out_shape=jax.ShapeDtypeStruct((M, N), out_dtype), scratch_shapes=scratch, name=name,
        compiler_params=_params("parallel", "parallel", "arbitrary"),
    )(*ins)


def _swiglu_fwd(hn, wg, wu, name):
    S, K = hn.shape
    F = wg.shape[1]
    tm = _pick(S, (1024, 512))
    tn = _pick(F, (512, 256, 128))

    def body(a_ref, wg_ref, wu_ref, g_ref, u_ref, act_ref):
        a = a_ref[...]
        g = _bdot(a, wg_ref[...], NN)
        u = _bdot(a, wu_ref[...], NN)
        g_ref[...] = g.astype(BF16)
        u_ref[...] = u.astype(BF16)
        act_ref[...] = (g * _sigmoid(g) * u).astype(BF16)

    w_spec = pl.BlockSpec((K, tn), lambda i, j: (0, j))
    o_spec = pl.BlockSpec((tm, tn), lambda i, j: (i, j))
    sh = jax.ShapeDtypeStruct((S, F), BF16)
    return pl.pallas_call(
        body, grid=(S // tm, F // tn), in_specs=[pl.BlockSpec((tm, K), lambda i, j: (i, 0)), w_spec, w_spec],
        out_specs=[o_spec, o_spec, o_spec], out_shape=[sh, sh, sh], name=name,
        compiler_params=_params("parallel", "parallel"),
    )(hn, wg, wu)


def _swiglu_bwd(dx, wd, g, u, name):
    S, K = dx.shape
    F = wd.shape[0]
    tm = _pick(S, (1024, 512))
    tn = _pick(F, (512, 256, 128))

    def body(dx_ref, wd_ref, g_ref, u_ref, dg_ref, du_ref):
        da = _bdot(dx_ref[...], wd_ref[...], NT)
        gv = g_ref[...].astype(F32)
        uv = u_ref[...].astype(F32)
        s = _sigmoid(gv)
        dg_ref[...] = (da * uv * _silu_grad(gv, s)).astype(BF16)
        du_ref[...] = (da * gv * s).astype(BF16)

    o_spec = pl.BlockSpec((tm, tn), lambda i, j: (i, j))
    sh = jax.ShapeDtypeStruct((S, F), BF16)
    return pl.pallas_call(
        body, grid=(S // tm, F // tn),
        in_specs=[pl.BlockSpec((tm, K), lambda i, j: (i, 0)), pl.BlockSpec((tn, K), lambda i, j: (j, 0)), o_spec, o_spec],
        out_specs=[o_spec, o_spec], out_shape=[sh, sh], name=name, compiler_params=_params("parallel", "parallel"),
    )(dx, wd, g, u)


def _mm_nt2(a1, b1, a2, b2, name):
    M, K = a1.shape
    N = b1.shape[0]
    tm = _pick(M, (1024, 512))
    tn = _pick(N, (1024, 512))
    tk = _pick(K, (1408, 1024, 512, 256, 128))
    nk = K // tk

    def body(a1_ref, b1_ref, a2_ref, b2_ref, o_ref, acc_ref):
        k = pl.program_id(2)
        p = _bdot(a1_ref[...], b1_ref[...], NT) + _bdot(a2_ref[...], b2_ref[...], NT)

        @pl.when(k == 0)
        def _():
            acc_ref[...] = p

        @pl.when(k > 0)
        def _():
            acc_ref[...] += p

        @pl.when(k == nk - 1)
        def _():
            o_ref[...] = acc_ref[...]

    a_spec = pl.BlockSpec((tm, tk), lambda i, j, k: (i, k))
    b_spec = pl.BlockSpec((tn, tk), lambda i, j, k: (j, k))
    return pl.pallas_call(
        body, grid=(M // tm, N // tn, nk), in_specs=[a_spec, b_spec, a_spec, b_spec],
        out_specs=pl.BlockSpec((tm, tn), lambda i, j, k: (i, j)), out_shape=jax.ShapeDtypeStruct((M, N), F32),
        scratch_shapes=[pltpu.VMEM((tm, tn), F32)], name=name,
        compiler_params=_params("parallel", "parallel", "arbitrary"),
    )(a1, b1, a2, b2)


XBC_BLK0 = SSD_WIDTH // 128


def _conv_fwd(proj, w, b, name):
    S = proj.shape[0]
    T = min(512, S)

    def body(x_ref, w_ref, b_ref, o_ref, xp_ref):
        xp_ref[pl.ds(0, 8), :] = jnp.zeros((8, 128), F32)
        xp_ref[pl.ds(8, S), :] = x_ref[...]
        wv = w_ref[...]
        bv = b_ref[...]

        def step(c, carry):
            base = pl.multiple_of(c * T, T)
            acc = wv[0:1] * xp_ref[pl.ds(base + 5, T), :]
            for i in range(1, SSD_CONV):
                acc = acc + wv[i:i + 1] * xp_ref[pl.ds(base + 5 + i, T), :]
            acc = bv + acc
            o_ref[pl.ds(base, T), :] = acc * _sigmoid(acc)
            return carry

        lax.fori_loop(0, S // T, step, 0)

    return pl.pallas_call(
        body, grid=(SSD_CONV_CH // 128,),
        in_specs=[pl.BlockSpec((S, 128), lambda j: (0, XBC_BLK0 + j)), pl.BlockSpec((SSD_CONV, 128), lambda j: (0, j)),
                  pl.BlockSpec((1, 128), lambda j: (0, j))],
        out_specs=pl.BlockSpec((S, 128), lambda j: (0, j)),
        out_shape=jax.ShapeDtypeStruct((S, SSD_CONV_CH), F32),
        scratch_shapes=[pltpu.VMEM((S + 8, 128), F32)], name=name, compiler_params=_params("parallel"),
    )(proj, w, b)


def _conv_bwd(dact, proj, w, b, name):
    S = proj.shape[0]
    T = min(512, S)

    def body(da_ref, x_ref, w_ref, b_ref, dx_ref, dw_ref, db_ref, xp_ref, dcp_ref):
        xp_ref[pl.ds(0, 8), :] = jnp.zeros((8, 128), F32)
        xp_ref[pl.ds(8, S), :] = x_ref[...]
        dcp_ref[pl.ds(S, 8), :] = jnp.zeros((8, 128), F32)
        wv = w_ref[...]
        bv = b_ref[...]

        def step1(c, carry):
            base = pl.multiple_of(c * T, T)
            xs = [xp_ref[pl.ds(base + 5 + i, T), :] for i in range(SSD_CONV)]
            acc = wv[0:1] * xs[0]
            for i in range(1, SSD_CONV):
                acc = acc + wv[i:i + 1] * xs[i]
            acc = bv + acc
            s = _sigmoid(acc)
            dc = da_ref[pl.ds(base, T), :] * _silu_grad(acc, s)
            dcp_ref[pl.ds(base, T), :] = dc
            new = tuple(carry[i] + jnp.sum(xs[i] * dc, axis=0, keepdims=True) for i in range(SSD_CONV))
            return new + (carry[SSD_CONV] + jnp.sum(dc, axis=0, keepdims=True),)

        z = jnp.zeros((1, 128), F32)
        res = lax.fori_loop(0, S // T, step1, (z,) * (SSD_CONV + 1))
        for i in range(SSD_CONV):
            dw_ref[pl.ds(i, 1), :] = res[i]
        db_ref[...] = res[SSD_CONV]

        def step2(c, carry):
            base = pl.multiple_of(c * T, T)
            acc = wv[0:1] * dcp_ref[pl.ds(base + 3, T), :]
            for i in range(1, SSD_CONV):
                acc = acc + wv[i:i + 1] * dcp_ref[pl.ds(base + 3 - i, T), :]
            dx_ref[pl.ds(base, T), :] = acc.astype(dx_ref.dtype)
            return carry

        lax.fori_loop(0, S // T, step2, 0)

    col = pl.BlockSpec((S, 128), lambda j: (0, j))
    return pl.pallas_call(
        body, grid=(SSD_CONV_CH // 128,),
        in_specs=[col, pl.BlockSpec((S, 128), lambda j: (0, XBC_BLK0 + j)), pl.BlockSpec((SSD_CONV, 128), lambda j: (0, j)),
                  pl.BlockSpec((1, 128), lambda j: (0, j))],
        out_specs=[col, pl.BlockSpec((SSD_CONV, 128), lambda j: (0, j)), pl.BlockSpec((1, 128), lambda j: (0, j))],
        out_shape=[jax.ShapeDtypeStruct((S, SSD_CONV_CH), BF16), jax.ShapeDtypeStruct((SSD_CONV, SSD_CONV_CH), F32),
                   jax.ShapeDtypeStruct((1, SSD_CONV_CH), F32)],
        scratch_shapes=[pltpu.VMEM((S + 8, 128), F32), pltpu.VMEM((S + 8, 128), F32)], name=name,
        compiler_params=_params("parallel"),
    )(dact, proj, w, b)


HPG = SSD_HEADS // SSD_GROUPS
GW = HPG * SSD_HEAD_DIM


def _ssd_chunk_terms(dtr, bias, alog, tril, triu):
    pre = dtr + bias
    dt = jnp.maximum(pre, 0.0) + jnp.log(1.0 + jnp.exp(-jnp.abs(pre)))
    a_neg = -jnp.exp(alog)
    a = dt * a_neg
    acum = _xdot(tril, a, NN)
    acum_t = _xdot(a, triu, TN)
    return pre, dt, a_neg, acum, acum_t


def _head_expanders():
    h64 = lax.broadcasted_iota(jnp.int32, (HPG, GW), 0) == lax.broadcasted_iota(jnp.int32, (HPG, GW), 1) // SSD_HEAD_DIM
    h128 = lax.broadcasted_iota(jnp.int32, (HPG, HPG * CHUNK), 0) == lax.broadcasted_iota(jnp.int32, (HPG, HPG * CHUNK), 1) // CHUNK
    return h64.astype(F32), h128.astype(F32)


def _ssd_fwd(xbc, proj, dtr, dt_bias, a_log, d_rep, gain, name):
    S = xbc.shape[0]
    L = CHUNK
    T = min(512, S)
    CPS = T // L
    NC = S // L

    def body(x_ref, b_ref, c_ref, z_ref, dtr_ref, bias_ref, alog_ref, d_ref, gain_ref, y_ref, yraw_ref, st_ref, state):
        i = pl.program_id(1)

        @pl.when(i == 0)
        def _():
            state[...] = jnp.zeros_like(state)

        row = lax.broadcasted_iota(jnp.int32, (L, L), 0)
        col = lax.broadcasted_iota(jnp.int32, (L, L), 1)
        causal = row >= col
        tril = causal.astype(F32)
        triu = (row <= col).astype(F32)
        low = col < SSD_HEAD_DIM
        e64, e128 = _head_expanders()
        for c in range(CPS):
            rows = pl.ds(c * L, L)
            xv = x_ref[rows, :]
            bm = b_ref[rows, :]
            cm = c_ref[rows, :]
            _, dt, _, acum, acum_t = _ssd_chunk_terms(dtr_ref[rows, :], bias_ref[...], alog_ref[...], tril, triu)
            ac = _xdot(acum, e64, NN)
            ac_sq = _xdot(acum, e128, NN)
            xd = xv * _xdot(dt, e64, NN)
            ac_last = ac[L - 1:L, :]
            sp = state[...]
            st_ref[c] = sp
            yoff = _bdot(cm, sp, NN) * jnp.exp(ac)
            state[...] = sp * jnp.exp(ac_last) + _bdot(bm, xd * jnp.exp(ac_last - ac), TN)
            gmat = _bdot(cm, bm, NT)
            for q in range(HPG // 2):
                pair = slice(q * 128, (q + 1) * 128)
                tile = xd[:, pair]
                y = yoff[:, pair]
                for j, keep in ((2 * q, low), (2 * q + 1, ~low)):
                    lam = jnp.exp(jnp.where(causal, ac_sq[:, j * L:(j + 1) * L] - acum_t[j:j + 1, :], NEG))
                    y = y + _bdot(gmat * lam, jnp.where(keep, tile, 0.0), NN)
                yraw_ref[rows, pair] = y
            zz = z_ref[rows, :]
            u = (yraw_ref[rows, :] + xv * d_ref[...]) * (zz * _sigmoid(zz))
            r = lax.rsqrt(jnp.mean(u * u, axis=-1, keepdims=True) + EPS)
            y_ref[rows, :] = (u * r * gain_ref[...]).astype(y_ref.dtype)

    vec8 = pl.BlockSpec((None, 1, HPG), lambda g, i: (g, 0, 0))
    return pl.pallas_call(
        body, grid=(SSD_GROUPS, S // T),
        in_specs=[pl.BlockSpec((T, GW), lambda g, i: (i, g)),
                  pl.BlockSpec((T, SSD_STATE), lambda g, i: (i, SSD_WIDTH // SSD_STATE + g)),
                  pl.BlockSpec((T, SSD_STATE), lambda g, i: (i, SSD_WIDTH // SSD_STATE + SSD_GROUPS + g)),
                  pl.BlockSpec((T, GW), lambda g, i: (i, g)),
                  pl.BlockSpec((None, T, HPG), lambda g, i: (g, i, 0)),
                  vec8, vec8,
                  pl.BlockSpec((1, GW), lambda g, i: (0, g)), pl.BlockSpec((1, GW), lambda g, i: (0, g))],
        out_specs=[pl.BlockSpec((T, GW), lambda g, i: (i, g)), pl.BlockSpec((T, GW), lambda g, i: (i, g)),
                   pl.BlockSpec((CPS, None, SSD_STATE, GW), lambda g, i: (i, g, 0, 0))],
        out_shape=[jax.ShapeDtypeStruct((S, SSD_WIDTH), BF16), jax.ShapeDtypeStruct((S, SSD_WIDTH), F32),
                   jax.ShapeDtypeStruct((NC, SSD_GROUPS, SSD_STATE, GW), F32)],
        scratch_shapes=[pltpu.VMEM((SSD_STATE, GW), F32)], name=name,
        compiler_params=_params("arbitrary", "arbitrary"),
    )(xbc, xbc, xbc, proj, dtr, dt_bias, a_log, d_rep, gain)


def _ssd_bwd(dy, yraw, xbc, proj, dtr, dt_bias, a_log, d_rep, gain, states, name):
    S = xbc.shape[0]
    L = CHUNK
    T = min(512, S)
    CPS = T // L
    NI = S // T

    def body(dy_ref, yraw_ref, x_ref, b_ref, c_ref, z_ref, dtr_ref, bias_ref, alog_ref, d_ref, gain_ref, st_ref,
             dz_ref, dx_ref, db_ref, dc_ref, ddtr_ref, dbias_ref, dalog_ref, dd_ref, dgain_ref, dstate, dxd_ref):
        i = pl.program_id(1)

        @pl.when(i == 0)
        def _():
            dstate[...] = jnp.zeros_like(dstate)
            dbias_ref[...] = jnp.zeros_like(dbias_ref)
            dalog_ref[...] = jnp.zeros_like(dalog_ref)
            dd_ref[...] = jnp.zeros_like(dd_ref)
            dgain_ref[...] = jnp.zeros_like(dgain_ref)

        row = lax.broadcasted_iota(jnp.int32, (L, L), 0)
        col = lax.broadcasted_iota(jnp.int32, (L, L), 1)
        causal = row >= col
        tril = causal.astype(F32)
        triu = (row <= col).astype(F32)
        low = col < SSD_HEAD_DIM
        e64, e128 = _head_expanders()
        lane8 = lax.broadcasted_iota(jnp.int32, (1, HPG), 1)
        sub8 = lax.broadcasted_iota(jnp.int32, (HPG, 1), 0)
        eye8 = (lax.broadcasted_iota(jnp.int32, (HPG, HPG), 0) == lax.broadcasted_iota(jnp.int32, (HPG, HPG), 1)).astype(F32)
        last_row = (lax.broadcasted_iota(jnp.int32, (L, 1), 0) == L - 1).astype(F32)
        for c in reversed(range(CPS)):
            rows = pl.ds(c * L, L)
            xv = x_ref[rows, :]
            bm = b_ref[rows, :]
            cm = c_ref[rows, :]
            zz = z_ref[rows, :]
            dvec = d_ref[...]
            sz = _sigmoid(zz)
            silu_z = zz * sz
            v = yraw_ref[rows, :] + xv * dvec
            u = v * silu_z
            r = lax.rsqrt(jnp.mean(u * u, axis=-1, keepdims=True) + EPS)
            n = u * r
            do = dy_ref[rows, :]
            dgain_ref[...] += jnp.sum(do * n, axis=0, keepdims=True)
            dn = do * gain_ref[...]
            du = r * (dn - n * jnp.mean(dn * n, axis=-1, keepdims=True))
            dz_ref[rows, :] = (du * v * _silu_grad(zz, sz)).astype(dz_ref.dtype)
            dyv = du * silu_z
            dd_ref[...] += _xdot(jnp.sum(dyv * xv, axis=0, keepdims=True), e64, NT)
            pre, dt, a_neg, acum, acum_t = _ssd_chunk_terms(dtr_ref[rows, :], bias_ref[...], alog_ref[...], tril, triu)
            ac = _xdot(acum, e64, NN)
            ac_sq = _xdot(acum, e128, NN)
            dt_w = _xdot(dt, e64, NN)
            xd = xv * dt_w
            ac_last = ac[L - 1:L, :]
            ea = jnp.exp(ac)
            w = jnp.exp(ac_last - ac)
            ea_last = jnp.exp(ac_last)
            sp = st_ref[c]
            ds = dstate[...]
            dye = dyv * ea
            yoff = _bdot(cm, sp, NN) * ea
            bds = _bdot(bm, ds, NN)
            dcm = _bdot(dye, sp, NT)
            dbm = _bdot(xd * w, ds, NT)
            dstate[...] = ds * ea_last + _bdot(cm, dye, TN)
            w8 = jnp.exp(acum[L - 1:L, :] - acum)
            dw8 = _xdot(xd * bds, e64, NT)
            dac8 = _xdot(dyv * yoff, e64, NT) - dw8 * w8
            tail8 = jnp.sum(dw8 * w8, axis=0, keepdims=True) + jnp.exp(acum[L - 1:L, :]) * _xdot(
                jnp.sum(ds * sp, axis=0, keepdims=True), e64, NT)
            dac8 = dac8 + last_row * tail8
            gmat = _bdot(cm, bm, NT)
            dgmat = jnp.zeros((L, L), F32)
            colsum_t = jnp.zeros((HPG, L), F32)
            for q in range(HPG // 2):
                pair = slice(q * 128, (q + 1) * 128)
                xd_tile = xd[:, pair]
                dy_tile = dyv[:, pair]
                dxd_tile = bds[:, pair] * w[:, pair]
                for j, keep in ((2 * q, low), (2 * q + 1, ~low)):
                    lam = jnp.exp(jnp.where(causal, ac_sq[:, j * L:(j + 1) * L] - acum_t[j:j + 1, :], NEG))
                    mh = gmat * lam
                    dyj = jnp.where(keep, dy_tile, 0.0)
                    dxd_tile = dxd_tile + _bdot(mh, dyj, TN)
                    dm = _bdot(dyj, xd_tile, NT)
                    dgmat = dgmat + dm * lam
                    qm = dm * mh
                    dac8 = dac8 + jnp.sum(qm, axis=1, keepdims=True) * (lane8 == j).astype(F32)
                    colsum_t = colsum_t + (sub8 == j).astype(F32) * jnp.sum(qm, axis=0, keepdims=True)
                dxd_ref[:, pair] = dxd_tile
            dac8 = dac8 - _xdot(colsum_t, eye8, TN)
            dxd = dxd_ref[...]
            dx_ref[rows, :] = dxd * dt_w + dyv * dvec
            dc_ref[rows, :] = dcm + _bdot(dgmat, bm, NN)
            db_ref[rows, :] = dbm + _bdot(dgmat, cm, TN)
            da8 = _xdot(triu, dac8, NN)
            ddt8 = _xdot(dxd * xv, e64, NT) + da8 * a_neg
            dalog_ref[...] += jnp.sum(da8 * dt, axis=0, keepdims=True) * a_neg
            dpre = ddt8 * _sigmoid(pre)
            ddtr_ref[rows, :] = dpre
            dbias_ref[...] += jnp.sum(dpre, axis=0, keepdims=True)

    rev = lambda i: NI - 1 - i
    vec8 = pl.BlockSpec((None, 1, HPG), lambda g, i: (g, 0, 0))
    grp = pl.BlockSpec((T, GW), lambda g, i: (rev(i), g))
    bspec = pl.BlockSpec((T, SSD_STATE), lambda g, i: (rev(i), SSD_WIDTH // SSD_STATE + g))
    cspec = pl.BlockSpec((T, SSD_STATE), lambda g, i: (rev(i), SSD_WIDTH // SSD_STATE + SSD_GROUPS + g))
    gvec = pl.BlockSpec((1, GW), lambda g, i: (0, g))
    st_spec = pl.BlockSpec((CPS, None, SSD_STATE, GW), lambda g, i: (rev(i), g, 0, 0))
    small = jax.ShapeDtypeStruct((SSD_GROUPS, 1, HPG), F32)
    return pl.pallas_call(
        body, grid=(SSD_GROUPS, NI),
        in_specs=[grp, grp, grp, bspec, cspec, grp, pl.BlockSpec((None, T, HPG), lambda g, i: (g, rev(i), 0)),
                  vec8, vec8, gvec, gvec, st_spec],
        out_specs=[grp, grp, pl.BlockSpec((T, SSD_STATE), lambda g, i: (rev(i), g)),
                   pl.BlockSpec((T, SSD_STATE), lambda g, i: (rev(i), g)),
                   pl.BlockSpec((None, T, HPG), lambda g, i: (g, rev(i), 0)), vec8, vec8, vec8, gvec],
        out_shape=[jax.ShapeDtypeStruct((S, SSD_WIDTH), BF16), jax.ShapeDtypeStruct((S, SSD_WIDTH), F32),
                   jax.ShapeDtypeStruct((S, SSD_GROUPS * SSD_STATE), F32), jax.ShapeDtypeStruct((S, SSD_GROUPS * SSD_STATE), F32),
                   jax.ShapeDtypeStruct((SSD_GROUPS, S, HPG), F32), small, small, small,
                   jax.ShapeDtypeStruct((1, SSD_WIDTH), F32)],
        scratch_shapes=[pltpu.VMEM((SSD_STATE, GW), F32), pltpu.VMEM((L, GW), F32)],
        name=name, compiler_params=_params("arbitrary", "arbitrary"),
    )(dy, yraw, xbc, xbc, xbc, proj, dtr, dt_bias, a_log, d_rep, gain, states)


def _swap_halves(t):
    w = t.shape[1]
    lane = lax.broadcasted_iota(jnp.int32, t.shape, 1)
    return jnp.where((lane % 64) < 32, pltpu.roll(t, w - 32, axis=1), pltpu.roll(t, 32, axis=1))


def _widen(tab, w):
    return tab if w == 128 else jnp.concatenate([tab] * (w // 128), axis=1)


def _rope(t, cos, sin_signed):
    return t * cos + _swap_halves(t) * sin_signed


def _rope_t(d, cos, sin_signed):
    return d * cos - _swap_halves(d) * sin_signed


def _group_sum64(v, bd):
    hi = v.astype(BF16)
    lo = (v - hi.astype(F32)).astype(BF16)
    return (lax.dot_general(hi, bd, NN, preferred_element_type=F32)
            + lax.dot_general(lo, bd, NN, preferred_element_type=F32))


AQ_BLK = 2560 // ATT_WIDTH


def _att_prep_fwd(proj, qg, kg, cos, sin, bd, name):
    S = proj.shape[0]
    T = min(512, S)

    def body(q_ref, k_ref, v_ref, qg_ref, kg_ref, cos_ref, sin_ref, bd_ref, qo_ref, ko_ref, vo_ref):
        cw = _widen(cos_ref[...], ATT_WIDTH)
        sw = _widen(sin_ref[...], ATT_WIDTH)
        bdv = bd_ref[...]

        def norm_rope(t, gain):
            ss = _group_sum64(t * t, bdv)
            return _rope(t * lax.rsqrt(ss * (1.0 / ATT_HEAD_DIM) + EPS) * gain, cw, sw)

        qo_ref[...] = (norm_rope(q_ref[...], qg_ref[...]) * (ATT_HEAD_DIM ** -0.5)).astype(BF16)
        kt = norm_rope(k_ref[...], kg_ref[...]).astype(BF16)
        vt = v_ref[...].astype(BF16)
        for h in range(ATT_HEADS):
            ko_ref[h] = kt[:, h * 64:(h + 1) * 64]
            vo_ref[h] = vt[:, h * 64:(h + 1) * 64]

    vec = pl.BlockSpec((1, ATT_WIDTH), lambda i: (0, 0))
    tab = pl.BlockSpec((T, 128), lambda i: (i, 0))
    hm = pl.BlockSpec((ATT_HEADS, T, ATT_HEAD_DIM), lambda i: (0, i, 0))
    return pl.pallas_call(
        body, grid=(S // T,),
        in_specs=[pl.BlockSpec((T, ATT_WIDTH), lambda i: (i, AQ_BLK)), pl.BlockSpec((T, ATT_WIDTH), lambda i: (i, AQ_BLK + 1)),
                  pl.BlockSpec((T, ATT_WIDTH), lambda i: (i, AQ_BLK + 2)), vec, vec, tab, tab,
                  pl.BlockSpec((ATT_WIDTH, ATT_WIDTH), lambda i: (0, 0))],
        out_specs=[pl.BlockSpec((T, ATT_WIDTH), lambda i: (i, 0)), hm, hm],
        out_shape=[jax.ShapeDtypeStruct((S, ATT_WIDTH), BF16), jax.ShapeDtypeStruct((ATT_HEADS, S, ATT_HEAD_DIM), BF16),
                   jax.ShapeDtypeStruct((ATT_HEADS, S, ATT_HEAD_DIM), BF16)],
        name=name, compiler_params=_params("parallel"),
    )(proj, proj, proj, qg, kg, cos, sin, bd)


def _att_prep_bwd(proj, dq, dk, qg, kg, cos, sin, bd, name):
    S = proj.shape[0]
    T = min(512, S)
    NI = S // T

    def body(q_ref, k_ref, dq_ref, dk_ref, qg_ref, kg_ref, cos_ref, sin_ref, bd_ref, dqo_ref, dko_ref, dqg_ref, dkg_ref, acc_ref):
        i = pl.program_id(0)

        @pl.when(i == 0)
        def _():
            acc_ref[...] = jnp.zeros_like(acc_ref)

        cw = _widen(cos_ref[...], ATT_WIDTH)
        sw = _widen(sin_ref[...], ATT_WIDTH)
        bdv = bd_ref[...]

        def one(t, d_rot, gain, scale, slot):
            ss = _group_sum64(t * t, bdv)
            r = lax.rsqrt(ss * (1.0 / ATT_HEAD_DIM) + EPS)
            n = t * r
            d_ng = _rope_t(d_rot * scale, cw, sw)
            acc_ref[pl.ds(slot, 1), :] += jnp.sum(d_ng * n, axis=0, keepdims=True)
            dn = d_ng * gain
            return r * (dn - n * (_group_sum64(dn * n, bdv) * (1.0 / ATT_HEAD_DIM)))

        dqo_ref[...] = one(q_ref[...], dq_ref[...], qg_ref[...], ATT_HEAD_DIM ** -0.5, 0).astype(BF16)
        dko_ref[...] = one(k_ref[...], dk_ref[...], kg_ref[...], 1.0, 1).astype(BF16)

        @pl.when(i == NI - 1)
        def _():
            a = acc_ref[...]
            f = a[:, 0:64]
            for h in range(1, ATT_HEADS):
                f = f + a[:, h * 64:(h + 1) * 64]
            dqg_ref[...] = f[0:1]
            dkg_ref[...] = f[1:2]

    vec = pl.BlockSpec((1, ATT_WIDTH), lambda i: (0, 0))
    tab = pl.BlockSpec((T, 128), lambda i: (i, 0))
    row = pl.BlockSpec((T, ATT_WIDTH), lambda i: (i, 0))
    g64 = pl.BlockSpec((1, ATT_HEAD_DIM), lambda i: (0, 0))
    return pl.pallas_call(
        body, grid=(NI,),
        in_specs=[pl.BlockSpec((T, ATT_WIDTH), lambda i: (i, AQ_BLK)), pl.BlockSpec((T, ATT_WIDTH), lambda i: (i, AQ_BLK + 1)),
                  row, row, vec, vec, tab, tab, pl.BlockSpec((ATT_WIDTH, ATT_WIDTH), lambda i: (0, 0))],
        out_specs=[row, row, g64, g64],
        out_shape=[jax.ShapeDtypeStruct((S, ATT_WIDTH), BF16), jax.ShapeDtypeStruct((S, ATT_WIDTH), BF16),
                   jax.ShapeDtypeStruct((1, ATT_HEAD_DIM), F32), jax.ShapeDtypeStruct((1, ATT_HEAD_DIM), F32)],
        scratch_shapes=[pltpu.VMEM((8, ATT_WIDTH), F32)], name=name, compiler_params=_params("arbitrary"),
    )(proj, proj, dq, dk, qg, kg, cos, sin, bd)


def _att_bias():
    qpos = np.arange(CHUNK)[:, None] + ATT_SPAN
    kpos = np.arange(ATT_STRIP)[None, :]
    rel = qpos - kpos
    mult = np.zeros((CHUNK, ATT_STRIP), np.float64)
    for window, dil in DILATED_PAIRS:
        mult += (rel >= 0) & (rel % dil == 0) & (rel // dil <= window // dil)
    return np.where(mult > 0, np.log(np.maximum(mult, 1.0)), NEG).astype(np.float32)


def _att_scores(q, ks, bias, i):
    s = _bdot(q, ks, NT) + bias
    kcol = lax.broadcasted_iota(jnp.int32, (1, ATT_STRIP), 1) + i * CHUNK
    return jnp.where(kcol >= ATT_SPAN, s, NEG)


def _att_fwd(q, kp, vp, bias, name):
    S = q.shape[0]
    SP = kp.shape[1]

    def body(q_ref, k_ref, v_ref, bias_ref, o_ref):
        i = pl.program_id(1)
        start = pl.multiple_of(i * CHUNK, CHUNK)
        qv = q_ref[...]
        for hh in range(2):
            s = _att_scores(qv[:, hh * 64:(hh + 1) * 64], k_ref[hh, pl.ds(start, ATT_STRIP), :], bias_ref[...], i)
            m = jnp.max(s, axis=-1, keepdims=True)
            p = jnp.exp(s - m)
            den = jnp.sum(p, axis=-1, keepdims=True)
            o = _bdot(p, v_ref[hh, pl.ds(start, ATT_STRIP), :], NN) / den
            o_ref[:, hh * 64:(hh + 1) * 64] = o.astype(o_ref.dtype)

    kv = pl.BlockSpec((2, SP, ATT_HEAD_DIM), lambda hp, i: (hp, 0, 0))
    return pl.pallas_call(
        body, grid=(ATT_HEADS // 2, S // CHUNK),
        in_specs=[pl.BlockSpec((CHUNK, 128), lambda hp, i: (i, hp)), kv, kv,
                  pl.BlockSpec((CHUNK, ATT_STRIP), lambda hp, i: (0, 0))],
        out_specs=pl.BlockSpec((CHUNK, 128), lambda hp, i: (i, hp)),
        out_shape=jax.ShapeDtypeStruct((S, ATT_WIDTH), BF16), name=name, compiler_params=_params("parallel", "arbitrary"),
    )(q, kp, vp, bias)


def _att_bwd(q, kp, vp, bias, dy, name):
    S = q.shape[0]
    SP = kp.shape[1]

    def body(q_ref, k_ref, v_ref, bias_ref, do_ref, dq_ref, dk_ref, dv_ref):
        h = pl.program_id(0)
        i = pl.program_id(1)

        @pl.when(i == 0)
        def _():
            dk_ref[...] = jnp.zeros_like(dk_ref)
            dv_ref[...] = jnp.zeros_like(dv_ref)

        start = pl.multiple_of(i * CHUNK, CHUNK)
        strip = pl.ds(start, ATT_STRIP)
        odd = (h % 2) == 1
        qv = q_ref[...]
        dov = do_ref[...]
        qh = jnp.where(odd, qv[:, 64:128], qv[:, 0:64])
        doh = jnp.where(odd, dov[:, 64:128], dov[:, 0:64])
        ks = k_ref[strip, :]
        vs = v_ref[strip, :]
        s = _att_scores(qh, ks, bias_ref[...], i)
        m = jnp.max(s, axis=-1, keepdims=True)
        p = jnp.exp(s - m)
        p = p / jnp.sum(p, axis=-1, keepdims=True)
        dp = _bdot(doh, vs, NT)
        dsc = p * (dp - jnp.sum(p * dp, axis=-1, keepdims=True))
        dq_ref[...] = _bdot(dsc, ks, NN)
        dv_ref[strip, :] += _bdot(p, doh, TN)
        dk_ref[strip, :] += _bdot(dsc, qh, TN)

    kv = pl.BlockSpec((None, SP, ATT_HEAD_DIM), lambda h, i: (h, 0, 0))
    return pl.pallas_call(
        body, grid=(ATT_HEADS, S // CHUNK),
        in_specs=[pl.BlockSpec((CHUNK, 128), lambda h, i: (i, h // 2)), kv, kv,
                  pl.BlockSpec((CHUNK, ATT_STRIP), lambda h, i: (0, 0)),
                  pl.BlockSpec((CHUNK, 128), lambda h, i: (i, SSD_WIDTH // 128 + h // 2))],
        out_specs=[pl.BlockSpec((None, CHUNK, ATT_HEAD_DIM), lambda h, i: (h, i, 0)), kv, kv],
        out_shape=[jax.ShapeDtypeStruct((ATT_HEADS, S, ATT_HEAD_DIM), F32), jax.ShapeDtypeStruct((ATT_HEADS, SP, ATT_HEAD_DIM), F32),
                   jax.ShapeDtypeStruct((ATT_HEADS, SP, ATT_HEAD_DIM), F32)],
        name=name, compiler_params=_params("parallel", "arbitrary"),
    )(q, kp, vp, bias, dy)


RQ_BLK = 4096 // RET_QK_WIDTH
RV_BLK = 4608 // RET_V_WIDTH
RET_LOG_GAMMA = tuple(math.log1p(-2.0 ** (-5.0 - h)) for h in range(RET_HEADS))


def _ret_decays(h):
    L = CHUNK
    lg = RET_LOG_GAMMA[h]
    row = lax.broadcasted_iota(jnp.int32, (L, L), 0)
    col = lax.broadcasted_iota(jnp.int32, (L, L), 1)
    rel = (row - col).astype(F32)
    dm = jnp.where(rel >= 0, jnp.exp(jnp.maximum(rel, 0.0) * lg), 0.0)
    idx = lax.broadcasted_iota(jnp.int32, (L, 1), 0).astype(F32)
    kte = jnp.exp((L - 1 - idx) * lg)
    qfs = jnp.exp((idx + 1.0) * lg)
    return dm, kte, qfs, math.exp(L * lg)


def _ret_fwd(proj, cos, sin, gain, name):
    S = proj.shape[0]
    L = CHUNK
    T = min(512, S)
    CPS = T // L
    NC = S // L

    def body(q_ref, k_ref, v_ref, g_ref, cos_ref, sin_ref, gain_ref, y_ref, o_ref, st_ref, state):
        i = pl.program_id(0)

        @pl.when(i == 0)
        def _():
            state[...] = jnp.zeros_like(state)

        dec = [_ret_decays(h) for h in range(RET_HEADS)]
        for c in range(CPS):
            rows = pl.ds(c * L, L)
            cw = _widen(cos_ref[rows, :], RET_QK_WIDTH)
            sw = _widen(sin_ref[rows, :], RET_QK_WIDTH)
            qv = _rope(q_ref[rows, :], cw, sw)
            kv = _rope(k_ref[rows, :], cw, sw) * (RET_QK_DIM ** -0.5)
            for h in range(RET_HEADS):
                dm, kte, qfs, cd = dec[h]
                qh = qv[:, h * 64:(h + 1) * 64]
                kh = kv[:, h * 64:(h + 1) * 64]
                vs = slice(h * RET_V_DIM, (h + 1) * RET_V_DIM)
                vh = v_ref[rows, vs]
                sp = state[h]
                st_ref[c, h] = sp
                o = _bdot(_bdot(qh, kh, NT) * dm, vh, NN) + _bdot(qh * qfs, sp, NN)
                state[h] = cd * sp + _bdot(kh * kte, vh, TN)
                o_ref[rows, vs] = o
                gh = g_ref[rows, vs]
                r = lax.rsqrt(jnp.mean(o * o, axis=-1, keepdims=True) + EPS)
                y_ref[rows, vs] = (o * r * gain_ref[:, vs] * (gh * _sigmoid(gh))).astype(y_ref.dtype)

    tab = pl.BlockSpec((T, 128), lambda i: (i, 0))
    wide = pl.BlockSpec((T, RET_V_WIDTH), lambda i: (i, 0))
    return pl.pallas_call(
        body, grid=(S // T,),
        in_specs=[pl.BlockSpec((T, RET_QK_WIDTH), lambda i: (i, RQ_BLK)), pl.BlockSpec((T, RET_QK_WIDTH), lambda i: (i, RQ_BLK + 1)),
                  pl.BlockSpec((T, RET_V_WIDTH), lambda i: (i, RV_BLK)), pl.BlockSpec((T, RET_V_WIDTH), lambda i: (i, RV_BLK + 1)),
                  tab, tab, pl.BlockSpec((1, RET_V_WIDTH), lambda i: (0, 0))],
        out_specs=[wide, wide, pl.BlockSpec((CPS, RET_HEADS, RET_QK_DIM, RET_V_DIM), lambda i: (i, 0, 0, 0))],
        out_shape=[jax.ShapeDtypeStruct((S, RET_V_WIDTH), BF16), jax.ShapeDtypeStruct((S, RET_V_WIDTH), F32),
                   jax.ShapeDtypeStruct((NC, RET_HEADS, RET_QK_DIM, RET_V_DIM), F32)],
        scratch_shapes=[pltpu.VMEM((RET_HEADS, RET_QK_DIM, RET_V_DIM), F32)], name=name,
        compiler_params=_params("arbitrary"),
    )(proj, proj, proj, proj, cos, sin, gain)


def _ret_bwd(dy, oraw, proj, cos, sin, gain, states, name):
    S = proj.shape[0]
    L = CHUNK
    T = min(512, S)
    CPS = T // L
    NI = S // T

    def body(dy_ref, o_ref, q_ref, k_ref, v_ref, g_ref, cos_ref, sin_ref, gain_ref, st_ref,
             dq_ref, dk_ref, dv_ref, dg_ref, dgain_ref, dstate, dqs, dks):
        i = pl.program_id(0)

        @pl.when(i == 0)
        def _():
            dstate[...] = jnp.zeros_like(dstate)
            dgain_ref[...] = jnp.zeros_like(dgain_ref)

        dec = [_ret_decays(h) for h in range(RET_HEADS)]
        for c in reversed(range(CPS)):
            rows = pl.ds(c * L, L)
            cw = _widen(cos_ref[rows, :], RET_QK_WIDTH)
            sw = _widen(sin_ref[rows, :], RET_QK_WIDTH)
            qv = _rope(q_ref[rows, :], cw, sw)
            kv = _rope(k_ref[rows, :], cw, sw) * (RET_QK_DIM ** -0.5)
            for h in range(RET_HEADS):
                dm, kte, qfs, cd = dec[h]
                qs = slice(h * 64, (h + 1) * 64)
                vs = slice(h * RET_V_DIM, (h + 1) * RET_V_DIM)
                qh = qv[:, qs]
                kh = kv[:, qs]
                vh = v_ref[rows, vs]
                gh = g_ref[rows, vs]
                gn = gain_ref[:, vs]
                o = o_ref[rows, vs]
                dyh = dy_ref[rows, vs]
                sg = _sigmoid(gh)
                silu_g = gh * sg
                r = lax.rsqrt(jnp.mean(o * o, axis=-1, keepdims=True) + EPS)
                n = o * r
                dgain_ref[:, vs] += jnp.sum(dyh * n * silu_g, axis=0, keepdims=True)
                dg_ref[rows, vs] = (dyh * n * gn * _silu_grad(gh, sg)).astype(dg_ref.dtype)
                dn = dyh * gn * silu_g
                do = r * (dn - n * jnp.mean(dn * n, axis=-1, keepdims=True))
                sp = st_ref[c, h]
                ds = dstate[h]
                sc = _bdot(qh, kh, NT) * dm
                dsc = _bdot(do, vh, NT) * dm
                dv_ref[rows, vs] = (_bdot(sc, do, TN) + _bdot(kh * kte, ds, NN)).astype(dv_ref.dtype)
                dqs[:, qs] = _bdot(dsc, kh, NN) + _bdot(do, sp, NT) * qfs
                dks[:, qs] = _bdot(dsc, qh, TN) + _bdot(vh, ds, NT) * kte
                dstate[h] = cd * ds + _bdot(qh * qfs, do, TN)
            dq_ref[rows, :] = _rope_t(dqs[...], cw, sw).astype(dq_ref.dtype)
            dk_ref[rows, :] = _rope_t(dks[...] * (RET_QK_DIM ** -0.5), cw, sw).astype(dk_ref.dtype)

    rev = lambda i: NI - 1 - i
    tab = pl.BlockSpec((T, 128), lambda i: (rev(i), 0))
    wide = pl.BlockSpec((T, RET_V_WIDTH), lambda i: (rev(i), 0))
    narrow = pl.BlockSpec((T, RET_QK_WIDTH), lambda i: (rev(i), 0))
    gvec = pl.BlockSpec((1, RET_V_WIDTH), lambda i: (0, 0))
    return pl.pallas_call(
        body, grid=(NI,),
        in_specs=[pl.BlockSpec((T, RET_V_WIDTH), lambda i: (rev(i), (SSD_WIDTH + ATT_WIDTH) // RET_V_WIDTH)), wide,
                  pl.BlockSpec((T, RET_QK_WIDTH), lambda i: (rev(i), RQ_BLK)), pl.BlockSpec((T, RET_QK_WIDTH), lambda i: (rev(i), RQ_BLK + 1)),
                  pl.BlockSpec((T, RET_V_WIDTH), lambda i: (rev(i), RV_BLK)), pl.BlockSpec((T, RET_V_WIDTH), lambda i: (rev(i), RV_BLK + 1)),
                  tab, tab, gvec,
                  pl.BlockSpec((CPS, RET_HEADS, RET_QK_DIM, RET_V_DIM), lambda i: (rev(i), 0, 0, 0))],
        out_specs=[narrow, narrow, wide, wide, gvec],
        out_shape=[jax.ShapeDtypeStruct((S, RET_QK_WIDTH), BF16), jax.ShapeDtypeStruct((S, RET_QK_WIDTH), BF16),
                   jax.ShapeDtypeStruct((S, RET_V_WIDTH), BF16), jax.ShapeDtypeStruct((S, RET_V_WIDTH), BF16),
                   jax.ShapeDtypeStruct((1, RET_V_WIDTH), F32)],
        scratch_shapes=[pltpu.VMEM((RET_HEADS, RET_QK_DIM, RET_V_DIM), F32), pltpu.VMEM((L, RET_QK_WIDTH), F32),
                        pltpu.VMEM((L, RET_QK_WIDTH), F32)],
        name=name, compiler_params=_params("arbitrary"),
    )(dy, oraw, proj, proj, proj, proj, cos, sin, gain, states)


def _adamw_update(g_ref, nb, w_ref, m_ref, v_ref, go_ref, d_ref, mo_ref, vo_ref):
    g = g_ref[0].astype(F32)
    for k in range(1, nb):
        g = g + g_ref[k].astype(F32)
    mn = ADAM_B1 * m_ref[...] + (1.0 - ADAM_B1) * g
    vn = ADAM_B2 * v_ref[...] + (1.0 - ADAM_B2) * (g * g)
    go_ref[...] = g
    mo_ref[...] = mn
    vo_ref[...] = vn
    c1 = 1.0 - ADAM_B1 ** ADAM_STEP
    c2 = 1.0 - ADAM_B2 ** ADAM_STEP
    d_ref[...] = -ADAM_LR * ((mn / c1) / (jnp.sqrt(vn / c2) + ADAM_EPS) + ADAM_WD * w_ref[...])


def _adamw_rows(R, C):
    return _pick(R, tuple(t for t in (512, 256, 128, 64, 32, 16, 8) if t * C <= 256 * 1024))


def _adamw(gblocks, w, m, v, name):
    nb, R, C = gblocks.shape
    tr = _adamw_rows(R, C)

    def body(g_ref, *refs):
        _adamw_update(g_ref, nb, *refs)

    row = pl.BlockSpec((tr, C), lambda i: (i, 0))
    sh = jax.ShapeDtypeStruct((R, C), F32)
    return pl.pallas_call(
        body, grid=(R // tr,), in_specs=[pl.BlockSpec((nb, tr, C), lambda i: (0, i, 0)), row, row, row],
        out_specs=[row, row, row, row], out_shape=[sh, sh, sh, sh], name=name, compiler_params=_params("parallel"),
    )(gblocks, w, m, v)


def _adamw_layers(g0, g1, w, m, v, name):
    nb, R, C = g0.shape
    tr = _adamw_rows(R, C)

    def body(g0_ref, g1_ref, *refs):
        l = pl.program_id(0)

        @pl.when(l == 0)
        def _():
            _adamw_update(g0_ref, nb, *refs)

        @pl.when(l == 1)
        def _():
            _adamw_update(g1_ref, nb, *refs)

    row = pl.BlockSpec((None, tr, C), lambda l, i: (l, i, 0))
    sh = jax.ShapeDtypeStruct((DEPTH, R, C), F32)
    return pl.pallas_call(
        body, grid=(DEPTH, R // tr),
        in_specs=[pl.BlockSpec((nb, tr, C), lambda l, i: (0, i * (1 - l), 0)), pl.BlockSpec((nb, tr, C), lambda l, i: (0, i * l, 0)),
                  row, row, row],
        out_specs=[row, row, row, row], out_shape=[sh, sh, sh, sh], name=name, compiler_params=_params("arbitrary", "arbitrary"),
    )(g0, g1, w, m, v)


def _peers():
    x, y, c = lax.axis_index("x"), lax.axis_index("y"), lax.axis_index("c")
    flips = ((0, 0, 1), (1, 0, 0), (0, 1, 0), (1, 1, 0), (1, 0, 1), (0, 1, 1), (1, 1, 1))
    me = 4 * x + 2 * y + c
    peers = [(x ^ fx, y ^ fy, c ^ fc) for fx, fy, fc in flips]
    return me, peers


def _exchange(arrs, scatter, name):
    n = len(arrs)
    npeer = N_DEV - 1

    def body(*refs):
        ins, outs = refs[:n], refs[n:2 * n]
        send_sems, recv_sems, local_sems = refs[2 * n:]
        me, peers = _peers()
        copies = []
        for a in range(n):
            src_own = ins[a].at[me] if scatter else ins[a]
            own = pltpu.make_async_copy(src_own, outs[a].at[me], local_sems.at[a])
            own.start()
            copies.append(own)
            for k, peer in enumerate(peers):
                src = ins[a].at[4 * peer[0] + 2 * peer[1] + peer[2]] if scatter else ins[a]
                cp = pltpu.make_async_remote_copy(
                    src_ref=src, dst_ref=outs[a].at[me], send_sem=send_sems.at[a * npeer + k],
                    recv_sem=recv_sems.at[a * npeer + k], device_id=peer, device_id_type=pl.DeviceIdType.MESH)
                cp.start()
                copies.append(cp)
        for cp in copies:
            cp.wait()

    out_shape = [jax.ShapeDtypeStruct(((N_DEV,) + a.shape[1:]) if scatter else ((N_DEV,) + a.shape), a.dtype) for a in arrs]
    anyspec = pl.BlockSpec(memory_space=pl.ANY)
    return pl.pallas_call(
        body, in_specs=[anyspec] * n, out_specs=[anyspec] * n, out_shape=out_shape,
        scratch_shapes=[pltpu.SemaphoreType.DMA((n * npeer,)), pltpu.SemaphoreType.DMA((n * npeer,)),
                        pltpu.SemaphoreType.DMA((n,))],
        name=name,
    )(*arrs)


def _dev_index(peer):
    return 4 * peer[0] + 2 * peer[1] + peer[2]


def _push_copies(src_refs, land_refs, send_sems, recv_sems, scatter, as_receiver):
    me, peers = _peers()
    npeer = N_DEV - 1
    copies = []
    for a in range(len(src_refs)):
        for k, peer in enumerate(peers):
            src = src_refs[a].at[_dev_index(peer)] if scatter else src_refs[a]
            slot = _dev_index(peer) if as_receiver else me
            copies.append(pltpu.make_async_remote_copy(
                src_ref=src, dst_ref=land_refs[a].at[slot], send_sem=send_sems.at[a * npeer + k],
                recv_sem=recv_sems.at[a * npeer + k], device_id=peer, device_id_type=pl.DeviceIdType.MESH))
    return copies


def _push_start(srcs, lands, scatter, name):
    n = len(srcs)
    nsem = n * (N_DEV - 1)

    def body(*refs):
        for cp in _push_copies(refs[:n], refs[n:2 * n], refs[2 * n], refs[2 * n + 1], scatter, False):
            cp.start()
        token = refs[-1]
        token[...] = jnp.zeros_like(token)

    hbm = pl.BlockSpec(memory_space=pltpu.HBM)
    sem = pl.BlockSpec(memory_space=pltpu.SEMAPHORE)
    arrs = list(srcs) + list(lands)
    return pl.pallas_call(
        body, name=name,
        out_shape=(pltpu.SemaphoreType.DMA((nsem,)), pltpu.SemaphoreType.DMA((nsem,)),
                   *[pltpu.HBM(a.shape, a.dtype) for a in arrs], jax.ShapeDtypeStruct((8, 128), F32)),
        in_specs=[hbm] * (2 * n), out_specs=(sem, sem, *([hbm] * (2 * n)), pl.BlockSpec(memory_space=pltpu.VMEM)),
        input_output_aliases={i: 2 + i for i in range(2 * n)},
        compiler_params=pltpu.CompilerParams(has_side_effects=pltpu.SideEffectType.DATAFLOW_SIDE_EFFECTING),
    )(*[pltpu.with_memory_space_constraint(a, pltpu.HBM) for a in arrs])


def _push_wait(handle, after, scatter, name):
    send_sems, recv_sems, *thru, _ = handle
    n = len(thru) // 2

    def body(*refs):
        for cp in _push_copies(refs[:n], refs[n:2 * n], refs[2 * n], refs[2 * n + 1], scatter, True):
            cp.wait_send()
            cp.wait_recv()

    hbm = pl.BlockSpec(memory_space=pltpu.HBM)
    sem = pl.BlockSpec(memory_space=pltpu.SEMAPHORE)
    outs = pl.pallas_call(
        body, name=name, out_shape=tuple(pltpu.HBM(a.shape, a.dtype) for a in thru),
        in_specs=[hbm] * (2 * n) + [sem, sem, pl.BlockSpec(memory_space=pl.ANY)], out_specs=tuple([hbm] * (2 * n)),
        input_output_aliases={i: i for i in range(2 * n)},
        compiler_params=pltpu.CompilerParams(has_side_effects=pltpu.SideEffectType.DATAFLOW_SIDE_EFFECTING),
    )(*thru, send_sems, recv_sems, after)
    return list(outs[n:])


def _landing(own, me):
    return lax.dynamic_update_index_in_dim(lax.empty((N_DEV,) + own.shape, own.dtype), own, me, 0)


def _tables(S):
    pos = jnp.arange(S, dtype=F32)
    inv = ROPE_THETA ** (-jnp.arange(0, ATT_HEAD_DIM, 2, dtype=F32) / ATT_HEAD_DIM)
    ang = pos[:, None] * inv[None, :]
    cos, sin = jnp.cos(ang), jnp.sin(ang)
    cos128 = jnp.tile(cos, (1, 4))
    sin128 = jnp.tile(jnp.concatenate([-sin, sin], axis=1), (1, 2))
    lane = np.arange(ATT_WIDTH)
    bd = jnp.asarray((lane[:, None] // 64 == lane[None, :] // 64).astype(np.float32), dtype=BF16)
    return cos128, sin128, bd, jnp.asarray(_att_bias())


def _permute_in(w):
    pad = jnp.zeros((w.shape[0], IN_PAD - IN_WIDTH), w.dtype)
    return jnp.concatenate([w[:, :2560], w[:, 2576:], w[:, 2560:2576], pad], axis=1)


def _unpermute_in(g):
    return jnp.concatenate([g[:, :2560], g[:, DT_COL:DT_COL + SSD_HEADS], g[:, 2560:DT_COL]], axis=1)


def _layer_fwd(l, x, p, tabs, late=None):
    cos, sin, bd, bias = tabs
    S = x.shape[0]
    row = lambda v: v.reshape(1, -1)
    hn = _rmsnorm_fwd(x, row(p["ln_mix"]), f"norm_mix_fwd{l}")
    proj = _mm(hn, p["w_in"], "nn", f"in_proj{l}", tn=1152)
    xbc = _conv_fwd(proj, p["conv_w"], row(p["conv_b"]), f"conv_fwd{l}")
    dtr = proj[:, DT_COL:DT_COL + SSD_HEADS].reshape(S, SSD_GROUPS, HPG).transpose(1, 0, 2)
    grp = lambda v: v.reshape(SSD_GROUPS, 1, HPG)
    d_rep = row(jnp.repeat(p["d_skip"], SSD_HEAD_DIM))
    y_ssd, yraw, ssd_st = _ssd_fwd(xbc, proj, dtr, grp(p["dt_bias"]), grp(p["a_log"]), d_rep, row(p["ssd_norm"]), f"ssd_fwd{l}")
    qg = row(jnp.tile(p["q_norm"], ATT_HEADS))
    kg = row(jnp.tile(p["k_norm"], ATT_HEADS))
    aq, ak, av = _att_prep_fwd(proj, qg, kg, cos, sin, bd, f"att_prep_fwd{l}")
    padk = lambda t: jnp.pad(t, ((0, 0), (ATT_SPAN, 0), (0, 0)))
    akp, avp = padk(ak), padk(av)
    y_att = _att_fwd(aq, akp, avp, bias, f"att_fwd{l}")
    y_ret, oraw, ret_st = _ret_fwd(proj, cos, sin, row(p["ret_norm"]), f"ret_fwd{l}")
    y = jnp.concatenate([y_ssd, y_att, y_ret], axis=1)
    if late is not None:
        p.update(late(y))
    x1 = _mm(y, p["w_out"], "nn", f"out_proj{l}", residual=x)
    hn2 = _rmsnorm_fwd(x1, row(p["ln_ffn"]), f"norm_ffn_fwd{l}")
    g, u, act = _swiglu_fwd(hn2, p["w_gate"], p["w_up"], f"swiglu_fwd{l}")
    x2 = _mm(act, p["w_down"], "nn", f"down_proj{l}", residual=x1, tk=1408)
    saved = dict(x=x, hn=hn, proj=proj, xbc=xbc, dtr=dtr, yraw=yraw, ssd_st=ssd_st, aq=aq, akp=akp, avp=avp,
                 oraw=oraw, ret_st=ret_st, y=y, x1=x1, hn2=hn2, g=g, u=u, act=act, d_rep=d_rep, qg=qg, kg=kg)
    return x2, saved


def _layer_bwd(l, dx2, p, sv, tabs, on_ffn=None, on_all=None):
    cos, sin, bd, bias = tabs
    S = dx2.shape[0]
    row = lambda v: v.reshape(1, -1)
    grp = lambda v: v.reshape(SSD_GROUPS, 1, HPG)
    gr = {}
    dg, du = _swiglu_bwd(dx2, p["w_down"], sv["g"], sv["u"], f"swiglu_bwd{l}")
    gr["w_down"] = _mm(sv["act"], dx2, "tn", f"down_wgrad{l}", tm=1408, tn=1024, tk=1024)
    dhn2 = _mm_nt2(dg, p["w_gate"], du, p["w_up"], f"ffn_dgrad{l}")
    gr["w_gate"] = _mm(sv["hn2"], dg, "tn", f"gate_wgrad{l}", tm=1024, tn=1408, tk=1024)
    gr["w_up"] = _mm(sv["hn2"], du, "tn", f"up_wgrad{l}", tm=1024, tn=1408, tk=1024)
    ffn_gain = row(p["ln_ffn"]) + (on_ffn(gr)[0, 0] if on_ffn is not None else 0.0)
    dx1, dln_ffn = _rmsnorm_bwd(sv["x1"], dhn2, ffn_gain, dx2, f"norm_ffn_bwd{l}")
    gr["ln_ffn"] = dln_ffn[0]
    dy = _mm(dx1, p["w_out"], "nt", f"out_dgrad{l}")
    gr["w_out"] = _mm(sv["y"], dx1, "tn", f"out_wgrad{l}", tm=1024, tn=1024, tk=1024)
    dz, dxs, dbm, dcm, ddtr, dbias, dalog, dd, dssd_gain = _ssd_bwd(
        dy, sv["yraw"], sv["xbc"], sv["proj"], sv["dtr"], grp(p["dt_bias"]), grp(p["a_log"]), sv["d_rep"],
        row(p["ssd_norm"]), sv["ssd_st"], f"ssd_bwd{l}")
    gr["dt_bias"], gr["a_log"], gr["d_skip"] = dbias.reshape(-1), dalog.reshape(-1), dd.reshape(-1)
    gr["ssd_norm"] = dssd_gain[0]
    dxbc_act = jnp.concatenate([dxs, dbm, dcm], axis=1)
    dxbc, dconv_w, dconv_b = _conv_bwd(dxbc_act, sv["proj"], p["conv_w"], row(p["conv_b"]), f"conv_bwd{l}")
    gr["conv_w"], gr["conv_b"] = dconv_w, dconv_b[0]
    dq_h, dk_h, dv_h = _att_bwd(sv["aq"], sv["akp"], sv["avp"], bias, dy, f"att_bwd{l}")
    flat = lambda t: t.transpose(1, 0, 2).reshape(S, ATT_WIDTH)
    daq, dak, dqg, dkg = _att_prep_bwd(sv["proj"], flat(dq_h), flat(dk_h[:, ATT_SPAN:]), sv["qg"], sv["kg"], cos, sin, bd,
                                       f"att_prep_bwd{l}")
    dav = flat(dv_h[:, ATT_SPAN:]).astype(BF16)
    gr["q_norm"], gr["k_norm"] = dqg[0], dkg[0]
    drq, drk, drv, drg, dret_gain = _ret_bwd(dy, sv["oraw"], sv["proj"], cos, sin, row(p["ret_norm"]), sv["ret_st"], f"ret_bwd{l}")
    gr["ret_norm"] = dret_gain[0]
    ddt_cols = ddtr.transpose(1, 0, 2).reshape(S, SSD_HEADS).astype(BF16)
    dproj = jnp.concatenate([dz, dxbc, daq, dak, dav, drq, drk, drv, drg, ddt_cols,
                             jnp.zeros((S, IN_PAD - IN_WIDTH), BF16)], axis=1)
    gr["w_in"] = _mm(sv["hn"], dproj, "tn", f"in_wgrad{l}", tm=1024, tn=1152, tk=1024)
    launched = on_all(gr) if on_all is not None else None
    dhn = _mm(dproj, p["w_in"], "nt", f"in_dgrad{l}", tk=1920, after=launched)
    dx0, dln_mix = _rmsnorm_bwd(sv["x"], dhn, row(p["ln_mix"]), dx1, f"norm_mix_bwd{l}")
    gr["ln_mix"] = dln_mix[0]
    return dx0, gr


def _local_step(x, tgt, layers, late=None, on_ffn=None, on_all=None):
    n = len(layers)
    none = [None] * n
    late, on_ffn, on_all = late or none, on_ffn or none, on_all or none
    tabs = _tables(x.shape[0])
    saved, params = [], []
    h = x
    for l in range(n):
        p = dict(layers[l](h) if callable(layers[l]) else layers[l])
        h, sv = _layer_fwd(l, h, p, tabs, late[l])
        saved.append(sv)
        params.append(p)
    dh, lacc = _loss_grad(h, tgt, "loss_grad")
    grads = [None] * n
    for l in reversed(range(n)):
        dh, grads[l] = _layer_bwd(l, dh, params[l], saved[l], tabs, on_ffn[l], on_all[l])
    return lacc[0, 0], dh, grads


BIG = ("w_in", "w_out", "w_gate", "w_up", "w_down")
SMALL = ("ln_mix", "conv_b", "dt_bias", "a_log", "d_skip", "ssd_norm", "q_norm", "k_norm", "ret_norm", "ln_ffn")
ORDER = ("ln_mix", "w_in", "conv_w", "conv_b", "dt_bias", "a_log", "d_skip", "ssd_norm", "q_norm", "k_norm", "ret_norm",
         "w_out", "ln_ffn", "w_gate", "w_up", "w_down")


COL_SHARDED = ("w_in", "w_gate", "w_up", "conv_w")


def _full_weight(k, gathered):
    if k in COL_SHARDED:
        full = gathered.transpose(1, 0, 2).reshape(gathered.shape[1], -1)
        return _permute_in(full) if k == "w_in" else full
    return gathered.reshape(-1, gathered.shape[2])


def _shard_block(k, g):
    if k == "w_in":
        g = _unpermute_in(g)
    if k in COL_SHARDED:
        return g.reshape(g.shape[0], N_DEV, -1).transpose(1, 0, 2).astype(BF16)
    return g.reshape(N_DEV, -1, g.shape[1]).astype(BF16)


def kernel(x, ln_mix, w_in, conv_w, conv_b, dt_bias, a_log, d_skip, ssd_norm, q_norm, k_norm, ret_norm, w_out, ln_ffn, w_gate, w_up, w_down, loss_target, m_ln_mix, m_w_in, m_conv_w, m_conv_b, m_dt_bias, m_a_log, m_d_skip, m_ssd_norm, m_q_norm, m_k_norm, m_ret_norm, m_w_out, m_ln_ffn, m_w_gate, m_w_up, m_w_down, v_ln_mix, v_w_in, v_conv_w, v_conv_b, v_dt_bias, v_a_log, v_d_skip, v_ssd_norm, v_q_norm, v_k_norm, v_ret_norm, v_w_out, v_ln_ffn, v_w_gate, v_w_up, v_w_down):
    w = dict(ln_mix=ln_mix, w_in=w_in, conv_w=conv_w, conv_b=conv_b, dt_bias=dt_bias, a_log=a_log, d_skip=d_skip,
             ssd_norm=ssd_norm, q_norm=q_norm, k_norm=k_norm, ret_norm=ret_norm, w_out=w_out, ln_ffn=ln_ffn,
             w_gate=w_gate, w_up=w_up, w_down=w_down)
    m = dict(ln_mix=m_ln_mix, w_in=m_w_in, conv_w=m_conv_w, conv_b=m_conv_b, dt_bias=m_dt_bias, a_log=m_a_log,
             d_skip=m_d_skip, ssd_norm=m_ssd_norm, q_norm=m_q_norm, k_norm=m_k_norm, ret_norm=m_ret_norm, w_out=m_w_out,
             ln_ffn=m_ln_ffn, w_gate=m_w_gate, w_up=m_w_up, w_down=m_w_down)
    v = dict(ln_mix=v_ln_mix, w_in=v_w_in, conv_w=v_conv_w, conv_b=v_conv_b, dt_bias=v_dt_bias, a_log=v_a_log,
             d_skip=v_d_skip, ssd_norm=v_ssd_norm, q_norm=v_q_norm, k_norm=v_k_norm, ret_norm=v_ret_norm, w_out=v_w_out,
             ln_ffn=v_ln_ffn, w_gate=v_w_gate, w_up=v_w_up, w_down=v_w_down)
    me = 4 * lax.axis_index("x") + 2 * lax.axis_index("y") + lax.axis_index("c")

    waves = {"a": [("w_in", 0), ("conv_w", 0), ("conv_w", 1)],
             "b": [(k, 0) for k in ("w_out", "w_gate", "w_up", "w_down")],
             "c": [(k, 1) for k in BIG]}
    gather = {}
    for tag, items in waves.items():
        srcs = [w[k][l] if k == "conv_w" else w[k][l].astype(BF16) for k, l in items]
        gather[tag] = _push_start(srcs, [_landing(s, me) for s in srcs], False, f"gather_{tag}_start")
    started = gather["a"][-1][0, 0] + gather["b"][-1][0, 0] + gather["c"][-1][0, 0]
    full = {}

    def arrive(tag, after):
        for (k, l), g in zip(waves[tag], _push_wait(gather[tag], after, False, f"gather_{tag}_wait")):
            full[k, l] = _full_weight(k, g)

    def layer_weights(l, names):
        return {k: full[k, l] for k in names}

    def small_weights(l):
        return {k: w[k][l] for k in SMALL}

    def layer0(h):
        arrive("a", h)
        p = small_weights(0)
        p["ln_mix"] = p["ln_mix"] + started
        return {**p, **layer_weights(0, ("w_in", "conv_w"))}

    def late0(y):
        arrive("b", y)
        return layer_weights(0, waves_b_names)

    def layer1(h):
        arrive("c", h)
        return {**small_weights(1), **layer_weights(1, BIG + ("conv_w",))}

    waves_b_names = tuple(k for k, _ in waves["b"])

    groups = {"1": [(k, 1) for k in BIG], "0a": [(k, 0) for k in ("w_down", "w_gate", "w_up")],
              "0b": [(k, 0) for k in ("w_out", "w_in")]}
    scatter = {}

    def push_grads(tag, gr):
        blocks = [_shard_block(k, gr[k]) for k, _ in groups[tag]]
        lands = [_landing(lax.dynamic_index_in_dim(b, me, 0, keepdims=False), me) for b in blocks]
        scatter[tag] = _push_start(blocks, lands, True, f"scatter_{tag}_start")
        return scatter[tag][-1]

    loss_part, gx, grads = _local_step(
        x[0], loss_target[0], [layer0, layer1], late=[late0, None],
        on_ffn=[functools.partial(push_grads, "0a"), None],
        on_all=[functools.partial(push_grads, "0b"), functools.partial(push_grads, "1")])
    loss = lax.psum(loss_part, MESH_AXES)

    out = {}
    recv = {}
    for tag, items in groups.items():
        for item, r in zip(items, _push_wait(scatter[tag], gx, True, f"scatter_{tag}_wait")):
            recv[item] = r
    for k in BIG:
        out[k] = _adamw_layers(recv[k, 0], recv[k, 1], w[k], m[k], v[k], f"adamw_{k}")
    names = SMALL + ("conv_w",)
    sizes = [int(np.prod(grads[0][k].shape)) for k in names]
    packed = jnp.concatenate([jnp.stack([grads[l][k] for l in range(DEPTH)]).reshape(-1) for k in names])
    n_small = packed.shape[0]
    rows_small = -(-n_small // 1024) * 8
    pad = lambda t, fill: jnp.concatenate([t, jnp.full((rows_small * 128 - n_small,), fill, F32)]).reshape(rows_small, 128)
    parts = _exchange([pad(packed, 0.0)], False, "gather_small_grads")[0]
    n_rep = DEPTH * sum(sizes[:-1])
    pack_rep = lambda d, fill: pad(jnp.concatenate([d[k].reshape(-1) for k in SMALL]
                                                   + [jnp.full((n_small - n_rep,), fill, F32)]), fill)
    res = _adamw(parts, pack_rep(w, 1.0), pack_rep(m, 1.0), pack_rep(v, 1.0), "adamw_small")
    res = [t.reshape(-1) for t in res]
    off = 0
    for k, sz in zip(SMALL, sizes[:-1]):
        out[k] = [t[off:off + DEPTH * sz].reshape(w[k].shape) for t in res]
        off += DEPTH * sz
    gconv = res[0][off:off + DEPTH * sizes[-1]].reshape(DEPTH, SSD_CONV, SSD_CONV_CH)
    gconv = lax.dynamic_slice_in_dim(gconv, me * conv_w.shape[2], conv_w.shape[2], axis=2)
    flat = lambda t: t.reshape(8, -1)
    resc = _adamw(flat(gconv)[None], flat(conv_w), flat(m_conv_w), flat(v_conv_w), "adamw_conv_w")
    out["conv_w"] = [t.reshape(conv_w.shape) for t in resc]

    return (loss, gx[None], *[out[k][0] for k in ORDER], *[out[k][1] for k in ORDER],
            *[out[k][2] for k in ORDER], *[out[k][3] for k in ORDER])
```

```python
import functools
import math

import jax
import jax.numpy as jnp
import numpy as np
from jax import lax
from jax.experimental import pallas as pl
from jax.experimental.pallas import tpu as pltpu

F32 = jnp.float32
BF16 = jnp.bfloat16

N_DEV = 8
MESH_AXES = ("x", "y", "c")
D_MODEL = 2048
DEPTH = 2
EPS = 1e-6
ROPE_THETA = 10000.0
SSD_HEADS = 16
SSD_HEAD_DIM = 64
SSD_WIDTH = 1024
SSD_GROUPS = 2
SSD_STATE = 128
SSD_CONV = 4
SSD_CONV_CH = 1536
ATT_HEADS = 8
ATT_HEAD_DIM = 64
ATT_WIDTH = 512
DILATED_PAIRS = ((128, 1), (512, 4), (2048, 16))
RET_HEADS = 4
RET_QK_DIM = 64
RET_V_DIM = 128
RET_QK_WIDTH = 256
RET_V_WIDTH = 512
CHUNK = 128
MIX_WIDTH = 2048
ATT_SPAN = 2048
ATT_STRIP = ATT_SPAN + CHUNK
IN_WIDTH = 5648
IN_PAD = 5760
DT_COL = 5632
D_FF = 5632
ADAM_LR = 0.001
ADAM_B1 = 0.9
ADAM_B2 = 0.999
ADAM_EPS = 1e-08
ADAM_WD = 0.01
ADAM_STEP = 10
NEG = -1e30
VMEM_LIMIT_V7X = 60 * 1024 * 1024

NN = (((1,), (0,)), ((), ()))
NT = (((1,), (1,)), ((), ()))
TN = (((0,), (0,)), ((), ()))


def _bdot(a, b, dims):
    return lax.dot_general(a.astype(BF16), b.astype(BF16), dims, preferred_element_type=F32)


def _xdot(a, b, dims):
    return lax.dot_general(a, b, dims, precision=lax.Precision.HIGHEST, preferred_element_type=F32)


def _params(*sem):
    return pltpu.CompilerParams(dimension_semantics=sem, vmem_limit_bytes=VMEM_LIMIT_V7X)


def _sigmoid(v):
    return 1.0 / (1.0 + jnp.exp(-v))


def _silu_grad(v, s):
    return s * (1.0 + v * (1.0 - s))


def _rmsnorm_fwd(x, g, name):
    S, D = x.shape
    tr = min(512, S)

    def body(x_ref, g_ref, o_ref):
        xv = x_ref[...]
        r = lax.rsqrt(jnp.mean(xv * xv, axis=-1, keepdims=True) + EPS)
        o_ref[...] = (xv * r * g_ref[...]).astype(o_ref.dtype)

    return pl.pallas_call(
        body, grid=(S // tr,),
        in_specs=[pl.BlockSpec((tr, D), lambda i: (i, 0)), pl.BlockSpec((1, D), lambda i: (0, 0))],
        out_specs=pl.BlockSpec((tr, D), lambda i: (i, 0)),
        out_shape=jax.ShapeDtypeStruct((S, D), BF16), name=name, compiler_params=_params("parallel"),
    )(x, g)


def _rmsnorm_bwd(x, dy, g, dres, name):
    S, D = x.shape
    tr = min(512, S)

    def body(x_ref, dy_ref, g_ref, dres_ref, dx_ref, dxb_ref, dg_ref):
        i = pl.program_id(0)
        xv = x_ref[...]
        r = lax.rsqrt(jnp.mean(xv * xv, axis=-1, keepdims=True) + EPS)
        n = xv * r
        dy = dy_ref[...]
        dn = dy * g_ref[...]
        dx = dres_ref[...] + r * (dn - n * jnp.mean(dn * n, axis=-1, keepdims=True))
        dx_ref[...] = dx
        dxb_ref[...] = dx.astype(BF16)
        part = jnp.sum(dy * n, axis=0, keepdims=True)

        @pl.when(i == 0)
        def _():
            dg_ref[...] = part

        @pl.when(i > 0)
        def _():
            dg_ref[...] += part

    row = pl.BlockSpec((tr, D), lambda i: (i, 0))
    vec = pl.BlockSpec((1, D), lambda i: (0, 0))
    return pl.pallas_call(
        body, grid=(S // tr,), in_specs=[row, row, vec, row], out_specs=[row, row, vec],
        out_shape=[jax.ShapeDtypeStruct((S, D), F32), jax.ShapeDtypeStruct((S, D), BF16), jax.ShapeDtypeStruct((1, D), F32)],
        name=name, compiler_params=_params("arbitrary"),
    )(x, dy, g, dres)


def _loss_grad(y, tgt, name):
    S, D = y.shape
    tr = min(512, S)

    def body(y_ref, t_ref, dy_ref, dyb_ref, l_ref):
        i = pl.program_id(0)
        err = y_ref[...] - t_ref[...]
        dy = err * (1.0 / D)
        dy_ref[...] = dy
        dyb_ref[...] = dy.astype(BF16)
        part = jnp.sum(jnp.sum(err * err, axis=1, keepdims=True), axis=0, keepdims=True) * (0.5 / D)

        @pl.when(i == 0)
        def _():
            l_ref[...] = jnp.zeros_like(l_ref)

        l_ref[...] += part

    row = pl.BlockSpec((tr, D), lambda i: (i, 0))
    return pl.pallas_call(
        body, grid=(S // tr,), in_specs=[row, row],
        out_specs=[row, row, pl.BlockSpec((8, 128), lambda i: (0, 0))],
        out_shape=[jax.ShapeDtypeStruct((S, D), F32), jax.ShapeDtypeStruct((S, D), BF16), jax.ShapeDtypeStruct((8, 128), F32)],
        name=name, compiler_params=_params("arbitrary"),
    )(y, tgt)


def _pick(n, cands):
    for c in cands:
        if n % c == 0:
            return c
    return n


def _mm(a, b, mode, name, out_dtype=F32, residual=None, tm=None, tn=None, tk=None, after=None):
    if mode == "nn":
        (M, K), (_, N) = a.shape, b.shape
    elif mode == "nt":
        (M, K), (N, _) = a.shape, b.shape
    else:
        (K, M), (_, N) = a.shape, b.shape
    tm = min(tm, M) if tm else _pick(M, (1024, 512, 256, 128))
    tn = min(tn, N) if tn else _pick(N, (1024, 1152, 1408, 512, 256, 128))
    tk = min(tk, K) if tk else _pick(K, (2048, 1920, 1408, 1024, 512, 256, 128))
    assert M % tm == 0 and N % tn == 0 and K % tk == 0, (name, M, N, K, tm, tn, tk)
    nk = K // tk
    a_spec = pl.BlockSpec((tk, tm), lambda i, j, k: (k, i)) if mode == "tn" else pl.BlockSpec((tm, tk), lambda i, j, k: (i, k))
    b_spec = pl.BlockSpec((tn, tk), lambda i, j, k: (j, k)) if mode == "nt" else pl.BlockSpec((tk, tn), lambda i, j, k: (k, j))
    o_spec = pl.BlockSpec((tm, tn), lambda i, j, k: (i, j))
    dims = {"nn": NN, "nt": NT, "tn": TN}[mode]
    has_res = residual is not None

    has_after = after is not None

    def body(*refs):
        a_ref, b_ref = refs[0], refs[1]
        r_ref = refs[2] if has_res else None
        o_ref = refs[2 + has_res + has_after]
        p = _bdot(a_ref[...], b_ref[...], dims)

        def finish(acc):
            if has_res:
                acc = acc + r_ref[...]
            o_ref[...] = acc.astype(o_ref.dtype)

        if nk == 1:
            finish(p)
        else:
            acc_ref = refs[-1]
            k = pl.program_id(2)

            @pl.when(k == 0)
            def _():
                acc_ref[...] = p

            @pl.when(k > 0)
            def _():
                acc_ref[...] += p

            @pl.when(k == nk - 1)
            def _():
                finish(acc_ref[...])

    ins = [a, b] + ([residual] if has_res else []) + ([after] if has_after else [])
    in_specs = [a_spec, b_spec] + ([o_spec] if has_res else []) + ([pl.BlockSpec(memory_space=pl.ANY)] if has_after else [])
    scratch = [pltpu.VMEM((tm, tn), F32)] if nk > 1 else []
    return pl.pallas_call(
        body, grid=(M // tm, N // tn, nk), in_specs=in_specs, out_specs=o_spec,
        out_shape=jax.ShapeDtypeStruct((M, N), out_dtype), scratch_shapes=scratch, name=name,
        compiler_params=_params("parallel", "parallel", "arbitrary"),
    )(*ins)


def _accumulate(acc_ref, p, k, nk, finish):
    @pl.when(k == 0)
    def _():
        acc_ref[...] = p

    @pl.when(k > 0)
    def _():
        acc_ref[...] += p

    @pl.when(k == nk - 1)
    def _():
        finish(acc_ref[...])


def _swiglu_fwd(hn, wg, wu, name):
    S, K = hn.shape
    nsh, _, Fs = wg.shape
    tm = _pick(S, (1024, 512))

    def body(a_ref, wg_ref, wu_ref, g_ref, u_ref, act_ref):
        a = a_ref[...]
        g = _bdot(a, wg_ref[...], NN)
        u = _bdot(a, wu_ref[...], NN)
        g_ref[...] = g.astype(BF16)
        u_ref[...] = u.astype(BF16)
        act_ref[...] = (g * _sigmoid(g) * u).astype(BF16)

    w_spec = pl.BlockSpec((None, K, Fs), lambda j, i: (j, 0, 0))
    o_spec = pl.BlockSpec((None, tm, Fs), lambda j, i: (j, i, 0))
    sh = jax.ShapeDtypeStruct((nsh, S, Fs), BF16)
    return pl.pallas_call(
        body, grid=(nsh, S // tm), in_specs=[pl.BlockSpec((tm, K), lambda j, i: (i, 0)), w_spec, w_spec],
        out_specs=[o_spec, o_spec, o_spec], out_shape=[sh, sh, sh], name=name,
        compiler_params=_params("parallel", "parallel"),
    )(hn, wg, wu)


def _swiglu_bwd(dx, wd, g, u, name):
    S, K = dx.shape
    nsh, Fs, _ = wd.shape
    tm = _pick(S, (1024, 512))

    def body(dx_ref, wd_ref, g_ref, u_ref, dg_ref, du_ref):
        da = _bdot(dx_ref[...], wd_ref[...], NT)
        gv = g_ref[...].astype(F32)
        uv = u_ref[...].astype(F32)
        s = _sigmoid(gv)
        dg_ref[...] = (da * uv * _silu_grad(gv, s)).astype(BF16)
        du_ref[...] = (da * gv * s).astype(BF16)

    o_spec = pl.BlockSpec((None, tm, Fs), lambda j, i: (j, i, 0))
    sh = jax.ShapeDtypeStruct((nsh, S, Fs), BF16)
    return pl.pallas_call(
        body, grid=(nsh, S // tm),
        in_specs=[pl.BlockSpec((tm, K), lambda j, i: (i, 0)), pl.BlockSpec((None, Fs, K), lambda j, i: (j, 0, 0)), o_spec, o_spec],
        out_specs=[o_spec, o_spec], out_shape=[sh, sh], name=name, compiler_params=_params("parallel", "parallel"),
    )(dx, wd, g, u)


def _down_proj(act, wd, residual, name):
    nsh, S, Fs = act.shape
    D = wd.shape[2]
    tm = _pick(S, (1024, 512))
    tn = _pick(D, (1024, 512))

    def body(a_ref, b_ref, r_ref, o_ref, acc_ref):
        def finish(acc):
            o_ref[...] = acc + r_ref[...]

        _accumulate(acc_ref, _bdot(a_ref[...], b_ref[...], NN), pl.program_id(2), nsh, finish)

    o_spec = pl.BlockSpec((tm, tn), lambda i, n, k: (i, n))
    return pl.pallas_call(
        body, grid=(S // tm, D // tn, nsh),
        in_specs=[pl.BlockSpec((None, tm, Fs), lambda i, n, k: (k, i, 0)), pl.BlockSpec((None, Fs, tn), lambda i, n, k: (k, 0, n)), o_spec],
        out_specs=o_spec, out_shape=jax.ShapeDtypeStruct((S, D), F32), scratch_shapes=[pltpu.VMEM((tm, tn), F32)], name=name,
        compiler_params=_params("parallel", "parallel", "arbitrary"),
    )(act, wd, residual)


def _ffn_dgrad(dg, wg, du, wu, name):
    nsh, S, Fs = dg.shape
    D = wg.shape[1]
    tm = _pick(S, (1024, 512))
    tn = _pick(D, (1024, 512))

    def body(a1_ref, b1_ref, a2_ref, b2_ref, o_ref, acc_ref):
        def finish(acc):
            o_ref[...] = acc

        p = _bdot(a1_ref[...], b1_ref[...], NT) + _bdot(a2_ref[...], b2_ref[...], NT)
        _accumulate(acc_ref, p, pl.program_id(2), nsh, finish)

    a_spec = pl.BlockSpec((None, tm, Fs), lambda i, n, k: (k, i, 0))
    b_spec = pl.BlockSpec((None, tn, Fs), lambda i, n, k: (k, n, 0))
    return pl.pallas_call(
        body, grid=(S // tm, D // tn, nsh), in_specs=[a_spec, b_spec, a_spec, b_spec],
        out_specs=pl.BlockSpec((tm, tn), lambda i, n, k: (i, n)), out_shape=jax.ShapeDtypeStruct((S, D), F32),
        scratch_shapes=[pltpu.VMEM((tm, tn), F32)], name=name,
        compiler_params=_params("parallel", "parallel", "arbitrary"),
    )(dg, wg, du, wu)


def _ffn_in_wgrad(hn, dg, du, name):
    S, D = hn.shape
    nsh, _, Fs = dg.shape
    tk = _pick(S, (1024, 512))
    nk = S // tk

    def body(a_ref, g_ref, u_ref, og_ref, ou_ref, accg_ref, accu_ref):
        k = pl.program_id(1)
        a = a_ref[...]

        def fin_g(acc):
            og_ref[...] = acc.astype(og_ref.dtype)

        def fin_u(acc):
            ou_ref[...] = acc.astype(ou_ref.dtype)

        _accumulate(accg_ref, _bdot(a, g_ref[...], TN), k, nk, fin_g)
        _accumulate(accu_ref, _bdot(a, u_ref[...], TN), k, nk, fin_u)

    d_spec = pl.BlockSpec((None, tk, Fs), lambda j, k: (j, k, 0))
    o_spec = pl.BlockSpec((None, D, Fs), lambda j, k: (j, 0, 0))
    sh = jax.ShapeDtypeStruct((nsh, D, Fs), BF16)
    return pl.pallas_call(
        body, grid=(nsh, nk), in_specs=[pl.BlockSpec((tk, D), lambda j, k: (k, 0)), d_spec, d_spec],
        out_specs=[o_spec, o_spec], out_shape=[sh, sh],
        scratch_shapes=[pltpu.VMEM((D, Fs), F32), pltpu.VMEM((D, Fs), F32)], name=name,
        compiler_params=_params("parallel", "arbitrary"),
    )(hn, dg, du)


def _down_wgrad(act, dx, name):
    nsh, S, Fs = act.shape
    D = dx.shape[1]
    tk = _pick(S, (1024, 512))
    nk = S // tk

    def body(a_ref, b_ref, o_ref, acc_ref):
        def finish(acc):
            o_ref[...] = acc.astype(o_ref.dtype)

        _accumulate(acc_ref, _bdot(a_ref[...], b_ref[...], TN), pl.program_id(1), nk, finish)

    return pl.pallas_call(
        body, grid=(nsh, nk),
        in_specs=[pl.BlockSpec((None, tk, Fs), lambda j, k: (j, k, 0)), pl.BlockSpec((tk, D), lambda j, k: (k, 0))],
        out_specs=pl.BlockSpec((None, Fs, D), lambda j, k: (j, 0, 0)), out_shape=jax.ShapeDtypeStruct((nsh, Fs, D), BF16),
        scratch_shapes=[pltpu.VMEM((Fs, D), F32)], name=name, compiler_params=_params("parallel", "arbitrary"),
    )(act, dx)


XBC_BLK0 = SSD_WIDTH // 128


def _conv_fwd(proj, w, b, name):
    S = proj.shape[0]
    T = min(512, S)

    def body(x_ref, w_ref, b_ref, o_ref, xp_ref):
        xp_ref[pl.ds(0, 8), :] = jnp.zeros((8, 128), F32)
        xp_ref[pl.ds(8, S), :] = x_ref[...]
        wv = w_ref[...]
        bv = b_ref[...]

        def step(c, carry):
            base = pl.multiple_of(c * T, T)
            acc = wv[0:1] * xp_ref[pl.ds(base + 5, T), :]
            for i in range(1, SSD_CONV):
                acc = acc + wv[i:i + 1] * xp_ref[pl.ds(base + 5 + i, T), :]
            acc = bv + acc
            o_ref[pl.ds(base, T), :] = acc * _sigmoid(acc)
            return carry

        lax.fori_loop(0, S // T, step, 0)

    return pl.pallas_call(
        body, grid=(SSD_CONV_CH // 128,),
        in_specs=[pl.BlockSpec((S, 128), lambda j: (0, XBC_BLK0 + j)), pl.BlockSpec((SSD_CONV, 128), lambda j: (0, j)),
                  pl.BlockSpec((1, 128), lambda j: (0, j))],
        out_specs=pl.BlockSpec((S, 128), lambda j: (0, j)),
        out_shape=jax.ShapeDtypeStruct((S, SSD_CONV_CH), F32),
        scratch_shapes=[pltpu.VMEM((S + 8, 128), F32)], name=name, compiler_params=_params("parallel"),
    )(proj, w, b)


def _conv_bwd(dxs, dbm, dcm, proj, w, b, name):
    S = proj.shape[0]
    T = min(512, S)
    NX, NB = SSD_WIDTH // 128, SSD_GROUPS * SSD_STATE // 128

    def body(dxs_ref, dbm_ref, dcm_ref, x_ref, w_ref, b_ref, dx_ref, dw_ref, db_ref, xp_ref, dcp_ref):
        j = pl.program_id(0)

        @pl.when(j < NX)
        def _():
            dcp_ref[pl.ds(0, S), :] = dxs_ref[...]

        @pl.when((j >= NX) & (j < NX + NB))
        def _():
            dcp_ref[pl.ds(0, S), :] = dbm_ref[...]

        @pl.when(j >= NX + NB)
        def _():
            dcp_ref[pl.ds(0, S), :] = dcm_ref[...]

        da_ref = dcp_ref
        xp_ref[pl.ds(0, 8), :] = jnp.zeros((8, 128), F32)
        xp_ref[pl.ds(8, S), :] = x_ref[...]
        dcp_ref[pl.ds(S, 8), :] = jnp.zeros((8, 128), F32)
        wv = w_ref[...]
        bv = b_ref[...]

        def step1(c, carry):
            base = pl.multiple_of(c * T, T)
            xs = [xp_ref[pl.ds(base + 5 + i, T), :] for i in range(SSD_CONV)]
            acc = wv[0:1] * xs[0]
            for i in range(1, SSD_CONV):
                acc = acc + wv[i:i + 1] * xs[i]
            acc = bv + acc
            s = _sigmoid(acc)
            dc = da_ref[pl.ds(base, T), :] * _silu_grad(acc, s)
            dcp_ref[pl.ds(base, T), :] = dc
            new = tuple(carry[i] + jnp.sum(xs[i] * dc, axis=0, keepdims=True) for i in range(SSD_CONV))
            return new + (carry[SSD_CONV] + jnp.sum(dc, axis=0, keepdims=True),)

        z = jnp.zeros((1, 128), F32)
        res = lax.fori_loop(0, S // T, step1, (z,) * (SSD_CONV + 1))
        for i in range(SSD_CONV):
            dw_ref[pl.ds(i, 1), :] = res[i]
        db_ref[...] = res[SSD_CONV]

        def step2(c, carry):
            base = pl.multiple_of(c * T, T)
            acc = wv[0:1] * dcp_ref[pl.ds(base + 3, T), :]
            for i in range(1, SSD_CONV):
                acc = acc + wv[i:i + 1] * dcp_ref[pl.ds(base + 3 - i, T), :]
            dx_ref[pl.ds(base, T), :] = acc.astype(dx_ref.dtype)
            return carry

        lax.fori_loop(0, S // T, step2, 0)

    col = pl.BlockSpec((S, 128), lambda j: (0, j))
    clamp = lambda j, lo, n: jnp.clip(j - lo, 0, n - 1)
    return pl.pallas_call(
        body, grid=(SSD_CONV_CH // 128,),
        in_specs=[pl.BlockSpec((S, 128), lambda j: (0, clamp(j, 0, NX))), pl.BlockSpec((S, 128), lambda j: (0, clamp(j, NX, NB))),
                  pl.BlockSpec((S, 128), lambda j: (0, clamp(j, NX + NB, NB))),
                  pl.BlockSpec((S, 128), lambda j: (0, XBC_BLK0 + j)), pl.BlockSpec((SSD_CONV, 128), lambda j: (0, j)),
                  pl.BlockSpec((1, 128), lambda j: (0, j))],
        out_specs=[col, pl.BlockSpec((SSD_CONV, 128), lambda j: (0, j)), pl.BlockSpec((1, 128), lambda j: (0, j))],
        out_shape=[jax.ShapeDtypeStruct((S, SSD_CONV_CH), BF16), jax.ShapeDtypeStruct((SSD_CONV, SSD_CONV_CH), F32),
                   jax.ShapeDtypeStruct((1, SSD_CONV_CH), F32)],
        scratch_shapes=[pltpu.VMEM((S + 8, 128), F32), pltpu.VMEM((S + 8, 128), F32)], name=name,
        compiler_params=_params("arbitrary"),
    )(dxs, dbm, dcm, proj, w, b)


HPG = SSD_HEADS // SSD_GROUPS
GW = HPG * SSD_HEAD_DIM


def _ssd_chunk_terms(dtr, bias, alog, tril, triu):
    pre = dtr + bias
    dt = jnp.maximum(pre, 0.0) + jnp.log(1.0 + jnp.exp(-jnp.abs(pre)))
    a_neg = -jnp.exp(alog)
    a = dt * a_neg
    acum = _xdot(tril, a, NN)
    acum_t = _xdot(a, triu, TN)
    return pre, dt, a_neg, acum, acum_t


def _head_expanders():
    h64 = lax.broadcasted_iota(jnp.int32, (HPG, GW), 0) == lax.broadcasted_iota(jnp.int32, (HPG, GW), 1) // SSD_HEAD_DIM
    h128 = lax.broadcasted_iota(jnp.int32, (HPG, HPG * CHUNK), 0) == lax.broadcasted_iota(jnp.int32, (HPG, HPG * CHUNK), 1) // CHUNK
    return h64.astype(F32), h128.astype(F32)


def _ssd_fwd(xbc, proj, dtr, dt_bias, a_log, d_rep, gain, name):
    S = xbc.shape[0]
    L = CHUNK
    T = min(512, S)
    CPS = T // L
    NC = S // L

    def body(x_ref, b_ref, c_ref, z_ref, dtr_ref, bias_ref, alog_ref, d_ref, gain_ref, y_ref, yraw_ref, st_ref, state):
        i = pl.program_id(1)

        @pl.when(i == 0)
        def _():
            state[...] = jnp.zeros_like(state)

        row = lax.broadcasted_iota(jnp.int32, (L, L), 0)
        col = lax.broadcasted_iota(jnp.int32, (L, L), 1)
        causal = row >= col
        tril = causal.astype(F32)
        triu = (row <= col).astype(F32)
        low = col < SSD_HEAD_DIM
        e64, e128 = _head_expanders()
        for c in range(CPS):
            rows = pl.ds(c * L, L)
            xv = x_ref[rows, :]
            bm = b_ref[rows, :]
            cm = c_ref[rows, :]
            _, dt, _, acum, acum_t = _ssd_chunk_terms(dtr_ref[rows, :], bias_ref[...], alog_ref[...], tril, triu)
            ac = _xdot(acum, e64, NN)
            ac_sq = _xdot(acum, e128, NN)
            xd = xv * _xdot(dt, e64, NN)
            ac_last = ac[L - 1:L, :]
            sp = state[...]
            st_ref[c] = sp
            yoff = _bdot(cm, sp, NN) * jnp.exp(ac)
            state[...] = sp * jnp.exp(ac_last) + _bdot(bm, xd * jnp.exp(ac_last - ac), TN)
            gmat = _bdot(cm, bm, NT)
            for q in range(HPG // 2):
                pair = slice(q * 128, (q + 1) * 128)
                tile = xd[:, pair]
                y = yoff[:, pair]
                for j, keep in ((2 * q, low), (2 * q + 1, ~low)):
                    lam = jnp.exp(jnp.where(causal, ac_sq[:, j * L:(j + 1) * L] - acum_t[j:j + 1, :], NEG))
                    y = y + _bdot(gmat * lam, jnp.where(keep, tile, 0.0), NN)
                yraw_ref[rows, pair] = y
            zz = z_ref[rows, :]
            u = (yraw_ref[rows, :] + xv * d_ref[...]) * (zz * _sigmoid(zz))
            r = lax.rsqrt(jnp.mean(u * u, axis=-1, keepdims=True) + EPS)
            y_ref[rows, :] = (u * r * gain_ref[...]).astype(y_ref.dtype)

    vec8 = pl.BlockSpec((None, 1, HPG), lambda g, i: (g, 0, 0))
    return pl.pallas_call(
        body, grid=(SSD_GROUPS, S // T),
        in_specs=[pl.BlockSpec((T, GW), lambda g, i: (i, g)),
                  pl.BlockSpec((T, SSD_STATE), lambda g, i: (i, SSD_WIDTH // SSD_STATE + g)),
                  pl.BlockSpec((T, SSD_STATE), lambda g, i: (i, SSD_WIDTH // SSD_STATE + SSD_GROUPS + g)),
                  pl.BlockSpec((T, GW), lambda g, i: (i, g)),
                  pl.BlockSpec((None, T, HPG), lambda g, i: (g, i, 0)),
                  vec8, vec8,
                  pl.BlockSpec((1, GW), lambda g, i: (0, g)), pl.BlockSpec((1, GW), lambda g, i: (0, g))],
        out_specs=[pl.BlockSpec((T, GW), lambda g, i: (i, g)), pl.BlockSpec((T, GW), lambda g, i: (i, g)),
                   pl.BlockSpec((CPS, None, SSD_STATE, GW), lambda g, i: (i, g, 0, 0))],
        out_shape=[jax.ShapeDtypeStruct((S, MIX_WIDTH), BF16), jax.ShapeDtypeStruct((S, SSD_WIDTH), F32),
                   jax.ShapeDtypeStruct((NC, SSD_GROUPS, SSD_STATE, GW), F32)],
        scratch_shapes=[pltpu.VMEM((SSD_STATE, GW), F32)], name=name,
        compiler_params=_params("arbitrary", "arbitrary"),
    )(xbc, xbc, xbc, proj, dtr, dt_bias, a_log, d_rep, gain)


def _ssd_bwd(dy, yraw, xbc, proj, dtr, dt_bias, a_log, d_rep, gain, states, name):
    S = xbc.shape[0]
    L = CHUNK
    T = min(512, S)
    CPS = T // L
    NI = S // T

    def body(dy_ref, yraw_ref, x_ref, b_ref, c_ref, z_ref, dtr_ref, bias_ref, alog_ref, d_ref, gain_ref, st_ref,
             dz_ref, dx_ref, db_ref, dc_ref, ddtr_ref, dbias_ref, dalog_ref, dd_ref, dgain_ref, dstate, dxd_ref):
        i = pl.program_id(1)

        @pl.when(i == 0)
        def _():
            dstate[...] = jnp.zeros_like(dstate)
            dbias_ref[...] = jnp.zeros_like(dbias_ref)
            dalog_ref[...] = jnp.zeros_like(dalog_ref)
            dd_ref[...] = jnp.zeros_like(dd_ref)
            dgain_ref[...] = jnp.zeros_like(dgain_ref)

        row = lax.broadcasted_iota(jnp.int32, (L, L), 0)
        col = lax.broadcasted_iota(jnp.int32, (L, L), 1)
        causal = row >= col
        tril = causal.astype(F32)
        triu = (row <= col).astype(F32)
        low = col < SSD_HEAD_DIM
        e64, e128 = _head_expanders()
        lane8 = lax.broadcasted_iota(jnp.int32, (1, HPG), 1)
        sub8 = lax.broadcasted_iota(jnp.int32, (HPG, 1), 0)
        eye8 = (lax.broadcasted_iota(jnp.int32, (HPG, HPG), 0) == lax.broadcasted_iota(jnp.int32, (HPG, HPG), 1)).astype(F32)
        last_row = (lax.broadcasted_iota(jnp.int32, (L, 1), 0) == L - 1).astype(F32)
        for c in reversed(range(CPS)):
            rows = pl.ds(c * L, L)
            xv = x_ref[rows, :]
            bm = b_ref[rows, :]
            cm = c_ref[rows, :]
            zz = z_ref[rows, :]
            dvec = d_ref[...]
            sz = _sigmoid(zz)
            silu_z = zz * sz
            v = yraw_ref[rows, :] + xv * dvec
            u = v * silu_z
            r = lax.rsqrt(jnp.mean(u * u, axis=-1, keepdims=True) + EPS)
            n = u * r
            do = dy_ref[rows, :]
            dgain_ref[...] += jnp.sum(do * n, axis=0, keepdims=True)
            dn = do * gain_ref[...]
            du = r * (dn - n * jnp.mean(dn * n, axis=-1, keepdims=True))
            dz_ref[rows, :] = (du * v * _silu_grad(zz, sz)).astype(dz_ref.dtype)
            dyv = du * silu_z
            dd_ref[...] += _xdot(jnp.sum(dyv * xv, axis=0, keepdims=True), e64, NT)
            pre, dt, a_neg, acum, acum_t = _ssd_chunk_terms(dtr_ref[rows, :], bias_ref[...], alog_ref[...], tril, triu)
            ac = _xdot(acum, e64, NN)
            ac_sq = _xdot(acum, e128, NN)
            dt_w = _xdot(dt, e64, NN)
            xd = xv * dt_w
            ac_last = ac[L - 1:L, :]
            ea = jnp.exp(ac)
            w = jnp.exp(ac_last - ac)
            ea_last = jnp.exp(ac_last)
            sp = st_ref[c]
            ds = dstate[...]
            dye = dyv * ea
            yoff = _bdot(cm, sp, NN) * ea
            bds = _bdot(bm, ds, NN)
            dcm = _bdot(dye, sp, NT)
            dbm = _bdot(xd * w, ds, NT)
            dstate[...] = ds * ea_last + _bdot(cm, dye, TN)
            w8 = jnp.exp(acum[L - 1:L, :] - acum)
            dw8 = _xdot(xd * bds, e64, NT)
            dac8 = _xdot(dyv * yoff, e64, NT) - dw8 * w8
            tail8 = jnp.sum(dw8 * w8, axis=0, keepdims=True) + jnp.exp(acum[L - 1:L, :]) * _xdot(
                jnp.sum(ds * sp, axis=0, keepdims=True), e64, NT)
            dac8 = dac8 + last_row * tail8
            gmat = _bdot(cm, bm, NT)
            dgmat = jnp.zeros((L, L), F32)
            colsum_t = jnp.zeros((HPG, L), F32)
            for q in range(HPG // 2):
                pair = slice(q * 128, (q + 1) * 128)
                xd_tile = xd[:, pair]
                dy_tile = dyv[:, pair]
                dxd_tile = bds[:, pair] * w[:, pair]
                for j, keep in ((2 * q, low), (2 * q + 1, ~low)):
                    lam = jnp.exp(jnp.where(causal, ac_sq[:, j * L:(j + 1) * L] - acum_t[j:j + 1, :], NEG))
                    mh = gmat * lam
                    dyj = jnp.where(keep, dy_tile, 0.0)
                    dxd_tile = dxd_tile + _bdot(mh, dyj, TN)
                    dm = _bdot(dyj, xd_tile, NT)
                    dgmat = dgmat + dm * lam
                    qm = dm * mh
                    dac8 = dac8 + jnp.sum(qm, axis=1, keepdims=True) * (lane8 == j).astype(F32)
                    colsum_t = colsum_t + (sub8 == j).astype(F32) * jnp.sum(qm, axis=0, keepdims=True)
                dxd_ref[:, pair] = dxd_tile
            dac8 = dac8 - _xdot(colsum_t, eye8, TN)
            dxd = dxd_ref[...]
            dx_ref[rows, :] = dxd * dt_w + dyv * dvec
            dc_ref[rows, :] = dcm + _bdot(dgmat, bm, NN)
            db_ref[rows, :] = dbm + _bdot(dgmat, cm, TN)
            da8 = _xdot(triu, dac8, NN)
            ddt8 = _xdot(dxd * xv, e64, NT) + da8 * a_neg
            dalog_ref[...] += jnp.sum(da8 * dt, axis=0, keepdims=True) * a_neg
            dpre = ddt8 * _sigmoid(pre)
            ddtr_ref[rows, :] = dpre
            dbias_ref[...] += jnp.sum(dpre, axis=0, keepdims=True)

    rev = lambda i: NI - 1 - i
    vec8 = pl.BlockSpec((None, 1, HPG), lambda g, i: (g, 0, 0))
    grp = pl.BlockSpec((T, GW), lambda g, i: (rev(i), g))
    bspec = pl.BlockSpec((T, SSD_STATE), lambda g, i: (rev(i), SSD_WIDTH // SSD_STATE + g))
    cspec = pl.BlockSpec((T, SSD_STATE), lambda g, i: (rev(i), SSD_WIDTH // SSD_STATE + SSD_GROUPS + g))
    gvec = pl.BlockSpec((1, GW), lambda g, i: (0, g))
    st_spec = pl.BlockSpec((CPS, None, SSD_STATE, GW), lambda g, i: (rev(i), g, 0, 0))
    small = jax.ShapeDtypeStruct((SSD_GROUPS, 1, HPG), F32)
    return pl.pallas_call(
        body, grid=(SSD_GROUPS, NI),
        in_specs=[grp, grp, grp, bspec, cspec, grp, pl.BlockSpec((None, T, HPG), lambda g, i: (g, rev(i), 0)),
                  vec8, vec8, gvec, gvec, st_spec],
        out_specs=[grp, grp, pl.BlockSpec((T, SSD_STATE), lambda g, i: (rev(i), g)),
                   pl.BlockSpec((T, SSD_STATE), lambda g, i: (rev(i), g)),
                   pl.BlockSpec((None, T, HPG), lambda g, i: (g, rev(i), 0)), vec8, vec8, vec8, gvec],
        out_shape=[jax.ShapeDtypeStruct((S, SSD_WIDTH), BF16), jax.ShapeDtypeStruct((S, SSD_WIDTH), F32),
                   jax.ShapeDtypeStruct((S, SSD_GROUPS * SSD_STATE), F32), jax.ShapeDtypeStruct((S, SSD_GROUPS * SSD_STATE), F32),
                   jax.ShapeDtypeStruct((SSD_GROUPS, S, HPG), F32), small, small, small,
                   jax.ShapeDtypeStruct((1, SSD_WIDTH), F32)],
        scratch_shapes=[pltpu.VMEM((SSD_STATE, GW), F32), pltpu.VMEM((L, GW), F32)],
        name=name, compiler_params=_params("arbitrary", "arbitrary"),
    )(dy, yraw, xbc, xbc, xbc, proj, dtr, dt_bias, a_log, d_rep, gain, states)


def _swap_halves(t):
    w = t.shape[1]
    lane = lax.broadcasted_iota(jnp.int32, t.shape, 1)
    return jnp.where((lane % 64) < 32, pltpu.roll(t, w - 32, axis=1), pltpu.roll(t, 32, axis=1))


def _widen(tab, w):
    return tab if w == 128 else jnp.concatenate([tab] * (w // 128), axis=1)


def _rope(t, cos, sin_signed):
    return t * cos + _swap_halves(t) * sin_signed


def _rope_t(d, cos, sin_signed):
    return d * cos - _swap_halves(d) * sin_signed


def _group_sum64(v, bd):
    hi = v.astype(BF16)
    lo = (v - hi.astype(F32)).astype(BF16)
    return (lax.dot_general(hi, bd, NN, preferred_element_type=F32)
            + lax.dot_general(lo, bd, NN, preferred_element_type=F32))


AQ_BLK = 2560 // ATT_WIDTH


def _att_prep_fwd(proj, qg, kg, cos, sin, bd, name):
    S = proj.shape[0]
    T = min(512, S)
    PB = ATT_SPAN // T
    src = lambda i: jnp.maximum(i - PB, 0)

    def body(q_ref, k_ref, v_ref, qg_ref, kg_ref, cos_ref, sin_ref, bd_ref, qo_ref, ko_ref, vo_ref):
        i = pl.program_id(0)

        @pl.when(i < PB)
        def _():
            ko_ref[...] = jnp.zeros_like(ko_ref)
            vo_ref[...] = jnp.zeros_like(vo_ref)

        @pl.when(i >= PB)
        def _():
            cw = _widen(cos_ref[...], ATT_WIDTH)
            sw = _widen(sin_ref[...], ATT_WIDTH)
            bdv = bd_ref[...]

            def norm_rope(t, gain):
                ss = _group_sum64(t * t, bdv)
                return _rope(t * lax.rsqrt(ss * (1.0 / ATT_HEAD_DIM) + EPS) * gain, cw, sw)

            qo_ref[...] = (norm_rope(q_ref[...], qg_ref[...]) * (ATT_HEAD_DIM ** -0.5)).astype(BF16)
            kt = norm_rope(k_ref[...], kg_ref[...]).astype(BF16)
            vt = v_ref[...].astype(BF16)
            for h in range(ATT_HEADS):
                ko_ref[h] = kt[:, h * 64:(h + 1) * 64]
                vo_ref[h] = vt[:, h * 64:(h + 1) * 64]

    vec = pl.BlockSpec((1, ATT_WIDTH), lambda i: (0, 0))
    tab = pl.BlockSpec((T, 128), lambda i: (src(i), 0))
    hm = pl.BlockSpec((ATT_HEADS, T, ATT_HEAD_DIM), lambda i: (0, i, 0))
    hm_shape = jax.ShapeDtypeStruct((ATT_HEADS, ATT_SPAN + S, ATT_HEAD_DIM), BF16)
    return pl.pallas_call(
        body, grid=(PB + S // T,),
        in_specs=[pl.BlockSpec((T, ATT_WIDTH), lambda i: (src(i), AQ_BLK)), pl.BlockSpec((T, ATT_WIDTH), lambda i: (src(i), AQ_BLK + 1)),
                  pl.BlockSpec((T, ATT_WIDTH), lambda i: (src(i), AQ_BLK + 2)), vec, vec, tab, tab,
                  pl.BlockSpec((ATT_WIDTH, ATT_WIDTH), lambda i: (0, 0))],
        out_specs=[pl.BlockSpec((T, ATT_WIDTH), lambda i: (src(i), 0)), hm, hm],
        out_shape=[jax.ShapeDtypeStruct((S, ATT_WIDTH), BF16), hm_shape, hm_shape],
        name=name, compiler_params=_params("arbitrary"),
    )(proj, proj, proj, qg, kg, cos, sin, bd)


def _att_prep_bwd(proj, dq_h, dk_h, dv_h, qg, kg, cos, sin, bd, name):
    S = proj.shape[0]
    T = min(512, S)
    NI = S // T
    PB = ATT_SPAN // T

    def body(q_ref, k_ref, dqh_ref, dkh_ref, dvh_ref, qg_ref, kg_ref, cos_ref, sin_ref, bd_ref,
             dqo_ref, dko_ref, dvo_ref, dqg_ref, dkg_ref, acc_ref, dq_ref, dk_ref):
        i = pl.program_id(0)

        @pl.when(i == 0)
        def _():
            acc_ref[...] = jnp.zeros_like(acc_ref)

        for h in range(ATT_HEADS):
            hs = slice(h * 64, (h + 1) * 64)
            dq_ref[:, hs] = dqh_ref[h]
            dk_ref[:, hs] = dkh_ref[h]
            dvo_ref[:, hs] = dvh_ref[h].astype(BF16)
        cw = _widen(cos_ref[...], ATT_WIDTH)
        sw = _widen(sin_ref[...], ATT_WIDTH)
        bdv = bd_ref[...]

        def one(t, d_rot, gain, scale, slot):
            ss = _group_sum64(t * t, bdv)
            r = lax.rsqrt(ss * (1.0 / ATT_HEAD_DIM) + EPS)
            n = t * r
            d_ng = _rope_t(d_rot * scale, cw, sw)
            acc_ref[pl.ds(slot, 1), :] += jnp.sum(d_ng * n, axis=0, keepdims=True)
            dn = d_ng * gain
            return r * (dn - n * (_group_sum64(dn * n, bdv) * (1.0 / ATT_HEAD_DIM)))

        dqo_ref[...] = one(q_ref[...], dq_ref[...], qg_ref[...], ATT_HEAD_DIM ** -0.5, 0).astype(BF16)
        dko_ref[...] = one(k_ref[...], dk_ref[...], kg_ref[...], 1.0, 1).astype(BF16)

        @pl.when(i == NI - 1)
        def _():
            a = acc_ref[...]
            f = a[:, 0:64]
            for h in range(1, ATT_HEADS):
                f = f + a[:, h * 64:(h + 1) * 64]
            dqg_ref[...] = f[0:1]
            dkg_ref[...] = f[1:2]

    vec = pl.BlockSpec((1, ATT_WIDTH), lambda i: (0, 0))
    tab = pl.BlockSpec((T, 128), lambda i: (i, 0))
    row = pl.BlockSpec((T, ATT_WIDTH), lambda i: (i, 0))
    g64 = pl.BlockSpec((1, ATT_HEAD_DIM), lambda i: (0, 0))
    padded = pl.BlockSpec((ATT_HEADS, T, ATT_HEAD_DIM), lambda i: (0, i + PB, 0))
    flat = jax.ShapeDtypeStruct((S, ATT_WIDTH), BF16)
    return pl.pallas_call(
        body, grid=(NI,),
        in_specs=[pl.BlockSpec((T, ATT_WIDTH), lambda i: (i, AQ_BLK)), pl.BlockSpec((T, ATT_WIDTH), lambda i: (i, AQ_BLK + 1)),
                  pl.BlockSpec((ATT_HEADS, T, ATT_HEAD_DIM), lambda i: (0, i, 0)), padded, padded,
                  vec, vec, tab, tab, pl.BlockSpec((ATT_WIDTH, ATT_WIDTH), lambda i: (0, 0))],
        out_specs=[row, row, row, g64, g64],
        out_shape=[flat, flat, flat, jax.ShapeDtypeStruct((1, ATT_HEAD_DIM), F32), jax.ShapeDtypeStruct((1, ATT_HEAD_DIM), F32)],
        scratch_shapes=[pltpu.VMEM((8, ATT_WIDTH), F32), pltpu.VMEM((T, ATT_WIDTH), F32), pltpu.VMEM((T, ATT_WIDTH), F32)],
        name=name, compiler_params=_params("arbitrary"),
    )(proj, proj, dq_h, dk_h, dv_h, qg, kg, cos, sin, bd)


def _att_bias():
    qpos = np.arange(CHUNK)[:, None] + ATT_SPAN
    kpos = np.arange(ATT_STRIP)[None, :]
    rel = qpos - kpos
    mult = np.zeros((CHUNK, ATT_STRIP), np.float64)
    for window, dil in DILATED_PAIRS:
        mult += (rel >= 0) & (rel % dil == 0) & (rel // dil <= window // dil)
    return np.where(mult > 0, np.log(np.maximum(mult, 1.0)), NEG).astype(np.float32)


def _att_scores(q, ks, bias, i):
    s = _bdot(q, ks, NT) + bias
    kcol = lax.broadcasted_iota(jnp.int32, (1, ATT_STRIP), 1) + i * CHUNK
    return jnp.where(kcol >= ATT_SPAN, s, NEG)


def _att_fwd(q, kp, vp, bias, y, name):
    S = q.shape[0]
    SP = kp.shape[1]

    def body(q_ref, k_ref, v_ref, bias_ref, y_ref, o_ref):
        i = pl.program_id(1)
        start = pl.multiple_of(i * CHUNK, CHUNK)
        qv = q_ref[...]
        for hh in range(2):
            s = _att_scores(qv[:, hh * 64:(hh + 1) * 64], k_ref[hh, pl.ds(start, ATT_STRIP), :], bias_ref[...], i)
            m = jnp.max(s, axis=-1, keepdims=True)
            p = jnp.exp(s - m)
            den = jnp.sum(p, axis=-1, keepdims=True)
            o = _bdot(p, v_ref[hh, pl.ds(start, ATT_STRIP), :], NN) / den
            o_ref[:, hh * 64:(hh + 1) * 64] = o.astype(o_ref.dtype)

    kv = pl.BlockSpec((2, SP, ATT_HEAD_DIM), lambda hp, i: (hp, 0, 0))
    return pl.pallas_call(
        body, grid=(ATT_HEADS // 2, S // CHUNK),
        in_specs=[pl.BlockSpec((CHUNK, 128), lambda hp, i: (i, hp)), kv, kv,
                  pl.BlockSpec((CHUNK, ATT_STRIP), lambda hp, i: (0, 0)), pl.BlockSpec(memory_space=pl.ANY)],
        out_specs=pl.BlockSpec((CHUNK, 128), lambda hp, i: (i, SSD_WIDTH // 128 + hp)),
        out_shape=jax.ShapeDtypeStruct(y.shape, y.dtype), input_output_aliases={4: 0}, name=name,
        compiler_params=_params("parallel", "arbitrary"),
    )(q, kp, vp, bias, y)


def _att_bwd(q, kp, vp, bias, dy, name):
    S = q.shape[0]
    SP = kp.shape[1]

    def body(q_ref, k_ref, v_ref, bias_ref, do_ref, dq_ref, dk_ref, dv_ref):
        h = pl.program_id(0)
        i = pl.program_id(1)

        @pl.when(i == 0)
        def _():
            dk_ref[...] = jnp.zeros_like(dk_ref)
            dv_ref[...] = jnp.zeros_like(dv_ref)

        start = pl.multiple_of(i * CHUNK, CHUNK)
        strip = pl.ds(start, ATT_STRIP)
        odd = (h % 2) == 1
        qv = q_ref[...]
        dov = do_ref[...]
        qh = jnp.where(odd, qv[:, 64:128], qv[:, 0:64])
        doh = jnp.where(odd, dov[:, 64:128], dov[:, 0:64])
        ks = k_ref[strip, :]
        vs = v_ref[strip, :]
        s = _att_scores(qh, ks, bias_ref[...], i)
        m = jnp.max(s, axis=-1, keepdims=True)
        p = jnp.exp(s - m)
        p = p / jnp.sum(p, axis=-1, keepdims=True)
        dp = _bdot(doh, vs, NT)
        dsc = p * (dp - jnp.sum(p * dp, axis=-1, keepdims=True))
        dq_ref[...] = _bdot(dsc, ks, NN)
        dv_ref[strip, :] += _bdot(p, doh, TN)
        dk_ref[strip, :] += _bdot(dsc, qh, TN)

    kv = pl.BlockSpec((None, SP, ATT_HEAD_DIM), lambda h, i: (h, 0, 0))
    return pl.pallas_call(
        body, grid=(ATT_HEADS, S // CHUNK),
        in_specs=[pl.BlockSpec((CHUNK, 128), lambda h, i: (i, h // 2)), kv, kv,
                  pl.BlockSpec((CHUNK, ATT_STRIP), lambda h, i: (0, 0)),
                  pl.BlockSpec((CHUNK, 128), lambda h, i: (i, SSD_WIDTH // 128 + h // 2))],
        out_specs=[pl.BlockSpec((None, CHUNK, ATT_HEAD_DIM), lambda h, i: (h, i, 0)), kv, kv],
        out_shape=[jax.ShapeDtypeStruct((ATT_HEADS, S, ATT_HEAD_DIM), F32), jax.ShapeDtypeStruct((ATT_HEADS, SP, ATT_HEAD_DIM), F32),
                   jax.ShapeDtypeStruct((ATT_HEADS, SP, ATT_HEAD_DIM), F32)],
        name=name, compiler_params=_params("parallel", "arbitrary"),
    )(q, kp, vp, bias, dy)


RQ_BLK = 4096 // RET_QK_WIDTH
RV_BLK = 4608 // RET_V_WIDTH
RET_LOG_GAMMA = tuple(math.log1p(-2.0 ** (-5.0 - h)) for h in range(RET_HEADS))


def _ret_decays(h):
    L = CHUNK
    lg = RET_LOG_GAMMA[h]
    row = lax.broadcasted_iota(jnp.int32, (L, L), 0)
    col = lax.broadcasted_iota(jnp.int32, (L, L), 1)
    rel = (row - col).astype(F32)
    dm = jnp.where(rel >= 0, jnp.exp(jnp.maximum(rel, 0.0) * lg), 0.0)
    idx = lax.broadcasted_iota(jnp.int32, (L, 1), 0).astype(F32)
    kte = jnp.exp((L - 1 - idx) * lg)
    qfs = jnp.exp((idx + 1.0) * lg)
    return dm, kte, qfs, math.exp(L * lg)


def _ret_fwd(proj, cos, sin, gain, y, name):
    S = proj.shape[0]
    L = CHUNK
    T = min(512, S)
    CPS = T // L
    NC = S // L

    def body(q_ref, k_ref, v_ref, g_ref, cos_ref, sin_ref, gain_ref, yin_ref, y_ref, o_ref, st_ref, state):
        i = pl.program_id(0)

        @pl.when(i == 0)
        def _():
            state[...] = jnp.zeros_like(state)

        dec = [_ret_decays(h) for h in range(RET_HEADS)]
        for c in range(CPS):
            rows = pl.ds(c * L, L)
            cw = _widen(cos_ref[rows, :], RET_QK_WIDTH)
            sw = _widen(sin_ref[rows, :], RET_QK_WIDTH)
            qv = _rope(q_ref[rows, :], cw, sw)
            kv = _rope(k_ref[rows, :], cw, sw) * (RET_QK_DIM ** -0.5)
            for h in range(RET_HEADS):
                dm, kte, qfs, cd = dec[h]
                qh = qv[:, h * 64:(h + 1) * 64]
                kh = kv[:, h * 64:(h + 1) * 64]
                vs = slice(h * RET_V_DIM, (h + 1) * RET_V_DIM)
                vh = v_ref[rows, vs]
                sp = state[h]
                st_ref[c, h] = sp
                o = _bdot(_bdot(qh, kh, NT) * dm, vh, NN) + _bdot(qh * qfs, sp, NN)
                state[h] = cd * sp + _bdot(kh * kte, vh, TN)
                o_ref[rows, vs] = o
                gh = g_ref[rows, vs]
                r = lax.rsqrt(jnp.mean(o * o, axis=-1, keepdims=True) + EPS)
                y_ref[rows, vs] = (o * r * gain_ref[:, vs] * (gh * _sigmoid(gh))).astype(y_ref.dtype)

    tab = pl.BlockSpec((T, 128), lambda i: (i, 0))
    wide = pl.BlockSpec((T, RET_V_WIDTH), lambda i: (i, 0))
    return pl.pallas_call(
        body, grid=(S // T,),
        in_specs=[pl.BlockSpec((T, RET_QK_WIDTH), lambda i: (i, RQ_BLK)), pl.BlockSpec((T, RET_QK_WIDTH), lambda i: (i, RQ_BLK + 1)),
                  pl.BlockSpec((T, RET_V_WIDTH), lambda i: (i, RV_BLK)), pl.BlockSpec((T, RET_V_WIDTH), lambda i: (i, RV_BLK + 1)),
                  tab, tab, pl.BlockSpec((1, RET_V_WIDTH), lambda i: (0, 0)), pl.BlockSpec(memory_space=pl.ANY)],
        out_specs=[pl.BlockSpec((T, RET_V_WIDTH), lambda i: (i, (SSD_WIDTH + ATT_WIDTH) // RET_V_WIDTH)), wide,
                   pl.BlockSpec((CPS, RET_HEADS, RET_QK_DIM, RET_V_DIM), lambda i: (i, 0, 0, 0))],
        out_shape=[jax.ShapeDtypeStruct(y.shape, y.dtype), jax.ShapeDtypeStruct((S, RET_V_WIDTH), F32),
                   jax.ShapeDtypeStruct((NC, RET_HEADS, RET_QK_DIM, RET_V_DIM), F32)],
        input_output_aliases={7: 0},
        scratch_shapes=[pltpu.VMEM((RET_HEADS, RET_QK_DIM, RET_V_DIM), F32)], name=name,
        compiler_params=_params("arbitrary"),
    )(proj, proj, proj, proj, cos, sin, gain, y)


def _ret_bwd(dy, oraw, proj, cos, sin, gain, states, name):
    S = proj.shape[0]
    L = CHUNK
    T = min(512, S)
    CPS = T // L
    NI = S // T

    def body(dy_ref, o_ref, q_ref, k_ref, v_ref, g_ref, cos_ref, sin_ref, gain_ref, st_ref,
             dq_ref, dk_ref, dv_ref, dg_ref, dgain_ref, dstate, dqs, dks):
        i = pl.program_id(0)

        @pl.when(i == 0)
        def _():
            dstate[...] = jnp.zeros_like(dstate)
            dgain_ref[...] = jnp.zeros_like(dgain_ref)

        dec = [_ret_decays(h) for h in range(RET_HEADS)]
        for c in reversed(range(CPS)):
            rows = pl.ds(c * L, L)
            cw = _widen(cos_ref[rows, :], RET_QK_WIDTH)
            sw = _widen(sin_ref[rows, :], RET_QK_WIDTH)
            qv = _rope(q_ref[rows, :], cw, sw)
            kv = _rope(k_ref[rows, :], cw, sw) * (RET_QK_DIM ** -0.5)
            for h in range(RET_HEADS):
                dm, kte, qfs, cd = dec[h]
                qs = slice(h * 64, (h + 1) * 64)
                vs = slice(h * RET_V_DIM, (h + 1) * RET_V_DIM)
                qh = qv[:, qs]
                kh = kv[:, qs]
                vh = v_ref[rows, vs]
                gh = g_ref[rows, vs]
                gn = gain_ref[:, vs]
                o = o_ref[rows, vs]
                dyh = dy_ref[rows, vs]
                sg = _sigmoid(gh)
                silu_g = gh * sg
                r = lax.rsqrt(jnp.mean(o * o, axis=-1, keepdims=True) + EPS)
                n = o * r
                dgain_ref[:, vs] += jnp.sum(dyh * n * silu_g, axis=0, keepdims=True)
                dg_ref[rows, vs] = (dyh * n * gn * _silu_grad(gh, sg)).astype(dg_ref.dtype)
                dn = dyh * gn * silu_g
                do = r * (dn - n * jnp.mean(dn * n, axis=-1, keepdims=True))
                sp = st_ref[c, h]
                ds = dstate[h]
                sc = _bdot(qh, kh, NT) * dm
                dsc = _bdot(do, vh, NT) * dm
                dv_ref[rows, vs] = (_bdot(sc, do, TN) + _bdot(kh * kte, ds, NN)).astype(dv_ref.dtype)
                dqs[:, qs] = _bdot(dsc, kh, NN) + _bdot(do, sp, NT) * qfs
                dks[:, qs] = _bdot(dsc, qh, TN) + _bdot(vh, ds, NT) * kte
                dstate[h] = cd * ds + _bdot(qh * qfs, do, TN)
            dq_ref[rows, :] = _rope_t(dqs[...], cw, sw).astype(dq_ref.dtype)
            dk_ref[rows, :] = _rope_t(dks[...] * (RET_QK_DIM ** -0.5), cw, sw).astype(dk_ref.dtype)

    rev = lambda i: NI - 1 - i
    tab = pl.BlockSpec((T, 128), lambda i: (rev(i), 0))
    wide = pl.BlockSpec((T, RET_V_WIDTH), lambda i: (rev(i), 0))
    narrow = pl.BlockSpec((T, RET_QK_WIDTH), lambda i: (rev(i), 0))
    gvec = pl.BlockSpec((1, RET_V_WIDTH), lambda i: (0, 0))
    return pl.pallas_call(
        body, grid=(NI,),
        in_specs=[pl.BlockSpec((T, RET_V_WIDTH), lambda i: (rev(i), (SSD_WIDTH + ATT_WIDTH) // RET_V_WIDTH)), wide,
                  pl.BlockSpec((T, RET_QK_WIDTH), lambda i: (rev(i), RQ_BLK)), pl.BlockSpec((T, RET_QK_WIDTH), lambda i: (rev(i), RQ_BLK + 1)),
                  pl.BlockSpec((T, RET_V_WIDTH), lambda i: (rev(i), RV_BLK)), pl.BlockSpec((T, RET_V_WIDTH), lambda i: (rev(i), RV_BLK + 1)),
                  tab, tab, gvec,
                  pl.BlockSpec((CPS, RET_HEADS, RET_QK_DIM, RET_V_DIM), lambda i: (rev(i), 0, 0, 0))],
        out_specs=[narrow, narrow, wide, wide, gvec],
        out_shape=[jax.ShapeDtypeStruct((S, RET_QK_WIDTH), BF16), jax.ShapeDtypeStruct((S, RET_QK_WIDTH), BF16),
                   jax.ShapeDtypeStruct((S, RET_V_WIDTH), BF16), jax.ShapeDtypeStruct((S, RET_V_WIDTH), BF16),
                   jax.ShapeDtypeStruct((1, RET_V_WIDTH), F32)],
        scratch_shapes=[pltpu.VMEM((RET_HEADS, RET_QK_DIM, RET_V_DIM), F32), pltpu.VMEM((L, RET_QK_WIDTH), F32),
                        pltpu.VMEM((L, RET_QK_WIDTH), F32)],
        name=name, compiler_params=_params("arbitrary"),
    )(dy, oraw, proj, proj, proj, proj, cos, sin, gain, states)


def _adamw_update(g_ref, nb, w_ref, m_ref, v_ref, go_ref, d_ref, mo_ref, vo_ref):
    g = g_ref[0].astype(F32)
    for k in range(1, nb):
        g = g + g_ref[k].astype(F32)
    mn = ADAM_B1 * m_ref[...] + (1.0 - ADAM_B1) * g
    vn = ADAM_B2 * v_ref[...] + (1.0 - ADAM_B2) * (g * g)
    go_ref[...] = g
    mo_ref[...] = mn
    vo_ref[...] = vn
    c1 = 1.0 - ADAM_B1 ** ADAM_STEP
    c2 = 1.0 - ADAM_B2 ** ADAM_STEP
    d_ref[...] = -ADAM_LR * ((mn / c1) / (jnp.sqrt(vn / c2) + ADAM_EPS) + ADAM_WD * w_ref[...])


def _adamw_rows(R, C):
    return _pick(R, tuple(t for t in (512, 256, 128, 64, 32, 16, 8) if t * C <= 256 * 1024))


def _adamw(gblocks, w, m, v, name):
    nb, R, C = gblocks.shape
    tr = _adamw_rows(R, C)

    def body(g_ref, *refs):
        _adamw_update(g_ref, nb, *refs)

    row = pl.BlockSpec((tr, C), lambda i: (i, 0))
    sh = jax.ShapeDtypeStruct((R, C), F32)
    return pl.pallas_call(
        body, grid=(R // tr,), in_specs=[pl.BlockSpec((nb, tr, C), lambda i: (0, i, 0)), row, row, row],
        out_specs=[row, row, row, row], out_shape=[sh, sh, sh, sh], name=name, compiler_params=_params("parallel"),
    )(gblocks, w, m, v)


def _adamw_layers(g0, g1, w, m, v, name):
    nb, R, C = g0.shape
    tr = _adamw_rows(R, C)

    def body(g0_ref, g1_ref, *refs):
        l = pl.program_id(0)

        @pl.when(l == 0)
        def _():
            _adamw_update(g0_ref, nb, *refs)

        @pl.when(l == 1)
        def _():
            _adamw_update(g1_ref, nb, *refs)

    row = pl.BlockSpec((None, tr, C), lambda l, i: (l, i, 0))
    sh = jax.ShapeDtypeStruct((DEPTH, R, C), F32)
    return pl.pallas_call(
        body, grid=(DEPTH, R // tr),
        in_specs=[pl.BlockSpec((nb, tr, C), lambda l, i: (0, i * (1 - l), 0)), pl.BlockSpec((nb, tr, C), lambda l, i: (0, i * l, 0)),
                  row, row, row],
        out_specs=[row, row, row, row], out_shape=[sh, sh, sh, sh], name=name, compiler_params=_params("arbitrary", "arbitrary"),
    )(g0, g1, w, m, v)


def _peers():
    x, y, c = lax.axis_index("x"), lax.axis_index("y"), lax.axis_index("c")
    flips = ((0, 0, 1), (1, 0, 0), (0, 1, 0), (1, 1, 0), (1, 0, 1), (0, 1, 1), (1, 1, 1))
    me = 4 * x + 2 * y + c
    peers = [(x ^ fx, y ^ fy, c ^ fc) for fx, fy, fc in flips]
    return me, peers


def _exchange(arrs, scatter, name):
    n = len(arrs)
    npeer = N_DEV - 1

    def body(*refs):
        ins, outs = refs[:n], refs[n:2 * n]
        send_sems, recv_sems, local_sems = refs[2 * n:]
        me, peers = _peers()
        copies = []
        for a in range(n):
            src_own = ins[a].at[me] if scatter else ins[a]
            own = pltpu.make_async_copy(src_own, outs[a].at[me], local_sems.at[a])
            own.start()
            copies.append(own)
            for k, peer in enumerate(peers):
                src = ins[a].at[4 * peer[0] + 2 * peer[1] + peer[2]] if scatter else ins[a]
                cp = pltpu.make_async_remote_copy(
                    src_ref=src, dst_ref=outs[a].at[me], send_sem=send_sems.at[a * npeer + k],
                    recv_sem=recv_sems.at[a * npeer + k], device_id=peer, device_id_type=pl.DeviceIdType.MESH)
                cp.start()
                copies.append(cp)
        for cp in copies:
            cp.wait()

    out_shape = [jax.ShapeDtypeStruct(((N_DEV,) + a.shape[1:]) if scatter else ((N_DEV,) + a.shape), a.dtype) for a in arrs]
    anyspec = pl.BlockSpec(memory_space=pl.ANY)
    return pl.pallas_call(
        body, in_specs=[anyspec] * n, out_specs=[anyspec] * n, out_shape=out_shape,
        scratch_shapes=[pltpu.SemaphoreType.DMA((n * npeer,)), pltpu.SemaphoreType.DMA((n * npeer,)),
                        pltpu.SemaphoreType.DMA((n,))],
        name=name,
    )(*arrs)


def _dev_index(peer):
    return 4 * peer[0] + 2 * peer[1] + peer[2]


def _push_copies(src_refs, land_refs, send_sems, recv_sems, scatter, as_receiver):
    me, peers = _peers()
    npeer = N_DEV - 1
    copies = []
    for a in range(len(src_refs)):
        for k, peer in enumerate(peers):
            src = src_refs[a].at[_dev_index(peer)] if scatter else src_refs[a]
            slot = _dev_index(peer) if as_receiver else me
            copies.append(pltpu.make_async_remote_copy(
                src_ref=src, dst_ref=land_refs[a].at[slot], send_sem=send_sems.at[a * npeer + k],
                recv_sem=recv_sems.at[a * npeer + k], device_id=peer, device_id_type=pl.DeviceIdType.MESH))
    return copies


def _push_start(srcs, lands, scatter, name):
    n = len(srcs)
    nsem = n * (N_DEV - 1)

    def body(*refs):
        for cp in _push_copies(refs[:n], refs[n:2 * n], refs[2 * n], refs[2 * n + 1], scatter, False):
            cp.start()
        token = refs[-1]
        token[...] = jnp.zeros_like(token)

    hbm = pl.BlockSpec(memory_space=pltpu.HBM)
    sem = pl.BlockSpec(memory_space=pltpu.SEMAPHORE)
    arrs = list(srcs) + list(lands)
    return pl.pallas_call(
        body, name=name,
        out_shape=(pltpu.SemaphoreType.DMA((nsem,)), pltpu.SemaphoreType.DMA((nsem,)),
                   *[pltpu.HBM(a.shape, a.dtype) for a in arrs], jax.ShapeDtypeStruct((8, 128), F32)),
        in_specs=[hbm] * (2 * n), out_specs=(sem, sem, *([hbm] * (2 * n)), pl.BlockSpec(memory_space=pltpu.VMEM)),
        input_output_aliases={i: 2 + i for i in range(2 * n)},
        compiler_params=pltpu.CompilerParams(has_side_effects=pltpu.SideEffectType.DATAFLOW_SIDE_EFFECTING),
    )(*[pltpu.with_memory_space_constraint(a, pltpu.HBM) for a in arrs])


def _push_wait(handle, after, scatter, name):
    send_sems, recv_sems, *thru, _ = handle
    n = len(thru) // 2

    def body(*refs):
        for cp in _push_copies(refs[:n], refs[n:2 * n], refs[2 * n], refs[2 * n + 1], scatter, True):
            cp.wait_send()
            cp.wait_recv()

    hbm = pl.BlockSpec(memory_space=pltpu.HBM)
    sem = pl.BlockSpec(memory_space=pltpu.SEMAPHORE)
    outs = pl.pallas_call(
        body, name=name, out_shape=tuple(pltpu.HBM(a.shape, a.dtype) for a in thru),
        in_specs=[hbm] * (2 * n) + [sem, sem, pl.BlockSpec(memory_space=pl.ANY)], out_specs=tuple([hbm] * (2 * n)),
        input_output_aliases={i: i for i in range(2 * n)},
        compiler_params=pltpu.CompilerParams(has_side_effects=pltpu.SideEffectType.DATAFLOW_SIDE_EFFECTING),
    )(*thru, send_sems, recv_sems, after)
    return list(outs[n:])


def _landing(own, me):
    return lax.dynamic_update_index_in_dim(lax.empty((N_DEV,) + own.shape, own.dtype), own, me, 0)


def _tables(S):
    pos = jnp.arange(S, dtype=F32)
    inv = ROPE_THETA ** (-jnp.arange(0, ATT_HEAD_DIM, 2, dtype=F32) / ATT_HEAD_DIM)
    ang = pos[:, None] * inv[None, :]
    cos, sin = jnp.cos(ang), jnp.sin(ang)
    cos128 = jnp.tile(cos, (1, 4))
    sin128 = jnp.tile(jnp.concatenate([-sin, sin], axis=1), (1, 2))
    lane = np.arange(ATT_WIDTH)
    bd = jnp.asarray((lane[:, None] // 64 == lane[None, :] // 64).astype(np.float32), dtype=BF16)
    return cos128, sin128, bd, jnp.asarray(_att_bias())


def _permute_in(w):
    pad = jnp.zeros((w.shape[0], IN_PAD - IN_WIDTH), w.dtype)
    return jnp.concatenate([w[:, :2560], w[:, 2576:], w[:, 2560:2576], pad], axis=1)


def _unpermute_in(g):
    return jnp.concatenate([g[:, :2560], g[:, DT_COL:DT_COL + SSD_HEADS], g[:, 2560:DT_COL]], axis=1)


def _layer_fwd(l, x, p, tabs, late=None):
    cos, sin, bd, bias = tabs
    S = x.shape[0]
    row = lambda v: v.reshape(1, -1)
    hn = _rmsnorm_fwd(x, row(p["ln_mix"]), f"norm_mix_fwd{l}")
    proj = _mm(hn, p["w_in"], "nn", f"in_proj{l}", tn=1152)
    xbc = _conv_fwd(proj, p["conv_w"], row(p["conv_b"]), f"conv_fwd{l}")
    dtr = proj[:, DT_COL:DT_COL + SSD_HEADS].reshape(S, SSD_GROUPS, HPG).transpose(1, 0, 2)
    grp = lambda v: v.reshape(SSD_GROUPS, 1, HPG)
    d_rep = row(jnp.repeat(p["d_skip"], SSD_HEAD_DIM))
    y, yraw, ssd_st = _ssd_fwd(xbc, proj, dtr, grp(p["dt_bias"]), grp(p["a_log"]), d_rep, row(p["ssd_norm"]), f"ssd_fwd{l}")
    qg = row(jnp.tile(p["q_norm"], ATT_HEADS))
    kg = row(jnp.tile(p["k_norm"], ATT_HEADS))
    aq, akp, avp = _att_prep_fwd(proj, qg, kg, cos, sin, bd, f"att_prep_fwd{l}")
    y = _att_fwd(aq, akp, avp, bias, y, f"att_fwd{l}")
    y, oraw, ret_st = _ret_fwd(proj, cos, sin, row(p["ret_norm"]), y, f"ret_fwd{l}")
    if late is not None:
        p.update(late(y))
    x1 = _mm(y, p["w_out"], "nn", f"out_proj{l}", residual=x)
    hn2 = _rmsnorm_fwd(x1, row(p["ln_ffn"]), f"norm_ffn_fwd{l}")
    g, u, act = _swiglu_fwd(hn2, p["w_gate"], p["w_up"], f"swiglu_fwd{l}")
    x2 = _down_proj(act, p["w_down"], x1, f"down_proj{l}")
    saved = dict(x=x, hn=hn, proj=proj, xbc=xbc, dtr=dtr, yraw=yraw, ssd_st=ssd_st, aq=aq, akp=akp, avp=avp,
                 oraw=oraw, ret_st=ret_st, y=y, x1=x1, hn2=hn2, g=g, u=u, act=act, d_rep=d_rep, qg=qg, kg=kg)
    return x2, saved


def _layer_bwd(l, dx2, dx2_bf, p, sv, tabs, on_ffn=None, on_all=None):
    cos, sin, bd, bias = tabs
    S = dx2.shape[0]
    row = lambda v: v.reshape(1, -1)
    grp = lambda v: v.reshape(SSD_GROUPS, 1, HPG)
    gr = {}
    dg, du = _swiglu_bwd(dx2_bf, p["w_down"], sv["g"], sv["u"], f"swiglu_bwd{l}")
    gr["w_down"] = _down_wgrad(sv["act"], dx2_bf, f"down_wgrad{l}")
    dhn2 = _ffn_dgrad(dg, p["w_gate"], du, p["w_up"], f"ffn_dgrad{l}")
    gr["w_gate"], gr["w_up"] = _ffn_in_wgrad(sv["hn2"], dg, du, f"ffn_in_wgrad{l}")
    ffn_gain = row(p["ln_ffn"]) + (on_ffn(gr)[0, 0] if on_ffn is not None else 0.0)
    dx1, dx1_bf, dln_ffn = _rmsnorm_bwd(sv["x1"], dhn2, ffn_gain, dx2, f"norm_ffn_bwd{l}")
    gr["ln_ffn"] = dln_ffn[0]
    dy = _mm(dx1_bf, p["w_out"], "nt", f"out_dgrad{l}")
    gr["w_out"] = _mm(sv["y"], dx1_bf, "tn", f"out_wgrad{l}", out_dtype=BF16, tm=1024, tn=1024, tk=1024)
    dz, dxs, dbm, dcm, ddtr, dbias, dalog, dd, dssd_gain = _ssd_bwd(
        dy, sv["yraw"], sv["xbc"], sv["proj"], sv["dtr"], grp(p["dt_bias"]), grp(p["a_log"]), sv["d_rep"],
        row(p["ssd_norm"]), sv["ssd_st"], f"ssd_bwd{l}")
    gr["dt_bias"], gr["a_log"], gr["d_skip"] = dbias.reshape(-1), dalog.reshape(-1), dd.reshape(-1)
    gr["ssd_norm"] = dssd_gain[0]
    dxbc, dconv_w, dconv_b = _conv_bwd(dxs, dbm, dcm, sv["proj"], p["conv_w"], row(p["conv_b"]), f"conv_bwd{l}")
    gr["conv_w"], gr["conv_b"] = dconv_w, dconv_b[0]
    dq_h, dk_h, dv_h = _att_bwd(sv["aq"], sv["akp"], sv["avp"], bias, dy, f"att_bwd{l}")
    daq, dak, dav, dqg, dkg = _att_prep_bwd(sv["proj"], dq_h, dk_h, dv_h, sv["qg"], sv["kg"], cos, sin, bd, f"att_prep_bwd{l}")
    gr["q_norm"], gr["k_norm"] = dqg[0], dkg[0]
    drq, drk, drv, drg, dret_gain = _ret_bwd(dy, sv["oraw"], sv["proj"], cos, sin, row(p["ret_norm"]), sv["ret_st"], f"ret_bwd{l}")
    gr["ret_norm"] = dret_gain[0]
    ddt_cols = ddtr.transpose(1, 0, 2).reshape(S, SSD_HEADS).astype(BF16)
    dproj = jnp.concatenate([dz, dxbc, daq, dak, dav, drq, drk, drv, drg, ddt_cols,
                             jnp.zeros((S, IN_PAD - IN_WIDTH), BF16)], axis=1)
    gr["w_in"] = _mm(sv["hn"], dproj, "tn", f"in_wgrad{l}", out_dtype=BF16, tm=1024, tn=1152, tk=1024)
    launched = on_all(gr) if on_all is not None else None
    dhn = _mm(dproj, p["w_in"], "nt", f"in_dgrad{l}", tk=1920, after=launched)
    dx0, dx0_bf, dln_mix = _rmsnorm_bwd(sv["x"], dhn, row(p["ln_mix"]), dx1, f"norm_mix_bwd{l}")
    gr["ln_mix"] = dln_mix[0]
    return dx0, dx0_bf, gr


def _local_step(x, tgt, layers, late=None, on_ffn=None, on_all=None):
    n = len(layers)
    none = [None] * n
    late, on_ffn, on_all = late or none, on_ffn or none, on_all or none
    tabs = _tables(x.shape[0])
    saved, params = [], []
    h = x
    for l in range(n):
        p = dict(layers[l](h) if callable(layers[l]) else layers[l])
        h, sv = _layer_fwd(l, h, p, tabs, late[l])
        saved.append(sv)
        params.append(p)
    dh, dh_bf, lacc = _loss_grad(h, tgt, "loss_grad")
    grads = [None] * n
    for l in reversed(range(n)):
        dh, dh_bf, grads[l] = _layer_bwd(l, dh, dh_bf, params[l], saved[l], tabs, on_ffn[l], on_all[l])
    return lacc[0, 0], dh, grads


BIG = ("w_in", "w_out", "w_gate", "w_up", "w_down")
SMALL = ("ln_mix", "conv_b", "dt_bias", "a_log", "d_skip", "ssd_norm", "q_norm", "k_norm", "ret_norm", "ln_ffn")
ORDER = ("ln_mix", "w_in", "conv_w", "conv_b", "dt_bias", "a_log", "d_skip", "ssd_norm", "q_norm", "k_norm", "ret_norm",
         "w_out", "ln_ffn", "w_gate", "w_up", "w_down")


FFN_WEIGHTS = ("w_gate", "w_up", "w_down")


def _full_weight(k, gathered):
    if k in FFN_WEIGHTS:
        return gathered
    if k == "w_out":
        return gathered.reshape(-1, gathered.shape[2])
    full = gathered.transpose(1, 0, 2).reshape(gathered.shape[1], -1)
    return _permute_in(full) if k == "w_in" else full


def _shard_block(k, g):
    if k in FFN_WEIGHTS:
        return g
    if k == "w_out":
        return g.reshape(N_DEV, -1, g.shape[1])
    g = _unpermute_in(g)
    return g.reshape(g.shape[0], N_DEV, -1).transpose(1, 0, 2)


def kernel(x, ln_mix, w_in, conv_w, conv_b, dt_bias, a_log, d_skip, ssd_norm, q_norm, k_norm, ret_norm, w_out, ln_ffn, w_gate, w_up, w_down, loss_target, m_ln_mix, m_w_in, m_conv_w, m_conv_b, m_dt_bias, m_a_log, m_d_skip, m_ssd_norm, m_q_norm, m_k_norm, m_ret_norm, m_w_out, m_ln_ffn, m_w_gate, m_w_up, m_w_down, v_ln_mix, v_w_in, v_conv_w, v_conv_b, v_dt_bias, v_a_log, v_d_skip, v_ssd_norm, v_q_norm, v_k_norm, v_ret_norm, v_w_out, v_ln_ffn, v_w_gate, v_w_up, v_w_down):
    w = dict(ln_mix=ln_mix, w_in=w_in, conv_w=conv_w, conv_b=conv_b, dt_bias=dt_bias, a_log=a_log, d_skip=d_skip,
             ssd_norm=ssd_norm, q_norm=q_norm, k_norm=k_norm, ret_norm=ret_norm, w_out=w_out, ln_ffn=ln_ffn,
             w_gate=w_gate, w_up=w_up, w_down=w_down)
    m = dict(ln_mix=m_ln_mix, w_in=m_w_in, conv_w=m_conv_w, conv_b=m_conv_b, dt_bias=m_dt_bias, a_log=m_a_log,
             d_skip=m_d_skip, ssd_norm=m_ssd_norm, q_norm=m_q_norm, k_norm=m_k_norm, ret_norm=m_ret_norm, w_out=m_w_out,
             ln_ffn=m_ln_ffn, w_gate=m_w_gate, w_up=m_w_up, w_down=m_w_down)
    v = dict(ln_mix=v_ln_mix, w_in=v_w_in, conv_w=v_conv_w, conv_b=v_conv_b, dt_bias=v_dt_bias, a_log=v_a_log,
             d_skip=v_d_skip, ssd_norm=v_ssd_norm, q_norm=v_q_norm, k_norm=v_k_norm, ret_norm=v_ret_norm, w_out=v_w_out,
             ln_ffn=v_ln_ffn, w_gate=v_w_gate, w_up=v_w_up, w_down=v_w_down)
    me = 4 * lax.axis_index("x") + 2 * lax.axis_index("y") + lax.axis_index("c")

    waves = {"a": [("w_in", 0), ("conv_w", 0), ("conv_w", 1)],
             "b": [(k, 0) for k in ("w_out", "w_gate", "w_up", "w_down")],
             "c": [(k, 1) for k in BIG]}
    gather = {}
    for tag, items in waves.items():
        srcs = [w[k][l] if k == "conv_w" else w[k][l].astype(BF16) for k, l in items]
        gather[tag] = _push_start(srcs, [_landing(s, me) for s in srcs], False, f"gather_{tag}_start")
    started = gather["a"][-1][0, 0] + gather["b"][-1][0, 0] + gather["c"][-1][0, 0]
    full = {}

    def arrive(tag, after):
        for (k, l), g in zip(waves[tag], _push_wait(gather[tag], after, False, f"gather_{tag}_wait")):
            full[k, l] = _full_weight(k, g)

    def layer_weights(l, names):
        return {k: full[k, l] for k in names}

    def small_weights(l):
        return {k: w[k][l] for k in SMALL}

    def layer0(h):
        arrive("a", h)
        p = small_weights(0)
        p["ln_mix"] = p["ln_mix"] + started
        return {**p, **layer_weights(0, ("w_in", "conv_w"))}

    def late0(y):
        arrive("b", y)
        return layer_weights(0, waves_b_names)

    def layer1(h):
        arrive("c", h)
        return {**small_weights(1), **layer_weights(1, BIG + ("conv_w",))}

    waves_b_names = tuple(k for k, _ in waves["b"])

    groups = {"1": [(k, 1) for k in BIG], "0a": [(k, 0) for k in ("w_down", "w_gate", "w_up")],
              "0b": [(k, 0) for k in ("w_out", "w_in")]}
    scatter = {}

    def push_grads(tag, gr):
        blocks = [_shard_block(k, gr[k]) for k, _ in groups[tag]]
        lands = [_landing(lax.dynamic_index_in_dim(b, me, 0, keepdims=False), me) for b in blocks]
        scatter[tag] = _push_start(blocks, lands, True, f"scatter_{tag}_start")
        return scatter[tag][-1]

    loss_part, gx, grads = _local_step(
        x[0], loss_target[0], [layer0, layer1], late=[late0, None],
        on_ffn=[functools.partial(push_grads, "0a"), None],
        on_all=[functools.partial(push_grads, "0b"), functools.partial(push_grads, "1")])
    loss = lax.psum(loss_part, MESH_AXES)

    out = {}
    recv = {}
    for tag, items in groups.items():
        for item, r in zip(items, _push_wait(scatter[tag], gx, True, f"scatter_{tag}_wait")):
            recv[item] = r
    for k in BIG:
        out[k] = _adamw_layers(recv[k, 0], recv[k, 1], w[k], m[k], v[k], f"adamw_{k}")
    names = SMALL + ("conv_w",)
    sizes = [int(np.prod(grads[0][k].shape)) for k in names]
    packed = jnp.concatenate([jnp.stack([grads[l][k] for l in range(DEPTH)]).reshape(-1) for k in names])
    n_small = packed.shape[0]
    rows_small = -(-n_small // 1024) * 8
    pad = lambda t, fill: jnp.concatenate([t, jnp.full((rows_small * 128 - n_small,), fill, F32)]).reshape(rows_small, 128)
    parts = _exchange([pad(packed, 0.0)], False, "gather_small_grads")[0]
    n_rep = DEPTH * sum(sizes[:-1])
    pack_rep = lambda d, fill: pad(jnp.concatenate([d[k].reshape(-1) for k in SMALL]
                                                   + [jnp.full((n_small - n_rep,), fill, F32)]), fill)
    res = _adamw(parts, pack_rep(w, 1.0), pack_rep(m, 1.0), pack_rep(v, 1.0), "adamw_small")
    res = [t.reshape(-1) for t in res]
    off = 0
    for k, sz in zip(SMALL, sizes[:-1]):
        out[k] = [t[off:off + DEPTH * sz].reshape(w[k].shape) for t in res]
        off += DEPTH * sz
    gconv = res[0][off:off + DEPTH * sizes[-1]].reshape(DEPTH, SSD_CONV, SSD_CONV_CH)
    gconv = lax.dynamic_slice_in_dim(gconv, me * conv_w.shape[2], conv_w.shape[2], axis=2)
    flat = lambda t: t.reshape(8, -1)
    resc = _adamw(flat(gconv)[None], flat(conv_w), flat(m_conv_w), flat(v_conv_w), "adamw_conv_w")
    out["conv_w"] = [t.reshape(conv_w.shape) for t in resc]

    return (loss, gx[None], *[out[k][0] for k in ORDER], *[out[k][1] for k in ORDER],
            *[out[k][2] for k in ORDER], *[out[k][3] for k in ORDER])
```

```python
import functools
import math

import jax
import jax.numpy as jnp
import numpy as np
from jax import lax
from jax.experimental import pallas as pl
from jax.experimental.pallas import tpu as pltpu

F32 = jnp.float32
BF16 = jnp.bfloat16

N_DEV = 8
MESH_AXES = ("x", "y", "c")
D_MODEL = 2048
DEPTH = 2
EPS = 1e-6
ROPE_THETA = 10000.0
SSD_HEADS = 16
SSD_HEAD_DIM = 64
SSD_WIDTH = 1024
SSD_GROUPS = 2
SSD_STATE = 128
SSD_CONV = 4
SSD_CONV_CH = 1536
ATT_HEADS = 8
ATT_HEAD_DIM = 64
ATT_WIDTH = 512
DILATED_PAIRS = ((128, 1), (512, 4), (2048, 16))
RET_HEADS = 4
RET_QK_DIM = 64
RET_V_DIM = 128
RET_QK_WIDTH = 256
RET_V_WIDTH = 512
CHUNK = 128
MIX_WIDTH = 2048
ATT_SPAN = 2048
ATT_STRIP = ATT_SPAN + CHUNK
IN_WIDTH = 5648
IN_PAD = 5760
DT_COL = 5632
D_FF = 5632
ADAM_LR = 0.001
ADAM_B1 = 0.9
ADAM_B2 = 0.999
ADAM_EPS = 1e-08
ADAM_WD = 0.01
ADAM_STEP = 10
NEG = -1e30
VMEM_LIMIT_V7X = 60 * 1024 * 1024

NN = (((1,), (0,)), ((), ()))
NT = (((1,), (1,)), ((), ()))
TN = (((0,), (0,)), ((), ()))


def _bdot(a, b, dims):
    return lax.dot_general(a.astype(BF16), b.astype(BF16), dims, preferred_element_type=F32)


def _xdot(a, b, dims):
    return lax.dot_general(a, b, dims, precision=lax.Precision.HIGHEST, preferred_element_type=F32)


def _params(*sem):
    return pltpu.CompilerParams(dimension_semantics=sem, vmem_limit_bytes=VMEM_LIMIT_V7X)


def _sigmoid(v):
    return 1.0 / (1.0 + jnp.exp(-v))


def _silu_grad(v, s):
    return s * (1.0 + v * (1.0 - s))


def _rmsnorm_fwd(x, g, name):
    S, D = x.shape
    tr = min(512, S)

    def body(x_ref, g_ref, o_ref):
        xv = x_ref[...]
        r = lax.rsqrt(jnp.mean(xv * xv, axis=-1, keepdims=True) + EPS)
        o_ref[...] = (xv * r * g_ref[...]).astype(o_ref.dtype)

    return pl.pallas_call(
        body, grid=(S // tr,),
        in_specs=[pl.BlockSpec((tr, D), lambda i: (i, 0)), pl.BlockSpec((1, D), lambda i: (0, 0))],
        out_specs=pl.BlockSpec((tr, D), lambda i: (i, 0)),
        out_shape=jax.ShapeDtypeStruct((S, D), BF16), name=name, compiler_params=_params("parallel"),
    )(x, g)


def _rmsnorm_bwd(x, dy, g, dres, name):
    S, D = x.shape
    tr = min(512, S)

    def body(x_ref, dy_ref, g_ref, dres_ref, dx_ref, dxb_ref, dg_ref):
        i = pl.program_id(0)
        xv = x_ref[...]
        r = lax.rsqrt(jnp.mean(xv * xv, axis=-1, keepdims=True) + EPS)
        n = xv * r
        dy = dy_ref[...]
        dn = dy * g_ref[...]
        dx = dres_ref[...] + r * (dn - n * jnp.mean(dn * n, axis=-1, keepdims=True))
        dx_ref[...] = dx
        dxb_ref[...] = dx.astype(BF16)
        part = jnp.sum(dy * n, axis=0, keepdims=True)

        @pl.when(i == 0)
        def _():
            dg_ref[...] = part

        @pl.when(i > 0)
        def _():
            dg_ref[...] += part

    row = pl.BlockSpec((tr, D), lambda i: (i, 0))
    vec = pl.BlockSpec((1, D), lambda i: (0, 0))
    return pl.pallas_call(
        body, grid=(S // tr,), in_specs=[row, row, vec, row], out_specs=[row, row, vec],
        out_shape=[jax.ShapeDtypeStruct((S, D), F32), jax.ShapeDtypeStruct((S, D), BF16), jax.ShapeDtypeStruct((1, D), F32)],
        name=name, compiler_params=_params("arbitrary"),
    )(x, dy, g, dres)


def _loss_grad(y, tgt, name):
    S, D = y.shape
    tr = min(512, S)

    def body(y_ref, t_ref, dy_ref, dyb_ref, l_ref):
        i = pl.program_id(0)
        err = y_ref[...] - t_ref[...]
        dy = err * (1.0 / D)
        dy_ref[...] = dy
        dyb_ref[...] = dy.astype(BF16)
        part = jnp.sum(jnp.sum(err * err, axis=1, keepdims=True), axis=0, keepdims=True) * (0.5 / D)

        @pl.when(i == 0)
        def _():
            l_ref[...] = jnp.zeros_like(l_ref)

        l_ref[...] += part

    row = pl.BlockSpec((tr, D), lambda i: (i, 0))
    return pl.pallas_call(
        body, grid=(S // tr,), in_specs=[row, row],
        out_specs=[row, row, pl.BlockSpec((8, 128), lambda i: (0, 0))],
        out_shape=[jax.ShapeDtypeStruct((S, D), F32), jax.ShapeDtypeStruct((S, D), BF16), jax.ShapeDtypeStruct((8, 128), F32)],
        name=name, compiler_params=_params("arbitrary"),
    )(y, tgt)


def _pick(n, cands):
    for c in cands:
        if n % c == 0:
            return c
    return n


def _mm(a, b, mode, name, out_dtype=F32, residual=None, tm=None, tn=None, tk=None, after=None):
    if mode == "nn":
        (M, K), (_, N) = a.shape, b.shape
    elif mode == "nt":
        (M, K), (N, _) = a.shape, b.shape
    else:
        (K, M), (_, N) = a.shape, b.shape
    tm = min(tm, M) if tm else _pick(M, (1024, 512, 256, 128))
    tn = min(tn, N) if tn else _pick(N, (1024, 1152, 1408, 512, 256, 128))
    tk = min(tk, K) if tk else _pick(K, (2048, 1920, 1408, 1024, 512, 256, 128))
    assert M % tm == 0 and N % tn == 0 and K % tk == 0, (name, M, N, K, tm, tn, tk)
    nk = K // tk
    a_spec = pl.BlockSpec((tk, tm), lambda i, j, k: (k, i)) if mode == "tn" else pl.BlockSpec((tm, tk), lambda i, j, k: (i, k))
    b_spec = pl.BlockSpec((tn, tk), lambda i, j, k: (j, k)) if mode == "nt" else pl.BlockSpec((tk, tn), lambda i, j, k: (k, j))
    o_spec = pl.BlockSpec((tm, tn), lambda i, j, k: (i, j))
    dims = {"nn": NN, "nt": NT, "tn": TN}[mode]
    has_res = residual is not None

    has_after = after is not None

    def body(*refs):
        a_ref, b_ref = refs[0], refs[1]
        r_ref = refs[2] if has_res else None
        o_ref = refs[2 + has_res + has_after]
        p = _bdot(a_ref[...], b_ref[...], dims)

        def finish(acc):
            if has_res:
                acc = acc + r_ref[...]
            o_ref[...] = acc.astype(o_ref.dtype)

        if nk == 1:
            finish(p)
        else:
            acc_ref = refs[-1]
            k = pl.program_id(2)

            @pl.when(k == 0)
            def _():
                acc_ref[...] = p

            @pl.when(k > 0)
            def _():
                acc_ref[...] += p

            @pl.when(k == nk - 1)
            def _():
                finish(acc_ref[...])

    ins = [a, b] + ([residual] if has_res else []) + ([after] if has_after else [])
    in_specs = [a_spec, b_spec] + ([o_spec] if has_res else []) + ([pl.BlockSpec(memory_space=pl.ANY)] if has_after else [])
    scratch = [pltpu.VMEM((tm, tn), F32)] if nk > 1 else []
    return pl.pallas_call(
        body, grid=(M // tm, N // tn, nk), in_specs=in_specs, out_specs=o_spec,
        out_shape=jax.ShapeDtypeStruct((M, N), out_dtype), scratch_shapes=scratch, name=name,
        compiler_params=_params("parallel", "parallel", "arbitrary"),
    )(*ins)


def _accumulate(acc_ref, p, k, nk, finish):
    @pl.when(k == 0)
    def _():
        acc_ref[...] = p

    @pl.when(k > 0)
    def _():
        acc_ref[...] += p

    @pl.when(k == nk - 1)
    def _():
        finish(acc_ref[...])


def _swiglu_fwd(hn, wg, wu, name):
    S, K = hn.shape
    F = wg.shape[1]
    tm = _pick(S, (1024, 512))
    tn = _pick(F, (512, 256, 128))

    def body(a_ref, wg_ref, wu_ref, g_ref, u_ref, act_ref):
        a = a_ref[...]
        g = _bdot(a, wg_ref[...], NN)
        u = _bdot(a, wu_ref[...], NN)
        g_ref[...] = g.astype(BF16)
        u_ref[...] = u.astype(BF16)
        act_ref[...] = (g * _sigmoid(g) * u).astype(BF16)

    w_spec = pl.BlockSpec((K, tn), lambda i, j: (0, j))
    o_spec = pl.BlockSpec((tm, tn), lambda i, j: (i, j))
    sh = jax.ShapeDtypeStruct((S, F), BF16)
    return pl.pallas_call(
        body, grid=(S // tm, F // tn), in_specs=[pl.BlockSpec((tm, K), lambda i, j: (i, 0)), w_spec, w_spec],
        out_specs=[o_spec, o_spec, o_spec], out_shape=[sh, sh, sh], name=name,
        compiler_params=_params("parallel", "parallel"),
    )(hn, wg, wu)


def _swiglu_bwd(dx, wd, g, u, name):
    S, K = dx.shape
    F = wd.shape[0]
    tm = _pick(S, (1024, 512))
    tn = _pick(F, (512, 256, 128))

    def body(dx_ref, wd_ref, g_ref, u_ref, dg_ref, du_ref):
        da = _bdot(dx_ref[...], wd_ref[...], NT)
        gv = g_ref[...].astype(F32)
        uv = u_ref[...].astype(F32)
        s = _sigmoid(gv)
        dg_ref[...] = (da * uv * _silu_grad(gv, s)).astype(BF16)
        du_ref[...] = (da * gv * s).astype(BF16)

    o_spec = pl.BlockSpec((tm, tn), lambda i, j: (i, j))
    sh = jax.ShapeDtypeStruct((S, F), BF16)
    return pl.pallas_call(
        body, grid=(S // tm, F // tn),
        in_specs=[pl.BlockSpec((tm, K), lambda i, j: (i, 0)), pl.BlockSpec((tn, K), lambda i, j: (j, 0)), o_spec, o_spec],
        out_specs=[o_spec, o_spec], out_shape=[sh, sh], name=name, compiler_params=_params("parallel", "parallel"),
    )(dx, wd, g, u)


def _mm_nt2(a1, b1, a2, b2, name):
    M, K = a1.shape
    N = b1.shape[0]
    tm = _pick(M, (1024, 512))
    tn = _pick(N, (1024, 512))
    tk = _pick(K, (1408, 1024, 512, 256, 128))
    nk = K // tk

    def body(a1_ref, b1_ref, a2_ref, b2_ref, o_ref, acc_ref):
        def finish(acc):
            o_ref[...] = acc

        p = _bdot(a1_ref[...], b1_ref[...], NT) + _bdot(a2_ref[...], b2_ref[...], NT)
        _accumulate(acc_ref, p, pl.program_id(2), nk, finish)

    a_spec = pl.BlockSpec((tm, tk), lambda i, j, k: (i, k))
    b_spec = pl.BlockSpec((tn, tk), lambda i, j, k: (j, k))
    return pl.pallas_call(
        body, grid=(M // tm, N // tn, nk), in_specs=[a_spec, b_spec, a_spec, b_spec],
        out_specs=pl.BlockSpec((tm, tn), lambda i, j, k: (i, j)), out_shape=jax.ShapeDtypeStruct((M, N), F32),
        scratch_shapes=[pltpu.VMEM((tm, tn), F32)], name=name,
        compiler_params=_params("parallel", "parallel", "arbitrary"),
    )(a1, b1, a2, b2)


XBC_BLK0 = SSD_WIDTH // 128


def _conv_fwd(proj, w, b, name):
    S = proj.shape[0]
    T = min(512, S)

    def body(x_ref, w_ref, b_ref, o_ref, xp_ref):
        xp_ref[pl.ds(0, 8), :] = jnp.zeros((8, 128), F32)
        xp_ref[pl.ds(8, S), :] = x_ref[...]
        wv = w_ref[...]
        bv = b_ref[...]

        def step(c, carry):
            base = pl.multiple_of(c * T, T)
            acc = wv[0:1] * xp_ref[pl.ds(base + 5, T), :]
            for i in range(1, SSD_CONV):
                acc = acc + wv[i:i + 1] * xp_ref[pl.ds(base + 5 + i, T), :]
            acc = bv + acc
            o_ref[pl.ds(base, T), :] = acc * _sigmoid(acc)
            return carry

        lax.fori_loop(0, S // T, step, 0)

    return pl.pallas_call(
        body, grid=(SSD_CONV_CH // 128,),
        in_specs=[pl.BlockSpec((S, 128), lambda j: (0, XBC_BLK0 + j)), pl.BlockSpec((SSD_CONV, 128), lambda j: (0, j)),
                  pl.BlockSpec((1, 128), lambda j: (0, j))],
        out_specs=pl.BlockSpec((S, 128), lambda j: (0, j)),
        out_shape=jax.ShapeDtypeStruct((S, SSD_CONV_CH), F32),
        scratch_shapes=[pltpu.VMEM((S + 8, 128), F32)], name=name, compiler_params=_params("parallel"),
    )(proj, w, b)


def _conv_bwd(dxs, dbm, dcm, proj, w, b, name):
    S = proj.shape[0]
    T = min(512, S)
    NX, NB = SSD_WIDTH // 128, SSD_GROUPS * SSD_STATE // 128

    def body(dxs_ref, dbm_ref, dcm_ref, x_ref, w_ref, b_ref, dx_ref, dw_ref, db_ref, xp_ref, dcp_ref):
        j = pl.program_id(0)

        @pl.when(j < NX)
        def _():
            dcp_ref[pl.ds(0, S), :] = dxs_ref[...]

        @pl.when((j >= NX) & (j < NX + NB))
        def _():
            dcp_ref[pl.ds(0, S), :] = dbm_ref[...]

        @pl.when(j >= NX + NB)
        def _():
            dcp_ref[pl.ds(0, S), :] = dcm_ref[...]

        da_ref = dcp_ref
        xp_ref[pl.ds(0, 8), :] = jnp.zeros((8, 128), F32)
        xp_ref[pl.ds(8, S), :] = x_ref[...]
        dcp_ref[pl.ds(S, 8), :] = jnp.zeros((8, 128), F32)
        wv = w_ref[...]
        bv = b_ref[...]

        def step1(c, carry):
            base = pl.multiple_of(c * T, T)
            xs = [xp_ref[pl.ds(base + 5 + i, T), :] for i in range(SSD_CONV)]
            acc = wv[0:1] * xs[0]
            for i in range(1, SSD_CONV):
                acc = acc + wv[i:i + 1] * xs[i]
            acc = bv + acc
            s = _sigmoid(acc)
            dc = da_ref[pl.ds(base, T), :] * _silu_grad(acc, s)
            dcp_ref[pl.ds(base, T), :] = dc
            new = tuple(carry[i] + jnp.sum(xs[i] * dc, axis=0, keepdims=True) for i in range(SSD_CONV))
            return new + (carry[SSD_CONV] + jnp.sum(dc, axis=0, keepdims=True),)

        z = jnp.zeros((1, 128), F32)
        res = lax.fori_loop(0, S // T, step1, (z,) * (SSD_CONV + 1))
        for i in range(SSD_CONV):
            dw_ref[pl.ds(i, 1), :] = res[i]
        db_ref[...] = res[SSD_CONV]

        def step2(c, carry):
            base = pl.multiple_of(c * T, T)
            acc = wv[0:1] * dcp_ref[pl.ds(base + 3, T), :]
            for i in range(1, SSD_CONV):
                acc = acc + wv[i:i + 1] * dcp_ref[pl.ds(base + 3 - i, T), :]
            dx_ref[pl.ds(base, T), :] = acc.astype(dx_ref.dtype)
            return carry

        lax.fori_loop(0, S // T, step2, 0)

    col = pl.BlockSpec((S, 128), lambda j: (0, j))
    clamp = lambda j, lo, n: jnp.clip(j - lo, 0, n - 1)
    return pl.pallas_call(
        body, grid=(SSD_CONV_CH // 128,),
        in_specs=[pl.BlockSpec((S, 128), lambda j: (0, clamp(j, 0, NX))), pl.BlockSpec((S, 128), lambda j: (0, clamp(j, NX, NB))),
                  pl.BlockSpec((S, 128), lambda j: (0, clamp(j, NX + NB, NB))),
                  pl.BlockSpec((S, 128), lambda j: (0, XBC_BLK0 + j)), pl.BlockSpec((SSD_CONV, 128), lambda j: (0, j)),
                  pl.BlockSpec((1, 128), lambda j: (0, j))],
        out_specs=[col, pl.BlockSpec((SSD_CONV, 128), lambda j: (0, j)), pl.BlockSpec((1, 128), lambda j: (0, j))],
        out_shape=[jax.ShapeDtypeStruct((S, SSD_CONV_CH), BF16), jax.ShapeDtypeStruct((SSD_CONV, SSD_CONV_CH), F32),
                   jax.ShapeDtypeStruct((1, SSD_CONV_CH), F32)],
        scratch_shapes=[pltpu.VMEM((S + 8, 128), F32), pltpu.VMEM((S + 8, 128), F32)], name=name,
        compiler_params=_params("arbitrary"),
    )(dxs, dbm, dcm, proj, w, b)


HPG = SSD_HEADS // SSD_GROUPS
GW = HPG * SSD_HEAD_DIM


def _ssd_chunk_terms(dtr, bias, alog, tril, triu):
    pre = dtr + bias
    dt = jnp.maximum(pre, 0.0) + jnp.log(1.0 + jnp.exp(-jnp.abs(pre)))
    a_neg = -jnp.exp(alog)
    a = dt * a_neg
    acum = _xdot(tril, a, NN)
    acum_t = _xdot(a, triu, TN)
    return pre, dt, a_neg, acum, acum_t


def _head_expanders():
    h64 = lax.broadcasted_iota(jnp.int32, (HPG, GW), 0) == lax.broadcasted_iota(jnp.int32, (HPG, GW), 1) // SSD_HEAD_DIM
    h128 = lax.broadcasted_iota(jnp.int32, (HPG, HPG * CHUNK), 0) == lax.broadcasted_iota(jnp.int32, (HPG, HPG * CHUNK), 1) // CHUNK
    return h64.astype(F32), h128.astype(F32)


def _ssd_fwd(xbc, proj, dtr, dt_bias, a_log, d_rep, gain, name):
    S = xbc.shape[0]
    L = CHUNK
    T = min(512, S)
    CPS = T // L
    NC = S // L

    def body(x_ref, b_ref, c_ref, z_ref, dtr_ref, bias_ref, alog_ref, d_ref, gain_ref, y_ref, yraw_ref, st_ref, state):
        i = pl.program_id(1)

        @pl.when(i == 0)
        def _():
            state[...] = jnp.zeros_like(state)

        row = lax.broadcasted_iota(jnp.int32, (L, L), 0)
        col = lax.broadcasted_iota(jnp.int32, (L, L), 1)
        causal = row >= col
        tril = causal.astype(F32)
        triu = (row <= col).astype(F32)
        low = col < SSD_HEAD_DIM
        e64, e128 = _head_expanders()
        for c in range(CPS):
            rows = pl.ds(c * L, L)
            xv = x_ref[rows, :]
            bm = b_ref[rows, :]
            cm = c_ref[rows, :]
            _, dt, _, acum, acum_t = _ssd_chunk_terms(dtr_ref[rows, :], bias_ref[...], alog_ref[...], tril, triu)
            ac = _xdot(acum, e64, NN)
            ac_sq = _xdot(acum, e128, NN)
            xd = xv * _xdot(dt, e64, NN)
            ac_last = ac[L - 1:L, :]
            sp = state[...]
            st_ref[c] = sp
            yoff = _bdot(cm, sp, NN) * jnp.exp(ac)
            state[...] = sp * jnp.exp(ac_last) + _bdot(bm, xd * jnp.exp(ac_last - ac), TN)
            gmat = _bdot(cm, bm, NT)
            for q in range(HPG // 2):
                pair = slice(q * 128, (q + 1) * 128)
                tile = xd[:, pair]
                y = yoff[:, pair]
                for j, keep in ((2 * q, low), (2 * q + 1, ~low)):
                    lam = jnp.exp(jnp.where(causal, ac_sq[:, j * L:(j + 1) * L] - acum_t[j:j + 1, :], NEG))
                    y = y + _bdot(gmat * lam, jnp.where(keep, tile, 0.0), NN)
                yraw_ref[rows, pair] = y
            zz = z_ref[rows, :]
            u = (yraw_ref[rows, :] + xv * d_ref[...]) * (zz * _sigmoid(zz))
            r = lax.rsqrt(jnp.mean(u * u, axis=-1, keepdims=True) + EPS)
            y_ref[rows, :] = (u * r * gain_ref[...]).astype(y_ref.dtype)

    vec8 = pl.BlockSpec((None, 1, HPG), lambda g, i: (g, 0, 0))
    return pl.pallas_call(
        body, grid=(SSD_GROUPS, S // T),
        in_specs=[pl.BlockSpec((T, GW), lambda g, i: (i, g)),
                  pl.BlockSpec((T, SSD_STATE), lambda g, i: (i, SSD_WIDTH // SSD_STATE + g)),
                  pl.BlockSpec((T, SSD_STATE), lambda g, i: (i, SSD_WIDTH // SSD_STATE + SSD_GROUPS + g)),
                  pl.BlockSpec((T, GW), lambda g, i: (i, g)),
                  pl.BlockSpec((None, T, HPG), lambda g, i: (g, i, 0)),
                  vec8, vec8,
                  pl.BlockSpec((1, GW), lambda g, i: (0, g)), pl.BlockSpec((1, GW), lambda g, i: (0, g))],
        out_specs=[pl.BlockSpec((T, GW), lambda g, i: (i, g)), pl.BlockSpec((T, GW), lambda g, i: (i, g)),
                   pl.BlockSpec((CPS, None, SSD_STATE, GW), lambda g, i: (i, g, 0, 0))],
        out_shape=[jax.ShapeDtypeStruct((S, MIX_WIDTH), BF16), jax.ShapeDtypeStruct((S, SSD_WIDTH), F32),
                   jax.ShapeDtypeStruct((NC, SSD_GROUPS, SSD_STATE, GW), F32)],
        scratch_shapes=[pltpu.VMEM((SSD_STATE, GW), F32)], name=name,
        compiler_params=_params("arbitrary", "arbitrary"),
    )(xbc, xbc, xbc, proj, dtr, dt_bias, a_log, d_rep, gain)


def _ssd_bwd(dy, yraw, xbc, proj, dtr, dt_bias, a_log, d_rep, gain, states, name):
    S = xbc.shape[0]
    L = CHUNK
    T = min(512, S)
    CPS = T // L
    NI = S // T

    def body(dy_ref, yraw_ref, x_ref, b_ref, c_ref, z_ref, dtr_ref, bias_ref, alog_ref, d_ref, gain_ref, st_ref,
             dz_ref, dx_ref, db_ref, dc_ref, ddtr_ref, dbias_ref, dalog_ref, dd_ref, dgain_ref, dstate, dxd_ref):
        i = pl.program_id(1)

        @pl.when(i == 0)
        def _():
            dstate[...] = jnp.zeros_like(dstate)
            dbias_ref[...] = jnp.zeros_like(dbias_ref)
            dalog_ref[...] = jnp.zeros_like(dalog_ref)
            dd_ref[...] = jnp.zeros_like(dd_ref)
            dgain_ref[...] = jnp.zeros_like(dgain_ref)

        row = lax.broadcasted_iota(jnp.int32, (L, L), 0)
        col = lax.broadcasted_iota(jnp.int32, (L, L), 1)
        causal = row >= col
        tril = causal.astype(F32)
        triu = (row <= col).astype(F32)
        low = col < SSD_HEAD_DIM
        e64, e128 = _head_expanders()
        lane8 = lax.broadcasted_iota(jnp.int32, (1, HPG), 1)
        sub8 = lax.broadcasted_iota(jnp.int32, (HPG, 1), 0)
        eye8 = (lax.broadcasted_iota(jnp.int32, (HPG, HPG), 0) == lax.broadcasted_iota(jnp.int32, (HPG, HPG), 1)).astype(F32)
        last_row = (lax.broadcasted_iota(jnp.int32, (L, 1), 0) == L - 1).astype(F32)
        for c in reversed(range(CPS)):
            rows = pl.ds(c * L, L)
            xv = x_ref[rows, :]
            bm = b_ref[rows, :]
            cm = c_ref[rows, :]
            zz = z_ref[rows, :]
            dvec = d_ref[...]
            sz = _sigmoid(zz)
            silu_z = zz * sz
            v = yraw_ref[rows, :] + xv * dvec
            u = v * silu_z
            r = lax.rsqrt(jnp.mean(u * u, axis=-1, keepdims=True) + EPS)
            n = u * r
            do = dy_ref[rows, :]
            dgain_ref[...] += jnp.sum(do * n, axis=0, keepdims=True)
            dn = do * gain_ref[...]
            du = r * (dn - n * jnp.mean(dn * n, axis=-1, keepdims=True))
            dz_ref[rows, :] = (du * v * _silu_grad(zz, sz)).astype(dz_ref.dtype)
            dyv = du * silu_z
            dd_ref[...] += _xdot(jnp.sum(dyv * xv, axis=0, keepdims=True), e64, NT)
            pre, dt, a_neg, acum, acum_t = _ssd_chunk_terms(dtr_ref[rows, :], bias_ref[...], alog_ref[...], tril, triu)
            ac = _xdot(acum, e64, NN)
            ac_sq = _xdot(acum, e128, NN)
            dt_w = _xdot(dt, e64, NN)
            xd = xv * dt_w
            ac_last = ac[L - 1:L, :]
            ea = jnp.exp(ac)
            w = jnp.exp(ac_last - ac)
            ea_last = jnp.exp(ac_last)
            sp = st_ref[c]
            ds = dstate[...]
            dye = dyv * ea
            yoff = _bdot(cm, sp, NN) * ea
            bds = _bdot(bm, ds, NN)
            dcm = _bdot(dye, sp, NT)
            dbm = _bdot(xd * w, ds, NT)
            dstate[...] = ds * ea_last + _bdot(cm, dye, TN)
            w8 = jnp.exp(acum[L - 1:L, :] - acum)
            dw8 = _xdot(xd * bds, e64, NT)
            dac8 = _xdot(dyv * yoff, e64, NT) - dw8 * w8
            tail8 = jnp.sum(dw8 * w8, axis=0, keepdims=True) + jnp.exp(acum[L - 1:L, :]) * _xdot(
                jnp.sum(ds * sp, axis=0, keepdims=True), e64, NT)
            dac8 = dac8 + last_row * tail8
            gmat = _bdot(cm, bm, NT)
            dgmat = jnp.zeros((L, L), F32)
            colsum_t = jnp.zeros((HPG, L), F32)
            for q in range(HPG // 2):
                pair = slice(q * 128, (q + 1) * 128)
                xd_tile = xd[:, pair]
                dy_tile = dyv[:, pair]
                dxd_tile = bds[:, pair] * w[:, pair]
                for j, keep in ((2 * q, low), (2 * q + 1, ~low)):
                    lam = jnp.exp(jnp.where(causal, ac_sq[:, j * L:(j + 1) * L] - acum_t[j:j + 1, :], NEG))
                    mh = gmat * lam
                    dyj = jnp.where(keep, dy_tile, 0.0)
                    dxd_tile = dxd_tile + _bdot(mh, dyj, TN)
                    dm = _bdot(dyj, xd_tile, NT)
                    dgmat = dgmat + dm * lam
                    qm = dm * mh
                    dac8 = dac8 + jnp.sum(qm, axis=1, keepdims=True) * (lane8 == j).astype(F32)
                    colsum_t = colsum_t + (sub8 == j).astype(F32) * jnp.sum(qm, axis=0, keepdims=True)
                dxd_ref[:, pair] = dxd_tile
            dac8 = dac8 - _xdot(colsum_t, eye8, TN)
            dxd = dxd_ref[...]
            dx_ref[rows, :] = dxd * dt_w + dyv * dvec
            dc_ref[rows, :] = dcm + _bdot(dgmat, bm, NN)
            db_ref[rows, :] = dbm + _bdot(dgmat, cm, TN)
            da8 = _xdot(triu, dac8, NN)
            ddt8 = _xdot(dxd * xv, e64, NT) + da8 * a_neg
            dalog_ref[...] += jnp.sum(da8 * dt, axis=0, keepdims=True) * a_neg
            dpre = ddt8 * _sigmoid(pre)
            ddtr_ref[rows, :] = dpre
            dbias_ref[...] += jnp.sum(dpre, axis=0, keepdims=True)

    rev = lambda i: NI - 1 - i
    vec8 = pl.BlockSpec((None, 1, HPG), lambda g, i: (g, 0, 0))
    grp = pl.BlockSpec((T, GW), lambda g, i: (rev(i), g))
    bspec = pl.BlockSpec((T, SSD_STATE), lambda g, i: (rev(i), SSD_WIDTH // SSD_STATE + g))
    cspec = pl.BlockSpec((T, SSD_STATE), lambda g, i: (rev(i), SSD_WIDTH // SSD_STATE + SSD_GROUPS + g))
    gvec = pl.BlockSpec((1, GW), lambda g, i: (0, g))
    st_spec = pl.BlockSpec((CPS, None, SSD_STATE, GW), lambda g, i: (rev(i), g, 0, 0))
    small = jax.ShapeDtypeStruct((SSD_GROUPS, 1, HPG), F32)
    return pl.pallas_call(
        body, grid=(SSD_GROUPS, NI),
        in_specs=[grp, grp, grp, bspec, cspec, grp, pl.BlockSpec((None, T, HPG), lambda g, i: (g, rev(i), 0)),
                  vec8, vec8, gvec, gvec, st_spec],
        out_specs=[grp, grp, pl.BlockSpec((T, SSD_STATE), lambda g, i: (rev(i), g)),
                   pl.BlockSpec((T, SSD_STATE), lambda g, i: (rev(i), g)),
                   pl.BlockSpec((None, T, HPG), lambda g, i: (g, rev(i), 0)), vec8, vec8, vec8, gvec],
        out_shape=[jax.ShapeDtypeStruct((S, SSD_WIDTH), BF16), jax.ShapeDtypeStruct((S, SSD_WIDTH), F32),
                   jax.ShapeDtypeStruct((S, SSD_GROUPS * SSD_STATE), F32), jax.ShapeDtypeStruct((S, SSD_GROUPS * SSD_STATE), F32),
                   jax.ShapeDtypeStruct((SSD_GROUPS, S, HPG), F32), small, small, small,
                   jax.ShapeDtypeStruct((1, SSD_WIDTH), F32)],
        scratch_shapes=[pltpu.VMEM((SSD_STATE, GW), F32), pltpu.VMEM((L, GW), F32)],
        name=name, compiler_params=_params("arbitrary", "arbitrary"),
    )(dy, yraw, xbc, xbc, xbc, proj, dtr, dt_bias, a_log, d_rep, gain, states)


def _swap_halves(t):
    w = t.shape[1]
    lane = lax.broadcasted_iota(jnp.int32, t.shape, 1)
    return jnp.where((lane % 64) < 32, pltpu.roll(t, w - 32, axis=1), pltpu.roll(t, 32, axis=1))


def _widen(tab, w):
    return tab if w == 128 else jnp.concatenate([tab] * (w // 128), axis=1)


def _rope(t, cos, sin_signed):
    return t * cos + _swap_halves(t) * sin_signed


def _rope_t(d, cos, sin_signed):
    return d * cos - _swap_halves(d) * sin_signed


def _group_sum64(v, bd):
    hi = v.astype(BF16)
    lo = (v - hi.astype(F32)).astype(BF16)
    return (lax.dot_general(hi, bd, NN, preferred_element_type=F32)
            + lax.dot_general(lo, bd, NN, preferred_element_type=F32))


AQ_BLK = 2560 // ATT_WIDTH


def _att_prep_fwd(proj, qg, kg, cos, sin, bd, name):
    S = proj.shape[0]
    T = min(512, S)
    PB = ATT_SPAN // T
    src = lambda i: jnp.maximum(i - PB, 0)

    def body(q_ref, k_ref, v_ref, qg_ref, kg_ref, cos_ref, sin_ref, bd_ref, qo_ref, ko_ref, vo_ref):
        i = pl.program_id(0)

        @pl.when(i < PB)
        def _():
            ko_ref[...] = jnp.zeros_like(ko_ref)
            vo_ref[...] = jnp.zeros_like(vo_ref)

        @pl.when(i >= PB)
        def _():
            cw = _widen(cos_ref[...], ATT_WIDTH)
            sw = _widen(sin_ref[...], ATT_WIDTH)
            bdv = bd_ref[...]

            def norm_rope(t, gain):
                ss = _group_sum64(t * t, bdv)
                return _rope(t * lax.rsqrt(ss * (1.0 / ATT_HEAD_DIM) + EPS) * gain, cw, sw)

            qo_ref[...] = (norm_rope(q_ref[...], qg_ref[...]) * (ATT_HEAD_DIM ** -0.5)).astype(BF16)
            kt = norm_rope(k_ref[...], kg_ref[...]).astype(BF16)
            vt = v_ref[...].astype(BF16)
            for pr in range(ATT_HEADS // 2):
                ko_ref[pr] = kt[:, pr * 128:(pr + 1) * 128]
                vo_ref[pr] = vt[:, pr * 128:(pr + 1) * 128]

    vec = pl.BlockSpec((1, ATT_WIDTH), lambda i: (0, 0))
    tab = pl.BlockSpec((T, 128), lambda i: (src(i), 0))
    hm = pl.BlockSpec((ATT_HEADS // 2, T, 128), lambda i: (0, i, 0))
    hm_shape = jax.ShapeDtypeStruct((ATT_HEADS // 2, ATT_SPAN + S, 128), BF16)
    return pl.pallas_call(
        body, grid=(PB + S // T,),
        in_specs=[pl.BlockSpec((T, ATT_WIDTH), lambda i: (src(i), AQ_BLK)), pl.BlockSpec((T, ATT_WIDTH), lambda i: (src(i), AQ_BLK + 1)),
                  pl.BlockSpec((T, ATT_WIDTH), lambda i: (src(i), AQ_BLK + 2)), vec, vec, tab, tab,
                  pl.BlockSpec((ATT_WIDTH, ATT_WIDTH), lambda i: (0, 0))],
        out_specs=[pl.BlockSpec((T, ATT_WIDTH), lambda i: (src(i), 0)), hm, hm],
        out_shape=[jax.ShapeDtypeStruct((S, ATT_WIDTH), BF16), hm_shape, hm_shape],
        name=name, compiler_params=_params("arbitrary"),
    )(proj, proj, proj, qg, kg, cos, sin, bd)


def _att_prep_bwd(proj, dq, dk_p, dv_p, qg, kg, cos, sin, bd, name):
    S = proj.shape[0]
    T = min(512, S)
    NI = S // T
    PB = ATT_SPAN // T

    def body(q_ref, k_ref, dq_ref, dkp_ref, dvp_ref, qg_ref, kg_ref, cos_ref, sin_ref, bd_ref,
             dqo_ref, dko_ref, dvo_ref, dqg_ref, dkg_ref, acc_ref):
        i = pl.program_id(0)

        @pl.when(i == 0)
        def _():
            acc_ref[...] = jnp.zeros_like(acc_ref)

        npair = ATT_HEADS // 2
        dk_all = jnp.concatenate([dkp_ref[pr] for pr in range(npair)], axis=1)
        dvo_ref[...] = jnp.concatenate([dvp_ref[pr] for pr in range(npair)], axis=1).astype(BF16)
        cw = _widen(cos_ref[...], ATT_WIDTH)
        sw = _widen(sin_ref[...], ATT_WIDTH)
        bdv = bd_ref[...]

        def one(t, d_rot, gain, scale, slot):
            ss = _group_sum64(t * t, bdv)
            r = lax.rsqrt(ss * (1.0 / ATT_HEAD_DIM) + EPS)
            n = t * r
            d_ng = _rope_t(d_rot * scale, cw, sw)
            acc_ref[pl.ds(slot, 1), :] += jnp.sum(d_ng * n, axis=0, keepdims=True)
            dn = d_ng * gain
            return r * (dn - n * (_group_sum64(dn * n, bdv) * (1.0 / ATT_HEAD_DIM)))

        dqo_ref[...] = one(q_ref[...], dq_ref[...], qg_ref[...], ATT_HEAD_DIM ** -0.5, 0).astype(BF16)
        dko_ref[...] = one(k_ref[...], dk_all, kg_ref[...], 1.0, 1).astype(BF16)

        @pl.when(i == NI - 1)
        def _():
            a = acc_ref[...]
            f = a[:, 0:64]
            for h in range(1, ATT_HEADS):
                f = f + a[:, h * 64:(h + 1) * 64]
            dqg_ref[...] = f[0:1]
            dkg_ref[...] = f[1:2]

    vec = pl.BlockSpec((1, ATT_WIDTH), lambda i: (0, 0))
    tab = pl.BlockSpec((T, 128), lambda i: (i, 0))
    row = pl.BlockSpec((T, ATT_WIDTH), lambda i: (i, 0))
    g64 = pl.BlockSpec((1, ATT_HEAD_DIM), lambda i: (0, 0))
    padded = pl.BlockSpec((ATT_HEADS // 2, T, 128), lambda i: (0, i + PB, 0))
    flat = jax.ShapeDtypeStruct((S, ATT_WIDTH), BF16)
    return pl.pallas_call(
        body, grid=(NI,),
        in_specs=[pl.BlockSpec((T, ATT_WIDTH), lambda i: (i, AQ_BLK)), pl.BlockSpec((T, ATT_WIDTH), lambda i: (i, AQ_BLK + 1)),
                  row, padded, padded, vec, vec, tab, tab, pl.BlockSpec((ATT_WIDTH, ATT_WIDTH), lambda i: (0, 0))],
        out_specs=[row, row, row, g64, g64],
        out_shape=[flat, flat, flat, jax.ShapeDtypeStruct((1, ATT_HEAD_DIM), F32), jax.ShapeDtypeStruct((1, ATT_HEAD_DIM), F32)],
        scratch_shapes=[pltpu.VMEM((8, ATT_WIDTH), F32)], name=name, compiler_params=_params("arbitrary"),
    )(proj, proj, dq, dk_p, dv_p, qg, kg, cos, sin, bd)


def _att_bias():
    qpos = np.arange(CHUNK)[:, None] + ATT_SPAN
    kpos = np.arange(ATT_STRIP)[None, :]
    rel = qpos - kpos
    mult = np.zeros((CHUNK, ATT_STRIP), np.float64)
    for window, dil in DILATED_PAIRS:
        mult += (rel >= 0) & (rel % dil == 0) & (rel // dil <= window // dil)
    return np.where(mult > 0, np.log(np.maximum(mult, 1.0)), NEG).astype(np.float32)


def _att_scores(q, ks, bias, i):
    s = _bdot(q, ks, NT) + bias
    kcol = lax.broadcasted_iota(jnp.int32, (1, ATT_STRIP), 1) + i * CHUNK
    return jnp.where(kcol >= ATT_SPAN, s, NEG)


def _pair_masks():
    low = lax.broadcasted_iota(jnp.int32, (CHUNK, 128), 1) < ATT_HEAD_DIM
    return low, ~low


def _att_fwd(q, kp, vp, bias, y, name):
    S = q.shape[0]
    SP = kp.shape[1]

    def body(q_ref, k_ref, v_ref, bias_ref, y_ref, o_ref):
        i = pl.program_id(1)
        strip = pl.ds(pl.multiple_of(i * CHUNK, CHUNK), ATT_STRIP)
        qv = q_ref[...]
        ks = k_ref[strip, :]
        vs = v_ref[strip, :]
        outs = []
        for keep in _pair_masks():
            s = _att_scores(jnp.where(keep, qv, jnp.zeros_like(qv)), ks, bias_ref[...], i)
            m = jnp.max(s, axis=-1, keepdims=True)
            p = jnp.exp(s - m)
            den = jnp.sum(p, axis=-1, keepdims=True)
            outs.append(_bdot(p, vs, NN) / den)
        o_ref[...] = jnp.where(_pair_masks()[0], outs[0], outs[1]).astype(o_ref.dtype)

    kv = pl.BlockSpec((None, SP, 128), lambda hp, i: (hp, 0, 0))
    return pl.pallas_call(
        body, grid=(ATT_HEADS // 2, S // CHUNK),
        in_specs=[pl.BlockSpec((CHUNK, 128), lambda hp, i: (i, hp)), kv, kv,
                  pl.BlockSpec((CHUNK, ATT_STRIP), lambda hp, i: (0, 0)), pl.BlockSpec(memory_space=pl.ANY)],
        out_specs=pl.BlockSpec((CHUNK, 128), lambda hp, i: (i, SSD_WIDTH // 128 + hp)),
        out_shape=jax.ShapeDtypeStruct(y.shape, y.dtype), input_output_aliases={4: 0}, name=name,
        compiler_params=_params("parallel", "arbitrary"),
    )(q, kp, vp, bias, y)


def _att_bwd(q, kp, vp, bias, dy, name):
    S = q.shape[0]
    SP = kp.shape[1]

    def body(q_ref, k_ref, v_ref, bias_ref, do_ref, dq_ref, dk_ref, dv_ref):
        i = pl.program_id(1)

        @pl.when(i == 0)
        def _():
            dk_ref[...] = jnp.zeros_like(dk_ref)
            dv_ref[...] = jnp.zeros_like(dv_ref)

        strip = pl.ds(pl.multiple_of(i * CHUNK, CHUNK), ATT_STRIP)
        qv = q_ref[...]
        dov = do_ref[...]
        ks = k_ref[strip, :]
        vs = v_ref[strip, :]
        dq = jnp.zeros((CHUNK, 128), F32)
        dk = jnp.zeros((ATT_STRIP, 128), F32)
        dv = jnp.zeros((ATT_STRIP, 128), F32)
        for keep in _pair_masks():
            qh = jnp.where(keep, qv, jnp.zeros_like(qv))
            doh = jnp.where(keep, dov, 0.0)
            s = _att_scores(qh, ks, bias_ref[...], i)
            m = jnp.max(s, axis=-1, keepdims=True)
            p = jnp.exp(s - m)
            p = p / jnp.sum(p, axis=-1, keepdims=True)
            dp = _bdot(doh, vs, NT)
            dsc = p * (dp - jnp.sum(p * dp, axis=-1, keepdims=True))
            dq = dq + jnp.where(keep, _bdot(dsc, ks, NN), 0.0)
            dv = dv + _bdot(p, doh, TN)
            dk = dk + _bdot(dsc, qh, TN)
        dq_ref[...] = dq
        dv_ref[strip, :] += dv
        dk_ref[strip, :] += dk

    kv = pl.BlockSpec((None, SP, 128), lambda hp, i: (hp, 0, 0))
    pairs = jax.ShapeDtypeStruct((ATT_HEADS // 2, SP, 128), F32)
    return pl.pallas_call(
        body, grid=(ATT_HEADS // 2, S // CHUNK),
        in_specs=[pl.BlockSpec((CHUNK, 128), lambda hp, i: (i, hp)), kv, kv,
                  pl.BlockSpec((CHUNK, ATT_STRIP), lambda hp, i: (0, 0)),
                  pl.BlockSpec((CHUNK, 128), lambda hp, i: (i, SSD_WIDTH // 128 + hp))],
        out_specs=[pl.BlockSpec((CHUNK, 128), lambda hp, i: (i, hp)), kv, kv],
        out_shape=[jax.ShapeDtypeStruct((S, ATT_WIDTH), F32), pairs, pairs],
        name=name, compiler_params=_params("parallel", "arbitrary"),
    )(q, kp, vp, bias, dy)


RQ_BLK = 4096 // RET_QK_WIDTH
RV_BLK = 4608 // RET_V_WIDTH
RET_LOG_GAMMA = tuple(math.log1p(-2.0 ** (-5.0 - h)) for h in range(RET_HEADS))


def _ret_decays(h):
    L = CHUNK
    lg = RET_LOG_GAMMA[h]
    row = lax.broadcasted_iota(jnp.int32, (L, L), 0)
    col = lax.broadcasted_iota(jnp.int32, (L, L), 1)
    rel = (row - col).astype(F32)
    dm = jnp.where(rel >= 0, jnp.exp(jnp.maximum(rel, 0.0) * lg), 0.0)
    idx = lax.broadcasted_iota(jnp.int32, (L, 1), 0).astype(F32)
    kte = jnp.exp((L - 1 - idx) * lg)
    qfs = jnp.exp((idx + 1.0) * lg)
    return dm, kte, qfs, math.exp(L * lg)


def _ret_fwd(proj, cos, sin, gain, y, name):
    S = proj.shape[0]
    L = CHUNK
    T = min(512, S)
    CPS = T // L
    NC = S // L

    def body(q_ref, k_ref, v_ref, g_ref, cos_ref, sin_ref, gain_ref, yin_ref, y_ref, o_ref, st_ref, state):
        i = pl.program_id(0)

        @pl.when(i == 0)
        def _():
            state[...] = jnp.zeros_like(state)

        dec = [_ret_decays(h) for h in range(RET_HEADS)]
        for c in range(CPS):
            rows = pl.ds(c * L, L)
            cw = _widen(cos_ref[rows, :], RET_QK_WIDTH)
            sw = _widen(sin_ref[rows, :], RET_QK_WIDTH)
            qv = _rope(q_ref[rows, :], cw, sw)
            kv = _rope(k_ref[rows, :], cw, sw) * (RET_QK_DIM ** -0.5)
            for h in range(RET_HEADS):
                dm, kte, qfs, cd = dec[h]
                qh = qv[:, h * 64:(h + 1) * 64]
                kh = kv[:, h * 64:(h + 1) * 64]
                vs = slice(h * RET_V_DIM, (h + 1) * RET_V_DIM)
                vh = v_ref[rows, vs]
                sp = state[h]
                st_ref[c, h] = sp
                o = _bdot(_bdot(qh, kh, NT) * dm, vh, NN) + _bdot(qh * qfs, sp, NN)
                state[h] = cd * sp + _bdot(kh * kte, vh, TN)
                o_ref[rows, vs] = o
                gh = g_ref[rows, vs]
                r = lax.rsqrt(jnp.mean(o * o, axis=-1, keepdims=True) + EPS)
                y_ref[rows, vs] = (o * r * gain_ref[:, vs] * (gh * _sigmoid(gh))).astype(y_ref.dtype)

    tab = pl.BlockSpec((T, 128), lambda i: (i, 0))
    wide = pl.BlockSpec((T, RET_V_WIDTH), lambda i: (i, 0))
    return pl.pallas_call(
        body, grid=(S // T,),
        in_specs=[pl.BlockSpec((T, RET_QK_WIDTH), lambda i: (i, RQ_BLK)), pl.BlockSpec((T, RET_QK_WIDTH), lambda i: (i, RQ_BLK + 1)),
                  pl.BlockSpec((T, RET_V_WIDTH), lambda i: (i, RV_BLK)), pl.BlockSpec((T, RET_V_WIDTH), lambda i: (i, RV_BLK + 1)),
                  tab, tab, pl.BlockSpec((1, RET_V_WIDTH), lambda i: (0, 0)), pl.BlockSpec(memory_space=pl.ANY)],
        out_specs=[pl.BlockSpec((T, RET_V_WIDTH), lambda i: (i, (SSD_WIDTH + ATT_WIDTH) // RET_V_WIDTH)), wide,
                   pl.BlockSpec((CPS, RET_HEADS, RET_QK_DIM, RET_V_DIM), lambda i: (i, 0, 0, 0))],
        out_shape=[jax.ShapeDtypeStruct(y.shape, y.dtype), jax.ShapeDtypeStruct((S, RET_V_WIDTH), F32),
                   jax.ShapeDtypeStruct((NC, RET_HEADS, RET_QK_DIM, RET_V_DIM), F32)],
        input_output_aliases={7: 0},
        scratch_shapes=[pltpu.VMEM((RET_HEADS, RET_QK_DIM, RET_V_DIM), F32)], name=name,
        compiler_params=_params("arbitrary"),
    )(proj, proj, proj, proj, cos, sin, gain, y)


def _ret_bwd(dy, oraw, proj, cos, sin, gain, states, name):
    S = proj.shape[0]
    L = CHUNK
    T = min(512, S)
    CPS = T // L
    NI = S // T

    def body(dy_ref, o_ref, q_ref, k_ref, v_ref, g_ref, cos_ref, sin_ref, gain_ref, st_ref,
             dq_ref, dk_ref, dv_ref, dg_ref, dgain_ref, dstate, dqs, dks):
        i = pl.program_id(0)

        @pl.when(i == 0)
        def _():
            dstate[...] = jnp.zeros_like(dstate)
            dgain_ref[...] = jnp.zeros_like(dgain_ref)

        dec = [_ret_decays(h) for h in range(RET_HEADS)]
        for c in reversed(range(CPS)):
            rows = pl.ds(c * L, L)
            cw = _widen(cos_ref[rows, :], RET_QK_WIDTH)
            sw = _widen(sin_ref[rows, :], RET_QK_WIDTH)
            qv = _rope(q_ref[rows, :], cw, sw)
            kv = _rope(k_ref[rows, :], cw, sw) * (RET_QK_DIM ** -0.5)
            for h in range(RET_HEADS):
                dm, kte, qfs, cd = dec[h]
                qs = slice(h * 64, (h + 1) * 64)
                vs = slice(h * RET_V_DIM, (h + 1) * RET_V_DIM)
                qh = qv[:, qs]
                kh = kv[:, qs]
                vh = v_ref[rows, vs]
                gh = g_ref[rows, vs]
                gn = gain_ref[:, vs]
                o = o_ref[rows, vs]
                dyh = dy_ref[rows, vs]
                sg = _sigmoid(gh)
                silu_g = gh * sg
                r = lax.rsqrt(jnp.mean(o * o, axis=-1, keepdims=True) + EPS)
                n = o * r
                dgain_ref[:, vs] += jnp.sum(dyh * n * silu_g, axis=0, keepdims=True)
                dg_ref[rows, vs] = (dyh * n * gn * _silu_grad(gh, sg)).astype(dg_ref.dtype)
                dn = dyh * gn * silu_g
                do = r * (dn - n * jnp.mean(dn * n, axis=-1, keepdims=True))
                sp = st_ref[c, h]
                ds = dstate[h]
                sc = _bdot(qh, kh, NT) * dm
                dsc = _bdot(do, vh, NT) * dm
                dv_ref[rows, vs] = (_bdot(sc, do, TN) + _bdot(kh * kte, ds, NN)).astype(dv_ref.dtype)
                dqs[:, qs] = _bdot(dsc, kh, NN) + _bdot(do, sp, NT) * qfs
                dks[:, qs] = _bdot(dsc, qh, TN) + _bdot(vh, ds, NT) * kte
                dstate[h] = cd * ds + _bdot(qh * qfs, do, TN)
            dq_ref[rows, :] = _rope_t(dqs[...], cw, sw).astype(dq_ref.dtype)
            dk_ref[rows, :] = _rope_t(dks[...] * (RET_QK_DIM ** -0.5), cw, sw).astype(dk_ref.dtype)

    rev = lambda i: NI - 1 - i
    tab = pl.BlockSpec((T, 128), lambda i: (rev(i), 0))
    wide = pl.BlockSpec((T, RET_V_WIDTH), lambda i: (rev(i), 0))
    narrow = pl.BlockSpec((T, RET_QK_WIDTH), lambda i: (rev(i), 0))
    gvec = pl.BlockSpec((1, RET_V_WIDTH), lambda i: (0, 0))
    return pl.pallas_call(
        body, grid=(NI,),
        in_specs=[pl.BlockSpec((T, RET_V_WIDTH), lambda i: (rev(i), (SSD_WIDTH + ATT_WIDTH) // RET_V_WIDTH)), wide,
                  pl.BlockSpec((T, RET_QK_WIDTH), lambda i: (rev(i), RQ_BLK)), pl.BlockSpec((T, RET_QK_WIDTH), lambda i: (rev(i), RQ_BLK + 1)),
                  pl.BlockSpec((T, RET_V_WIDTH), lambda i: (rev(i), RV_BLK)), pl.BlockSpec((T, RET_V_WIDTH), lambda i: (rev(i), RV_BLK + 1)),
                  tab, tab, gvec,
                  pl.BlockSpec((CPS, RET_HEADS, RET_QK_DIM, RET_V_DIM), lambda i: (rev(i), 0, 0, 0))],
        out_specs=[narrow, narrow, wide, wide, gvec],
        out_shape=[jax.ShapeDtypeStruct((S, RET_QK_WIDTH), BF16), jax.ShapeDtypeStruct((S, RET_QK_WIDTH), BF16),
                   jax.ShapeDtypeStruct((S, RET_V_WIDTH), BF16), jax.ShapeDtypeStruct((S, RET_V_WIDTH), BF16),
                   jax.ShapeDtypeStruct((1, RET_V_WIDTH), F32)],
        scratch_shapes=[pltpu.VMEM((RET_HEADS, RET_QK_DIM, RET_V_DIM), F32), pltpu.VMEM((L, RET_QK_WIDTH), F32),
                        pltpu.VMEM((L, RET_QK_WIDTH), F32)],
        name=name, compiler_params=_params("arbitrary"),
    )(dy, oraw, proj, proj, proj, proj, cos, sin, gain, states)


def _adamw_update(g_ref, nb, w_ref, m_ref, v_ref, go_ref, d_ref, mo_ref, vo_ref):
    g = g_ref[0].astype(F32)
    for k in range(1, nb):
        g = g + g_ref[k].astype(F32)
    mn = ADAM_B1 * m_ref[...] + (1.0 - ADAM_B1) * g
    vn = ADAM_B2 * v_ref[...] + (1.0 - ADAM_B2) * (g * g)
    go_ref[...] = g
    mo_ref[...] = mn
    vo_ref[...] = vn
    c1 = 1.0 - ADAM_B1 ** ADAM_STEP
    c2 = 1.0 - ADAM_B2 ** ADAM_STEP
    d_ref[...] = -ADAM_LR * ((mn / c1) / (jnp.sqrt(vn / c2) + ADAM_EPS) + ADAM_WD * w_ref[...])


def _adamw_rows(R, C):
    return _pick(R, tuple(t for t in (512, 256, 128, 64, 32, 16, 8) if t * C <= 256 * 1024))


def _adamw(gblocks, w, m, v, name):
    nb, R, C = gblocks.shape
    tr = _adamw_rows(R, C)

    def body(g_ref, *refs):
        _adamw_update(g_ref, nb, *refs)

    row = pl.BlockSpec((tr, C), lambda i: (i, 0))
    sh = jax.ShapeDtypeStruct((R, C), F32)
    return pl.pallas_call(
        body, grid=(R // tr,), in_specs=[pl.BlockSpec((nb, tr, C), lambda i: (0, i, 0)), row, row, row],
        out_specs=[row, row, row, row], out_shape=[sh, sh, sh, sh], name=name, compiler_params=_params("parallel"),
    )(gblocks, w, m, v)


def _adamw_layers(g0, g1, w, m, v, name):
    nb, R, C = g0.shape
    tr = _adamw_rows(R, C)

    def body(g0_ref, g1_ref, *refs):
        l = pl.program_id(0)

        @pl.when(l == 0)
        def _():
            _adamw_update(g0_ref, nb, *refs)

        @pl.when(l == 1)
        def _():
            _adamw_update(g1_ref, nb, *refs)

    row = pl.BlockSpec((None, tr, C), lambda l, i: (l, i, 0))
    sh = jax.ShapeDtypeStruct((DEPTH, R, C), F32)
    return pl.pallas_call(
        body, grid=(DEPTH, R // tr),
        in_specs=[pl.BlockSpec((nb, tr, C), lambda l, i: (0, i * (1 - l), 0)), pl.BlockSpec((nb, tr, C), lambda l, i: (0, i * l, 0)),
                  row, row, row],
        out_specs=[row, row, row, row], out_shape=[sh, sh, sh, sh], name=name, compiler_params=_params("arbitrary", "arbitrary"),
    )(g0, g1, w, m, v)


def _peers():
    x, y, c = lax.axis_index("x"), lax.axis_index("y"), lax.axis_index("c")
    flips = ((0, 0, 1), (1, 0, 0), (0, 1, 0), (1, 1, 0), (1, 0, 1), (0, 1, 1), (1, 1, 1))
    me = 4 * x + 2 * y + c
    peers = [(x ^ fx, y ^ fy, c ^ fc) for fx, fy, fc in flips]
    return me, peers


def _exchange(arrs, scatter, name):
    n = len(arrs)
    npeer = N_DEV - 1

    def body(*refs):
        ins, outs = refs[:n], refs[n:2 * n]
        send_sems, recv_sems, local_sems = refs[2 * n:]
        me, peers = _peers()
        copies = []
        for a in range(n):
            src_own = ins[a].at[me] if scatter else ins[a]
            own = pltpu.make_async_copy(src_own, outs[a].at[me], local_sems.at[a])
            own.start()
            copies.append(own)
            for k, peer in enumerate(peers):
                src = ins[a].at[4 * peer[0] + 2 * peer[1] + peer[2]] if scatter else ins[a]
                cp = pltpu.make_async_remote_copy(
                    src_ref=src, dst_ref=outs[a].at[me], send_sem=send_sems.at[a * npeer + k],
                    recv_sem=recv_sems.at[a * npeer + k], device_id=peer, device_id_type=pl.DeviceIdType.MESH)
                cp.start()
                copies.append(cp)
        for cp in copies:
            cp.wait()

    out_shape = [jax.ShapeDtypeStruct(((N_DEV,) + a.shape[1:]) if scatter else ((N_DEV,) + a.shape), a.dtype) for a in arrs]
    anyspec = pl.BlockSpec(memory_space=pl.ANY)
    return pl.pallas_call(
        body, in_specs=[anyspec] * n, out_specs=[anyspec] * n, out_shape=out_shape,
        scratch_shapes=[pltpu.SemaphoreType.DMA((n * npeer,)), pltpu.SemaphoreType.DMA((n * npeer,)),
                        pltpu.SemaphoreType.DMA((n,))],
        name=name,
    )(*arrs)


def _dev_index(peer):
    return 4 * peer[0] + 2 * peer[1] + peer[2]


def _push_copies(src_refs, land_refs, send_sems, recv_sems, scatter, as_receiver):
    me, peers = _peers()
    npeer = N_DEV - 1
    copies = []
    for a in range(len(src_refs)):
        for k, peer in enumerate(peers):
            src = src_refs[a].at[_dev_index(peer)] if scatter else src_refs[a]
            slot = _dev_index(peer) if as_receiver else me
            copies.append(pltpu.make_async_remote_copy(
                src_ref=src, dst_ref=land_refs[a].at[slot], send_sem=send_sems.at[a * npeer + k],
                recv_sem=recv_sems.at[a * npeer + k], device_id=peer, device_id_type=pl.DeviceIdType.MESH))
    return copies


def _push_start(srcs, lands, scatter, name):
    n = len(srcs)
    nsem = n * (N_DEV - 1)

    def body(*refs):
        for cp in _push_copies(refs[:n], refs[n:2 * n], refs[2 * n], refs[2 * n + 1], scatter, False):
            cp.start()
        token = refs[-1]
        token[...] = jnp.zeros_like(token)

    hbm = pl.BlockSpec(memory_space=pltpu.HBM)
    sem = pl.BlockSpec(memory_space=pltpu.SEMAPHORE)
    arrs = list(srcs) + list(lands)
    return pl.pallas_call(
        body, name=name,
        out_shape=(pltpu.SemaphoreType.DMA((nsem,)), pltpu.SemaphoreType.DMA((nsem,)),
                   *[pltpu.HBM(a.shape, a.dtype) for a in arrs], jax.ShapeDtypeStruct((8, 128), F32)),
        in_specs=[hbm] * (2 * n), out_specs=(sem, sem, *([hbm] * (2 * n)), pl.BlockSpec(memory_space=pltpu.VMEM)),
        input_output_aliases={i: 2 + i for i in range(2 * n)},
        compiler_params=pltpu.CompilerParams(has_side_effects=pltpu.SideEffectType.DATAFLOW_SIDE_EFFECTING),
    )(*[pltpu.with_memory_space_constraint(a, pltpu.HBM) for a in arrs])


def _push_wait(handle, after, scatter, name):
    send_sems, recv_sems, *thru, _ = handle
    n = len(thru) // 2

    def body(*refs):
        for cp in _push_copies(refs[:n], refs[n:2 * n], refs[2 * n], refs[2 * n + 1], scatter, True):
            cp.wait_send()
            cp.wait_recv()

    hbm = pl.BlockSpec(memory_space=pltpu.HBM)
    sem = pl.BlockSpec(memory_space=pltpu.SEMAPHORE)
    outs = pl.pallas_call(
        body, name=name, out_shape=tuple(pltpu.HBM(a.shape, a.dtype) for a in thru),
        in_specs=[hbm] * (2 * n) + [sem, sem, pl.BlockSpec(memory_space=pl.ANY)], out_specs=tuple([hbm] * (2 * n)),
        input_output_aliases={i: i for i in range(2 * n)},
        compiler_params=pltpu.CompilerParams(has_side_effects=pltpu.SideEffectType.DATAFLOW_SIDE_EFFECTING),
    )(*thru, send_sems, recv_sems, after)
    return list(outs[n:])


def _landing(own, me):
    return lax.dynamic_update_index_in_dim(lax.empty((N_DEV,) + own.shape, own.dtype), own, me, 0)


def _tables(S):
    pos = jnp.arange(S, dtype=F32)
    inv = ROPE_THETA ** (-jnp.arange(0, ATT_HEAD_DIM, 2, dtype=F32) / ATT_HEAD_DIM)
    ang = pos[:, None] * inv[None, :]
    cos, sin = jnp.cos(ang), jnp.sin(ang)
    cos128 = jnp.tile(cos, (1, 4))
    sin128 = jnp.tile(jnp.concatenate([-sin, sin], axis=1), (1, 2))
    lane = np.arange(ATT_WIDTH)
    bd = jnp.asarray((lane[:, None] // 64 == lane[None, :] // 64).astype(np.float32), dtype=BF16)
    return cos128, sin128, bd, jnp.asarray(_att_bias())


def _permute_in(w):
    pad = jnp.zeros((w.shape[0], IN_PAD - IN_WIDTH), w.dtype)
    return jnp.concatenate([w[:, :2560], w[:, 2576:], w[:, 2560:2576], pad], axis=1)


def _unpermute_in(g):
    return jnp.concatenate([g[:, :2560], g[:, DT_COL:DT_COL + SSD_HEADS], g[:, 2560:DT_COL]], axis=1)


def _layer_fwd(l, x, p, tabs, late=None):
    cos, sin, bd, bias = tabs
    S = x.shape[0]
    row = lambda v: v.reshape(1, -1)
    hn = _rmsnorm_fwd(x, row(p["ln_mix"]), f"norm_mix_fwd{l}")
    proj = _mm(hn, p["w_in"], "nn", f"in_proj{l}", tn=1152)
    xbc = _conv_fwd(proj, p["conv_w"], row(p["conv_b"]), f"conv_fwd{l}")
    dtr = proj[:, DT_COL:DT_COL + SSD_HEADS].reshape(S, SSD_GROUPS, HPG).transpose(1, 0, 2)
    grp = lambda v: v.reshape(SSD_GROUPS, 1, HPG)
    d_rep = row(jnp.repeat(p["d_skip"], SSD_HEAD_DIM))
    y, yraw, ssd_st = _ssd_fwd(xbc, proj, dtr, grp(p["dt_bias"]), grp(p["a_log"]), d_rep, row(p["ssd_norm"]), f"ssd_fwd{l}")
    qg = row(jnp.tile(p["q_norm"], ATT_HEADS))
    kg = row(jnp.tile(p["k_norm"], ATT_HEADS))
    aq, akp, avp = _att_prep_fwd(proj, qg, kg, cos, sin, bd, f"att_prep_fwd{l}")
    y = _att_fwd(aq, akp, avp, bias, y, f"att_fwd{l}")
    y, oraw, ret_st = _ret_fwd(proj, cos, sin, row(p["ret_norm"]), y, f"ret_fwd{l}")
    if late is not None:
        p.update(late(y))
    x1 = _mm(y, p["w_out"], "nn", f"out_proj{l}", residual=x)
    hn2 = _rmsnorm_fwd(x1, row(p["ln_ffn"]), f"norm_ffn_fwd{l}")
    g, u, act = _swiglu_fwd(hn2, p["w_gate"], p["w_up"], f"swiglu_fwd{l}")
    x2 = _mm(act, p["w_down"], "nn", f"down_proj{l}", residual=x1, tk=2816)
    saved = dict(x=x, hn=hn, proj=proj, xbc=xbc, dtr=dtr, yraw=yraw, ssd_st=ssd_st, aq=aq, akp=akp, avp=avp,
                 oraw=oraw, ret_st=ret_st, y=y, x1=x1, hn2=hn2, g=g, u=u, act=act, d_rep=d_rep, qg=qg, kg=kg)
    return x2, saved


def _layer_bwd(l, dx2, dx2_bf, p, sv, tabs, on_ffn=None, on_all=None):
    cos, sin, bd, bias = tabs
    S = dx2.shape[0]
    row = lambda v: v.reshape(1, -1)
    grp = lambda v: v.reshape(SSD_GROUPS, 1, HPG)
    gr = {}
    dg, du = _swiglu_bwd(dx2_bf, p["w_down"], sv["g"], sv["u"], f"swiglu_bwd{l}")
    gr["w_down"] = _mm(sv["act"], dx2_bf, "tn", f"down_wgrad{l}", out_dtype=BF16, tm=1408, tn=1024, tk=2048)
    dhn2 = _mm_nt2(dg, p["w_gate"], du, p["w_up"], f"ffn_dgrad{l}")
    gr["w_gate"] = _mm(sv["hn2"], dg, "tn", f"gate_wgrad{l}", out_dtype=BF16, tm=1024, tn=1408, tk=2048)
    gr["w_up"] = _mm(sv["hn2"], du, "tn", f"up_wgrad{l}", out_dtype=BF16, tm=1024, tn=1408, tk=2048)
    ffn_gain = row(p["ln_ffn"]) + (on_ffn(gr)[0, 0] if on_ffn is not None else 0.0)
    dx1, dx1_bf, dln_ffn = _rmsnorm_bwd(sv["x1"], dhn2, ffn_gain, dx2, f"norm_ffn_bwd{l}")
    gr["ln_ffn"] = dln_ffn[0]
    dy = _mm(dx1_bf, p["w_out"], "nt", f"out_dgrad{l}")
    gr["w_out"] = _mm(sv["y"], dx1_bf, "tn", f"out_wgrad{l}", out_dtype=BF16, tm=1024, tn=1024, tk=2048)
    dz, dxs, dbm, dcm, ddtr, dbias, dalog, dd, dssd_gain = _ssd_bwd(
        dy, sv["yraw"], sv["xbc"], sv["proj"], sv["dtr"], grp(p["dt_bias"]), grp(p["a_log"]), sv["d_rep"],
        row(p["ssd_norm"]), sv["ssd_st"], f"ssd_bwd{l}")
    gr["dt_bias"], gr["a_log"], gr["d_skip"] = dbias.reshape(-1), dalog.reshape(-1), dd.reshape(-1)
    gr["ssd_norm"] = dssd_gain[0]
    dxbc, dconv_w, dconv_b = _conv_bwd(dxs, dbm, dcm, sv["proj"], p["conv_w"], row(p["conv_b"]), f"conv_bwd{l}")
    gr["conv_w"], gr["conv_b"] = dconv_w, dconv_b[0]
    dq_h, dk_h, dv_h = _att_bwd(sv["aq"], sv["akp"], sv["avp"], bias, dy, f"att_bwd{l}")
    daq, dak, dav, dqg, dkg = _att_prep_bwd(sv["proj"], dq_h, dk_h, dv_h, sv["qg"], sv["kg"], cos, sin, bd, f"att_prep_bwd{l}")
    gr["q_norm"], gr["k_norm"] = dqg[0], dkg[0]
    drq, drk, drv, drg, dret_gain = _ret_bwd(dy, sv["oraw"], sv["proj"], cos, sin, row(p["ret_norm"]), sv["ret_st"], f"ret_bwd{l}")
    gr["ret_norm"] = dret_gain[0]
    ddt_cols = ddtr.transpose(1, 0, 2).reshape(S, SSD_HEADS).astype(BF16)
    dproj = jnp.concatenate([dz, dxbc, daq, dak, dav, drq, drk, drv, drg, ddt_cols,
                             jnp.zeros((S, IN_PAD - IN_WIDTH), BF16)], axis=1)
    gr["w_in"] = _mm(sv["hn"], dproj, "tn", f"in_wgrad{l}", out_dtype=BF16, tm=1024, tn=1152, tk=2048)
    launched = on_all(gr) if on_all is not None else None
    dhn = _mm(dproj, p["w_in"], "nt", f"in_dgrad{l}", tk=1920, after=launched)
    dx0, dx0_bf, dln_mix = _rmsnorm_bwd(sv["x"], dhn, row(p["ln_mix"]), dx1, f"norm_mix_bwd{l}")
    gr["ln_mix"] = dln_mix[0]
    return dx0, dx0_bf, gr


def _local_step(x, tgt, layers, late=None, on_ffn=None, on_all=None):
    n = len(layers)
    none = [None] * n
    late, on_ffn, on_all = late or none, on_ffn or none, on_all or none
    tabs = _tables(x.shape[0])
    saved, params = [], []
    h = x
    for l in range(n):
        p = dict(layers[l](h) if callable(layers[l]) else layers[l])
        h, sv = _layer_fwd(l, h, p, tabs, late[l])
        saved.append(sv)
        params.append(p)
    dh, dh_bf, lacc = _loss_grad(h, tgt, "loss_grad")
    grads = [None] * n
    for l in reversed(range(n)):
        dh, dh_bf, grads[l] = _layer_bwd(l, dh, dh_bf, params[l], saved[l], tabs, on_ffn[l], on_all[l])
    return lacc[0, 0], dh, grads


BIG = ("w_in", "w_out", "w_gate", "w_up", "w_down")
SMALL = ("ln_mix", "conv_b", "dt_bias", "a_log", "d_skip", "ssd_norm", "q_norm", "k_norm", "ret_norm", "ln_ffn")
ORDER = ("ln_mix", "w_in", "conv_w", "conv_b", "dt_bias", "a_log", "d_skip", "ssd_norm", "q_norm", "k_norm", "ret_norm",
         "w_out", "ln_ffn", "w_gate", "w_up", "w_down")


COL_SHARDED = ("w_in", "w_gate", "w_up", "conv_w")


def _full_weight(k, gathered):
    if k in COL_SHARDED:
        full = gathered.transpose(1, 0, 2).reshape(gathered.shape[1], -1)
        return _permute_in(full) if k == "w_in" else full
    return gathered.reshape(-1, gathered.shape[2])


def _shard_block(k, g):
    if k == "w_in":
        g = _unpermute_in(g)
    if k in COL_SHARDED:
        return g.reshape(g.shape[0], N_DEV, -1).transpose(1, 0, 2)
    return g.reshape(N_DEV, -1, g.shape[1])


def kernel(x, ln_mix, w_in, conv_w, conv_b, dt_bias, a_log, d_skip, ssd_norm, q_norm, k_norm, ret_norm, w_out, ln_ffn, w_gate, w_up, w_down, loss_target, m_ln_mix, m_w_in, m_conv_w, m_conv_b, m_dt_bias, m_a_log, m_d_skip, m_ssd_norm, m_q_norm, m_k_norm, m_ret_norm, m_w_out, m_ln_ffn, m_w_gate, m_w_up, m_w_down, v_ln_mix, v_w_in, v_conv_w, v_conv_b, v_dt_bias, v_a_log, v_d_skip, v_ssd_norm, v_q_norm, v_k_norm, v_ret_norm, v_w_out, v_ln_ffn, v_w_gate, v_w_up, v_w_down):
    w = dict(ln_mix=ln_mix, w_in=w_in, conv_w=conv_w, conv_b=conv_b, dt_bias=dt_bias, a_log=a_log, d_skip=d_skip,
             ssd_norm=ssd_norm, q_norm=q_norm, k_norm=k_norm, ret_norm=ret_norm, w_out=w_out, ln_ffn=ln_ffn,
             w_gate=w_gate, w_up=w_up, w_down=w_down)
    m = dict(ln_mix=m_ln_mix, w_in=m_w_in, conv_w=m_conv_w, conv_b=m_conv_b, dt_bias=m_dt_bias, a_log=m_a_log,
             d_skip=m_d_skip, ssd_norm=m_ssd_norm, q_norm=m_q_norm, k_norm=m_k_norm, ret_norm=m_ret_norm, w_out=m_w_out,
             ln_ffn=m_ln_ffn, w_gate=m_w_gate, w_up=m_w_up, w_down=m_w_down)
    v = dict(ln_mix=v_ln_mix, w_in=v_w_in, conv_w=v_conv_w, conv_b=v_conv_b, dt_bias=v_dt_bias, a_log=v_a_log,
             d_skip=v_d_skip, ssd_norm=v_ssd_norm, q_norm=v_q_norm, k_norm=v_k_norm, ret_norm=v_ret_norm, w_out=v_w_out,
             ln_ffn=v_ln_ffn, w_gate=v_w_gate, w_up=v_w_up, w_down=v_w_down)
    me = 4 * lax.axis_index("x") + 2 * lax.axis_index("y") + lax.axis_index("c")

    waves = {"a": [("w_in", 0), ("conv_w", 0), ("conv_w", 1)],
             "b": [(k, 0) for k in ("w_out", "w_gate", "w_up", "w_down")],
             "c": [(k, 1) for k in BIG]}
    gather = {}
    for tag, items in waves.items():
        srcs = [w[k][l] if k == "conv_w" else w[k][l].astype(BF16) for k, l in items]
        gather[tag] = _push_start(srcs, [_landing(s, me) for s in srcs], False, f"gather_{tag}_start")
    started = gather["a"][-1][0, 0] + gather["b"][-1][0, 0] + gather["c"][-1][0, 0]
    full = {}

    def arrive(tag, after):
        for (k, l), g in zip(waves[tag], _push_wait(gather[tag], after, False, f"gather_{tag}_wait")):
            full[k, l] = _full_weight(k, g)

    def layer_weights(l, names):
        return {k: full[k, l] for k in names}

    def small_weights(l):
        return {k: w[k][l] for k in SMALL}

    def layer0(h):
        arrive("a", h)
        p = small_weights(0)
        p["ln_mix"] = p["ln_mix"] + started
        return {**p, **layer_weights(0, ("w_in", "conv_w"))}

    def late0(y):
        arrive("b", y)
        return layer_weights(0, waves_b_names)

    def layer1(h):
        arrive("c", h)
        return {**small_weights(1), **layer_weights(1, BIG + ("conv_w",))}

    waves_b_names = tuple(k for k, _ in waves["b"])

    groups = {"1": [(k, 1) for k in BIG], "0a": [(k, 0) for k in ("w_down", "w_gate", "w_up")],
              "0b": [(k, 0) for k in ("w_out", "w_in")]}
    scatter = {}

    def push_grads(tag, gr):
        blocks = [_shard_block(k, gr[k]) for k, _ in groups[tag]]
        lands = [_landing(lax.dynamic_index_in_dim(b, me, 0, keepdims=False), me) for b in blocks]
        scatter[tag] = _push_start(blocks, lands, True, f"scatter_{tag}_start")
        return scatter[tag][-1]

    loss_part, gx, grads = _local_step(
        x[0], loss_target[0], [layer0, layer1], late=[late0, None],
        on_ffn=[functools.partial(push_grads, "0a"), None],
        on_all=[functools.partial(push_grads, "0b"), functools.partial(push_grads, "1")])
    loss = lax.psum(loss_part, MESH_AXES)

    out = {}
    recv = {}
    for tag, items in groups.items():
        for item, r in zip(items, _push_wait(scatter[tag], gx, True, f"scatter_{tag}_wait")):
            recv[item] = r
    for k in BIG:
        out[k] = _adamw_layers(recv[k, 0], recv[k, 1], w[k], m[k], v[k], f"adamw_{k}")
    names = SMALL + ("conv_w",)
    sizes = [int(np.prod(grads[0][k].shape)) for k in names]
    packed = jnp.concatenate([jnp.stack([grads[l][k] for l in range(DEPTH)]).reshape(-1) for k in names])
    n_small = packed.shape[0]
    rows_small = -(-n_small // 1024) * 8
    pad = lambda t, fill: jnp.concatenate([t, jnp.full((rows_small * 128 - n_small,), fill, F32)]).reshape(rows_small, 128)
    parts = _exchange([pad(packed, 0.0)], False, "gather_small_grads")[0]
    n_rep = DEPTH * sum(sizes[:-1])
    pack_rep = lambda d, fill: pad(jnp.concatenate([d[k].reshape(-1) for k in SMALL]
                                                   + [jnp.full((n_small - n_rep,), fill, F32)]), fill)
    res = _adamw(parts, pack_rep(w, 1.0), pack_rep(m, 1.0), pack_rep(v, 1.0), "adamw_small")
    res = [t.reshape(-1) for t in res]
    off = 0
    for k, sz in zip(SMALL, sizes[:-1]):
        out[k] = [t[off:off + DEPTH * sz].reshape(w[k].shape) for t in res]
        off += DEPTH * sz
    gconv = res[0][off:off + DEPTH * sizes[-1]].reshape(DEPTH, SSD_CONV, SSD_CONV_CH)
    gconv = lax.dynamic_slice_in_dim(gconv, me * conv_w.shape[2], conv_w.shape[2], axis=2)
    flat = lambda t: t.reshape(8, -1)
    resc = _adamw(flat(gconv)[None], flat(conv_w), flat(m_conv_w), flat(v_conv_w), "adamw_conv_w")
    out["conv_w"] = [t.reshape(conv_w.shape) for t in resc]

    return (loss, gx[None], *[out[k][0] for k in ORDER], *[out[k][1] for k in ORDER],
            *[out[k][2] for k in ORDER], *[out[k][3] for k in ORDER])
```

```python
import functools
import math

import jax
import jax.numpy as jnp
import numpy as np
from jax import lax
from jax.experimental import pallas as pl
from jax.experimental.pallas import tpu as pltpu

F32 = jnp.float32
BF16 = jnp.bfloat16

N_DEV = 8
MESH_AXES = ("x", "y", "c")
D_MODEL = 2048
DEPTH = 2
EPS = 1e-6
ROPE_THETA = 10000.0
SSD_HEADS = 16
SSD_HEAD_DIM = 64
SSD_WIDTH = 1024
SSD_GROUPS = 2
SSD_STATE = 128
SSD_CONV = 4
SSD_CONV_CH = 1536
ATT_HEADS = 8
ATT_HEAD_DIM = 64
ATT_WIDTH = 512
DILATED_PAIRS = ((128, 1), (512, 4), (2048, 16))
RET_HEADS = 4
RET_QK_DIM = 64
RET_V_DIM = 128
RET_QK_WIDTH = 256
RET_V_WIDTH = 512
CHUNK = 128
MIX_WIDTH = 2048
ATT_SPAN = 2048
ATT_STRIP = ATT_SPAN + CHUNK
IN_WIDTH = 5648
IN_PAD = 5760
RET_COL, ATT_COL, Z_COL, XBC_COL, DT_COL = 0, 1536, 3072, 4096, 5632
ORIG_Z_XBC, ORIG_DT, ORIG_ATT, ORIG_RET = (0, 2560), (2560, 2576), (2576, 4112), (4112, 5648)
D_FF = 5632
ADAM_LR = 0.001
ADAM_B1 = 0.9
ADAM_B2 = 0.999
ADAM_EPS = 1e-08
ADAM_WD = 0.01
ADAM_STEP = 10
NEG = -1e30
VMEM_LIMIT_V7X = 60 * 1024 * 1024

NN = (((1,), (0,)), ((), ()))
NT = (((1,), (1,)), ((), ()))
TN = (((0,), (0,)), ((), ()))


def _bdot(a, b, dims):
    return lax.dot_general(a.astype(BF16), b.astype(BF16), dims, preferred_element_type=F32)


def _xdot(a, b, dims):
    return lax.dot_general(a, b, dims, precision=lax.Precision.HIGHEST, preferred_element_type=F32)


def _params(*sem):
    return pltpu.CompilerParams(dimension_semantics=sem, vmem_limit_bytes=VMEM_LIMIT_V7X)


def _sigmoid(v):
    return 1.0 / (1.0 + jnp.exp(-v))


def _silu_grad(v, s):
    return s * (1.0 + v * (1.0 - s))


def _rmsnorm_fwd(x, g, name):
    S, D = x.shape
    tr = min(512, S)

    def body(x_ref, g_ref, o_ref):
        xv = x_ref[...]
        r = lax.rsqrt(jnp.mean(xv * xv, axis=-1, keepdims=True) + EPS)
        o_ref[...] = (xv * r * g_ref[...]).astype(o_ref.dtype)

    return pl.pallas_call(
        body, grid=(S // tr,),
        in_specs=[pl.BlockSpec((tr, D), lambda i: (i, 0)), pl.BlockSpec((1, D), lambda i: (0, 0))],
        out_specs=pl.BlockSpec((tr, D), lambda i: (i, 0)),
        out_shape=jax.ShapeDtypeStruct((S, D), BF16), name=name, compiler_params=_params("parallel"),
    )(x, g)


def _rmsnorm_bwd(x, dy, g, dres, name):
    S, D = x.shape
    tr = min(512, S)

    def body(x_ref, dy_ref, g_ref, dres_ref, dx_ref, dxb_ref, dg_ref):
        i = pl.program_id(0)
        xv = x_ref[...]
        r = lax.rsqrt(jnp.mean(xv * xv, axis=-1, keepdims=True) + EPS)
        n = xv * r
        dy = dy_ref[...]
        dn = dy * g_ref[...]
        dx = dres_ref[...] + r * (dn - n * jnp.mean(dn * n, axis=-1, keepdims=True))
        dx_ref[...] = dx
        dxb_ref[...] = dx.astype(BF16)
        part = jnp.sum(dy * n, axis=0, keepdims=True)

        @pl.when(i == 0)
        def _():
            dg_ref[...] = part

        @pl.when(i > 0)
        def _():
            dg_ref[...] += part

    row = pl.BlockSpec((tr, D), lambda i: (i, 0))
    vec = pl.BlockSpec((1, D), lambda i: (0, 0))
    return pl.pallas_call(
        body, grid=(S // tr,), in_specs=[row, row, vec, row], out_specs=[row, row, vec],
        out_shape=[jax.ShapeDtypeStruct((S, D), F32), jax.ShapeDtypeStruct((S, D), BF16), jax.ShapeDtypeStruct((1, D), F32)],
        name=name, compiler_params=_params("arbitrary"),
    )(x, dy, g, dres)


def _loss_grad(y, tgt, name):
    S, D = y.shape
    tr = min(512, S)

    def body(y_ref, t_ref, dy_ref, dyb_ref, l_ref):
        i = pl.program_id(0)
        err = y_ref[...] - t_ref[...]
        dy = err * (1.0 / D)
        dy_ref[...] = dy
        dyb_ref[...] = dy.astype(BF16)
        part = jnp.sum(jnp.sum(err * err, axis=1, keepdims=True), axis=0, keepdims=True) * (0.5 / D)

        @pl.when(i == 0)
        def _():
            l_ref[...] = jnp.zeros_like(l_ref)

        l_ref[...] += part

    row = pl.BlockSpec((tr, D), lambda i: (i, 0))
    return pl.pallas_call(
        body, grid=(S // tr,), in_specs=[row, row],
        out_specs=[row, row, pl.BlockSpec((8, 128), lambda i: (0, 0))],
        out_shape=[jax.ShapeDtypeStruct((S, D), F32), jax.ShapeDtypeStruct((S, D), BF16), jax.ShapeDtypeStruct((8, 128), F32)],
        name=name, compiler_params=_params("arbitrary"),
    )(y, tgt)


def _pick(n, cands):
    for c in cands:
        if n % c == 0:
            return c
    return n


def _mm(a, b, mode, name, out_dtype=F32, residual=None, tm=None, tn=None, tk=None, after=None):
    if mode == "nn":
        (M, K), (_, N) = a.shape, b.shape
    elif mode == "nt":
        (M, K), (N, _) = a.shape, b.shape
    else:
        (K, M), (_, N) = a.shape, b.shape
    tm = min(tm, M) if tm else _pick(M, (1024, 512, 256, 128))
    tn = min(tn, N) if tn else _pick(N, (1024, 1152, 1408, 512, 256, 128))
    tk = min(tk, K) if tk else _pick(K, (2048, 1920, 1408, 1024, 512, 256, 128))
    assert M % tm == 0 and N % tn == 0 and K % tk == 0, (name, M, N, K, tm, tn, tk)
    nk = K // tk
    a_spec = pl.BlockSpec((tk, tm), lambda i, j, k: (k, i)) if mode == "tn" else pl.BlockSpec((tm, tk), lambda i, j, k: (i, k))
    b_spec = pl.BlockSpec((tn, tk), lambda i, j, k: (j, k)) if mode == "nt" else pl.BlockSpec((tk, tn), lambda i, j, k: (k, j))
    o_spec = pl.BlockSpec((tm, tn), lambda i, j, k: (i, j))
    dims = {"nn": NN, "nt": NT, "tn": TN}[mode]
    has_res = residual is not None

    has_after = after is not None

    def body(*refs):
        a_ref, b_ref = refs[0], refs[1]
        r_ref = refs[2] if has_res else None
        o_ref = refs[2 + has_res + has_after]
        p = _bdot(a_ref[...], b_ref[...], dims)

        def finish(acc):
            if has_res:
                acc = acc + r_ref[...]
            o_ref[...] = acc.astype(o_ref.dtype)

        if nk == 1:
            finish(p)
        else:
            acc_ref = refs[-1]
            k = pl.program_id(2)

            @pl.when(k == 0)
            def _():
                acc_ref[...] = p

            @pl.when(k > 0)
            def _():
                acc_ref[...] += p

            @pl.when(k == nk - 1)
            def _():
                finish(acc_ref[...])

    ins = [a, b] + ([residual] if has_res else []) + ([after] if has_after else [])
    in_specs = [a_spec, b_spec] + ([o_spec] if has_res else []) + ([pl.BlockSpec(memory_space=pl.ANY)] if has_after else [])
    scratch = [pltpu.VMEM((tm, tn), F32)] if nk > 1 else []
    return pl.pallas_call(
        body, grid=(M // tm, N // tn, nk), in_specs=in_specs, out_specs=o_spec,
        out_shape=jax.ShapeDtypeStruct((M, N), out_dtype), scratch_shapes=scratch, name=name,
        compiler_params=_params("parallel", "parallel", "arbitrary"),
    )(*ins)


def _accumulate(acc_ref, p, k, nk, finish):
    @pl.when(k == 0)
    def _():
        acc_ref[...] = p

    @pl.when(k > 0)
    def _():
        acc_ref[...] += p

    @pl.when(k == nk - 1)
    def _():
        finish(acc_ref[...])


def _swiglu_fwd(hn, wg, wu, name):
    S, K = hn.shape
    F = wg.shape[1]
    tm = _pick(S, (1024, 512))
    tn = _pick(F, (512, 256, 128))

    def body(a_ref, wg_ref, wu_ref, g_ref, u_ref, act_ref):
        a = a_ref[...]
        g = _bdot(a, wg_ref[...], NN)
        u = _bdot(a, wu_ref[...], NN)
        g_ref[...] = g.astype(BF16)
        u_ref[...] = u.astype(BF16)
        act_ref[...] = (g * _sigmoid(g) * u).astype(BF16)

    w_spec = pl.BlockSpec((K, tn), lambda i, j: (0, j))
    o_spec = pl.BlockSpec((tm, tn), lambda i, j: (i, j))
    sh = jax.ShapeDtypeStruct((S, F), BF16)
    return pl.pallas_call(
        body, grid=(S // tm, F // tn), in_specs=[pl.BlockSpec((tm, K), lambda i, j: (i, 0)), w_spec, w_spec],
        out_specs=[o_spec, o_spec, o_spec], out_shape=[sh, sh, sh], name=name,
        compiler_params=_params("parallel", "parallel"),
    )(hn, wg, wu)


def _swiglu_bwd(dx, wd, g, u, name):
    S, K = dx.shape
    F = wd.shape[0]
    tm = _pick(S, (1024, 512))
    tn = _pick(F, (512, 256, 128))

    def body(dx_ref, wd_ref, g_ref, u_ref, dg_ref, du_ref):
        da = _bdot(dx_ref[...], wd_ref[...], NT)
        gv = g_ref[...].astype(F32)
        uv = u_ref[...].astype(F32)
        s = _sigmoid(gv)
        dg_ref[...] = (da * uv * _silu_grad(gv, s)).astype(BF16)
        du_ref[...] = (da * gv * s).astype(BF16)

    o_spec = pl.BlockSpec((tm, tn), lambda i, j: (i, j))
    sh = jax.ShapeDtypeStruct((S, F), BF16)
    return pl.pallas_call(
        body, grid=(S // tm, F // tn),
        in_specs=[pl.BlockSpec((tm, K), lambda i, j: (i, 0)), pl.BlockSpec((tn, K), lambda i, j: (j, 0)), o_spec, o_spec],
        out_specs=[o_spec, o_spec], out_shape=[sh, sh], name=name, compiler_params=_params("parallel", "parallel"),
    )(dx, wd, g, u)


def _mm_nt2(a1, b1, a2, b2, name):
    M, K = a1.shape
    N = b1.shape[0]
    tm = _pick(M, (1024, 512))
    tn = _pick(N, (1024, 512))
    tk = _pick(K, (1408, 1024, 512, 256, 128))
    nk = K // tk

    def body(a1_ref, b1_ref, a2_ref, b2_ref, o_ref, acc_ref):
        def finish(acc):
            o_ref[...] = acc

        p = _bdot(a1_ref[...], b1_ref[...], NT) + _bdot(a2_ref[...], b2_ref[...], NT)
        _accumulate(acc_ref, p, pl.program_id(2), nk, finish)

    a_spec = pl.BlockSpec((tm, tk), lambda i, j, k: (i, k))
    b_spec = pl.BlockSpec((tn, tk), lambda i, j, k: (j, k))
    return pl.pallas_call(
        body, grid=(M // tm, N // tn, nk), in_specs=[a_spec, b_spec, a_spec, b_spec],
        out_specs=pl.BlockSpec((tm, tn), lambda i, j, k: (i, j)), out_shape=jax.ShapeDtypeStruct((M, N), F32),
        scratch_shapes=[pltpu.VMEM((tm, tn), F32)], name=name,
        compiler_params=_params("parallel", "parallel", "arbitrary"),
    )(a1, b1, a2, b2)


XBC_BLK0 = XBC_COL // 128


def _conv_fwd(proj, w, b, name):
    S = proj.shape[0]
    T = min(512, S)

    def body(x_ref, w_ref, b_ref, o_ref, xp_ref):
        xp_ref[pl.ds(0, 8), :] = jnp.zeros((8, 128), F32)
        xp_ref[pl.ds(8, S), :] = x_ref[...]
        wv = w_ref[...]
        bv = b_ref[...]

        def step(c, carry):
            base = pl.multiple_of(c * T, T)
            acc = wv[0:1] * xp_ref[pl.ds(base + 5, T), :]
            for i in range(1, SSD_CONV):
                acc = acc + wv[i:i + 1] * xp_ref[pl.ds(base + 5 + i, T), :]
            acc = bv + acc
            o_ref[pl.ds(base, T), :] = acc * _sigmoid(acc)
            return carry

        lax.fori_loop(0, S // T, step, 0)

    return pl.pallas_call(
        body, grid=(SSD_CONV_CH // 128,),
        in_specs=[pl.BlockSpec((S, 128), lambda j: (0, XBC_BLK0 + j)), pl.BlockSpec((SSD_CONV, 128), lambda j: (0, j)),
                  pl.BlockSpec((1, 128), lambda j: (0, j))],
        out_specs=pl.BlockSpec((S, 128), lambda j: (0, j)),
        out_shape=jax.ShapeDtypeStruct((S, SSD_CONV_CH), F32),
        scratch_shapes=[pltpu.VMEM((S + 8, 128), F32)], name=name, compiler_params=_params("parallel"),
    )(proj, w, b)


def _conv_bwd(dxs, dbm, dcm, proj, w, b, dproj, name):
    S = proj.shape[0]
    T = min(512, S)
    NX, NB = SSD_WIDTH // 128, SSD_GROUPS * SSD_STATE // 128

    def body(dxs_ref, dbm_ref, dcm_ref, x_ref, w_ref, b_ref, dproj_ref, dx_ref, dw_ref, db_ref, xp_ref, dcp_ref):
        j = pl.program_id(0)

        @pl.when(j < NX)
        def _():
            dcp_ref[pl.ds(0, S), :] = dxs_ref[...]

        @pl.when((j >= NX) & (j < NX + NB))
        def _():
            dcp_ref[pl.ds(0, S), :] = dbm_ref[...]

        @pl.when(j >= NX + NB)
        def _():
            dcp_ref[pl.ds(0, S), :] = dcm_ref[...]

        da_ref = dcp_ref
        xp_ref[pl.ds(0, 8), :] = jnp.zeros((8, 128), F32)
        xp_ref[pl.ds(8, S), :] = x_ref[...]
        dcp_ref[pl.ds(S, 8), :] = jnp.zeros((8, 128), F32)
        wv = w_ref[...]
        bv = b_ref[...]

        def step1(c, carry):
            base = pl.multiple_of(c * T, T)
            xs = [xp_ref[pl.ds(base + 5 + i, T), :] for i in range(SSD_CONV)]
            acc = wv[0:1] * xs[0]
            for i in range(1, SSD_CONV):
                acc = acc + wv[i:i + 1] * xs[i]
            acc = bv + acc
            s = _sigmoid(acc)
            dc = da_ref[pl.ds(base, T), :] * _silu_grad(acc, s)
            dcp_ref[pl.ds(base, T), :] = dc
            new = tuple(carry[i] + jnp.sum(xs[i] * dc, axis=0, keepdims=True) for i in range(SSD_CONV))
            return new + (carry[SSD_CONV] + jnp.sum(dc, axis=0, keepdims=True),)

        z = jnp.zeros((1, 128), F32)
        res = lax.fori_loop(0, S // T, step1, (z,) * (SSD_CONV + 1))
        for i in range(SSD_CONV):
            dw_ref[pl.ds(i, 1), :] = res[i]
        db_ref[...] = res[SSD_CONV]

        def step2(c, carry):
            base = pl.multiple_of(c * T, T)
            acc = wv[0:1] * dcp_ref[pl.ds(base + 3, T), :]
            for i in range(1, SSD_CONV):
                acc = acc + wv[i:i + 1] * dcp_ref[pl.ds(base + 3 - i, T), :]
            dx_ref[pl.ds(base, T), :] = acc.astype(dx_ref.dtype)
            return carry

        lax.fori_loop(0, S // T, step2, 0)

    clamp = lambda j, lo, n: jnp.clip(j - lo, 0, n - 1)
    return pl.pallas_call(
        body, grid=(SSD_CONV_CH // 128,),
        in_specs=[pl.BlockSpec((S, 128), lambda j: (0, clamp(j, 0, NX))), pl.BlockSpec((S, 128), lambda j: (0, clamp(j, NX, NB))),
                  pl.BlockSpec((S, 128), lambda j: (0, clamp(j, NX + NB, NB))),
                  pl.BlockSpec((S, 128), lambda j: (0, XBC_BLK0 + j)), pl.BlockSpec((SSD_CONV, 128), lambda j: (0, j)),
                  pl.BlockSpec((1, 128), lambda j: (0, j)), pl.BlockSpec(memory_space=pl.ANY)],
        out_specs=[pl.BlockSpec((S, 128), lambda j: (0, XBC_BLK0 + j)), pl.BlockSpec((SSD_CONV, 128), lambda j: (0, j)),
                   pl.BlockSpec((1, 128), lambda j: (0, j))],
        out_shape=[jax.ShapeDtypeStruct(dproj.shape, dproj.dtype), jax.ShapeDtypeStruct((SSD_CONV, SSD_CONV_CH), F32),
                   jax.ShapeDtypeStruct((1, SSD_CONV_CH), F32)],
        input_output_aliases={6: 0},
        scratch_shapes=[pltpu.VMEM((S + 8, 128), F32), pltpu.VMEM((S + 8, 128), F32)], name=name,
        compiler_params=_params("arbitrary"),
    )(dxs, dbm, dcm, proj, w, b, dproj)


HPG = SSD_HEADS // SSD_GROUPS
GW = HPG * SSD_HEAD_DIM


def _ssd_chunk_terms(dtr, bias, alog, tril, triu):
    pre = dtr + bias
    dt = jnp.maximum(pre, 0.0) + jnp.log(1.0 + jnp.exp(-jnp.abs(pre)))
    a_neg = -jnp.exp(alog)
    a = dt * a_neg
    acum = _xdot(tril, a, NN)
    acum_t = _xdot(a, triu, TN)
    return pre, dt, a_neg, acum, acum_t


def _head_expanders():
    h64 = lax.broadcasted_iota(jnp.int32, (HPG, GW), 0) == lax.broadcasted_iota(jnp.int32, (HPG, GW), 1) // SSD_HEAD_DIM
    h128 = lax.broadcasted_iota(jnp.int32, (HPG, HPG * CHUNK), 0) == lax.broadcasted_iota(jnp.int32, (HPG, HPG * CHUNK), 1) // CHUNK
    return h64.astype(F32), h128.astype(F32)


def _ssd_fwd(xbc, proj, dtr, dt_bias, a_log, d_rep, gain, name):
    S = xbc.shape[0]
    L = CHUNK
    T = min(512, S)
    CPS = T // L
    NC = S // L

    def body(x_ref, b_ref, c_ref, z_ref, dtr_ref, bias_ref, alog_ref, d_ref, gain_ref, y_ref, yraw_ref, st_ref, state):
        i = pl.program_id(1)

        @pl.when(i == 0)
        def _():
            state[...] = jnp.zeros_like(state)

        row = lax.broadcasted_iota(jnp.int32, (L, L), 0)
        col = lax.broadcasted_iota(jnp.int32, (L, L), 1)
        causal = row >= col
        tril = causal.astype(F32)
        triu = (row <= col).astype(F32)
        low = col < SSD_HEAD_DIM
        e64, e128 = _head_expanders()
        for c in range(CPS):
            rows = pl.ds(c * L, L)
            xv = x_ref[rows, :]
            bm = b_ref[rows, :]
            cm = c_ref[rows, :]
            _, dt, _, acum, acum_t = _ssd_chunk_terms(dtr_ref[rows, :], bias_ref[...], alog_ref[...], tril, triu)
            ac = _xdot(acum, e64, NN)
            ac_sq = _xdot(acum, e128, NN)
            xd = xv * _xdot(dt, e64, NN)
            ac_last = ac[L - 1:L, :]
            sp = state[...]
            st_ref[c] = sp
            yoff = _bdot(cm, sp, NN) * jnp.exp(ac)
            state[...] = sp * jnp.exp(ac_last) + _bdot(bm, xd * jnp.exp(ac_last - ac), TN)
            gmat = _bdot(cm, bm, NT)
            for q in range(HPG // 2):
                pair = slice(q * 128, (q + 1) * 128)
                tile = xd[:, pair]
                y = yoff[:, pair]
                for j, keep in ((2 * q, low), (2 * q + 1, ~low)):
                    lam = jnp.exp(jnp.where(causal, ac_sq[:, j * L:(j + 1) * L] - acum_t[j:j + 1, :], NEG))
                    y = y + _bdot(gmat * lam, jnp.where(keep, tile, 0.0), NN)
                yraw_ref[rows, pair] = y
            zz = z_ref[rows, :]
            u = (yraw_ref[rows, :] + xv * d_ref[...]) * (zz * _sigmoid(zz))
            r = lax.rsqrt(jnp.mean(u * u, axis=-1, keepdims=True) + EPS)
            y_ref[rows, :] = (u * r * gain_ref[...]).astype(y_ref.dtype)

    vec8 = pl.BlockSpec((None, 1, HPG), lambda g, i: (g, 0, 0))
    return pl.pallas_call(
        body, grid=(SSD_GROUPS, S // T),
        in_specs=[pl.BlockSpec((T, GW), lambda g, i: (i, g)),
                  pl.BlockSpec((T, SSD_STATE), lambda g, i: (i, SSD_WIDTH // SSD_STATE + g)),
                  pl.BlockSpec((T, SSD_STATE), lambda g, i: (i, SSD_WIDTH // SSD_STATE + SSD_GROUPS + g)),
                  pl.BlockSpec((T, GW), lambda g, i: (i, Z_COL // GW + g)),
                  pl.BlockSpec((None, T, HPG), lambda g, i: (g, i, 0)),
                  vec8, vec8,
                  pl.BlockSpec((1, GW), lambda g, i: (0, g)), pl.BlockSpec((1, GW), lambda g, i: (0, g))],
        out_specs=[pl.BlockSpec((T, GW), lambda g, i: (i, g)), pl.BlockSpec((T, GW), lambda g, i: (i, g)),
                   pl.BlockSpec((CPS, None, SSD_STATE, GW), lambda g, i: (i, g, 0, 0))],
        out_shape=[jax.ShapeDtypeStruct((S, MIX_WIDTH), BF16), jax.ShapeDtypeStruct((S, SSD_WIDTH), F32),
                   jax.ShapeDtypeStruct((NC, SSD_GROUPS, SSD_STATE, GW), F32)],
        scratch_shapes=[pltpu.VMEM((SSD_STATE, GW), F32)], name=name,
        compiler_params=_params("arbitrary", "arbitrary"),
    )(xbc, xbc, xbc, proj, dtr, dt_bias, a_log, d_rep, gain)


def _ssd_bwd(dy, yraw, xbc, proj, dtr, dt_bias, a_log, d_rep, gain, states, name):
    S = xbc.shape[0]
    L = CHUNK
    T = min(512, S)
    CPS = T // L
    NI = S // T

    def body(dy_ref, yraw_ref, x_ref, b_ref, c_ref, z_ref, dtr_ref, bias_ref, alog_ref, d_ref, gain_ref, st_ref,
             dz_ref, dx_ref, db_ref, dc_ref, ddtr_ref, dbias_ref, dalog_ref, dd_ref, dgain_ref, dstate, dxd_ref):
        i = pl.program_id(1)

        @pl.when(i == 0)
        def _():
            dstate[...] = jnp.zeros_like(dstate)
            dbias_ref[...] = jnp.zeros_like(dbias_ref)
            dalog_ref[...] = jnp.zeros_like(dalog_ref)
            dd_ref[...] = jnp.zeros_like(dd_ref)
            dgain_ref[...] = jnp.zeros_like(dgain_ref)

        row = lax.broadcasted_iota(jnp.int32, (L, L), 0)
        col = lax.broadcasted_iota(jnp.int32, (L, L), 1)
        causal = row >= col
        tril = causal.astype(F32)
        triu = (row <= col).astype(F32)
        low = col < SSD_HEAD_DIM
        e64, e128 = _head_expanders()
        lane8 = lax.broadcasted_iota(jnp.int32, (1, HPG), 1)
        sub8 = lax.broadcasted_iota(jnp.int32, (HPG, 1), 0)
        eye8 = (lax.broadcasted_iota(jnp.int32, (HPG, HPG), 0) == lax.broadcasted_iota(jnp.int32, (HPG, HPG), 1)).astype(F32)
        last_row = (lax.broadcasted_iota(jnp.int32, (L, 1), 0) == L - 1).astype(F32)
        for c in reversed(range(CPS)):
            rows = pl.ds(c * L, L)
            xv = x_ref[rows, :]
            bm = b_ref[rows, :]
            cm = c_ref[rows, :]
            zz = z_ref[rows, :]
            dvec = d_ref[...]
            sz = _sigmoid(zz)
            silu_z = zz * sz
            v = yraw_ref[rows, :] + xv * dvec
            u = v * silu_z
            r = lax.rsqrt(jnp.mean(u * u, axis=-1, keepdims=True) + EPS)
            n = u * r
            do = dy_ref[rows, :]
            dgain_ref[...] += jnp.sum(do * n, axis=0, keepdims=True)
            dn = do * gain_ref[...]
            du = r * (dn - n * jnp.mean(dn * n, axis=-1, keepdims=True))
            dz_ref[rows, :] = (du * v * _silu_grad(zz, sz)).astype(dz_ref.dtype)
            dyv = du * silu_z
            dd_ref[...] += _xdot(jnp.sum(dyv * xv, axis=0, keepdims=True), e64, NT)
            pre, dt, a_neg, acum, acum_t = _ssd_chunk_terms(dtr_ref[rows, :], bias_ref[...], alog_ref[...], tril, triu)
            ac = _xdot(acum, e64, NN)
            ac_sq = _xdot(acum, e128, NN)
            dt_w = _xdot(dt, e64, NN)
            xd = xv * dt_w
            ac_last = ac[L - 1:L, :]
            ea = jnp.exp(ac)
            w = jnp.exp(ac_last - ac)
            ea_last = jnp.exp(ac_last)
            sp = st_ref[c]
            ds = dstate[...]
            dye = dyv * ea
            yoff = _bdot(cm, sp, NN) * ea
            bds = _bdot(bm, ds, NN)
            dcm = _bdot(dye, sp, NT)
            dbm = _bdot(xd * w, ds, NT)
            dstate[...] = ds * ea_last + _bdot(cm, dye, TN)
            w8 = jnp.exp(acum[L - 1:L, :] - acum)
            dw8 = _xdot(xd * bds, e64, NT)
            dac8 = _xdot(dyv * yoff, e64, NT) - dw8 * w8
            tail8 = jnp.sum(dw8 * w8, axis=0, keepdims=True) + jnp.exp(acum[L - 1:L, :]) * _xdot(
                jnp.sum(ds * sp, axis=0, keepdims=True), e64, NT)
            dac8 = dac8 + last_row * tail8
            gmat = _bdot(cm, bm, NT)
            dgmat = jnp.zeros((L, L), F32)
            colsum_t = jnp.zeros((HPG, L), F32)
            for q in range(HPG // 2):
                pair = slice(q * 128, (q + 1) * 128)
                xd_tile = xd[:, pair]
                dy_tile = dyv[:, pair]
                dxd_tile = bds[:, pair] * w[:, pair]
                for j, keep in ((2 * q, low), (2 * q + 1, ~low)):
                    lam = jnp.exp(jnp.where(causal, ac_sq[:, j * L:(j + 1) * L] - acum_t[j:j + 1, :], NEG))
                    mh = gmat * lam
                    dyj = jnp.where(keep, dy_tile, 0.0)
                    dxd_tile = dxd_tile + _bdot(mh, dyj, TN)
                    dm = _bdot(dyj, xd_tile, NT)
                    dgmat = dgmat + dm * lam
                    qm = dm * mh
                    dac8 = dac8 + jnp.sum(qm, axis=1, keepdims=True) * (lane8 == j).astype(F32)
                    colsum_t = colsum_t + (sub8 == j).astype(F32) * jnp.sum(qm, axis=0, keepdims=True)
                dxd_ref[:, pair] = dxd_tile
            dac8 = dac8 - _xdot(colsum_t, eye8, TN)
            dxd = dxd_ref[...]
            dx_ref[rows, :] = dxd * dt_w + dyv * dvec
            dc_ref[rows, :] = dcm + _bdot(dgmat, bm, NN)
            db_ref[rows, :] = dbm + _bdot(dgmat, cm, TN)
            da8 = _xdot(triu, dac8, NN)
            ddt8 = _xdot(dxd * xv, e64, NT) + da8 * a_neg
            dalog_ref[...] += jnp.sum(da8 * dt, axis=0, keepdims=True) * a_neg
            dpre = ddt8 * _sigmoid(pre)
            ddtr_ref[rows, :] = dpre
            dbias_ref[...] += jnp.sum(dpre, axis=0, keepdims=True)

    rev = lambda i: NI - 1 - i
    vec8 = pl.BlockSpec((None, 1, HPG), lambda g, i: (g, 0, 0))
    grp = pl.BlockSpec((T, GW), lambda g, i: (rev(i), g))
    bspec = pl.BlockSpec((T, SSD_STATE), lambda g, i: (rev(i), SSD_WIDTH // SSD_STATE + g))
    cspec = pl.BlockSpec((T, SSD_STATE), lambda g, i: (rev(i), SSD_WIDTH // SSD_STATE + SSD_GROUPS + g))
    gvec = pl.BlockSpec((1, GW), lambda g, i: (0, g))
    st_spec = pl.BlockSpec((CPS, None, SSD_STATE, GW), lambda g, i: (rev(i), g, 0, 0))
    small = jax.ShapeDtypeStruct((SSD_GROUPS, 1, HPG), F32)
    zspec = pl.BlockSpec((T, GW), lambda g, i: (rev(i), Z_COL // GW + g))
    return pl.pallas_call(
        body, grid=(SSD_GROUPS, NI),
        in_specs=[grp, grp, grp, bspec, cspec, zspec, pl.BlockSpec((None, T, HPG), lambda g, i: (g, rev(i), 0)),
                  vec8, vec8, gvec, gvec, st_spec],
        out_specs=[zspec, grp, pl.BlockSpec((T, SSD_STATE), lambda g, i: (rev(i), g)),
                   pl.BlockSpec((T, SSD_STATE), lambda g, i: (rev(i), g)),
                   pl.BlockSpec((None, T, HPG), lambda g, i: (g, rev(i), 0)), vec8, vec8, vec8, gvec],
        out_shape=[jax.ShapeDtypeStruct((S, IN_PAD), BF16), jax.ShapeDtypeStruct((S, SSD_WIDTH), F32),
                   jax.ShapeDtypeStruct((S, SSD_GROUPS * SSD_STATE), F32), jax.ShapeDtypeStruct((S, SSD_GROUPS * SSD_STATE), F32),
                   jax.ShapeDtypeStruct((SSD_GROUPS, S, HPG), F32), small, small, small,
                   jax.ShapeDtypeStruct((1, SSD_WIDTH), F32)],
        scratch_shapes=[pltpu.VMEM((SSD_STATE, GW), F32), pltpu.VMEM((L, GW), F32)],
        name=name, compiler_params=_params("arbitrary", "arbitrary"),
    )(dy, yraw, xbc, xbc, xbc, proj, dtr, dt_bias, a_log, d_rep, gain, states)


def _swap_halves(t):
    w = t.shape[1]
    lane = lax.broadcasted_iota(jnp.int32, t.shape, 1)
    return jnp.where((lane % 64) < 32, pltpu.roll(t, w - 32, axis=1), pltpu.roll(t, 32, axis=1))


def _widen(tab, w):
    return tab if w == 128 else jnp.concatenate([tab] * (w // 128), axis=1)


def _rope(t, cos, sin_signed):
    return t * cos + _swap_halves(t) * sin_signed


def _rope_t(d, cos, sin_signed):
    return d * cos - _swap_halves(d) * sin_signed


def _group_sum64(v, bd):
    hi = v.astype(BF16)
    lo = (v - hi.astype(F32)).astype(BF16)
    return (lax.dot_general(hi, bd, NN, preferred_element_type=F32)
            + lax.dot_general(lo, bd, NN, preferred_element_type=F32))


AQ_BLK = ATT_COL // ATT_WIDTH


def _att_prep_fwd(proj, qg, kg, cos, sin, bd, name):
    S = proj.shape[0]
    T = min(512, S)
    PB = ATT_SPAN // T
    src = lambda i: jnp.maximum(i - PB, 0)

    def body(q_ref, k_ref, v_ref, qg_ref, kg_ref, cos_ref, sin_ref, bd_ref, qo_ref, ko_ref, vo_ref):
        i = pl.program_id(0)

        @pl.when(i < PB)
        def _():
            ko_ref[...] = jnp.zeros_like(ko_ref)
            vo_ref[...] = jnp.zeros_like(vo_ref)

        @pl.when(i >= PB)
        def _():
            cw = _widen(cos_ref[...], ATT_WIDTH)
            sw = _widen(sin_ref[...], ATT_WIDTH)
            bdv = bd_ref[...]

            def norm_rope(t, gain):
                ss = _group_sum64(t * t, bdv)
                return _rope(t * lax.rsqrt(ss * (1.0 / ATT_HEAD_DIM) + EPS) * gain, cw, sw)

            qo_ref[...] = (norm_rope(q_ref[...], qg_ref[...]) * (ATT_HEAD_DIM ** -0.5)).astype(BF16)
            kt = norm_rope(k_ref[...], kg_ref[...]).astype(BF16)
            vt = v_ref[...].astype(BF16)
            for pr in range(ATT_HEADS // 2):
                ko_ref[pr] = kt[:, pr * 128:(pr + 1) * 128]
                vo_ref[pr] = vt[:, pr * 128:(pr + 1) * 128]

    vec = pl.BlockSpec((1, ATT_WIDTH), lambda i: (0, 0))
    tab = pl.BlockSpec((T, 128), lambda i: (src(i), 0))
    hm = pl.BlockSpec((ATT_HEADS // 2, T, 128), lambda i: (0, i, 0))
    hm_shape = jax.ShapeDtypeStruct((ATT_HEADS // 2, ATT_SPAN + S, 128), BF16)
    return pl.pallas_call(
        body, grid=(PB + S // T,),
        in_specs=[pl.BlockSpec((T, ATT_WIDTH), lambda i: (src(i), AQ_BLK)), pl.BlockSpec((T, ATT_WIDTH), lambda i: (src(i), AQ_BLK + 1)),
                  pl.BlockSpec((T, ATT_WIDTH), lambda i: (src(i), AQ_BLK + 2)), vec, vec, tab, tab,
                  pl.BlockSpec((ATT_WIDTH, ATT_WIDTH), lambda i: (0, 0))],
        out_specs=[pl.BlockSpec((T, ATT_WIDTH), lambda i: (src(i), 0)), hm, hm],
        out_shape=[jax.ShapeDtypeStruct((S, ATT_WIDTH), BF16), hm_shape, hm_shape],
        name=name, compiler_params=_params("arbitrary"),
    )(proj, proj, proj, qg, kg, cos, sin, bd)


def _att_prep_bwd(proj, dq, dk_p, dv_p, qg, kg, cos, sin, bd, dproj, name):
    S = proj.shape[0]
    T = min(512, S)
    NI = S // T
    PB = ATT_SPAN // T
    W = ATT_WIDTH

    def body(q_ref, k_ref, dq_ref, dkp_ref, dvp_ref, qg_ref, kg_ref, cos_ref, sin_ref, bd_ref, dproj_ref,
             do_ref, dqg_ref, dkg_ref, acc_ref):
        i = pl.program_id(0)

        @pl.when(i == 0)
        def _():
            acc_ref[...] = jnp.zeros_like(acc_ref)

        npair = ATT_HEADS // 2
        dk_all = jnp.concatenate([dkp_ref[pr] for pr in range(npair)], axis=1)
        do_ref[:, 2 * W:3 * W] = jnp.concatenate([dvp_ref[pr] for pr in range(npair)], axis=1).astype(BF16)
        cw = _widen(cos_ref[...], ATT_WIDTH)
        sw = _widen(sin_ref[...], ATT_WIDTH)
        bdv = bd_ref[...]

        def one(t, d_rot, gain, scale, slot):
            ss = _group_sum64(t * t, bdv)
            r = lax.rsqrt(ss * (1.0 / ATT_HEAD_DIM) + EPS)
            n = t * r
            d_ng = _rope_t(d_rot * scale, cw, sw)
            acc_ref[pl.ds(slot, 1), :] += jnp.sum(d_ng * n, axis=0, keepdims=True)
            dn = d_ng * gain
            return r * (dn - n * (_group_sum64(dn * n, bdv) * (1.0 / ATT_HEAD_DIM)))

        do_ref[:, 0:W] = one(q_ref[...], dq_ref[...], qg_ref[...], ATT_HEAD_DIM ** -0.5, 0).astype(BF16)
        do_ref[:, W:2 * W] = one(k_ref[...], dk_all, kg_ref[...], 1.0, 1).astype(BF16)

        @pl.when(i == NI - 1)
        def _():
            a = acc_ref[...]
            f = a[:, 0:64]
            for h in range(1, ATT_HEADS):
                f = f + a[:, h * 64:(h + 1) * 64]
            dqg_ref[...] = f[0:1]
            dkg_ref[...] = f[1:2]

    vec = pl.BlockSpec((1, ATT_WIDTH), lambda i: (0, 0))
    tab = pl.BlockSpec((T, 128), lambda i: (i, 0))
    row = pl.BlockSpec((T, ATT_WIDTH), lambda i: (i, 0))
    g64 = pl.BlockSpec((1, ATT_HEAD_DIM), lambda i: (0, 0))
    padded = pl.BlockSpec((ATT_HEADS // 2, T, 128), lambda i: (0, i + PB, 0))
    return pl.pallas_call(
        body, grid=(NI,),
        in_specs=[pl.BlockSpec((T, ATT_WIDTH), lambda i: (i, AQ_BLK)), pl.BlockSpec((T, ATT_WIDTH), lambda i: (i, AQ_BLK + 1)),
                  row, padded, padded, vec, vec, tab, tab, pl.BlockSpec((ATT_WIDTH, ATT_WIDTH), lambda i: (0, 0)),
                  pl.BlockSpec(memory_space=pl.ANY)],
        out_specs=[pl.BlockSpec((T, 3 * W), lambda i: (i, ATT_COL // (3 * W))), g64, g64],
        out_shape=[jax.ShapeDtypeStruct(dproj.shape, dproj.dtype), jax.ShapeDtypeStruct((1, ATT_HEAD_DIM), F32),
                   jax.ShapeDtypeStruct((1, ATT_HEAD_DIM), F32)],
        input_output_aliases={10: 0},
        scratch_shapes=[pltpu.VMEM((8, ATT_WIDTH), F32)], name=name, compiler_params=_params("arbitrary"),
    )(proj, proj, dq, dk_p, dv_p, qg, kg, cos, sin, bd, dproj)


def _att_bias():
    qpos = np.arange(CHUNK)[:, None] + ATT_SPAN
    kpos = np.arange(ATT_STRIP)[None, :]
    rel = qpos - kpos
    mult = np.zeros((CHUNK, ATT_STRIP), np.float64)
    for window, dil in DILATED_PAIRS:
        mult += (rel >= 0) & (rel % dil == 0) & (rel // dil <= window // dil)
    return np.where(mult > 0, np.log(np.maximum(mult, 1.0)), NEG).astype(np.float32)


def _att_scores(q, ks, bias, i):
    s = _bdot(q, ks, NT) + bias
    kcol = lax.broadcasted_iota(jnp.int32, (1, ATT_STRIP), 1) + i * CHUNK
    return jnp.where(kcol >= ATT_SPAN, s, NEG)


def _pair_masks():
    low = lax.broadcasted_iota(jnp.int32, (CHUNK, 128), 1) < ATT_HEAD_DIM
    return low, ~low


def _att_fwd(q, kp, vp, bias, y, name):
    S = q.shape[0]
    SP = kp.shape[1]

    def body(q_ref, k_ref, v_ref, bias_ref, y_ref, o_ref):
        i = pl.program_id(1)
        strip = pl.ds(pl.multiple_of(i * CHUNK, CHUNK), ATT_STRIP)
        qv = q_ref[...]
        ks = k_ref[strip, :]
        vs = v_ref[strip, :]
        outs = []
        for keep in _pair_masks():
            s = _att_scores(jnp.where(keep, qv, jnp.zeros_like(qv)), ks, bias_ref[...], i)
            m = jnp.max(s, axis=-1, keepdims=True)
            p = jnp.exp(s - m)
            den = jnp.sum(p, axis=-1, keepdims=True)
            outs.append(_bdot(p, vs, NN) / den)
        o_ref[...] = jnp.where(_pair_masks()[0], outs[0], outs[1]).astype(o_ref.dtype)

    kv = pl.BlockSpec((None, SP, 128), lambda hp, i: (hp, 0, 0))
    return pl.pallas_call(
        body, grid=(ATT_HEADS // 2, S // CHUNK),
        in_specs=[pl.BlockSpec((CHUNK, 128), lambda hp, i: (i, hp)), kv, kv,
                  pl.BlockSpec((CHUNK, ATT_STRIP), lambda hp, i: (0, 0)), pl.BlockSpec(memory_space=pl.ANY)],
        out_specs=pl.BlockSpec((CHUNK, 128), lambda hp, i: (i, SSD_WIDTH // 128 + hp)),
        out_shape=jax.ShapeDtypeStruct(y.shape, y.dtype), input_output_aliases={4: 0}, name=name,
        compiler_params=_params("parallel", "arbitrary"),
    )(q, kp, vp, bias, y)


def _att_bwd(q, kp, vp, bias, dy, name):
    S = q.shape[0]
    SP = kp.shape[1]

    def body(q_ref, k_ref, v_ref, bias_ref, do_ref, dq_ref, dk_ref, dv_ref):
        i = pl.program_id(1)

        @pl.when(i == 0)
        def _():
            dk_ref[...] = jnp.zeros_like(dk_ref)
            dv_ref[...] = jnp.zeros_like(dv_ref)

        strip = pl.ds(pl.multiple_of(i * CHUNK, CHUNK), ATT_STRIP)
        qv = q_ref[...]
        dov = do_ref[...]
        ks = k_ref[strip, :]
        vs = v_ref[strip, :]
        dq = jnp.zeros((CHUNK, 128), F32)
        dk = jnp.zeros((ATT_STRIP, 128), F32)
        dv = jnp.zeros((ATT_STRIP, 128), F32)
        for keep in _pair_masks():
            qh = jnp.where(keep, qv, jnp.zeros_like(qv))
            doh = jnp.where(keep, dov, 0.0)
            s = _att_scores(qh, ks, bias_ref[...], i)
            m = jnp.max(s, axis=-1, keepdims=True)
            p = jnp.exp(s - m)
            p = p / jnp.sum(p, axis=-1, keepdims=True)
            dp = _bdot(doh, vs, NT)
            dsc = p * (dp - jnp.sum(p * dp, axis=-1, keepdims=True))
            dq = dq + jnp.where(keep, _bdot(dsc, ks, NN), 0.0)
            dv = dv + _bdot(p, doh, TN)
            dk = dk + _bdot(dsc, qh, TN)
        dq_ref[...] = dq
        dv_ref[strip, :] += dv
        dk_ref[strip, :] += dk

    kv = pl.BlockSpec((None, SP, 128), lambda hp, i: (hp, 0, 0))
    pairs = jax.ShapeDtypeStruct((ATT_HEADS // 2, SP, 128), F32)
    return pl.pallas_call(
        body, grid=(ATT_HEADS // 2, S // CHUNK),
        in_specs=[pl.BlockSpec((CHUNK, 128), lambda hp, i: (i, hp)), kv, kv,
                  pl.BlockSpec((CHUNK, ATT_STRIP), lambda hp, i: (0, 0)),
                  pl.BlockSpec((CHUNK, 128), lambda hp, i: (i, SSD_WIDTH // 128 + hp))],
        out_specs=[pl.BlockSpec((CHUNK, 128), lambda hp, i: (i, hp)), kv, kv],
        out_shape=[jax.ShapeDtypeStruct((S, ATT_WIDTH), F32), pairs, pairs],
        name=name, compiler_params=_params("parallel", "arbitrary"),
    )(q, kp, vp, bias, dy)


RQ_BLK = RET_COL // RET_QK_WIDTH
RV_BLK = (RET_COL + 2 * RET_QK_WIDTH) // RET_V_WIDTH
RET_LOG_GAMMA = tuple(math.log1p(-2.0 ** (-5.0 - h)) for h in range(RET_HEADS))


def _ret_decays(h):
    L = CHUNK
    lg = RET_LOG_GAMMA[h]
    row = lax.broadcasted_iota(jnp.int32, (L, L), 0)
    col = lax.broadcasted_iota(jnp.int32, (L, L), 1)
    rel = (row - col).astype(F32)
    dm = jnp.where(rel >= 0, jnp.exp(jnp.maximum(rel, 0.0) * lg), 0.0)
    idx = lax.broadcasted_iota(jnp.int32, (L, 1), 0).astype(F32)
    kte = jnp.exp((L - 1 - idx) * lg)
    qfs = jnp.exp((idx + 1.0) * lg)
    return dm, kte, qfs, math.exp(L * lg)


def _ret_fwd(proj, cos, sin, gain, y, name):
    S = proj.shape[0]
    L = CHUNK
    T = min(512, S)
    CPS = T // L
    NC = S // L

    def body(q_ref, k_ref, v_ref, g_ref, cos_ref, sin_ref, gain_ref, yin_ref, y_ref, o_ref, st_ref, state):
        i = pl.program_id(0)

        @pl.when(i == 0)
        def _():
            state[...] = jnp.zeros_like(state)

        dec = [_ret_decays(h) for h in range(RET_HEADS)]
        for c in range(CPS):
            rows = pl.ds(c * L, L)
            cw = _widen(cos_ref[rows, :], RET_QK_WIDTH)
            sw = _widen(sin_ref[rows, :], RET_QK_WIDTH)
            qv = _rope(q_ref[rows, :], cw, sw)
            kv = _rope(k_ref[rows, :], cw, sw) * (RET_QK_DIM ** -0.5)
            for h in range(RET_HEADS):
                dm, kte, qfs, cd = dec[h]
                qh = qv[:, h * 64:(h + 1) * 64]
                kh = kv[:, h * 64:(h + 1) * 64]
                vs = slice(h * RET_V_DIM, (h + 1) * RET_V_DIM)
                vh = v_ref[rows, vs]
                sp = state[h]
                st_ref[c, h] = sp
                o = _bdot(_bdot(qh, kh, NT) * dm, vh, NN) + _bdot(qh * qfs, sp, NN)
                state[h] = cd * sp + _bdot(kh * kte, vh, TN)
                o_ref[rows, vs] = o
                gh = g_ref[rows, vs]
                r = lax.rsqrt(jnp.mean(o * o, axis=-1, keepdims=True) + EPS)
                y_ref[rows, vs] = (o * r * gain_ref[:, vs] * (gh * _sigmoid(gh))).astype(y_ref.dtype)

    tab = pl.BlockSpec((T, 128), lambda i: (i, 0))
    wide = pl.BlockSpec((T, RET_V_WIDTH), lambda i: (i, 0))
    return pl.pallas_call(
        body, grid=(S // T,),
        in_specs=[pl.BlockSpec((T, RET_QK_WIDTH), lambda i: (i, RQ_BLK)), pl.BlockSpec((T, RET_QK_WIDTH), lambda i: (i, RQ_BLK + 1)),
                  pl.BlockSpec((T, RET_V_WIDTH), lambda i: (i, RV_BLK)), pl.BlockSpec((T, RET_V_WIDTH), lambda i: (i, RV_BLK + 1)),
                  tab, tab, pl.BlockSpec((1, RET_V_WIDTH), lambda i: (0, 0)), pl.BlockSpec(memory_space=pl.ANY)],
        out_specs=[pl.BlockSpec((T, RET_V_WIDTH), lambda i: (i, (SSD_WIDTH + ATT_WIDTH) // RET_V_WIDTH)), wide,
                   pl.BlockSpec((CPS, RET_HEADS, RET_QK_DIM, RET_V_DIM), lambda i: (i, 0, 0, 0))],
        out_shape=[jax.ShapeDtypeStruct(y.shape, y.dtype), jax.ShapeDtypeStruct((S, RET_V_WIDTH), F32),
                   jax.ShapeDtypeStruct((NC, RET_HEADS, RET_QK_DIM, RET_V_DIM), F32)],
        input_output_aliases={7: 0},
        scratch_shapes=[pltpu.VMEM((RET_HEADS, RET_QK_DIM, RET_V_DIM), F32)], name=name,
        compiler_params=_params("arbitrary"),
    )(proj, proj, proj, proj, cos, sin, gain, y)


def _ret_bwd(dy, oraw, proj, cos, sin, gain, states, dproj, name):
    S = proj.shape[0]
    L = CHUNK
    T = min(512, S)
    CPS = T // L
    NI = S // T
    QW, VW = RET_QK_WIDTH, RET_V_WIDTH
    V0, G0 = 2 * QW, 2 * QW + VW

    def body(dy_ref, o_ref, q_ref, k_ref, v_ref, g_ref, cos_ref, sin_ref, gain_ref, st_ref, dproj_ref,
             out_ref, dgain_ref, dstate, dqs, dks):
        i = pl.program_id(0)

        @pl.when(i == 0)
        def _():
            dstate[...] = jnp.zeros_like(dstate)
            dgain_ref[...] = jnp.zeros_like(dgain_ref)

        dec = [_ret_decays(h) for h in range(RET_HEADS)]
        for c in reversed(range(CPS)):
            rows = pl.ds(c * L, L)
            cw = _widen(cos_ref[rows, :], RET_QK_WIDTH)
            sw = _widen(sin_ref[rows, :], RET_QK_WIDTH)
            qv = _rope(q_ref[rows, :], cw, sw)
            kv = _rope(k_ref[rows, :], cw, sw) * (RET_QK_DIM ** -0.5)
            for h in range(RET_HEADS):
                dm, kte, qfs, cd = dec[h]
                qs = slice(h * 64, (h + 1) * 64)
                vs = slice(h * RET_V_DIM, (h + 1) * RET_V_DIM)
                qh = qv[:, qs]
                kh = kv[:, qs]
                vh = v_ref[rows, vs]
                gh = g_ref[rows, vs]
                gn = gain_ref[:, vs]
                o = o_ref[rows, vs]
                dyh = dy_ref[rows, vs]
                sg = _sigmoid(gh)
                silu_g = gh * sg
                r = lax.rsqrt(jnp.mean(o * o, axis=-1, keepdims=True) + EPS)
                n = o * r
                dgain_ref[:, vs] += jnp.sum(dyh * n * silu_g, axis=0, keepdims=True)
                out_ref[rows, G0 + h * RET_V_DIM:G0 + (h + 1) * RET_V_DIM] = (dyh * n * gn * _silu_grad(gh, sg)).astype(out_ref.dtype)
                dn = dyh * gn * silu_g
                do = r * (dn - n * jnp.mean(dn * n, axis=-1, keepdims=True))
                sp = st_ref[c, h]
                ds = dstate[h]
                sc = _bdot(qh, kh, NT) * dm
                dsc = _bdot(do, vh, NT) * dm
                out_ref[rows, V0 + h * RET_V_DIM:V0 + (h + 1) * RET_V_DIM] = (_bdot(sc, do, TN) + _bdot(kh * kte, ds, NN)).astype(out_ref.dtype)
                dqs[:, qs] = _bdot(dsc, kh, NN) + _bdot(do, sp, NT) * qfs
                dks[:, qs] = _bdot(dsc, qh, TN) + _bdot(vh, ds, NT) * kte
                dstate[h] = cd * ds + _bdot(qh * qfs, do, TN)
            out_ref[rows, 0:QW] = _rope_t(dqs[...], cw, sw).astype(out_ref.dtype)
            out_ref[rows, QW:2 * QW] = _rope_t(dks[...] * (RET_QK_DIM ** -0.5), cw, sw).astype(out_ref.dtype)

    rev = lambda i: NI - 1 - i
    tab = pl.BlockSpec((T, 128), lambda i: (rev(i), 0))
    wide = pl.BlockSpec((T, RET_V_WIDTH), lambda i: (rev(i), 0))
    group = pl.BlockSpec((T, G0 + VW), lambda i: (rev(i), RET_COL // (G0 + VW)))
    gvec = pl.BlockSpec((1, RET_V_WIDTH), lambda i: (0, 0))
    return pl.pallas_call(
        body, grid=(NI,),
        in_specs=[pl.BlockSpec((T, RET_V_WIDTH), lambda i: (rev(i), (SSD_WIDTH + ATT_WIDTH) // RET_V_WIDTH)), wide,
                  pl.BlockSpec((T, RET_QK_WIDTH), lambda i: (rev(i), RQ_BLK)), pl.BlockSpec((T, RET_QK_WIDTH), lambda i: (rev(i), RQ_BLK + 1)),
                  pl.BlockSpec((T, RET_V_WIDTH), lambda i: (rev(i), RV_BLK)), pl.BlockSpec((T, RET_V_WIDTH), lambda i: (rev(i), RV_BLK + 1)),
                  tab, tab, gvec,
                  pl.BlockSpec((CPS, RET_HEADS, RET_QK_DIM, RET_V_DIM), lambda i: (rev(i), 0, 0, 0)),
                  pl.BlockSpec(memory_space=pl.ANY)],
        out_specs=[group, gvec],
        out_shape=[jax.ShapeDtypeStruct(dproj.shape, dproj.dtype), jax.ShapeDtypeStruct((1, RET_V_WIDTH), F32)],
        input_output_aliases={10: 0},
        scratch_shapes=[pltpu.VMEM((RET_HEADS, RET_QK_DIM, RET_V_DIM), F32), pltpu.VMEM((L, RET_QK_WIDTH), F32),
                        pltpu.VMEM((L, RET_QK_WIDTH), F32)],
        name=name, compiler_params=_params("arbitrary"),
    )(dy, oraw, proj, proj, proj, proj, cos, sin, gain, states, dproj)


def _adamw_update(g_ref, nb, w_ref, m_ref, v_ref, go_ref, d_ref, mo_ref, vo_ref):
    g = g_ref[0].astype(F32)
    for k in range(1, nb):
        g = g + g_ref[k].astype(F32)
    mn = ADAM_B1 * m_ref[...] + (1.0 - ADAM_B1) * g
    vn = ADAM_B2 * v_ref[...] + (1.0 - ADAM_B2) * (g * g)
    go_ref[...] = g
    mo_ref[...] = mn
    vo_ref[...] = vn
    c1 = 1.0 - ADAM_B1 ** ADAM_STEP
    c2 = 1.0 - ADAM_B2 ** ADAM_STEP
    d_ref[...] = -ADAM_LR * ((mn / c1) / (jnp.sqrt(vn / c2) + ADAM_EPS) + ADAM_WD * w_ref[...])


def _adamw_rows(R, C):
    return _pick(R, tuple(t for t in (512, 256, 128, 64, 32, 16, 8) if t * C <= 256 * 1024))


def _adamw(gblocks, w, m, v, name):
    nb, R, C = gblocks.shape
    tr = _adamw_rows(R, C)

    def body(g_ref, *refs):
        _adamw_update(g_ref, nb, *refs)

    row = pl.BlockSpec((tr, C), lambda i: (i, 0))
    sh = jax.ShapeDtypeStruct((R, C), F32)
    return pl.pallas_call(
        body, grid=(R // tr,), in_specs=[pl.BlockSpec((nb, tr, C), lambda i: (0, i, 0)), row, row, row],
        out_specs=[row, row, row, row], out_shape=[sh, sh, sh, sh], name=name, compiler_params=_params("parallel"),
    )(gblocks, w, m, v)


def _adamw_layers(g0, g1, w, m, v, name):
    nb, R, C = g0.shape
    tr = _adamw_rows(R, C)

    def body(g0_ref, g1_ref, *refs):
        l = pl.program_id(0)

        @pl.when(l == 0)
        def _():
            _adamw_update(g0_ref, nb, *refs)

        @pl.when(l == 1)
        def _():
            _adamw_update(g1_ref, nb, *refs)

    row = pl.BlockSpec((None, tr, C), lambda l, i: (l, i, 0))
    sh = jax.ShapeDtypeStruct((DEPTH, R, C), F32)
    return pl.pallas_call(
        body, grid=(DEPTH, R // tr),
        in_specs=[pl.BlockSpec((nb, tr, C), lambda l, i: (0, i * (1 - l), 0)), pl.BlockSpec((nb, tr, C), lambda l, i: (0, i * l, 0)),
                  row, row, row],
        out_specs=[row, row, row, row], out_shape=[sh, sh, sh, sh], name=name, compiler_params=_params("arbitrary", "arbitrary"),
    )(g0, g1, w, m, v)


def _peers():
    x, y, c = lax.axis_index("x"), lax.axis_index("y"), lax.axis_index("c")
    flips = ((0, 0, 1), (1, 0, 0), (0, 1, 0), (1, 1, 0), (1, 0, 1), (0, 1, 1), (1, 1, 1))
    me = 4 * x + 2 * y + c
    peers = [(x ^ fx, y ^ fy, c ^ fc) for fx, fy, fc in flips]
    return me, peers


def _exchange(arrs, scatter, name):
    n = len(arrs)
    npeer = N_DEV - 1

    def body(*refs):
        ins, outs = refs[:n], refs[n:2 * n]
        send_sems, recv_sems, local_sems = refs[2 * n:]
        me, peers = _peers()
        copies = []
        for a in range(n):
            src_own = ins[a].at[me] if scatter else ins[a]
            own = pltpu.make_async_copy(src_own, outs[a].at[me], local_sems.at[a])
            own.start()
            copies.append(own)
            for k, peer in enumerate(peers):
                src = ins[a].at[4 * peer[0] + 2 * peer[1] + peer[2]] if scatter else ins[a]
                cp = pltpu.make_async_remote_copy(
                    src_ref=src, dst_ref=outs[a].at[me], send_sem=send_sems.at[a * npeer + k],
                    recv_sem=recv_sems.at[a * npeer + k], device_id=peer, device_id_type=pl.DeviceIdType.MESH)
                cp.start()
                copies.append(cp)
        for cp in copies:
            cp.wait()

    out_shape = [jax.ShapeDtypeStruct(((N_DEV,) + a.shape[1:]) if scatter else ((N_DEV,) + a.shape), a.dtype) for a in arrs]
    anyspec = pl.BlockSpec(memory_space=pl.ANY)
    return pl.pallas_call(
        body, in_specs=[anyspec] * n, out_specs=[anyspec] * n, out_shape=out_shape,
        scratch_shapes=[pltpu.SemaphoreType.DMA((n * npeer,)), pltpu.SemaphoreType.DMA((n * npeer,)),
                        pltpu.SemaphoreType.DMA((n,))],
        name=name,
    )(*arrs)


def _dev_index(peer):
    return 4 * peer[0] + 2 * peer[1] + peer[2]


def _push_copies(src_refs, land_refs, send_sems, recv_sems, scatter, as_receiver):
    me, peers = _peers()
    npeer = N_DEV - 1
    copies = []
    for a in range(len(src_refs)):
        for k, peer in enumerate(peers):
            src = src_refs[a].at[_dev_index(peer)] if scatter else src_refs[a]
            slot = _dev_index(peer) if as_receiver else me
            copies.append(pltpu.make_async_remote_copy(
                src_ref=src, dst_ref=land_refs[a].at[slot], send_sem=send_sems.at[a * npeer + k],
                recv_sem=recv_sems.at[a * npeer + k], device_id=peer, device_id_type=pl.DeviceIdType.MESH))
    return copies


def _push_start(srcs, lands, scatter, name):
    n = len(srcs)
    nsem = n * (N_DEV - 1)

    def body(*refs):
        for cp in _push_copies(refs[:n], refs[n:2 * n], refs[2 * n], refs[2 * n + 1], scatter, False):
            cp.start()
        token = refs[-1]
        token[...] = jnp.zeros_like(token)

    hbm = pl.BlockSpec(memory_space=pltpu.HBM)
    sem = pl.BlockSpec(memory_space=pltpu.SEMAPHORE)
    arrs = list(srcs) + list(lands)
    return pl.pallas_call(
        body, name=name,
        out_shape=(pltpu.SemaphoreType.DMA((nsem,)), pltpu.SemaphoreType.DMA((nsem,)),
                   *[pltpu.HBM(a.shape, a.dtype) for a in arrs], jax.ShapeDtypeStruct((8, 128), F32)),
        in_specs=[hbm] * (2 * n), out_specs=(sem, sem, *([hbm] * (2 * n)), pl.BlockSpec(memory_space=pltpu.VMEM)),
        input_output_aliases={i: 2 + i for i in range(2 * n)},
        compiler_params=pltpu.CompilerParams(has_side_effects=pltpu.SideEffectType.DATAFLOW_SIDE_EFFECTING),
    )(*[pltpu.with_memory_space_constraint(a, pltpu.HBM) for a in arrs])


def _push_wait(handle, after, scatter, name):
    send_sems, recv_sems, *thru, _ = handle
    n = len(thru) // 2

    def body(*refs):
        for cp in _push_copies(refs[:n], refs[n:2 * n], refs[2 * n], refs[2 * n + 1], scatter, True):
            cp.wait_send()
            cp.wait_recv()

    hbm = pl.BlockSpec(memory_space=pltpu.HBM)
    sem = pl.BlockSpec(memory_space=pltpu.SEMAPHORE)
    outs = pl.pallas_call(
        body, name=name, out_shape=tuple(pltpu.HBM(a.shape, a.dtype) for a in thru),
        in_specs=[hbm] * (2 * n) + [sem, sem, pl.BlockSpec(memory_space=pl.ANY)], out_specs=tuple([hbm] * (2 * n)),
        input_output_aliases={i: i for i in range(2 * n)},
        compiler_params=pltpu.CompilerParams(has_side_effects=pltpu.SideEffectType.DATAFLOW_SIDE_EFFECTING),
    )(*thru, send_sems, recv_sems, after)
    return list(outs[n:])


def _landing(own, me):
    return lax.dynamic_update_index_in_dim(lax.empty((N_DEV,) + own.shape, own.dtype), own, me, 0)


def _tables(S):
    pos = jnp.arange(S, dtype=F32)
    inv = ROPE_THETA ** (-jnp.arange(0, ATT_HEAD_DIM, 2, dtype=F32) / ATT_HEAD_DIM)
    ang = pos[:, None] * inv[None, :]
    cos, sin = jnp.cos(ang), jnp.sin(ang)
    cos128 = jnp.tile(cos, (1, 4))
    sin128 = jnp.tile(jnp.concatenate([-sin, sin], axis=1), (1, 2))
    lane = np.arange(ATT_WIDTH)
    bd = jnp.asarray((lane[:, None] // 64 == lane[None, :] // 64).astype(np.float32), dtype=BF16)
    return cos128, sin128, bd, jnp.asarray(_att_bias())


def _permute_in(w):
    cols = lambda span: w[:, span[0]:span[1]]
    pad = jnp.zeros((w.shape[0], IN_PAD - IN_WIDTH), w.dtype)
    return jnp.concatenate([cols(ORIG_RET), cols(ORIG_ATT), cols(ORIG_Z_XBC), cols(ORIG_DT), pad], axis=1)


def _unpermute_in(g):
    return jnp.concatenate([g[:, Z_COL:DT_COL], g[:, DT_COL:DT_COL + SSD_HEADS], g[:, ATT_COL:Z_COL], g[:, RET_COL:ATT_COL]], axis=1)


def _layer_fwd(l, x, p, tabs, late=None):
    cos, sin, bd, bias = tabs
    S = x.shape[0]
    row = lambda v: v.reshape(1, -1)
    hn = _rmsnorm_fwd(x, row(p["ln_mix"]), f"norm_mix_fwd{l}")
    proj = _mm(hn, p["w_in"], "nn", f"in_proj{l}", tn=1152)
    xbc = _conv_fwd(proj, p["conv_w"], row(p["conv_b"]), f"conv_fwd{l}")
    dtr = proj[:, DT_COL:DT_COL + SSD_HEADS].reshape(S, SSD_GROUPS, HPG).transpose(1, 0, 2)
    grp = lambda v: v.reshape(SSD_GROUPS, 1, HPG)
    d_rep = row(jnp.repeat(p["d_skip"], SSD_HEAD_DIM))
    y, yraw, ssd_st = _ssd_fwd(xbc, proj, dtr, grp(p["dt_bias"]), grp(p["a_log"]), d_rep, row(p["ssd_norm"]), f"ssd_fwd{l}")
    qg = row(jnp.tile(p["q_norm"], ATT_HEADS))
    kg = row(jnp.tile(p["k_norm"], ATT_HEADS))
    aq, akp, avp = _att_prep_fwd(proj, qg, kg, cos, sin, bd, f"att_prep_fwd{l}")
    y = _att_fwd(aq, akp, avp, bias, y, f"att_fwd{l}")
    y, oraw, ret_st = _ret_fwd(proj, cos, sin, row(p["ret_norm"]), y, f"ret_fwd{l}")
    if late is not None:
        p.update(late(y))
    x1 = _mm(y, p["w_out"], "nn", f"out_proj{l}", residual=x)
    hn2 = _rmsnorm_fwd(x1, row(p["ln_ffn"]), f"norm_ffn_fwd{l}")
    g, u, act = _swiglu_fwd(hn2, p["w_gate"], p["w_up"], f"swiglu_fwd{l}")
    x2 = _mm(act, p["w_down"], "nn", f"down_proj{l}", residual=x1, tk=2816)
    saved = dict(x=x, hn=hn, proj=proj, xbc=xbc, dtr=dtr, yraw=yraw, ssd_st=ssd_st, aq=aq, akp=akp, avp=avp,
                 oraw=oraw, ret_st=ret_st, y=y, x1=x1, hn2=hn2, g=g, u=u, act=act, d_rep=d_rep, qg=qg, kg=kg)
    return x2, saved


def _layer_bwd(l, dx2, dx2_bf, p, sv, tabs, on_ffn=None, on_all=None):
    cos, sin, bd, bias = tabs
    S = dx2.shape[0]
    row = lambda v: v.reshape(1, -1)
    grp = lambda v: v.reshape(SSD_GROUPS, 1, HPG)
    gr = {}
    dg, du = _swiglu_bwd(dx2_bf, p["w_down"], sv["g"], sv["u"], f"swiglu_bwd{l}")
    gr["w_down"] = _mm(sv["act"], dx2_bf, "tn", f"down_wgrad{l}", out_dtype=BF16, tm=1408, tn=1024, tk=2048)
    dhn2 = _mm_nt2(dg, p["w_gate"], du, p["w_up"], f"ffn_dgrad{l}")
    gr["w_gate"] = _mm(sv["hn2"], dg, "tn", f"gate_wgrad{l}", out_dtype=BF16, tm=1024, tn=1408, tk=2048)
    gr["w_up"] = _mm(sv["hn2"], du, "tn", f"up_wgrad{l}", out_dtype=BF16, tm=1024, tn=1408, tk=2048)
    ffn_gain = row(p["ln_ffn"]) + (on_ffn(gr)[0, 0] if on_ffn is not None else 0.0)
    dx1, dx1_bf, dln_ffn = _rmsnorm_bwd(sv["x1"], dhn2, ffn_gain, dx2, f"norm_ffn_bwd{l}")
    gr["ln_ffn"] = dln_ffn[0]
    dy = _mm(dx1_bf, p["w_out"], "nt", f"out_dgrad{l}")
    gr["w_out"] = _mm(sv["y"], dx1_bf, "tn", f"out_wgrad{l}", out_dtype=BF16, tm=1024, tn=1024, tk=2048)
    dproj, dxs, dbm, dcm, ddtr, dbias, dalog, dd, dssd_gain = _ssd_bwd(
        dy, sv["yraw"], sv["xbc"], sv["proj"], sv["dtr"], grp(p["dt_bias"]), grp(p["a_log"]), sv["d_rep"],
        row(p["ssd_norm"]), sv["ssd_st"], f"ssd_bwd{l}")
    gr["dt_bias"], gr["a_log"], gr["d_skip"] = dbias.reshape(-1), dalog.reshape(-1), dd.reshape(-1)
    gr["ssd_norm"] = dssd_gain[0]
    dproj, dconv_w, dconv_b = _conv_bwd(dxs, dbm, dcm, sv["proj"], p["conv_w"], row(p["conv_b"]), dproj, f"conv_bwd{l}")
    gr["conv_w"], gr["conv_b"] = dconv_w, dconv_b[0]
    dq, dk_p, dv_p = _att_bwd(sv["aq"], sv["akp"], sv["avp"], bias, dy, f"att_bwd{l}")
    dproj, dqg, dkg = _att_prep_bwd(sv["proj"], dq, dk_p, dv_p, sv["qg"], sv["kg"], cos, sin, bd, dproj, f"att_prep_bwd{l}")
    gr["q_norm"], gr["k_norm"] = dqg[0], dkg[0]
    dproj, dret_gain = _ret_bwd(dy, sv["oraw"], sv["proj"], cos, sin, row(p["ret_norm"]), sv["ret_st"], dproj, f"ret_bwd{l}")
    gr["ret_norm"] = dret_gain[0]
    ddt_cols = ddtr.transpose(1, 0, 2).reshape(S, SSD_HEADS).astype(BF16)
    dproj = lax.dynamic_update_slice(dproj, jnp.pad(ddt_cols, ((0, 0), (0, IN_PAD - DT_COL - SSD_HEADS))), (0, DT_COL))
    gr["w_in"] = _mm(sv["hn"], dproj, "tn", f"in_wgrad{l}", out_dtype=BF16, tm=1024, tn=1152, tk=2048)
    launched = on_all(gr) if on_all is not None else None
    dhn = _mm(dproj, p["w_in"], "nt", f"in_dgrad{l}", tk=1920, after=launched)
    dx0, dx0_bf, dln_mix = _rmsnorm_bwd(sv["x"], dhn, row(p["ln_mix"]), dx1, f"norm_mix_bwd{l}")
    gr["ln_mix"] = dln_mix[0]
    return dx0, dx0_bf, gr


def _local_step(x, tgt, layers, late=None, on_ffn=None, on_all=None):
    n = len(layers)
    none = [None] * n
    late, on_ffn, on_all = late or none, on_ffn or none, on_all or none
    tabs = _tables(x.shape[0])
    saved, params = [], []
    h = x
    for l in range(n):
        p = dict(layers[l](h) if callable(layers[l]) else layers[l])
        h, sv = _layer_fwd(l, h, p, tabs, late[l])
        saved.append(sv)
        params.append(p)
    dh, dh_bf, lacc = _loss_grad(h, tgt, "loss_grad")
    grads = [None] * n
    for l in reversed(range(n)):
        dh, dh_bf, grads[l] = _layer_bwd(l, dh, dh_bf, params[l], saved[l], tabs, on_ffn[l], on_all[l])
    return lacc[0, 0], dh, grads


BIG = ("w_in", "w_out", "w_gate", "w_up", "w_down")
SMALL = ("ln_mix", "conv_b", "dt_bias", "a_log", "d_skip", "ssd_norm", "q_norm", "k_norm", "ret_norm", "ln_ffn")
ORDER = ("ln_mix", "w_in", "conv_w", "conv_b", "dt_bias", "a_log", "d_skip", "ssd_norm", "q_norm", "k_norm", "ret_norm",
         "w_out", "ln_ffn", "w_gate", "w_up", "w_down")


COL_SHARDED = ("w_in", "w_gate", "w_up", "conv_w")


def _full_weight(k, gathered):
    if k in COL_SHARDED:
        full = gathered.transpose(1, 0, 2).reshape(gathered.shape[1], -1)
        return _permute_in(full) if k == "w_in" else full
    return gathered.reshape(-1, gathered.shape[2])


def _shard_block(k, g):
    if k == "w_in":
        g = _unpermute_in(g)
    if k in COL_SHARDED:
        return g.reshape(g.shape[0], N_DEV, -1).transpose(1, 0, 2)
    return g.reshape(N_DEV, -1, g.shape[1])


def kernel(x, ln_mix, w_in, conv_w, conv_b, dt_bias, a_log, d_skip, ssd_norm, q_norm, k_norm, ret_norm, w_out, ln_ffn, w_gate, w_up, w_down, loss_target, m_ln_mix, m_w_in, m_conv_w, m_conv_b, m_dt_bias, m_a_log, m_d_skip, m_ssd_norm, m_q_norm, m_k_norm, m_ret_norm, m_w_out, m_ln_ffn, m_w_gate, m_w_up, m_w_down, v_ln_mix, v_w_in, v_conv_w, v_conv_b, v_dt_bias, v_a_log, v_d_skip, v_ssd_norm, v_q_norm, v_k_norm, v_ret_norm, v_w_out, v_ln_ffn, v_w_gate, v_w_up, v_w_down):
    w = dict(ln_mix=ln_mix, w_in=w_in, conv_w=conv_w, conv_b=conv_b, dt_bias=dt_bias, a_log=a_log, d_skip=d_skip,
             ssd_norm=ssd_norm, q_norm=q_norm, k_norm=k_norm, ret_norm=ret_norm, w_out=w_out, ln_ffn=ln_ffn,
             w_gate=w_gate, w_up=w_up, w_down=w_down)
    m = dict(ln_mix=m_ln_mix, w_in=m_w_in, conv_w=m_conv_w, conv_b=m_conv_b, dt_bias=m_dt_bias, a_log=m_a_log,
             d_skip=m_d_skip, ssd_norm=m_ssd_norm, q_norm=m_q_norm, k_norm=m_k_norm, ret_norm=m_ret_norm, w_out=m_w_out,
             ln_ffn=m_ln_ffn, w_gate=m_w_gate, w_up=m_w_up, w_down=m_w_down)
    v = dict(ln_mix=v_ln_mix, w_in=v_w_in, conv_w=v_conv_w, conv_b=v_conv_b, dt_bias=v_dt_bias, a_log=v_a_log,
             d_skip=v_d_skip, ssd_norm=v_ssd_norm, q_norm=v_q_norm, k_norm=v_k_norm, ret_norm=v_ret_norm, w_out=v_w_out,
             ln_ffn=v_ln_ffn, w_gate=v_w_gate, w_up=v_w_up, w_down=v_w_down)
    me = 4 * lax.axis_index("x") + 2 * lax.axis_index("y") + lax.axis_index("c")

    waves = {"a": [("w_in", 0), ("conv_w", 0), ("conv_w", 1)],
             "b": [(k, 0) for k in ("w_out", "w_gate", "w_up", "w_down")],
             "c": [(k, 1) for k in BIG]}
    gather = {}
    behind = 0.0
    for tag, items in waves.items():
        srcs = [w[k][l] if k == "conv_w" else (w[k][l] + behind).astype(BF16) for k, l in items]
        gather[tag] = _push_start(srcs, [_landing(s, me) for s in srcs], False, f"gather_{tag}_start")
        behind = gather[tag][-1][0, 0]
    started = behind
    full = {}

    def arrive(tag, after):
        for (k, l), g in zip(waves[tag], _push_wait(gather[tag], after, False, f"gather_{tag}_wait")):
            full[k, l] = _full_weight(k, g)

    def layer_weights(l, names):
        return {k: full[k, l] for k in names}

    def small_weights(l):
        return {k: w[k][l] for k in SMALL}

    def layer0(h):
        arrive("a", gather["c"][-1])
        p = small_weights(0)
        p["ln_mix"] = p["ln_mix"] + started
        return {**p, **layer_weights(0, ("w_in", "conv_w"))}

    def late0(y):
        arrive("b", y)
        return layer_weights(0, waves_b_names)

    def layer1(h):
        arrive("c", h)
        return {**small_weights(1), **layer_weights(1, BIG + ("conv_w",))}

    waves_b_names = tuple(k for k, _ in waves["b"])

    groups = {"1": [(k, 1) for k in BIG], "0a": [(k, 0) for k in ("w_down", "w_gate", "w_up")],
              "0b": [(k, 0) for k in ("w_out", "w_in")]}
    scatter = {}

    def push_grads(tag, gr):
        blocks = [_shard_block(k, gr[k]) for k, _ in groups[tag]]
        lands = [_landing(lax.dynamic_index_in_dim(b, me, 0, keepdims=False), me) for b in blocks]
        scatter[tag] = _push_start(blocks, lands, True, f"scatter_{tag}_start")
        return scatter[tag][-1]

    loss_part, gx, grads = _local_step(
        x[0], loss_target[0], [layer0, layer1], late=[late0, None],
        on_ffn=[functools.partial(push_grads, "0a"), None],
        on_all=[functools.partial(push_grads, "0b"), functools.partial(push_grads, "1")])
    loss = lax.psum(loss_part, MESH_AXES)

    out = {}
    recv = {}
    for tag, items in groups.items():
        for item, r in zip(items, _push_wait(scatter[tag], gx, True, f"scatter_{tag}_wait")):
            recv[item] = r
    for k in BIG:
        out[k] = _adamw_layers(recv[k, 0], recv[k, 1], w[k], m[k], v[k], f"adamw_{k}")
    names = SMALL + ("conv_w",)
    sizes = [int(np.prod(grads[0][k].shape)) for k in names]
    packed = jnp.concatenate([jnp.stack([grads[l][k] for l in range(DEPTH)]).reshape(-1) for k in names])
    n_small = packed.shape[0]
    rows_small = -(-n_small // 1024) * 8
    pad = lambda t, fill: jnp.concatenate([t, jnp.full((rows_small * 128 - n_small,), fill, F32)]).reshape(rows_small, 128)
    parts = _exchange([pad(packed, 0.0)], False, "gather_small_grads")[0]
    n_rep = DEPTH * sum(sizes[:-1])
    pack_rep = lambda d, fill: pad(jnp.concatenate([d[k].reshape(-1) for k in SMALL]
                                                   + [jnp.full((n_small - n_rep,), fill, F32)]), fill)
    res = _adamw(parts, pack_rep(w, 1.0), pack_rep(m, 1.0), pack_rep(v, 1.0), "adamw_small")
    res = [t.reshape(-1) for t in res]
    off = 0
    for k, sz in zip(SMALL, sizes[:-1]):
        out[k] = [t[off:off + DEPTH * sz].reshape(w[k].shape) for t in res]
        off += DEPTH * sz
    gconv = res[0][off:off + DEPTH * sizes[-1]].reshape(DEPTH, SSD_CONV, SSD_CONV_CH)
    gconv = lax.dynamic_slice_in_dim(gconv, me * conv_w.shape[2], conv_w.shape[2], axis=2)
    flat = lambda t: t.reshape(8, -1)
    resc = _adamw(flat(gconv)[None], flat(conv_w), flat(m_conv_w), flat(v_conv_w), "adamw_conv_w")
    out["conv_w"] = [t.reshape(conv_w.shape) for t in resc]

    return (loss, gx[None], *[out[k][0] for k in ORDER], *[out[k][1] for k in ORDER],
            *[out[k][2] for k in ORDER], *[out[k][3] for k in ORDER])
```

```python
import functools
import math

import jax
import jax.numpy as jnp
import numpy as np
from jax import lax
from jax.experimental import pallas as pl
from jax.experimental.pallas import tpu as pltpu

F32 = jnp.float32
BF16 = jnp.bfloat16

N_DEV = 8
MESH_AXES = ("x", "y", "c")
D_MODEL = 2048
DEPTH = 2
EPS = 1e-6
ROPE_THETA = 10000.0
SSD_HEADS = 16
SSD_HEAD_DIM = 64
SSD_WIDTH = 1024
SSD_GROUPS = 2
SSD_STATE = 128
SSD_CONV = 4
SSD_CONV_CH = 1536
ATT_HEADS = 8
ATT_HEAD_DIM = 64
ATT_WIDTH = 512
DILATED_PAIRS = ((128, 1), (512, 4), (2048, 16))
RET_HEADS = 4
RET_QK_DIM = 64
RET_V_DIM = 128
RET_QK_WIDTH = 256
RET_V_WIDTH = 512
CHUNK = 128
MIX_WIDTH = 2048
ATT_SPAN = 2048
ATT_STRIP = ATT_SPAN + CHUNK
IN_WIDTH = 5648
IN_PAD = 5760
RET_COL, ATT_COL, Z_COL, XBC_COL, DT_COL = 0, 1536, 3072, 4096, 5632
ORIG_Z_XBC, ORIG_DT, ORIG_ATT, ORIG_RET = (0, 2560), (2560, 2576), (2576, 4112), (4112, 5648)
D_FF = 5632
ADAM_LR = 0.001
ADAM_B1 = 0.9
ADAM_B2 = 0.999
ADAM_EPS = 1e-08
ADAM_WD = 0.01
ADAM_STEP = 10
NEG = -1e30
VMEM_LIMIT_V7X = 60 * 1024 * 1024

NN = (((1,), (0,)), ((), ()))
NT = (((1,), (1,)), ((), ()))
TN = (((0,), (0,)), ((), ()))


def _bdot(a, b, dims):
    return lax.dot_general(a.astype(BF16), b.astype(BF16), dims, preferred_element_type=F32)


def _xdot(a, b, dims):
    return lax.dot_general(a, b, dims, precision=lax.Precision.HIGHEST, preferred_element_type=F32)


def _params(*sem):
    return pltpu.CompilerParams(dimension_semantics=sem, vmem_limit_bytes=VMEM_LIMIT_V7X)


def _sigmoid(v):
    return 1.0 / (1.0 + jnp.exp(-v))


def _silu_grad(v, s):
    return s * (1.0 + v * (1.0 - s))


def _rmsnorm_fwd(x, g, name):
    S, D = x.shape
    tr = min(512, S)

    def body(x_ref, g_ref, o_ref):
        xv = x_ref[...]
        r = lax.rsqrt(jnp.mean(xv * xv, axis=-1, keepdims=True) + EPS)
        o_ref[...] = (xv * r * g_ref[...]).astype(o_ref.dtype)

    return pl.pallas_call(
        body, grid=(S // tr,),
        in_specs=[pl.BlockSpec((tr, D), lambda i: (i, 0)), pl.BlockSpec((1, D), lambda i: (0, 0))],
        out_specs=pl.BlockSpec((tr, D), lambda i: (i, 0)),
        out_shape=jax.ShapeDtypeStruct((S, D), BF16), name=name, compiler_params=_params("parallel"),
    )(x, g)


def _rmsnorm_bwd(x, dy, g, dres, name):
    S, D = x.shape
    tr = min(512, S)

    def body(x_ref, dy_ref, g_ref, dres_ref, dx_ref, dxb_ref, dg_ref):
        i = pl.program_id(0)
        xv = x_ref[...]
        r = lax.rsqrt(jnp.mean(xv * xv, axis=-1, keepdims=True) + EPS)
        n = xv * r
        dy = dy_ref[...]
        dn = dy * g_ref[...]
        dx = dres_ref[...] + r * (dn - n * jnp.mean(dn * n, axis=-1, keepdims=True))
        dx_ref[...] = dx
        dxb_ref[...] = dx.astype(BF16)
        part = jnp.sum(dy * n, axis=0, keepdims=True)

        @pl.when(i == 0)
        def _():
            dg_ref[...] = part

        @pl.when(i > 0)
        def _():
            dg_ref[...] += part

    row = pl.BlockSpec((tr, D), lambda i: (i, 0))
    vec = pl.BlockSpec((1, D), lambda i: (0, 0))
    return pl.pallas_call(
        body, grid=(S // tr,), in_specs=[row, row, vec, row], out_specs=[row, row, vec],
        out_shape=[jax.ShapeDtypeStruct((S, D), F32), jax.ShapeDtypeStruct((S, D), BF16), jax.ShapeDtypeStruct((1, D), F32)],
        name=name, compiler_params=_params("arbitrary"),
    )(x, dy, g, dres)


def _loss_grad(y, tgt, name):
    S, D = y.shape
    tr = min(512, S)

    def body(y_ref, t_ref, dy_ref, dyb_ref, l_ref):
        i = pl.program_id(0)
        err = y_ref[...] - t_ref[...]
        dy = err * (1.0 / D)
        dy_ref[...] = dy
        dyb_ref[...] = dy.astype(BF16)
        part = jnp.sum(jnp.sum(err * err, axis=1, keepdims=True), axis=0, keepdims=True) * (0.5 / D)

        @pl.when(i == 0)
        def _():
            l_ref[...] = jnp.zeros_like(l_ref)

        l_ref[...] += part

    row = pl.BlockSpec((tr, D), lambda i: (i, 0))
    return pl.pallas_call(
        body, grid=(S // tr,), in_specs=[row, row],
        out_specs=[row, row, pl.BlockSpec((8, 128), lambda i: (0, 0))],
        out_shape=[jax.ShapeDtypeStruct((S, D), F32), jax.ShapeDtypeStruct((S, D), BF16), jax.ShapeDtypeStruct((8, 128), F32)],
        name=name, compiler_params=_params("arbitrary"),
    )(y, tgt)


def _pick(n, cands):
    for c in cands:
        if n % c == 0:
            return c
    return n


def _mm(a, b, mode, name, out_dtype=F32, residual=None, tm=None, tn=None, tk=None, after=None):
    if mode == "nn":
        (M, K), (_, N) = a.shape, b.shape
    elif mode == "nt":
        (M, K), (N, _) = a.shape, b.shape
    else:
        (K, M), (_, N) = a.shape, b.shape
    tm = min(tm, M) if tm else _pick(M, (1024, 512, 256, 128))
    tn = min(tn, N) if tn else _pick(N, (1024, 1152, 1408, 512, 256, 128))
    tk = min(tk, K) if tk else _pick(K, (2048, 1920, 1408, 1024, 512, 256, 128))
    assert M % tm == 0 and N % tn == 0 and K % tk == 0, (name, M, N, K, tm, tn, tk)
    nk = K // tk
    a_spec = pl.BlockSpec((tk, tm), lambda i, j, k: (k, i)) if mode == "tn" else pl.BlockSpec((tm, tk), lambda i, j, k: (i, k))
    b_spec = pl.BlockSpec((tn, tk), lambda i, j, k: (j, k)) if mode == "nt" else pl.BlockSpec((tk, tn), lambda i, j, k: (k, j))
    o_spec = pl.BlockSpec((tm, tn), lambda i, j, k: (i, j))
    dims = {"nn": NN, "nt": NT, "tn": TN}[mode]
    has_res = residual is not None

    has_after = after is not None

    def body(*refs):
        a_ref, b_ref = refs[0], refs[1]
        r_ref = refs[2] if has_res else None
        o_ref = refs[2 + has_res + has_after]
        p = _bdot(a_ref[...], b_ref[...], dims)

        def finish(acc):
            if has_res:
                acc = acc + r_ref[...]
            o_ref[...] = acc.astype(o_ref.dtype)

        if nk == 1:
            finish(p)
        else:
            acc_ref = refs[-1]
            k = pl.program_id(2)

            @pl.when(k == 0)
            def _():
                acc_ref[...] = p

            @pl.when(k > 0)
            def _():
                acc_ref[...] += p

            @pl.when(k == nk - 1)
            def _():
                finish(acc_ref[...])

    ins = [a, b] + ([residual] if has_res else []) + ([after] if has_after else [])
    in_specs = [a_spec, b_spec] + ([o_spec] if has_res else []) + ([pl.BlockSpec(memory_space=pl.ANY)] if has_after else [])
    scratch = [pltpu.VMEM((tm, tn), F32)] if nk > 1 else []
    return pl.pallas_call(
        body, grid=(M // tm, N // tn, nk), in_specs=in_specs, out_specs=o_spec,
        out_shape=jax.ShapeDtypeStruct((M, N), out_dtype), scratch_shapes=scratch, name=name,
        compiler_params=_params("parallel", "parallel", "arbitrary"),
    )(*ins)


def _accumulate(acc_ref, p, k, nk, finish):
    @pl.when(k == 0)
    def _():
        acc_ref[...] = p

    @pl.when(k > 0)
    def _():
        acc_ref[...] += p

    @pl.when(k == nk - 1)
    def _():
        finish(acc_ref[...])


def _swiglu_fwd(hn, wg, wu, name):
    S, K = hn.shape
    F = wg.shape[1]
    tm = _pick(S, (1024, 512))
    tn = _pick(F, (512, 256, 128))

    def body(a_ref, wg_ref, wu_ref, g_ref, u_ref, act_ref):
        a = a_ref[...]
        g = _bdot(a, wg_ref[...], NN)
        u = _bdot(a, wu_ref[...], NN)
        g_ref[...] = g.astype(BF16)
        u_ref[...] = u.astype(BF16)
        act_ref[...] = (g * _sigmoid(g) * u).astype(BF16)

    w_spec = pl.BlockSpec((K, tn), lambda i, j: (0, j))
    o_spec = pl.BlockSpec((tm, tn), lambda i, j: (i, j))
    sh = jax.ShapeDtypeStruct((S, F), BF16)
    return pl.pallas_call(
        body, grid=(S // tm, F // tn), in_specs=[pl.BlockSpec((tm, K), lambda i, j: (i, 0)), w_spec, w_spec],
        out_specs=[o_spec, o_spec, o_spec], out_shape=[sh, sh, sh], name=name,
        compiler_params=_params("parallel", "parallel"),
    )(hn, wg, wu)


def _swiglu_bwd(dx, wd, g, u, name):
    S, K = dx.shape
    F = wd.shape[0]
    tm = _pick(S, (1024, 512))
    tn = _pick(F, (512, 256, 128))

    def body(dx_ref, wd_ref, g_ref, u_ref, dg_ref, du_ref):
        da = _bdot(dx_ref[...], wd_ref[...], NT)
        gv = g_ref[...].astype(F32)
        uv = u_ref[...].astype(F32)
        s = _sigmoid(gv)
        dg_ref[...] = (da * uv * _silu_grad(gv, s)).astype(BF16)
        du_ref[...] = (da * gv * s).astype(BF16)

    o_spec = pl.BlockSpec((tm, tn), lambda i, j: (i, j))
    sh = jax.ShapeDtypeStruct((S, F), BF16)
    return pl.pallas_call(
        body, grid=(S // tm, F // tn),
        in_specs=[pl.BlockSpec((tm, K), lambda i, j: (i, 0)), pl.BlockSpec((tn, K), lambda i, j: (j, 0)), o_spec, o_spec],
        out_specs=[o_spec, o_spec], out_shape=[sh, sh], name=name, compiler_params=_params("parallel", "parallel"),
    )(dx, wd, g, u)


def _mm_nt2(a1, b1, a2, b2, name):
    M, K = a1.shape
    N = b1.shape[0]
    tm = _pick(M, (1024, 512))
    tn = _pick(N, (1024, 512))
    tk = _pick(K, (1408, 1024, 512, 256, 128))
    nk = K // tk

    def body(a1_ref, b1_ref, a2_ref, b2_ref, o_ref, acc_ref):
        def finish(acc):
            o_ref[...] = acc

        p = _bdot(a1_ref[...], b1_ref[...], NT) + _bdot(a2_ref[...], b2_ref[...], NT)
        _accumulate(acc_ref, p, pl.program_id(2), nk, finish)

    a_spec = pl.BlockSpec((tm, tk), lambda i, j, k: (i, k))
    b_spec = pl.BlockSpec((tn, tk), lambda i, j, k: (j, k))
    return pl.pallas_call(
        body, grid=(M // tm, N // tn, nk), in_specs=[a_spec, b_spec, a_spec, b_spec],
        out_specs=pl.BlockSpec((tm, tn), lambda i, j, k: (i, j)), out_shape=jax.ShapeDtypeStruct((M, N), F32),
        scratch_shapes=[pltpu.VMEM((tm, tn), F32)], name=name,
        compiler_params=_params("parallel", "parallel", "arbitrary"),
    )(a1, b1, a2, b2)


XBC_BLK0 = XBC_COL // 128


def _conv_fwd(proj, w, b, name):
    S = proj.shape[0]
    T = min(512, S)

    def body(x_ref, w_ref, b_ref, o_ref, xp_ref):
        xp_ref[pl.ds(0, 8), :] = jnp.zeros((8, 128), F32)
        xp_ref[pl.ds(8, S), :] = x_ref[...]
        wv = w_ref[...]
        bv = b_ref[...]

        def step(c, carry):
            base = pl.multiple_of(c * T, T)
            acc = wv[0:1] * xp_ref[pl.ds(base + 5, T), :]
            for i in range(1, SSD_CONV):
                acc = acc + wv[i:i + 1] * xp_ref[pl.ds(base + 5 + i, T), :]
            acc = bv + acc
            o_ref[pl.ds(base, T), :] = acc * _sigmoid(acc)
            return carry

        lax.fori_loop(0, S // T, step, 0)

    return pl.pallas_call(
        body, grid=(SSD_CONV_CH // 128,),
        in_specs=[pl.BlockSpec((S, 128), lambda j: (0, XBC_BLK0 + j)), pl.BlockSpec((SSD_CONV, 128), lambda j: (0, j)),
                  pl.BlockSpec((1, 128), lambda j: (0, j))],
        out_specs=pl.BlockSpec((S, 128), lambda j: (0, j)),
        out_shape=jax.ShapeDtypeStruct((S, SSD_CONV_CH), F32),
        scratch_shapes=[pltpu.VMEM((S + 8, 128), F32)], name=name, compiler_params=_params("parallel"),
    )(proj, w, b)


def _conv_bwd(dxs, dbm, dcm, proj, w, b, dproj, name):
    S = proj.shape[0]
    T = min(512, S)
    NX, NB = SSD_WIDTH // 128, SSD_GROUPS * SSD_STATE // 128

    def body(dxs_ref, dbm_ref, dcm_ref, x_ref, w_ref, b_ref, dproj_ref, dx_ref, dw_ref, db_ref, xp_ref, dcp_ref):
        j = pl.program_id(0)

        @pl.when(j < NX)
        def _():
            dcp_ref[pl.ds(0, S), :] = dxs_ref[...]

        @pl.when((j >= NX) & (j < NX + NB))
        def _():
            dcp_ref[pl.ds(0, S), :] = dbm_ref[...]

        @pl.when(j >= NX + NB)
        def _():
            dcp_ref[pl.ds(0, S), :] = dcm_ref[...]

        da_ref = dcp_ref
        xp_ref[pl.ds(0, 8), :] = jnp.zeros((8, 128), F32)
        xp_ref[pl.ds(8, S), :] = x_ref[...]
        dcp_ref[pl.ds(S, 8), :] = jnp.zeros((8, 128), F32)
        wv = w_ref[...]
        bv = b_ref[...]

        def step1(c, carry):
            base = pl.multiple_of(c * T, T)
            xs = [xp_ref[pl.ds(base + 5 + i, T), :] for i in range(SSD_CONV)]
            acc = wv[0:1] * xs[0]
            for i in range(1, SSD_CONV):
                acc = acc + wv[i:i + 1] * xs[i]
            acc = bv + acc
            s = _sigmoid(acc)
            dc = da_ref[pl.ds(base, T), :] * _silu_grad(acc, s)
            dcp_ref[pl.ds(base, T), :] = dc
            new = tuple(carry[i] + jnp.sum(xs[i] * dc, axis=0, keepdims=True) for i in range(SSD_CONV))
            return new + (carry[SSD_CONV] + jnp.sum(dc, axis=0, keepdims=True),)

        z = jnp.zeros((1, 128), F32)
        res = lax.fori_loop(0, S // T, step1, (z,) * (SSD_CONV + 1))
        for i in range(SSD_CONV):
            dw_ref[pl.ds(i, 1), :] = res[i]
        db_ref[...] = res[SSD_CONV]

        def step2(c, carry):
            base = pl.multiple_of(c * T, T)
            acc = wv[0:1] * dcp_ref[pl.ds(base + 3, T), :]
            for i in range(1, SSD_CONV):
                acc = acc + wv[i:i + 1] * dcp_ref[pl.ds(base + 3 - i, T), :]
            dx_ref[pl.ds(base, T), :] = acc.astype(dx_ref.dtype)
            return carry

        lax.fori_loop(0, S // T, step2, 0)

    clamp = lambda j, lo, n: jnp.clip(j - lo, 0, n - 1)
    return pl.pallas_call(
        body, grid=(SSD_CONV_CH // 128,),
        in_specs=[pl.BlockSpec((S, 128), lambda j: (0, clamp(j, 0, NX))), pl.BlockSpec((S, 128), lambda j: (0, clamp(j, NX, NB))),
                  pl.BlockSpec((S, 128), lambda j: (0, clamp(j, NX + NB, NB))),
                  pl.BlockSpec((S, 128), lambda j: (0, XBC_BLK0 + j)), pl.BlockSpec((SSD_CONV, 128), lambda j: (0, j)),
                  pl.BlockSpec((1, 128), lambda j: (0, j)), pl.BlockSpec(memory_space=pl.ANY)],
        out_specs=[pl.BlockSpec((S, 128), lambda j: (0, XBC_BLK0 + j)), pl.BlockSpec((SSD_CONV, 128), lambda j: (0, j)),
                   pl.BlockSpec((1, 128), lambda j: (0, j))],
        out_shape=[jax.ShapeDtypeStruct(dproj.shape, dproj.dtype), jax.ShapeDtypeStruct((SSD_CONV, SSD_CONV_CH), F32),
                   jax.ShapeDtypeStruct((1, SSD_CONV_CH), F32)],
        input_output_aliases={6: 0},
        scratch_shapes=[pltpu.VMEM((S + 8, 128), F32), pltpu.VMEM((S + 8, 128), F32)], name=name,
        compiler_params=_params("arbitrary"),
    )(dxs, dbm, dcm, proj, w, b, dproj)


HPG = SSD_HEADS // SSD_GROUPS
GW = HPG * SSD_HEAD_DIM


def _ssd_chunk_terms(dtr, bias, alog, tril, triu):
    pre = dtr + bias
    dt = jnp.maximum(pre, 0.0) + jnp.log(1.0 + jnp.exp(-jnp.abs(pre)))
    a_neg = -jnp.exp(alog)
    a = dt * a_neg
    acum = _xdot(tril, a, NN)
    acum_t = _xdot(a, triu, TN)
    return pre, dt, a_neg, acum, acum_t


def _head_expanders():
    h64 = lax.broadcasted_iota(jnp.int32, (HPG, GW), 0) == lax.broadcasted_iota(jnp.int32, (HPG, GW), 1) // SSD_HEAD_DIM
    h128 = lax.broadcasted_iota(jnp.int32, (HPG, HPG * CHUNK), 0) == lax.broadcasted_iota(jnp.int32, (HPG, HPG * CHUNK), 1) // CHUNK
    return h64.astype(F32), h128.astype(F32)


def _ssd_fwd(xbc, proj, dtr, dt_bias, a_log, d_rep, gain, name):
    S = xbc.shape[0]
    L = CHUNK
    T = min(512, S)
    CPS = T // L
    NC = S // L

    def body(x_ref, b_ref, c_ref, z_ref, dtr_ref, bias_ref, alog_ref, d_ref, gain_ref, y_ref, yraw_ref, st_ref, state):
        i = pl.program_id(1)

        @pl.when(i == 0)
        def _():
            state[...] = jnp.zeros_like(state)

        row = lax.broadcasted_iota(jnp.int32, (L, L), 0)
        col = lax.broadcasted_iota(jnp.int32, (L, L), 1)
        causal = row >= col
        tril = causal.astype(F32)
        triu = (row <= col).astype(F32)
        low = col < SSD_HEAD_DIM
        e64, e128 = _head_expanders()
        for c in range(CPS):
            rows = pl.ds(c * L, L)
            xv = x_ref[rows, :]
            bm = b_ref[rows, :]
            cm = c_ref[rows, :]
            _, dt, _, acum, acum_t = _ssd_chunk_terms(dtr_ref[rows, :], bias_ref[...], alog_ref[...], tril, triu)
            ac = _xdot(acum, e64, NN)
            ac_sq = _xdot(acum, e128, NN)
            xd = xv * _xdot(dt, e64, NN)
            ac_last = ac[L - 1:L, :]
            sp = state[...]
            st_ref[c] = sp
            yoff = _bdot(cm, sp, NN) * jnp.exp(ac)
            state[...] = sp * jnp.exp(ac_last) + _bdot(bm, xd * jnp.exp(ac_last - ac), TN)
            gmat = _bdot(cm, bm, NT)
            for q in range(HPG // 2):
                pair = slice(q * 128, (q + 1) * 128)
                tile = xd[:, pair]
                y = yoff[:, pair]
                for j, keep in ((2 * q, low), (2 * q + 1, ~low)):
                    lam = jnp.exp(jnp.where(causal, ac_sq[:, j * L:(j + 1) * L] - acum_t[j:j + 1, :], NEG))
                    y = y + _bdot(gmat * lam, jnp.where(keep, tile, 0.0), NN)
                yraw_ref[rows, pair] = y
            zz = z_ref[rows, :]
            u = (yraw_ref[rows, :] + xv * d_ref[...]) * (zz * _sigmoid(zz))
            r = lax.rsqrt(jnp.mean(u * u, axis=-1, keepdims=True) + EPS)
            y_ref[rows, :] = (u * r * gain_ref[...]).astype(y_ref.dtype)

    vec8 = pl.BlockSpec((None, 1, HPG), lambda g, i: (g, 0, 0))
    return pl.pallas_call(
        body, grid=(SSD_GROUPS, S // T),
        in_specs=[pl.BlockSpec((T, GW), lambda g, i: (i, g)),
                  pl.BlockSpec((T, SSD_STATE), lambda g, i: (i, SSD_WIDTH // SSD_STATE + g)),
                  pl.BlockSpec((T, SSD_STATE), lambda g, i: (i, SSD_WIDTH // SSD_STATE + SSD_GROUPS + g)),
                  pl.BlockSpec((T, GW), lambda g, i: (i, Z_COL // GW + g)),
                  pl.BlockSpec((None, T, HPG), lambda g, i: (g, i, 0)),
                  vec8, vec8,
                  pl.BlockSpec((1, GW), lambda g, i: (0, g)), pl.BlockSpec((1, GW), lambda g, i: (0, g))],
        out_specs=[pl.BlockSpec((T, GW), lambda g, i: (i, g)), pl.BlockSpec((T, GW), lambda g, i: (i, g)),
                   pl.BlockSpec((CPS, None, SSD_STATE, GW), lambda g, i: (i, g, 0, 0))],
        out_shape=[jax.ShapeDtypeStruct((S, MIX_WIDTH), BF16), jax.ShapeDtypeStruct((S, SSD_WIDTH), F32),
                   jax.ShapeDtypeStruct((NC, SSD_GROUPS, SSD_STATE, GW), F32)],
        scratch_shapes=[pltpu.VMEM((SSD_STATE, GW), F32)], name=name,
        compiler_params=_params("arbitrary", "arbitrary"),
    )(xbc, xbc, xbc, proj, dtr, dt_bias, a_log, d_rep, gain)


def _ssd_bwd(dy, yraw, xbc, proj, dtr, dt_bias, a_log, d_rep, gain, states, name):
    S = xbc.shape[0]
    L = CHUNK
    T = min(512, S)
    CPS = T // L
    NI = S // T

    def body(dy_ref, yraw_ref, x_ref, b_ref, c_ref, z_ref, dtr_ref, bias_ref, alog_ref, d_ref, gain_ref, st_ref,
             dz_ref, dx_ref, db_ref, dc_ref, ddtr_ref, dbias_ref, dalog_ref, dd_ref, dgain_ref, dstate, dxd_ref):
        i = pl.program_id(1)

        @pl.when(i == 0)
        def _():
            dstate[...] = jnp.zeros_like(dstate)
            dbias_ref[...] = jnp.zeros_like(dbias_ref)
            dalog_ref[...] = jnp.zeros_like(dalog_ref)
            dd_ref[...] = jnp.zeros_like(dd_ref)
            dgain_ref[...] = jnp.zeros_like(dgain_ref)

        row = lax.broadcasted_iota(jnp.int32, (L, L), 0)
        col = lax.broadcasted_iota(jnp.int32, (L, L), 1)
        causal = row >= col
        tril = causal.astype(F32)
        triu = (row <= col).astype(F32)
        low = col < SSD_HEAD_DIM
        e64, e128 = _head_expanders()
        lane8 = lax.broadcasted_iota(jnp.int32, (1, HPG), 1)
        sub8 = lax.broadcasted_iota(jnp.int32, (HPG, 1), 0)
        eye8 = (lax.broadcasted_iota(jnp.int32, (HPG, HPG), 0) == lax.broadcasted_iota(jnp.int32, (HPG, HPG), 1)).astype(F32)
        last_row = (lax.broadcasted_iota(jnp.int32, (L, 1), 0) == L - 1).astype(F32)
        for c in reversed(range(CPS)):
            rows = pl.ds(c * L, L)
            xv = x_ref[rows, :]
            bm = b_ref[rows, :]
            cm = c_ref[rows, :]
            zz = z_ref[rows, :]
            dvec = d_ref[...]
            sz = _sigmoid(zz)
            silu_z = zz * sz
            v = yraw_ref[rows, :] + xv * dvec
            u = v * silu_z
            r = lax.rsqrt(jnp.mean(u * u, axis=-1, keepdims=True) + EPS)
            n = u * r
            do = dy_ref[rows, :]
            dgain_ref[...] += jnp.sum(do * n, axis=0, keepdims=True)
            dn = do * gain_ref[...]
            du = r * (dn - n * jnp.mean(dn * n, axis=-1, keepdims=True))
            dz_ref[rows, :] = (du * v * _silu_grad(zz, sz)).astype(dz_ref.dtype)
            dyv = du * silu_z
            dd_ref[...] += _xdot(jnp.sum(dyv * xv, axis=0, keepdims=True), e64, NT)
            pre, dt, a_neg, acum, acum_t = _ssd_chunk_terms(dtr_ref[rows, :], bias_ref[...], alog_ref[...], tril, triu)
            ac = _xdot(acum, e64, NN)
            ac_sq = _xdot(acum, e128, NN)
            dt_w = _xdot(dt, e64, NN)
            xd = xv * dt_w
            ac_last = ac[L - 1:L, :]
            ea = jnp.exp(ac)
            w = jnp.exp(ac_last - ac)
            ea_last = jnp.exp(ac_last)
            sp = st_ref[c]
            ds = dstate[...]
            dye = dyv * ea
            yoff = _bdot(cm, sp, NN) * ea
            bds = _bdot(bm, ds, NN)
            dcm = _bdot(dye, sp, NT)
            dbm = _bdot(xd * w, ds, NT)
            dstate[...] = ds * ea_last + _bdot(cm, dye, TN)
            w8 = jnp.exp(acum[L - 1:L, :] - acum)
            dw8 = _xdot(xd * bds, e64, NT)
            dac8 = _xdot(dyv * yoff, e64, NT) - dw8 * w8
            tail8 = jnp.sum(dw8 * w8, axis=0, keepdims=True) + jnp.exp(acum[L - 1:L, :]) * _xdot(
                jnp.sum(ds * sp, axis=0, keepdims=True), e64, NT)
            dac8 = dac8 + last_row * tail8
            gmat = _bdot(cm, bm, NT)
            dgmat = jnp.zeros((L, L), F32)
            colsum_t = jnp.zeros((HPG, L), F32)
            for q in range(HPG // 2):
                pair = slice(q * 128, (q + 1) * 128)
                xd_tile = xd[:, pair]
                dy_tile = dyv[:, pair]
                dxd_tile = bds[:, pair] * w[:, pair]
                for j, keep in ((2 * q, low), (2 * q + 1, ~low)):
                    lam = jnp.exp(jnp.where(causal, ac_sq[:, j * L:(j + 1) * L] - acum_t[j:j + 1, :], NEG))
                    mh = gmat * lam
                    dyj = jnp.where(keep, dy_tile, 0.0)
                    dxd_tile = dxd_tile + _bdot(mh, dyj, TN)
                    dm = _bdot(dyj, xd_tile, NT)
                    dgmat = dgmat + dm * lam
                    qm = dm * mh
                    dac8 = dac8 + jnp.sum(qm, axis=1, keepdims=True) * (lane8 == j).astype(F32)
                    colsum_t = colsum_t + (sub8 == j).astype(F32) * jnp.sum(qm, axis=0, keepdims=True)
                dxd_ref[:, pair] = dxd_tile
            dac8 = dac8 - _xdot(colsum_t, eye8, TN)
            dxd = dxd_ref[...]
            dx_ref[rows, :] = dxd * dt_w + dyv * dvec
            dc_ref[rows, :] = dcm + _bdot(dgmat, bm, NN)
            db_ref[rows, :] = dbm + _bdot(dgmat, cm, TN)
            da8 = _xdot(triu, dac8, NN)
            ddt8 = _xdot(dxd * xv, e64, NT) + da8 * a_neg
            dalog_ref[...] += jnp.sum(da8 * dt, axis=0, keepdims=True) * a_neg
            dpre = ddt8 * _sigmoid(pre)
            ddtr_ref[rows, :] = dpre
            dbias_ref[...] += jnp.sum(dpre, axis=0, keepdims=True)

    rev = lambda i: NI - 1 - i
    vec8 = pl.BlockSpec((None, 1, HPG), lambda g, i: (g, 0, 0))
    grp = pl.BlockSpec((T, GW), lambda g, i: (rev(i), g))
    bspec = pl.BlockSpec((T, SSD_STATE), lambda g, i: (rev(i), SSD_WIDTH // SSD_STATE + g))
    cspec = pl.BlockSpec((T, SSD_STATE), lambda g, i: (rev(i), SSD_WIDTH // SSD_STATE + SSD_GROUPS + g))
    gvec = pl.BlockSpec((1, GW), lambda g, i: (0, g))
    st_spec = pl.BlockSpec((CPS, None, SSD_STATE, GW), lambda g, i: (rev(i), g, 0, 0))
    small = jax.ShapeDtypeStruct((SSD_GROUPS, 1, HPG), F32)
    zspec = pl.BlockSpec((T, GW), lambda g, i: (rev(i), Z_COL // GW + g))
    return pl.pallas_call(
        body, grid=(SSD_GROUPS, NI),
        in_specs=[grp, grp, grp, bspec, cspec, zspec, pl.BlockSpec((None, T, HPG), lambda g, i: (g, rev(i), 0)),
                  vec8, vec8, gvec, gvec, st_spec],
        out_specs=[zspec, grp, pl.BlockSpec((T, SSD_STATE), lambda g, i: (rev(i), g)),
                   pl.BlockSpec((T, SSD_STATE), lambda g, i: (rev(i), g)),
                   pl.BlockSpec((None, T, HPG), lambda g, i: (g, rev(i), 0)), vec8, vec8, vec8, gvec],
        out_shape=[jax.ShapeDtypeStruct((S, IN_PAD), BF16), jax.ShapeDtypeStruct((S, SSD_WIDTH), F32),
                   jax.ShapeDtypeStruct((S, SSD_GROUPS * SSD_STATE), F32), jax.ShapeDtypeStruct((S, SSD_GROUPS * SSD_STATE), F32),
                   jax.ShapeDtypeStruct((SSD_GROUPS, S, HPG), F32), small, small, small,
                   jax.ShapeDtypeStruct((1, SSD_WIDTH), F32)],
        scratch_shapes=[pltpu.VMEM((SSD_STATE, GW), F32), pltpu.VMEM((L, GW), F32)],
        name=name, compiler_params=_params("arbitrary", "arbitrary"),
    )(dy, yraw, xbc, xbc, xbc, proj, dtr, dt_bias, a_log, d_rep, gain, states)


def _swap_halves(t):
    w = t.shape[1]
    lane = lax.broadcasted_iota(jnp.int32, t.shape, 1)
    return jnp.where((lane % 64) < 32, pltpu.roll(t, w - 32, axis=1), pltpu.roll(t, 32, axis=1))


def _widen(tab, w):
    return tab if w == 128 else jnp.concatenate([tab] * (w // 128), axis=1)


def _rope(t, cos, sin_signed):
    return t * cos + _swap_halves(t) * sin_signed


def _rope_t(d, cos, sin_signed):
    return d * cos - _swap_halves(d) * sin_signed


def _group_sum64(v, bd):
    hi = v.astype(BF16)
    lo = (v - hi.astype(F32)).astype(BF16)
    return (lax.dot_general(hi, bd, NN, preferred_element_type=F32)
            + lax.dot_general(lo, bd, NN, preferred_element_type=F32))


AQ_BLK = ATT_COL // ATT_WIDTH


def _att_prep_fwd(proj, qg, kg, cos, sin, bd, name):
    S = proj.shape[0]
    T = min(512, S)
    PB = ATT_SPAN // T
    src = lambda i: jnp.maximum(i - PB, 0)

    def body(q_ref, k_ref, v_ref, qg_ref, kg_ref, cos_ref, sin_ref, bd_ref, qo_ref, ko_ref, vo_ref):
        i = pl.program_id(0)

        @pl.when(i < PB)
        def _():
            ko_ref[...] = jnp.zeros_like(ko_ref)
            vo_ref[...] = jnp.zeros_like(vo_ref)

        @pl.when(i >= PB)
        def _():
            cw = _widen(cos_ref[...], ATT_WIDTH)
            sw = _widen(sin_ref[...], ATT_WIDTH)
            bdv = bd_ref[...]

            def norm_rope(t, gain):
                ss = _group_sum64(t * t, bdv)
                return _rope(t * lax.rsqrt(ss * (1.0 / ATT_HEAD_DIM) + EPS) * gain, cw, sw)

            qo_ref[...] = (norm_rope(q_ref[...], qg_ref[...]) * (ATT_HEAD_DIM ** -0.5)).astype(BF16)
            kt = norm_rope(k_ref[...], kg_ref[...]).astype(BF16)
            vt = v_ref[...].astype(BF16)
            for pr in range(ATT_HEADS // 2):
                ko_ref[pr] = kt[:, pr * 128:(pr + 1) * 128]
                vo_ref[pr] = vt[:, pr * 128:(pr + 1) * 128]

    vec = pl.BlockSpec((1, ATT_WIDTH), lambda i: (0, 0))
    tab = pl.BlockSpec((T, 128), lambda i: (src(i), 0))
    hm = pl.BlockSpec((ATT_HEADS // 2, T, 128), lambda i: (0, i, 0))
    hm_shape = jax.ShapeDtypeStruct((ATT_HEADS // 2, ATT_SPAN + S, 128), BF16)
    return pl.pallas_call(
        body, grid=(PB + S // T,),
        in_specs=[pl.BlockSpec((T, ATT_WIDTH), lambda i: (src(i), AQ_BLK)), pl.BlockSpec((T, ATT_WIDTH), lambda i: (src(i), AQ_BLK + 1)),
                  pl.BlockSpec((T, ATT_WIDTH), lambda i: (src(i), AQ_BLK + 2)), vec, vec, tab, tab,
                  pl.BlockSpec((ATT_WIDTH, ATT_WIDTH), lambda i: (0, 0))],
        out_specs=[pl.BlockSpec((T, ATT_WIDTH), lambda i: (src(i), 0)), hm, hm],
        out_shape=[jax.ShapeDtypeStruct((S, ATT_WIDTH), BF16), hm_shape, hm_shape],
        name=name, compiler_params=_params("arbitrary"),
    )(proj, proj, proj, qg, kg, cos, sin, bd)


def _att_prep_bwd(proj, dq, dk_p, dv_p, qg, kg, cos, sin, bd, dproj, name):
    S = proj.shape[0]
    T = min(512, S)
    NI = S // T
    PB = ATT_SPAN // T
    W = ATT_WIDTH

    def body(q_ref, k_ref, dq_ref, dkp_ref, dvp_ref, qg_ref, kg_ref, cos_ref, sin_ref, bd_ref, dproj_ref,
             do_ref, dqg_ref, dkg_ref, acc_ref):
        i = pl.program_id(0)

        @pl.when(i == 0)
        def _():
            acc_ref[...] = jnp.zeros_like(acc_ref)

        npair = ATT_HEADS // 2
        dk_all = jnp.concatenate([dkp_ref[pr].T for pr in range(npair)], axis=1)
        do_ref[:, 2 * W:3 * W] = jnp.concatenate([dvp_ref[pr].T for pr in range(npair)], axis=1).astype(BF16)
        cw = _widen(cos_ref[...], ATT_WIDTH)
        sw = _widen(sin_ref[...], ATT_WIDTH)
        bdv = bd_ref[...]

        def one(t, d_rot, gain, scale, slot):
            ss = _group_sum64(t * t, bdv)
            r = lax.rsqrt(ss * (1.0 / ATT_HEAD_DIM) + EPS)
            n = t * r
            d_ng = _rope_t(d_rot * scale, cw, sw)
            acc_ref[pl.ds(slot, 1), :] += jnp.sum(d_ng * n, axis=0, keepdims=True)
            dn = d_ng * gain
            return r * (dn - n * (_group_sum64(dn * n, bdv) * (1.0 / ATT_HEAD_DIM)))

        do_ref[:, 0:W] = one(q_ref[...], dq_ref[...], qg_ref[...], ATT_HEAD_DIM ** -0.5, 0).astype(BF16)
        do_ref[:, W:2 * W] = one(k_ref[...], dk_all, kg_ref[...], 1.0, 1).astype(BF16)

        @pl.when(i == NI - 1)
        def _():
            a = acc_ref[...]
            f = a[:, 0:64]
            for h in range(1, ATT_HEADS):
                f = f + a[:, h * 64:(h + 1) * 64]
            dqg_ref[...] = f[0:1]
            dkg_ref[...] = f[1:2]

    vec = pl.BlockSpec((1, ATT_WIDTH), lambda i: (0, 0))
    tab = pl.BlockSpec((T, 128), lambda i: (i, 0))
    row = pl.BlockSpec((T, ATT_WIDTH), lambda i: (i, 0))
    g64 = pl.BlockSpec((1, ATT_HEAD_DIM), lambda i: (0, 0))
    padded = pl.BlockSpec((ATT_HEADS // 2, 128, T), lambda i: (0, 0, i + PB))
    return pl.pallas_call(
        body, grid=(NI,),
        in_specs=[pl.BlockSpec((T, ATT_WIDTH), lambda i: (i, AQ_BLK)), pl.BlockSpec((T, ATT_WIDTH), lambda i: (i, AQ_BLK + 1)),
                  row, padded, padded, vec, vec, tab, tab, pl.BlockSpec((ATT_WIDTH, ATT_WIDTH), lambda i: (0, 0)),
                  pl.BlockSpec(memory_space=pl.ANY)],
        out_specs=[pl.BlockSpec((T, 3 * W), lambda i: (i, ATT_COL // (3 * W))), g64, g64],
        out_shape=[jax.ShapeDtypeStruct(dproj.shape, dproj.dtype), jax.ShapeDtypeStruct((1, ATT_HEAD_DIM), F32),
                   jax.ShapeDtypeStruct((1, ATT_HEAD_DIM), F32)],
        input_output_aliases={10: 0},
        scratch_shapes=[pltpu.VMEM((8, ATT_WIDTH), F32)], name=name, compiler_params=_params("arbitrary"),
    )(proj, proj, dq, dk_p, dv_p, qg, kg, cos, sin, bd, dproj)


def _att_bias():
    qpos = np.arange(CHUNK)[:, None] + ATT_SPAN
    kpos = np.arange(ATT_STRIP)[None, :]
    rel = qpos - kpos
    mult = np.zeros((CHUNK, ATT_STRIP), np.float64)
    for window, dil in DILATED_PAIRS:
        mult += (rel >= 0) & (rel % dil == 0) & (rel // dil <= window // dil)
    return np.where(mult > 0, np.log(np.maximum(mult, 1.0)), NEG).astype(np.float32)


def _att_scores(q, ks, bias, i):
    s = _bdot(q, ks, NT) + bias
    kcol = lax.broadcasted_iota(jnp.int32, (1, ATT_STRIP), 1) + i * CHUNK
    return jnp.where(kcol >= ATT_SPAN, s, NEG)


def _pair_masks():
    low = lax.broadcasted_iota(jnp.int32, (CHUNK, 128), 1) < ATT_HEAD_DIM
    return low, ~low


def _att_fwd(q, kp, vp, bias, y, name):
    S = q.shape[0]
    SP = kp.shape[1]

    def body(q_ref, k_ref, v_ref, bias_ref, y_ref, o_ref):
        i = pl.program_id(1)
        strip = pl.ds(pl.multiple_of(i * CHUNK, CHUNK), ATT_STRIP)
        qv = q_ref[...]
        ks = k_ref[strip, :]
        vs = v_ref[strip, :]
        outs = []
        for keep in _pair_masks():
            s = _att_scores(jnp.where(keep, qv, jnp.zeros_like(qv)), ks, bias_ref[...], i)
            m = jnp.max(s, axis=-1, keepdims=True)
            p = jnp.exp(s - m)
            den = jnp.sum(p, axis=-1, keepdims=True)
            outs.append(_bdot(p, vs, NN) / den)
        o_ref[...] = jnp.where(_pair_masks()[0], outs[0], outs[1]).astype(o_ref.dtype)

    kv = pl.BlockSpec((None, SP, 128), lambda hp, i: (hp, 0, 0))
    return pl.pallas_call(
        body, grid=(ATT_HEADS // 2, S // CHUNK),
        in_specs=[pl.BlockSpec((CHUNK, 128), lambda hp, i: (i, hp)), kv, kv,
                  pl.BlockSpec((CHUNK, ATT_STRIP), lambda hp, i: (0, 0)), pl.BlockSpec(memory_space=pl.ANY)],
        out_specs=pl.BlockSpec((CHUNK, 128), lambda hp, i: (i, SSD_WIDTH // 128 + hp)),
        out_shape=jax.ShapeDtypeStruct(y.shape, y.dtype), input_output_aliases={4: 0}, name=name,
        compiler_params=_params("parallel", "arbitrary"),
    )(q, kp, vp, bias, y)


def _att_bwd(q, kp, vp, bias, dy, name):
    S = q.shape[0]
    SP = kp.shape[1]

    def body(q_ref, k_ref, v_ref, bias_ref, do_ref, dq_ref, dk_ref, dv_ref):
        i = pl.program_id(1)

        @pl.when(i == 0)
        def _():
            dk_ref[...] = jnp.zeros_like(dk_ref)
            dv_ref[...] = jnp.zeros_like(dv_ref)

        strip = pl.ds(pl.multiple_of(i * CHUNK, CHUNK), ATT_STRIP)
        qv = q_ref[...]
        dov = do_ref[...]
        ks = k_ref[strip, :]
        vs = v_ref[strip, :]
        dq = jnp.zeros((CHUNK, 128), F32)
        dk_t = jnp.zeros((128, ATT_STRIP), F32)
        dv_t = jnp.zeros((128, ATT_STRIP), F32)
        for keep in _pair_masks():
            qh = jnp.where(keep, qv, jnp.zeros_like(qv))
            doh = jnp.where(keep, dov, 0.0)
            s = _att_scores(qh, ks, bias_ref[...], i)
            m = jnp.max(s, axis=-1, keepdims=True)
            p = jnp.exp(s - m)
            p = p / jnp.sum(p, axis=-1, keepdims=True)
            dp = _bdot(doh, vs, NT)
            dsc = p * (dp - jnp.sum(p * dp, axis=-1, keepdims=True))
            dq = dq + jnp.where(keep, _bdot(dsc, ks, NN), 0.0)
            dv_t = dv_t + _bdot(doh, p, TN)
            dk_t = dk_t + _bdot(qh, dsc, TN)
        dq_ref[...] = dq
        dv_ref[:, strip] += dv_t
        dk_ref[:, strip] += dk_t

    kv = pl.BlockSpec((None, SP, 128), lambda hp, i: (hp, 0, 0))
    kv_t = pl.BlockSpec((None, 128, SP), lambda hp, i: (hp, 0, 0))
    pairs = jax.ShapeDtypeStruct((ATT_HEADS // 2, 128, SP), F32)
    return pl.pallas_call(
        body, grid=(ATT_HEADS // 2, S // CHUNK),
        in_specs=[pl.BlockSpec((CHUNK, 128), lambda hp, i: (i, hp)), kv, kv,
                  pl.BlockSpec((CHUNK, ATT_STRIP), lambda hp, i: (0, 0)),
                  pl.BlockSpec((CHUNK, 128), lambda hp, i: (i, SSD_WIDTH // 128 + hp))],
        out_specs=[pl.BlockSpec((CHUNK, 128), lambda hp, i: (i, hp)), kv_t, kv_t],
        out_shape=[jax.ShapeDtypeStruct((S, ATT_WIDTH), F32), pairs, pairs],
        name=name, compiler_params=_params("parallel", "arbitrary"),
    )(q, kp, vp, bias, dy)


RQ_BLK = RET_COL // RET_QK_WIDTH
RV_BLK = (RET_COL + 2 * RET_QK_WIDTH) // RET_V_WIDTH
RET_LOG_GAMMA = tuple(math.log1p(-2.0 ** (-5.0 - h)) for h in range(RET_HEADS))


def _ret_decays(h):
    L = CHUNK
    lg = RET_LOG_GAMMA[h]
    row = lax.broadcasted_iota(jnp.int32, (L, L), 0)
    col = lax.broadcasted_iota(jnp.int32, (L, L), 1)
    rel = (row - col).astype(F32)
    dm = jnp.where(rel >= 0, jnp.exp(jnp.maximum(rel, 0.0) * lg), 0.0)
    idx = lax.broadcasted_iota(jnp.int32, (L, 1), 0).astype(F32)
    kte = jnp.exp((L - 1 - idx) * lg)
    qfs = jnp.exp((idx + 1.0) * lg)
    return dm, kte, qfs, math.exp(L * lg)


def _ret_fwd(proj, cos, sin, gain, y, name):
    S = proj.shape[0]
    L = CHUNK
    T = min(512, S)
    CPS = T // L
    NC = S // L

    def body(q_ref, k_ref, v_ref, g_ref, cos_ref, sin_ref, gain_ref, yin_ref, y_ref, o_ref, st_ref, state):
        i = pl.program_id(0)

        @pl.when(i == 0)
        def _():
            state[...] = jnp.zeros_like(state)

        dec = [_ret_decays(h) for h in range(RET_HEADS)]
        for c in range(CPS):
            rows = pl.ds(c * L, L)
            cw = _widen(cos_ref[rows, :], RET_QK_WIDTH)
            sw = _widen(sin_ref[rows, :], RET_QK_WIDTH)
            qv = _rope(q_ref[rows, :], cw, sw)
            kv = _rope(k_ref[rows, :], cw, sw) * (RET_QK_DIM ** -0.5)
            for h in range(RET_HEADS):
                dm, kte, qfs, cd = dec[h]
                qh = qv[:, h * 64:(h + 1) * 64]
                kh = kv[:, h * 64:(h + 1) * 64]
                vs = slice(h * RET_V_DIM, (h + 1) * RET_V_DIM)
                vh = v_ref[rows, vs]
                sp = state[h]
                st_ref[c, h] = sp
                o = _bdot(_bdot(qh, kh, NT) * dm, vh, NN) + _bdot(qh * qfs, sp, NN)
                state[h] = cd * sp + _bdot(kh * kte, vh, TN)
                o_ref[rows, vs] = o
                gh = g_ref[rows, vs]
                r = lax.rsqrt(jnp.mean(o * o, axis=-1, keepdims=True) + EPS)
                y_ref[rows, vs] = (o * r * gain_ref[:, vs] * (gh * _sigmoid(gh))).astype(y_ref.dtype)

    tab = pl.BlockSpec((T, 128), lambda i: (i, 0))
    wide = pl.BlockSpec((T, RET_V_WIDTH), lambda i: (i, 0))
    return pl.pallas_call(
        body, grid=(S // T,),
        in_specs=[pl.BlockSpec((T, RET_QK_WIDTH), lambda i: (i, RQ_BLK)), pl.BlockSpec((T, RET_QK_WIDTH), lambda i: (i, RQ_BLK + 1)),
                  pl.BlockSpec((T, RET_V_WIDTH), lambda i: (i, RV_BLK)), pl.BlockSpec((T, RET_V_WIDTH), lambda i: (i, RV_BLK + 1)),
                  tab, tab, pl.BlockSpec((1, RET_V_WIDTH), lambda i: (0, 0)), pl.BlockSpec(memory_space=pl.ANY)],
        out_specs=[pl.BlockSpec((T, RET_V_WIDTH), lambda i: (i, (SSD_WIDTH + ATT_WIDTH) // RET_V_WIDTH)), wide,
                   pl.BlockSpec((CPS, RET_HEADS, RET_QK_DIM, RET_V_DIM), lambda i: (i, 0, 0, 0))],
        out_shape=[jax.ShapeDtypeStruct(y.shape, y.dtype), jax.ShapeDtypeStruct((S, RET_V_WIDTH), F32),
                   jax.ShapeDtypeStruct((NC, RET_HEADS, RET_QK_DIM, RET_V_DIM), F32)],
        input_output_aliases={7: 0},
        scratch_shapes=[pltpu.VMEM((RET_HEADS, RET_QK_DIM, RET_V_DIM), F32)], name=name,
        compiler_params=_params("arbitrary"),
    )(proj, proj, proj, proj, cos, sin, gain, y)


def _ret_bwd(dy, oraw, proj, cos, sin, gain, states, dproj, name):
    S = proj.shape[0]
    L = CHUNK
    T = min(512, S)
    CPS = T // L
    NI = S // T
    QW, VW = RET_QK_WIDTH, RET_V_WIDTH
    V0, G0 = 2 * QW, 2 * QW + VW

    def body(dy_ref, o_ref, q_ref, k_ref, v_ref, g_ref, cos_ref, sin_ref, gain_ref, st_ref, dproj_ref,
             out_ref, dgain_ref, dstate, dqs, dks):
        i = pl.program_id(0)

        @pl.when(i == 0)
        def _():
            dstate[...] = jnp.zeros_like(dstate)
            dgain_ref[...] = jnp.zeros_like(dgain_ref)

        dec = [_ret_decays(h) for h in range(RET_HEADS)]
        for c in reversed(range(CPS)):
            rows = pl.ds(c * L, L)
            cw = _widen(cos_ref[rows, :], RET_QK_WIDTH)
            sw = _widen(sin_ref[rows, :], RET_QK_WIDTH)
            qv = _rope(q_ref[rows, :], cw, sw)
            kv = _rope(k_ref[rows, :], cw, sw) * (RET_QK_DIM ** -0.5)
            for h in range(RET_HEADS):
                dm, kte, qfs, cd = dec[h]
                qs = slice(h * 64, (h + 1) * 64)
                vs = slice(h * RET_V_DIM, (h + 1) * RET_V_DIM)
                qh = qv[:, qs]
                kh = kv[:, qs]
                vh = v_ref[rows, vs]
                gh = g_ref[rows, vs]
                gn = gain_ref[:, vs]
                o = o_ref[rows, vs]
                dyh = dy_ref[rows, vs]
                sg = _sigmoid(gh)
                silu_g = gh * sg
                r = lax.rsqrt(jnp.mean(o * o, axis=-1, keepdims=True) + EPS)
                n = o * r
                dgain_ref[:, vs] += jnp.sum(dyh * n * silu_g, axis=0, keepdims=True)
                out_ref[rows, G0 + h * RET_V_DIM:G0 + (h + 1) * RET_V_DIM] = (dyh * n * gn * _silu_grad(gh, sg)).astype(out_ref.dtype)
                dn = dyh * gn * silu_g
                do = r * (dn - n * jnp.mean(dn * n, axis=-1, keepdims=True))
                sp = st_ref[c, h]
                ds = dstate[h]
                sc = _bdot(qh, kh, NT) * dm
                dsc = _bdot(do, vh, NT) * dm
                out_ref[rows, V0 + h * RET_V_DIM:V0 + (h + 1) * RET_V_DIM] = (_bdot(sc, do, TN) + _bdot(kh * kte, ds, NN)).astype(out_ref.dtype)
                dqs[:, qs] = _bdot(dsc, kh, NN) + _bdot(do, sp, NT) * qfs
                dks[:, qs] = _bdot(dsc, qh, TN) + _bdot(vh, ds, NT) * kte
                dstate[h] = cd * ds + _bdot(qh * qfs, do, TN)
            out_ref[rows, 0:QW] = _rope_t(dqs[...], cw, sw).astype(out_ref.dtype)
            out_ref[rows, QW:2 * QW] = _rope_t(dks[...] * (RET_QK_DIM ** -0.5), cw, sw).astype(out_ref.dtype)

    rev = lambda i: NI - 1 - i
    tab = pl.BlockSpec((T, 128), lambda i: (rev(i), 0))
    wide = pl.BlockSpec((T, RET_V_WIDTH), lambda i: (rev(i), 0))
    group = pl.BlockSpec((T, G0 + VW), lambda i: (rev(i), RET_COL // (G0 + VW)))
    gvec = pl.BlockSpec((1, RET_V_WIDTH), lambda i: (0, 0))
    return pl.pallas_call(
        body, grid=(NI,),
        in_specs=[pl.BlockSpec((T, RET_V_WIDTH), lambda i: (rev(i), (SSD_WIDTH + ATT_WIDTH) // RET_V_WIDTH)), wide,
                  pl.BlockSpec((T, RET_QK_WIDTH), lambda i: (rev(i), RQ_BLK)), pl.BlockSpec((T, RET_QK_WIDTH), lambda i: (rev(i), RQ_BLK + 1)),
                  pl.BlockSpec((T, RET_V_WIDTH), lambda i: (rev(i), RV_BLK)), pl.BlockSpec((T, RET_V_WIDTH), lambda i: (rev(i), RV_BLK + 1)),
                  tab, tab, gvec,
                  pl.BlockSpec((CPS, RET_HEADS, RET_QK_DIM, RET_V_DIM), lambda i: (rev(i), 0, 0, 0)),
                  pl.BlockSpec(memory_space=pl.ANY)],
        out_specs=[group, gvec],
        out_shape=[jax.ShapeDtypeStruct(dproj.shape, dproj.dtype), jax.ShapeDtypeStruct((1, RET_V_WIDTH), F32)],
        input_output_aliases={10: 0},
        scratch_shapes=[pltpu.VMEM((RET_HEADS, RET_QK_DIM, RET_V_DIM), F32), pltpu.VMEM((L, RET_QK_WIDTH), F32),
                        pltpu.VMEM((L, RET_QK_WIDTH), F32)],
        name=name, compiler_params=_params("arbitrary"),
    )(dy, oraw, proj, proj, proj, proj, cos, sin, gain, states, dproj)


def _adamw_update(g_ref, nb, w_ref, m_ref, v_ref, go_ref, d_ref, mo_ref, vo_ref):
    g = g_ref[0].astype(F32)
    for k in range(1, nb):
        g = g + g_ref[k].astype(F32)
    mn = ADAM_B1 * m_ref[...] + (1.0 - ADAM_B1) * g
    vn = ADAM_B2 * v_ref[...] + (1.0 - ADAM_B2) * (g * g)
    go_ref[...] = g
    mo_ref[...] = mn
    vo_ref[...] = vn
    c1 = 1.0 - ADAM_B1 ** ADAM_STEP
    c2 = 1.0 - ADAM_B2 ** ADAM_STEP
    d_ref[...] = -ADAM_LR * ((mn / c1) / (jnp.sqrt(vn / c2) + ADAM_EPS) + ADAM_WD * w_ref[...])


def _adamw_rows(R, C):
    return _pick(R, tuple(t for t in (512, 256, 128, 64, 32, 16, 8) if t * C <= 256 * 1024))


def _adamw(gblocks, w, m, v, name):
    nb, R, C = gblocks.shape
    tr = _adamw_rows(R, C)

    def body(g_ref, *refs):
        _adamw_update(g_ref, nb, *refs)

    row = pl.BlockSpec((tr, C), lambda i: (i, 0))
    sh = jax.ShapeDtypeStruct((R, C), F32)
    return pl.pallas_call(
        body, grid=(R // tr,), in_specs=[pl.BlockSpec((nb, tr, C), lambda i: (0, i, 0)), row, row, row],
        out_specs=[row, row, row, row], out_shape=[sh, sh, sh, sh], name=name, compiler_params=_params("parallel"),
    )(gblocks, w, m, v)


def _adamw_layers(g0, g1, w, m, v, name):
    nb, R, C = g0.shape
    tr = _adamw_rows(R, C)

    def body(g0_ref, g1_ref, *refs):
        l = pl.program_id(0)

        @pl.when(l == 0)
        def _():
            _adamw_update(g0_ref, nb, *refs)

        @pl.when(l == 1)
        def _():
            _adamw_update(g1_ref, nb, *refs)

    row = pl.BlockSpec((None, tr, C), lambda l, i: (l, i, 0))
    sh = jax.ShapeDtypeStruct((DEPTH, R, C), F32)
    return pl.pallas_call(
        body, grid=(DEPTH, R // tr),
        in_specs=[pl.BlockSpec((nb, tr, C), lambda l, i: (0, i * (1 - l), 0)), pl.BlockSpec((nb, tr, C), lambda l, i: (0, i * l, 0)),
                  row, row, row],
        out_specs=[row, row, row, row], out_shape=[sh, sh, sh, sh], name=name, compiler_params=_params("arbitrary", "arbitrary"),
    )(g0, g1, w, m, v)


def _peers():
    x, y, c = lax.axis_index("x"), lax.axis_index("y"), lax.axis_index("c")
    flips = ((0, 0, 1), (1, 0, 0), (0, 1, 0), (1, 1, 0), (1, 0, 1), (0, 1, 1), (1, 1, 1))
    me = 4 * x + 2 * y + c
    peers = [(x ^ fx, y ^ fy, c ^ fc) for fx, fy, fc in flips]
    return me, peers


def _exchange(arrs, scatter, name):
    n = len(arrs)
    npeer = N_DEV - 1

    def body(*refs):
        ins, outs = refs[:n], refs[n:2 * n]
        send_sems, recv_sems, local_sems = refs[2 * n:]
        me, peers = _peers()
        copies = []
        for a in range(n):
            src_own = ins[a].at[me] if scatter else ins[a]
            own = pltpu.make_async_copy(src_own, outs[a].at[me], local_sems.at[a])
            own.start()
            copies.append(own)
            for k, peer in enumerate(peers):
                src = ins[a].at[4 * peer[0] + 2 * peer[1] + peer[2]] if scatter else ins[a]
                cp = pltpu.make_async_remote_copy(
                    src_ref=src, dst_ref=outs[a].at[me], send_sem=send_sems.at[a * npeer + k],
                    recv_sem=recv_sems.at[a * npeer + k], device_id=peer, device_id_type=pl.DeviceIdType.MESH)
                cp.start()
                copies.append(cp)
        for cp in copies:
            cp.wait()

    out_shape = [jax.ShapeDtypeStruct(((N_DEV,) + a.shape[1:]) if scatter else ((N_DEV,) + a.shape), a.dtype) for a in arrs]
    anyspec = pl.BlockSpec(memory_space=pl.ANY)
    return pl.pallas_call(
        body, in_specs=[anyspec] * n, out_specs=[anyspec] * n, out_shape=out_shape,
        scratch_shapes=[pltpu.SemaphoreType.DMA((n * npeer,)), pltpu.SemaphoreType.DMA((n * npeer,)),
                        pltpu.SemaphoreType.DMA((n,))],
        name=name,
    )(*arrs)


def _dev_index(peer):
    return 4 * peer[0] + 2 * peer[1] + peer[2]


def _push_copies(src_refs, land_refs, send_sems, recv_sems, scatter, as_receiver):
    me, peers = _peers()
    npeer = N_DEV - 1
    copies = []
    for a in range(len(src_refs)):
        for k, peer in enumerate(peers):
            src = src_refs[a].at[_dev_index(peer)] if scatter else src_refs[a]
            slot = _dev_index(peer) if as_receiver else me
            copies.append(pltpu.make_async_remote_copy(
                src_ref=src, dst_ref=land_refs[a].at[slot], send_sem=send_sems.at[a * npeer + k],
                recv_sem=recv_sems.at[a * npeer + k], device_id=peer, device_id_type=pl.DeviceIdType.MESH))
    return copies


def _push_start(srcs, lands, scatter, name):
    n = len(srcs)
    nsem = n * (N_DEV - 1)

    def body(*refs):
        for cp in _push_copies(refs[:n], refs[n:2 * n], refs[2 * n], refs[2 * n + 1], scatter, False):
            cp.start()
        token = refs[-1]
        token[...] = jnp.zeros_like(token)

    hbm = pl.BlockSpec(memory_space=pltpu.HBM)
    sem = pl.BlockSpec(memory_space=pltpu.SEMAPHORE)
    arrs = list(srcs) + list(lands)
    return pl.pallas_call(
        body, name=name,
        out_shape=(pltpu.SemaphoreType.DMA((nsem,)), pltpu.SemaphoreType.DMA((nsem,)),
                   *[pltpu.HBM(a.shape, a.dtype) for a in arrs], jax.ShapeDtypeStruct((8, 128), F32)),
        in_specs=[hbm] * (2 * n), out_specs=(sem, sem, *([hbm] * (2 * n)), pl.BlockSpec(memory_space=pltpu.VMEM)),
        input_output_aliases={i: 2 + i for i in range(2 * n)},
        compiler_params=pltpu.CompilerParams(has_side_effects=pltpu.SideEffectType.DATAFLOW_SIDE_EFFECTING),
    )(*[pltpu.with_memory_space_constraint(a, pltpu.HBM) for a in arrs])


def _push_wait(handle, after, scatter, name):
    send_sems, recv_sems, *thru, _ = handle
    n = len(thru) // 2

    def body(*refs):
        for cp in _push_copies(refs[:n], refs[n:2 * n], refs[2 * n], refs[2 * n + 1], scatter, True):
            cp.wait_send()
            cp.wait_recv()

    hbm = pl.BlockSpec(memory_space=pltpu.HBM)
    sem = pl.BlockSpec(memory_space=pltpu.SEMAPHORE)
    outs = pl.pallas_call(
        body, name=name, out_shape=tuple(pltpu.HBM(a.shape, a.dtype) for a in thru),
        in_specs=[hbm] * (2 * n) + [sem, sem, pl.BlockSpec(memory_space=pl.ANY)], out_specs=tuple([hbm] * (2 * n)),
        input_output_aliases={i: i for i in range(2 * n)},
        compiler_params=pltpu.CompilerParams(has_side_effects=pltpu.SideEffectType.DATAFLOW_SIDE_EFFECTING),
    )(*thru, send_sems, recv_sems, after)
    return list(outs[n:])


def _landing(own, me):
    return lax.dynamic_update_index_in_dim(lax.empty((N_DEV,) + own.shape, own.dtype), own, me, 0)


def _tables(S):
    pos = jnp.arange(S, dtype=F32)
    inv = ROPE_THETA ** (-jnp.arange(0, ATT_HEAD_DIM, 2, dtype=F32) / ATT_HEAD_DIM)
    ang = pos[:, None] * inv[None, :]
    cos, sin = jnp.cos(ang), jnp.sin(ang)
    cos128 = jnp.tile(cos, (1, 4))
    sin128 = jnp.tile(jnp.concatenate([-sin, sin], axis=1), (1, 2))
    lane = np.arange(ATT_WIDTH)
    bd = jnp.asarray((lane[:, None] // 64 == lane[None, :] // 64).astype(np.float32), dtype=BF16)
    return cos128, sin128, bd, jnp.asarray(_att_bias())


def _layer_fwd(l, x, p, tabs, late=None):
    cos, sin, bd, bias = tabs
    S = x.shape[0]
    row = lambda v: v.reshape(1, -1)
    hn = _rmsnorm_fwd(x, row(p["ln_mix"]), f"norm_mix_fwd{l}")
    proj = _mm(hn, p["w_in"], "nn", f"in_proj{l}", tn=1152)
    xbc = _conv_fwd(proj, p["conv_w"], row(p["conv_b"]), f"conv_fwd{l}")
    dtr = proj[:, DT_COL:DT_COL + SSD_HEADS].reshape(S, SSD_GROUPS, HPG).transpose(1, 0, 2)
    grp = lambda v: v.reshape(SSD_GROUPS, 1, HPG)
    d_rep = row(jnp.repeat(p["d_skip"], SSD_HEAD_DIM))
    y, yraw, ssd_st = _ssd_fwd(xbc, proj, dtr, grp(p["dt_bias"]), grp(p["a_log"]), d_rep, row(p["ssd_norm"]), f"ssd_fwd{l}")
    qg = row(jnp.tile(p["q_norm"], ATT_HEADS))
    kg = row(jnp.tile(p["k_norm"], ATT_HEADS))
    aq, akp, avp = _att_prep_fwd(proj, qg, kg, cos, sin, bd, f"att_prep_fwd{l}")
    y = _att_fwd(aq, akp, avp, bias, y, f"att_fwd{l}")
    y, oraw, ret_st = _ret_fwd(proj, cos, sin, row(p["ret_norm"]), y, f"ret_fwd{l}")
    if late is not None:
        p.update(late(y))
    x1 = _mm(y, p["w_out"], "nn", f"out_proj{l}", residual=x)
    hn2 = _rmsnorm_fwd(x1, row(p["ln_ffn"]), f"norm_ffn_fwd{l}")
    g, u, act = _swiglu_fwd(hn2, p["w_gate"], p["w_up"], f"swiglu_fwd{l}")
    x2 = _mm(act, p["w_down"], "nn", f"down_proj{l}", residual=x1, tk=2816)
    saved = dict(x=x, hn=hn, proj=proj, xbc=xbc, dtr=dtr, yraw=yraw, ssd_st=ssd_st, aq=aq, akp=akp, avp=avp,
                 oraw=oraw, ret_st=ret_st, y=y, x1=x1, hn2=hn2, g=g, u=u, act=act, d_rep=d_rep, qg=qg, kg=kg)
    return x2, saved


def _layer_bwd(l, dx2, dx2_bf, p, sv, tabs, on_ffn=None, on_all=None):
    cos, sin, bd, bias = tabs
    S = dx2.shape[0]
    row = lambda v: v.reshape(1, -1)
    grp = lambda v: v.reshape(SSD_GROUPS, 1, HPG)
    gr = {}
    dg, du = _swiglu_bwd(dx2_bf, p["w_down"], sv["g"], sv["u"], f"swiglu_bwd{l}")
    gr["w_down"] = _mm(sv["act"], dx2_bf, "tn", f"down_wgrad{l}", out_dtype=BF16, tm=1408, tn=1024, tk=2048)
    dhn2 = _mm_nt2(dg, p["w_gate"], du, p["w_up"], f"ffn_dgrad{l}")
    gr["w_gate"] = _mm(sv["hn2"], dg, "tn", f"gate_wgrad{l}", out_dtype=BF16, tm=1024, tn=1408, tk=2048)
    gr["w_up"] = _mm(sv["hn2"], du, "tn", f"up_wgrad{l}", out_dtype=BF16, tm=1024, tn=1408, tk=2048)
    ffn_gain = row(p["ln_ffn"]) + (on_ffn(gr)[0, 0] if on_ffn is not None else 0.0)
    dx1, dx1_bf, dln_ffn = _rmsnorm_bwd(sv["x1"], dhn2, ffn_gain, dx2, f"norm_ffn_bwd{l}")
    gr["ln_ffn"] = dln_ffn[0]
    dy = _mm(dx1_bf, p["w_out"], "nt", f"out_dgrad{l}")
    gr["w_out"] = _mm(sv["y"], dx1_bf, "tn", f"out_wgrad{l}", out_dtype=BF16, tm=1024, tn=1024, tk=2048)
    dproj, dxs, dbm, dcm, ddtr, dbias, dalog, dd, dssd_gain = _ssd_bwd(
        dy, sv["yraw"], sv["xbc"], sv["proj"], sv["dtr"], grp(p["dt_bias"]), grp(p["a_log"]), sv["d_rep"],
        row(p["ssd_norm"]), sv["ssd_st"], f"ssd_bwd{l}")
    gr["dt_bias"], gr["a_log"], gr["d_skip"] = dbias.reshape(-1), dalog.reshape(-1), dd.reshape(-1)
    gr["ssd_norm"] = dssd_gain[0]
    dproj, dconv_w, dconv_b = _conv_bwd(dxs, dbm, dcm, sv["proj"], p["conv_w"], row(p["conv_b"]), dproj, f"conv_bwd{l}")
    gr["conv_w"], gr["conv_b"] = dconv_w, dconv_b[0]
    dq, dk_p, dv_p = _att_bwd(sv["aq"], sv["akp"], sv["avp"], bias, dy, f"att_bwd{l}")
    dproj, dqg, dkg = _att_prep_bwd(sv["proj"], dq, dk_p, dv_p, sv["qg"], sv["kg"], cos, sin, bd, dproj, f"att_prep_bwd{l}")
    gr["q_norm"], gr["k_norm"] = dqg[0], dkg[0]
    dproj, dret_gain = _ret_bwd(dy, sv["oraw"], sv["proj"], cos, sin, row(p["ret_norm"]), sv["ret_st"], dproj, f"ret_bwd{l}")
    gr["ret_norm"] = dret_gain[0]
    ddt_cols = ddtr.transpose(1, 0, 2).reshape(S, SSD_HEADS).astype(BF16)
    dproj = lax.dynamic_update_slice(dproj, jnp.pad(ddt_cols, ((0, 0), (0, IN_PAD - DT_COL - SSD_HEADS))), (0, DT_COL))
    gr["w_in"] = _mm(sv["hn"], dproj, "tn", f"in_wgrad{l}", out_dtype=BF16, tm=1024, tn=1152, tk=2048)
    launched = on_all(gr) if on_all is not None else None
    dhn = _mm(dproj, p["w_in"], "nt", f"in_dgrad{l}", tk=1920, after=launched)
    dx0, dx0_bf, dln_mix = _rmsnorm_bwd(sv["x"], dhn, row(p["ln_mix"]), dx1, f"norm_mix_bwd{l}")
    gr["ln_mix"] = dln_mix[0]
    return dx0, dx0_bf, gr


def _local_step(x, tgt, layers, late=None, on_ffn=None, on_all=None):
    n = len(layers)
    none = [None] * n
    late, on_ffn, on_all = late or none, on_ffn or none, on_all or none
    tabs = _tables(x.shape[0])
    saved, params = [], []
    h = x
    for l in range(n):
        p = dict(layers[l](h) if callable(layers[l]) else layers[l])
        h, sv = _layer_fwd(l, h, p, tabs, late[l])
        saved.append(sv)
        params.append(p)
    dh, dh_bf, lacc = _loss_grad(h, tgt, "loss_grad")
    grads = [None] * n
    for l in reversed(range(n)):
        dh, dh_bf, grads[l] = _layer_bwd(l, dh, dh_bf, params[l], saved[l], tabs, on_ffn[l], on_all[l])
    return lacc[0, 0], dh, grads


BIG = ("w_in", "w_out", "w_gate", "w_up", "w_down")
SMALL = ("ln_mix", "conv_b", "dt_bias", "a_log", "d_skip", "ssd_norm", "q_norm", "k_norm", "ret_norm", "ln_ffn")
ORDER = ("ln_mix", "w_in", "conv_w", "conv_b", "dt_bias", "a_log", "d_skip", "ssd_norm", "q_norm", "k_norm", "ret_norm",
         "w_out", "ln_ffn", "w_gate", "w_up", "w_down")


COL_SHARDED = ("w_in", "w_gate", "w_up", "conv_w")


IN_GROUPS = ((ORIG_Z_XBC, Z_COL), (ORIG_DT, DT_COL), (ORIG_ATT, ATT_COL), (ORIG_RET, RET_COL))


def _full_weight(k, gathered):
    if k == "w_in":
        cs = gathered.shape[2]
        pieces = []
        for (lo, hi), _ in sorted(IN_GROUPS, key=lambda grp: grp[1]):
            for j in range(N_DEV):
                a, b = max(lo, j * cs), min(hi, (j + 1) * cs)
                if a < b:
                    pieces.append(gathered[j][:, a - j * cs:b - j * cs])
        pieces.append(jnp.zeros((gathered.shape[1], IN_PAD - IN_WIDTH), gathered.dtype))
        return jnp.concatenate(pieces, axis=1)
    if k in COL_SHARDED:
        return gathered.transpose(1, 0, 2).reshape(gathered.shape[1], -1)
    return gathered.reshape(-1, gathered.shape[2])


def _shard_block(k, g):
    if k == "w_in":
        cs = IN_WIDTH // N_DEV
        blocks = []
        for j in range(N_DEV):
            pieces = []
            for (lo, hi), col in IN_GROUPS:
                a, b = max(lo, j * cs), min(hi, (j + 1) * cs)
                if a < b:
                    pieces.append(g[:, col + a - lo:col + b - lo])
            blocks.append(jnp.concatenate(pieces, axis=1))
        return jnp.stack(blocks)
    if k in COL_SHARDED:
        return g.reshape(g.shape[0], N_DEV, -1).transpose(1, 0, 2)
    return g.reshape(N_DEV, -1, g.shape[1])


def kernel(x, ln_mix, w_in, conv_w, conv_b, dt_bias, a_log, d_skip, ssd_norm, q_norm, k_norm, ret_norm, w_out, ln_ffn, w_gate, w_up, w_down, loss_target, m_ln_mix, m_w_in, m_conv_w, m_conv_b, m_dt_bias, m_a_log, m_d_skip, m_ssd_norm, m_q_norm, m_k_norm, m_ret_norm, m_w_out, m_ln_ffn, m_w_gate, m_w_up, m_w_down, v_ln_mix, v_w_in, v_conv_w, v_conv_b, v_dt_bias, v_a_log, v_d_skip, v_ssd_norm, v_q_norm, v_k_norm, v_ret_norm, v_w_out, v_ln_ffn, v_w_gate, v_w_up, v_w_down):
    w = dict(ln_mix=ln_mix, w_in=w_in, conv_w=conv_w, conv_b=conv_b, dt_bias=dt_bias, a_log=a_log, d_skip=d_skip,
             ssd_norm=ssd_norm, q_norm=q_norm, k_norm=k_norm, ret_norm=ret_norm, w_out=w_out, ln_ffn=ln_ffn,
             w_gate=w_gate, w_up=w_up, w_down=w_down)
    m = dict(ln_mix=m_ln_mix, w_in=m_w_in, conv_w=m_conv_w, conv_b=m_conv_b, dt_bias=m_dt_bias, a_log=m_a_log,
             d_skip=m_d_skip, ssd_norm=m_ssd_norm, q_norm=m_q_norm, k_norm=m_k_norm, ret_norm=m_ret_norm, w_out=m_w_out,
             ln_ffn=m_ln_ffn, w_gate=m_w_gate, w_up=m_w_up, w_down=m_w_down)
    v = dict(ln_mix=v_ln_mix, w_in=v_w_in, conv_w=v_conv_w, conv_b=v_conv_b, dt_bias=v_dt_bias, a_log=v_a_log,
             d_skip=v_d_skip, ssd_norm=v_ssd_norm, q_norm=v_q_norm, k_norm=v_k_norm, ret_norm=v_ret_norm, w_out=v_w_out,
             ln_ffn=v_ln_ffn, w_gate=v_w_gate, w_up=v_w_up, w_down=v_w_down)
    me = 4 * lax.axis_index("x") + 2 * lax.axis_index("y") + lax.axis_index("c")

    waves = {"a": [("w_in", 0), ("conv_w", 0), ("conv_w", 1)],
             "b": [(k, 0) for k in ("w_out", "w_gate", "w_up", "w_down")],
             "c": [(k, 1) for k in BIG]}
    gather = {}
    behind = 0.0
    for tag, items in waves.items():
        srcs = [w[k][l] if k == "conv_w" else (w[k][l] + behind).astype(BF16) for k, l in items]
        gather[tag] = _push_start(srcs, [_landing(s, me) for s in srcs], False, f"gather_{tag}_start")
        behind = gather[tag][-1][0, 0]
    started = behind
    full = {}

    def arrive(tag, after):
        for (k, l), g in zip(waves[tag], _push_wait(gather[tag], after, False, f"gather_{tag}_wait")):
            full[k, l] = _full_weight(k, g)

    def layer_weights(l, names):
        return {k: full[k, l] for k in names}

    def small_weights(l):
        return {k: w[k][l] for k in SMALL}

    def layer0(h):
        arrive("a", gather["c"][-1])
        p = small_weights(0)
        p["ln_mix"] = p["ln_mix"] + started
        return {**p, **layer_weights(0, ("w_in", "conv_w"))}

    def late0(y):
        arrive("b", y)
        return layer_weights(0, waves_b_names)

    def layer1(h):
        arrive("c", h)
        return {**small_weights(1), **layer_weights(1, BIG + ("conv_w",))}

    waves_b_names = tuple(k for k, _ in waves["b"])

    groups = {"1": [(k, 1) for k in BIG], "0a": [(k, 0) for k in ("w_down", "w_gate", "w_up")],
              "0b": [(k, 0) for k in ("w_out", "w_in")]}
    scatter = {}

    def push_grads(tag, gr):
        blocks = [_shard_block(k, gr[k]) for k, _ in groups[tag]]
        lands = [_landing(lax.dynamic_index_in_dim(b, me, 0, keepdims=False), me) for b in blocks]
        scatter[tag] = _push_start(blocks, lands, True, f"scatter_{tag}_start")
        return scatter[tag][-1]

    loss_part, gx, grads = _local_step(
        x[0], loss_target[0], [layer0, layer1], late=[late0, None],
        on_ffn=[functools.partial(push_grads, "0a"), None],
        on_all=[functools.partial(push_grads, "0b"), functools.partial(push_grads, "1")])
    loss = lax.psum(loss_part, MESH_AXES)

    out = {}
    recv = {}
    for tag, items in groups.items():
        for item, r in zip(items, _push_wait(scatter[tag], gx, True, f"scatter_{tag}_wait")):
            recv[item] = r
    for k in BIG:
        out[k] = _adamw_layers(recv[k, 0], recv[k, 1], w[k], m[k], v[k], f"adamw_{k}")
    names = SMALL + ("conv_w",)
    sizes = [int(np.prod(grads[0][k].shape)) for k in names]
    packed = jnp.concatenate([jnp.stack([grads[l][k] for l in range(DEPTH)]).reshape(-1) for k in names])
    n_small = packed.shape[0]
    rows_small = -(-n_small // 1024) * 8
    pad = lambda t, fill: jnp.concatenate([t, jnp.full((rows_small * 128 - n_small,), fill, F32)]).reshape(rows_small, 128)
    parts = _exchange([pad(packed, 0.0)], False, "gather_small_grads")[0]
    n_rep = DEPTH * sum(sizes[:-1])
    pack_rep = lambda d, fill: pad(jnp.concatenate([d[k].reshape(-1) for k in SMALL]
                                                   + [jnp.full((n_small - n_rep,), fill, F32)]), fill)
    res = _adamw(parts, pack_rep(w, 1.0), pack_rep(m, 1.0), pack_rep(v, 1.0), "adamw_small")
    res = [t.reshape(-1) for t in res]
    off = 0
    for k, sz in zip(SMALL, sizes[:-1]):
        out[k] = [t[off:off + DEPTH * sz].reshape(w[k].shape) for t in res]
        off += DEPTH * sz
    gconv = res[0][off:off + DEPTH * sizes[-1]].reshape(DEPTH, SSD_CONV, SSD_CONV_CH)
    gconv = lax.dynamic_slice_in_dim(gconv, me * conv_w.shape[2], conv_w.shape[2], axis=2)
    flat = lambda t: t.reshape(8, -1)
    resc = _adamw(flat(gconv)[None], flat(conv_w), flat(m_conv_w), flat(v_conv_w), "adamw_conv_w")
    out["conv_w"] = [t.reshape(conv_w.shape) for t in resc]

    return (loss, gx[None], *[out[k][0] for k in ORDER], *[out[k][1] for k in ORDER],
            *[out[k][2] for k in ORDER], *[out[k][3] for k in ORDER])
```

```python
import functools
import math

import jax
import jax.numpy as jnp
import numpy as np
from jax import lax
from jax.experimental import pallas as pl
from jax.experimental.pallas import tpu as pltpu

F32 = jnp.float32
BF16 = jnp.bfloat16

N_DEV = 8
MESH_AXES = ("x", "y", "c")
D_MODEL = 2048
DEPTH = 2
EPS = 1e-6
ROPE_THETA = 10000.0
SSD_HEADS = 16
SSD_HEAD_DIM = 64
SSD_WIDTH = 1024
SSD_GROUPS = 2
SSD_STATE = 128
SSD_CONV = 4
SSD_CONV_CH = 1536
ATT_HEADS = 8
ATT_HEAD_DIM = 64
ATT_WIDTH = 512
DILATED_PAIRS = ((128, 1), (512, 4), (2048, 16))
RET_HEADS = 4
RET_QK_DIM = 64
RET_V_DIM = 128
RET_QK_WIDTH = 256
RET_V_WIDTH = 512
CHUNK = 128
MIX_WIDTH = 2048
ATT_SPAN = 2048
ATT_STRIP = ATT_SPAN + CHUNK
ATT_QB = 4
IN_WIDTH = 5648
IN_PAD = 5760
RET_COL, ATT_COL, Z_COL, XBC_COL, DT_COL = 0, 1536, 3072, 4096, 5632
ORIG_Z_XBC, ORIG_DT, ORIG_ATT, ORIG_RET = (0, 2560), (2560, 2576), (2576, 4112), (4112, 5648)
D_FF = 5632
ADAM_LR = 0.001
ADAM_B1 = 0.9
ADAM_B2 = 0.999
ADAM_EPS = 1e-08
ADAM_WD = 0.01
ADAM_STEP = 10
NEG = -1e30
VMEM_LIMIT_V7X = 60 * 1024 * 1024

NN = (((1,), (0,)), ((), ()))
NT = (((1,), (1,)), ((), ()))
TN = (((0,), (0,)), ((), ()))


def _bdot(a, b, dims):
    return lax.dot_general(a.astype(BF16), b.astype(BF16), dims, preferred_element_type=F32)


def _xdot(a, b, dims):
    return lax.dot_general(a, b, dims, precision=lax.Precision.HIGHEST, preferred_element_type=F32)


def _params(*sem):
    return pltpu.CompilerParams(dimension_semantics=sem, vmem_limit_bytes=VMEM_LIMIT_V7X)


def _sigmoid(v):
    return 1.0 / (1.0 + jnp.exp(-v))


def _silu_grad(v, s):
    return s * (1.0 + v * (1.0 - s))


def _rmsnorm_fwd(x, g, name):
    S, D = x.shape
    tr = min(512, S)

    def body(x_ref, g_ref, o_ref):
        xv = x_ref[...]
        r = lax.rsqrt(jnp.mean(xv * xv, axis=-1, keepdims=True) + EPS)
        o_ref[...] = (xv * r * g_ref[...]).astype(o_ref.dtype)

    return pl.pallas_call(
        body, grid=(S // tr,),
        in_specs=[pl.BlockSpec((tr, D), lambda i: (i, 0)), pl.BlockSpec((1, D), lambda i: (0, 0))],
        out_specs=pl.BlockSpec((tr, D), lambda i: (i, 0)),
        out_shape=jax.ShapeDtypeStruct((S, D), BF16), name=name, compiler_params=_params("parallel"),
    )(x, g)


def _rmsnorm_bwd(x, dy, g, dres, name):
    S, D = x.shape
    tr = min(512, S)

    def body(x_ref, dy_ref, g_ref, dres_ref, dx_ref, dxb_ref, dg_ref):
        i = pl.program_id(0)
        xv = x_ref[...]
        r = lax.rsqrt(jnp.mean(xv * xv, axis=-1, keepdims=True) + EPS)
        n = xv * r
        dy = dy_ref[...]
        dn = dy * g_ref[...]
        dx = dres_ref[...] + r * (dn - n * jnp.mean(dn * n, axis=-1, keepdims=True))
        dx_ref[...] = dx
        dxb_ref[...] = dx.astype(BF16)
        part = jnp.sum(dy * n, axis=0, keepdims=True)

        @pl.when(i == 0)
        def _():
            dg_ref[...] = part

        @pl.when(i > 0)
        def _():
            dg_ref[...] += part

    row = pl.BlockSpec((tr, D), lambda i: (i, 0))
    vec = pl.BlockSpec((1, D), lambda i: (0, 0))
    return pl.pallas_call(
        body, grid=(S // tr,), in_specs=[row, row, vec, row], out_specs=[row, row, vec],
        out_shape=[jax.ShapeDtypeStruct((S, D), F32), jax.ShapeDtypeStruct((S, D), BF16), jax.ShapeDtypeStruct((1, D), F32)],
        name=name, compiler_params=_params("arbitrary"),
    )(x, dy, g, dres)


def _loss_grad(y, tgt, name):
    S, D = y.shape
    tr = min(512, S)

    def body(y_ref, t_ref, dy_ref, dyb_ref, l_ref):
        i = pl.program_id(0)
        err = y_ref[...] - t_ref[...]
        dy = err * (1.0 / D)
        dy_ref[...] = dy
        dyb_ref[...] = dy.astype(BF16)
        part = jnp.sum(jnp.sum(err * err, axis=1, keepdims=True), axis=0, keepdims=True) * (0.5 / D)

        @pl.when(i == 0)
        def _():
            l_ref[...] = jnp.zeros_like(l_ref)

        l_ref[...] += part

    row = pl.BlockSpec((tr, D), lambda i: (i, 0))
    return pl.pallas_call(
        body, grid=(S // tr,), in_specs=[row, row],
        out_specs=[row, row, pl.BlockSpec((8, 128), lambda i: (0, 0))],
        out_shape=[jax.ShapeDtypeStruct((S, D), F32), jax.ShapeDtypeStruct((S, D), BF16), jax.ShapeDtypeStruct((8, 128), F32)],
        name=name, compiler_params=_params("arbitrary"),
    )(y, tgt)


def _pick(n, cands):
    for c in cands:
        if n % c == 0:
            return c
    return n


def _mm(a, b, mode, name, out_dtype=F32, residual=None, tm=None, tn=None, tk=None, after=None):
    if mode == "nn":
        (M, K), (_, N) = a.shape, b.shape
    elif mode == "nt":
        (M, K), (N, _) = a.shape, b.shape
    else:
        (K, M), (_, N) = a.shape, b.shape
    tm = min(tm, M) if tm else _pick(M, (1024, 512, 256, 128))
    tn = min(tn, N) if tn else _pick(N, (1024, 1152, 1408, 512, 256, 128))
    tk = min(tk, K) if tk else _pick(K, (2048, 1920, 1408, 1024, 512, 256, 128))
    assert M % tm == 0 and N % tn == 0 and K % tk == 0, (name, M, N, K, tm, tn, tk)
    nk = K // tk
    a_spec = pl.BlockSpec((tk, tm), lambda i, j, k: (k, i)) if mode == "tn" else pl.BlockSpec((tm, tk), lambda i, j, k: (i, k))
    b_spec = pl.BlockSpec((tn, tk), lambda i, j, k: (j, k)) if mode == "nt" else pl.BlockSpec((tk, tn), lambda i, j, k: (k, j))
    o_spec = pl.BlockSpec((tm, tn), lambda i, j, k: (i, j))
    dims = {"nn": NN, "nt": NT, "tn": TN}[mode]
    has_res = residual is not None

    has_after = after is not None

    def body(*refs):
        a_ref, b_ref = refs[0], refs[1]
        r_ref = refs[2] if has_res else None
        o_ref = refs[2 + has_res + has_after]
        p = _bdot(a_ref[...], b_ref[...], dims)

        def finish(acc):
            if has_res:
                acc = acc + r_ref[...]
            o_ref[...] = acc.astype(o_ref.dtype)

        if nk == 1:
            finish(p)
        else:
            acc_ref = refs[-1]
            k = pl.program_id(2)

            @pl.when(k == 0)
            def _():
                acc_ref[...] = p

            @pl.when(k > 0)
            def _():
                acc_ref[...] += p

            @pl.when(k == nk - 1)
            def _():
                finish(acc_ref[...])

    ins = [a, b] + ([residual] if has_res else []) + ([after] if has_after else [])
    in_specs = [a_spec, b_spec] + ([o_spec] if has_res else []) + ([pl.BlockSpec(memory_space=pl.ANY)] if has_after else [])
    scratch = [pltpu.VMEM((tm, tn), F32)] if nk > 1 else []
    return pl.pallas_call(
        body, grid=(M // tm, N // tn, nk), in_specs=in_specs, out_specs=o_spec,
        out_shape=jax.ShapeDtypeStruct((M, N), out_dtype), scratch_shapes=scratch, name=name,
        compiler_params=_params("parallel", "parallel", "arbitrary"),
    )(*ins)


def _accumulate(acc_ref, p, k, nk, finish):
    @pl.when(k == 0)
    def _():
        acc_ref[...] = p

    @pl.when(k > 0)
    def _():
        acc_ref[...] += p

    @pl.when(k == nk - 1)
    def _():
        finish(acc_ref[...])


def _swiglu_fwd(hn, wg, wu, name):
    S, K = hn.shape
    F = wg.shape[1]
    tm = _pick(S, (1024, 512))
    tn = _pick(F, (512, 256, 128))

    def body(a_ref, wg_ref, wu_ref, g_ref, u_ref, act_ref):
        a = a_ref[...]
        g = _bdot(a, wg_ref[...], NN)
        u = _bdot(a, wu_ref[...], NN)
        g_ref[...] = g.astype(BF16)
        u_ref[...] = u.astype(BF16)
        act_ref[...] = (g * _sigmoid(g) * u).astype(BF16)

    w_spec = pl.BlockSpec((K, tn), lambda i, j: (0, j))
    o_spec = pl.BlockSpec((tm, tn), lambda i, j: (i, j))
    sh = jax.ShapeDtypeStruct((S, F), BF16)
    return pl.pallas_call(
        body, grid=(S // tm, F // tn), in_specs=[pl.BlockSpec((tm, K), lambda i, j: (i, 0)), w_spec, w_spec],
        out_specs=[o_spec, o_spec, o_spec], out_shape=[sh, sh, sh], name=name,
        compiler_params=_params("parallel", "parallel"),
    )(hn, wg, wu)


def _swiglu_bwd(dx, wd, g, u, name):
    S, K = dx.shape
    F = wd.shape[0]
    tm = _pick(S, (1024, 512))
    tn = _pick(F, (512, 256, 128))

    def body(dx_ref, wd_ref, g_ref, u_ref, dg_ref, du_ref):
        da = _bdot(dx_ref[...], wd_ref[...], NT)
        gv = g_ref[...].astype(F32)
        uv = u_ref[...].astype(F32)
        s = _sigmoid(gv)
        dg_ref[...] = (da * uv * _silu_grad(gv, s)).astype(BF16)
        du_ref[...] = (da * gv * s).astype(BF16)

    o_spec = pl.BlockSpec((tm, tn), lambda i, j: (i, j))
    sh = jax.ShapeDtypeStruct((S, F), BF16)
    return pl.pallas_call(
        body, grid=(S // tm, F // tn),
        in_specs=[pl.BlockSpec((tm, K), lambda i, j: (i, 0)), pl.BlockSpec((tn, K), lambda i, j: (j, 0)), o_spec, o_spec],
        out_specs=[o_spec, o_spec], out_shape=[sh, sh], name=name, compiler_params=_params("parallel", "parallel"),
    )(dx, wd, g, u)


def _mm_nt2(a1, b1, a2, b2, name):
    M, K = a1.shape
    N = b1.shape[0]
    tm = _pick(M, (1024, 512))
    tn = _pick(N, (1024, 512))
    tk = _pick(K, (1408, 1024, 512, 256, 128))
    nk = K // tk

    def body(a1_ref, b1_ref, a2_ref, b2_ref, o_ref, acc_ref):
        def finish(acc):
            o_ref[...] = acc

        p = _bdot(a1_ref[...], b1_ref[...], NT) + _bdot(a2_ref[...], b2_ref[...], NT)
        _accumulate(acc_ref, p, pl.program_id(2), nk, finish)

    a_spec = pl.BlockSpec((tm, tk), lambda i, j, k: (i, k))
    b_spec = pl.BlockSpec((tn, tk), lambda i, j, k: (j, k))
    return pl.pallas_call(
        body, grid=(M // tm, N // tn, nk), in_specs=[a_spec, b_spec, a_spec, b_spec],
        out_specs=pl.BlockSpec((tm, tn), lambda i, j, k: (i, j)), out_shape=jax.ShapeDtypeStruct((M, N), F32),
        scratch_shapes=[pltpu.VMEM((tm, tn), F32)], name=name,
        compiler_params=_params("parallel", "parallel", "arbitrary"),
    )(a1, b1, a2, b2)


XBC_BLK0 = XBC_COL // 128


def _conv_fwd(proj, w, b, name):
    S = proj.shape[0]
    T = min(512, S)

    def body(x_ref, w_ref, b_ref, o_ref, xp_ref):
        xp_ref[pl.ds(0, 8), :] = jnp.zeros((8, 128), F32)
        xp_ref[pl.ds(8, S), :] = x_ref[...]
        wv = w_ref[...]
        bv = b_ref[...]

        def step(c, carry):
            base = pl.multiple_of(c * T, T)
            acc = wv[0:1] * xp_ref[pl.ds(base + 5, T), :]
            for i in range(1, SSD_CONV):
                acc = acc + wv[i:i + 1] * xp_ref[pl.ds(base + 5 + i, T), :]
            acc = bv + acc
            o_ref[pl.ds(base, T), :] = acc * _sigmoid(acc)
            return carry

        lax.fori_loop(0, S // T, step, 0)

    return pl.pallas_call(
        body, grid=(SSD_CONV_CH // 128,),
        in_specs=[pl.BlockSpec((S, 128), lambda j: (0, XBC_BLK0 + j)), pl.BlockSpec((SSD_CONV, 128), lambda j: (0, j)),
                  pl.BlockSpec((1, 128), lambda j: (0, j))],
        out_specs=pl.BlockSpec((S, 128), lambda j: (0, j)),
        out_shape=jax.ShapeDtypeStruct((S, SSD_CONV_CH), F32),
        scratch_shapes=[pltpu.VMEM((S + 8, 128), F32)], name=name, compiler_params=_params("parallel"),
    )(proj, w, b)


def _conv_bwd(dxs, dbm, dcm, proj, w, b, dproj, name):
    S = proj.shape[0]
    T = min(512, S)
    NX, NB = SSD_WIDTH // 128, SSD_GROUPS * SSD_STATE // 128

    def body(dxs_ref, dbm_ref, dcm_ref, x_ref, w_ref, b_ref, dproj_ref, dx_ref, dw_ref, db_ref, xp_ref, dcp_ref):
        j = pl.program_id(0)

        @pl.when(j < NX)
        def _():
            dcp_ref[pl.ds(0, S), :] = dxs_ref[...]

        @pl.when((j >= NX) & (j < NX + NB))
        def _():
            dcp_ref[pl.ds(0, S), :] = dbm_ref[...]

        @pl.when(j >= NX + NB)
        def _():
            dcp_ref[pl.ds(0, S), :] = dcm_ref[...]

        da_ref = dcp_ref
        xp_ref[pl.ds(0, 8), :] = jnp.zeros((8, 128), F32)
        xp_ref[pl.ds(8, S), :] = x_ref[...]
        dcp_ref[pl.ds(S, 8), :] = jnp.zeros((8, 128), F32)
        wv = w_ref[...]
        bv = b_ref[...]

        def step1(c, carry):
            base = pl.multiple_of(c * T, T)
            xs = [xp_ref[pl.ds(base + 5 + i, T), :] for i in range(SSD_CONV)]
            acc = wv[0:1] * xs[0]
            for i in range(1, SSD_CONV):
                acc = acc + wv[i:i + 1] * xs[i]
            acc = bv + acc
            s = _sigmoid(acc)
            dc = da_ref[pl.ds(base, T), :] * _silu_grad(acc, s)
            dcp_ref[pl.ds(base, T), :] = dc
            new = tuple(carry[i] + jnp.sum(xs[i] * dc, axis=0, keepdims=True) for i in range(SSD_CONV))
            return new + (carry[SSD_CONV] + jnp.sum(dc, axis=0, keepdims=True),)

        z = jnp.zeros((1, 128), F32)
        res = lax.fori_loop(0, S // T, step1, (z,) * (SSD_CONV + 1))
        for i in range(SSD_CONV):
            dw_ref[pl.ds(i, 1), :] = res[i]
        db_ref[...] = res[SSD_CONV]

        def step2(c, carry):
            base = pl.multiple_of(c * T, T)
            acc = wv[0:1] * dcp_ref[pl.ds(base + 3, T), :]
            for i in range(1, SSD_CONV):
                acc = acc + wv[i:i + 1] * dcp_ref[pl.ds(base + 3 - i, T), :]
            dx_ref[pl.ds(base, T), :] = acc.astype(dx_ref.dtype)
            return carry

        lax.fori_loop(0, S // T, step2, 0)

    clamp = lambda j, lo, n: jnp.clip(j - lo, 0, n - 1)
    return pl.pallas_call(
        body, grid=(SSD_CONV_CH // 128,),
        in_specs=[pl.BlockSpec((S, 128), lambda j: (0, clamp(j, 0, NX))), pl.BlockSpec((S, 128), lambda j: (0, clamp(j, NX, NB))),
                  pl.BlockSpec((S, 128), lambda j: (0, clamp(j, NX + NB, NB))),
                  pl.BlockSpec((S, 128), lambda j: (0, XBC_BLK0 + j)), pl.BlockSpec((SSD_CONV, 128), lambda j: (0, j)),
                  pl.BlockSpec((1, 128), lambda j: (0, j)), pl.BlockSpec(memory_space=pl.ANY)],
        out_specs=[pl.BlockSpec((S, 128), lambda j: (0, XBC_BLK0 + j)), pl.BlockSpec((SSD_CONV, 128), lambda j: (0, j)),
                   pl.BlockSpec((1, 128), lambda j: (0, j))],
        out_shape=[jax.ShapeDtypeStruct(dproj.shape, dproj.dtype), jax.ShapeDtypeStruct((SSD_CONV, SSD_CONV_CH), F32),
                   jax.ShapeDtypeStruct((1, SSD_CONV_CH), F32)],
        input_output_aliases={6: 0},
        scratch_shapes=[pltpu.VMEM((S + 8, 128), F32), pltpu.VMEM((S + 8, 128), F32)], name=name,
        compiler_params=_params("arbitrary"),
    )(dxs, dbm, dcm, proj, w, b, dproj)


HPG = SSD_HEADS // SSD_GROUPS
GW = HPG * SSD_HEAD_DIM


def _ssd_chunk_terms(dtr, bias, alog, tril, triu):
    pre = dtr + bias
    dt = jnp.maximum(pre, 0.0) + jnp.log(1.0 + jnp.exp(-jnp.abs(pre)))
    a_neg = -jnp.exp(alog)
    a = dt * a_neg
    acum = _xdot(tril, a, NN)
    acum_t = _xdot(a, triu, TN)
    return pre, dt, a_neg, acum, acum_t


def _head_expanders():
    h64 = lax.broadcasted_iota(jnp.int32, (HPG, GW), 0) == lax.broadcasted_iota(jnp.int32, (HPG, GW), 1) // SSD_HEAD_DIM
    h128 = lax.broadcasted_iota(jnp.int32, (HPG, HPG * CHUNK), 0) == lax.broadcasted_iota(jnp.int32, (HPG, HPG * CHUNK), 1) // CHUNK
    return h64.astype(F32), h128.astype(F32)


def _ssd_fwd(xbc, proj, dtr, dt_bias, a_log, d_rep, gain, name):
    S = xbc.shape[0]
    L = CHUNK
    T = min(512, S)
    CPS = T // L
    NC = S // L

    def body(x_ref, b_ref, c_ref, z_ref, dtr_ref, bias_ref, alog_ref, d_ref, gain_ref, y_ref, yraw_ref, st_ref, state):
        i = pl.program_id(1)

        @pl.when(i == 0)
        def _():
            state[...] = jnp.zeros_like(state)

        row = lax.broadcasted_iota(jnp.int32, (L, L), 0)
        col = lax.broadcasted_iota(jnp.int32, (L, L), 1)
        causal = row >= col
        tril = causal.astype(F32)
        triu = (row <= col).astype(F32)
        low = col < SSD_HEAD_DIM
        e64, e128 = _head_expanders()
        for c in range(CPS):
            rows = pl.ds(c * L, L)
            xv = x_ref[rows, :]
            bm = b_ref[rows, :]
            cm = c_ref[rows, :]
            _, dt, _, acum, acum_t = _ssd_chunk_terms(dtr_ref[rows, :], bias_ref[...], alog_ref[...], tril, triu)
            ac = _xdot(acum, e64, NN)
            ac_sq = _xdot(acum, e128, NN)
            xd = xv * _xdot(dt, e64, NN)
            ac_last = ac[L - 1:L, :]
            sp = state[...]
            st_ref[c] = sp
            yoff = _bdot(cm, sp, NN) * jnp.exp(ac)
            state[...] = sp * jnp.exp(ac_last) + _bdot(bm, xd * jnp.exp(ac_last - ac), TN)
            gmat = _bdot(cm, bm, NT)
            for q in range(HPG // 2):
                pair = slice(q * 128, (q + 1) * 128)
                tile = xd[:, pair]
                y = yoff[:, pair]
                for j, keep in ((2 * q, low), (2 * q + 1, ~low)):
                    lam = jnp.exp(jnp.where(causal, ac_sq[:, j * L:(j + 1) * L] - acum_t[j:j + 1, :], NEG))
                    y = y + _bdot(gmat * lam, jnp.where(keep, tile, 0.0), NN)
                yraw_ref[rows, pair] = y
            zz = z_ref[rows, :]
            u = (yraw_ref[rows, :] + xv * d_ref[...]) * (zz * _sigmoid(zz))
            r = lax.rsqrt(jnp.mean(u * u, axis=-1, keepdims=True) + EPS)
            y_ref[rows, :] = (u * r * gain_ref[...]).astype(y_ref.dtype)

    vec8 = pl.BlockSpec((None, 1, HPG), lambda g, i: (g, 0, 0))
    return pl.pallas_call(
        body, grid=(SSD_GROUPS, S // T),
        in_specs=[pl.BlockSpec((T, GW), lambda g, i: (i, g)),
                  pl.BlockSpec((T, SSD_STATE), lambda g, i: (i, SSD_WIDTH // SSD_STATE + g)),
                  pl.BlockSpec((T, SSD_STATE), lambda g, i: (i, SSD_WIDTH // SSD_STATE + SSD_GROUPS + g)),
                  pl.BlockSpec((T, GW), lambda g, i: (i, Z_COL // GW + g)),
                  pl.BlockSpec((None, T, HPG), lambda g, i: (g, i, 0)),
                  vec8, vec8,
                  pl.BlockSpec((1, GW), lambda g, i: (0, g)), pl.BlockSpec((1, GW), lambda g, i: (0, g))],
        out_specs=[pl.BlockSpec((T, GW), lambda g, i: (i, g)), pl.BlockSpec((T, GW), lambda g, i: (i, g)),
                   pl.BlockSpec((CPS, None, SSD_STATE, GW), lambda g, i: (i, g, 0, 0))],
        out_shape=[jax.ShapeDtypeStruct((S, MIX_WIDTH), BF16), jax.ShapeDtypeStruct((S, SSD_WIDTH), F32),
                   jax.ShapeDtypeStruct((NC, SSD_GROUPS, SSD_STATE, GW), F32)],
        scratch_shapes=[pltpu.VMEM((SSD_STATE, GW), F32)], name=name,
        compiler_params=_params("arbitrary", "arbitrary"),
    )(xbc, xbc, xbc, proj, dtr, dt_bias, a_log, d_rep, gain)


def _ssd_bwd(dy, yraw, xbc, proj, dtr, dt_bias, a_log, d_rep, gain, states, name):
    S = xbc.shape[0]
    L = CHUNK
    T = min(512, S)
    CPS = T // L
    NI = S // T

    def body(dy_ref, yraw_ref, x_ref, b_ref, c_ref, z_ref, dtr_ref, bias_ref, alog_ref, d_ref, gain_ref, st_ref,
             dz_ref, dx_ref, db_ref, dc_ref, ddtr_ref, dbias_ref, dalog_ref, dd_ref, dgain_ref, dstate, dxd_ref):
        i = pl.program_id(1)

        @pl.when(i == 0)
        def _():
            dstate[...] = jnp.zeros_like(dstate)
            dbias_ref[...] = jnp.zeros_like(dbias_ref)
            dalog_ref[...] = jnp.zeros_like(dalog_ref)
            dd_ref[...] = jnp.zeros_like(dd_ref)
            dgain_ref[...] = jnp.zeros_like(dgain_ref)

        row = lax.broadcasted_iota(jnp.int32, (L, L), 0)
        col = lax.broadcasted_iota(jnp.int32, (L, L), 1)
        causal = row >= col
        tril = causal.astype(F32)
        triu = (row <= col).astype(F32)
        low = col < SSD_HEAD_DIM
        e64, e128 = _head_expanders()
        lane8 = lax.broadcasted_iota(jnp.int32, (1, HPG), 1)
        sub8 = lax.broadcasted_iota(jnp.int32, (HPG, 1), 0)
        eye8 = (lax.broadcasted_iota(jnp.int32, (HPG, HPG), 0) == lax.broadcasted_iota(jnp.int32, (HPG, HPG), 1)).astype(F32)
        last_row = (lax.broadcasted_iota(jnp.int32, (L, 1), 0) == L - 1).astype(F32)
        for c in reversed(range(CPS)):
            rows = pl.ds(c * L, L)
            xv = x_ref[rows, :]
            bm = b_ref[rows, :]
            cm = c_ref[rows, :]
            zz = z_ref[rows, :]
            dvec = d_ref[...]
            sz = _sigmoid(zz)
            silu_z = zz * sz
            v = yraw_ref[rows, :] + xv * dvec
            u = v * silu_z
            r = lax.rsqrt(jnp.mean(u * u, axis=-1, keepdims=True) + EPS)
            n = u * r
            do = dy_ref[rows, :]
            dgain_ref[...] += jnp.sum(do * n, axis=0, keepdims=True)
            dn = do * gain_ref[...]
            du = r * (dn - n * jnp.mean(dn * n, axis=-1, keepdims=True))
            dz_ref[rows, :] = (du * v * _silu_grad(zz, sz)).astype(dz_ref.dtype)
            dyv = du * silu_z
            dd_ref[...] += _xdot(jnp.sum(dyv * xv, axis=0, keepdims=True), e64, NT)
            pre, dt, a_neg, acum, acum_t = _ssd_chunk_terms(dtr_ref[rows, :], bias_ref[...], alog_ref[...], tril, triu)
            ac = _xdot(acum, e64, NN)
            ac_sq = _xdot(acum, e128, NN)
            dt_w = _xdot(dt, e64, NN)
            xd = xv * dt_w
            ac_last = ac[L - 1:L, :]
            ea = jnp.exp(ac)
            w = jnp.exp(ac_last - ac)
            ea_last = jnp.exp(ac_last)
            sp = st_ref[c]
            ds = dstate[...]
            dye = dyv * ea
            yoff = _bdot(cm, sp, NN) * ea
            bds = _bdot(bm, ds, NN)
            dcm = _bdot(dye, sp, NT)
            dbm = _bdot(xd * w, ds, NT)
            dstate[...] = ds * ea_last + _bdot(cm, dye, TN)
            w8 = jnp.exp(acum[L - 1:L, :] - acum)
            dw8 = _xdot(xd * bds, e64, NT)
            dac8 = _xdot(dyv * yoff, e64, NT) - dw8 * w8
            tail8 = jnp.sum(dw8 * w8, axis=0, keepdims=True) + jnp.exp(acum[L - 1:L, :]) * _xdot(
                jnp.sum(ds * sp, axis=0, keepdims=True), e64, NT)
            dac8 = dac8 + last_row * tail8
            gmat = _bdot(cm, bm, NT)
            dgmat = jnp.zeros((L, L), F32)
            colsum_t = jnp.zeros((HPG, L), F32)
            for q in range(HPG // 2):
                pair = slice(q * 128, (q + 1) * 128)
                xd_tile = xd[:, pair]
                dy_tile = dyv[:, pair]
                dxd_tile = bds[:, pair] * w[:, pair]
                for j, keep in ((2 * q, low), (2 * q + 1, ~low)):
                    lam = jnp.exp(jnp.where(causal, ac_sq[:, j * L:(j + 1) * L] - acum_t[j:j + 1, :], NEG))
                    mh = gmat * lam
                    dyj = jnp.where(keep, dy_tile, 0.0)
                    dxd_tile = dxd_tile + _bdot(mh, dyj, TN)
                    dm = _bdot(dyj, xd_tile, NT)
                    dgmat = dgmat + dm * lam
                    qm = dm * mh
                    dac8 = dac8 + jnp.sum(qm, axis=1, keepdims=True) * (lane8 == j).astype(F32)
                    colsum_t = colsum_t + (sub8 == j).astype(F32) * jnp.sum(qm, axis=0, keepdims=True)
                dxd_ref[:, pair] = dxd_tile
            dac8 = dac8 - _xdot(colsum_t, eye8, TN)
            dxd = dxd_ref[...]
            dx_ref[rows, :] = dxd * dt_w + dyv * dvec
            dc_ref[rows, :] = dcm + _bdot(dgmat, bm, NN)
            db_ref[rows, :] = dbm + _bdot(dgmat, cm, TN)
            da8 = _xdot(triu, dac8, NN)
            ddt8 = _xdot(dxd * xv, e64, NT) + da8 * a_neg
            dalog_ref[...] += jnp.sum(da8 * dt, axis=0, keepdims=True) * a_neg
            dpre = ddt8 * _sigmoid(pre)
            ddtr_ref[rows, :] = dpre
            dbias_ref[...] += jnp.sum(dpre, axis=0, keepdims=True)

    rev = lambda i: NI - 1 - i
    vec8 = pl.BlockSpec((None, 1, HPG), lambda g, i: (g, 0, 0))
    grp = pl.BlockSpec((T, GW), lambda g, i: (rev(i), g))
    bspec = pl.BlockSpec((T, SSD_STATE), lambda g, i: (rev(i), SSD_WIDTH // SSD_STATE + g))
    cspec = pl.BlockSpec((T, SSD_STATE), lambda g, i: (rev(i), SSD_WIDTH // SSD_STATE + SSD_GROUPS + g))
    gvec = pl.BlockSpec((1, GW), lambda g, i: (0, g))
    st_spec = pl.BlockSpec((CPS, None, SSD_STATE, GW), lambda g, i: (rev(i), g, 0, 0))
    small = jax.ShapeDtypeStruct((SSD_GROUPS, 1, HPG), F32)
    zspec = pl.BlockSpec((T, GW), lambda g, i: (rev(i), Z_COL // GW + g))
    return pl.pallas_call(
        body, grid=(SSD_GROUPS, NI),
        in_specs=[grp, grp, grp, bspec, cspec, zspec, pl.BlockSpec((None, T, HPG), lambda g, i: (g, rev(i), 0)),
                  vec8, vec8, gvec, gvec, st_spec],
        out_specs=[zspec, grp, pl.BlockSpec((T, SSD_STATE), lambda g, i: (rev(i), g)),
                   pl.BlockSpec((T, SSD_STATE), lambda g, i: (rev(i), g)),
                   pl.BlockSpec((None, T, HPG), lambda g, i: (g, rev(i), 0)), vec8, vec8, vec8, gvec],
        out_shape=[jax.ShapeDtypeStruct((S, IN_PAD), BF16), jax.ShapeDtypeStruct((S, SSD_WIDTH), F32),
                   jax.ShapeDtypeStruct((S, SSD_GROUPS * SSD_STATE), F32), jax.ShapeDtypeStruct((S, SSD_GROUPS * SSD_STATE), F32),
                   jax.ShapeDtypeStruct((SSD_GROUPS, S, HPG), F32), small, small, small,
                   jax.ShapeDtypeStruct((1, SSD_WIDTH), F32)],
        scratch_shapes=[pltpu.VMEM((SSD_STATE, GW), F32), pltpu.VMEM((L, GW), F32)],
        name=name, compiler_params=_params("arbitrary", "arbitrary"),
    )(dy, yraw, xbc, xbc, xbc, proj, dtr, dt_bias, a_log, d_rep, gain, states)


def _swap_halves(t):
    w = t.shape[1]
    lane = lax.broadcasted_iota(jnp.int32, t.shape, 1)
    return jnp.where((lane % 64) < 32, pltpu.roll(t, w - 32, axis=1), pltpu.roll(t, 32, axis=1))


def _widen(tab, w):
    return tab if w == 128 else jnp.concatenate([tab] * (w // 128), axis=1)


def _rope(t, cos, sin_signed):
    return t * cos + _swap_halves(t) * sin_signed


def _rope_t(d, cos, sin_signed):
    return d * cos - _swap_halves(d) * sin_signed


def _group_sum64(v, bd):
    hi = v.astype(BF16)
    lo = (v - hi.astype(F32)).astype(BF16)
    return (lax.dot_general(hi, bd, NN, preferred_element_type=F32)
            + lax.dot_general(lo, bd, NN, preferred_element_type=F32))


AQ_BLK = ATT_COL // ATT_WIDTH


def _att_prep_fwd(proj, qg, kg, cos, sin, bd, name):
    S = proj.shape[0]
    T = min(512, S)
    PB = ATT_SPAN // T
    src = lambda i: jnp.maximum(i - PB, 0)

    def body(q_ref, k_ref, v_ref, qg_ref, kg_ref, cos_ref, sin_ref, bd_ref, qo_ref, ko_ref, vo_ref):
        i = pl.program_id(0)

        @pl.when(i < PB)
        def _():
            ko_ref[...] = jnp.zeros_like(ko_ref)
            vo_ref[...] = jnp.zeros_like(vo_ref)

        @pl.when(i >= PB)
        def _():
            cw = _widen(cos_ref[...], ATT_WIDTH)
            sw = _widen(sin_ref[...], ATT_WIDTH)
            bdv = bd_ref[...]

            def norm_rope(t, gain):
                ss = _group_sum64(t * t, bdv)
                return _rope(t * lax.rsqrt(ss * (1.0 / ATT_HEAD_DIM) + EPS) * gain, cw, sw)

            qo_ref[...] = (norm_rope(q_ref[...], qg_ref[...]) * (ATT_HEAD_DIM ** -0.5)).astype(BF16)
            kt = norm_rope(k_ref[...], kg_ref[...]).astype(BF16)
            vt = v_ref[...].astype(BF16)
            for pr in range(ATT_HEADS // 2):
                ko_ref[pr] = kt[:, pr * 128:(pr + 1) * 128]
                vo_ref[pr] = vt[:, pr * 128:(pr + 1) * 128]

    vec = pl.BlockSpec((1, ATT_WIDTH), lambda i: (0, 0))
    tab = pl.BlockSpec((T, 128), lambda i: (src(i), 0))
    hm = pl.BlockSpec((ATT_HEADS // 2, T, 128), lambda i: (0, i, 0))
    hm_shape = jax.ShapeDtypeStruct((ATT_HEADS // 2, ATT_SPAN + S, 128), BF16)
    return pl.pallas_call(
        body, grid=(PB + S // T,),
        in_specs=[pl.BlockSpec((T, ATT_WIDTH), lambda i: (src(i), AQ_BLK)), pl.BlockSpec((T, ATT_WIDTH), lambda i: (src(i), AQ_BLK + 1)),
                  pl.BlockSpec((T, ATT_WIDTH), lambda i: (src(i), AQ_BLK + 2)), vec, vec, tab, tab,
                  pl.BlockSpec((ATT_WIDTH, ATT_WIDTH), lambda i: (0, 0))],
        out_specs=[pl.BlockSpec((T, ATT_WIDTH), lambda i: (src(i), 0)), hm, hm],
        out_shape=[jax.ShapeDtypeStruct((S, ATT_WIDTH), BF16), hm_shape, hm_shape],
        name=name, compiler_params=_params("arbitrary"),
    )(proj, proj, proj, qg, kg, cos, sin, bd)


def _att_prep_bwd(proj, dq, dk_p, dv_p, qg, kg, cos, sin, bd, dproj, name):
    S = proj.shape[0]
    T = min(512, S)
    NI = S // T
    PB = ATT_SPAN // T
    W = ATT_WIDTH

    def body(q_ref, k_ref, dq_ref, dkp_ref, dvp_ref, qg_ref, kg_ref, cos_ref, sin_ref, bd_ref, dproj_ref,
             do_ref, dqg_ref, dkg_ref, acc_ref):
        i = pl.program_id(0)

        @pl.when(i == 0)
        def _():
            acc_ref[...] = jnp.zeros_like(acc_ref)

        npair = ATT_HEADS // 2
        dk_all = jnp.concatenate([dkp_ref[pr].T for pr in range(npair)], axis=1)
        do_ref[:, 2 * W:3 * W] = jnp.concatenate([dvp_ref[pr].T for pr in range(npair)], axis=1).astype(BF16)
        cw = _widen(cos_ref[...], ATT_WIDTH)
        sw = _widen(sin_ref[...], ATT_WIDTH)
        bdv = bd_ref[...]

        def one(t, d_rot, gain, scale, slot):
            ss = _group_sum64(t * t, bdv)
            r = lax.rsqrt(ss * (1.0 / ATT_HEAD_DIM) + EPS)
            n = t * r
            d_ng = _rope_t(d_rot * scale, cw, sw)
            acc_ref[pl.ds(slot, 1), :] += jnp.sum(d_ng * n, axis=0, keepdims=True)
            dn = d_ng * gain
            return r * (dn - n * (_group_sum64(dn * n, bdv) * (1.0 / ATT_HEAD_DIM)))

        do_ref[:, 0:W] = one(q_ref[...], dq_ref[...], qg_ref[...], ATT_HEAD_DIM ** -0.5, 0).astype(BF16)
        do_ref[:, W:2 * W] = one(k_ref[...], dk_all, kg_ref[...], 1.0, 1).astype(BF16)

        @pl.when(i == NI - 1)
        def _():
            a = acc_ref[...]
            f = a[:, 0:64]
            for h in range(1, ATT_HEADS):
                f = f + a[:, h * 64:(h + 1) * 64]
            dqg_ref[...] = f[0:1]
            dkg_ref[...] = f[1:2]

    vec = pl.BlockSpec((1, ATT_WIDTH), lambda i: (0, 0))
    tab = pl.BlockSpec((T, 128), lambda i: (i, 0))
    row = pl.BlockSpec((T, ATT_WIDTH), lambda i: (i, 0))
    g64 = pl.BlockSpec((1, ATT_HEAD_DIM), lambda i: (0, 0))
    padded = pl.BlockSpec((ATT_HEADS // 2, 128, T), lambda i: (0, 0, i + PB))
    return pl.pallas_call(
        body, grid=(NI,),
        in_specs=[pl.BlockSpec((T, ATT_WIDTH), lambda i: (i, AQ_BLK)), pl.BlockSpec((T, ATT_WIDTH), lambda i: (i, AQ_BLK + 1)),
                  row, padded, padded, vec, vec, tab, tab, pl.BlockSpec((ATT_WIDTH, ATT_WIDTH), lambda i: (0, 0)),
                  pl.BlockSpec(memory_space=pl.ANY)],
        out_specs=[pl.BlockSpec((T, 3 * W), lambda i: (i, ATT_COL // (3 * W))), g64, g64],
        out_shape=[jax.ShapeDtypeStruct(dproj.shape, dproj.dtype), jax.ShapeDtypeStruct((1, ATT_HEAD_DIM), F32),
                   jax.ShapeDtypeStruct((1, ATT_HEAD_DIM), F32)],
        input_output_aliases={10: 0},
        scratch_shapes=[pltpu.VMEM((8, ATT_WIDTH), F32)], name=name, compiler_params=_params("arbitrary"),
    )(proj, proj, dq, dk_p, dv_p, qg, kg, cos, sin, bd, dproj)


def _att_bias():
    qpos = np.arange(CHUNK)[:, None] + ATT_SPAN
    kpos = np.arange(ATT_STRIP)[None, :]
    rel = qpos - kpos
    mult = np.zeros((CHUNK, ATT_STRIP), np.float64)
    for window, dil in DILATED_PAIRS:
        mult += (rel >= 0) & (rel % dil == 0) & (rel // dil <= window // dil)
    return np.where(mult > 0, np.log(np.maximum(mult, 1.0)), NEG).astype(np.float32)


def _att_scores(q, ks, bias, i):
    s = _bdot(q, ks, NT) + bias
    kcol = lax.broadcasted_iota(jnp.int32, (1, ATT_STRIP), 1) + i * CHUNK
    return jnp.where(kcol >= ATT_SPAN, s, NEG)


def _pair_masks():
    low = lax.broadcasted_iota(jnp.int32, (CHUNK, 128), 1) < ATT_HEAD_DIM
    return low, ~low


def _att_fwd(q, kp, vp, bias, y, name):
    S = q.shape[0]
    SP = kp.shape[1]

    TQ = ATT_QB * CHUNK

    def body(q_ref, k_ref, v_ref, bias_ref, y_ref, o_ref):
        i = pl.program_id(1)
        for b in range(ATT_QB):
            blk = i * ATT_QB + b
            strip = pl.ds(pl.multiple_of(blk * CHUNK, CHUNK), ATT_STRIP)
            rows = pl.ds(b * CHUNK, CHUNK)
            qv = q_ref[rows, :]
            ks = k_ref[strip, :]
            vs = v_ref[strip, :]
            outs = []
            for keep in _pair_masks():
                s = _att_scores(jnp.where(keep, qv, jnp.zeros_like(qv)), ks, bias_ref[...], blk)
                m = jnp.max(s, axis=-1, keepdims=True)
                p = jnp.exp(s - m)
                den = jnp.sum(p, axis=-1, keepdims=True)
                outs.append(_bdot(p, vs, NN) / den)
            o_ref[rows, :] = jnp.where(_pair_masks()[0], outs[0], outs[1]).astype(o_ref.dtype)

    kv = pl.BlockSpec((None, SP, 128), lambda hp, i: (hp, 0, 0))
    return pl.pallas_call(
        body, grid=(ATT_HEADS // 2, S // TQ),
        in_specs=[pl.BlockSpec((TQ, 128), lambda hp, i: (i, hp)), kv, kv,
                  pl.BlockSpec((CHUNK, ATT_STRIP), lambda hp, i: (0, 0)), pl.BlockSpec(memory_space=pl.ANY)],
        out_specs=pl.BlockSpec((TQ, 128), lambda hp, i: (i, SSD_WIDTH // 128 + hp)),
        out_shape=jax.ShapeDtypeStruct(y.shape, y.dtype), input_output_aliases={4: 0}, name=name,
        compiler_params=_params("parallel", "arbitrary"),
    )(q, kp, vp, bias, y)


def _att_bwd(q, kp, vp, bias, dy, name):
    S = q.shape[0]
    SP = kp.shape[1]

    def body(q_ref, k_ref, v_ref, bias_ref, do_ref, dq_ref, dk_ref, dv_ref):
        i = pl.program_id(1)

        @pl.when(i == 0)
        def _():
            dk_ref[...] = jnp.zeros_like(dk_ref)
            dv_ref[...] = jnp.zeros_like(dv_ref)

        for b in range(ATT_QB):
            blk = i * ATT_QB + b
            strip = pl.ds(pl.multiple_of(blk * CHUNK, CHUNK), ATT_STRIP)
            rows = pl.ds(b * CHUNK, CHUNK)
            qv = q_ref[rows, :]
            dov = do_ref[rows, :]
            ks = k_ref[strip, :]
            vs = v_ref[strip, :]
            dq = jnp.zeros((CHUNK, 128), F32)
            dk_t = jnp.zeros((128, ATT_STRIP), F32)
            dv_t = jnp.zeros((128, ATT_STRIP), F32)
            for keep in _pair_masks():
                qh = jnp.where(keep, qv, jnp.zeros_like(qv))
                doh = jnp.where(keep, dov, 0.0)
                s = _att_scores(qh, ks, bias_ref[...], blk)
                m = jnp.max(s, axis=-1, keepdims=True)
                p = jnp.exp(s - m)
                p = p / jnp.sum(p, axis=-1, keepdims=True)
                dp = _bdot(doh, vs, NT)
                dsc = p * (dp - jnp.sum(p * dp, axis=-1, keepdims=True))
                dq = dq + jnp.where(keep, _bdot(dsc, ks, NN), 0.0)
                dv_t = dv_t + _bdot(doh, p, TN)
                dk_t = dk_t + _bdot(qh, dsc, TN)
            dq_ref[rows, :] = dq
            dv_ref[:, strip] += dv_t
            dk_ref[:, strip] += dk_t

    TQ = ATT_QB * CHUNK
    kv = pl.BlockSpec((None, SP, 128), lambda hp, i: (hp, 0, 0))
    kv_t = pl.BlockSpec((None, 128, SP), lambda hp, i: (hp, 0, 0))
    pairs = jax.ShapeDtypeStruct((ATT_HEADS // 2, 128, SP), F32)
    return pl.pallas_call(
        body, grid=(ATT_HEADS // 2, S // TQ),
        in_specs=[pl.BlockSpec((TQ, 128), lambda hp, i: (i, hp)), kv, kv,
                  pl.BlockSpec((CHUNK, ATT_STRIP), lambda hp, i: (0, 0)),
                  pl.BlockSpec((TQ, 128), lambda hp, i: (i, SSD_WIDTH // 128 + hp))],
        out_specs=[pl.BlockSpec((TQ, 128), lambda hp, i: (i, hp)), kv_t, kv_t],
        out_shape=[jax.ShapeDtypeStruct((S, ATT_WIDTH), F32), pairs, pairs],
        name=name, compiler_params=_params("parallel", "arbitrary"),
    )(q, kp, vp, bias, dy)


RQ_BLK = RET_COL // RET_QK_WIDTH
RV_BLK = (RET_COL + 2 * RET_QK_WIDTH) // RET_V_WIDTH
RET_LOG_GAMMA = tuple(math.log1p(-2.0 ** (-5.0 - h)) for h in range(RET_HEADS))


def _ret_decays(h):
    L = CHUNK
    lg = RET_LOG_GAMMA[h]
    row = lax.broadcasted_iota(jnp.int32, (L, L), 0)
    col = lax.broadcasted_iota(jnp.int32, (L, L), 1)
    rel = (row - col).astype(F32)
    dm = jnp.where(rel >= 0, jnp.exp(jnp.maximum(rel, 0.0) * lg), 0.0)
    idx = lax.broadcasted_iota(jnp.int32, (L, 1), 0).astype(F32)
    kte = jnp.exp((L - 1 - idx) * lg)
    qfs = jnp.exp((idx + 1.0) * lg)
    return dm, kte, qfs, math.exp(L * lg)


def _ret_fwd(proj, cos, sin, gain, y, name):
    S = proj.shape[0]
    L = CHUNK
    T = min(512, S)
    CPS = T // L
    NC = S // L

    def body(q_ref, k_ref, v_ref, g_ref, cos_ref, sin_ref, gain_ref, yin_ref, y_ref, o_ref, st_ref, state):
        i = pl.program_id(0)

        @pl.when(i == 0)
        def _():
            state[...] = jnp.zeros_like(state)

        dec = [_ret_decays(h) for h in range(RET_HEADS)]
        for c in range(CPS):
            rows = pl.ds(c * L, L)
            cw = _widen(cos_ref[rows, :], RET_QK_WIDTH)
            sw = _widen(sin_ref[rows, :], RET_QK_WIDTH)
            qv = _rope(q_ref[rows, :], cw, sw)
            kv = _rope(k_ref[rows, :], cw, sw) * (RET_QK_DIM ** -0.5)
            for h in range(RET_HEADS):
                dm, kte, qfs, cd = dec[h]
                qh = qv[:, h * 64:(h + 1) * 64]
                kh = kv[:, h * 64:(h + 1) * 64]
                vs = slice(h * RET_V_DIM, (h + 1) * RET_V_DIM)
                vh = v_ref[rows, vs]
                sp = state[h]
                st_ref[c, h] = sp
                o = _bdot(_bdot(qh, kh, NT) * dm, vh, NN) + _bdot(qh * qfs, sp, NN)
                state[h] = cd * sp + _bdot(kh * kte, vh, TN)
                o_ref[rows, vs] = o
                gh = g_ref[rows, vs]
                r = lax.rsqrt(jnp.mean(o * o, axis=-1, keepdims=True) + EPS)
                y_ref[rows, vs] = (o * r * gain_ref[:, vs] * (gh * _sigmoid(gh))).astype(y_ref.dtype)

    tab = pl.BlockSpec((T, 128), lambda i: (i, 0))
    wide = pl.BlockSpec((T, RET_V_WIDTH), lambda i: (i, 0))
    return pl.pallas_call(
        body, grid=(S // T,),
        in_specs=[pl.BlockSpec((T, RET_QK_WIDTH), lambda i: (i, RQ_BLK)), pl.BlockSpec((T, RET_QK_WIDTH), lambda i: (i, RQ_BLK + 1)),
                  pl.BlockSpec((T, RET_V_WIDTH), lambda i: (i, RV_BLK)), pl.BlockSpec((T, RET_V_WIDTH), lambda i: (i, RV_BLK + 1)),
                  tab, tab, pl.BlockSpec((1, RET_V_WIDTH), lambda i: (0, 0)), pl.BlockSpec(memory_space=pl.ANY)],
        out_specs=[pl.BlockSpec((T, RET_V_WIDTH), lambda i: (i, (SSD_WIDTH + ATT_WIDTH) // RET_V_WIDTH)), wide,
                   pl.BlockSpec((CPS, RET_HEADS, RET_QK_DIM, RET_V_DIM), lambda i: (i, 0, 0, 0))],
        out_shape=[jax.ShapeDtypeStruct(y.shape, y.dtype), jax.ShapeDtypeStruct((S, RET_V_WIDTH), F32),
                   jax.ShapeDtypeStruct((NC, RET_HEADS, RET_QK_DIM, RET_V_DIM), F32)],
        input_output_aliases={7: 0},
        scratch_shapes=[pltpu.VMEM((RET_HEADS, RET_QK_DIM, RET_V_DIM), F32)], name=name,
        compiler_params=_params("arbitrary"),
    )(proj, proj, proj, proj, cos, sin, gain, y)


def _ret_bwd(dy, oraw, proj, cos, sin, gain, states, dproj, name):
    S = proj.shape[0]
    L = CHUNK
    T = min(512, S)
    CPS = T // L
    NI = S // T
    QW, VW = RET_QK_WIDTH, RET_V_WIDTH
    V0, G0 = 2 * QW, 2 * QW + VW

    def body(dy_ref, o_ref, q_ref, k_ref, v_ref, g_ref, cos_ref, sin_ref, gain_ref, st_ref, dproj_ref,
             out_ref, dgain_ref, dstate, dqs, dks):
        i = pl.program_id(0)

        @pl.when(i == 0)
        def _():
            dstate[...] = jnp.zeros_like(dstate)
            dgain_ref[...] = jnp.zeros_like(dgain_ref)

        dec = [_ret_decays(h) for h in range(RET_HEADS)]
        for c in reversed(range(CPS)):
            rows = pl.ds(c * L, L)
            cw = _widen(cos_ref[rows, :], RET_QK_WIDTH)
            sw = _widen(sin_ref[rows, :], RET_QK_WIDTH)
            qv = _rope(q_ref[rows, :], cw, sw)
            kv = _rope(k_ref[rows, :], cw, sw) * (RET_QK_DIM ** -0.5)
            for h in range(RET_HEADS):
                dm, kte, qfs, cd = dec[h]
                qs = slice(h * 64, (h + 1) * 64)
                vs = slice(h * RET_V_DIM, (h + 1) * RET_V_DIM)
                qh = qv[:, qs]
                kh = kv[:, qs]
                vh = v_ref[rows, vs]
                gh = g_ref[rows, vs]
                gn = gain_ref[:, vs]
                o = o_ref[rows, vs]
                dyh = dy_ref[rows, vs]
                sg = _sigmoid(gh)
                silu_g = gh * sg
                r = lax.rsqrt(jnp.mean(o * o, axis=-1, keepdims=True) + EPS)
                n = o * r
                dgain_ref[:, vs] += jnp.sum(dyh * n * silu_g, axis=0, keepdims=True)
                out_ref[rows, G0 + h * RET_V_DIM:G0 + (h + 1) * RET_V_DIM] = (dyh * n * gn * _silu_grad(gh, sg)).astype(out_ref.dtype)
                dn = dyh * gn * silu_g
                do = r * (dn - n * jnp.mean(dn * n, axis=-1, keepdims=True))
                sp = st_ref[c, h]
                ds = dstate[h]
                sc = _bdot(qh, kh, NT) * dm
                dsc = _bdot(do, vh, NT) * dm
                out_ref[rows, V0 + h * RET_V_DIM:V0 + (h + 1) * RET_V_DIM] = (_bdot(sc, do, TN) + _bdot(kh * kte, ds, NN)).astype(out_ref.dtype)
                dqs[:, qs] = _bdot(dsc, kh, NN) + _bdot(do, sp, NT) * qfs
                dks[:, qs] = _bdot(dsc, qh, TN) + _bdot(vh, ds, NT) * kte
                dstate[h] = cd * ds + _bdot(qh * qfs, do, TN)
            out_ref[rows, 0:QW] = _rope_t(dqs[...], cw, sw).astype(out_ref.dtype)
            out_ref[rows, QW:2 * QW] = _rope_t(dks[...] * (RET_QK_DIM ** -0.5), cw, sw).astype(out_ref.dtype)

    rev = lambda i: NI - 1 - i
    tab = pl.BlockSpec((T, 128), lambda i: (rev(i), 0))
    wide = pl.BlockSpec((T, RET_V_WIDTH), lambda i: (rev(i), 0))
    group = pl.BlockSpec((T, G0 + VW), lambda i: (rev(i), RET_COL // (G0 + VW)))
    gvec = pl.BlockSpec((1, RET_V_WIDTH), lambda i: (0, 0))
    return pl.pallas_call(
        body, grid=(NI,),
        in_specs=[pl.BlockSpec((T, RET_V_WIDTH), lambda i: (rev(i), (SSD_WIDTH + ATT_WIDTH) // RET_V_WIDTH)), wide,
                  pl.BlockSpec((T, RET_QK_WIDTH), lambda i: (rev(i), RQ_BLK)), pl.BlockSpec((T, RET_QK_WIDTH), lambda i: (rev(i), RQ_BLK + 1)),
                  pl.BlockSpec((T, RET_V_WIDTH), lambda i: (rev(i), RV_BLK)), pl.BlockSpec((T, RET_V_WIDTH), lambda i: (rev(i), RV_BLK + 1)),
                  tab, tab, gvec,
                  pl.BlockSpec((CPS, RET_HEADS, RET_QK_DIM, RET_V_DIM), lambda i: (rev(i), 0, 0, 0)),
                  pl.BlockSpec(memory_space=pl.ANY)],
        out_specs=[group, gvec],
        out_shape=[jax.ShapeDtypeStruct(dproj.shape, dproj.dtype), jax.ShapeDtypeStruct((1, RET_V_WIDTH), F32)],
        input_output_aliases={10: 0},
        scratch_shapes=[pltpu.VMEM((RET_HEADS, RET_QK_DIM, RET_V_DIM), F32), pltpu.VMEM((L, RET_QK_WIDTH), F32),
                        pltpu.VMEM((L, RET_QK_WIDTH), F32)],
        name=name, compiler_params=_params("arbitrary"),
    )(dy, oraw, proj, proj, proj, proj, cos, sin, gain, states, dproj)


def _adamw_update(g_ref, nb, w_ref, m_ref, v_ref, go_ref, d_ref, mo_ref, vo_ref):
    g = g_ref[0].astype(F32)
    for k in range(1, nb):
        g = g + g_ref[k].astype(F32)
    mn = ADAM_B1 * m_ref[...] + (1.0 - ADAM_B1) * g
    vn = ADAM_B2 * v_ref[...] + (1.0 - ADAM_B2) * (g * g)
    go_ref[...] = g
    mo_ref[...] = mn
    vo_ref[...] = vn
    c1 = 1.0 - ADAM_B1 ** ADAM_STEP
    c2 = 1.0 - ADAM_B2 ** ADAM_STEP
    d_ref[...] = -ADAM_LR * ((mn / c1) / (jnp.sqrt(vn / c2) + ADAM_EPS) + ADAM_WD * w_ref[...])


def _adamw_rows(R, C):
    return _pick(R, tuple(t for t in (512, 256, 128, 64, 32, 16, 8) if t * C <= 256 * 1024))


def _adamw(gblocks, w, m, v, name):
    nb, R, C = gblocks.shape
    tr = _adamw_rows(R, C)

    def body(g_ref, *refs):
        _adamw_update(g_ref, nb, *refs)

    row = pl.BlockSpec((tr, C), lambda i: (i, 0))
    sh = jax.ShapeDtypeStruct((R, C), F32)
    return pl.pallas_call(
        body, grid=(R // tr,), in_specs=[pl.BlockSpec((nb, tr, C), lambda i: (0, i, 0)), row, row, row],
        out_specs=[row, row, row, row], out_shape=[sh, sh, sh, sh], name=name, compiler_params=_params("parallel"),
    )(gblocks, w, m, v)


def _adamw_layers(g0, g1, w, m, v, name):
    nb, R, C = g0.shape
    tr = _adamw_rows(R, C)

    def body(g0_ref, g1_ref, *refs):
        l = pl.program_id(0)

        @pl.when(l == 0)
        def _():
            _adamw_update(g0_ref, nb, *refs)

        @pl.when(l == 1)
        def _():
            _adamw_update(g1_ref, nb, *refs)

    row = pl.BlockSpec((None, tr, C), lambda l, i: (l, i, 0))
    sh = jax.ShapeDtypeStruct((DEPTH, R, C), F32)
    return pl.pallas_call(
        body, grid=(DEPTH, R // tr),
        in_specs=[pl.BlockSpec((nb, tr, C), lambda l, i: (0, i * (1 - l), 0)), pl.BlockSpec((nb, tr, C), lambda l, i: (0, i * l, 0)),
                  row, row, row],
        out_specs=[row, row, row, row], out_shape=[sh, sh, sh, sh], name=name, compiler_params=_params("arbitrary", "arbitrary"),
    )(g0, g1, w, m, v)


def _peers():
    x, y, c = lax.axis_index("x"), lax.axis_index("y"), lax.axis_index("c")
    flips = ((0, 0, 1), (1, 0, 0), (0, 1, 0), (1, 1, 0), (1, 0, 1), (0, 1, 1), (1, 1, 1))
    me = 4 * x + 2 * y + c
    peers = [(x ^ fx, y ^ fy, c ^ fc) for fx, fy, fc in flips]
    return me, peers


def _exchange(arrs, scatter, name):
    n = len(arrs)
    npeer = N_DEV - 1

    def body(*refs):
        ins, outs = refs[:n], refs[n:2 * n]
        send_sems, recv_sems, local_sems = refs[2 * n:]
        me, peers = _peers()
        copies = []
        for a in range(n):
            src_own = ins[a].at[me] if scatter else ins[a]
            own = pltpu.make_async_copy(src_own, outs[a].at[me], local_sems.at[a])
            own.start()
            copies.append(own)
            for k, peer in enumerate(peers):
                src = ins[a].at[4 * peer[0] + 2 * peer[1] + peer[2]] if scatter else ins[a]
                cp = pltpu.make_async_remote_copy(
                    src_ref=src, dst_ref=outs[a].at[me], send_sem=send_sems.at[a * npeer + k],
                    recv_sem=recv_sems.at[a * npeer + k], device_id=peer, device_id_type=pl.DeviceIdType.MESH)
                cp.start()
                copies.append(cp)
        for cp in copies:
            cp.wait()

    out_shape = [jax.ShapeDtypeStruct(((N_DEV,) + a.shape[1:]) if scatter else ((N_DEV,) + a.shape), a.dtype) for a in arrs]
    anyspec = pl.BlockSpec(memory_space=pl.ANY)
    return pl.pallas_call(
        body, in_specs=[anyspec] * n, out_specs=[anyspec] * n, out_shape=out_shape,
        scratch_shapes=[pltpu.SemaphoreType.DMA((n * npeer,)), pltpu.SemaphoreType.DMA((n * npeer,)),
                        pltpu.SemaphoreType.DMA((n,))],
        name=name,
    )(*arrs)


def _dev_index(peer):
    return 4 * peer[0] + 2 * peer[1] + peer[2]


def _push_copies(src_refs, land_refs, send_sems, recv_sems, scatter, as_receiver):
    me, peers = _peers()
    npeer = N_DEV - 1
    copies = []
    for a in range(len(src_refs)):
        for k, peer in enumerate(peers):
            src = src_refs[a].at[_dev_index(peer)] if scatter else src_refs[a]
            slot = _dev_index(peer) if as_receiver else me
            copies.append(pltpu.make_async_remote_copy(
                src_ref=src, dst_ref=land_refs[a].at[slot], send_sem=send_sems.at[a * npeer + k],
                recv_sem=recv_sems.at[a * npeer + k], device_id=peer, device_id_type=pl.DeviceIdType.MESH))
    return copies


def _own_copies(src_refs, land_refs, own_sems, scatter):
    me, _ = _peers()
    return [pltpu.make_async_copy(src_refs[a].at[me] if scatter else src_refs[a], land_refs[a].at[me], own_sems.at[a])
            for a in range(len(src_refs))]


def _push_start(srcs, scatter, name):
    n = len(srcs)
    nsem = n * (N_DEV - 1)

    def body(*refs):
        srcs_r, lands_r = refs[:n], refs[n:2 * n]
        for cp in _push_copies(srcs_r, lands_r, refs[2 * n], refs[2 * n + 1], scatter, False):
            cp.start()
        for cp in _own_copies(srcs_r, lands_r, refs[2 * n + 2], scatter):
            cp.start()
        token = refs[-1]
        token[...] = jnp.zeros_like(token)

    hbm = pl.BlockSpec(memory_space=pltpu.HBM)
    sem = pl.BlockSpec(memory_space=pltpu.SEMAPHORE)
    lands = [lax.empty((N_DEV,) + (s.shape[1:] if scatter else s.shape), s.dtype) for s in srcs]
    arrs = list(srcs) + lands
    return pl.pallas_call(
        body, name=name,
        out_shape=(pltpu.SemaphoreType.DMA((nsem,)), pltpu.SemaphoreType.DMA((nsem,)), pltpu.SemaphoreType.DMA((n,)),
                   *[pltpu.HBM(a.shape, a.dtype) for a in arrs], jax.ShapeDtypeStruct((8, 128), F32)),
        in_specs=[hbm] * (2 * n), out_specs=(sem, sem, sem, *([hbm] * (2 * n)), pl.BlockSpec(memory_space=pltpu.VMEM)),
        input_output_aliases={i: 3 + i for i in range(2 * n)},
        compiler_params=pltpu.CompilerParams(has_side_effects=pltpu.SideEffectType.DATAFLOW_SIDE_EFFECTING),
    )(*[pltpu.with_memory_space_constraint(a, pltpu.HBM) for a in arrs])


def _push_wait(handle, after, scatter, name):
    send_sems, recv_sems, own_sems, *thru, _ = handle
    n = len(thru) // 2

    def body(*refs):
        srcs_r, lands_r = refs[:n], refs[n:2 * n]
        for cp in _push_copies(srcs_r, lands_r, refs[2 * n], refs[2 * n + 1], scatter, True):
            cp.wait_send()
            cp.wait_recv()
        for cp in _own_copies(srcs_r, lands_r, refs[2 * n + 2], scatter):
            cp.wait()

    hbm = pl.BlockSpec(memory_space=pltpu.HBM)
    sem = pl.BlockSpec(memory_space=pltpu.SEMAPHORE)
    outs = pl.pallas_call(
        body, name=name, out_shape=tuple(pltpu.HBM(a.shape, a.dtype) for a in thru),
        in_specs=[hbm] * (2 * n) + [sem, sem, sem, pl.BlockSpec(memory_space=pl.ANY)], out_specs=tuple([hbm] * (2 * n)),
        input_output_aliases={i: i for i in range(2 * n)},
        compiler_params=pltpu.CompilerParams(has_side_effects=pltpu.SideEffectType.DATAFLOW_SIDE_EFFECTING),
    )(*thru, send_sems, recv_sems, own_sems, after)
    return list(outs[n:])


def _tables(S):
    pos = jnp.arange(S, dtype=F32)
    inv = ROPE_THETA ** (-jnp.arange(0, ATT_HEAD_DIM, 2, dtype=F32) / ATT_HEAD_DIM)
    ang = pos[:, None] * inv[None, :]
    cos, sin = jnp.cos(ang), jnp.sin(ang)
    cos128 = jnp.tile(cos, (1, 4))
    sin128 = jnp.tile(jnp.concatenate([-sin, sin], axis=1), (1, 2))
    lane = np.arange(ATT_WIDTH)
    bd = jnp.asarray((lane[:, None] // 64 == lane[None, :] // 64).astype(np.float32), dtype=BF16)
    return cos128, sin128, bd, jnp.asarray(_att_bias())


def _layer_fwd(l, x, p, tabs, late=None):
    cos, sin, bd, bias = tabs
    S = x.shape[0]
    row = lambda v: v.reshape(1, -1)
    hn = _rmsnorm_fwd(x, row(p["ln_mix"]), f"norm_mix_fwd{l}")
    proj = _mm(hn, p["w_in"], "nn", f"in_proj{l}", tn=1152)
    xbc = _conv_fwd(proj, p["conv_w"], row(p["conv_b"]), f"conv_fwd{l}")
    dtr = proj[:, DT_COL:DT_COL + SSD_HEADS].reshape(S, SSD_GROUPS, HPG).transpose(1, 0, 2)
    grp = lambda v: v.reshape(SSD_GROUPS, 1, HPG)
    d_rep = row(jnp.repeat(p["d_skip"], SSD_HEAD_DIM))
    y, yraw, ssd_st = _ssd_fwd(xbc, proj, dtr, grp(p["dt_bias"]), grp(p["a_log"]), d_rep, row(p["ssd_norm"]), f"ssd_fwd{l}")
    qg = row(jnp.tile(p["q_norm"], ATT_HEADS))
    kg = row(jnp.tile(p["k_norm"], ATT_HEADS))
    aq, akp, avp = _att_prep_fwd(proj, qg, kg, cos, sin, bd, f"att_prep_fwd{l}")
    y = _att_fwd(aq, akp, avp, bias, y, f"att_fwd{l}")
    y, oraw, ret_st = _ret_fwd(proj, cos, sin, row(p["ret_norm"]), y, f"ret_fwd{l}")
    if late is not None:
        p.update(late(y))
    x1 = _mm(y, p["w_out"], "nn", f"out_proj{l}", residual=x)
    hn2 = _rmsnorm_fwd(x1, row(p["ln_ffn"]), f"norm_ffn_fwd{l}")
    g, u, act = _swiglu_fwd(hn2, p["w_gate"], p["w_up"], f"swiglu_fwd{l}")
    x2 = _mm(act, p["w_down"], "nn", f"down_proj{l}", residual=x1, tk=2816)
    saved = dict(x=x, hn=hn, proj=proj, xbc=xbc, dtr=dtr, yraw=yraw, ssd_st=ssd_st, aq=aq, akp=akp, avp=avp,
                 oraw=oraw, ret_st=ret_st, y=y, x1=x1, hn2=hn2, g=g, u=u, act=act, d_rep=d_rep, qg=qg, kg=kg)
    return x2, saved


def _layer_bwd(l, dx2, dx2_bf, p, sv, tabs, on_ffn=None, on_all=None):
    cos, sin, bd, bias = tabs
    S = dx2.shape[0]
    row = lambda v: v.reshape(1, -1)
    grp = lambda v: v.reshape(SSD_GROUPS, 1, HPG)
    gr = {}
    dg, du = _swiglu_bwd(dx2_bf, p["w_down"], sv["g"], sv["u"], f"swiglu_bwd{l}")
    gr["w_down"] = _mm(sv["act"], dx2_bf, "tn", f"down_wgrad{l}", out_dtype=BF16, tm=1408, tn=1024, tk=2048)
    dhn2 = _mm_nt2(dg, p["w_gate"], du, p["w_up"], f"ffn_dgrad{l}")
    gr["w_gate"] = _mm(sv["hn2"], dg, "tn", f"gate_wgrad{l}", out_dtype=BF16, tm=1024, tn=1408, tk=2048)
    gr["w_up"] = _mm(sv["hn2"], du, "tn", f"up_wgrad{l}", out_dtype=BF16, tm=1024, tn=1408, tk=2048)
    ffn_gain = row(p["ln_ffn"]) + (on_ffn(gr)[0, 0] if on_ffn is not None else 0.0)
    dx1, dx1_bf, dln_ffn = _rmsnorm_bwd(sv["x1"], dhn2, ffn_gain, dx2, f"norm_ffn_bwd{l}")
    gr["ln_ffn"] = dln_ffn[0]
    dy = _mm(dx1_bf, p["w_out"], "nt", f"out_dgrad{l}")
    gr["w_out"] = _mm(sv["y"], dx1_bf, "tn", f"out_wgrad{l}", out_dtype=BF16, tm=1024, tn=1024, tk=2048)
    dproj, dxs, dbm, dcm, ddtr, dbias, dalog, dd, dssd_gain = _ssd_bwd(
        dy, sv["yraw"], sv["xbc"], sv["proj"], sv["dtr"], grp(p["dt_bias"]), grp(p["a_log"]), sv["d_rep"],
        row(p["ssd_norm"]), sv["ssd_st"], f"ssd_bwd{l}")
    gr["dt_bias"], gr["a_log"], gr["d_skip"] = dbias.reshape(-1), dalog.reshape(-1), dd.reshape(-1)
    gr["ssd_norm"] = dssd_gain[0]
    dproj, dconv_w, dconv_b = _conv_bwd(dxs, dbm, dcm, sv["proj"], p["conv_w"], row(p["conv_b"]), dproj, f"conv_bwd{l}")
    gr["conv_w"], gr["conv_b"] = dconv_w, dconv_b[0]
    dq, dk_p, dv_p = _att_bwd(sv["aq"], sv["akp"], sv["avp"], bias, dy, f"att_bwd{l}")
    dproj, dqg, dkg = _att_prep_bwd(sv["proj"], dq, dk_p, dv_p, sv["qg"], sv["kg"], cos, sin, bd, dproj, f"att_prep_bwd{l}")
    gr["q_norm"], gr["k_norm"] = dqg[0], dkg[0]
    dproj, dret_gain = _ret_bwd(dy, sv["oraw"], sv["proj"], cos, sin, row(p["ret_norm"]), sv["ret_st"], dproj, f"ret_bwd{l}")
    gr["ret_norm"] = dret_gain[0]
    ddt_cols = ddtr.transpose(1, 0, 2).reshape(S, SSD_HEADS).astype(BF16)
    dproj = lax.dynamic_update_slice(dproj, jnp.pad(ddt_cols, ((0, 0), (0, IN_PAD - DT_COL - SSD_HEADS))), (0, DT_COL))
    gr["w_in"] = _mm(sv["hn"], dproj, "tn", f"in_wgrad{l}", out_dtype=BF16, tm=1024, tn=1152, tk=2048)
    launched = on_all(gr) if on_all is not None else None
    dhn = _mm(dproj, p["w_in"], "nt", f"in_dgrad{l}", tk=1920, after=launched)
    dx0, dx0_bf, dln_mix = _rmsnorm_bwd(sv["x"], dhn, row(p["ln_mix"]), dx1, f"norm_mix_bwd{l}")
    gr["ln_mix"] = dln_mix[0]
    return dx0, dx0_bf, gr


def _local_step(x, tgt, layers, late=None, on_ffn=None, on_all=None):
    n = len(layers)
    none = [None] * n
    late, on_ffn, on_all = late or none, on_ffn or none, on_all or none
    tabs = _tables(x.shape[0])
    saved, params = [], []
    h = x
    for l in range(n):
        p = dict(layers[l](h) if callable(layers[l]) else layers[l])
        h, sv = _layer_fwd(l, h, p, tabs, late[l])
        saved.append(sv)
        params.append(p)
    dh, dh_bf, lacc = _loss_grad(h, tgt, "loss_grad")
    grads = [None] * n
    for l in reversed(range(n)):
        dh, dh_bf, grads[l] = _layer_bwd(l, dh, dh_bf, params[l], saved[l], tabs, on_ffn[l], on_all[l])
    return lacc[0, 0], dh, grads


BIG = ("w_in", "w_out", "w_gate", "w_up", "w_down")
SMALL = ("ln_mix", "conv_b", "dt_bias", "a_log", "d_skip", "ssd_norm", "q_norm", "k_norm", "ret_norm", "ln_ffn")
ORDER = ("ln_mix", "w_in", "conv_w", "conv_b", "dt_bias", "a_log", "d_skip", "ssd_norm", "q_norm", "k_norm", "ret_norm",
         "w_out", "ln_ffn", "w_gate", "w_up", "w_down")


COL_SHARDED = ("w_in", "w_gate", "w_up", "conv_w")


IN_GROUPS = ((ORIG_Z_XBC, Z_COL), (ORIG_DT, DT_COL), (ORIG_ATT, ATT_COL), (ORIG_RET, RET_COL))


def _full_weight(k, gathered):
    if k == "w_in":
        cs = gathered.shape[2]
        pieces = []
        for (lo, hi), _ in sorted(IN_GROUPS, key=lambda grp: grp[1]):
            for j in range(N_DEV):
                a, b = max(lo, j * cs), min(hi, (j + 1) * cs)
                if a < b:
                    pieces.append(gathered[j][:, a - j * cs:b - j * cs])
        pieces.append(jnp.zeros((gathered.shape[1], IN_PAD - IN_WIDTH), gathered.dtype))
        return jnp.concatenate(pieces, axis=1)
    if k in COL_SHARDED:
        return gathered.transpose(1, 0, 2).reshape(gathered.shape[1], -1)
    return gathered.reshape(-1, gathered.shape[2])


def _shard_block(k, g):
    if k == "w_in":
        cs = IN_WIDTH // N_DEV
        blocks = []
        for j in range(N_DEV):
            pieces = []
            for (lo, hi), col in IN_GROUPS:
                a, b = max(lo, j * cs), min(hi, (j + 1) * cs)
                if a < b:
                    pieces.append(g[:, col + a - lo:col + b - lo])
            blocks.append(jnp.concatenate(pieces, axis=1))
        return jnp.stack(blocks)
    if k in COL_SHARDED:
        return g.reshape(g.shape[0], N_DEV, -1).transpose(1, 0, 2)
    return g.reshape(N_DEV, -1, g.shape[1])


def kernel(x, ln_mix, w_in, conv_w, conv_b, dt_bias, a_log, d_skip, ssd_norm, q_norm, k_norm, ret_norm, w_out, ln_ffn, w_gate, w_up, w_down, loss_target, m_ln_mix, m_w_in, m_conv_w, m_conv_b, m_dt_bias, m_a_log, m_d_skip, m_ssd_norm, m_q_norm, m_k_norm, m_ret_norm, m_w_out, m_ln_ffn, m_w_gate, m_w_up, m_w_down, v_ln_mix, v_w_in, v_conv_w, v_conv_b, v_dt_bias, v_a_log, v_d_skip, v_ssd_norm, v_q_norm, v_k_norm, v_ret_norm, v_w_out, v_ln_ffn, v_w_gate, v_w_up, v_w_down):
    w = dict(ln_mix=ln_mix, w_in=w_in, conv_w=conv_w, conv_b=conv_b, dt_bias=dt_bias, a_log=a_log, d_skip=d_skip,
             ssd_norm=ssd_norm, q_norm=q_norm, k_norm=k_norm, ret_norm=ret_norm, w_out=w_out, ln_ffn=ln_ffn,
             w_gate=w_gate, w_up=w_up, w_down=w_down)
    m = dict(ln_mix=m_ln_mix, w_in=m_w_in, conv_w=m_conv_w, conv_b=m_conv_b, dt_bias=m_dt_bias, a_log=m_a_log,
             d_skip=m_d_skip, ssd_norm=m_ssd_norm, q_norm=m_q_norm, k_norm=m_k_norm, ret_norm=m_ret_norm, w_out=m_w_out,
             ln_ffn=m_ln_ffn, w_gate=m_w_gate, w_up=m_w_up, w_down=m_w_down)
    v = dict(ln_mix=v_ln_mix, w_in=v_w_in, conv_w=v_conv_w, conv_b=v_conv_b, dt_bias=v_dt_bias, a_log=v_a_log,
             d_skip=v_d_skip, ssd_norm=v_ssd_norm, q_norm=v_q_norm, k_norm=v_k_norm, ret_norm=v_ret_norm, w_out=v_w_out,
             ln_ffn=v_ln_ffn, w_gate=v_w_gate, w_up=v_w_up, w_down=v_w_down)
    me = 4 * lax.axis_index("x") + 2 * lax.axis_index("y") + lax.axis_index("c")

    waves = {"a": [("w_in", 0), ("conv_w", 0), ("conv_w", 1)],
             "b": [(k, 0) for k in ("w_out", "w_gate", "w_up", "w_down")],
             "c": [(k, 1) for k in BIG]}
    gather = {}
    behind = 0.0
    for tag, items in waves.items():
        srcs = [w[k][l] if k == "conv_w" else (w[k][l] + behind).astype(BF16) for k, l in items]
        gather[tag] = _push_start(srcs, False, f"gather_{tag}_start")
        behind = gather[tag][-1][0, 0]
    started = behind
    full = {}

    def arrive(tag, after):
        for (k, l), g in zip(waves[tag], _push_wait(gather[tag], after, False, f"gather_{tag}_wait")):
            full[k, l] = _full_weight(k, g)

    def layer_weights(l, names):
        return {k: full[k, l] for k in names}

    def small_weights(l):
        return {k: w[k][l] for k in SMALL}

    def layer0(h):
        arrive("a", gather["c"][-1])
        p = small_weights(0)
        p["ln_mix"] = p["ln_mix"] + started
        return {**p, **layer_weights(0, ("w_in", "conv_w"))}

    def late0(y):
        arrive("b", y)
        return layer_weights(0, waves_b_names)

    def layer1(h):
        arrive("c", h)
        return {**small_weights(1), **layer_weights(1, BIG + ("conv_w",))}

    waves_b_names = tuple(k for k, _ in waves["b"])

    groups = {"1": [(k, 1) for k in BIG], "0a": [(k, 0) for k in ("w_down", "w_gate", "w_up")],
              "0b": [(k, 0) for k in ("w_out", "w_in")]}
    scatter = {}

    def push_grads(tag, gr):
        blocks = [_shard_block(k, gr[k]) for k, _ in groups[tag]]
        scatter[tag] = _push_start(blocks, True, f"scatter_{tag}_start")
        return scatter[tag][-1]

    loss_part, gx, grads = _local_step(
        x[0], loss_target[0], [layer0, layer1], late=[late0, None],
        on_ffn=[functools.partial(push_grads, "0a"), None],
        on_all=[functools.partial(push_grads, "0b"), functools.partial(push_grads, "1")])
    loss = lax.psum(loss_part, MESH_AXES)

    out = {}
    recv = {}
    for tag, items in groups.items():
        for item, r in zip(items, _push_wait(scatter[tag], gx, True, f"scatter_{tag}_wait")):
            recv[item] = r
    for k in BIG:
        out[k] = _adamw_layers(recv[k, 0], recv[k, 1], w[k], m[k], v[k], f"adamw_{k}")
    names = SMALL + ("conv_w",)
    sizes = [int(np.prod(grads[0][k].shape)) for k in names]
    packed = jnp.concatenate([jnp.stack([grads[l][k] for l in range(DEPTH)]).reshape(-1) for k in names])
    n_small = packed.shape[0]
    rows_small = -(-n_small // 1024) * 8
    pad = lambda t, fill: jnp.concatenate([t, jnp.full((rows_small * 128 - n_small,), fill, F32)]).reshape(rows_small, 128)
    parts = _exchange([pad(packed, 0.0)], False, "gather_small_grads")[0]
    n_rep = DEPTH * sum(sizes[:-1])
    pack_rep = lambda d, fill: pad(jnp.concatenate([d[k].reshape(-1) for k in SMALL]
                                                   + [jnp.full((n_small - n_rep,), fill, F32)]), fill)
    res = _adamw(parts, pack_rep(w, 1.0), pack_rep(m, 1.0), pack_rep(v, 1.0), "adamw_small")
    res = [t.reshape(-1) for t in res]
    off = 0
    for k, sz in zip(SMALL, sizes[:-1]):
        out[k] = [t[off:off + DEPTH * sz].reshape(w[k].shape) for t in res]
        off += DEPTH * sz
    gconv = res[0][off:off + DEPTH * sizes[-1]].reshape(DEPTH, SSD_CONV, SSD_CONV_CH)
    gconv = lax.dynamic_slice_in_dim(gconv, me * conv_w.shape[2], conv_w.shape[2], axis=2)
    flat = lambda t: t.reshape(8, -1)
    resc = _adamw(flat(gconv)[None], flat(conv_w), flat(m_conv_w), flat(v_conv_w), "adamw_conv_w")
    out["conv_w"] = [t.reshape(conv_w.shape) for t in resc]

    return (loss, gx[None], *[out[k][0] for k in ORDER], *[out[k][1] for k in ORDER],
            *[out[k][2] for k in ORDER], *[out[k][3] for k in ORDER])
```

```python
import functools
import math

import jax
import jax.numpy as jnp
import numpy as np
from jax import lax
from jax.experimental import pallas as pl
from jax.experimental.pallas import tpu as pltpu

F32 = jnp.float32
BF16 = jnp.bfloat16

N_DEV = 8
MESH_AXES = ("x", "y", "c")
D_MODEL = 2048
DEPTH = 2
EPS = 1e-6
ROPE_THETA = 10000.0
SSD_HEADS = 16
SSD_HEAD_DIM = 64
SSD_WIDTH = 1024
SSD_GROUPS = 2
SSD_STATE = 128
SSD_CONV = 4
SSD_CONV_CH = 1536
ATT_HEADS = 8
ATT_HEAD_DIM = 64
ATT_WIDTH = 512
DILATED_PAIRS = ((128, 1), (512, 4), (2048, 16))
RET_HEADS = 4
RET_QK_DIM = 64
RET_V_DIM = 128
RET_QK_WIDTH = 256
RET_V_WIDTH = 512
CHUNK = 128
MIX_WIDTH = 2048
ATT_SPAN = 2048
ATT_STRIP = ATT_SPAN + CHUNK
ATT_QB = 4
IN_WIDTH = 5648
IN_PAD = 5760
RET_COL, ATT_COL, Z_COL, XBC_COL, DT_COL = 0, 1536, 3072, 4096, 5632
ORIG_Z_XBC, ORIG_DT, ORIG_ATT, ORIG_RET = (0, 2560), (2560, 2576), (2576, 4112), (4112, 5648)
D_FF = 5632
ADAM_LR = 0.001
ADAM_B1 = 0.9
ADAM_B2 = 0.999
ADAM_EPS = 1e-08
ADAM_WD = 0.01
ADAM_STEP = 10
NEG = -1e30
VMEM_LIMIT_V7X = 60 * 1024 * 1024

NN = (((1,), (0,)), ((), ()))
NT = (((1,), (1,)), ((), ()))
TN = (((0,), (0,)), ((), ()))


def _bdot(a, b, dims):
    return lax.dot_general(a.astype(BF16), b.astype(BF16), dims, preferred_element_type=F32)


def _xdot(a, b, dims, exact_first=False):
    ones, x = (a, b) if exact_first else (b, a)
    ones = ones.astype(BF16)
    acc, rest = None, x
    for _ in range(3):
        piece = rest.astype(BF16)
        rest = rest - piece.astype(F32)
        part = lax.dot_general(*((ones, piece) if exact_first else (piece, ones)), dims, preferred_element_type=F32)
        acc = part if acc is None else acc + part
    return acc


def _params(*sem):
    return pltpu.CompilerParams(dimension_semantics=sem, vmem_limit_bytes=VMEM_LIMIT_V7X)


def _sigmoid(v):
    return 1.0 / (1.0 + jnp.exp(-v))


def _silu_grad(v, s):
    return s * (1.0 + v * (1.0 - s))


def _rmsnorm_fwd(x, g, name):
    S, D = x.shape
    tr = min(512, S)

    def body(x_ref, g_ref, o_ref):
        xv = x_ref[...]
        r = lax.rsqrt(jnp.mean(xv * xv, axis=-1, keepdims=True) + EPS)
        o_ref[...] = (xv * r * g_ref[...]).astype(o_ref.dtype)

    return pl.pallas_call(
        body, grid=(S // tr,),
        in_specs=[pl.BlockSpec((tr, D), lambda i: (i, 0)), pl.BlockSpec((1, D), lambda i: (0, 0))],
        out_specs=pl.BlockSpec((tr, D), lambda i: (i, 0)),
        out_shape=jax.ShapeDtypeStruct((S, D), BF16), name=name, compiler_params=_params("parallel"),
    )(x, g)


def _rmsnorm_bwd(x, dy, g, dres, name):
    S, D = x.shape
    tr = min(512, S)

    def body(x_ref, dy_ref, g_ref, dres_ref, dx_ref, dxb_ref, dg_ref):
        i = pl.program_id(0)
        xv = x_ref[...]
        r = lax.rsqrt(jnp.mean(xv * xv, axis=-1, keepdims=True) + EPS)
        n = xv * r
        dy = dy_ref[...]
        dn = dy * g_ref[...]
        dx = dres_ref[...] + r * (dn - n * jnp.mean(dn * n, axis=-1, keepdims=True))
        dx_ref[...] = dx
        dxb_ref[...] = dx.astype(BF16)
        part = jnp.sum(dy * n, axis=0, keepdims=True)

        @pl.when(i == 0)
        def _():
            dg_ref[...] = part

        @pl.when(i > 0)
        def _():
            dg_ref[...] += part

    row = pl.BlockSpec((tr, D), lambda i: (i, 0))
    vec = pl.BlockSpec((1, D), lambda i: (0, 0))
    return pl.pallas_call(
        body, grid=(S // tr,), in_specs=[row, row, vec, row], out_specs=[row, row, vec],
        out_shape=[jax.ShapeDtypeStruct((S, D), F32), jax.ShapeDtypeStruct((S, D), BF16), jax.ShapeDtypeStruct((1, D), F32)],
        name=name, compiler_params=_params("arbitrary"),
    )(x, dy, g, dres)


def _loss_grad(y, tgt, name):
    S, D = y.shape
    tr = min(512, S)

    def body(y_ref, t_ref, dy_ref, dyb_ref, l_ref):
        i = pl.program_id(0)
        err = y_ref[...] - t_ref[...]
        dy = err * (1.0 / D)
        dy_ref[...] = dy
        dyb_ref[...] = dy.astype(BF16)
        part = jnp.sum(jnp.sum(err * err, axis=1, keepdims=True), axis=0, keepdims=True) * (0.5 / D)

        @pl.when(i == 0)
        def _():
            l_ref[...] = jnp.zeros_like(l_ref)

        l_ref[...] += part

    row = pl.BlockSpec((tr, D), lambda i: (i, 0))
    return pl.pallas_call(
        body, grid=(S // tr,), in_specs=[row, row],
        out_specs=[row, row, pl.BlockSpec((8, 128), lambda i: (0, 0))],
        out_shape=[jax.ShapeDtypeStruct((S, D), F32), jax.ShapeDtypeStruct((S, D), BF16), jax.ShapeDtypeStruct((8, 128), F32)],
        name=name, compiler_params=_params("arbitrary"),
    )(y, tgt)


def _pick(n, cands):
    for c in cands:
        if n % c == 0:
            return c
    return n


def _mm(a, b, mode, name, out_dtype=F32, residual=None, tm=None, tn=None, tk=None, after=None):
    if mode == "nn":
        (M, K), (_, N) = a.shape, b.shape
    elif mode == "nt":
        (M, K), (N, _) = a.shape, b.shape
    else:
        (K, M), (_, N) = a.shape, b.shape
    tm = min(tm, M) if tm else _pick(M, (1024, 512, 256, 128))
    tn = min(tn, N) if tn else _pick(N, (1024, 1152, 1408, 512, 256, 128))
    tk = min(tk, K) if tk else _pick(K, (2048, 1920, 1408, 1024, 512, 256, 128))
    assert M % tm == 0 and N % tn == 0 and K % tk == 0, (name, M, N, K, tm, tn, tk)
    nk = K // tk
    a_spec = pl.BlockSpec((tk, tm), lambda i, j, k: (k, i)) if mode == "tn" else pl.BlockSpec((tm, tk), lambda i, j, k: (i, k))
    b_spec = pl.BlockSpec((tn, tk), lambda i, j, k: (j, k)) if mode == "nt" else pl.BlockSpec((tk, tn), lambda i, j, k: (k, j))
    o_spec = pl.BlockSpec((tm, tn), lambda i, j, k: (i, j))
    dims = {"nn": NN, "nt": NT, "tn": TN}[mode]
    has_res = residual is not None

    has_after = after is not None

    def body(*refs):
        a_ref, b_ref = refs[0], refs[1]
        r_ref = refs[2] if has_res else None
        o_ref = refs[2 + has_res + has_after]
        p = _bdot(a_ref[...], b_ref[...], dims)

        def finish(acc):
            if has_res:
                acc = acc + r_ref[...]
            o_ref[...] = acc.astype(o_ref.dtype)

        if nk == 1:
            finish(p)
        else:
            acc_ref = refs[-1]
            k = pl.program_id(2)

            @pl.when(k == 0)
            def _():
                acc_ref[...] = p

            @pl.when(k > 0)
            def _():
                acc_ref[...] += p

            @pl.when(k == nk - 1)
            def _():
                finish(acc_ref[...])

    ins = [a, b] + ([residual] if has_res else []) + ([after] if has_after else [])
    in_specs = [a_spec, b_spec] + ([o_spec] if has_res else []) + ([pl.BlockSpec(memory_space=pl.ANY)] if has_after else [])
    scratch = [pltpu.VMEM((tm, tn), F32)] if nk > 1 else []
    return pl.pallas_call(
        body, grid=(M // tm, N // tn, nk), in_specs=in_specs, out_specs=o_spec,
        out_shape=jax.ShapeDtypeStruct((M, N), out_dtype), scratch_shapes=scratch, name=name,
        compiler_params=_params("parallel", "parallel", "arbitrary"),
    )(*ins)


def _accumulate(acc_ref, p, k, nk, finish):
    @pl.when(k == 0)
    def _():
        acc_ref[...] = p

    @pl.when(k > 0)
    def _():
        acc_ref[...] += p

    @pl.when(k == nk - 1)
    def _():
        finish(acc_ref[...])


def _swiglu_fwd(hn, wg, wu, name):
    S, K = hn.shape
    F = wg.shape[1]
    tm = _pick(S, (1024, 512))
    tn = _pick(F, (512, 256, 128))

    def body(a_ref, wg_ref, wu_ref, g_ref, u_ref, act_ref):
        a = a_ref[...]
        g = _bdot(a, wg_ref[...], NN)
        u = _bdot(a, wu_ref[...], NN)
        g_ref[...] = g.astype(BF16)
        u_ref[...] = u.astype(BF16)
        act_ref[...] = (g * _sigmoid(g) * u).astype(BF16)

    w_spec = pl.BlockSpec((K, tn), lambda i, j: (0, j))
    o_spec = pl.BlockSpec((tm, tn), lambda i, j: (i, j))
    sh = jax.ShapeDtypeStruct((S, F), BF16)
    return pl.pallas_call(
        body, grid=(S // tm, F // tn), in_specs=[pl.BlockSpec((tm, K), lambda i, j: (i, 0)), w_spec, w_spec],
        out_specs=[o_spec, o_spec, o_spec], out_shape=[sh, sh, sh], name=name,
        compiler_params=_params("parallel", "parallel"),
    )(hn, wg, wu)


def _swiglu_bwd(dx, wd, g, u, name):
    S, K = dx.shape
    F = wd.shape[0]
    tm = _pick(S, (1024, 512))
    tn = _pick(F, (512, 256, 128))

    def body(dx_ref, wd_ref, g_ref, u_ref, dg_ref, du_ref):
        da = _bdot(dx_ref[...], wd_ref[...], NT)
        gv = g_ref[...].astype(F32)
        uv = u_ref[...].astype(F32)
        s = _sigmoid(gv)
        dg_ref[...] = (da * uv * _silu_grad(gv, s)).astype(BF16)
        du_ref[...] = (da * gv * s).astype(BF16)

    o_spec = pl.BlockSpec((tm, tn), lambda i, j: (i, j))
    sh = jax.ShapeDtypeStruct((S, F), BF16)
    return pl.pallas_call(
        body, grid=(S // tm, F // tn),
        in_specs=[pl.BlockSpec((tm, K), lambda i, j: (i, 0)), pl.BlockSpec((tn, K), lambda i, j: (j, 0)), o_spec, o_spec],
        out_specs=[o_spec, o_spec], out_shape=[sh, sh], name=name, compiler_params=_params("parallel", "parallel"),
    )(dx, wd, g, u)


def _mm_nt2(a1, b1, a2, b2, name):
    M, K = a1.shape
    N = b1.shape[0]
    tm = _pick(M, (1024, 512))
    tn = _pick(N, (1024, 512))
    tk = _pick(K, (1408, 1024, 512, 256, 128))
    nk = K // tk

    def body(a1_ref, b1_ref, a2_ref, b2_ref, o_ref, acc_ref):
        def finish(acc):
            o_ref[...] = acc

        p = _bdot(a1_ref[...], b1_ref[...], NT) + _bdot(a2_ref[...], b2_ref[...], NT)
        _accumulate(acc_ref, p, pl.program_id(2), nk, finish)

    a_spec = pl.BlockSpec((tm, tk), lambda i, j, k: (i, k))
    b_spec = pl.BlockSpec((tn, tk), lambda i, j, k: (j, k))
    return pl.pallas_call(
        body, grid=(M // tm, N // tn, nk), in_specs=[a_spec, b_spec, a_spec, b_spec],
        out_specs=pl.BlockSpec((tm, tn), lambda i, j, k: (i, j)), out_shape=jax.ShapeDtypeStruct((M, N), F32),
        scratch_shapes=[pltpu.VMEM((tm, tn), F32)], name=name,
        compiler_params=_params("parallel", "parallel", "arbitrary"),
    )(a1, b1, a2, b2)


XBC_BLK0 = XBC_COL // 128


def _conv_fwd(proj, w, b, name):
    S = proj.shape[0]
    T = min(512, S)

    def body(x_ref, w_ref, b_ref, o_ref, xp_ref):
        xp_ref[pl.ds(0, 8), :] = jnp.zeros((8, 128), F32)
        xp_ref[pl.ds(8, S), :] = x_ref[...]
        wv = w_ref[...]
        bv = b_ref[...]

        def step(c, carry):
            base = pl.multiple_of(c * T, T)
            acc = wv[0:1] * xp_ref[pl.ds(base + 5, T), :]
            for i in range(1, SSD_CONV):
                acc = acc + wv[i:i + 1] * xp_ref[pl.ds(base + 5 + i, T), :]
            acc = bv + acc
            o_ref[pl.ds(base, T), :] = acc * _sigmoid(acc)
            return carry

        lax.fori_loop(0, S // T, step, 0)

    return pl.pallas_call(
        body, grid=(SSD_CONV_CH // 128,),
        in_specs=[pl.BlockSpec((S, 128), lambda j: (0, XBC_BLK0 + j)), pl.BlockSpec((SSD_CONV, 128), lambda j: (0, j)),
                  pl.BlockSpec((1, 128), lambda j: (0, j))],
        out_specs=pl.BlockSpec((S, 128), lambda j: (0, j)),
        out_shape=jax.ShapeDtypeStruct((S, SSD_CONV_CH), F32),
        scratch_shapes=[pltpu.VMEM((S + 8, 128), F32)], name=name, compiler_params=_params("parallel"),
    )(proj, w, b)


def _conv_bwd(dxs, dbm, dcm, proj, w, b, dproj, name):
    S = proj.shape[0]
    T = min(512, S)
    NX, NB = SSD_WIDTH // 128, SSD_GROUPS * SSD_STATE // 128

    def body(dxs_ref, dbm_ref, dcm_ref, x_ref, w_ref, b_ref, dproj_ref, dx_ref, dw_ref, db_ref, xp_ref, dcp_ref):
        j = pl.program_id(0)

        @pl.when(j < NX)
        def _():
            dcp_ref[pl.ds(0, S), :] = dxs_ref[...]

        @pl.when((j >= NX) & (j < NX + NB))
        def _():
            dcp_ref[pl.ds(0, S), :] = dbm_ref[...]

        @pl.when(j >= NX + NB)
        def _():
            dcp_ref[pl.ds(0, S), :] = dcm_ref[...]

        da_ref = dcp_ref
        xp_ref[pl.ds(0, 8), :] = jnp.zeros((8, 128), F32)
        xp_ref[pl.ds(8, S), :] = x_ref[...]
        dcp_ref[pl.ds(S, 8), :] = jnp.zeros((8, 128), F32)
        wv = w_ref[...]
        bv = b_ref[...]

        def step1(c, carry):
            base = pl.multiple_of(c * T, T)
            xs = [xp_ref[pl.ds(base + 5 + i, T), :] for i in range(SSD_CONV)]
            acc = wv[0:1] * xs[0]
            for i in range(1, SSD_CONV):
                acc = acc + wv[i:i + 1] * xs[i]
            acc = bv + acc
            s = _sigmoid(acc)
            dc = da_ref[pl.ds(base, T), :] * _silu_grad(acc, s)
            dcp_ref[pl.ds(base, T), :] = dc
            new = tuple(carry[i] + jnp.sum(xs[i] * dc, axis=0, keepdims=True) for i in range(SSD_CONV))
            return new + (carry[SSD_CONV] + jnp.sum(dc, axis=0, keepdims=True),)

        z = jnp.zeros((1, 128), F32)
        res = lax.fori_loop(0, S // T, step1, (z,) * (SSD_CONV + 1))
        for i in range(SSD_CONV):
            dw_ref[pl.ds(i, 1), :] = res[i]
        db_ref[...] = res[SSD_CONV]

        def step2(c, carry):
            base = pl.multiple_of(c * T, T)
            acc = wv[0:1] * dcp_ref[pl.ds(base + 3, T), :]
            for i in range(1, SSD_CONV):
                acc = acc + wv[i:i + 1] * dcp_ref[pl.ds(base + 3 - i, T), :]
            dx_ref[pl.ds(base, T), :] = acc.astype(dx_ref.dtype)
            return carry

        lax.fori_loop(0, S // T, step2, 0)

    clamp = lambda j, lo, n: jnp.clip(j - lo, 0, n - 1)
    return pl.pallas_call(
        body, grid=(SSD_CONV_CH // 128,),
        in_specs=[pl.BlockSpec((S, 128), lambda j: (0, clamp(j, 0, NX))), pl.BlockSpec((S, 128), lambda j: (0, clamp(j, NX, NB))),
                  pl.BlockSpec((S, 128), lambda j: (0, clamp(j, NX + NB, NB))),
                  pl.BlockSpec((S, 128), lambda j: (0, XBC_BLK0 + j)), pl.BlockSpec((SSD_CONV, 128), lambda j: (0, j)),
                  pl.BlockSpec((1, 128), lambda j: (0, j)), pl.BlockSpec(memory_space=pl.ANY)],
        out_specs=[pl.BlockSpec((S, 128), lambda j: (0, XBC_BLK0 + j)), pl.BlockSpec((SSD_CONV, 128), lambda j: (0, j)),
                   pl.BlockSpec((1, 128), lambda j: (0, j))],
        out_shape=[jax.ShapeDtypeStruct(dproj.shape, dproj.dtype), jax.ShapeDtypeStruct((SSD_CONV, SSD_CONV_CH), F32),
                   jax.ShapeDtypeStruct((1, SSD_CONV_CH), F32)],
        input_output_aliases={6: 0},
        scratch_shapes=[pltpu.VMEM((S + 8, 128), F32), pltpu.VMEM((S + 8, 128), F32)], name=name,
        compiler_params=_params("arbitrary"),
    )(dxs, dbm, dcm, proj, w, b, dproj)


HPG = SSD_HEADS // SSD_GROUPS
GW = HPG * SSD_HEAD_DIM


def _ssd_chunk_terms(dtr, bias, alog, tril, triu):
    pre = dtr + bias
    dt = jnp.maximum(pre, 0.0) + jnp.log(1.0 + jnp.exp(-jnp.abs(pre)))
    a_neg = -jnp.exp(alog)
    a = dt * a_neg
    acum = _xdot(tril, a, NN, exact_first=True)
    acum_t = _xdot(a, triu, TN)
    return pre, dt, a_neg, acum, acum_t


def _head_expanders():
    h64 = lax.broadcasted_iota(jnp.int32, (HPG, GW), 0) == lax.broadcasted_iota(jnp.int32, (HPG, GW), 1) // SSD_HEAD_DIM
    h128 = lax.broadcasted_iota(jnp.int32, (HPG, HPG * CHUNK), 0) == lax.broadcasted_iota(jnp.int32, (HPG, HPG * CHUNK), 1) // CHUNK
    return h64.astype(F32), h128.astype(F32)


def _ssd_fwd(xbc, proj, dtr, dt_bias, a_log, d_rep, gain, name):
    S = xbc.shape[0]
    L = CHUNK
    T = min(512, S)
    CPS = T // L
    NC = S // L

    def body(x_ref, b_ref, c_ref, z_ref, dtr_ref, bias_ref, alog_ref, d_ref, gain_ref, y_ref, yraw_ref, st_ref, state):
        i = pl.program_id(1)

        @pl.when(i == 0)
        def _():
            state[...] = jnp.zeros_like(state)

        row = lax.broadcasted_iota(jnp.int32, (L, L), 0)
        col = lax.broadcasted_iota(jnp.int32, (L, L), 1)
        causal = row >= col
        tril = causal.astype(F32)
        triu = (row <= col).astype(F32)
        low = col < SSD_HEAD_DIM
        e64, e128 = _head_expanders()
        for c in range(CPS):
            rows = pl.ds(c * L, L)
            xv = x_ref[rows, :]
            bm = b_ref[rows, :]
            cm = c_ref[rows, :]
            _, dt, _, acum, acum_t = _ssd_chunk_terms(dtr_ref[rows, :], bias_ref[...], alog_ref[...], tril, triu)
            ac = _xdot(acum, e64, NN)
            ac_sq = _xdot(acum, e128, NN)
            xd = xv * _xdot(dt, e64, NN)
            ac_last = ac[L - 1:L, :]
            sp = state[...]
            st_ref[c] = sp
            yoff = _bdot(cm, sp, NN) * jnp.exp(ac)
            state[...] = sp * jnp.exp(ac_last) + _bdot(bm, xd * jnp.exp(ac_last - ac), TN)
            gmat = _bdot(cm, bm, NT)
            for q in range(HPG // 2):
                pair = slice(q * 128, (q + 1) * 128)
                tile = xd[:, pair]
                y = yoff[:, pair]
                for j, keep in ((2 * q, low), (2 * q + 1, ~low)):
                    lam = jnp.exp(jnp.where(causal, ac_sq[:, j * L:(j + 1) * L] - acum_t[j:j + 1, :], NEG))
                    y = y + _bdot(gmat * lam, jnp.where(keep, tile, 0.0), NN)
                yraw_ref[rows, pair] = y
            zz = z_ref[rows, :]
            u = (yraw_ref[rows, :] + xv * d_ref[...]) * (zz * _sigmoid(zz))
            r = lax.rsqrt(jnp.mean(u * u, axis=-1, keepdims=True) + EPS)
            y_ref[rows, :] = (u * r * gain_ref[...]).astype(y_ref.dtype)

    vec8 = pl.BlockSpec((None, 1, HPG), lambda g, i: (g, 0, 0))
    return pl.pallas_call(
        body, grid=(SSD_GROUPS, S // T),
        in_specs=[pl.BlockSpec((T, GW), lambda g, i: (i, g)),
                  pl.BlockSpec((T, SSD_STATE), lambda g, i: (i, SSD_WIDTH // SSD_STATE + g)),
                  pl.BlockSpec((T, SSD_STATE), lambda g, i: (i, SSD_WIDTH // SSD_STATE + SSD_GROUPS + g)),
                  pl.BlockSpec((T, GW), lambda g, i: (i, Z_COL // GW + g)),
                  pl.BlockSpec((None, T, HPG), lambda g, i: (g, i, 0)),
                  vec8, vec8,
                  pl.BlockSpec((1, GW), lambda g, i: (0, g)), pl.BlockSpec((1, GW), lambda g, i: (0, g))],
        out_specs=[pl.BlockSpec((T, GW), lambda g, i: (i, g)), pl.BlockSpec((T, GW), lambda g, i: (i, g)),
                   pl.BlockSpec((CPS, None, SSD_STATE, GW), lambda g, i: (i, g, 0, 0))],
        out_shape=[jax.ShapeDtypeStruct((S, MIX_WIDTH), BF16), jax.ShapeDtypeStruct((S, SSD_WIDTH), F32),
                   jax.ShapeDtypeStruct((NC, SSD_GROUPS, SSD_STATE, GW), F32)],
        scratch_shapes=[pltpu.VMEM((SSD_STATE, GW), F32)], name=name,
        compiler_params=_params("arbitrary", "arbitrary"),
    )(xbc, xbc, xbc, proj, dtr, dt_bias, a_log, d_rep, gain)


def _ssd_bwd(dy, yraw, xbc, proj, dtr, dt_bias, a_log, d_rep, gain, states, name):
    S = xbc.shape[0]
    L = CHUNK
    T = min(512, S)
    CPS = T // L
    NI = S // T

    def body(dy_ref, yraw_ref, x_ref, b_ref, c_ref, z_ref, dtr_ref, bias_ref, alog_ref, d_ref, gain_ref, st_ref,
             dz_ref, dx_ref, db_ref, dc_ref, ddtr_ref, dbias_ref, dalog_ref, dd_ref, dgain_ref, dstate, dxd_ref):
        i = pl.program_id(1)

        @pl.when(i == 0)
        def _():
            dstate[...] = jnp.zeros_like(dstate)
            dbias_ref[...] = jnp.zeros_like(dbias_ref)
            dalog_ref[...] = jnp.zeros_like(dalog_ref)
            dd_ref[...] = jnp.zeros_like(dd_ref)
            dgain_ref[...] = jnp.zeros_like(dgain_ref)

        row = lax.broadcasted_iota(jnp.int32, (L, L), 0)
        col = lax.broadcasted_iota(jnp.int32, (L, L), 1)
        causal = row >= col
        tril = causal.astype(F32)
        triu = (row <= col).astype(F32)
        low = col < SSD_HEAD_DIM
        e64, e128 = _head_expanders()
        lane8 = lax.broadcasted_iota(jnp.int32, (1, HPG), 1)
        sub8 = lax.broadcasted_iota(jnp.int32, (HPG, 1), 0)
        eye8 = (lax.broadcasted_iota(jnp.int32, (HPG, HPG), 0) == lax.broadcasted_iota(jnp.int32, (HPG, HPG), 1)).astype(F32)
        last_row = (lax.broadcasted_iota(jnp.int32, (L, 1), 0) == L - 1).astype(F32)
        for c in reversed(range(CPS)):
            rows = pl.ds(c * L, L)
            xv = x_ref[rows, :]
            bm = b_ref[rows, :]
            cm = c_ref[rows, :]
            zz = z_ref[rows, :]
            dvec = d_ref[...]
            sz = _sigmoid(zz)
            silu_z = zz * sz
            v = yraw_ref[rows, :] + xv * dvec
            u = v * silu_z
            r = lax.rsqrt(jnp.mean(u * u, axis=-1, keepdims=True) + EPS)
            n = u * r
            do = dy_ref[rows, :]
            dgain_ref[...] += jnp.sum(do * n, axis=0, keepdims=True)
            dn = do * gain_ref[...]
            du = r * (dn - n * jnp.mean(dn * n, axis=-1, keepdims=True))
            dz_ref[rows, :] = (du * v * _silu_grad(zz, sz)).astype(dz_ref.dtype)
            dyv = du * silu_z
            dd_ref[...] += _xdot(jnp.sum(dyv * xv, axis=0, keepdims=True), e64, NT)
            pre, dt, a_neg, acum, acum_t = _ssd_chunk_terms(dtr_ref[rows, :], bias_ref[...], alog_ref[...], tril, triu)
            ac = _xdot(acum, e64, NN)
            ac_sq = _xdot(acum, e128, NN)
            dt_w = _xdot(dt, e64, NN)
            xd = xv * dt_w
            ac_last = ac[L - 1:L, :]
            ea = jnp.exp(ac)
            w = jnp.exp(ac_last - ac)
            ea_last = jnp.exp(ac_last)
            sp = st_ref[c]
            ds = dstate[...]
            dye = dyv * ea
            yoff = _bdot(cm, sp, NN) * ea
            bds = _bdot(bm, ds, NN)
            dcm = _bdot(dye, sp, NT)
            dbm = _bdot(xd * w, ds, NT)
            dstate[...] = ds * ea_last + _bdot(cm, dye, TN)
            w8 = jnp.exp(acum[L - 1:L, :] - acum)
            dw8 = _xdot(xd * bds, e64, NT)
            dac8 = _xdot(dyv * yoff, e64, NT) - dw8 * w8
            tail8 = jnp.sum(dw8 * w8, axis=0, keepdims=True) + jnp.exp(acum[L - 1:L, :]) * _xdot(
                jnp.sum(ds * sp, axis=0, keepdims=True), e64, NT)
            dac8 = dac8 + last_row * tail8
            gmat = _bdot(cm, bm, NT)
            dgmat = jnp.zeros((L, L), F32)
            colsum_t = jnp.zeros((HPG, L), F32)
            for q in range(HPG // 2):
                pair = slice(q * 128, (q + 1) * 128)
                xd_tile = xd[:, pair]
                dy_tile = dyv[:, pair]
                dxd_tile = bds[:, pair] * w[:, pair]
                for j, keep in ((2 * q, low), (2 * q + 1, ~low)):
                    lam = jnp.exp(jnp.where(causal, ac_sq[:, j * L:(j + 1) * L] - acum_t[j:j + 1, :], NEG))
                    mh = gmat * lam
                    dyj = jnp.where(keep, dy_tile, 0.0)
                    dxd_tile = dxd_tile + _bdot(mh, dyj, TN)
                    dm = _bdot(dyj, xd_tile, NT)
                    dgmat = dgmat + dm * lam
                    qm = dm * mh
                    dac8 = dac8 + jnp.sum(qm, axis=1, keepdims=True) * (lane8 == j).astype(F32)
                    colsum_t = colsum_t + (sub8 == j).astype(F32) * jnp.sum(qm, axis=0, keepdims=True)
                dxd_ref[:, pair] = dxd_tile
            dac8 = dac8 - _xdot(colsum_t, eye8, TN)
            dxd = dxd_ref[...]
            dx_ref[rows, :] = dxd * dt_w + dyv * dvec
            dc_ref[rows, :] = dcm + _bdot(dgmat, bm, NN)
            db_ref[rows, :] = dbm + _bdot(dgmat, cm, TN)
            da8 = _xdot(triu, dac8, NN, exact_first=True)
            ddt8 = _xdot(dxd * xv, e64, NT) + da8 * a_neg
            dalog_ref[...] += jnp.sum(da8 * dt, axis=0, keepdims=True) * a_neg
            dpre = ddt8 * _sigmoid(pre)
            ddtr_ref[rows, :] = dpre
            dbias_ref[...] += jnp.sum(dpre, axis=0, keepdims=True)

    rev = lambda i: NI - 1 - i
    vec8 = pl.BlockSpec((None, 1, HPG), lambda g, i: (g, 0, 0))
    grp = pl.BlockSpec((T, GW), lambda g, i: (rev(i), g))
    bspec = pl.BlockSpec((T, SSD_STATE), lambda g, i: (rev(i), SSD_WIDTH // SSD_STATE + g))
    cspec = pl.BlockSpec((T, SSD_STATE), lambda g, i: (rev(i), SSD_WIDTH // SSD_STATE + SSD_GROUPS + g))
    gvec = pl.BlockSpec((1, GW), lambda g, i: (0, g))
    st_spec = pl.BlockSpec((CPS, None, SSD_STATE, GW), lambda g, i: (rev(i), g, 0, 0))
    small = jax.ShapeDtypeStruct((SSD_GROUPS, 1, HPG), F32)
    zspec = pl.BlockSpec((T, GW), lambda g, i: (rev(i), Z_COL // GW + g))
    return pl.pallas_call(
        body, grid=(SSD_GROUPS, NI),
        in_specs=[grp, grp, grp, bspec, cspec, zspec, pl.BlockSpec((None, T, HPG), lambda g, i: (g, rev(i), 0)),
                  vec8, vec8, gvec, gvec, st_spec],
        out_specs=[zspec, grp, pl.BlockSpec((T, SSD_STATE), lambda g, i: (rev(i), g)),
                   pl.BlockSpec((T, SSD_STATE), lambda g, i: (rev(i), g)),
                   pl.BlockSpec((None, T, HPG), lambda g, i: (g, rev(i), 0)), vec8, vec8, vec8, gvec],
        out_shape=[jax.ShapeDtypeStruct((S, IN_PAD), BF16), jax.ShapeDtypeStruct((S, SSD_WIDTH), F32),
                   jax.ShapeDtypeStruct((S, SSD_GROUPS * SSD_STATE), F32), jax.ShapeDtypeStruct((S, SSD_GROUPS * SSD_STATE), F32),
                   jax.ShapeDtypeStruct((SSD_GROUPS, S, HPG), F32), small, small, small,
                   jax.ShapeDtypeStruct((1, SSD_WIDTH), F32)],
        scratch_shapes=[pltpu.VMEM((SSD_STATE, GW), F32), pltpu.VMEM((L, GW), F32)],
        name=name, compiler_params=_params("arbitrary", "arbitrary"),
    )(dy, yraw, xbc, xbc, xbc, proj, dtr, dt_bias, a_log, d_rep, gain, states)


def _swap_halves(t):
    w = t.shape[1]
    lane = lax.broadcasted_iota(jnp.int32, t.shape, 1)
    return jnp.where((lane % 64) < 32, pltpu.roll(t, w - 32, axis=1), pltpu.roll(t, 32, axis=1))


def _widen(tab, w):
    return tab if w == 128 else jnp.concatenate([tab] * (w // 128), axis=1)


def _rope(t, cos, sin_signed):
    return t * cos + _swap_halves(t) * sin_signed


def _rope_t(d, cos, sin_signed):
    return d * cos - _swap_halves(d) * sin_signed


def _group_sum64(v, bd):
    hi = v.astype(BF16)
    lo = (v - hi.astype(F32)).astype(BF16)
    return (lax.dot_general(hi, bd, NN, preferred_element_type=F32)
            + lax.dot_general(lo, bd, NN, preferred_element_type=F32))


AQ_BLK = ATT_COL // ATT_WIDTH


def _att_prep_fwd(proj, qg, kg, cos, sin, bd, name):
    S = proj.shape[0]
    T = min(512, S)
    PB = ATT_SPAN // T
    src = lambda i: jnp.maximum(i - PB, 0)

    def body(q_ref, k_ref, v_ref, qg_ref, kg_ref, cos_ref, sin_ref, bd_ref, qo_ref, ko_ref, vo_ref):
        i = pl.program_id(0)

        @pl.when(i < PB)
        def _():
            ko_ref[...] = jnp.zeros_like(ko_ref)
            vo_ref[...] = jnp.zeros_like(vo_ref)

        @pl.when(i >= PB)
        def _():
            cw = _widen(cos_ref[...], ATT_WIDTH)
            sw = _widen(sin_ref[...], ATT_WIDTH)
            bdv = bd_ref[...]

            def norm_rope(t, gain):
                ss = _group_sum64(t * t, bdv)
                return _rope(t * lax.rsqrt(ss * (1.0 / ATT_HEAD_DIM) + EPS) * gain, cw, sw)

            qo_ref[...] = (norm_rope(q_ref[...], qg_ref[...]) * (ATT_HEAD_DIM ** -0.5)).astype(BF16)
            kt = norm_rope(k_ref[...], kg_ref[...]).astype(BF16)
            vt = v_ref[...].astype(BF16)
            for pr in range(ATT_HEADS // 2):
                ko_ref[pr] = kt[:, pr * 128:(pr + 1) * 128]
                vo_ref[pr] = vt[:, pr * 128:(pr + 1) * 128]

    vec = pl.BlockSpec((1, ATT_WIDTH), lambda i: (0, 0))
    tab = pl.BlockSpec((T, 128), lambda i: (src(i), 0))
    hm = pl.BlockSpec((ATT_HEADS // 2, T, 128), lambda i: (0, i, 0))
    hm_shape = jax.ShapeDtypeStruct((ATT_HEADS // 2, ATT_SPAN + S, 128), BF16)
    return pl.pallas_call(
        body, grid=(PB + S // T,),
        in_specs=[pl.BlockSpec((T, ATT_WIDTH), lambda i: (src(i), AQ_BLK)), pl.BlockSpec((T, ATT_WIDTH), lambda i: (src(i), AQ_BLK + 1)),
                  pl.BlockSpec((T, ATT_WIDTH), lambda i: (src(i), AQ_BLK + 2)), vec, vec, tab, tab,
                  pl.BlockSpec((ATT_WIDTH, ATT_WIDTH), lambda i: (0, 0))],
        out_specs=[pl.BlockSpec((T, ATT_WIDTH), lambda i: (src(i), 0)), hm, hm],
        out_shape=[jax.ShapeDtypeStruct((S, ATT_WIDTH), BF16), hm_shape, hm_shape],
        name=name, compiler_params=_params("arbitrary"),
    )(proj, proj, proj, qg, kg, cos, sin, bd)


def _att_prep_bwd(proj, dq, dk_p, dv_p, qg, kg, cos, sin, bd, dproj, name):
    S = proj.shape[0]
    T = min(512, S)
    NI = S // T
    PB = ATT_SPAN // T
    W = ATT_WIDTH

    def body(q_ref, k_ref, dq_ref, dkp_ref, dvp_ref, qg_ref, kg_ref, cos_ref, sin_ref, bd_ref, dproj_ref,
             do_ref, dqg_ref, dkg_ref, acc_ref):
        i = pl.program_id(0)

        @pl.when(i == 0)
        def _():
            acc_ref[...] = jnp.zeros_like(acc_ref)

        npair = ATT_HEADS // 2
        dk_all = jnp.concatenate([dkp_ref[pr].T for pr in range(npair)], axis=1)
        do_ref[:, 2 * W:3 * W] = jnp.concatenate([dvp_ref[pr].T for pr in range(npair)], axis=1).astype(BF16)
        cw = _widen(cos_ref[...], ATT_WIDTH)
        sw = _widen(sin_ref[...], ATT_WIDTH)
        bdv = bd_ref[...]

        def one(t, d_rot, gain, scale, slot):
            ss = _group_sum64(t * t, bdv)
            r = lax.rsqrt(ss * (1.0 / ATT_HEAD_DIM) + EPS)
            n = t * r
            d_ng = _rope_t(d_rot * scale, cw, sw)
            acc_ref[pl.ds(slot, 1), :] += jnp.sum(d_ng * n, axis=0, keepdims=True)
            dn = d_ng * gain
            return r * (dn - n * (_group_sum64(dn * n, bdv) * (1.0 / ATT_HEAD_DIM)))

        do_ref[:, 0:W] = one(q_ref[...], dq_ref[...], qg_ref[...], ATT_HEAD_DIM ** -0.5, 0).astype(BF16)
        do_ref[:, W:2 * W] = one(k_ref[...], dk_all, kg_ref[...], 1.0, 1).astype(BF16)

        @pl.when(i == NI - 1)
        def _():
            a = acc_ref[...]
            f = a[:, 0:64]
            for h in range(1, ATT_HEADS):
                f = f + a[:, h * 64:(h + 1) * 64]
            dqg_ref[...] = f[0:1]
            dkg_ref[...] = f[1:2]

    vec = pl.BlockSpec((1, ATT_WIDTH), lambda i: (0, 0))
    tab = pl.BlockSpec((T, 128), lambda i: (i, 0))
    row = pl.BlockSpec((T, ATT_WIDTH), lambda i: (i, 0))
    g64 = pl.BlockSpec((1, ATT_HEAD_DIM), lambda i: (0, 0))
    padded = pl.BlockSpec((ATT_HEADS // 2, 128, T), lambda i: (0, 0, i + PB))
    return pl.pallas_call(
        body, grid=(NI,),
        in_specs=[pl.BlockSpec((T, ATT_WIDTH), lambda i: (i, AQ_BLK)), pl.BlockSpec((T, ATT_WIDTH), lambda i: (i, AQ_BLK + 1)),
                  row, padded, padded, vec, vec, tab, tab, pl.BlockSpec((ATT_WIDTH, ATT_WIDTH), lambda i: (0, 0)),
                  pl.BlockSpec(memory_space=pl.ANY)],
        out_specs=[pl.BlockSpec((T, 3 * W), lambda i: (i, ATT_COL // (3 * W))), g64, g64],
        out_shape=[jax.ShapeDtypeStruct(dproj.shape, dproj.dtype), jax.ShapeDtypeStruct((1, ATT_HEAD_DIM), F32),
                   jax.ShapeDtypeStruct((1, ATT_HEAD_DIM), F32)],
        input_output_aliases={10: 0},
        scratch_shapes=[pltpu.VMEM((8, ATT_WIDTH), F32)], name=name, compiler_params=_params("arbitrary"),
    )(proj, proj, dq, dk_p, dv_p, qg, kg, cos, sin, bd, dproj)


def _att_bias():
    qpos = np.arange(CHUNK)[:, None] + ATT_SPAN
    kpos = np.arange(ATT_STRIP)[None, :]
    rel = qpos - kpos
    mult = np.zeros((CHUNK, ATT_STRIP), np.float64)
    for window, dil in DILATED_PAIRS:
        mult += (rel >= 0) & (rel % dil == 0) & (rel // dil <= window // dil)
    return np.where(mult > 0, np.log(np.maximum(mult, 1.0)), NEG).astype(np.float32)


def _att_scores(q, ks, bias, i):
    s = _bdot(q, ks, NT) + bias
    kcol = lax.broadcasted_iota(jnp.int32, (1, ATT_STRIP), 1) + i * CHUNK
    return jnp.where(kcol >= ATT_SPAN, s, NEG)


def _pair_masks():
    low = lax.broadcasted_iota(jnp.int32, (CHUNK, 128), 1) < ATT_HEAD_DIM
    return low, ~low


def _att_fwd(q, kp, vp, bias, y, name):
    S = q.shape[0]
    SP = kp.shape[1]

    TQ = ATT_QB * CHUNK

    def body(q_ref, k_ref, v_ref, bias_ref, y_ref, o_ref):
        i = pl.program_id(1)
        for b in range(ATT_QB):
            blk = i * ATT_QB + b
            strip = pl.ds(pl.multiple_of(blk * CHUNK, CHUNK), ATT_STRIP)
            rows = pl.ds(b * CHUNK, CHUNK)
            qv = q_ref[rows, :]
            ks = k_ref[strip, :]
            vs = v_ref[strip, :]
            outs = []
            for keep in _pair_masks():
                s = _att_scores(jnp.where(keep, qv, jnp.zeros_like(qv)), ks, bias_ref[...], blk)
                m = jnp.max(s, axis=-1, keepdims=True)
                p = jnp.exp(s - m)
                den = jnp.sum(p, axis=-1, keepdims=True)
                outs.append(_bdot(p, vs, NN) / den)
            o_ref[rows, :] = jnp.where(_pair_masks()[0], outs[0], outs[1]).astype(o_ref.dtype)

    kv = pl.BlockSpec((None, SP, 128), lambda hp, i: (hp, 0, 0))
    return pl.pallas_call(
        body, grid=(ATT_HEADS // 2, S // TQ),
        in_specs=[pl.BlockSpec((TQ, 128), lambda hp, i: (i, hp)), kv, kv,
                  pl.BlockSpec((CHUNK, ATT_STRIP), lambda hp, i: (0, 0)), pl.BlockSpec(memory_space=pl.ANY)],
        out_specs=pl.BlockSpec((TQ, 128), lambda hp, i: (i, SSD_WIDTH // 128 + hp)),
        out_shape=jax.ShapeDtypeStruct(y.shape, y.dtype), input_output_aliases={4: 0}, name=name,
        compiler_params=_params("parallel", "arbitrary"),
    )(q, kp, vp, bias, y)


def _att_bwd(q, kp, vp, bias, dy, name):
    S = q.shape[0]
    SP = kp.shape[1]

    def body(q_ref, k_ref, v_ref, bias_ref, do_ref, dq_ref, dk_ref, dv_ref):
        i = pl.program_id(1)

        @pl.when(i == 0)
        def _():
            dk_ref[...] = jnp.zeros_like(dk_ref)
            dv_ref[...] = jnp.zeros_like(dv_ref)

        for b in range(ATT_QB):
            blk = i * ATT_QB + b
            strip = pl.ds(pl.multiple_of(blk * CHUNK, CHUNK), ATT_STRIP)
            rows = pl.ds(b * CHUNK, CHUNK)
            qv = q_ref[rows, :]
            dov = do_ref[rows, :]
            ks = k_ref[strip, :]
            vs = v_ref[strip, :]
            dq = jnp.zeros((CHUNK, 128), F32)
            dk_t = jnp.zeros((128, ATT_STRIP), F32)
            dv_t = jnp.zeros((128, ATT_STRIP), F32)
            for keep in _pair_masks():
                qh = jnp.where(keep, qv, jnp.zeros_like(qv))
                doh = jnp.where(keep, dov, 0.0)
                s = _att_scores(qh, ks, bias_ref[...], blk)
                m = jnp.max(s, axis=-1, keepdims=True)
                p = jnp.exp(s - m)
                p = p / jnp.sum(p, axis=-1, keepdims=True)
                dp = _bdot(doh, vs, NT)
                dsc = p * (dp - jnp.sum(p * dp, axis=-1, keepdims=True))
                dq = dq + jnp.where(keep, _bdot(dsc, ks, NN), 0.0)
                dv_t = dv_t + _bdot(doh, p, TN)
                dk_t = dk_t + _bdot(qh, dsc, TN)
            dq_ref[rows, :] = dq
            dv_ref[:, strip] += dv_t
            dk_ref[:, strip] += dk_t

    TQ = ATT_QB * CHUNK
    kv = pl.BlockSpec((None, SP, 128), lambda hp, i: (hp, 0, 0))
    kv_t = pl.BlockSpec((None, 128, SP), lambda hp, i: (hp, 0, 0))
    pairs = jax.ShapeDtypeStruct((ATT_HEADS // 2, 128, SP), F32)
    return pl.pallas_call(
        body, grid=(ATT_HEADS // 2, S // TQ),
        in_specs=[pl.BlockSpec((TQ, 128), lambda hp, i: (i, hp)), kv, kv,
                  pl.BlockSpec((CHUNK, ATT_STRIP), lambda hp, i: (0, 0)),
                  pl.BlockSpec((TQ, 128), lambda hp, i: (i, SSD_WIDTH // 128 + hp))],
        out_specs=[pl.BlockSpec((TQ, 128), lambda hp, i: (i, hp)), kv_t, kv_t],
        out_shape=[jax.ShapeDtypeStruct((S, ATT_WIDTH), F32), pairs, pairs],
        name=name, compiler_params=_params("parallel", "arbitrary"),
    )(q, kp, vp, bias, dy)


RQ_BLK = RET_COL // RET_QK_WIDTH
RV_BLK = (RET_COL + 2 * RET_QK_WIDTH) // RET_V_WIDTH
RET_LOG_GAMMA = tuple(math.log1p(-2.0 ** (-5.0 - h)) for h in range(RET_HEADS))


def _ret_decays(h):
    L = CHUNK
    lg = RET_LOG_GAMMA[h]
    row = lax.broadcasted_iota(jnp.int32, (L, L), 0)
    col = lax.broadcasted_iota(jnp.int32, (L, L), 1)
    rel = (row - col).astype(F32)
    dm = jnp.where(rel >= 0, jnp.exp(jnp.maximum(rel, 0.0) * lg), 0.0)
    idx = lax.broadcasted_iota(jnp.int32, (L, 1), 0).astype(F32)
    kte = jnp.exp((L - 1 - idx) * lg)
    qfs = jnp.exp((idx + 1.0) * lg)
    return dm, kte, qfs, math.exp(L * lg)


def _ret_fwd(proj, cos, sin, gain, y, name):
    S = proj.shape[0]
    L = CHUNK
    T = min(512, S)
    CPS = T // L
    NC = S // L

    def body(q_ref, k_ref, v_ref, g_ref, cos_ref, sin_ref, gain_ref, yin_ref, y_ref, o_ref, st_ref, state):
        i = pl.program_id(0)

        @pl.when(i == 0)
        def _():
            state[...] = jnp.zeros_like(state)

        dec = [_ret_decays(h) for h in range(RET_HEADS)]
        for c in range(CPS):
            rows = pl.ds(c * L, L)
            cw = _widen(cos_ref[rows, :], RET_QK_WIDTH)
            sw = _widen(sin_ref[rows, :], RET_QK_WIDTH)
            qv = _rope(q_ref[rows, :], cw, sw)
            kv = _rope(k_ref[rows, :], cw, sw) * (RET_QK_DIM ** -0.5)
            for h in range(RET_HEADS):
                dm, kte, qfs, cd = dec[h]
                qh = qv[:, h * 64:(h + 1) * 64]
                kh = kv[:, h * 64:(h + 1) * 64]
                vs = slice(h * RET_V_DIM, (h + 1) * RET_V_DIM)
                vh = v_ref[rows, vs]
                sp = state[h]
                st_ref[c, h] = sp
                o = _bdot(_bdot(qh, kh, NT) * dm, vh, NN) + _bdot(qh * qfs, sp, NN)
                state[h] = cd * sp + _bdot(kh * kte, vh, TN)
                o_ref[rows, vs] = o
                gh = g_ref[rows, vs]
                r = lax.rsqrt(jnp.mean(o * o, axis=-1, keepdims=True) + EPS)
                y_ref[rows, vs] = (o * r * gain_ref[:, vs] * (gh * _sigmoid(gh))).astype(y_ref.dtype)

    tab = pl.BlockSpec((T, 128), lambda i: (i, 0))
    wide = pl.BlockSpec((T, RET_V_WIDTH), lambda i: (i, 0))
    return pl.pallas_call(
        body, grid=(S // T,),
        in_specs=[pl.BlockSpec((T, RET_QK_WIDTH), lambda i: (i, RQ_BLK)), pl.BlockSpec((T, RET_QK_WIDTH), lambda i: (i, RQ_BLK + 1)),
                  pl.BlockSpec((T, RET_V_WIDTH), lambda i: (i, RV_BLK)), pl.BlockSpec((T, RET_V_WIDTH), lambda i: (i, RV_BLK + 1)),
                  tab, tab, pl.BlockSpec((1, RET_V_WIDTH), lambda i: (0, 0)), pl.BlockSpec(memory_space=pl.ANY)],
        out_specs=[pl.BlockSpec((T, RET_V_WIDTH), lambda i: (i, (SSD_WIDTH + ATT_WIDTH) // RET_V_WIDTH)), wide,
                   pl.BlockSpec((CPS, RET_HEADS, RET_QK_DIM, RET_V_DIM), lambda i: (i, 0, 0, 0))],
        out_shape=[jax.ShapeDtypeStruct(y.shape, y.dtype), jax.ShapeDtypeStruct((S, RET_V_WIDTH), F32),
                   jax.ShapeDtypeStruct((NC, RET_HEADS, RET_QK_DIM, RET_V_DIM), F32)],
        input_output_aliases={7: 0},
        scratch_shapes=[pltpu.VMEM((RET_HEADS, RET_QK_DIM, RET_V_DIM), F32)], name=name,
        compiler_params=_params("arbitrary"),
    )(proj, proj, proj, proj, cos, sin, gain, y)


def _ret_bwd(dy, oraw, proj, cos, sin, gain, states, dproj, name):
    S = proj.shape[0]
    L = CHUNK
    T = min(512, S)
    CPS = T // L
    NI = S // T
    QW, VW = RET_QK_WIDTH, RET_V_WIDTH
    V0, G0 = 2 * QW, 2 * QW + VW

    def body(dy_ref, o_ref, q_ref, k_ref, v_ref, g_ref, cos_ref, sin_ref, gain_ref, st_ref, dproj_ref,
             out_ref, dgain_ref, dstate, dqs, dks):
        i = pl.program_id(0)

        @pl.when(i == 0)
        def _():
            dstate[...] = jnp.zeros_like(dstate)
            dgain_ref[...] = jnp.zeros_like(dgain_ref)

        dec = [_ret_decays(h) for h in range(RET_HEADS)]
        for c in reversed(range(CPS)):
            rows = pl.ds(c * L, L)
            cw = _widen(cos_ref[rows, :], RET_QK_WIDTH)
            sw = _widen(sin_ref[rows, :], RET_QK_WIDTH)
            qv = _rope(q_ref[rows, :], cw, sw)
            kv = _rope(k_ref[rows, :], cw, sw) * (RET_QK_DIM ** -0.5)
            for h in range(RET_HEADS):
                dm, kte, qfs, cd = dec[h]
                qs = slice(h * 64, (h + 1) * 64)
                vs = slice(h * RET_V_DIM, (h + 1) * RET_V_DIM)
                qh = qv[:, qs]
                kh = kv[:, qs]
                vh = v_ref[rows, vs]
                gh = g_ref[rows, vs]
                gn = gain_ref[:, vs]
                o = o_ref[rows, vs]
                dyh = dy_ref[rows, vs]
                sg = _sigmoid(gh)
                silu_g = gh * sg
                r = lax.rsqrt(jnp.mean(o * o, axis=-1, keepdims=True) + EPS)
                n = o * r
                dgain_ref[:, vs] += jnp.sum(dyh * n * silu_g, axis=0, keepdims=True)
                out_ref[rows, G0 + h * RET_V_DIM:G0 + (h + 1) * RET_V_DIM] = (dyh * n * gn * _silu_grad(gh, sg)).astype(out_ref.dtype)
                dn = dyh * gn * silu_g
                do = r * (dn - n * jnp.mean(dn * n, axis=-1, keepdims=True))
                sp = st_ref[c, h]
                ds = dstate[h]
                sc = _bdot(qh, kh, NT) * dm
                dsc = _bdot(do, vh, NT) * dm
                out_ref[rows, V0 + h * RET_V_DIM:V0 + (h + 1) * RET_V_DIM] = (_bdot(sc, do, TN) + _bdot(kh * kte, ds, NN)).astype(out_ref.dtype)
                dqs[:, qs] = _bdot(dsc, kh, NN) + _bdot(do, sp, NT) * qfs
                dks[:, qs] = _bdot(dsc, qh, TN) + _bdot(vh, ds, NT) * kte
                dstate[h] = cd * ds + _bdot(qh * qfs, do, TN)
            out_ref[rows, 0:QW] = _rope_t(dqs[...], cw, sw).astype(out_ref.dtype)
            out_ref[rows, QW:2 * QW] = _rope_t(dks[...] * (RET_QK_DIM ** -0.5), cw, sw).astype(out_ref.dtype)

    rev = lambda i: NI - 1 - i
    tab = pl.BlockSpec((T, 128), lambda i: (rev(i), 0))
    wide = pl.BlockSpec((T, RET_V_WIDTH), lambda i: (rev(i), 0))
    group = pl.BlockSpec((T, G0 + VW), lambda i: (rev(i), RET_COL // (G0 + VW)))
    gvec = pl.BlockSpec((1, RET_V_WIDTH), lambda i: (0, 0))
    return pl.pallas_call(
        body, grid=(NI,),
        in_specs=[pl.BlockSpec((T, RET_V_WIDTH), lambda i: (rev(i), (SSD_WIDTH + ATT_WIDTH) // RET_V_WIDTH)), wide,
                  pl.BlockSpec((T, RET_QK_WIDTH), lambda i: (rev(i), RQ_BLK)), pl.BlockSpec((T, RET_QK_WIDTH), lambda i: (rev(i), RQ_BLK + 1)),
                  pl.BlockSpec((T, RET_V_WIDTH), lambda i: (rev(i), RV_BLK)), pl.BlockSpec((T, RET_V_WIDTH), lambda i: (rev(i), RV_BLK + 1)),
                  tab, tab, gvec,
                  pl.BlockSpec((CPS, RET_HEADS, RET_QK_DIM, RET_V_DIM), lambda i: (rev(i), 0, 0, 0)),
                  pl.BlockSpec(memory_space=pl.ANY)],
        out_specs=[group, gvec],
        out_shape=[jax.ShapeDtypeStruct(dproj.shape, dproj.dtype), jax.ShapeDtypeStruct((1, RET_V_WIDTH), F32)],
        input_output_aliases={10: 0},
        scratch_shapes=[pltpu.VMEM((RET_HEADS, RET_QK_DIM, RET_V_DIM), F32), pltpu.VMEM((L, RET_QK_WIDTH), F32),
                        pltpu.VMEM((L, RET_QK_WIDTH), F32)],
        name=name, compiler_params=_params("arbitrary"),
    )(dy, oraw, proj, proj, proj, proj, cos, sin, gain, states, dproj)


def _adamw_update(g_ref, nb, w_ref, m_ref, v_ref, go_ref, d_ref, mo_ref, vo_ref):
    g = g_ref[0].astype(F32)
    for k in range(1, nb):
        g = g + g_ref[k].astype(F32)
    mn = ADAM_B1 * m_ref[...] + (1.0 - ADAM_B1) * g
    vn = ADAM_B2 * v_ref[...] + (1.0 - ADAM_B2) * (g * g)
    go_ref[...] = g
    mo_ref[...] = mn
    vo_ref[...] = vn
    c1 = 1.0 - ADAM_B1 ** ADAM_STEP
    c2 = 1.0 - ADAM_B2 ** ADAM_STEP
    d_ref[...] = -ADAM_LR * ((mn / c1) / (jnp.sqrt(vn / c2) + ADAM_EPS) + ADAM_WD * w_ref[...])


def _adamw_rows(R, C):
    return _pick(R, tuple(t for t in (512, 256, 128, 64, 32, 16, 8) if t * C <= 256 * 1024))


def _adamw(gblocks, w, m, v, name):
    nb, R, C = gblocks.shape
    tr = _adamw_rows(R, C)

    def body(g_ref, *refs):
        _adamw_update(g_ref, nb, *refs)

    row = pl.BlockSpec((tr, C), lambda i: (i, 0))
    sh = jax.ShapeDtypeStruct((R, C), F32)
    return pl.pallas_call(
        body, grid=(R // tr,), in_specs=[pl.BlockSpec((nb, tr, C), lambda i: (0, i, 0)), row, row, row],
        out_specs=[row, row, row, row], out_shape=[sh, sh, sh, sh], name=name, compiler_params=_params("parallel"),
    )(gblocks, w, m, v)


def _adamw_layers(g0, g1, w, m, v, name):
    nb, R, C = g0.shape
    tr = _adamw_rows(R, C)

    def body(g0_ref, g1_ref, *refs):
        l = pl.program_id(0)

        @pl.when(l == 0)
        def _():
            _adamw_update(g0_ref, nb, *refs)

        @pl.when(l == 1)
        def _():
            _adamw_update(g1_ref, nb, *refs)

    row = pl.BlockSpec((None, tr, C), lambda l, i: (l, i, 0))
    sh = jax.ShapeDtypeStruct((DEPTH, R, C), F32)
    return pl.pallas_call(
        body, grid=(DEPTH, R // tr),
        in_specs=[pl.BlockSpec((nb, tr, C), lambda l, i: (0, i * (1 - l), 0)), pl.BlockSpec((nb, tr, C), lambda l, i: (0, i * l, 0)),
                  row, row, row],
        out_specs=[row, row, row, row], out_shape=[sh, sh, sh, sh], name=name, compiler_params=_params("arbitrary", "arbitrary"),
    )(g0, g1, w, m, v)


def _peers():
    x, y, c = lax.axis_index("x"), lax.axis_index("y"), lax.axis_index("c")
    flips = ((0, 0, 1), (1, 0, 0), (0, 1, 0), (1, 1, 0), (1, 0, 1), (0, 1, 1), (1, 1, 1))
    me = 4 * x + 2 * y + c
    peers = [(x ^ fx, y ^ fy, c ^ fc) for fx, fy, fc in flips]
    return me, peers


def _exchange(arrs, scatter, name):
    n = len(arrs)
    npeer = N_DEV - 1

    def body(*refs):
        ins, outs = refs[:n], refs[n:2 * n]
        send_sems, recv_sems, local_sems = refs[2 * n:]
        me, peers = _peers()
        copies = []
        for a in range(n):
            src_own = ins[a].at[me] if scatter else ins[a]
            own = pltpu.make_async_copy(src_own, outs[a].at[me], local_sems.at[a])
            own.start()
            copies.append(own)
            for k, peer in enumerate(peers):
                src = ins[a].at[4 * peer[0] + 2 * peer[1] + peer[2]] if scatter else ins[a]
                cp = pltpu.make_async_remote_copy(
                    src_ref=src, dst_ref=outs[a].at[me], send_sem=send_sems.at[a * npeer + k],
                    recv_sem=recv_sems.at[a * npeer + k], device_id=peer, device_id_type=pl.DeviceIdType.MESH)
                cp.start()
                copies.append(cp)
        for cp in copies:
            cp.wait()

    out_shape = [jax.ShapeDtypeStruct(((N_DEV,) + a.shape[1:]) if scatter else ((N_DEV,) + a.shape), a.dtype) for a in arrs]
    anyspec = pl.BlockSpec(memory_space=pl.ANY)
    return pl.pallas_call(
        body, in_specs=[anyspec] * n, out_specs=[anyspec] * n, out_shape=out_shape,
        scratch_shapes=[pltpu.SemaphoreType.DMA((n * npeer,)), pltpu.SemaphoreType.DMA((n * npeer,)),
                        pltpu.SemaphoreType.DMA((n,))],
        name=name,
    )(*arrs)


def _dev_index(peer):
    return 4 * peer[0] + 2 * peer[1] + peer[2]


def _push_copies(src_refs, land_refs, send_sems, recv_sems, scatter, as_receiver):
    me, peers = _peers()
    npeer = N_DEV - 1
    copies = []
    for a in range(len(src_refs)):
        for k, peer in enumerate(peers):
            src = src_refs[a].at[_dev_index(peer)] if scatter else src_refs[a]
            slot = _dev_index(peer) if as_receiver else me
            copies.append(pltpu.make_async_remote_copy(
                src_ref=src, dst_ref=land_refs[a].at[slot], send_sem=send_sems.at[a * npeer + k],
                recv_sem=recv_sems.at[a * npeer + k], device_id=peer, device_id_type=pl.DeviceIdType.MESH))
    return copies


def _own_copies(src_refs, land_refs, own_sems, scatter):
    me, _ = _peers()
    return [pltpu.make_async_copy(src_refs[a].at[me] if scatter else src_refs[a], land_refs[a].at[me], own_sems.at[a])
            for a in range(len(src_refs))]


def _push_start(srcs, scatter, name):
    n = len(srcs)
    nsem = n * (N_DEV - 1)

    def body(*refs):
        srcs_r, lands_r = refs[:n], refs[n:2 * n]
        for cp in _push_copies(srcs_r, lands_r, refs[2 * n], refs[2 * n + 1], scatter, False):
            cp.start()
        for cp in _own_copies(srcs_r, lands_r, refs[2 * n + 2], scatter):
            cp.start()
        token = refs[-1]
        token[...] = jnp.zeros_like(token)

    hbm = pl.BlockSpec(memory_space=pltpu.HBM)
    sem = pl.BlockSpec(memory_space=pltpu.SEMAPHORE)
    lands = [lax.empty((N_DEV,) + (s.shape[1:] if scatter else s.shape), s.dtype) for s in srcs]
    arrs = list(srcs) + lands
    return pl.pallas_call(
        body, name=name,
        out_shape=(pltpu.SemaphoreType.DMA((nsem,)), pltpu.SemaphoreType.DMA((nsem,)), pltpu.SemaphoreType.DMA((n,)),
                   *[pltpu.HBM(a.shape, a.dtype) for a in arrs], jax.ShapeDtypeStruct((8, 128), F32)),
        in_specs=[hbm] * (2 * n), out_specs=(sem, sem, sem, *([hbm] * (2 * n)), pl.BlockSpec(memory_space=pltpu.VMEM)),
        input_output_aliases={i: 3 + i for i in range(2 * n)},
        compiler_params=pltpu.CompilerParams(has_side_effects=pltpu.SideEffectType.DATAFLOW_SIDE_EFFECTING),
    )(*[pltpu.with_memory_space_constraint(a, pltpu.HBM) for a in arrs])


def _push_wait(handle, after, scatter, name):
    send_sems, recv_sems, own_sems, *thru, _ = handle
    n = len(thru) // 2

    def body(*refs):
        srcs_r, lands_r = refs[:n], refs[n:2 * n]
        for cp in _push_copies(srcs_r, lands_r, refs[2 * n], refs[2 * n + 1], scatter, True):
            cp.wait_send()
            cp.wait_recv()
        for cp in _own_copies(srcs_r, lands_r, refs[2 * n + 2], scatter):
            cp.wait()

    hbm = pl.BlockSpec(memory_space=pltpu.HBM)
    sem = pl.BlockSpec(memory_space=pltpu.SEMAPHORE)
    outs = pl.pallas_call(
        body, name=name, out_shape=tuple(pltpu.HBM(a.shape, a.dtype) for a in thru),
        in_specs=[hbm] * (2 * n) + [sem, sem, sem, pl.BlockSpec(memory_space=pl.ANY)], out_specs=tuple([hbm] * (2 * n)),
        input_output_aliases={i: i for i in range(2 * n)},
        compiler_params=pltpu.CompilerParams(has_side_effects=pltpu.SideEffectType.DATAFLOW_SIDE_EFFECTING),
    )(*thru, send_sems, recv_sems, own_sems, after)
    return list(outs[n:])


def _tables(S):
    pos = jnp.arange(S, dtype=F32)
    inv = ROPE_THETA ** (-jnp.arange(0, ATT_HEAD_DIM, 2, dtype=F32) / ATT_HEAD_DIM)
    ang = pos[:, None] * inv[None, :]
    cos, sin = jnp.cos(ang), jnp.sin(ang)
    cos128 = jnp.tile(cos, (1, 4))
    sin128 = jnp.tile(jnp.concatenate([-sin, sin], axis=1), (1, 2))
    lane = np.arange(ATT_WIDTH)
    bd = jnp.asarray((lane[:, None] // 64 == lane[None, :] // 64).astype(np.float32), dtype=BF16)
    return cos128, sin128, bd, jnp.asarray(_att_bias())


def _layer_fwd(l, x, p, tabs, late=None):
    cos, sin, bd, bias = tabs
    S = x.shape[0]
    row = lambda v: v.reshape(1, -1)
    hn = _rmsnorm_fwd(x, row(p["ln_mix"]), f"norm_mix_fwd{l}")
    proj = _mm(hn, p["w_in"], "nn", f"in_proj{l}", tn=1152)
    xbc = _conv_fwd(proj, p["conv_w"], row(p["conv_b"]), f"conv_fwd{l}")
    dtr = proj[:, DT_COL:DT_COL + SSD_HEADS].reshape(S, SSD_GROUPS, HPG).transpose(1, 0, 2)
    grp = lambda v: v.reshape(SSD_GROUPS, 1, HPG)
    d_rep = row(jnp.repeat(p["d_skip"], SSD_HEAD_DIM))
    y, yraw, ssd_st = _ssd_fwd(xbc, proj, dtr, grp(p["dt_bias"]), grp(p["a_log"]), d_rep, row(p["ssd_norm"]), f"ssd_fwd{l}")
    qg = row(jnp.tile(p["q_norm"], ATT_HEADS))
    kg = row(jnp.tile(p["k_norm"], ATT_HEADS))
    aq, akp, avp = _att_prep_fwd(proj, qg, kg, cos, sin, bd, f"att_prep_fwd{l}")
    y = _att_fwd(aq, akp, avp, bias, y, f"att_fwd{l}")
    y, oraw, ret_st = _ret_fwd(proj, cos, sin, row(p["ret_norm"]), y, f"ret_fwd{l}")
    if late is not None:
        p.update(late(y))
    x1 = _mm(y, p["w_out"], "nn", f"out_proj{l}", residual=x)
    hn2 = _rmsnorm_fwd(x1, row(p["ln_ffn"]), f"norm_ffn_fwd{l}")
    g, u, act = _swiglu_fwd(hn2, p["w_gate"], p["w_up"], f"swiglu_fwd{l}")
    x2 = _mm(act, p["w_down"], "nn", f"down_proj{l}", residual=x1, tk=2816)
    saved = dict(x=x, hn=hn, proj=proj, xbc=xbc, dtr=dtr, yraw=yraw, ssd_st=ssd_st, aq=aq, akp=akp, avp=avp,
                 oraw=oraw, ret_st=ret_st, y=y, x1=x1, hn2=hn2, g=g, u=u, act=act, d_rep=d_rep, qg=qg, kg=kg)
    return x2, saved


def _layer_bwd(l, dx2, dx2_bf, p, sv, tabs, on_ffn=None, on_all=None):
    cos, sin, bd, bias = tabs
    S = dx2.shape[0]
    row = lambda v: v.reshape(1, -1)
    grp = lambda v: v.reshape(SSD_GROUPS, 1, HPG)
    gr = {}
    dg, du = _swiglu_bwd(dx2_bf, p["w_down"], sv["g"], sv["u"], f"swiglu_bwd{l}")
    gr["w_down"] = _mm(sv["act"], dx2_bf, "tn", f"down_wgrad{l}", out_dtype=BF16, tm=1408, tn=1024, tk=2048)
    dhn2 = _mm_nt2(dg, p["w_gate"], du, p["w_up"], f"ffn_dgrad{l}")
    gr["w_gate"] = _mm(sv["hn2"], dg, "tn", f"gate_wgrad{l}", out_dtype=BF16, tm=1024, tn=1408, tk=2048)
    gr["w_up"] = _mm(sv["hn2"], du, "tn", f"up_wgrad{l}", out_dtype=BF16, tm=1024, tn=1408, tk=2048)
    ffn_gain = row(p["ln_ffn"]) + (on_ffn(gr)[0, 0] if on_ffn is not None else 0.0)
    dx1, dx1_bf, dln_ffn = _rmsnorm_bwd(sv["x1"], dhn2, ffn_gain, dx2, f"norm_ffn_bwd{l}")
    gr["ln_ffn"] = dln_ffn[0]
    dy = _mm(dx1_bf, p["w_out"], "nt", f"out_dgrad{l}")
    gr["w_out"] = _mm(sv["y"], dx1_bf, "tn", f"out_wgrad{l}", out_dtype=BF16, tm=1024, tn=1024, tk=2048)
    dproj, dxs, dbm, dcm, ddtr, dbias, dalog, dd, dssd_gain = _ssd_bwd(
        dy, sv["yraw"], sv["xbc"], sv["proj"], sv["dtr"], grp(p["dt_bias"]), grp(p["a_log"]), sv["d_rep"],
        row(p["ssd_norm"]), sv["ssd_st"], f"ssd_bwd{l}")
    gr["dt_bias"], gr["a_log"], gr["d_skip"] = dbias.reshape(-1), dalog.reshape(-1), dd.reshape(-1)
    gr["ssd_norm"] = dssd_gain[0]
    dproj, dconv_w, dconv_b = _conv_bwd(dxs, dbm, dcm, sv["proj"], p["conv_w"], row(p["conv_b"]), dproj, f"conv_bwd{l}")
    gr["conv_w"], gr["conv_b"] = dconv_w, dconv_b[0]
    dq, dk_p, dv_p = _att_bwd(sv["aq"], sv["akp"], sv["avp"], bias, dy, f"att_bwd{l}")
    dproj, dqg, dkg = _att_prep_bwd(sv["proj"], dq, dk_p, dv_p, sv["qg"], sv["kg"], cos, sin, bd, dproj, f"att_prep_bwd{l}")
    gr["q_norm"], gr["k_norm"] = dqg[0], dkg[0]
    dproj, dret_gain = _ret_bwd(dy, sv["oraw"], sv["proj"], cos, sin, row(p["ret_norm"]), sv["ret_st"], dproj, f"ret_bwd{l}")
    gr["ret_norm"] = dret_gain[0]
    ddt_cols = ddtr.transpose(1, 0, 2).reshape(S, SSD_HEADS).astype(BF16)
    dproj = lax.dynamic_update_slice(dproj, jnp.pad(ddt_cols, ((0, 0), (0, IN_PAD - DT_COL - SSD_HEADS))), (0, DT_COL))
    gr["w_in"] = _mm(sv["hn"], dproj, "tn", f"in_wgrad{l}", out_dtype=BF16, tm=1024, tn=1152, tk=2048)
    launched = on_all(gr) if on_all is not None else None
    dhn = _mm(dproj, p["w_in"], "nt", f"in_dgrad{l}", tk=1920, after=launched)
    dx0, dx0_bf, dln_mix = _rmsnorm_bwd(sv["x"], dhn, row(p["ln_mix"]), dx1, f"norm_mix_bwd{l}")
    gr["ln_mix"] = dln_mix[0]
    return dx0, dx0_bf, gr


def _local_step(x, tgt, layers, late=None, on_ffn=None, on_all=None):
    n = len(layers)
    none = [None] * n
    late, on_ffn, on_all = late or none, on_ffn or none, on_all or none
    tabs = _tables(x.shape[0])
    saved, params = [], []
    h = x
    for l in range(n):
        p = dict(layers[l](h) if callable(layers[l]) else layers[l])
        h, sv = _layer_fwd(l, h, p, tabs, late[l])
        saved.append(sv)
        params.append(p)
    dh, dh_bf, lacc = _loss_grad(h, tgt, "loss_grad")
    grads = [None] * n
    for l in reversed(range(n)):
        dh, dh_bf, grads[l] = _layer_bwd(l, dh, dh_bf, params[l], saved[l], tabs, on_ffn[l], on_all[l])
    return lacc[0, 0], dh, grads


BIG = ("w_in", "w_out", "w_gate", "w_up", "w_down")
SMALL = ("ln_mix", "conv_b", "dt_bias", "a_log", "d_skip", "ssd_norm", "q_norm", "k_norm", "ret_norm", "ln_ffn")
ORDER = ("ln_mix", "w_in", "conv_w", "conv_b", "dt_bias", "a_log", "d_skip", "ssd_norm", "q_norm", "k_norm", "ret_norm",
         "w_out", "ln_ffn", "w_gate", "w_up", "w_down")


COL_SHARDED = ("w_in", "w_gate", "w_up", "conv_w")


IN_GROUPS = ((ORIG_Z_XBC, Z_COL), (ORIG_DT, DT_COL), (ORIG_ATT, ATT_COL), (ORIG_RET, RET_COL))


def _full_weight(k, gathered):
    if k == "w_in":
        cs = gathered.shape[2]
        pieces = []
        for (lo, hi), _ in sorted(IN_GROUPS, key=lambda grp: grp[1]):
            for j in range(N_DEV):
                a, b = max(lo, j * cs), min(hi, (j + 1) * cs)
                if a < b:
                    pieces.append(gathered[j][:, a - j * cs:b - j * cs])
        pieces.append(jnp.zeros((gathered.shape[1], IN_PAD - IN_WIDTH), gathered.dtype))
        return jnp.concatenate(pieces, axis=1)
    if k in COL_SHARDED:
        return gathered.transpose(1, 0, 2).reshape(gathered.shape[1], -1)
    return gathered.reshape(-1, gathered.shape[2])


def _shard_block(k, g):
    if k == "w_in":
        cs = IN_WIDTH // N_DEV
        blocks = []
        for j in range(N_DEV):
            pieces = []
            for (lo, hi), col in IN_GROUPS:
                a, b = max(lo, j * cs), min(hi, (j + 1) * cs)
                if a < b:
                    pieces.append(g[:, col + a - lo:col + b - lo])
            blocks.append(jnp.concatenate(pieces, axis=1))
        return jnp.stack(blocks)
    if k in COL_SHARDED:
        return g.reshape(g.shape[0], N_DEV, -1).transpose(1, 0, 2)
    return g.reshape(N_DEV, -1, g.shape[1])


def kernel(x, ln_mix, w_in, conv_w, conv_b, dt_bias, a_log, d_skip, ssd_norm, q_norm, k_norm, ret_norm, w_out, ln_ffn, w_gate, w_up, w_down, loss_target, m_ln_mix, m_w_in, m_conv_w, m_conv_b, m_dt_bias, m_a_log, m_d_skip, m_ssd_norm, m_q_norm, m_k_norm, m_ret_norm, m_w_out, m_ln_ffn, m_w_gate, m_w_up, m_w_down, v_ln_mix, v_w_in, v_conv_w, v_conv_b, v_dt_bias, v_a_log, v_d_skip, v_ssd_norm, v_q_norm, v_k_norm, v_ret_norm, v_w_out, v_ln_ffn, v_w_gate, v_w_up, v_w_down):
    w = dict(ln_mix=ln_mix, w_in=w_in, conv_w=conv_w, conv_b=conv_b, dt_bias=dt_bias, a_log=a_log, d_skip=d_skip,
             ssd_norm=ssd_norm, q_norm=q_norm, k_norm=k_norm, ret_norm=ret_norm, w_out=w_out, ln_ffn=ln_ffn,
             w_gate=w_gate, w_up=w_up, w_down=w_down)
    m = dict(ln_mix=m_ln_mix, w_in=m_w_in, conv_w=m_conv_w, conv_b=m_conv_b, dt_bias=m_dt_bias, a_log=m_a_log,
             d_skip=m_d_skip, ssd_norm=m_ssd_norm, q_norm=m_q_norm, k_norm=m_k_norm, ret_norm=m_ret_norm, w_out=m_w_out,
             ln_ffn=m_ln_ffn, w_gate=m_w_gate, w_up=m_w_up, w_down=m_w_down)
    v = dict(ln_mix=v_ln_mix, w_in=v_w_in, conv_w=v_conv_w, conv_b=v_conv_b, dt_bias=v_dt_bias, a_log=v_a_log,
             d_skip=v_d_skip, ssd_norm=v_ssd_norm, q_norm=v_q_norm, k_norm=v_k_norm, ret_norm=v_ret_norm, w_out=v_w_out,
             ln_ffn=v_ln_ffn, w_gate=v_w_gate, w_up=v_w_up, w_down=v_w_down)
    me = 4 * lax.axis_index("x") + 2 * lax.axis_index("y") + lax.axis_index("c")

    late_names = ("w_out", "w_gate", "w_up", "w_down")
    waves = {"a": [("w_in", 0), ("conv_w", 0), ("conv_w", 1)], "b": [(k, 0) for k in late_names],
             "c": [("w_in", 1)], "d": [(k, 1) for k in late_names]}
    gather = {}
    behind = 0.0
    for tag, items in waves.items():
        srcs = [w[k][l] if k == "conv_w" else (w[k][l] + behind).astype(BF16) for k, l in items]
        gather[tag] = _push_start(srcs, False, f"gather_{tag}_start")
        behind = gather[tag][-1][0, 0]
    started = behind
    full = {}

    def arrive(tag, after):
        for (k, l), g in zip(waves[tag], _push_wait(gather[tag], after, False, f"gather_{tag}_wait")):
            full[k, l] = _full_weight(k, g)

    def layer_weights(l, names):
        return {k: full[k, l] for k in names}

    def small_weights(l):
        return {k: w[k][l] for k in SMALL}

    def layer0(h):
        arrive("a", gather["d"][-1])
        p = small_weights(0)
        p["ln_mix"] = p["ln_mix"] + started
        return {**p, **layer_weights(0, ("w_in", "conv_w"))}

    def late0(y):
        arrive("b", y)
        return layer_weights(0, late_names)

    def layer1(h):
        arrive("c", h)
        return {**small_weights(1), **layer_weights(1, ("w_in", "conv_w"))}

    def late1(y):
        arrive("d", y)
        return layer_weights(1, late_names)

    groups = {"1": [(k, 1) for k in BIG], "0a": [(k, 0) for k in ("w_down", "w_gate", "w_up")],
              "0b": [(k, 0) for k in ("w_out", "w_in")]}
    scatter = {}

    def push_grads(tag, gr):
        blocks = [_shard_block(k, gr[k]) for k, _ in groups[tag]]
        scatter[tag] = _push_start(blocks, True, f"scatter_{tag}_start")
        return scatter[tag][-1]

    loss_part, gx, grads = _local_step(
        x[0], loss_target[0], [layer0, layer1], late=[late0, late1],
        on_ffn=[functools.partial(push_grads, "0a"), None],
        on_all=[functools.partial(push_grads, "0b"), functools.partial(push_grads, "1")])
    loss = lax.psum(loss_part, MESH_AXES)

    out = {}
    recv = {}
    for tag, items in groups.items():
        for item, r in zip(items, _push_wait(scatter[tag], gx, True, f"scatter_{tag}_wait")):
            recv[item] = r
    for k in BIG:
        out[k] = _adamw_layers(recv[k, 0], recv[k, 1], w[k], m[k], v[k], f"adamw_{k}")
    names = SMALL + ("conv_w",)
    sizes = [int(np.prod(grads[0][k].shape)) for k in names]
    packed = jnp.concatenate([jnp.stack([grads[l][k] for l in range(DEPTH)]).reshape(-1) for k in names])
    n_small = packed.shape[0]
    rows_small = -(-n_small // 1024) * 8
    pad = lambda t, fill: jnp.concatenate([t, jnp.full((rows_small * 128 - n_small,), fill, F32)]).reshape(rows_small, 128)
    parts = _exchange([pad(packed, 0.0)], False, "gather_small_grads")[0]
    n_rep = DEPTH * sum(sizes[:-1])
    pack_rep = lambda d, fill: pad(jnp.concatenate([d[k].reshape(-1) for k in SMALL]
                                                   + [jnp.full((n_small - n_rep,), fill, F32)]), fill)
    res = _adamw(parts, pack_rep(w, 1.0), pack_rep(m, 1.0), pack_rep(v, 1.0), "adamw_small")
    res = [t.reshape(-1) for t in res]
    off = 0
    for k, sz in zip(SMALL, sizes[:-1]):
        out[k] = [t[off:off + DEPTH * sz].reshape(w[k].shape) for t in res]
        off += DEPTH * sz
    gconv = res[0][off:off + DEPTH * sizes[-1]].reshape(DEPTH, SSD_CONV, SSD_CONV_CH)
    gconv = lax.dynamic_slice_in_dim(gconv, me * conv_w.shape[2], conv_w.shape[2], axis=2)
    flat = lambda t: t.reshape(8, -1)
    resc = _adamw(flat(gconv)[None], flat(conv_w), flat(m_conv_w), flat(v_conv_w), "adamw_conv_w")
    out["conv_w"] = [t.reshape(conv_w.shape) for t in resc]

    return (loss, gx[None], *[out[k][0] for k in ORDER], *[out[k][1] for k in ORDER],
            *[out[k][2] for k in ORDER], *[out[k][3] for k in ORDER])
```

```python
import functools
import math

import jax
import jax.numpy as jnp
import numpy as np
from jax import lax
from jax.experimental import pallas as pl
from jax.experimental.pallas import tpu as pltpu

F32 = jnp.float32
BF16 = jnp.bfloat16

N_DEV = 8
MESH_AXES = ("x", "y", "c")
D_MODEL = 2048
DEPTH = 2
EPS = 1e-6
ROPE_THETA = 10000.0
SSD_HEADS = 16
SSD_HEAD_DIM = 64
SSD_WIDTH = 1024
SSD_GROUPS = 2
SSD_STATE = 128
SSD_CONV = 4
SSD_CONV_CH = 1536
ATT_HEADS = 8
ATT_HEAD_DIM = 64
ATT_WIDTH = 512
DILATED_PAIRS = ((128, 1), (512, 4), (2048, 16))
RET_HEADS = 4
RET_QK_DIM = 64
RET_V_DIM = 128
RET_QK_WIDTH = 256
RET_V_WIDTH = 512
CHUNK = 128
MIX_WIDTH = 2048
ATT_SPAN = 2048
ATT_STRIP = ATT_SPAN + CHUNK
ATT_QB = 4
IN_WIDTH = 5648
IN_PAD = 5760
RET_COL, ATT_COL, Z_COL, XBC_COL, DT_COL = 0, 1536, 3072, 4096, 5632
ORIG_Z_XBC, ORIG_DT, ORIG_ATT, ORIG_RET = (0, 2560), (2560, 2576), (2576, 4112), (4112, 5648)
D_FF = 5632
ADAM_LR = 0.001
ADAM_B1 = 0.9
ADAM_B2 = 0.999
ADAM_EPS = 1e-08
ADAM_WD = 0.01
ADAM_STEP = 10
NEG = -1e30
VMEM_LIMIT_V7X = 60 * 1024 * 1024

NN = (((1,), (0,)), ((), ()))
NT = (((1,), (1,)), ((), ()))
TN = (((0,), (0,)), ((), ()))


def _bdot(a, b, dims):
    return lax.dot_general(a.astype(BF16), b.astype(BF16), dims, preferred_element_type=F32)


def _xdot(a, b, dims, exact_first=False):
    ones, x = (a, b) if exact_first else (b, a)
    ones = ones.astype(BF16)
    acc, rest = None, x
    for _ in range(3):
        piece = rest.astype(BF16)
        rest = rest - piece.astype(F32)
        part = lax.dot_general(*((ones, piece) if exact_first else (piece, ones)), dims, preferred_element_type=F32)
        acc = part if acc is None else acc + part
    return acc


def _params(*sem):
    return pltpu.CompilerParams(dimension_semantics=sem, vmem_limit_bytes=VMEM_LIMIT_V7X)


def _sigmoid(v):
    return 1.0 / (1.0 + jnp.exp(-v))


def _silu_grad(v, s):
    return s * (1.0 + v * (1.0 - s))


def _rmsnorm_fwd(x, g, name):
    S, D = x.shape
    tr = min(512, S)

    def body(x_ref, g_ref, o_ref):
        xv = x_ref[...]
        r = lax.rsqrt(jnp.mean(xv * xv, axis=-1, keepdims=True) + EPS)
        o_ref[...] = (xv * r * g_ref[...]).astype(o_ref.dtype)

    return pl.pallas_call(
        body, grid=(S // tr,),
        in_specs=[pl.BlockSpec((tr, D), lambda i: (i, 0)), pl.BlockSpec((1, D), lambda i: (0, 0))],
        out_specs=pl.BlockSpec((tr, D), lambda i: (i, 0)),
        out_shape=jax.ShapeDtypeStruct((S, D), BF16), name=name, compiler_params=_params("parallel"),
    )(x, g)


def _rmsnorm_bwd(x, dy, g, dres, name):
    S, D = x.shape
    tr = min(512, S)

    def body(x_ref, dy_ref, g_ref, dres_ref, dx_ref, dxb_ref, dg_ref):
        i = pl.program_id(0)
        xv = x_ref[...]
        r = lax.rsqrt(jnp.mean(xv * xv, axis=-1, keepdims=True) + EPS)
        n = xv * r
        dy = dy_ref[...]
        dn = dy * g_ref[...]
        dx = dres_ref[...] + r * (dn - n * jnp.mean(dn * n, axis=-1, keepdims=True))
        dx_ref[...] = dx
        dxb_ref[...] = dx.astype(BF16)
        part = jnp.sum(dy * n, axis=0, keepdims=True)

        @pl.when(i == 0)
        def _():
            dg_ref[...] = part

        @pl.when(i > 0)
        def _():
            dg_ref[...] += part

    row = pl.BlockSpec((tr, D), lambda i: (i, 0))
    vec = pl.BlockSpec((1, D), lambda i: (0, 0))
    return pl.pallas_call(
        body, grid=(S // tr,), in_specs=[row, row, vec, row], out_specs=[row, row, vec],
        out_shape=[jax.ShapeDtypeStruct((S, D), F32), jax.ShapeDtypeStruct((S, D), BF16), jax.ShapeDtypeStruct((1, D), F32)],
        name=name, compiler_params=_params("arbitrary"),
    )(x, dy, g, dres)


def _loss_grad(y, tgt, name):
    S, D = y.shape
    tr = min(512, S)

    def body(y_ref, t_ref, dy_ref, dyb_ref, l_ref):
        i = pl.program_id(0)
        err = y_ref[...] - t_ref[...]
        dy = err * (1.0 / D)
        dy_ref[...] = dy
        dyb_ref[...] = dy.astype(BF16)
        part = jnp.sum(jnp.sum(err * err, axis=1, keepdims=True), axis=0, keepdims=True) * (0.5 / D)

        @pl.when(i == 0)
        def _():
            l_ref[...] = jnp.zeros_like(l_ref)

        l_ref[...] += part

    row = pl.BlockSpec((tr, D), lambda i: (i, 0))
    return pl.pallas_call(
        body, grid=(S // tr,), in_specs=[row, row],
        out_specs=[row, row, pl.BlockSpec((8, 128), lambda i: (0, 0))],
        out_shape=[jax.ShapeDtypeStruct((S, D), F32), jax.ShapeDtypeStruct((S, D), BF16), jax.ShapeDtypeStruct((8, 128), F32)],
        name=name, compiler_params=_params("arbitrary"),
    )(y, tgt)


def _pick(n, cands):
    for c in cands:
        if n % c == 0:
            return c
    return n


def _mm(a, b, mode, name, out_dtype=F32, residual=None, tm=None, tn=None, tk=None, after=None):
    if mode == "nn":
        (M, K), (_, N) = a.shape, b.shape
    elif mode == "nt":
        (M, K), (N, _) = a.shape, b.shape
    else:
        (K, M), (_, N) = a.shape, b.shape
    tm = min(tm, M) if tm else _pick(M, (1024, 512, 256, 128))
    tn = min(tn, N) if tn else _pick(N, (1024, 1152, 1408, 512, 256, 128))
    tk = min(tk, K) if tk else _pick(K, (2048, 1920, 1408, 1024, 512, 256, 128))
    assert M % tm == 0 and N % tn == 0 and K % tk == 0, (name, M, N, K, tm, tn, tk)
    nk = K // tk
    a_spec = pl.BlockSpec((tk, tm), lambda i, j, k: (k, i)) if mode == "tn" else pl.BlockSpec((tm, tk), lambda i, j, k: (i, k))
    b_spec = pl.BlockSpec((tn, tk), lambda i, j, k: (j, k)) if mode == "nt" else pl.BlockSpec((tk, tn), lambda i, j, k: (k, j))
    o_spec = pl.BlockSpec((tm, tn), lambda i, j, k: (i, j))
    dims = {"nn": NN, "nt": NT, "tn": TN}[mode]
    has_res = residual is not None

    has_after = after is not None

    def body(*refs):
        a_ref, b_ref = refs[0], refs[1]
        r_ref = refs[2] if has_res else None
        o_ref = refs[2 + has_res + has_after]
        p = _bdot(a_ref[...], b_ref[...], dims)

        def finish(acc):
            if has_res:
                acc = acc + r_ref[...]
            o_ref[...] = acc.astype(o_ref.dtype)

        if nk == 1:
            finish(p)
        else:
            acc_ref = refs[-1]
            k = pl.program_id(2)

            @pl.when(k == 0)
            def _():
                acc_ref[...] = p

            @pl.when(k > 0)
            def _():
                acc_ref[...] += p

            @pl.when(k == nk - 1)
            def _():
                finish(acc_ref[...])

    ins = [a, b] + ([residual] if has_res else []) + ([after] if has_after else [])
    in_specs = [a_spec, b_spec] + ([o_spec] if has_res else []) + ([pl.BlockSpec(memory_space=pl.ANY)] if has_after else [])
    scratch = [pltpu.VMEM((tm, tn), F32)] if nk > 1 else []
    return pl.pallas_call(
        body, grid=(M // tm, N // tn, nk), in_specs=in_specs, out_specs=o_spec,
        out_shape=jax.ShapeDtypeStruct((M, N), out_dtype), scratch_shapes=scratch, name=name,
        compiler_params=_params("parallel", "parallel", "arbitrary"),
    )(*ins)


def _accumulate(acc_ref, p, k, nk, finish):
    @pl.when(k == 0)
    def _():
        acc_ref[...] = p

    @pl.when(k > 0)
    def _():
        acc_ref[...] += p

    @pl.when(k == nk - 1)
    def _():
        finish(acc_ref[...])


def _swiglu_fwd(hn, wg, wu, name):
    S, K = hn.shape
    F = wg.shape[1]
    tm = _pick(S, (1024, 512))
    tn = _pick(F, (512, 256, 128))

    def body(a_ref, wg_ref, wu_ref, g_ref, u_ref, act_ref):
        a = a_ref[...]
        g = _bdot(a, wg_ref[...], NN)
        u = _bdot(a, wu_ref[...], NN)
        g_ref[...] = g.astype(BF16)
        u_ref[...] = u.astype(BF16)
        act_ref[...] = (g * _sigmoid(g) * u).astype(BF16)

    w_spec = pl.BlockSpec((K, tn), lambda i, j: (0, j))
    o_spec = pl.BlockSpec((tm, tn), lambda i, j: (i, j))
    sh = jax.ShapeDtypeStruct((S, F), BF16)
    return pl.pallas_call(
        body, grid=(S // tm, F // tn), in_specs=[pl.BlockSpec((tm, K), lambda i, j: (i, 0)), w_spec, w_spec],
        out_specs=[o_spec, o_spec, o_spec], out_shape=[sh, sh, sh], name=name,
        compiler_params=_params("parallel", "parallel"),
    )(hn, wg, wu)


def _swiglu_bwd(dx, wd, g, u, name):
    S, K = dx.shape
    F = wd.shape[0]
    tm = _pick(S, (1024, 512))
    tn = _pick(F, (512, 256, 128))

    def body(dx_ref, wd_ref, g_ref, u_ref, dg_ref, du_ref):
        da = _bdot(dx_ref[...], wd_ref[...], NT)
        gv = g_ref[...].astype(F32)
        uv = u_ref[...].astype(F32)
        s = _sigmoid(gv)
        dg_ref[...] = (da * uv * _silu_grad(gv, s)).astype(BF16)
        du_ref[...] = (da * gv * s).astype(BF16)

    o_spec = pl.BlockSpec((tm, tn), lambda i, j: (i, j))
    sh = jax.ShapeDtypeStruct((S, F), BF16)
    return pl.pallas_call(
        body, grid=(S // tm, F // tn),
        in_specs=[pl.BlockSpec((tm, K), lambda i, j: (i, 0)), pl.BlockSpec((tn, K), lambda i, j: (j, 0)), o_spec, o_spec],
        out_specs=[o_spec, o_spec], out_shape=[sh, sh], name=name, compiler_params=_params("parallel", "parallel"),
    )(dx, wd, g, u)


def _mm_nt2(a1, b1, a2, b2, name):
    M, K = a1.shape
    N = b1.shape[0]
    tm = _pick(M, (1024, 512))
    tn = _pick(N, (1024, 512))
    tk = _pick(K, (1408, 1024, 512, 256, 128))
    nk = K // tk

    def body(a1_ref, b1_ref, a2_ref, b2_ref, o_ref, acc_ref):
        def finish(acc):
            o_ref[...] = acc

        p = _bdot(a1_ref[...], b1_ref[...], NT) + _bdot(a2_ref[...], b2_ref[...], NT)
        _accumulate(acc_ref, p, pl.program_id(2), nk, finish)

    a_spec = pl.BlockSpec((tm, tk), lambda i, j, k: (i, k))
    b_spec = pl.BlockSpec((tn, tk), lambda i, j, k: (j, k))
    return pl.pallas_call(
        body, grid=(M // tm, N // tn, nk), in_specs=[a_spec, b_spec, a_spec, b_spec],
        out_specs=pl.BlockSpec((tm, tn), lambda i, j, k: (i, j)), out_shape=jax.ShapeDtypeStruct((M, N), F32),
        scratch_shapes=[pltpu.VMEM((tm, tn), F32)], name=name,
        compiler_params=_params("parallel", "parallel", "arbitrary"),
    )(a1, b1, a2, b2)


XBC_BLK0 = XBC_COL // 128


def _conv_fwd(proj, w, b, name):
    S = proj.shape[0]
    T = min(512, S)

    def body(x_ref, w_ref, b_ref, o_ref, xp_ref):
        xp_ref[pl.ds(0, 8), :] = jnp.zeros((8, 128), F32)
        xp_ref[pl.ds(8, S), :] = x_ref[...]
        wv = w_ref[...]
        bv = b_ref[...]

        def step(c, carry):
            base = pl.multiple_of(c * T, T)
            acc = wv[0:1] * xp_ref[pl.ds(base + 5, T), :]
            for i in range(1, SSD_CONV):
                acc = acc + wv[i:i + 1] * xp_ref[pl.ds(base + 5 + i, T), :]
            acc = bv + acc
            o_ref[pl.ds(base, T), :] = acc * _sigmoid(acc)
            return carry

        lax.fori_loop(0, S // T, step, 0)

    return pl.pallas_call(
        body, grid=(SSD_CONV_CH // 128,),
        in_specs=[pl.BlockSpec((S, 128), lambda j: (0, XBC_BLK0 + j)), pl.BlockSpec((SSD_CONV, 128), lambda j: (0, j)),
                  pl.BlockSpec((1, 128), lambda j: (0, j))],
        out_specs=pl.BlockSpec((S, 128), lambda j: (0, j)),
        out_shape=jax.ShapeDtypeStruct((S, SSD_CONV_CH), F32),
        scratch_shapes=[pltpu.VMEM((S + 8, 128), F32)], name=name, compiler_params=_params("parallel"),
    )(proj, w, b)


def _conv_bwd(dxs, dbm, dcm, proj, w, b, dproj, name):
    S = proj.shape[0]
    T = min(512, S)
    NX, NB = SSD_WIDTH // 128, SSD_GROUPS * SSD_STATE // 128

    def body(dxs_ref, dbm_ref, dcm_ref, x_ref, w_ref, b_ref, dproj_ref, dx_ref, dw_ref, db_ref, xp_ref, dcp_ref):
        j = pl.program_id(0)

        @pl.when(j < NX)
        def _():
            dcp_ref[pl.ds(0, S), :] = dxs_ref[...]

        @pl.when((j >= NX) & (j < NX + NB))
        def _():
            dcp_ref[pl.ds(0, S), :] = dbm_ref[...]

        @pl.when(j >= NX + NB)
        def _():
            dcp_ref[pl.ds(0, S), :] = dcm_ref[...]

        da_ref = dcp_ref
        xp_ref[pl.ds(0, 8), :] = jnp.zeros((8, 128), F32)
        xp_ref[pl.ds(8, S), :] = x_ref[...]
        dcp_ref[pl.ds(S, 8), :] = jnp.zeros((8, 128), F32)
        wv = w_ref[...]
        bv = b_ref[...]

        def step1(c, carry):
            base = pl.multiple_of(c * T, T)
            xs = [xp_ref[pl.ds(base + 5 + i, T), :] for i in range(SSD_CONV)]
            acc = wv[0:1] * xs[0]
            for i in range(1, SSD_CONV):
                acc = acc + wv[i:i + 1] * xs[i]
            acc = bv + acc
            s = _sigmoid(acc)
            dc = da_ref[pl.ds(base, T), :] * _silu_grad(acc, s)
            dcp_ref[pl.ds(base, T), :] = dc
            new = tuple(carry[i] + jnp.sum(xs[i] * dc, axis=0, keepdims=True) for i in range(SSD_CONV))
            return new + (carry[SSD_CONV] + jnp.sum(dc, axis=0, keepdims=True),)

        z = jnp.zeros((1, 128), F32)
        res = lax.fori_loop(0, S // T, step1, (z,) * (SSD_CONV + 1))
        for i in range(SSD_CONV):
            dw_ref[pl.ds(i, 1), :] = res[i]
        db_ref[...] = res[SSD_CONV]

        def step2(c, carry):
            base = pl.multiple_of(c * T, T)
            acc = wv[0:1] * dcp_ref[pl.ds(base + 3, T), :]
            for i in range(1, SSD_CONV):
                acc = acc + wv[i:i + 1] * dcp_ref[pl.ds(base + 3 - i, T), :]
            dx_ref[pl.ds(base, T), :] = acc.astype(dx_ref.dtype)
            return carry

        lax.fori_loop(0, S // T, step2, 0)

    clamp = lambda j, lo, n: jnp.clip(j - lo, 0, n - 1)
    return pl.pallas_call(
        body, grid=(SSD_CONV_CH // 128,),
        in_specs=[pl.BlockSpec((S, 128), lambda j: (0, clamp(j, 0, NX))), pl.BlockSpec((S, 128), lambda j: (0, clamp(j, NX, NB))),
                  pl.BlockSpec((S, 128), lambda j: (0, clamp(j, NX + NB, NB))),
                  pl.BlockSpec((S, 128), lambda j: (0, XBC_BLK0 + j)), pl.BlockSpec((SSD_CONV, 128), lambda j: (0, j)),
                  pl.BlockSpec((1, 128), lambda j: (0, j)), pl.BlockSpec(memory_space=pl.ANY)],
        out_specs=[pl.BlockSpec((S, 128), lambda j: (0, XBC_BLK0 + j)), pl.BlockSpec((SSD_CONV, 128), lambda j: (0, j)),
                   pl.BlockSpec((1, 128), lambda j: (0, j))],
        out_shape=[jax.ShapeDtypeStruct(dproj.shape, dproj.dtype), jax.ShapeDtypeStruct((SSD_CONV, SSD_CONV_CH), F32),
                   jax.ShapeDtypeStruct((1, SSD_CONV_CH), F32)],
        input_output_aliases={6: 0},
        scratch_shapes=[pltpu.VMEM((S + 8, 128), F32), pltpu.VMEM((S + 8, 128), F32)], name=name,
        compiler_params=_params("arbitrary"),
    )(dxs, dbm, dcm, proj, w, b, dproj)


HPG = SSD_HEADS // SSD_GROUPS
GW = HPG * SSD_HEAD_DIM


def _ssd_chunk_terms(dtr, bias, alog, tril, triu):
    pre = dtr + bias
    dt = jnp.maximum(pre, 0.0) + jnp.log(1.0 + jnp.exp(-jnp.abs(pre)))
    a_neg = -jnp.exp(alog)
    a = dt * a_neg
    acum = _xdot(tril, a, NN, exact_first=True)
    acum_t = _xdot(a, triu, TN)
    return pre, dt, a_neg, acum, acum_t


def _head_expanders():
    h64 = lax.broadcasted_iota(jnp.int32, (HPG, GW), 0) == lax.broadcasted_iota(jnp.int32, (HPG, GW), 1) // SSD_HEAD_DIM
    h128 = lax.broadcasted_iota(jnp.int32, (HPG, HPG * CHUNK), 0) == lax.broadcasted_iota(jnp.int32, (HPG, HPG * CHUNK), 1) // CHUNK
    return h64.astype(F32), h128.astype(F32)


def _ssd_fwd(xbc, proj, dtr, dt_bias, a_log, d_rep, gain, name):
    S = xbc.shape[0]
    L = CHUNK
    T = min(512, S)
    CPS = T // L
    NC = S // L

    def body(x_ref, b_ref, c_ref, z_ref, dtr_ref, bias_ref, alog_ref, d_ref, gain_ref, y_ref, yraw_ref, st_ref, state):
        i = pl.program_id(1)

        @pl.when(i == 0)
        def _():
            state[...] = jnp.zeros_like(state)

        row = lax.broadcasted_iota(jnp.int32, (L, L), 0)
        col = lax.broadcasted_iota(jnp.int32, (L, L), 1)
        causal = row >= col
        tril = causal.astype(F32)
        triu = (row <= col).astype(F32)
        low = col < SSD_HEAD_DIM
        e64, e128 = _head_expanders()
        for c in range(CPS):
            rows = pl.ds(c * L, L)
            xv = x_ref[rows, :]
            bm = b_ref[rows, :]
            cm = c_ref[rows, :]
            _, dt, _, acum, acum_t = _ssd_chunk_terms(dtr_ref[rows, :], bias_ref[...], alog_ref[...], tril, triu)
            ac = _xdot(acum, e64, NN)
            ac_sq = _xdot(acum, e128, NN)
            xd = xv * _xdot(dt, e64, NN)
            ac_last = ac[L - 1:L, :]
            sp = state[...]
            st_ref[c] = sp
            yoff = _bdot(cm, sp, NN) * jnp.exp(ac)
            state[...] = sp * jnp.exp(ac_last) + _bdot(bm, xd * jnp.exp(ac_last - ac), TN)
            gmat = _bdot(cm, bm, NT)
            for q in range(HPG // 2):
                pair = slice(q * 128, (q + 1) * 128)
                tile = xd[:, pair]
                y = yoff[:, pair]
                for j, keep in ((2 * q, low), (2 * q + 1, ~low)):
                    lam = jnp.exp(jnp.where(causal, ac_sq[:, j * L:(j + 1) * L] - acum_t[j:j + 1, :], NEG))
                    y = y + _bdot(gmat * lam, jnp.where(keep, tile, 0.0), NN)
                yraw_ref[rows, pair] = y
            zz = z_ref[rows, :]
            u = (yraw_ref[rows, :] + xv * d_ref[...]) * (zz * _sigmoid(zz))
            r = lax.rsqrt(jnp.mean(u * u, axis=-1, keepdims=True) + EPS)
            y_ref[rows, :] = (u * r * gain_ref[...]).astype(y_ref.dtype)

    vec8 = pl.BlockSpec((None, 1, HPG), lambda g, i: (g, 0, 0))
    return pl.pallas_call(
        body, grid=(SSD_GROUPS, S // T),
        in_specs=[pl.BlockSpec((T, GW), lambda g, i: (i, g)),
                  pl.BlockSpec((T, SSD_STATE), lambda g, i: (i, SSD_WIDTH // SSD_STATE + g)),
                  pl.BlockSpec((T, SSD_STATE), lambda g, i: (i, SSD_WIDTH // SSD_STATE + SSD_GROUPS + g)),
                  pl.BlockSpec((T, GW), lambda g, i: (i, Z_COL // GW + g)),
                  pl.BlockSpec((None, T, HPG), lambda g, i: (g, i, 0)),
                  vec8, vec8,
                  pl.BlockSpec((1, GW), lambda g, i: (0, g)), pl.BlockSpec((1, GW), lambda g, i: (0, g))],
        out_specs=[pl.BlockSpec((T, GW), lambda g, i: (i, g)), pl.BlockSpec((T, GW), lambda g, i: (i, g)),
                   pl.BlockSpec((CPS, None, SSD_STATE, GW), lambda g, i: (i, g, 0, 0))],
        out_shape=[jax.ShapeDtypeStruct((S, MIX_WIDTH), BF16), jax.ShapeDtypeStruct((S, SSD_WIDTH), F32),
                   jax.ShapeDtypeStruct((NC, SSD_GROUPS, SSD_STATE, GW), F32)],
        scratch_shapes=[pltpu.VMEM((SSD_STATE, GW), F32)], name=name,
        compiler_params=_params("arbitrary", "arbitrary"),
    )(xbc, xbc, xbc, proj, dtr, dt_bias, a_log, d_rep, gain)


def _ssd_bwd(dy, yraw, xbc, proj, dtr, dt_bias, a_log, d_rep, gain, states, name):
    S = xbc.shape[0]
    L = CHUNK
    T = min(512, S)
    CPS = T // L
    NI = S // T

    def body(dy_ref, yraw_ref, x_ref, b_ref, c_ref, z_ref, dtr_ref, bias_ref, alog_ref, d_ref, gain_ref, st_ref,
             dz_ref, dx_ref, db_ref, dc_ref, ddtr_ref, dbias_ref, dalog_ref, dd_ref, dgain_ref, dstate, dxd_ref):
        i = pl.program_id(1)

        @pl.when(i == 0)
        def _():
            dstate[...] = jnp.zeros_like(dstate)
            dbias_ref[...] = jnp.zeros_like(dbias_ref)
            dalog_ref[...] = jnp.zeros_like(dalog_ref)
            dd_ref[...] = jnp.zeros_like(dd_ref)
            dgain_ref[...] = jnp.zeros_like(dgain_ref)

        row = lax.broadcasted_iota(jnp.int32, (L, L), 0)
        col = lax.broadcasted_iota(jnp.int32, (L, L), 1)
        causal = row >= col
        tril = causal.astype(F32)
        triu = (row <= col).astype(F32)
        low = col < SSD_HEAD_DIM
        e64, e128 = _head_expanders()
        lane8 = lax.broadcasted_iota(jnp.int32, (1, HPG), 1)
        sub8 = lax.broadcasted_iota(jnp.int32, (HPG, 1), 0)
        eye8 = (lax.broadcasted_iota(jnp.int32, (HPG, HPG), 0) == lax.broadcasted_iota(jnp.int32, (HPG, HPG), 1)).astype(F32)
        last_row = (lax.broadcasted_iota(jnp.int32, (L, 1), 0) == L - 1).astype(F32)
        for c in reversed(range(CPS)):
            rows = pl.ds(c * L, L)
            xv = x_ref[rows, :]
            bm = b_ref[rows, :]
            cm = c_ref[rows, :]
            zz = z_ref[rows, :]
            dvec = d_ref[...]
            sz = _sigmoid(zz)
            silu_z = zz * sz
            v = yraw_ref[rows, :] + xv * dvec
            u = v * silu_z
            r = lax.rsqrt(jnp.mean(u * u, axis=-1, keepdims=True) + EPS)
            n = u * r
            do = dy_ref[rows, :]
            dgain_ref[...] += jnp.sum(do * n, axis=0, keepdims=True)
            dn = do * gain_ref[...]
            du = r * (dn - n * jnp.mean(dn * n, axis=-1, keepdims=True))
            dz_ref[rows, :] = (du * v * _silu_grad(zz, sz)).astype(dz_ref.dtype)
            dyv = du * silu_z
            dd_ref[...] += _xdot(jnp.sum(dyv * xv, axis=0, keepdims=True), e64, NT)
            pre, dt, a_neg, acum, acum_t = _ssd_chunk_terms(dtr_ref[rows, :], bias_ref[...], alog_ref[...], tril, triu)
            ac = _xdot(acum, e64, NN)
            ac_sq = _xdot(acum, e128, NN)
            dt_w = _xdot(dt, e64, NN)
            xd = xv * dt_w
            ac_last = ac[L - 1:L, :]
            ea = jnp.exp(ac)
            w = jnp.exp(ac_last - ac)
            ea_last = jnp.exp(ac_last)
            sp = st_ref[c]
            ds = dstate[...]
            dye = dyv * ea
            yoff = _bdot(cm, sp, NN) * ea
            bds = _bdot(bm, ds, NN)
            dcm = _bdot(dye, sp, NT)
            dbm = _bdot(xd * w, ds, NT)
            dstate[...] = ds * ea_last + _bdot(cm, dye, TN)
            w8 = jnp.exp(acum[L - 1:L, :] - acum)
            dw8 = _xdot(xd * bds, e64, NT)
            dac8 = _xdot(dyv * yoff, e64, NT) - dw8 * w8
            tail8 = jnp.sum(dw8 * w8, axis=0, keepdims=True) + jnp.exp(acum[L - 1:L, :]) * _xdot(
                jnp.sum(ds * sp, axis=0, keepdims=True), e64, NT)
            dac8 = dac8 + last_row * tail8
            gmat = _bdot(cm, bm, NT)
            dgmat = jnp.zeros((L, L), F32)
            colsum_t = jnp.zeros((HPG, L), F32)
            for q in range(HPG // 2):
                pair = slice(q * 128, (q + 1) * 128)
                xd_tile = xd[:, pair]
                dy_tile = dyv[:, pair]
                dxd_tile = bds[:, pair] * w[:, pair]
                for j, keep in ((2 * q, low), (2 * q + 1, ~low)):
                    lam = jnp.exp(jnp.where(causal, ac_sq[:, j * L:(j + 1) * L] - acum_t[j:j + 1, :], NEG))
                    mh = gmat * lam
                    dyj = jnp.where(keep, dy_tile, 0.0)
                    dxd_tile = dxd_tile + _bdot(mh, dyj, TN)
                    dm = _bdot(dyj, xd_tile, NT)
                    dgmat = dgmat + dm * lam
                    qm = dm * mh
                    dac8 = dac8 + jnp.sum(qm, axis=1, keepdims=True) * (lane8 == j).astype(F32)
                    colsum_t = colsum_t + (sub8 == j).astype(F32) * jnp.sum(qm, axis=0, keepdims=True)
                dxd_ref[:, pair] = dxd_tile
            dac8 = dac8 - _xdot(colsum_t, eye8, TN)
            dxd = dxd_ref[...]
            dx_ref[rows, :] = dxd * dt_w + dyv * dvec
            dc_ref[rows, :] = dcm + _bdot(dgmat, bm, NN)
            db_ref[rows, :] = dbm + _bdot(dgmat, cm, TN)
            da8 = _xdot(triu, dac8, NN, exact_first=True)
            ddt8 = _xdot(dxd * xv, e64, NT) + da8 * a_neg
            dalog_ref[...] += jnp.sum(da8 * dt, axis=0, keepdims=True) * a_neg
            dpre = ddt8 * _sigmoid(pre)
            ddtr_ref[rows, :] = dpre
            dbias_ref[...] += jnp.sum(dpre, axis=0, keepdims=True)

    rev = lambda i: NI - 1 - i
    vec8 = pl.BlockSpec((None, 1, HPG), lambda g, i: (g, 0, 0))
    grp = pl.BlockSpec((T, GW), lambda g, i: (rev(i), g))
    bspec = pl.BlockSpec((T, SSD_STATE), lambda g, i: (rev(i), SSD_WIDTH // SSD_STATE + g))
    cspec = pl.BlockSpec((T, SSD_STATE), lambda g, i: (rev(i), SSD_WIDTH // SSD_STATE + SSD_GROUPS + g))
    gvec = pl.BlockSpec((1, GW), lambda g, i: (0, g))
    st_spec = pl.BlockSpec((CPS, None, SSD_STATE, GW), lambda g, i: (rev(i), g, 0, 0))
    small = jax.ShapeDtypeStruct((SSD_GROUPS, 1, HPG), F32)
    zspec = pl.BlockSpec((T, GW), lambda g, i: (rev(i), Z_COL // GW + g))
    return pl.pallas_call(
        body, grid=(SSD_GROUPS, NI),
        in_specs=[grp, grp, grp, bspec, cspec, zspec, pl.BlockSpec((None, T, HPG), lambda g, i: (g, rev(i), 0)),
                  vec8, vec8, gvec, gvec, st_spec],
        out_specs=[zspec, grp, pl.BlockSpec((T, SSD_STATE), lambda g, i: (rev(i), g)),
                   pl.BlockSpec((T, SSD_STATE), lambda g, i: (rev(i), g)),
                   pl.BlockSpec((None, T, HPG), lambda g, i: (g, rev(i), 0)), vec8, vec8, vec8, gvec],
        out_shape=[jax.ShapeDtypeStruct((S, IN_PAD), BF16), jax.ShapeDtypeStruct((S, SSD_WIDTH), F32),
                   jax.ShapeDtypeStruct((S, SSD_GROUPS * SSD_STATE), F32), jax.ShapeDtypeStruct((S, SSD_GROUPS * SSD_STATE), F32),
                   jax.ShapeDtypeStruct((SSD_GROUPS, S, HPG), F32), small, small, small,
                   jax.ShapeDtypeStruct((1, SSD_WIDTH), F32)],
        scratch_shapes=[pltpu.VMEM((SSD_STATE, GW), F32), pltpu.VMEM((L, GW), F32)],
        name=name, compiler_params=_params("arbitrary", "arbitrary"),
    )(dy, yraw, xbc, xbc, xbc, proj, dtr, dt_bias, a_log, d_rep, gain, states)


def _swap_halves(t):
    w = t.shape[1]
    lane = lax.broadcasted_iota(jnp.int32, t.shape, 1)
    return jnp.where((lane % 64) < 32, pltpu.roll(t, w - 32, axis=1), pltpu.roll(t, 32, axis=1))


def _widen(tab, w):
    return tab if w == 128 else jnp.concatenate([tab] * (w // 128), axis=1)


def _rope(t, cos, sin_signed):
    return t * cos + _swap_halves(t) * sin_signed


def _rope_t(d, cos, sin_signed):
    return d * cos - _swap_halves(d) * sin_signed


def _group_sum64(v, bd):
    hi = v.astype(BF16)
    lo = (v - hi.astype(F32)).astype(BF16)
    return (lax.dot_general(hi, bd, NN, preferred_element_type=F32)
            + lax.dot_general(lo, bd, NN, preferred_element_type=F32))


AQ_BLK = ATT_COL // ATT_WIDTH


def _att_prep_fwd(proj, qg, kg, cos, sin, bd, name):
    S = proj.shape[0]
    T = min(512, S)
    PB = ATT_SPAN // T
    src = lambda i: jnp.maximum(i - PB, 0)

    def body(q_ref, k_ref, v_ref, qg_ref, kg_ref, cos_ref, sin_ref, bd_ref, qo_ref, ko_ref, vo_ref):
        i = pl.program_id(0)

        @pl.when(i < PB)
        def _():
            ko_ref[...] = jnp.zeros_like(ko_ref)
            vo_ref[...] = jnp.zeros_like(vo_ref)

        @pl.when(i >= PB)
        def _():
            cw = _widen(cos_ref[...], ATT_WIDTH)
            sw = _widen(sin_ref[...], ATT_WIDTH)
            bdv = bd_ref[...]

            def norm_rope(t, gain):
                ss = _group_sum64(t * t, bdv)
                return _rope(t * lax.rsqrt(ss * (1.0 / ATT_HEAD_DIM) + EPS) * gain, cw, sw)

            qo_ref[...] = (norm_rope(q_ref[...], qg_ref[...]) * (ATT_HEAD_DIM ** -0.5)).astype(BF16)
            kt = norm_rope(k_ref[...], kg_ref[...]).astype(BF16)
            vt = v_ref[...].astype(BF16)
            for pr in range(ATT_HEADS // 2):
                ko_ref[pr] = kt[:, pr * 128:(pr + 1) * 128]
                vo_ref[pr] = vt[:, pr * 128:(pr + 1) * 128]

    vec = pl.BlockSpec((1, ATT_WIDTH), lambda i: (0, 0))
    tab = pl.BlockSpec((T, 128), lambda i: (src(i), 0))
    hm = pl.BlockSpec((ATT_HEADS // 2, T, 128), lambda i: (0, i, 0))
    hm_shape = jax.ShapeDtypeStruct((ATT_HEADS // 2, ATT_SPAN + S, 128), BF16)
    return pl.pallas_call(
        body, grid=(PB + S // T,),
        in_specs=[pl.BlockSpec((T, ATT_WIDTH), lambda i: (src(i), AQ_BLK)), pl.BlockSpec((T, ATT_WIDTH), lambda i: (src(i), AQ_BLK + 1)),
                  pl.BlockSpec((T, ATT_WIDTH), lambda i: (src(i), AQ_BLK + 2)), vec, vec, tab, tab,
                  pl.BlockSpec((ATT_WIDTH, ATT_WIDTH), lambda i: (0, 0))],
        out_specs=[pl.BlockSpec((T, ATT_WIDTH), lambda i: (src(i), 0)), hm, hm],
        out_shape=[jax.ShapeDtypeStruct((S, ATT_WIDTH), BF16), hm_shape, hm_shape],
        name=name, compiler_params=_params("arbitrary"),
    )(proj, proj, proj, qg, kg, cos, sin, bd)


def _att_prep_bwd(proj, dq, dk_p, dv_p, qg, kg, cos, sin, bd, dproj, name):
    S = proj.shape[0]
    T = min(512, S)
    NI = S // T
    PB = ATT_SPAN // T
    W = ATT_WIDTH

    def body(q_ref, k_ref, dq_ref, dkp_ref, dvp_ref, qg_ref, kg_ref, cos_ref, sin_ref, bd_ref, dproj_ref,
             do_ref, dqg_ref, dkg_ref, acc_ref):
        i = pl.program_id(0)

        @pl.when(i == 0)
        def _():
            acc_ref[...] = jnp.zeros_like(acc_ref)

        npair = ATT_HEADS // 2
        dk_all = jnp.concatenate([dkp_ref[pr].T for pr in range(npair)], axis=1)
        do_ref[:, 2 * W:3 * W] = jnp.concatenate([dvp_ref[pr].T for pr in range(npair)], axis=1).astype(BF16)
        cw = _widen(cos_ref[...], ATT_WIDTH)
        sw = _widen(sin_ref[...], ATT_WIDTH)
        bdv = bd_ref[...]

        def one(t, d_rot, gain, scale, slot):
            ss = _group_sum64(t * t, bdv)
            r = lax.rsqrt(ss * (1.0 / ATT_HEAD_DIM) + EPS)
            n = t * r
            d_ng = _rope_t(d_rot * scale, cw, sw)
            acc_ref[pl.ds(slot, 1), :] += jnp.sum(d_ng * n, axis=0, keepdims=True)
            dn = d_ng * gain
            return r * (dn - n * (_group_sum64(dn * n, bdv) * (1.0 / ATT_HEAD_DIM)))

        do_ref[:, 0:W] = one(q_ref[...], dq_ref[...], qg_ref[...], ATT_HEAD_DIM ** -0.5, 0).astype(BF16)
        do_ref[:, W:2 * W] = one(k_ref[...], dk_all, kg_ref[...], 1.0, 1).astype(BF16)

        @pl.when(i == NI - 1)
        def _():
            a = acc_ref[...]
            f = a[:, 0:64]
            for h in range(1, ATT_HEADS):
                f = f + a[:, h * 64:(h + 1) * 64]
            dqg_ref[...] = f[0:1]
            dkg_ref[...] = f[1:2]

    vec = pl.BlockSpec((1, ATT_WIDTH), lambda i: (0, 0))
    tab = pl.BlockSpec((T, 128), lambda i: (i, 0))
    row = pl.BlockSpec((T, ATT_WIDTH), lambda i: (i, 0))
    g64 = pl.BlockSpec((1, ATT_HEAD_DIM), lambda i: (0, 0))
    padded = pl.BlockSpec((ATT_HEADS // 2, 128, T), lambda i: (0, 0, i + PB))
    return pl.pallas_call(
        body, grid=(NI,),
        in_specs=[pl.BlockSpec((T, ATT_WIDTH), lambda i: (i, AQ_BLK)), pl.BlockSpec((T, ATT_WIDTH), lambda i: (i, AQ_BLK + 1)),
                  row, padded, padded, vec, vec, tab, tab, pl.BlockSpec((ATT_WIDTH, ATT_WIDTH), lambda i: (0, 0)),
                  pl.BlockSpec(memory_space=pl.ANY)],
        out_specs=[pl.BlockSpec((T, 3 * W), lambda i: (i, ATT_COL // (3 * W))), g64, g64],
        out_shape=[jax.ShapeDtypeStruct(dproj.shape, dproj.dtype), jax.ShapeDtypeStruct((1, ATT_HEAD_DIM), F32),
                   jax.ShapeDtypeStruct((1, ATT_HEAD_DIM), F32)],
        input_output_aliases={10: 0},
        scratch_shapes=[pltpu.VMEM((8, ATT_WIDTH), F32)], name=name, compiler_params=_params("arbitrary"),
    )(proj, proj, dq, dk_p, dv_p, qg, kg, cos, sin, bd, dproj)


def _att_bias():
    qpos = np.arange(CHUNK)[:, None] + ATT_SPAN
    kpos = np.arange(ATT_STRIP)[None, :]
    rel = qpos - kpos
    mult = np.zeros((CHUNK, ATT_STRIP), np.float64)
    for window, dil in DILATED_PAIRS:
        mult += (rel >= 0) & (rel % dil == 0) & (rel // dil <= window // dil)
    return np.where(mult > 0, np.log(np.maximum(mult, 1.0)), NEG).astype(np.float32)


def _att_scores(q, ks, bias, i):
    s = _bdot(q, ks, NT) + bias
    kcol = lax.broadcasted_iota(jnp.int32, (1, ATT_STRIP), 1) + i * CHUNK
    return jnp.where(kcol >= ATT_SPAN, s, NEG)


def _pair_masks():
    low = lax.broadcasted_iota(jnp.int32, (CHUNK, 128), 1) < ATT_HEAD_DIM
    return low, ~low


def _att_fwd(q, kp, vp, bias, y, name):
    S = q.shape[0]
    SP = kp.shape[1]

    TQ = ATT_QB * CHUNK

    def body(q_ref, k_ref, v_ref, bias_ref, y_ref, o_ref):
        i = pl.program_id(1)
        for b in range(ATT_QB):
            blk = i * ATT_QB + b
            strip = pl.ds(pl.multiple_of(blk * CHUNK, CHUNK), ATT_STRIP)
            rows = pl.ds(b * CHUNK, CHUNK)
            qv = q_ref[rows, :]
            ks = k_ref[strip, :]
            vs = v_ref[strip, :]
            outs = []
            for keep in _pair_masks():
                s = _att_scores(jnp.where(keep, qv, jnp.zeros_like(qv)), ks, bias_ref[...], blk)
                m = jnp.max(s, axis=-1, keepdims=True)
                p = jnp.exp(s - m)
                den = jnp.sum(p, axis=-1, keepdims=True)
                outs.append(_bdot(p, vs, NN) / den)
            o_ref[rows, :] = jnp.where(_pair_masks()[0], outs[0], outs[1]).astype(o_ref.dtype)

    kv = pl.BlockSpec((None, SP, 128), lambda hp, i: (hp, 0, 0))
    return pl.pallas_call(
        body, grid=(ATT_HEADS // 2, S // TQ),
        in_specs=[pl.BlockSpec((TQ, 128), lambda hp, i: (i, hp)), kv, kv,
                  pl.BlockSpec((CHUNK, ATT_STRIP), lambda hp, i: (0, 0)), pl.BlockSpec(memory_space=pl.ANY)],
        out_specs=pl.BlockSpec((TQ, 128), lambda hp, i: (i, SSD_WIDTH // 128 + hp)),
        out_shape=jax.ShapeDtypeStruct(y.shape, y.dtype), input_output_aliases={4: 0}, name=name,
        compiler_params=_params("parallel", "arbitrary"),
    )(q, kp, vp, bias, y)


def _att_bwd(q, kp, vp, bias, dy, name):
    S = q.shape[0]
    SP = kp.shape[1]

    def body(q_ref, k_ref, v_ref, bias_ref, do_ref, dq_ref, dk_ref, dv_ref):
        i = pl.program_id(1)

        @pl.when(i == 0)
        def _():
            dk_ref[...] = jnp.zeros_like(dk_ref)
            dv_ref[...] = jnp.zeros_like(dv_ref)

        for b in range(ATT_QB):
            blk = i * ATT_QB + b
            strip = pl.ds(pl.multiple_of(blk * CHUNK, CHUNK), ATT_STRIP)
            rows = pl.ds(b * CHUNK, CHUNK)
            qv = q_ref[rows, :]
            dov = do_ref[rows, :]
            ks = k_ref[strip, :]
            vs = v_ref[strip, :]
            dq = jnp.zeros((CHUNK, 128), F32)
            dk_t = jnp.zeros((128, ATT_STRIP), F32)
            dv_t = jnp.zeros((128, ATT_STRIP), F32)
            for keep in _pair_masks():
                qh = jnp.where(keep, qv, jnp.zeros_like(qv))
                doh = jnp.where(keep, dov, 0.0)
                s = _att_scores(qh, ks, bias_ref[...], blk)
                m = jnp.max(s, axis=-1, keepdims=True)
                p = jnp.exp(s - m)
                p = p / jnp.sum(p, axis=-1, keepdims=True)
                dp = _bdot(doh, vs, NT)
                dsc = p * (dp - jnp.sum(p * dp, axis=-1, keepdims=True))
                dq = dq + jnp.where(keep, _bdot(dsc, ks, NN), 0.0)
                dv_t = dv_t + _bdot(doh, p, TN)
                dk_t = dk_t + _bdot(qh, dsc, TN)
            dq_ref[rows, :] = dq
            dv_ref[:, strip] += dv_t
            dk_ref[:, strip] += dk_t

    TQ = ATT_QB * CHUNK
    kv = pl.BlockSpec((None, SP, 128), lambda hp, i: (hp, 0, 0))
    kv_t = pl.BlockSpec((None, 128, SP), lambda hp, i: (hp, 0, 0))
    pairs = jax.ShapeDtypeStruct((ATT_HEADS // 2, 128, SP), F32)
    return pl.pallas_call(
        body, grid=(ATT_HEADS // 2, S // TQ),
        in_specs=[pl.BlockSpec((TQ, 128), lambda hp, i: (i, hp)), kv, kv,
                  pl.BlockSpec((CHUNK, ATT_STRIP), lambda hp, i: (0, 0)),
                  pl.BlockSpec((TQ, 128), lambda hp, i: (i, SSD_WIDTH // 128 + hp))],
        out_specs=[pl.BlockSpec((TQ, 128), lambda hp, i: (i, hp)), kv_t, kv_t],
        out_shape=[jax.ShapeDtypeStruct((S, ATT_WIDTH), F32), pairs, pairs],
        name=name, compiler_params=_params("parallel", "arbitrary"),
    )(q, kp, vp, bias, dy)


RQ_BLK = RET_COL // RET_QK_WIDTH
RV_BLK = (RET_COL + 2 * RET_QK_WIDTH) // RET_V_WIDTH
RET_LOG_GAMMA = tuple(math.log1p(-2.0 ** (-5.0 - h)) for h in range(RET_HEADS))


def _ret_decays(h):
    L = CHUNK
    lg = RET_LOG_GAMMA[h]
    row = lax.broadcasted_iota(jnp.int32, (L, L), 0)
    col = lax.broadcasted_iota(jnp.int32, (L, L), 1)
    rel = (row - col).astype(F32)
    dm = jnp.where(rel >= 0, jnp.exp(jnp.maximum(rel, 0.0) * lg), 0.0)
    idx = lax.broadcasted_iota(jnp.int32, (L, 1), 0).astype(F32)
    kte = jnp.exp((L - 1 - idx) * lg)
    qfs = jnp.exp((idx + 1.0) * lg)
    return dm, kte, qfs, math.exp(L * lg)


def _ret_fwd(proj, cos, sin, gain, y, name):
    S = proj.shape[0]
    L = CHUNK
    T = min(512, S)
    CPS = T // L
    NC = S // L

    def body(q_ref, k_ref, v_ref, g_ref, cos_ref, sin_ref, gain_ref, yin_ref, y_ref, o_ref, st_ref, state):
        i = pl.program_id(0)

        @pl.when(i == 0)
        def _():
            state[...] = jnp.zeros_like(state)

        dec = [_ret_decays(h) for h in range(RET_HEADS)]
        for c in range(CPS):
            rows = pl.ds(c * L, L)
            cw = _widen(cos_ref[rows, :], RET_QK_WIDTH)
            sw = _widen(sin_ref[rows, :], RET_QK_WIDTH)
            qv = _rope(q_ref[rows, :], cw, sw)
            kv = _rope(k_ref[rows, :], cw, sw) * (RET_QK_DIM ** -0.5)
            for h in range(RET_HEADS):
                dm, kte, qfs, cd = dec[h]
                qh = qv[:, h * 64:(h + 1) * 64]
                kh = kv[:, h * 64:(h + 1) * 64]
                vs = slice(h * RET_V_DIM, (h + 1) * RET_V_DIM)
                vh = v_ref[rows, vs]
                sp = state[h]
                st_ref[c, h] = sp
                o = _bdot(_bdot(qh, kh, NT) * dm, vh, NN) + _bdot(qh * qfs, sp, NN)
                state[h] = cd * sp + _bdot(kh * kte, vh, TN)
                o_ref[rows, vs] = o
                gh = g_ref[rows, vs]
                r = lax.rsqrt(jnp.mean(o * o, axis=-1, keepdims=True) + EPS)
                y_ref[rows, vs] = (o * r * gain_ref[:, vs] * (gh * _sigmoid(gh))).astype(y_ref.dtype)

    tab = pl.BlockSpec((T, 128), lambda i: (i, 0))
    wide = pl.BlockSpec((T, RET_V_WIDTH), lambda i: (i, 0))
    return pl.pallas_call(
        body, grid=(S // T,),
        in_specs=[pl.BlockSpec((T, RET_QK_WIDTH), lambda i: (i, RQ_BLK)), pl.BlockSpec((T, RET_QK_WIDTH), lambda i: (i, RQ_BLK + 1)),
                  pl.BlockSpec((T, RET_V_WIDTH), lambda i: (i, RV_BLK)), pl.BlockSpec((T, RET_V_WIDTH), lambda i: (i, RV_BLK + 1)),
                  tab, tab, pl.BlockSpec((1, RET_V_WIDTH), lambda i: (0, 0)), pl.BlockSpec(memory_space=pl.ANY)],
        out_specs=[pl.BlockSpec((T, RET_V_WIDTH), lambda i: (i, (SSD_WIDTH + ATT_WIDTH) // RET_V_WIDTH)), wide,
                   pl.BlockSpec((CPS, RET_HEADS, RET_QK_DIM, RET_V_DIM), lambda i: (i, 0, 0, 0))],
        out_shape=[jax.ShapeDtypeStruct(y.shape, y.dtype), jax.ShapeDtypeStruct((S, RET_V_WIDTH), F32),
                   jax.ShapeDtypeStruct((NC, RET_HEADS, RET_QK_DIM, RET_V_DIM), F32)],
        input_output_aliases={7: 0},
        scratch_shapes=[pltpu.VMEM((RET_HEADS, RET_QK_DIM, RET_V_DIM), F32)], name=name,
        compiler_params=_params("arbitrary"),
    )(proj, proj, proj, proj, cos, sin, gain, y)


def _ret_bwd(dy, oraw, proj, cos, sin, gain, states, dproj, name):
    S = proj.shape[0]
    L = CHUNK
    T = min(512, S)
    CPS = T // L
    NI = S // T
    QW, VW = RET_QK_WIDTH, RET_V_WIDTH
    V0, G0 = 2 * QW, 2 * QW + VW

    def body(dy_ref, o_ref, q_ref, k_ref, v_ref, g_ref, cos_ref, sin_ref, gain_ref, st_ref, dproj_ref,
             out_ref, dgain_ref, dstate, dqs, dks):
        i = pl.program_id(0)

        @pl.when(i == 0)
        def _():
            dstate[...] = jnp.zeros_like(dstate)
            dgain_ref[...] = jnp.zeros_like(dgain_ref)

        dec = [_ret_decays(h) for h in range(RET_HEADS)]
        for c in reversed(range(CPS)):
            rows = pl.ds(c * L, L)
            cw = _widen(cos_ref[rows, :], RET_QK_WIDTH)
            sw = _widen(sin_ref[rows, :], RET_QK_WIDTH)
            qv = _rope(q_ref[rows, :], cw, sw)
            kv = _rope(k_ref[rows, :], cw, sw) * (RET_QK_DIM ** -0.5)
            for h in range(RET_HEADS):
                dm, kte, qfs, cd = dec[h]
                qs = slice(h * 64, (h + 1) * 64)
                vs = slice(h * RET_V_DIM, (h + 1) * RET_V_DIM)
                qh = qv[:, qs]
                kh = kv[:, qs]
                vh = v_ref[rows, vs]
                gh = g_ref[rows, vs]
                gn = gain_ref[:, vs]
                o = o_ref[rows, vs]
                dyh = dy_ref[rows, vs]
                sg = _sigmoid(gh)
                silu_g = gh * sg
                r = lax.rsqrt(jnp.mean(o * o, axis=-1, keepdims=True) + EPS)
                n = o * r
                dgain_ref[:, vs] += jnp.sum(dyh * n * silu_g, axis=0, keepdims=True)
                out_ref[rows, G0 + h * RET_V_DIM:G0 + (h + 1) * RET_V_DIM] = (dyh * n * gn * _silu_grad(gh, sg)).astype(out_ref.dtype)
                dn = dyh * gn * silu_g
                do = r * (dn - n * jnp.mean(dn * n, axis=-1, keepdims=True))
                sp = st_ref[c, h]
                ds = dstate[h]
                sc = _bdot(qh, kh, NT) * dm
                dsc = _bdot(do, vh, NT) * dm
                out_ref[rows, V0 + h * RET_V_DIM:V0 + (h + 1) * RET_V_DIM] = (_bdot(sc, do, TN) + _bdot(kh * kte, ds, NN)).astype(out_ref.dtype)
                dqs[:, qs] = _bdot(dsc, kh, NN) + _bdot(do, sp, NT) * qfs
                dks[:, qs] = _bdot(dsc, qh, TN) + _bdot(vh, ds, NT) * kte
                dstate[h] = cd * ds + _bdot(qh * qfs, do, TN)
            out_ref[rows, 0:QW] = _rope_t(dqs[...], cw, sw).astype(out_ref.dtype)
            out_ref[rows, QW:2 * QW] = _rope_t(dks[...] * (RET_QK_DIM ** -0.5), cw, sw).astype(out_ref.dtype)

    rev = lambda i: NI - 1 - i
    tab = pl.BlockSpec((T, 128), lambda i: (rev(i), 0))
    wide = pl.BlockSpec((T, RET_V_WIDTH), lambda i: (rev(i), 0))
    group = pl.BlockSpec((T, G0 + VW), lambda i: (rev(i), RET_COL // (G0 + VW)))
    gvec = pl.BlockSpec((1, RET_V_WIDTH), lambda i: (0, 0))
    return pl.pallas_call(
        body, grid=(NI,),
        in_specs=[pl.BlockSpec((T, RET_V_WIDTH), lambda i: (rev(i), (SSD_WIDTH + ATT_WIDTH) // RET_V_WIDTH)), wide,
                  pl.BlockSpec((T, RET_QK_WIDTH), lambda i: (rev(i), RQ_BLK)), pl.BlockSpec((T, RET_QK_WIDTH), lambda i: (rev(i), RQ_BLK + 1)),
                  pl.BlockSpec((T, RET_V_WIDTH), lambda i: (rev(i), RV_BLK)), pl.BlockSpec((T, RET_V_WIDTH), lambda i: (rev(i), RV_BLK + 1)),
                  tab, tab, gvec,
                  pl.BlockSpec((CPS, RET_HEADS, RET_QK_DIM, RET_V_DIM), lambda i: (rev(i), 0, 0, 0)),
                  pl.BlockSpec(memory_space=pl.ANY)],
        out_specs=[group, gvec],
        out_shape=[jax.ShapeDtypeStruct(dproj.shape, dproj.dtype), jax.ShapeDtypeStruct((1, RET_V_WIDTH), F32)],
        input_output_aliases={10: 0},
        scratch_shapes=[pltpu.VMEM((RET_HEADS, RET_QK_DIM, RET_V_DIM), F32), pltpu.VMEM((L, RET_QK_WIDTH), F32),
                        pltpu.VMEM((L, RET_QK_WIDTH), F32)],
        name=name, compiler_params=_params("arbitrary"),
    )(dy, oraw, proj, proj, proj, proj, cos, sin, gain, states, dproj)


def _adamw_update(g_ref, nb, w_ref, m_ref, v_ref, go_ref, d_ref, mo_ref, vo_ref):
    g = g_ref[0].astype(F32)
    for k in range(1, nb):
        g = g + g_ref[k].astype(F32)
    mn = ADAM_B1 * m_ref[...] + (1.0 - ADAM_B1) * g
    vn = ADAM_B2 * v_ref[...] + (1.0 - ADAM_B2) * (g * g)
    go_ref[...] = g
    mo_ref[...] = mn
    vo_ref[...] = vn
    c1 = 1.0 - ADAM_B1 ** ADAM_STEP
    c2 = 1.0 - ADAM_B2 ** ADAM_STEP
    d_ref[...] = -ADAM_LR * ((mn / c1) / (jnp.sqrt(vn / c2) + ADAM_EPS) + ADAM_WD * w_ref[...])


def _adamw_rows(R, C):
    return _pick(R, tuple(t for t in (512, 256, 128, 64, 32, 16, 8) if t * C <= 256 * 1024))


def _adamw(gblocks, w, m, v, name):
    nb, R, C = gblocks.shape
    tr = _adamw_rows(R, C)

    def body(g_ref, *refs):
        _adamw_update(g_ref, nb, *refs)

    row = pl.BlockSpec((tr, C), lambda i: (i, 0))
    sh = jax.ShapeDtypeStruct((R, C), F32)
    return pl.pallas_call(
        body, grid=(R // tr,), in_specs=[pl.BlockSpec((nb, tr, C), lambda i: (0, i, 0)), row, row, row],
        out_specs=[row, row, row, row], out_shape=[sh, sh, sh, sh], name=name, compiler_params=_params("parallel"),
    )(gblocks, w, m, v)


def _adamw_layers(g0, g1, w, m, v, name):
    nb, R, C = g0.shape
    tr = _adamw_rows(R, C)

    def body(g0_ref, g1_ref, *refs):
        l = pl.program_id(0)

        @pl.when(l == 0)
        def _():
            _adamw_update(g0_ref, nb, *refs)

        @pl.when(l == 1)
        def _():
            _adamw_update(g1_ref, nb, *refs)

    row = pl.BlockSpec((None, tr, C), lambda l, i: (l, i, 0))
    sh = jax.ShapeDtypeStruct((DEPTH, R, C), F32)
    return pl.pallas_call(
        body, grid=(DEPTH, R // tr),
        in_specs=[pl.BlockSpec((nb, tr, C), lambda l, i: (0, i * (1 - l), 0)), pl.BlockSpec((nb, tr, C), lambda l, i: (0, i * l, 0)),
                  row, row, row],
        out_specs=[row, row, row, row], out_shape=[sh, sh, sh, sh], name=name, compiler_params=_params("arbitrary", "arbitrary"),
    )(g0, g1, w, m, v)


def _peers():
    x, y, c = lax.axis_index("x"), lax.axis_index("y"), lax.axis_index("c")
    flips = ((0, 0, 1), (1, 0, 0), (0, 1, 0), (1, 1, 0), (1, 0, 1), (0, 1, 1), (1, 1, 1))
    me = 4 * x + 2 * y + c
    peers = [(x ^ fx, y ^ fy, c ^ fc) for fx, fy, fc in flips]
    return me, peers


def _exchange(arrs, scatter, name):
    n = len(arrs)
    npeer = N_DEV - 1

    def body(*refs):
        ins, outs = refs[:n], refs[n:2 * n]
        send_sems, recv_sems, local_sems = refs[2 * n:]
        me, peers = _peers()
        copies = []
        for a in range(n):
            src_own = ins[a].at[me] if scatter else ins[a]
            own = pltpu.make_async_copy(src_own, outs[a].at[me], local_sems.at[a])
            own.start()
            copies.append(own)
            for k, peer in enumerate(peers):
                src = ins[a].at[4 * peer[0] + 2 * peer[1] + peer[2]] if scatter else ins[a]
                cp = pltpu.make_async_remote_copy(
                    src_ref=src, dst_ref=outs[a].at[me], send_sem=send_sems.at[a * npeer + k],
                    recv_sem=recv_sems.at[a * npeer + k], device_id=peer, device_id_type=pl.DeviceIdType.MESH)
                cp.start()
                copies.append(cp)
        for cp in copies:
            cp.wait()

    out_shape = [jax.ShapeDtypeStruct(((N_DEV,) + a.shape[1:]) if scatter else ((N_DEV,) + a.shape), a.dtype) for a in arrs]
    anyspec = pl.BlockSpec(memory_space=pl.ANY)
    return pl.pallas_call(
        body, in_specs=[anyspec] * n, out_specs=[anyspec] * n, out_shape=out_shape,
        scratch_shapes=[pltpu.SemaphoreType.DMA((n * npeer,)), pltpu.SemaphoreType.DMA((n * npeer,)),
                        pltpu.SemaphoreType.DMA((n,))],
        name=name,
    )(*arrs)


def _dev_index(peer):
    return 4 * peer[0] + 2 * peer[1] + peer[2]


def _push_copies(src_refs, land_refs, send_sems, recv_sems, scatter, as_receiver):
    me, peers = _peers()
    npeer = N_DEV - 1
    copies = []
    for a in range(len(src_refs)):
        for k, peer in enumerate(peers):
            src = src_refs[a].at[_dev_index(peer)] if scatter else src_refs[a]
            slot = _dev_index(peer) if as_receiver else me
            copies.append(pltpu.make_async_remote_copy(
                src_ref=src, dst_ref=land_refs[a].at[slot], send_sem=send_sems.at[a * npeer + k],
                recv_sem=recv_sems.at[a * npeer + k], device_id=peer, device_id_type=pl.DeviceIdType.MESH))
    return copies


def _own_copies(src_refs, land_refs, own_sems, scatter):
    me, _ = _peers()
    return [pltpu.make_async_copy(src_refs[a].at[me] if scatter else src_refs[a], land_refs[a].at[me], own_sems.at[a])
            for a in range(len(src_refs))]


def _push_start(srcs, scatter, name):
    n = len(srcs)
    nsem = n * (N_DEV - 1)

    def body(*refs):
        srcs_r, lands_r = refs[:n], refs[n:2 * n]
        for cp in _push_copies(srcs_r, lands_r, refs[2 * n], refs[2 * n + 1], scatter, False):
            cp.start()
        for cp in _own_copies(srcs_r, lands_r, refs[2 * n + 2], scatter):
            cp.start()
        token = refs[-1]
        token[...] = jnp.zeros_like(token)

    hbm = pl.BlockSpec(memory_space=pltpu.HBM)
    sem = pl.BlockSpec(memory_space=pltpu.SEMAPHORE)
    lands = [lax.empty((N_DEV,) + (s.shape[1:] if scatter else s.shape), s.dtype) for s in srcs]
    arrs = list(srcs) + lands
    return pl.pallas_call(
        body, name=name,
        out_shape=(pltpu.SemaphoreType.DMA((nsem,)), pltpu.SemaphoreType.DMA((nsem,)), pltpu.SemaphoreType.DMA((n,)),
                   *[pltpu.HBM(a.shape, a.dtype) for a in arrs], jax.ShapeDtypeStruct((8, 128), F32)),
        in_specs=[hbm] * (2 * n), out_specs=(sem, sem, sem, *([hbm] * (2 * n)), pl.BlockSpec(memory_space=pltpu.VMEM)),
        input_output_aliases={i: 3 + i for i in range(2 * n)},
        compiler_params=pltpu.CompilerParams(has_side_effects=pltpu.SideEffectType.DATAFLOW_SIDE_EFFECTING),
    )(*[pltpu.with_memory_space_constraint(a, pltpu.HBM) for a in arrs])


def _push_wait(handle, after, scatter, name):
    send_sems, recv_sems, own_sems, *thru, _ = handle
    n = len(thru) // 2

    def body(*refs):
        srcs_r, lands_r = refs[:n], refs[n:2 * n]
        for cp in _push_copies(srcs_r, lands_r, refs[2 * n], refs[2 * n + 1], scatter, True):
            cp.wait_send()
            cp.wait_recv()
        for cp in _own_copies(srcs_r, lands_r, refs[2 * n + 2], scatter):
            cp.wait()

    hbm = pl.BlockSpec(memory_space=pltpu.HBM)
    sem = pl.BlockSpec(memory_space=pltpu.SEMAPHORE)
    outs = pl.pallas_call(
        body, name=name, out_shape=tuple(pltpu.HBM(a.shape, a.dtype) for a in thru),
        in_specs=[hbm] * (2 * n) + [sem, sem, sem, pl.BlockSpec(memory_space=pl.ANY)], out_specs=tuple([hbm] * (2 * n)),
        input_output_aliases={i: i for i in range(2 * n)},
        compiler_params=pltpu.CompilerParams(has_side_effects=pltpu.SideEffectType.DATAFLOW_SIDE_EFFECTING),
    )(*thru, send_sems, recv_sems, own_sems, after)
    return list(outs[n:])


def _relay_copies(src_refs, land_refs, send1, recv1, send2, recv2, as_receiver):
    me, peers = _peers()
    sibling, chips = peers[0], peers[1:4]
    first, second = [], []
    for a in range(len(src_refs)):
        for k, peer in enumerate([sibling] + chips):
            slot = _dev_index(peer) if as_receiver else me
            first.append(pltpu.make_async_remote_copy(
                src_ref=src_refs[a], dst_ref=land_refs[a].at[slot], send_sem=send1.at[4 * a + k], recv_sem=recv1.at[4 * a + k],
                device_id=peer, device_id_type=pl.DeviceIdType.MESH))
        for k, chip in enumerate(chips):
            origin = _dev_index(chip)
            slot = origin ^ 1 if as_receiver else origin
            second.append(pltpu.make_async_remote_copy(
                src_ref=land_refs[a].at[origin], dst_ref=land_refs[a].at[slot], send_sem=send2.at[3 * a + k],
                recv_sem=recv2.at[3 * a + k], device_id=sibling, device_id_type=pl.DeviceIdType.MESH))
    return first, second


def _relay_start(srcs, name):
    n = len(srcs)

    def body(*refs):
        srcs_r, lands_r = refs[:n], refs[n:2 * n]
        for cp in _relay_copies(srcs_r, lands_r, refs[2 * n], refs[2 * n + 1], refs[2 * n], refs[2 * n + 1], False)[0]:
            cp.start()
        for cp in _own_copies(srcs_r, lands_r, refs[2 * n + 2], False):
            cp.start()
        token = refs[-1]
        token[...] = jnp.zeros_like(token)

    hbm = pl.BlockSpec(memory_space=pltpu.HBM)
    sem = pl.BlockSpec(memory_space=pltpu.SEMAPHORE)
    arrs = list(srcs) + [lax.empty((N_DEV,) + s.shape, s.dtype) for s in srcs]
    return pl.pallas_call(
        body, name=name,
        out_shape=(pltpu.SemaphoreType.DMA((4 * n,)), pltpu.SemaphoreType.DMA((4 * n,)), pltpu.SemaphoreType.DMA((n,)),
                   *[pltpu.HBM(a.shape, a.dtype) for a in arrs], jax.ShapeDtypeStruct((8, 128), F32)),
        in_specs=[hbm] * (2 * n), out_specs=(sem, sem, sem, *([hbm] * (2 * n)), pl.BlockSpec(memory_space=pltpu.VMEM)),
        input_output_aliases={i: 3 + i for i in range(2 * n)},
        compiler_params=pltpu.CompilerParams(has_side_effects=pltpu.SideEffectType.DATAFLOW_SIDE_EFFECTING),
    )(*[pltpu.with_memory_space_constraint(a, pltpu.HBM) for a in arrs])


def _relay_forward(handle, after, name):
    _, recv1, _, *thru, _ = handle
    n = len(thru) // 2

    def body(*refs):
        srcs_r, lands_r, recv1_r = refs[:n], refs[n:2 * n], refs[2 * n]
        send2_r, recv2_r = refs[2 * n + 2], refs[2 * n + 3]
        first, second = _relay_copies(srcs_r, lands_r, recv1_r, recv1_r, send2_r, recv2_r, True)
        for a in range(n):
            for k in range(1, 4):
                first[4 * a + k].wait_recv()
        for cp in _relay_copies(srcs_r, lands_r, recv1_r, recv1_r, send2_r, recv2_r, False)[1]:
            cp.start()
        token = refs[-1]
        token[...] = jnp.zeros_like(token)

    hbm = pl.BlockSpec(memory_space=pltpu.HBM)
    sem = pl.BlockSpec(memory_space=pltpu.SEMAPHORE)
    return pl.pallas_call(
        body, name=name,
        out_shape=(pltpu.SemaphoreType.DMA((3 * n,)), pltpu.SemaphoreType.DMA((3 * n,)),
                   *[pltpu.HBM(a.shape, a.dtype) for a in thru], jax.ShapeDtypeStruct((8, 128), F32)),
        in_specs=[hbm] * (2 * n) + [sem, pl.BlockSpec(memory_space=pl.ANY)],
        out_specs=(sem, sem, *([hbm] * (2 * n)), pl.BlockSpec(memory_space=pltpu.VMEM)),
        input_output_aliases={i: 2 + i for i in range(2 * n)},
        compiler_params=pltpu.CompilerParams(has_side_effects=pltpu.SideEffectType.DATAFLOW_SIDE_EFFECTING),
    )(*thru, recv1, after)


def _relay_wait(handle, forwarded, after, name):
    send1, recv1, own_sems, *_ = handle
    send2, recv2, *thru, _ = forwarded
    n = len(thru) // 2

    def body(*refs):
        srcs_r, lands_r = refs[:n], refs[n:2 * n]
        send1_r, recv1_r, own_r, send2_r, recv2_r = refs[2 * n:2 * n + 5]
        first, second = _relay_copies(srcs_r, lands_r, send1_r, recv1_r, send2_r, recv2_r, True)
        for i, cp in enumerate(first):
            cp.wait_send()
            if i % 4 == 0:
                cp.wait_recv()
        for cp in second:
            cp.wait_send()
            cp.wait_recv()
        for cp in _own_copies(srcs_r, lands_r, own_r, False):
            cp.wait()

    hbm = pl.BlockSpec(memory_space=pltpu.HBM)
    sem = pl.BlockSpec(memory_space=pltpu.SEMAPHORE)
    outs = pl.pallas_call(
        body, name=name, out_shape=tuple(pltpu.HBM(a.shape, a.dtype) for a in thru),
        in_specs=[hbm] * (2 * n) + [sem] * 5 + [pl.BlockSpec(memory_space=pl.ANY)], out_specs=tuple([hbm] * (2 * n)),
        input_output_aliases={i: i for i in range(2 * n)},
        compiler_params=pltpu.CompilerParams(has_side_effects=pltpu.SideEffectType.DATAFLOW_SIDE_EFFECTING),
    )(*thru, send1, recv1, own_sems, send2, recv2, after)
    return list(outs[n:])


def _tables(S):
    pos = jnp.arange(S, dtype=F32)
    inv = ROPE_THETA ** (-jnp.arange(0, ATT_HEAD_DIM, 2, dtype=F32) / ATT_HEAD_DIM)
    ang = pos[:, None] * inv[None, :]
    cos, sin = jnp.cos(ang), jnp.sin(ang)
    cos128 = jnp.tile(cos, (1, 4))
    sin128 = jnp.tile(jnp.concatenate([-sin, sin], axis=1), (1, 2))
    lane = np.arange(ATT_WIDTH)
    bd = jnp.asarray((lane[:, None] // 64 == lane[None, :] // 64).astype(np.float32), dtype=BF16)
    return cos128, sin128, bd, jnp.asarray(_att_bias())


def _layer_fwd(l, x, p, tabs, early=None, late=None):
    cos, sin, bd, bias = tabs
    S = x.shape[0]
    row = lambda v: v.reshape(1, -1)
    hn = _rmsnorm_fwd(x, row(p["ln_mix"]), f"norm_mix_fwd{l}")
    if early is not None:
        p.update(early(hn))
    proj = _mm(hn, p["w_in"], "nn", f"in_proj{l}", tn=1152)
    xbc = _conv_fwd(proj, p["conv_w"], row(p["conv_b"]), f"conv_fwd{l}")
    dtr = proj[:, DT_COL:DT_COL + SSD_HEADS].reshape(S, SSD_GROUPS, HPG).transpose(1, 0, 2)
    grp = lambda v: v.reshape(SSD_GROUPS, 1, HPG)
    d_rep = row(jnp.repeat(p["d_skip"], SSD_HEAD_DIM))
    y, yraw, ssd_st = _ssd_fwd(xbc, proj, dtr, grp(p["dt_bias"]), grp(p["a_log"]), d_rep, row(p["ssd_norm"]), f"ssd_fwd{l}")
    qg = row(jnp.tile(p["q_norm"], ATT_HEADS))
    kg = row(jnp.tile(p["k_norm"], ATT_HEADS))
    aq, akp, avp = _att_prep_fwd(proj, qg, kg, cos, sin, bd, f"att_prep_fwd{l}")
    y = _att_fwd(aq, akp, avp, bias, y, f"att_fwd{l}")
    y, oraw, ret_st = _ret_fwd(proj, cos, sin, row(p["ret_norm"]), y, f"ret_fwd{l}")
    if late is not None:
        p.update(late(y))
    x1 = _mm(y, p["w_out"], "nn", f"out_proj{l}", residual=x)
    hn2 = _rmsnorm_fwd(x1, row(p["ln_ffn"]), f"norm_ffn_fwd{l}")
    g, u, act = _swiglu_fwd(hn2, p["w_gate"], p["w_up"], f"swiglu_fwd{l}")
    x2 = _mm(act, p["w_down"], "nn", f"down_proj{l}", residual=x1, tk=2816)
    saved = dict(x=x, hn=hn, proj=proj, xbc=xbc, dtr=dtr, yraw=yraw, ssd_st=ssd_st, aq=aq, akp=akp, avp=avp,
                 oraw=oraw, ret_st=ret_st, y=y, x1=x1, hn2=hn2, g=g, u=u, act=act, d_rep=d_rep, qg=qg, kg=kg)
    return x2, saved


def _layer_bwd(l, dx2, dx2_bf, p, sv, tabs, on_ffn=None, on_all=None):
    cos, sin, bd, bias = tabs
    S = dx2.shape[0]
    row = lambda v: v.reshape(1, -1)
    grp = lambda v: v.reshape(SSD_GROUPS, 1, HPG)
    gr = {}
    dg, du = _swiglu_bwd(dx2_bf, p["w_down"], sv["g"], sv["u"], f"swiglu_bwd{l}")
    gr["w_down"] = _mm(sv["act"], dx2_bf, "tn", f"down_wgrad{l}", out_dtype=BF16, tm=1408, tn=1024, tk=2048)
    dhn2 = _mm_nt2(dg, p["w_gate"], du, p["w_up"], f"ffn_dgrad{l}")
    gr["w_gate"] = _mm(sv["hn2"], dg, "tn", f"gate_wgrad{l}", out_dtype=BF16, tm=1024, tn=1408, tk=2048)
    gr["w_up"] = _mm(sv["hn2"], du, "tn", f"up_wgrad{l}", out_dtype=BF16, tm=1024, tn=1408, tk=2048)
    ffn_gain = row(p["ln_ffn"]) + (on_ffn(gr)[0, 0] if on_ffn is not None else 0.0)
    dx1, dx1_bf, dln_ffn = _rmsnorm_bwd(sv["x1"], dhn2, ffn_gain, dx2, f"norm_ffn_bwd{l}")
    gr["ln_ffn"] = dln_ffn[0]
    dy = _mm(dx1_bf, p["w_out"], "nt", f"out_dgrad{l}")
    gr["w_out"] = _mm(sv["y"], dx1_bf, "tn", f"out_wgrad{l}", out_dtype=BF16, tm=1024, tn=1024, tk=2048)
    dproj, dxs, dbm, dcm, ddtr, dbias, dalog, dd, dssd_gain = _ssd_bwd(
        dy, sv["yraw"], sv["xbc"], sv["proj"], sv["dtr"], grp(p["dt_bias"]), grp(p["a_log"]), sv["d_rep"],
        row(p["ssd_norm"]), sv["ssd_st"], f"ssd_bwd{l}")
    gr["dt_bias"], gr["a_log"], gr["d_skip"] = dbias.reshape(-1), dalog.reshape(-1), dd.reshape(-1)
    gr["ssd_norm"] = dssd_gain[0]
    dproj, dconv_w, dconv_b = _conv_bwd(dxs, dbm, dcm, sv["proj"], p["conv_w"], row(p["conv_b"]), dproj, f"conv_bwd{l}")
    gr["conv_w"], gr["conv_b"] = dconv_w, dconv_b[0]
    dq, dk_p, dv_p = _att_bwd(sv["aq"], sv["akp"], sv["avp"], bias, dy, f"att_bwd{l}")
    dproj, dqg, dkg = _att_prep_bwd(sv["proj"], dq, dk_p, dv_p, sv["qg"], sv["kg"], cos, sin, bd, dproj, f"att_prep_bwd{l}")
    gr["q_norm"], gr["k_norm"] = dqg[0], dkg[0]
    dproj, dret_gain = _ret_bwd(dy, sv["oraw"], sv["proj"], cos, sin, row(p["ret_norm"]), sv["ret_st"], dproj, f"ret_bwd{l}")
    gr["ret_norm"] = dret_gain[0]
    ddt_cols = ddtr.transpose(1, 0, 2).reshape(S, SSD_HEADS).astype(BF16)
    dproj = lax.dynamic_update_slice(dproj, jnp.pad(ddt_cols, ((0, 0), (0, IN_PAD - DT_COL - SSD_HEADS))), (0, DT_COL))
    gr["w_in"] = _mm(sv["hn"], dproj, "tn", f"in_wgrad{l}", out_dtype=BF16, tm=1024, tn=1152, tk=2048)
    launched = on_all(gr) if on_all is not None else None
    dhn = _mm(dproj, p["w_in"], "nt", f"in_dgrad{l}", tk=1920, after=launched)
    dx0, dx0_bf, dln_mix = _rmsnorm_bwd(sv["x"], dhn, row(p["ln_mix"]), dx1, f"norm_mix_bwd{l}")
    gr["ln_mix"] = dln_mix[0]
    return dx0, dx0_bf, gr


def _local_step(x, tgt, layers, early=None, late=None, on_ffn=None, on_all=None):
    n = len(layers)
    none = [None] * n
    early, late, on_ffn, on_all = early or none, late or none, on_ffn or none, on_all or none
    tabs = _tables(x.shape[0])
    saved, params = [], []
    h = x
    for l in range(n):
        p = dict(layers[l](h) if callable(layers[l]) else layers[l])
        h, sv = _layer_fwd(l, h, p, tabs, early[l], late[l])
        saved.append(sv)
        params.append(p)
    dh, dh_bf, lacc = _loss_grad(h, tgt, "loss_grad")
    grads = [None] * n
    for l in reversed(range(n)):
        dh, dh_bf, grads[l] = _layer_bwd(l, dh, dh_bf, params[l], saved[l], tabs, on_ffn[l], on_all[l])
    return lacc[0, 0], dh, grads


BIG = ("w_in", "w_out", "w_gate", "w_up", "w_down")
SMALL = ("ln_mix", "conv_b", "dt_bias", "a_log", "d_skip", "ssd_norm", "q_norm", "k_norm", "ret_norm", "ln_ffn")
ORDER = ("ln_mix", "w_in", "conv_w", "conv_b", "dt_bias", "a_log", "d_skip", "ssd_norm", "q_norm", "k_norm", "ret_norm",
         "w_out", "ln_ffn", "w_gate", "w_up", "w_down")


COL_SHARDED = ("w_in", "w_gate", "w_up", "conv_w")


IN_GROUPS = ((ORIG_Z_XBC, Z_COL), (ORIG_DT, DT_COL), (ORIG_ATT, ATT_COL), (ORIG_RET, RET_COL))


def _full_weight(k, gathered):
    if k == "w_in":
        cs = gathered.shape[2]
        pieces = []
        for (lo, hi), _ in sorted(IN_GROUPS, key=lambda grp: grp[1]):
            for j in range(N_DEV):
                a, b = max(lo, j * cs), min(hi, (j + 1) * cs)
                if a < b:
                    pieces.append(gathered[j][:, a - j * cs:b - j * cs])
        pieces.append(jnp.zeros((gathered.shape[1], IN_PAD - IN_WIDTH), gathered.dtype))
        return jnp.concatenate(pieces, axis=1)
    if k in COL_SHARDED:
        return gathered.transpose(1, 0, 2).reshape(gathered.shape[1], -1)
    return gathered.reshape(-1, gathered.shape[2])


def _shard_block(k, g):
    if k == "w_in":
        cs = IN_WIDTH // N_DEV
        blocks = []
        for j in range(N_DEV):
            pieces = []
            for (lo, hi), col in IN_GROUPS:
                a, b = max(lo, j * cs), min(hi, (j + 1) * cs)
                if a < b:
                    pieces.append(g[:, col + a - lo:col + b - lo])
            blocks.append(jnp.concatenate(pieces, axis=1))
        return jnp.stack(blocks)
    if k in COL_SHARDED:
        return g.reshape(g.shape[0], N_DEV, -1).transpose(1, 0, 2)
    return g.reshape(N_DEV, -1, g.shape[1])


def kernel(x, ln_mix, w_in, conv_w, conv_b, dt_bias, a_log, d_skip, ssd_norm, q_norm, k_norm, ret_norm, w_out, ln_ffn, w_gate, w_up, w_down, loss_target, m_ln_mix, m_w_in, m_conv_w, m_conv_b, m_dt_bias, m_a_log, m_d_skip, m_ssd_norm, m_q_norm, m_k_norm, m_ret_norm, m_w_out, m_ln_ffn, m_w_gate, m_w_up, m_w_down, v_ln_mix, v_w_in, v_conv_w, v_conv_b, v_dt_bias, v_a_log, v_d_skip, v_ssd_norm, v_q_norm, v_k_norm, v_ret_norm, v_w_out, v_ln_ffn, v_w_gate, v_w_up, v_w_down):
    w = dict(ln_mix=ln_mix, w_in=w_in, conv_w=conv_w, conv_b=conv_b, dt_bias=dt_bias, a_log=a_log, d_skip=d_skip,
             ssd_norm=ssd_norm, q_norm=q_norm, k_norm=k_norm, ret_norm=ret_norm, w_out=w_out, ln_ffn=ln_ffn,
             w_gate=w_gate, w_up=w_up, w_down=w_down)
    m = dict(ln_mix=m_ln_mix, w_in=m_w_in, conv_w=m_conv_w, conv_b=m_conv_b, dt_bias=m_dt_bias, a_log=m_a_log,
             d_skip=m_d_skip, ssd_norm=m_ssd_norm, q_norm=m_q_norm, k_norm=m_k_norm, ret_norm=m_ret_norm, w_out=m_w_out,
             ln_ffn=m_ln_ffn, w_gate=m_w_gate, w_up=m_w_up, w_down=m_w_down)
    v = dict(ln_mix=v_ln_mix, w_in=v_w_in, conv_w=v_conv_w, conv_b=v_conv_b, dt_bias=v_dt_bias, a_log=v_a_log,
             d_skip=v_d_skip, ssd_norm=v_ssd_norm, q_norm=v_q_norm, k_norm=v_k_norm, ret_norm=v_ret_norm, w_out=v_w_out,
             ln_ffn=v_ln_ffn, w_gate=v_w_gate, w_up=v_w_up, w_down=v_w_down)
    me = 4 * lax.axis_index("x") + 2 * lax.axis_index("y") + lax.axis_index("c")

    late_names = ("w_out", "w_gate", "w_up", "w_down")
    waves = {"a": [("w_in", 0), ("conv_w", 0), ("conv_w", 1)], "b": [(k, 0) for k in late_names],
             "c": [("w_in", 1)], "d": [(k, 1) for k in late_names]}
    gather = {}
    behind = 0.0
    for tag, items in waves.items():
        srcs = [w[k][l] if k == "conv_w" else (w[k][l] + behind).astype(BF16) for k, l in items]
        start = _relay_start(srcs, f"gather_{tag}_start") if tag == "a" else _push_start(srcs, False, f"gather_{tag}_start")
        gather[tag] = start
        behind = start[-1][0, 0]
    forwarded = _relay_forward(gather["a"], gather["d"][-1], "gather_a_forward")
    started = forwarded[-1][0, 0]
    full = {}

    def arrive(tag, after):
        if tag == "a":
            lands = _relay_wait(gather["a"], forwarded, after, "gather_a_wait")
        else:
            lands = _push_wait(gather[tag], after, False, f"gather_{tag}_wait")
        for (k, l), g in zip(waves[tag], lands):
            full[k, l] = _full_weight(k, g)

    def layer_weights(l, names):
        return {k: full[k, l] for k in names}

    def small_weights(l):
        return {k: w[k][l] for k in SMALL}

    def layer0(h):
        p = small_weights(0)
        p["ln_mix"] = p["ln_mix"] + started
        return p

    def early0(hn):
        arrive("a", hn)
        return layer_weights(0, ("w_in", "conv_w"))

    def late0(y):
        arrive("b", y)
        return layer_weights(0, late_names)

    def layer1(h):
        arrive("c", h)
        return {**small_weights(1), **layer_weights(1, ("w_in", "conv_w"))}

    def late1(y):
        arrive("d", y)
        return layer_weights(1, late_names)

    groups = {"1": [(k, 1) for k in BIG], "0a": [(k, 0) for k in ("w_down", "w_gate", "w_up")],
              "0b": [(k, 0) for k in ("w_out", "w_in")]}
    scatter = {}

    def push_grads(tag, gr):
        blocks = [_shard_block(k, gr[k]) for k, _ in groups[tag]]
        scatter[tag] = _push_start(blocks, True, f"scatter_{tag}_start")
        return scatter[tag][-1]

    loss_part, gx, grads = _local_step(
        x[0], loss_target[0], [layer0, layer1], early=[early0, None], late=[late0, late1],
        on_ffn=[functools.partial(push_grads, "0a"), None],
        on_all=[functools.partial(push_grads, "0b"), functools.partial(push_grads, "1")])
    loss = lax.psum(loss_part, MESH_AXES)

    out = {}
    recv = {}
    for tag, items in groups.items():
        for item, r in zip(items, _push_wait(scatter[tag], gx, True, f"scatter_{tag}_wait")):
            recv[item] = r
    for k in BIG:
        out[k] = _adamw_layers(recv[k, 0], recv[k, 1], w[k], m[k], v[k], f"adamw_{k}")
    names = SMALL + ("conv_w",)
    sizes = [int(np.prod(grads[0][k].shape)) for k in names]
    packed = jnp.concatenate([jnp.stack([grads[l][k] for l in range(DEPTH)]).reshape(-1) for k in names])
    n_small = packed.shape[0]
    rows_small = -(-n_small // 1024) * 8
    pad = lambda t, fill: jnp.concatenate([t, jnp.full((rows_small * 128 - n_small,), fill, F32)]).reshape(rows_small, 128)
    parts = _exchange([pad(packed, 0.0)], False, "gather_small_grads")[0]
    n_rep = DEPTH * sum(sizes[:-1])
    pack_rep = lambda d, fill: pad(jnp.concatenate([d[k].reshape(-1) for k in SMALL]
                                                   + [jnp.full((n_small - n_rep,), fill, F32)]), fill)
    res = _adamw(parts, pack_rep(w, 1.0), pack_rep(m, 1.0), pack_rep(v, 1.0), "adamw_small")
    res = [t.reshape(-1) for t in res]
    off = 0
    for k, sz in zip(SMALL, sizes[:-1]):
        out[k] = [t[off:off + DEPTH * sz].reshape(w[k].shape) for t in res]
        off += DEPTH * sz
    gconv = res[0][off:off + DEPTH * sizes[-1]].reshape(DEPTH, SSD_CONV, SSD_CONV_CH)
    gconv = lax.dynamic_slice_in_dim(gconv, me * conv_w.shape[2], conv_w.shape[2], axis=2)
    flat = lambda t: t.reshape(8, -1)
    resc = _adamw(flat(gconv)[None], flat(conv_w), flat(m_conv_w), flat(v_conv_w), "adamw_conv_w")
    out["conv_w"] = [t.reshape(conv_w.shape) for t in resc]

    return (loss, gx[None], *[out[k][0] for k in ORDER], *[out[k][1] for k in ORDER],
            *[out[k][2] for k in ORDER], *[out[k][3] for k in ORDER])
```

```python
import functools
import math

import jax
import jax.numpy as jnp
import numpy as np
from jax import lax
from jax.experimental import pallas as pl
from jax.experimental.pallas import tpu as pltpu

F32 = jnp.float32
BF16 = jnp.bfloat16

N_DEV = 8
MESH_AXES = ("x", "y", "c")
D_MODEL = 2048
DEPTH = 2
EPS = 1e-6
ROPE_THETA = 10000.0
SSD_HEADS = 16
SSD_HEAD_DIM = 64
SSD_WIDTH = 1024
SSD_GROUPS = 2
SSD_STATE = 128
SSD_CONV = 4
SSD_CONV_CH = 1536
ATT_HEADS = 8
ATT_HEAD_DIM = 64
ATT_WIDTH = 512
DILATED_PAIRS = ((128, 1), (512, 4), (2048, 16))
RET_HEADS = 4
RET_QK_DIM = 64
RET_V_DIM = 128
RET_QK_WIDTH = 256
RET_V_WIDTH = 512
CHUNK = 128
MIX_WIDTH = 2048
ATT_SPAN = 2048
ATT_STRIP = ATT_SPAN + CHUNK
ATT_QB = 4
IN_WIDTH = 5648
IN_PAD = 5760
RET_COL, ATT_COL, Z_COL, XBC_COL, DT_COL = 0, 1536, 3072, 4096, 5632
ORIG_Z_XBC, ORIG_DT, ORIG_ATT, ORIG_RET = (0, 2560), (2560, 2576), (2576, 4112), (4112, 5648)
D_FF = 5632
ADAM_LR = 0.001
ADAM_B1 = 0.9
ADAM_B2 = 0.999
ADAM_EPS = 1e-08
ADAM_WD = 0.01
ADAM_STEP = 10
NEG = -1e30
VMEM_LIMIT_V7X = 60 * 1024 * 1024

NN = (((1,), (0,)), ((), ()))
NT = (((1,), (1,)), ((), ()))
TN = (((0,), (0,)), ((), ()))


def _bdot(a, b, dims):
    return lax.dot_general(a.astype(BF16), b.astype(BF16), dims, preferred_element_type=F32)


def _xdot(a, b, dims, exact_first=False):
    ones, x = (a, b) if exact_first else (b, a)
    ones = ones.astype(BF16)
    acc, rest = None, x
    for _ in range(3):
        piece = rest.astype(BF16)
        rest = rest - piece.astype(F32)
        part = lax.dot_general(*((ones, piece) if exact_first else (piece, ones)), dims, preferred_element_type=F32)
        acc = part if acc is None else acc + part
    return acc


def _params(*sem):
    return pltpu.CompilerParams(dimension_semantics=sem, vmem_limit_bytes=VMEM_LIMIT_V7X)


def _sigmoid(v):
    return 1.0 / (1.0 + jnp.exp(-v))


def _silu_grad(v, s):
    return s * (1.0 + v * (1.0 - s))


def _rmsnorm_fwd(x, g, name):
    S, D = x.shape
    tr = min(512, S)

    def body(x_ref, g_ref, o_ref):
        xv = x_ref[...]
        r = lax.rsqrt(jnp.mean(xv * xv, axis=-1, keepdims=True) + EPS)
        o_ref[...] = (xv * r * g_ref[...]).astype(o_ref.dtype)

    return pl.pallas_call(
        body, grid=(S // tr,),
        in_specs=[pl.BlockSpec((tr, D), lambda i: (i, 0)), pl.BlockSpec((1, D), lambda i: (0, 0))],
        out_specs=pl.BlockSpec((tr, D), lambda i: (i, 0)),
        out_shape=jax.ShapeDtypeStruct((S, D), BF16), name=name, compiler_params=_params("parallel"),
    )(x, g)


def _rmsnorm_bwd(x, dy, g, dres, name):
    S, D = x.shape
    tr = min(512, S)

    def body(x_ref, dy_ref, g_ref, dres_ref, dx_ref, dxb_ref, dg_ref):
        i = pl.program_id(0)
        xv = x_ref[...]
        r = lax.rsqrt(jnp.mean(xv * xv, axis=-1, keepdims=True) + EPS)
        n = xv * r
        dy = dy_ref[...]
        dn = dy * g_ref[...]
        dx = dres_ref[...] + r * (dn - n * jnp.mean(dn * n, axis=-1, keepdims=True))
        dx_ref[...] = dx
        dxb_ref[...] = dx.astype(BF16)
        part = jnp.sum(dy * n, axis=0, keepdims=True)

        @pl.when(i == 0)
        def _():
            dg_ref[...] = part

        @pl.when(i > 0)
        def _():
            dg_ref[...] += part

    row = pl.BlockSpec((tr, D), lambda i: (i, 0))
    vec = pl.BlockSpec((1, D), lambda i: (0, 0))
    return pl.pallas_call(
        body, grid=(S // tr,), in_specs=[row, row, vec, row], out_specs=[row, row, vec],
        out_shape=[jax.ShapeDtypeStruct((S, D), F32), jax.ShapeDtypeStruct((S, D), BF16), jax.ShapeDtypeStruct((1, D), F32)],
        name=name, compiler_params=_params("arbitrary"),
    )(x, dy, g, dres)


def _loss_grad(y, tgt, name):
    S, D = y.shape
    tr = min(512, S)

    def body(y_ref, t_ref, dy_ref, dyb_ref, l_ref):
        i = pl.program_id(0)
        err = y_ref[...] - t_ref[...]
        dy = err * (1.0 / D)
        dy_ref[...] = dy
        dyb_ref[...] = dy.astype(BF16)
        part = jnp.sum(jnp.sum(err * err, axis=1, keepdims=True), axis=0, keepdims=True) * (0.5 / D)

        @pl.when(i == 0)
        def _():
            l_ref[...] = jnp.zeros_like(l_ref)

        l_ref[...] += part

    row = pl.BlockSpec((tr, D), lambda i: (i, 0))
    return pl.pallas_call(
        body, grid=(S // tr,), in_specs=[row, row],
        out_specs=[row, row, pl.BlockSpec((8, 128), lambda i: (0, 0))],
        out_shape=[jax.ShapeDtypeStruct((S, D), F32), jax.ShapeDtypeStruct((S, D), BF16), jax.ShapeDtypeStruct((8, 128), F32)],
        name=name, compiler_params=_params("arbitrary"),
    )(y, tgt)


def _pick(n, cands):
    for c in cands:
        if n % c == 0:
            return c
    return n


def _mm(a, b, mode, name, out_dtype=F32, residual=None, tm=None, tn=None, tk=None, after=None):
    if mode == "nn":
        (M, K), (_, N) = a.shape, b.shape
    elif mode == "nt":
        (M, K), (N, _) = a.shape, b.shape
    else:
        (K, M), (_, N) = a.shape, b.shape
    tm = min(tm, M) if tm else _pick(M, (1024, 512, 256, 128))
    tn = min(tn, N) if tn else _pick(N, (1024, 1152, 1408, 512, 256, 128))
    tk = min(tk, K) if tk else _pick(K, (2048, 1920, 1408, 1024, 512, 256, 128))
    assert M % tm == 0 and N % tn == 0 and K % tk == 0, (name, M, N, K, tm, tn, tk)
    nk = K // tk
    a_spec = pl.BlockSpec((tk, tm), lambda i, j, k: (k, i)) if mode == "tn" else pl.BlockSpec((tm, tk), lambda i, j, k: (i, k))
    b_spec = pl.BlockSpec((tn, tk), lambda i, j, k: (j, k)) if mode == "nt" else pl.BlockSpec((tk, tn), lambda i, j, k: (k, j))
    o_spec = pl.BlockSpec((tm, tn), lambda i, j, k: (i, j))
    dims = {"nn": NN, "nt": NT, "tn": TN}[mode]
    has_res = residual is not None

    has_after = after is not None

    def body(*refs):
        a_ref, b_ref = refs[0], refs[1]
        r_ref = refs[2] if has_res else None
        o_ref = refs[2 + has_res + has_after]
        p = _bdot(a_ref[...], b_ref[...], dims)

        def finish(acc):
            if has_res:
                acc = acc + r_ref[...]
            o_ref[...] = acc.astype(o_ref.dtype)

        if nk == 1:
            finish(p)
        else:
            acc_ref = refs[-1]
            k = pl.program_id(2)

            @pl.when(k == 0)
            def _():
                acc_ref[...] = p

            @pl.when(k > 0)
            def _():
                acc_ref[...] += p

            @pl.when(k == nk - 1)
            def _():
                finish(acc_ref[...])

    ins = [a, b] + ([residual] if has_res else []) + ([after] if has_after else [])
    in_specs = [a_spec, b_spec] + ([o_spec] if has_res else []) + ([pl.BlockSpec(memory_space=pl.ANY)] if has_after else [])
    scratch = [pltpu.VMEM((tm, tn), F32)] if nk > 1 else []
    return pl.pallas_call(
        body, grid=(M // tm, N // tn, nk), in_specs=in_specs, out_specs=o_spec,
        out_shape=jax.ShapeDtypeStruct((M, N), out_dtype), scratch_shapes=scratch, name=name,
        compiler_params=_params("parallel", "parallel", "arbitrary"),
    )(*ins)


def _accumulate(acc_ref, p, k, nk, finish):
    @pl.when(k == 0)
    def _():
        acc_ref[...] = p

    @pl.when(k > 0)
    def _():
        acc_ref[...] += p

    @pl.when(k == nk - 1)
    def _():
        finish(acc_ref[...])


def _swiglu_fwd(hn, wg, wu, name):
    S, K = hn.shape
    F = wg.shape[1]
    tm = _pick(S, (1024, 512))
    tn = _pick(F, (512, 256, 128))

    def body(a_ref, wg_ref, wu_ref, g_ref, u_ref, act_ref):
        a = a_ref[...]
        g = _bdot(a, wg_ref[...], NN)
        u = _bdot(a, wu_ref[...], NN)
        g_ref[...] = g.astype(BF16)
        u_ref[...] = u.astype(BF16)
        act_ref[...] = (g * _sigmoid(g) * u).astype(BF16)

    w_spec = pl.BlockSpec((K, tn), lambda i, j: (0, j))
    o_spec = pl.BlockSpec((tm, tn), lambda i, j: (i, j))
    sh = jax.ShapeDtypeStruct((S, F), BF16)
    return pl.pallas_call(
        body, grid=(S // tm, F // tn), in_specs=[pl.BlockSpec((tm, K), lambda i, j: (i, 0)), w_spec, w_spec],
        out_specs=[o_spec, o_spec, o_spec], out_shape=[sh, sh, sh], name=name,
        compiler_params=_params("parallel", "parallel"),
    )(hn, wg, wu)


def _swiglu_bwd(dx, wd, g, u, name):
    S, K = dx.shape
    F = wd.shape[0]
    tm = _pick(S, (1024, 512))
    tn = _pick(F, (512, 256, 128))

    rc = min(256, tm)

    def body(dx_ref, wd_ref, g_ref, u_ref, dg_ref, du_ref):
        wv = wd_ref[...]
        for r in range(tm // rc):
            rows = pl.ds(r * rc, rc)
            da = _bdot(dx_ref[rows, :], wv, NT)
            gv = g_ref[rows, :].astype(F32)
            uv = u_ref[rows, :].astype(F32)
            s = _sigmoid(gv)
            dg_ref[rows, :] = (da * uv * _silu_grad(gv, s)).astype(BF16)
            du_ref[rows, :] = (da * gv * s).astype(BF16)

    o_spec = pl.BlockSpec((tm, tn), lambda i, j: (i, j))
    sh = jax.ShapeDtypeStruct((S, F), BF16)
    return pl.pallas_call(
        body, grid=(S // tm, F // tn),
        in_specs=[pl.BlockSpec((tm, K), lambda i, j: (i, 0)), pl.BlockSpec((tn, K), lambda i, j: (j, 0)), o_spec, o_spec],
        out_specs=[o_spec, o_spec], out_shape=[sh, sh], name=name, compiler_params=_params("parallel", "parallel"),
    )(dx, wd, g, u)


def _mm_nt2(a1, b1, a2, b2, name):
    M, K = a1.shape
    N = b1.shape[0]
    tm = _pick(M, (1024, 512))
    tn = _pick(N, (1024, 512))
    tk = _pick(K, (1408, 1024, 512, 256, 128))
    nk = K // tk

    def body(a1_ref, b1_ref, a2_ref, b2_ref, o_ref, acc_ref):
        def finish(acc):
            o_ref[...] = acc

        p = _bdot(a1_ref[...], b1_ref[...], NT) + _bdot(a2_ref[...], b2_ref[...], NT)
        _accumulate(acc_ref, p, pl.program_id(2), nk, finish)

    a_spec = pl.BlockSpec((tm, tk), lambda i, j, k: (i, k))
    b_spec = pl.BlockSpec((tn, tk), lambda i, j, k: (j, k))
    return pl.pallas_call(
        body, grid=(M // tm, N // tn, nk), in_specs=[a_spec, b_spec, a_spec, b_spec],
        out_specs=pl.BlockSpec((tm, tn), lambda i, j, k: (i, j)), out_shape=jax.ShapeDtypeStruct((M, N), F32),
        scratch_shapes=[pltpu.VMEM((tm, tn), F32)], name=name,
        compiler_params=_params("parallel", "parallel", "arbitrary"),
    )(a1, b1, a2, b2)


XBC_BLK0 = XBC_COL // 128


def _conv_fwd(proj, w, b, name):
    S = proj.shape[0]
    T = min(512, S)

    def body(x_ref, w_ref, b_ref, o_ref, xp_ref):
        xp_ref[pl.ds(0, 8), :] = jnp.zeros((8, 128), F32)
        xp_ref[pl.ds(8, S), :] = x_ref[...]
        wv = w_ref[...]
        bv = b_ref[...]

        def step(c, carry):
            base = pl.multiple_of(c * T, T)
            acc = wv[0:1] * xp_ref[pl.ds(base + 5, T), :]
            for i in range(1, SSD_CONV):
                acc = acc + wv[i:i + 1] * xp_ref[pl.ds(base + 5 + i, T), :]
            acc = bv + acc
            o_ref[pl.ds(base, T), :] = acc * _sigmoid(acc)
            return carry

        lax.fori_loop(0, S // T, step, 0)

    return pl.pallas_call(
        body, grid=(SSD_CONV_CH // 128,),
        in_specs=[pl.BlockSpec((S, 128), lambda j: (0, XBC_BLK0 + j)), pl.BlockSpec((SSD_CONV, 128), lambda j: (0, j)),
                  pl.BlockSpec((1, 128), lambda j: (0, j))],
        out_specs=pl.BlockSpec((S, 128), lambda j: (0, j)),
        out_shape=jax.ShapeDtypeStruct((S, SSD_CONV_CH), F32),
        scratch_shapes=[pltpu.VMEM((S + 8, 128), F32)], name=name, compiler_params=_params("parallel"),
    )(proj, w, b)


def _conv_bwd(dxs, dbm, dcm, proj, w, b, dproj, name):
    S = proj.shape[0]
    T = min(512, S)
    NX, NB = SSD_WIDTH // 128, SSD_GROUPS * SSD_STATE // 128

    def body(dxs_ref, dbm_ref, dcm_ref, x_ref, w_ref, b_ref, dproj_ref, dx_ref, dw_ref, db_ref, xp_ref, dcp_ref):
        j = pl.program_id(0)

        @pl.when(j < NX)
        def _():
            dcp_ref[pl.ds(0, S), :] = dxs_ref[...]

        @pl.when((j >= NX) & (j < NX + NB))
        def _():
            dcp_ref[pl.ds(0, S), :] = dbm_ref[...]

        @pl.when(j >= NX + NB)
        def _():
            dcp_ref[pl.ds(0, S), :] = dcm_ref[...]

        da_ref = dcp_ref
        xp_ref[pl.ds(0, 8), :] = jnp.zeros((8, 128), F32)
        xp_ref[pl.ds(8, S), :] = x_ref[...]
        dcp_ref[pl.ds(S, 8), :] = jnp.zeros((8, 128), F32)
        wv = w_ref[...]
        bv = b_ref[...]

        def step1(c, carry):
            base = pl.multiple_of(c * T, T)
            xs = [xp_ref[pl.ds(base + 5 + i, T), :] for i in range(SSD_CONV)]
            acc = wv[0:1] * xs[0]
            for i in range(1, SSD_CONV):
                acc = acc + wv[i:i + 1] * xs[i]
            acc = bv + acc
            s = _sigmoid(acc)
            dc = da_ref[pl.ds(base, T), :] * _silu_grad(acc, s)
            dcp_ref[pl.ds(base, T), :] = dc
            new = tuple(carry[i] + jnp.sum(xs[i] * dc, axis=0, keepdims=True) for i in range(SSD_CONV))
            return new + (carry[SSD_CONV] + jnp.sum(dc, axis=0, keepdims=True),)

        z = jnp.zeros((1, 128), F32)
        res = lax.fori_loop(0, S // T, step1, (z,) * (SSD_CONV + 1))
        for i in range(SSD_CONV):
            dw_ref[pl.ds(i, 1), :] = res[i]
        db_ref[...] = res[SSD_CONV]

        def step2(c, carry):
            base = pl.multiple_of(c * T, T)
            acc = wv[0:1] * dcp_ref[pl.ds(base + 3, T), :]
            for i in range(1, SSD_CONV):
                acc = acc + wv[i:i + 1] * dcp_ref[pl.ds(base + 3 - i, T), :]
            dx_ref[pl.ds(base, T), :] = acc.astype(dx_ref.dtype)
            return carry

        lax.fori_loop(0, S // T, step2, 0)

    clamp = lambda j, lo, n: jnp.clip(j - lo, 0, n - 1)
    return pl.pallas_call(
        body, grid=(SSD_CONV_CH // 128,),
        in_specs=[pl.BlockSpec((S, 128), lambda j: (0, clamp(j, 0, NX))), pl.BlockSpec((S, 128), lambda j: (0, clamp(j, NX, NB))),
                  pl.BlockSpec((S, 128), lambda j: (0, clamp(j, NX + NB, NB))),
                  pl.BlockSpec((S, 128), lambda j: (0, XBC_BLK0 + j)), pl.BlockSpec((SSD_CONV, 128), lambda j: (0, j)),
                  pl.BlockSpec((1, 128), lambda j: (0, j)), pl.BlockSpec(memory_space=pl.ANY)],
        out_specs=[pl.BlockSpec((S, 128), lambda j: (0, XBC_BLK0 + j)), pl.BlockSpec((SSD_CONV, 128), lambda j: (0, j)),
                   pl.BlockSpec((1, 128), lambda j: (0, j))],
        out_shape=[jax.ShapeDtypeStruct(dproj.shape, dproj.dtype), jax.ShapeDtypeStruct((SSD_CONV, SSD_CONV_CH), F32),
                   jax.ShapeDtypeStruct((1, SSD_CONV_CH), F32)],
        input_output_aliases={6: 0},
        scratch_shapes=[pltpu.VMEM((S + 8, 128), F32), pltpu.VMEM((S + 8, 128), F32)], name=name,
        compiler_params=_params("arbitrary"),
    )(dxs, dbm, dcm, proj, w, b, dproj)


HPG = SSD_HEADS // SSD_GROUPS
GW = HPG * SSD_HEAD_DIM


def _ssd_chunk_terms(dtr, bias, alog, tril, triu):
    pre = dtr + bias
    dt = jnp.maximum(pre, 0.0) + jnp.log(1.0 + jnp.exp(-jnp.abs(pre)))
    a_neg = -jnp.exp(alog)
    a = dt * a_neg
    acum = _xdot(tril, a, NN, exact_first=True)
    acum_t = _xdot(a, triu, TN)
    return pre, dt, a_neg, acum, acum_t


def _head_expanders():
    h64 = lax.broadcasted_iota(jnp.int32, (HPG, GW), 0) == lax.broadcasted_iota(jnp.int32, (HPG, GW), 1) // SSD_HEAD_DIM
    h128 = lax.broadcasted_iota(jnp.int32, (HPG, HPG * CHUNK), 0) == lax.broadcasted_iota(jnp.int32, (HPG, HPG * CHUNK), 1) // CHUNK
    return h64.astype(F32), h128.astype(F32)


def _ssd_fwd(xbc, proj, dtr, dt_bias, a_log, d_rep, gain, name):
    S = xbc.shape[0]
    L = CHUNK
    T = min(512, S)
    CPS = T // L
    NC = S // L

    def body(x_ref, b_ref, c_ref, z_ref, dtr_ref, bias_ref, alog_ref, d_ref, gain_ref, y_ref, yraw_ref, st_ref, state):
        i = pl.program_id(1)

        @pl.when(i == 0)
        def _():
            state[...] = jnp.zeros_like(state)

        row = lax.broadcasted_iota(jnp.int32, (L, L), 0)
        col = lax.broadcasted_iota(jnp.int32, (L, L), 1)
        causal = row >= col
        tril = causal.astype(F32)
        triu = (row <= col).astype(F32)
        low = col < SSD_HEAD_DIM
        e64, e128 = _head_expanders()
        for c in range(CPS):
            rows = pl.ds(c * L, L)
            xv = x_ref[rows, :]
            bm = b_ref[rows, :]
            cm = c_ref[rows, :]
            _, dt, _, acum, acum_t = _ssd_chunk_terms(dtr_ref[rows, :], bias_ref[...], alog_ref[...], tril, triu)
            ac = _xdot(acum, e64, NN)
            ac_sq = _xdot(acum, e128, NN)
            xd = xv * _xdot(dt, e64, NN)
            ac_last = ac[L - 1:L, :]
            sp = state[...]
            st_ref[c] = sp
            yoff = _bdot(cm, sp, NN) * jnp.exp(ac)
            state[...] = sp * jnp.exp(ac_last) + _bdot(bm, xd * jnp.exp(ac_last - ac), TN)
            gmat = _bdot(cm, bm, NT)
            for q in range(HPG // 2):
                pair = slice(q * 128, (q + 1) * 128)
                tile = xd[:, pair]
                y = yoff[:, pair]
                for j, keep in ((2 * q, low), (2 * q + 1, ~low)):
                    lam = jnp.exp(jnp.where(causal, ac_sq[:, j * L:(j + 1) * L] - acum_t[j:j + 1, :], NEG))
                    y = y + _bdot(gmat * lam, jnp.where(keep, tile, 0.0), NN)
                yraw_ref[rows, pair] = y
            zz = z_ref[rows, :]
            u = (yraw_ref[rows, :] + xv * d_ref[...]) * (zz * _sigmoid(zz))
            r = lax.rsqrt(jnp.mean(u * u, axis=-1, keepdims=True) + EPS)
            y_ref[rows, :] = (u * r * gain_ref[...]).astype(y_ref.dtype)

    vec8 = pl.BlockSpec((None, 1, HPG), lambda g, i: (g, 0, 0))
    return pl.pallas_call(
        body, grid=(SSD_GROUPS, S // T),
        in_specs=[pl.BlockSpec((T, GW), lambda g, i: (i, g)),
                  pl.BlockSpec((T, SSD_STATE), lambda g, i: (i, SSD_WIDTH // SSD_STATE + g)),
                  pl.BlockSpec((T, SSD_STATE), lambda g, i: (i, SSD_WIDTH // SSD_STATE + SSD_GROUPS + g)),
                  pl.BlockSpec((T, GW), lambda g, i: (i, Z_COL // GW + g)),
                  pl.BlockSpec((None, T, HPG), lambda g, i: (g, i, 0)),
                  vec8, vec8,
                  pl.BlockSpec((1, GW), lambda g, i: (0, g)), pl.BlockSpec((1, GW), lambda g, i: (0, g))],
        out_specs=[pl.BlockSpec((T, GW), lambda g, i: (i, g)), pl.BlockSpec((T, GW), lambda g, i: (i, g)),
                   pl.BlockSpec((CPS, None, SSD_STATE, GW), lambda g, i: (i, g, 0, 0))],
        out_shape=[jax.ShapeDtypeStruct((S, MIX_WIDTH), BF16), jax.ShapeDtypeStruct((S, SSD_WIDTH), F32),
                   jax.ShapeDtypeStruct((NC, SSD_GROUPS, SSD_STATE, GW), F32)],
        scratch_shapes=[pltpu.VMEM((SSD_STATE, GW), F32)], name=name,
        compiler_params=_params("arbitrary", "arbitrary"),
    )(xbc, xbc, xbc, proj, dtr, dt_bias, a_log, d_rep, gain)


def _ssd_bwd(dy, yraw, xbc, proj, dtr, dt_bias, a_log, d_rep, gain, states, name):
    S = xbc.shape[0]
    L = CHUNK
    T = min(512, S)
    CPS = T // L
    NI = S // T

    def body(dy_ref, yraw_ref, x_ref, b_ref, c_ref, z_ref, dtr_ref, bias_ref, alog_ref, d_ref, gain_ref, st_ref,
             dz_ref, dx_ref, db_ref, dc_ref, ddtr_ref, dbias_ref, dalog_ref, dd_ref, dgain_ref, dstate, dxd_ref):
        i = pl.program_id(1)

        @pl.when(i == 0)
        def _():
            dstate[...] = jnp.zeros_like(dstate)
            dbias_ref[...] = jnp.zeros_like(dbias_ref)
            dalog_ref[...] = jnp.zeros_like(dalog_ref)
            dd_ref[...] = jnp.zeros_like(dd_ref)
            dgain_ref[...] = jnp.zeros_like(dgain_ref)

        row = lax.broadcasted_iota(jnp.int32, (L, L), 0)
        col = lax.broadcasted_iota(jnp.int32, (L, L), 1)
        causal = row >= col
        tril = causal.astype(F32)
        triu = (row <= col).astype(F32)
        low = col < SSD_HEAD_DIM
        e64, e128 = _head_expanders()
        lane8 = lax.broadcasted_iota(jnp.int32, (1, HPG), 1)
        sub8 = lax.broadcasted_iota(jnp.int32, (HPG, 1), 0)
        eye8 = (lax.broadcasted_iota(jnp.int32, (HPG, HPG), 0) == lax.broadcasted_iota(jnp.int32, (HPG, HPG), 1)).astype(F32)
        last_row = (lax.broadcasted_iota(jnp.int32, (L, 1), 0) == L - 1).astype(F32)
        for c in reversed(range(CPS)):
            rows = pl.ds(c * L, L)
            xv = x_ref[rows, :]
            bm = b_ref[rows, :]
            cm = c_ref[rows, :]
            zz = z_ref[rows, :]
            dvec = d_ref[...]
            sz = _sigmoid(zz)
            silu_z = zz * sz
            v = yraw_ref[rows, :] + xv * dvec
            u = v * silu_z
            r = lax.rsqrt(jnp.mean(u * u, axis=-1, keepdims=True) + EPS)
            n = u * r
            do = dy_ref[rows, :]
            dgain_ref[...] += jnp.sum(do * n, axis=0, keepdims=True)
            dn = do * gain_ref[...]
            du = r * (dn - n * jnp.mean(dn * n, axis=-1, keepdims=True))
            dz_ref[rows, :] = (du * v * _silu_grad(zz, sz)).astype(dz_ref.dtype)
            dyv = du * silu_z
            dd_ref[...] += _xdot(jnp.sum(dyv * xv, axis=0, keepdims=True), e64, NT)
            pre, dt, a_neg, acum, acum_t = _ssd_chunk_terms(dtr_ref[rows, :], bias_ref[...], alog_ref[...], tril, triu)
            ac = _xdot(acum, e64, NN)
            ac_sq = _xdot(acum, e128, NN)
            dt_w = _xdot(dt, e64, NN)
            xd = xv * dt_w
            ac_last = ac[L - 1:L, :]
            ea = jnp.exp(ac)
            w = jnp.exp(ac_last - ac)
            ea_last = jnp.exp(ac_last)
            sp = st_ref[c]
            ds = dstate[...]
            dye = dyv * ea
            yoff = _bdot(cm, sp, NN) * ea
            bds = _bdot(bm, ds, NN)
            dcm = _bdot(dye, sp, NT)
            dbm = _bdot(xd * w, ds, NT)
            dstate[...] = ds * ea_last + _bdot(cm, dye, TN)
            w8 = jnp.exp(acum[L - 1:L, :] - acum)
            dw8 = _xdot(xd * bds, e64, NT)
            dac8 = _xdot(dyv * yoff, e64, NT) - dw8 * w8
            tail8 = jnp.sum(dw8 * w8, axis=0, keepdims=True) + jnp.exp(acum[L - 1:L, :]) * _xdot(
                jnp.sum(ds * sp, axis=0, keepdims=True), e64, NT)
            dac8 = dac8 + last_row * tail8
            gmat = _bdot(cm, bm, NT)
            dgmat = jnp.zeros((L, L), F32)
            colsum_t = jnp.zeros((HPG, L), F32)
            for q in range(HPG // 2):
                pair = slice(q * 128, (q + 1) * 128)
                xd_tile = xd[:, pair]
                dy_tile = dyv[:, pair]
                dxd_tile = bds[:, pair] * w[:, pair]
                for j, keep in ((2 * q, low), (2 * q + 1, ~low)):
                    lam = jnp.exp(jnp.where(causal, ac_sq[:, j * L:(j + 1) * L] - acum_t[j:j + 1, :], NEG))
                    mh = gmat * lam
                    dyj = jnp.where(keep, dy_tile, 0.0)
                    dxd_tile = dxd_tile + _bdot(mh, dyj, TN)
                    dm = _bdot(dyj, xd_tile, NT)
                    dgmat = dgmat + dm * lam
                    qm = dm * mh
                    dac8 = dac8 + jnp.sum(qm, axis=1, keepdims=True) * (lane8 == j).astype(F32)
                    colsum_t = colsum_t + (sub8 == j).astype(F32) * jnp.sum(qm, axis=0, keepdims=True)
                dxd_ref[:, pair] = dxd_tile
            dac8 = dac8 - _xdot(colsum_t, eye8, TN)
            dxd = dxd_ref[...]
            dx_ref[rows, :] = dxd * dt_w + dyv * dvec
            dc_ref[rows, :] = dcm + _bdot(dgmat, bm, NN)
            db_ref[rows, :] = dbm + _bdot(dgmat, cm, TN)
            da8 = _xdot(triu, dac8, NN, exact_first=True)
            ddt8 = _xdot(dxd * xv, e64, NT) + da8 * a_neg
            dalog_ref[...] += jnp.sum(da8 * dt, axis=0, keepdims=True) * a_neg
            dpre = ddt8 * _sigmoid(pre)
            ddtr_ref[rows, :] = dpre
            dbias_ref[...] += jnp.sum(dpre, axis=0, keepdims=True)

    rev = lambda i: NI - 1 - i
    vec8 = pl.BlockSpec((None, 1, HPG), lambda g, i: (g, 0, 0))
    grp = pl.BlockSpec((T, GW), lambda g, i: (rev(i), g))
    bspec = pl.BlockSpec((T, SSD_STATE), lambda g, i: (rev(i), SSD_WIDTH // SSD_STATE + g))
    cspec = pl.BlockSpec((T, SSD_STATE), lambda g, i: (rev(i), SSD_WIDTH // SSD_STATE + SSD_GROUPS + g))
    gvec = pl.BlockSpec((1, GW), lambda g, i: (0, g))
    st_spec = pl.BlockSpec((CPS, None, SSD_STATE, GW), lambda g, i: (rev(i), g, 0, 0))
    small = jax.ShapeDtypeStruct((SSD_GROUPS, 1, HPG), F32)
    zspec = pl.BlockSpec((T, GW), lambda g, i: (rev(i), Z_COL // GW + g))
    return pl.pallas_call(
        body, grid=(SSD_GROUPS, NI),
        in_specs=[grp, grp, grp, bspec, cspec, zspec, pl.BlockSpec((None, T, HPG), lambda g, i: (g, rev(i), 0)),
                  vec8, vec8, gvec, gvec, st_spec],
        out_specs=[zspec, grp, pl.BlockSpec((T, SSD_STATE), lambda g, i: (rev(i), g)),
                   pl.BlockSpec((T, SSD_STATE), lambda g, i: (rev(i), g)),
                   pl.BlockSpec((None, T, HPG), lambda g, i: (g, rev(i), 0)), vec8, vec8, vec8, gvec],
        out_shape=[jax.ShapeDtypeStruct((S, IN_PAD), BF16), jax.ShapeDtypeStruct((S, SSD_WIDTH), F32),
                   jax.ShapeDtypeStruct((S, SSD_GROUPS * SSD_STATE), F32), jax.ShapeDtypeStruct((S, SSD_GROUPS * SSD_STATE), F32),
                   jax.ShapeDtypeStruct((SSD_GROUPS, S, HPG), F32), small, small, small,
                   jax.ShapeDtypeStruct((1, SSD_WIDTH), F32)],
        scratch_shapes=[pltpu.VMEM((SSD_STATE, GW), F32), pltpu.VMEM((L, GW), F32)],
        name=name, compiler_params=_params("arbitrary", "arbitrary"),
    )(dy, yraw, xbc, xbc, xbc, proj, dtr, dt_bias, a_log, d_rep, gain, states)


def _swap_halves(t):
    w = t.shape[1]
    lane = lax.broadcasted_iota(jnp.int32, t.shape, 1)
    return jnp.where((lane % 64) < 32, pltpu.roll(t, w - 32, axis=1), pltpu.roll(t, 32, axis=1))


def _widen(tab, w):
    return tab if w == 128 else jnp.concatenate([tab] * (w // 128), axis=1)


def _rope(t, cos, sin_signed):
    return t * cos + _swap_halves(t) * sin_signed


def _rope_t(d, cos, sin_signed):
    return d * cos - _swap_halves(d) * sin_signed


def _group_sum64(v, bd):
    hi = v.astype(BF16)
    lo = (v - hi.astype(F32)).astype(BF16)
    return (lax.dot_general(hi, bd, NN, preferred_element_type=F32)
            + lax.dot_general(lo, bd, NN, preferred_element_type=F32))


AQ_BLK = ATT_COL // ATT_WIDTH


def _att_prep_fwd(proj, qg, kg, cos, sin, bd, name):
    S = proj.shape[0]
    T = min(512, S)
    PB = ATT_SPAN // T
    src = lambda i: jnp.maximum(i - PB, 0)

    def body(q_ref, k_ref, v_ref, qg_ref, kg_ref, cos_ref, sin_ref, bd_ref, qo_ref, ko_ref, vo_ref):
        i = pl.program_id(0)

        @pl.when(i < PB)
        def _():
            ko_ref[...] = jnp.zeros_like(ko_ref)
            vo_ref[...] = jnp.zeros_like(vo_ref)

        @pl.when(i >= PB)
        def _():
            cw = _widen(cos_ref[...], ATT_WIDTH)
            sw = _widen(sin_ref[...], ATT_WIDTH)
            bdv = bd_ref[...]

            def norm_rope(t, gain):
                ss = _group_sum64(t * t, bdv)
                return _rope(t * lax.rsqrt(ss * (1.0 / ATT_HEAD_DIM) + EPS) * gain, cw, sw)

            qo_ref[...] = (norm_rope(q_ref[...], qg_ref[...]) * (ATT_HEAD_DIM ** -0.5)).astype(BF16)
            kt = norm_rope(k_ref[...], kg_ref[...]).astype(BF16)
            vt = v_ref[...].astype(BF16)
            for pr in range(ATT_HEADS // 2):
                ko_ref[pr] = kt[:, pr * 128:(pr + 1) * 128]
                vo_ref[pr] = vt[:, pr * 128:(pr + 1) * 128]

    vec = pl.BlockSpec((1, ATT_WIDTH), lambda i: (0, 0))
    tab = pl.BlockSpec((T, 128), lambda i: (src(i), 0))
    hm = pl.BlockSpec((ATT_HEADS // 2, T, 128), lambda i: (0, i, 0))
    hm_shape = jax.ShapeDtypeStruct((ATT_HEADS // 2, ATT_SPAN + S, 128), BF16)
    return pl.pallas_call(
        body, grid=(PB + S // T,),
        in_specs=[pl.BlockSpec((T, ATT_WIDTH), lambda i: (src(i), AQ_BLK)), pl.BlockSpec((T, ATT_WIDTH), lambda i: (src(i), AQ_BLK + 1)),
                  pl.BlockSpec((T, ATT_WIDTH), lambda i: (src(i), AQ_BLK + 2)), vec, vec, tab, tab,
                  pl.BlockSpec((ATT_WIDTH, ATT_WIDTH), lambda i: (0, 0))],
        out_specs=[pl.BlockSpec((T, ATT_WIDTH), lambda i: (src(i), 0)), hm, hm],
        out_shape=[jax.ShapeDtypeStruct((S, ATT_WIDTH), BF16), hm_shape, hm_shape],
        name=name, compiler_params=_params("arbitrary"),
    )(proj, proj, proj, qg, kg, cos, sin, bd)


def _att_prep_bwd(proj, dq, dk_p, dv_p, qg, kg, cos, sin, bd, dproj, name):
    S = proj.shape[0]
    T = min(512, S)
    NI = S // T
    PB = ATT_SPAN // T
    W = ATT_WIDTH

    def body(q_ref, k_ref, dq_ref, dkp_ref, dvp_ref, qg_ref, kg_ref, cos_ref, sin_ref, bd_ref, dproj_ref,
             do_ref, dqg_ref, dkg_ref, acc_ref):
        i = pl.program_id(0)

        @pl.when(i == 0)
        def _():
            acc_ref[...] = jnp.zeros_like(acc_ref)

        npair = ATT_HEADS // 2
        dk_all = jnp.concatenate([dkp_ref[pr].T for pr in range(npair)], axis=1)
        do_ref[:, 2 * W:3 * W] = jnp.concatenate([dvp_ref[pr].T for pr in range(npair)], axis=1).astype(BF16)
        cw = _widen(cos_ref[...], ATT_WIDTH)
        sw = _widen(sin_ref[...], ATT_WIDTH)
        bdv = bd_ref[...]

        def one(t, d_rot, gain, scale, slot):
            ss = _group_sum64(t * t, bdv)
            r = lax.rsqrt(ss * (1.0 / ATT_HEAD_DIM) + EPS)
            n = t * r
            d_ng = _rope_t(d_rot * scale, cw, sw)
            acc_ref[pl.ds(slot, 1), :] += jnp.sum(d_ng * n, axis=0, keepdims=True)
            dn = d_ng * gain
            return r * (dn - n * (_group_sum64(dn * n, bdv) * (1.0 / ATT_HEAD_DIM)))

        do_ref[:, 0:W] = one(q_ref[...], dq_ref[...], qg_ref[...], ATT_HEAD_DIM ** -0.5, 0).astype(BF16)
        do_ref[:, W:2 * W] = one(k_ref[...], dk_all, kg_ref[...], 1.0, 1).astype(BF16)

        @pl.when(i == NI - 1)
        def _():
            a = acc_ref[...]
            f = a[:, 0:64]
            for h in range(1, ATT_HEADS):
                f = f + a[:, h * 64:(h + 1) * 64]
            dqg_ref[...] = f[0:1]
            dkg_ref[...] = f[1:2]

    vec = pl.BlockSpec((1, ATT_WIDTH), lambda i: (0, 0))
    tab = pl.BlockSpec((T, 128), lambda i: (i, 0))
    row = pl.BlockSpec((T, ATT_WIDTH), lambda i: (i, 0))
    g64 = pl.BlockSpec((1, ATT_HEAD_DIM), lambda i: (0, 0))
    padded = pl.BlockSpec((ATT_HEADS // 2, 128, T), lambda i: (0, 0, i + PB))
    return pl.pallas_call(
        body, grid=(NI,),
        in_specs=[pl.BlockSpec((T, ATT_WIDTH), lambda i: (i, AQ_BLK)), pl.BlockSpec((T, ATT_WIDTH), lambda i: (i, AQ_BLK + 1)),
                  row, padded, padded, vec, vec, tab, tab, pl.BlockSpec((ATT_WIDTH, ATT_WIDTH), lambda i: (0, 0)),
                  pl.BlockSpec(memory_space=pl.ANY)],
        out_specs=[pl.BlockSpec((T, 3 * W), lambda i: (i, ATT_COL // (3 * W))), g64, g64],
        out_shape=[jax.ShapeDtypeStruct(dproj.shape, dproj.dtype), jax.ShapeDtypeStruct((1, ATT_HEAD_DIM), F32),
                   jax.ShapeDtypeStruct((1, ATT_HEAD_DIM), F32)],
        input_output_aliases={10: 0},
        scratch_shapes=[pltpu.VMEM((8, ATT_WIDTH), F32)], name=name, compiler_params=_params("arbitrary"),
    )(proj, proj, dq, dk_p, dv_p, qg, kg, cos, sin, bd, dproj)


def _att_bias():
    qpos = np.arange(CHUNK)[:, None] + ATT_SPAN
    kpos = np.arange(ATT_STRIP)[None, :]
    rel = qpos - kpos
    mult = np.zeros((CHUNK, ATT_STRIP), np.float64)
    for window, dil in DILATED_PAIRS:
        mult += (rel >= 0) & (rel % dil == 0) & (rel // dil <= window // dil)
    return np.where(mult > 0, np.log(np.maximum(mult, 1.0)), NEG).astype(np.float32)


def _att_scores(q, ks, bias, i):
    s = _bdot(q, ks, NT) + bias
    kcol = lax.broadcasted_iota(jnp.int32, (1, ATT_STRIP), 1) + i * CHUNK
    return jnp.where(kcol >= ATT_SPAN, s, NEG)


def _pair_masks():
    low = lax.broadcasted_iota(jnp.int32, (CHUNK, 128), 1) < ATT_HEAD_DIM
    return low, ~low


def _att_fwd(q, kp, vp, bias, y, name):
    S = q.shape[0]
    SP = kp.shape[1]

    TQ = ATT_QB * CHUNK

    def body(q_ref, k_ref, v_ref, bias_ref, y_ref, o_ref):
        i = pl.program_id(1)
        for b in range(ATT_QB):
            blk = i * ATT_QB + b
            strip = pl.ds(pl.multiple_of(blk * CHUNK, CHUNK), ATT_STRIP)
            rows = pl.ds(b * CHUNK, CHUNK)
            qv = q_ref[rows, :]
            ks = k_ref[strip, :]
            vs = v_ref[strip, :]
            outs = []
            for keep in _pair_masks():
                s = _att_scores(jnp.where(keep, qv, jnp.zeros_like(qv)), ks, bias_ref[...], blk)
                m = jnp.max(s, axis=-1, keepdims=True)
                p = jnp.exp(s - m)
                den = jnp.sum(p, axis=-1, keepdims=True)
                outs.append(_bdot(p, vs, NN) / den)
            o_ref[rows, :] = jnp.where(_pair_masks()[0], outs[0], outs[1]).astype(o_ref.dtype)

    kv = pl.BlockSpec((None, SP, 128), lambda hp, i: (hp, 0, 0))
    return pl.pallas_call(
        body, grid=(ATT_HEADS // 2, S // TQ),
        in_specs=[pl.BlockSpec((TQ, 128), lambda hp, i: (i, hp)), kv, kv,
                  pl.BlockSpec((CHUNK, ATT_STRIP), lambda hp, i: (0, 0)), pl.BlockSpec(memory_space=pl.ANY)],
        out_specs=pl.BlockSpec((TQ, 128), lambda hp, i: (i, SSD_WIDTH // 128 + hp)),
        out_shape=jax.ShapeDtypeStruct(y.shape, y.dtype), input_output_aliases={4: 0}, name=name,
        compiler_params=_params("parallel", "arbitrary"),
    )(q, kp, vp, bias, y)


def _att_bwd(q, kp, vp, bias, dy, name):
    S = q.shape[0]
    SP = kp.shape[1]

    def body(q_ref, k_ref, v_ref, bias_ref, do_ref, dq_ref, dk_ref, dv_ref):
        i = pl.program_id(1)

        @pl.when(i == 0)
        def _():
            dk_ref[...] = jnp.zeros_like(dk_ref)
            dv_ref[...] = jnp.zeros_like(dv_ref)

        for b in range(ATT_QB):
            blk = i * ATT_QB + b
            strip = pl.ds(pl.multiple_of(blk * CHUNK, CHUNK), ATT_STRIP)
            rows = pl.ds(b * CHUNK, CHUNK)
            qv = q_ref[rows, :]
            dov = do_ref[rows, :]
            ks = k_ref[strip, :]
            vs = v_ref[strip, :]
            dq = jnp.zeros((CHUNK, 128), F32)
            dk_t = jnp.zeros((128, ATT_STRIP), F32)
            dv_t = jnp.zeros((128, ATT_STRIP), F32)
            for keep in _pair_masks():
                qh = jnp.where(keep, qv, jnp.zeros_like(qv))
                doh = jnp.where(keep, dov, 0.0)
                s = _att_scores(qh, ks, bias_ref[...], blk)
                m = jnp.max(s, axis=-1, keepdims=True)
                p = jnp.exp(s - m)
                p = p / jnp.sum(p, axis=-1, keepdims=True)
                dp = _bdot(doh, vs, NT)
                dsc = p * (dp - jnp.sum(p * dp, axis=-1, keepdims=True))
                dq = dq + jnp.where(keep, _bdot(dsc, ks, NN), 0.0)
                dv_t = dv_t + _bdot(doh, p, TN)
                dk_t = dk_t + _bdot(qh, dsc, TN)
            dq_ref[rows, :] = dq
            dv_ref[:, strip] += dv_t
            dk_ref[:, strip] += dk_t

    TQ = ATT_QB * CHUNK
    kv = pl.BlockSpec((None, SP, 128), lambda hp, i: (hp, 0, 0))
    kv_t = pl.BlockSpec((None, 128, SP), lambda hp, i: (hp, 0, 0))
    pairs = jax.ShapeDtypeStruct((ATT_HEADS // 2, 128, SP), F32)
    return pl.pallas_call(
        body, grid=(ATT_HEADS // 2, S // TQ),
        in_specs=[pl.BlockSpec((TQ, 128), lambda hp, i: (i, hp)), kv, kv,
                  pl.BlockSpec((CHUNK, ATT_STRIP), lambda hp, i: (0, 0)),
                  pl.BlockSpec((TQ, 128), lambda hp, i: (i, SSD_WIDTH // 128 + hp))],
        out_specs=[pl.BlockSpec((TQ, 128), lambda hp, i: (i, hp)), kv_t, kv_t],
        out_shape=[jax.ShapeDtypeStruct((S, ATT_WIDTH), F32), pairs, pairs],
        name=name, compiler_params=_params("parallel", "arbitrary"),
    )(q, kp, vp, bias, dy)


RQ_BLK = RET_COL // RET_QK_WIDTH
RV_BLK = (RET_COL + 2 * RET_QK_WIDTH) // RET_V_WIDTH
RET_LOG_GAMMA = tuple(math.log1p(-2.0 ** (-5.0 - h)) for h in range(RET_HEADS))


def _ret_decays(h):
    L = CHUNK
    lg = RET_LOG_GAMMA[h]
    row = lax.broadcasted_iota(jnp.int32, (L, L), 0)
    col = lax.broadcasted_iota(jnp.int32, (L, L), 1)
    rel = (row - col).astype(F32)
    dm = jnp.where(rel >= 0, jnp.exp(jnp.maximum(rel, 0.0) * lg), 0.0)
    idx = lax.broadcasted_iota(jnp.int32, (L, 1), 0).astype(F32)
    kte = jnp.exp((L - 1 - idx) * lg)
    qfs = jnp.exp((idx + 1.0) * lg)
    return dm, kte, qfs, math.exp(L * lg)


def _ret_fwd(proj, cos, sin, gain, y, name):
    S = proj.shape[0]
    L = CHUNK
    T = min(512, S)
    CPS = T // L
    NC = S // L

    def body(q_ref, k_ref, v_ref, g_ref, cos_ref, sin_ref, gain_ref, yin_ref, y_ref, o_ref, st_ref, state):
        i = pl.program_id(0)

        @pl.when(i == 0)
        def _():
            state[...] = jnp.zeros_like(state)

        dec = [_ret_decays(h) for h in range(RET_HEADS)]
        for c in range(CPS):
            rows = pl.ds(c * L, L)
            cw = _widen(cos_ref[rows, :], RET_QK_WIDTH)
            sw = _widen(sin_ref[rows, :], RET_QK_WIDTH)
            qv = _rope(q_ref[rows, :], cw, sw)
            kv = _rope(k_ref[rows, :], cw, sw) * (RET_QK_DIM ** -0.5)
            for h in range(RET_HEADS):
                dm, kte, qfs, cd = dec[h]
                qh = qv[:, h * 64:(h + 1) * 64]
                kh = kv[:, h * 64:(h + 1) * 64]
                vs = slice(h * RET_V_DIM, (h + 1) * RET_V_DIM)
                vh = v_ref[rows, vs]
                sp = state[h]
                st_ref[c, h] = sp
                o = _bdot(_bdot(qh, kh, NT) * dm, vh, NN) + _bdot(qh * qfs, sp, NN)
                state[h] = cd * sp + _bdot(kh * kte, vh, TN)
                o_ref[rows, vs] = o
                gh = g_ref[rows, vs]
                r = lax.rsqrt(jnp.mean(o * o, axis=-1, keepdims=True) + EPS)
                y_ref[rows, vs] = (o * r * gain_ref[:, vs] * (gh * _sigmoid(gh))).astype(y_ref.dtype)

    tab = pl.BlockSpec((T, 128), lambda i: (i, 0))
    wide = pl.BlockSpec((T, RET_V_WIDTH), lambda i: (i, 0))
    return pl.pallas_call(
        body, grid=(S // T,),
        in_specs=[pl.BlockSpec((T, RET_QK_WIDTH), lambda i: (i, RQ_BLK)), pl.BlockSpec((T, RET_QK_WIDTH), lambda i: (i, RQ_BLK + 1)),
                  pl.BlockSpec((T, RET_V_WIDTH), lambda i: (i, RV_BLK)), pl.BlockSpec((T, RET_V_WIDTH), lambda i: (i, RV_BLK + 1)),
                  tab, tab, pl.BlockSpec((1, RET_V_WIDTH), lambda i: (0, 0)), pl.BlockSpec(memory_space=pl.ANY)],
        out_specs=[pl.BlockSpec((T, RET_V_WIDTH), lambda i: (i, (SSD_WIDTH + ATT_WIDTH) // RET_V_WIDTH)), wide,
                   pl.BlockSpec((CPS, RET_HEADS, RET_QK_DIM, RET_V_DIM), lambda i: (i, 0, 0, 0))],
        out_shape=[jax.ShapeDtypeStruct(y.shape, y.dtype), jax.ShapeDtypeStruct((S, RET_V_WIDTH), F32),
                   jax.ShapeDtypeStruct((NC, RET_HEADS, RET_QK_DIM, RET_V_DIM), F32)],
        input_output_aliases={7: 0},
        scratch_shapes=[pltpu.VMEM((RET_HEADS, RET_QK_DIM, RET_V_DIM), F32)], name=name,
        compiler_params=_params("arbitrary"),
    )(proj, proj, proj, proj, cos, sin, gain, y)


def _ret_bwd(dy, oraw, proj, cos, sin, gain, states, dproj, name):
    S = proj.shape[0]
    L = CHUNK
    T = min(512, S)
    CPS = T // L
    NI = S // T
    QW, VW = RET_QK_WIDTH, RET_V_WIDTH
    V0, G0 = 2 * QW, 2 * QW + VW

    def body(dy_ref, o_ref, q_ref, k_ref, v_ref, g_ref, cos_ref, sin_ref, gain_ref, st_ref, dproj_ref,
             out_ref, dgain_ref, dstate, dqs, dks):
        i = pl.program_id(0)

        @pl.when(i == 0)
        def _():
            dstate[...] = jnp.zeros_like(dstate)
            dgain_ref[...] = jnp.zeros_like(dgain_ref)

        dec = [_ret_decays(h) for h in range(RET_HEADS)]
        for c in reversed(range(CPS)):
            rows = pl.ds(c * L, L)
            cw = _widen(cos_ref[rows, :], RET_QK_WIDTH)
            sw = _widen(sin_ref[rows, :], RET_QK_WIDTH)
            qv = _rope(q_ref[rows, :], cw, sw)
            kv = _rope(k_ref[rows, :], cw, sw) * (RET_QK_DIM ** -0.5)
            for h in range(RET_HEADS):
                dm, kte, qfs, cd = dec[h]
                qs = slice(h * 64, (h + 1) * 64)
                vs = slice(h * RET_V_DIM, (h + 1) * RET_V_DIM)
                qh = qv[:, qs]
                kh = kv[:, qs]
                vh = v_ref[rows, vs]
                gh = g_ref[rows, vs]
                gn = gain_ref[:, vs]
                o = o_ref[rows, vs]
                dyh = dy_ref[rows, vs]
                sg = _sigmoid(gh)
                silu_g = gh * sg
                r = lax.rsqrt(jnp.mean(o * o, axis=-1, keepdims=True) + EPS)
                n = o * r
                dgain_ref[:, vs] += jnp.sum(dyh * n * silu_g, axis=0, keepdims=True)
                out_ref[rows, G0 + h * RET_V_DIM:G0 + (h + 1) * RET_V_DIM] = (dyh * n * gn * _silu_grad(gh, sg)).astype(out_ref.dtype)
                dn = dyh * gn * silu_g
                do = r * (dn - n * jnp.mean(dn * n, axis=-1, keepdims=True))
                sp = st_ref[c, h]
                ds = dstate[h]
                sc = _bdot(qh, kh, NT) * dm
                dsc = _bdot(do, vh, NT) * dm
                out_ref[rows, V0 + h * RET_V_DIM:V0 + (h + 1) * RET_V_DIM] = (_bdot(sc, do, TN) + _bdot(kh * kte, ds, NN)).astype(out_ref.dtype)
                dqs[:, qs] = _bdot(dsc, kh, NN) + _bdot(do, sp, NT) * qfs
                dks[:, qs] = _bdot(dsc, qh, TN) + _bdot(vh, ds, NT) * kte
                dstate[h] = cd * ds + _bdot(qh * qfs, do, TN)
            out_ref[rows, 0:QW] = _rope_t(dqs[...], cw, sw).astype(out_ref.dtype)
            out_ref[rows, QW:2 * QW] = _rope_t(dks[...] * (RET_QK_DIM ** -0.5), cw, sw).astype(out_ref.dtype)

    rev = lambda i: NI - 1 - i
    tab = pl.BlockSpec((T, 128), lambda i: (rev(i), 0))
    wide = pl.BlockSpec((T, RET_V_WIDTH), lambda i: (rev(i), 0))
    group = pl.BlockSpec((T, G0 + VW), lambda i: (rev(i), RET_COL // (G0 + VW)))
    gvec = pl.BlockSpec((1, RET_V_WIDTH), lambda i: (0, 0))
    return pl.pallas_call(
        body, grid=(NI,),
        in_specs=[pl.BlockSpec((T, RET_V_WIDTH), lambda i: (rev(i), (SSD_WIDTH + ATT_WIDTH) // RET_V_WIDTH)), wide,
                  pl.BlockSpec((T, RET_QK_WIDTH), lambda i: (rev(i), RQ_BLK)), pl.BlockSpec((T, RET_QK_WIDTH), lambda i: (rev(i), RQ_BLK + 1)),
                  pl.BlockSpec((T, RET_V_WIDTH), lambda i: (rev(i), RV_BLK)), pl.BlockSpec((T, RET_V_WIDTH), lambda i: (rev(i), RV_BLK + 1)),
                  tab, tab, gvec,
                  pl.BlockSpec((CPS, RET_HEADS, RET_QK_DIM, RET_V_DIM), lambda i: (rev(i), 0, 0, 0)),
                  pl.BlockSpec(memory_space=pl.ANY)],
        out_specs=[group, gvec],
        out_shape=[jax.ShapeDtypeStruct(dproj.shape, dproj.dtype), jax.ShapeDtypeStruct((1, RET_V_WIDTH), F32)],
        input_output_aliases={10: 0},
        scratch_shapes=[pltpu.VMEM((RET_HEADS, RET_QK_DIM, RET_V_DIM), F32), pltpu.VMEM((L, RET_QK_WIDTH), F32),
                        pltpu.VMEM((L, RET_QK_WIDTH), F32)],
        name=name, compiler_params=_params("arbitrary"),
    )(dy, oraw, proj, proj, proj, proj, cos, sin, gain, states, dproj)


def _adamw_update(g_ref, nb, w_ref, m_ref, v_ref, go_ref, d_ref, mo_ref, vo_ref):
    g = g_ref[0].astype(F32)
    for k in range(1, nb):
        g = g + g_ref[k].astype(F32)
    mn = ADAM_B1 * m_ref[...] + (1.0 - ADAM_B1) * g
    vn = ADAM_B2 * v_ref[...] + (1.0 - ADAM_B2) * (g * g)
    go_ref[...] = g
    mo_ref[...] = mn
    vo_ref[...] = vn
    c1 = 1.0 - ADAM_B1 ** ADAM_STEP
    c2 = 1.0 - ADAM_B2 ** ADAM_STEP
    d_ref[...] = -ADAM_LR * ((mn / c1) / (jnp.sqrt(vn / c2) + ADAM_EPS) + ADAM_WD * w_ref[...])


def _adamw_rows(R, C):
    return _pick(R, tuple(t for t in (512, 256, 128, 64, 32, 16, 8) if t * C <= 256 * 1024))


def _adamw(gblocks, w, m, v, name):
    nb, R, C = gblocks.shape
    tr = _adamw_rows(R, C)

    def body(g_ref, *refs):
        _adamw_update(g_ref, nb, *refs)

    row = pl.BlockSpec((tr, C), lambda i: (i, 0))
    sh = jax.ShapeDtypeStruct((R, C), F32)
    return pl.pallas_call(
        body, grid=(R // tr,), in_specs=[pl.BlockSpec((nb, tr, C), lambda i: (0, i, 0)), row, row, row],
        out_specs=[row, row, row, row], out_shape=[sh, sh, sh, sh], name=name, compiler_params=_params("parallel"),
    )(gblocks, w, m, v)


def _adamw_layers(g0, g1, w, m, v, name):
    nb, R, C = g0.shape
    tr = _adamw_rows(R, C)

    def body(g0_ref, g1_ref, *refs):
        l = pl.program_id(0)

        @pl.when(l == 0)
        def _():
            _adamw_update(g0_ref, nb, *refs)

        @pl.when(l == 1)
        def _():
            _adamw_update(g1_ref, nb, *refs)

    row = pl.BlockSpec((None, tr, C), lambda l, i: (l, i, 0))
    sh = jax.ShapeDtypeStruct((DEPTH, R, C), F32)
    return pl.pallas_call(
        body, grid=(DEPTH, R // tr),
        in_specs=[pl.BlockSpec((nb, tr, C), lambda l, i: (0, i * (1 - l), 0)), pl.BlockSpec((nb, tr, C), lambda l, i: (0, i * l, 0)),
                  row, row, row],
        out_specs=[row, row, row, row], out_shape=[sh, sh, sh, sh], name=name, compiler_params=_params("arbitrary", "arbitrary"),
    )(g0, g1, w, m, v)


def _peers():
    x, y, c = lax.axis_index("x"), lax.axis_index("y"), lax.axis_index("c")
    flips = ((0, 0, 1), (1, 0, 0), (0, 1, 0), (1, 1, 0), (1, 0, 1), (0, 1, 1), (1, 1, 1))
    me = 4 * x + 2 * y + c
    peers = [(x ^ fx, y ^ fy, c ^ fc) for fx, fy, fc in flips]
    return me, peers


def _exchange(arrs, scatter, name):
    n = len(arrs)
    npeer = N_DEV - 1

    def body(*refs):
        ins, outs = refs[:n], refs[n:2 * n]
        send_sems, recv_sems, local_sems = refs[2 * n:]
        me, peers = _peers()
        copies = []
        for a in range(n):
            src_own = ins[a].at[me] if scatter else ins[a]
            own = pltpu.make_async_copy(src_own, outs[a].at[me], local_sems.at[a])
            own.start()
            copies.append(own)
            for k, peer in enumerate(peers):
                src = ins[a].at[4 * peer[0] + 2 * peer[1] + peer[2]] if scatter else ins[a]
                cp = pltpu.make_async_remote_copy(
                    src_ref=src, dst_ref=outs[a].at[me], send_sem=send_sems.at[a * npeer + k],
                    recv_sem=recv_sems.at[a * npeer + k], device_id=peer, device_id_type=pl.DeviceIdType.MESH)
                cp.start()
                copies.append(cp)
        for cp in copies:
            cp.wait()

    out_shape = [jax.ShapeDtypeStruct(((N_DEV,) + a.shape[1:]) if scatter else ((N_DEV,) + a.shape), a.dtype) for a in arrs]
    anyspec = pl.BlockSpec(memory_space=pl.ANY)
    return pl.pallas_call(
        body, in_specs=[anyspec] * n, out_specs=[anyspec] * n, out_shape=out_shape,
        scratch_shapes=[pltpu.SemaphoreType.DMA((n * npeer,)), pltpu.SemaphoreType.DMA((n * npeer,)),
                        pltpu.SemaphoreType.DMA((n,))],
        name=name,
    )(*arrs)


def _dev_index(peer):
    return 4 * peer[0] + 2 * peer[1] + peer[2]


def _push_copies(src_refs, land_refs, send_sems, recv_sems, scatter, as_receiver):
    me, peers = _peers()
    npeer = N_DEV - 1
    copies = []
    for a in range(len(src_refs)):
        for k, peer in enumerate(peers):
            src = src_refs[a].at[_dev_index(peer)] if scatter else src_refs[a]
            slot = _dev_index(peer) if as_receiver else me
            copies.append(pltpu.make_async_remote_copy(
                src_ref=src, dst_ref=land_refs[a].at[slot], send_sem=send_sems.at[a * npeer + k],
                recv_sem=recv_sems.at[a * npeer + k], device_id=peer, device_id_type=pl.DeviceIdType.MESH))
    return copies


def _own_copies(src_refs, land_refs, own_sems, scatter):
    me, _ = _peers()
    return [pltpu.make_async_copy(src_refs[a].at[me] if scatter else src_refs[a], land_refs[a].at[me], own_sems.at[a])
            for a in range(len(src_refs))]


def _push_start(srcs, scatter, name):
    n = len(srcs)
    nsem = n * (N_DEV - 1)

    def body(*refs):
        srcs_r, lands_r = refs[:n], refs[n:2 * n]
        for cp in _push_copies(srcs_r, lands_r, refs[2 * n], refs[2 * n + 1], scatter, False):
            cp.start()
        for cp in _own_copies(srcs_r, lands_r, refs[2 * n + 2], scatter):
            cp.start()
        token = refs[-1]
        token[...] = jnp.zeros_like(token)

    hbm = pl.BlockSpec(memory_space=pltpu.HBM)
    sem = pl.BlockSpec(memory_space=pltpu.SEMAPHORE)
    lands = [lax.empty((N_DEV,) + (s.shape[1:] if scatter else s.shape), s.dtype) for s in srcs]
    arrs = list(srcs) + lands
    return pl.pallas_call(
        body, name=name,
        out_shape=(pltpu.SemaphoreType.DMA((nsem,)), pltpu.SemaphoreType.DMA((nsem,)), pltpu.SemaphoreType.DMA((n,)),
                   *[pltpu.HBM(a.shape, a.dtype) for a in arrs], jax.ShapeDtypeStruct((8, 128), F32)),
        in_specs=[hbm] * (2 * n), out_specs=(sem, sem, sem, *([hbm] * (2 * n)), pl.BlockSpec(memory_space=pltpu.VMEM)),
        input_output_aliases={i: 3 + i for i in range(2 * n)},
        compiler_params=pltpu.CompilerParams(has_side_effects=pltpu.SideEffectType.DATAFLOW_SIDE_EFFECTING),
    )(*[pltpu.with_memory_space_constraint(a, pltpu.HBM) for a in arrs])


def _push_wait(handle, after, scatter, name):
    send_sems, recv_sems, own_sems, *thru, _ = handle
    n = len(thru) // 2

    def body(*refs):
        srcs_r, lands_r = refs[:n], refs[n:2 * n]
        for cp in _push_copies(srcs_r, lands_r, refs[2 * n], refs[2 * n + 1], scatter, True):
            cp.wait_send()
            cp.wait_recv()
        for cp in _own_copies(srcs_r, lands_r, refs[2 * n + 2], scatter):
            cp.wait()

    hbm = pl.BlockSpec(memory_space=pltpu.HBM)
    sem = pl.BlockSpec(memory_space=pltpu.SEMAPHORE)
    outs = pl.pallas_call(
        body, name=name, out_shape=tuple(pltpu.HBM(a.shape, a.dtype) for a in thru),
        in_specs=[hbm] * (2 * n) + [sem, sem, sem, pl.BlockSpec(memory_space=pl.ANY)], out_specs=tuple([hbm] * (2 * n)),
        input_output_aliases={i: i for i in range(2 * n)},
        compiler_params=pltpu.CompilerParams(has_side_effects=pltpu.SideEffectType.DATAFLOW_SIDE_EFFECTING),
    )(*thru, send_sems, recv_sems, own_sems, after)
    return list(outs[n:])


def _relay_copies(src_refs, land_refs, send1, recv1, send2, recv2, as_receiver):
    me, peers = _peers()
    sibling, chips = peers[0], peers[1:4]
    first, second = [], []
    for a in range(len(src_refs)):
        for k, peer in enumerate([sibling] + chips):
            slot = _dev_index(peer) if as_receiver else me
            first.append(pltpu.make_async_remote_copy(
                src_ref=src_refs[a], dst_ref=land_refs[a].at[slot], send_sem=send1.at[4 * a + k], recv_sem=recv1.at[4 * a + k],
                device_id=peer, device_id_type=pl.DeviceIdType.MESH))
        for k, chip in enumerate(chips):
            origin = _dev_index(chip)
            slot = origin ^ 1 if as_receiver else origin
            second.append(pltpu.make_async_remote_copy(
                src_ref=land_refs[a].at[origin], dst_ref=land_refs[a].at[slot], send_sem=send2.at[3 * a + k],
                recv_sem=recv2.at[3 * a + k], device_id=sibling, device_id_type=pl.DeviceIdType.MESH))
    return first, second


def _relay_start(srcs, name):
    n = len(srcs)

    def body(*refs):
        srcs_r, lands_r = refs[:n], refs[n:2 * n]
        for cp in _relay_copies(srcs_r, lands_r, refs[2 * n], refs[2 * n + 1], refs[2 * n], refs[2 * n + 1], False)[0]:
            cp.start()
        for cp in _own_copies(srcs_r, lands_r, refs[2 * n + 2], False):
            cp.start()
        token = refs[-1]
        token[...] = jnp.zeros_like(token)

    hbm = pl.BlockSpec(memory_space=pltpu.HBM)
    sem = pl.BlockSpec(memory_space=pltpu.SEMAPHORE)
    arrs = list(srcs) + [lax.empty((N_DEV,) + s.shape, s.dtype) for s in srcs]
    return pl.pallas_call(
        body, name=name,
        out_shape=(pltpu.SemaphoreType.DMA((4 * n,)), pltpu.SemaphoreType.DMA((4 * n,)), pltpu.SemaphoreType.DMA((n,)),
                   *[pltpu.HBM(a.shape, a.dtype) for a in arrs], jax.ShapeDtypeStruct((8, 128), F32)),
        in_specs=[hbm] * (2 * n), out_specs=(sem, sem, sem, *([hbm] * (2 * n)), pl.BlockSpec(memory_space=pltpu.VMEM)),
        input_output_aliases={i: 3 + i for i in range(2 * n)},
        compiler_params=pltpu.CompilerParams(has_side_effects=pltpu.SideEffectType.DATAFLOW_SIDE_EFFECTING),
    )(*[pltpu.with_memory_space_constraint(a, pltpu.HBM) for a in arrs])


def _relay_forward(handle, after, name):
    _, recv1, _, *thru, _ = handle
    n = len(thru) // 2

    def body(*refs):
        srcs_r, lands_r, recv1_r = refs[:n], refs[n:2 * n], refs[2 * n]
        send2_r, recv2_r = refs[2 * n + 2], refs[2 * n + 3]
        first, second = _relay_copies(srcs_r, lands_r, recv1_r, recv1_r, send2_r, recv2_r, True)
        for a in range(n):
            for k in range(1, 4):
                first[4 * a + k].wait_recv()
        for cp in _relay_copies(srcs_r, lands_r, recv1_r, recv1_r, send2_r, recv2_r, False)[1]:
            cp.start()
        token = refs[-1]
        token[...] = jnp.zeros_like(token)

    hbm = pl.BlockSpec(memory_space=pltpu.HBM)
    sem = pl.BlockSpec(memory_space=pltpu.SEMAPHORE)
    return pl.pallas_call(
        body, name=name,
        out_shape=(pltpu.SemaphoreType.DMA((3 * n,)), pltpu.SemaphoreType.DMA((3 * n,)),
                   *[pltpu.HBM(a.shape, a.dtype) for a in thru], jax.ShapeDtypeStruct((8, 128), F32)),
        in_specs=[hbm] * (2 * n) + [sem, pl.BlockSpec(memory_space=pl.ANY)],
        out_specs=(sem, sem, *([hbm] * (2 * n)), pl.BlockSpec(memory_space=pltpu.VMEM)),
        input_output_aliases={i: 2 + i for i in range(2 * n)},
        compiler_params=pltpu.CompilerParams(has_side_effects=pltpu.SideEffectType.DATAFLOW_SIDE_EFFECTING),
    )(*thru, recv1, after)


def _relay_wait(handle, forwarded, after, name):
    send1, recv1, own_sems, *_ = handle
    send2, recv2, *thru, _ = forwarded
    n = len(thru) // 2

    def body(*refs):
        srcs_r, lands_r = refs[:n], refs[n:2 * n]
        send1_r, recv1_r, own_r, send2_r, recv2_r = refs[2 * n:2 * n + 5]
        first, second = _relay_copies(srcs_r, lands_r, send1_r, recv1_r, send2_r, recv2_r, True)
        for i, cp in enumerate(first):
            cp.wait_send()
            if i % 4 == 0:
                cp.wait_recv()
        for cp in second:
            cp.wait_send()
            cp.wait_recv()
        for cp in _own_copies(srcs_r, lands_r, own_r, False):
            cp.wait()

    hbm = pl.BlockSpec(memory_space=pltpu.HBM)
    sem = pl.BlockSpec(memory_space=pltpu.SEMAPHORE)
    outs = pl.pallas_call(
        body, name=name, out_shape=tuple(pltpu.HBM(a.shape, a.dtype) for a in thru),
        in_specs=[hbm] * (2 * n) + [sem] * 5 + [pl.BlockSpec(memory_space=pl.ANY)], out_specs=tuple([hbm] * (2 * n)),
        input_output_aliases={i: i for i in range(2 * n)},
        compiler_params=pltpu.CompilerParams(has_side_effects=pltpu.SideEffectType.DATAFLOW_SIDE_EFFECTING),
    )(*thru, send1, recv1, own_sems, send2, recv2, after)
    return list(outs[n:])


def _tables(S):
    pos = jnp.arange(S, dtype=F32)
    inv = ROPE_THETA ** (-jnp.arange(0, ATT_HEAD_DIM, 2, dtype=F32) / ATT_HEAD_DIM)
    ang = pos[:, None] * inv[None, :]
    cos, sin = jnp.cos(ang), jnp.sin(ang)
    cos128 = jnp.tile(cos, (1, 4))
    sin128 = jnp.tile(jnp.concatenate([-sin, sin], axis=1), (1, 2))
    lane = np.arange(ATT_WIDTH)
    bd = jnp.asarray((lane[:, None] // 64 == lane[None, :] // 64).astype(np.float32), dtype=BF16)
    return cos128, sin128, bd, jnp.asarray(_att_bias())


def _layer_fwd(l, x, p, tabs, early=None, late=None):
    cos, sin, bd, bias = tabs
    S = x.shape[0]
    row = lambda v: v.reshape(1, -1)
    hn = _rmsnorm_fwd(x, row(p["ln_mix"]), f"norm_mix_fwd{l}")
    if early is not None:
        p.update(early(hn))
    proj = _mm(hn, p["w_in"], "nn", f"in_proj{l}", tn=1920)
    xbc = _conv_fwd(proj, p["conv_w"], row(p["conv_b"]), f"conv_fwd{l}")
    dtr = proj[:, DT_COL:DT_COL + SSD_HEADS].reshape(S, SSD_GROUPS, HPG).transpose(1, 0, 2)
    grp = lambda v: v.reshape(SSD_GROUPS, 1, HPG)
    d_rep = row(jnp.repeat(p["d_skip"], SSD_HEAD_DIM))
    y, yraw, ssd_st = _ssd_fwd(xbc, proj, dtr, grp(p["dt_bias"]), grp(p["a_log"]), d_rep, row(p["ssd_norm"]), f"ssd_fwd{l}")
    qg = row(jnp.tile(p["q_norm"], ATT_HEADS))
    kg = row(jnp.tile(p["k_norm"], ATT_HEADS))
    aq, akp, avp = _att_prep_fwd(proj, qg, kg, cos, sin, bd, f"att_prep_fwd{l}")
    y = _att_fwd(aq, akp, avp, bias, y, f"att_fwd{l}")
    y, oraw, ret_st = _ret_fwd(proj, cos, sin, row(p["ret_norm"]), y, f"ret_fwd{l}")
    if late is not None:
        p.update(late(y))
    x1 = _mm(y, p["w_out"], "nn", f"out_proj{l}", residual=x)
    hn2 = _rmsnorm_fwd(x1, row(p["ln_ffn"]), f"norm_ffn_fwd{l}")
    g, u, act = _swiglu_fwd(hn2, p["w_gate"], p["w_up"], f"swiglu_fwd{l}")
    x2 = _mm(act, p["w_down"], "nn", f"down_proj{l}", residual=x1, tk=2816)
    saved = dict(x=x, hn=hn, proj=proj, xbc=xbc, dtr=dtr, yraw=yraw, ssd_st=ssd_st, aq=aq, akp=akp, avp=avp,
                 oraw=oraw, ret_st=ret_st, y=y, x1=x1, hn2=hn2, g=g, u=u, act=act, d_rep=d_rep, qg=qg, kg=kg)
    return x2, saved


def _layer_bwd(l, dx2, dx2_bf, p, sv, tabs, on_ffn=None, on_all=None):
    cos, sin, bd, bias = tabs
    S = dx2.shape[0]
    row = lambda v: v.reshape(1, -1)
    grp = lambda v: v.reshape(SSD_GROUPS, 1, HPG)
    gr = {}
    dg, du = _swiglu_bwd(dx2_bf, p["w_down"], sv["g"], sv["u"], f"swiglu_bwd{l}")
    gr["w_down"] = _mm(sv["act"], dx2_bf, "tn", f"down_wgrad{l}", out_dtype=BF16, tm=1408, tn=1024, tk=2048)
    dhn2 = _mm_nt2(dg, p["w_gate"], du, p["w_up"], f"ffn_dgrad{l}")
    gr["w_gate"] = _mm(sv["hn2"], dg, "tn", f"gate_wgrad{l}", out_dtype=BF16, tm=512, tn=2816, tk=2048)
    gr["w_up"] = _mm(sv["hn2"], du, "tn", f"up_wgrad{l}", out_dtype=BF16, tm=512, tn=2816, tk=2048)
    ffn_gain = row(p["ln_ffn"]) + (on_ffn(gr)[0, 0] if on_ffn is not None else 0.0)
    dx1, dx1_bf, dln_ffn = _rmsnorm_bwd(sv["x1"], dhn2, ffn_gain, dx2, f"norm_ffn_bwd{l}")
    gr["ln_ffn"] = dln_ffn[0]
    dy = _mm(dx1_bf, p["w_out"], "nt", f"out_dgrad{l}")
    gr["w_out"] = _mm(sv["y"], dx1_bf, "tn", f"out_wgrad{l}", out_dtype=BF16, tm=1024, tn=1024, tk=2048)
    dproj, dxs, dbm, dcm, ddtr, dbias, dalog, dd, dssd_gain = _ssd_bwd(
        dy, sv["yraw"], sv["xbc"], sv["proj"], sv["dtr"], grp(p["dt_bias"]), grp(p["a_log"]), sv["d_rep"],
        row(p["ssd_norm"]), sv["ssd_st"], f"ssd_bwd{l}")
    gr["dt_bias"], gr["a_log"], gr["d_skip"] = dbias.reshape(-1), dalog.reshape(-1), dd.reshape(-1)
    gr["ssd_norm"] = dssd_gain[0]
    dproj, dconv_w, dconv_b = _conv_bwd(dxs, dbm, dcm, sv["proj"], p["conv_w"], row(p["conv_b"]), dproj, f"conv_bwd{l}")
    gr["conv_w"], gr["conv_b"] = dconv_w, dconv_b[0]
    dq, dk_p, dv_p = _att_bwd(sv["aq"], sv["akp"], sv["avp"], bias, dy, f"att_bwd{l}")
    dproj, dqg, dkg = _att_prep_bwd(sv["proj"], dq, dk_p, dv_p, sv["qg"], sv["kg"], cos, sin, bd, dproj, f"att_prep_bwd{l}")
    gr["q_norm"], gr["k_norm"] = dqg[0], dkg[0]
    dproj, dret_gain = _ret_bwd(dy, sv["oraw"], sv["proj"], cos, sin, row(p["ret_norm"]), sv["ret_st"], dproj, f"ret_bwd{l}")
    gr["ret_norm"] = dret_gain[0]
    ddt_cols = ddtr.transpose(1, 0, 2).reshape(S, SSD_HEADS).astype(BF16)
    dproj = lax.dynamic_update_slice(dproj, jnp.pad(ddt_cols, ((0, 0), (0, IN_PAD - DT_COL - SSD_HEADS))), (0, DT_COL))
    gr["w_in"] = _mm(sv["hn"], dproj, "tn", f"in_wgrad{l}", out_dtype=BF16, tm=1024, tn=1920, tk=2048)
    launched = on_all(gr) if on_all is not None else None
    dhn = _mm(dproj, p["w_in"], "nt", f"in_dgrad{l}", tk=1920, after=launched)
    dx0, dx0_bf, dln_mix = _rmsnorm_bwd(sv["x"], dhn, row(p["ln_mix"]), dx1, f"norm_mix_bwd{l}")
    gr["ln_mix"] = dln_mix[0]
    return dx0, dx0_bf, gr


def _local_step(x, tgt, layers, early=None, late=None, on_ffn=None, on_all=None):
    n = len(layers)
    none = [None] * n
    early, late, on_ffn, on_all = early or none, late or none, on_ffn or none, on_all or none
    tabs = _tables(x.shape[0])
    saved, params = [], []
    h = x
    for l in range(n):
        p = dict(layers[l](h) if callable(layers[l]) else layers[l])
        h, sv = _layer_fwd(l, h, p, tabs, early[l], late[l])
        saved.append(sv)
        params.append(p)
    dh, dh_bf, lacc = _loss_grad(h, tgt, "loss_grad")
    grads = [None] * n
    for l in reversed(range(n)):
        dh, dh_bf, grads[l] = _layer_bwd(l, dh, dh_bf, params[l], saved[l], tabs, on_ffn[l], on_all[l])
    return lacc[0, 0], dh, grads


BIG = ("w_in", "w_out", "w_gate", "w_up", "w_down")
SMALL = ("ln_mix", "conv_b", "dt_bias", "a_log", "d_skip", "ssd_norm", "q_norm", "k_norm", "ret_norm", "ln_ffn")
ORDER = ("ln_mix", "w_in", "conv_w", "conv_b", "dt_bias", "a_log", "d_skip", "ssd_norm", "q_norm", "k_norm", "ret_norm",
         "w_out", "ln_ffn", "w_gate", "w_up", "w_down")


COL_SHARDED = ("w_in", "w_gate", "w_up", "conv_w")


IN_GROUPS = ((ORIG_Z_XBC, Z_COL), (ORIG_DT, DT_COL), (ORIG_ATT, ATT_COL), (ORIG_RET, RET_COL))


def _full_weight(k, gathered):
    if k == "w_in":
        cs = gathered.shape[2]
        pieces = []
        for (lo, hi), _ in sorted(IN_GROUPS, key=lambda grp: grp[1]):
            for j in range(N_DEV):
                a, b = max(lo, j * cs), min(hi, (j + 1) * cs)
                if a < b:
                    pieces.append(gathered[j][:, a - j * cs:b - j * cs])
        pieces.append(jnp.zeros((gathered.shape[1], IN_PAD - IN_WIDTH), gathered.dtype))
        return jnp.concatenate(pieces, axis=1)
    if k in COL_SHARDED:
        return gathered.transpose(1, 0, 2).reshape(gathered.shape[1], -1)
    return gathered.reshape(-1, gathered.shape[2])


def _shard_block(k, g):
    if k == "w_in":
        cs = IN_WIDTH // N_DEV
        blocks = []
        for j in range(N_DEV):
            pieces = []
            for (lo, hi), col in IN_GROUPS:
                a, b = max(lo, j * cs), min(hi, (j + 1) * cs)
                if a < b:
                    pieces.append(g[:, col + a - lo:col + b - lo])
            blocks.append(jnp.concatenate(pieces, axis=1))
        return jnp.stack(blocks)
    if k in COL_SHARDED:
        return g.reshape(g.shape[0], N_DEV, -1).transpose(1, 0, 2)
    return g.reshape(N_DEV, -1, g.shape[1])


def kernel(x, ln_mix, w_in, conv_w, conv_b, dt_bias, a_log, d_skip, ssd_norm, q_norm, k_norm, ret_norm, w_out, ln_ffn, w_gate, w_up, w_down, loss_target, m_ln_mix, m_w_in, m_conv_w, m_conv_b, m_dt_bias, m_a_log, m_d_skip, m_ssd_norm, m_q_norm, m_k_norm, m_ret_norm, m_w_out, m_ln_ffn, m_w_gate, m_w_up, m_w_down, v_ln_mix, v_w_in, v_conv_w, v_conv_b, v_dt_bias, v_a_log, v_d_skip, v_ssd_norm, v_q_norm, v_k_norm, v_ret_norm, v_w_out, v_ln_ffn, v_w_gate, v_w_up, v_w_down):
    w = dict(ln_mix=ln_mix, w_in=w_in, conv_w=conv_w, conv_b=conv_b, dt_bias=dt_bias, a_log=a_log, d_skip=d_skip,
             ssd_norm=ssd_norm, q_norm=q_norm, k_norm=k_norm, ret_norm=ret_norm, w_out=w_out, ln_ffn=ln_ffn,
             w_gate=w_gate, w_up=w_up, w_down=w_down)
    m = dict(ln_mix=m_ln_mix, w_in=m_w_in, conv_w=m_conv_w, conv_b=m_conv_b, dt_bias=m_dt_bias, a_log=m_a_log,
             d_skip=m_d_skip, ssd_norm=m_ssd_norm, q_norm=m_q_norm, k_norm=m_k_norm, ret_norm=m_ret_norm, w_out=m_w_out,
             ln_ffn=m_ln_ffn, w_gate=m_w_gate, w_up=m_w_up, w_down=m_w_down)
    v = dict(ln_mix=v_ln_mix, w_in=v_w_in, conv_w=v_conv_w, conv_b=v_conv_b, dt_bias=v_dt_bias, a_log=v_a_log,
             d_skip=v_d_skip, ssd_norm=v_ssd_norm, q_norm=v_q_norm, k_norm=v_k_norm, ret_norm=v_ret_norm, w_out=v_w_out,
             ln_ffn=v_ln_ffn, w_gate=v_w_gate, w_up=v_w_up, w_down=v_w_down)
    me = 4 * lax.axis_index("x") + 2 * lax.axis_index("y") + lax.axis_index("c")

    late_names = ("w_out", "w_gate", "w_up", "w_down")
    waves = {"a": [("w_in", 0), ("conv_w", 0), ("conv_w", 1)], "b": [(k, 0) for k in late_names],
             "c": [("w_in", 1)], "d": [(k, 1) for k in late_names]}
    gather = {}
    behind = 0.0
    for tag, items in waves.items():
        srcs = [w[k][l] if k == "conv_w" else (w[k][l] + behind).astype(BF16) for k, l in items]
        start = _relay_start(srcs, f"gather_{tag}_start") if tag == "a" else _push_start(srcs, False, f"gather_{tag}_start")
        gather[tag] = start
        behind = start[-1][0, 0]
    forwarded = _relay_forward(gather["a"], gather["d"][-1], "gather_a_forward")
    started = forwarded[-1][0, 0]
    full = {}

    def arrive(tag, after):
        if tag == "a":
            lands = _relay_wait(gather["a"], forwarded, after, "gather_a_wait")
        else:
            lands = _push_wait(gather[tag], after, False, f"gather_{tag}_wait")
        for (k, l), g in zip(waves[tag], lands):
            full[k, l] = _full_weight(k, g)

    def layer_weights(l, names):
        return {k: full[k, l] for k in names}

    def small_weights(l):
        return {k: w[k][l] for k in SMALL}

    def layer0(h):
        p = small_weights(0)
        p["ln_mix"] = p["ln_mix"] + started
        return p

    def early0(hn):
        arrive("a", hn)
        return layer_weights(0, ("w_in", "conv_w"))

    def late0(y):
        arrive("b", y)
        return layer_weights(0, late_names)

    def layer1(h):
        arrive("c", h)
        return {**small_weights(1), **layer_weights(1, ("w_in", "conv_w"))}

    def late1(y):
        arrive("d", y)
        return layer_weights(1, late_names)

    groups = {"1": [(k, 1) for k in BIG], "0a": [(k, 0) for k in ("w_down", "w_gate", "w_up")],
              "0b": [(k, 0) for k in ("w_out", "w_in")]}
    scatter = {}

    def push_grads(tag, gr):
        blocks = [_shard_block(k, gr[k]) for k, _ in groups[tag]]
        scatter[tag] = _push_start(blocks, True, f"scatter_{tag}_start")
        return scatter[tag][-1]

    loss_part, gx, grads = _local_step(
        x[0], loss_target[0], [layer0, layer1], early=[early0, None], late=[late0, late1],
        on_ffn=[functools.partial(push_grads, "0a"), None],
        on_all=[functools.partial(push_grads, "0b"), functools.partial(push_grads, "1")])
    loss = lax.psum(loss_part, MESH_AXES)

    out = {}
    recv = {}
    for tag, items in groups.items():
        for item, r in zip(items, _push_wait(scatter[tag], gx, True, f"scatter_{tag}_wait")):
            recv[item] = r
    for k in BIG:
        out[k] = _adamw_layers(recv[k, 0], recv[k, 1], w[k], m[k], v[k], f"adamw_{k}")
    names = SMALL + ("conv_w",)
    sizes = [int(np.prod(grads[0][k].shape)) for k in names]
    packed = jnp.concatenate([jnp.stack([grads[l][k] for l in range(DEPTH)]).reshape(-1) for k in names])
    n_small = packed.shape[0]
    rows_small = -(-n_small // 1024) * 8
    pad = lambda t, fill: jnp.concatenate([t, jnp.full((rows_small * 128 - n_small,), fill, F32)]).reshape(rows_small, 128)
    parts = _exchange([pad(packed, 0.0)], False, "gather_small_grads")[0]
    n_rep = DEPTH * sum(sizes[:-1])
    pack_rep = lambda d, fill: pad(jnp.concatenate([d[k].reshape(-1) for k in SMALL]
                                                   + [jnp.full((n_small - n_rep,), fill, F32)]), fill)
    res = _adamw(parts, pack_rep(w, 1.0), pack_rep(m, 1.0), pack_rep(v, 1.0), "adamw_small")
    res = [t.reshape(-1) for t in res]
    off = 0
    for k, sz in zip(SMALL, sizes[:-1]):
        out[k] = [t[off:off + DEPTH * sz].reshape(w[k].shape) for t in res]
        off += DEPTH * sz
    gconv = res[0][off:off + DEPTH * sizes[-1]].reshape(DEPTH, SSD_CONV, SSD_CONV_CH)
    gconv = lax.dynamic_slice_in_dim(gconv, me * conv_w.shape[2], conv_w.shape[2], axis=2)
    flat = lambda t: t.reshape(8, -1)
    resc = _adamw(flat(gconv)[None], flat(conv_w), flat(m_conv_w), flat(v_conv_w), "adamw_conv_w")
    out["conv_w"] = [t.reshape(conv_w.shape) for t in resc]

    return (loss, gx[None], *[out[k][0] for k in ORDER], *[out[k][1] for k in ORDER],
            *[out[k][2] for k in ORDER], *[out[k][3] for k in ORDER])
```

```python
import functools
import math

import jax
import jax.numpy as jnp
import numpy as np
from jax import lax
from jax.experimental import pallas as pl
from jax.experimental.pallas import tpu as pltpu

F32 = jnp.float32
BF16 = jnp.bfloat16

N_DEV = 8
MESH_AXES = ("x", "y", "c")
D_MODEL = 2048
DEPTH = 2
EPS = 1e-6
ROPE_THETA = 10000.0
SSD_HEADS = 16
SSD_HEAD_DIM = 64
SSD_WIDTH = 1024
SSD_GROUPS = 2
SSD_STATE = 128
SSD_CONV = 4
SSD_CONV_CH = 1536
ATT_HEADS = 8
ATT_HEAD_DIM = 64
ATT_WIDTH = 512
DILATED_PAIRS = ((128, 1), (512, 4), (2048, 16))
RET_HEADS = 4
RET_QK_DIM = 64
RET_V_DIM = 128
RET_QK_WIDTH = 256
RET_V_WIDTH = 512
CHUNK = 128
MIX_WIDTH = 2048
ATT_SPAN = 2048
ATT_STRIP = ATT_SPAN + CHUNK
ATT_QB = 4
IN_WIDTH = 5648
IN_PAD = 5760
RET_COL, ATT_COL, Z_COL, XBC_COL, DT_COL = 0, 1536, 3072, 4096, 5632
ORIG_Z_XBC, ORIG_DT, ORIG_ATT, ORIG_RET = (0, 2560), (2560, 2576), (2576, 4112), (4112, 5648)
D_FF = 5632
ADAM_LR = 0.001
ADAM_B1 = 0.9
ADAM_B2 = 0.999
ADAM_EPS = 1e-08
ADAM_WD = 0.01
ADAM_STEP = 10
NEG = -1e30
VMEM_LIMIT_V7X = 60 * 1024 * 1024

NN = (((1,), (0,)), ((), ()))
NT = (((1,), (1,)), ((), ()))
TN = (((0,), (0,)), ((), ()))


def _bdot(a, b, dims):
    return lax.dot_general(a.astype(BF16), b.astype(BF16), dims, preferred_element_type=F32)


def _xdot(a, b, dims, exact_first=False):
    ones, x = (a, b) if exact_first else (b, a)
    ones = ones.astype(BF16)
    acc, rest = None, x
    for _ in range(3):
        piece = rest.astype(BF16)
        rest = rest - piece.astype(F32)
        part = lax.dot_general(*((ones, piece) if exact_first else (piece, ones)), dims, preferred_element_type=F32)
        acc = part if acc is None else acc + part
    return acc


def _params(*sem):
    return pltpu.CompilerParams(dimension_semantics=sem, vmem_limit_bytes=VMEM_LIMIT_V7X)


def _sigmoid(v):
    return 1.0 / (1.0 + jnp.exp(-v))


def _silu_grad(v, s):
    return s * (1.0 + v * (1.0 - s))


def _rmsnorm_fwd(x, g, name):
    S, D = x.shape
    tr = min(512, S)

    def body(x_ref, g_ref, o_ref):
        xv = x_ref[...]
        r = lax.rsqrt(jnp.mean(xv * xv, axis=-1, keepdims=True) + EPS)
        o_ref[...] = (xv * r * g_ref[...]).astype(o_ref.dtype)

    return pl.pallas_call(
        body, grid=(S // tr,),
        in_specs=[pl.BlockSpec((tr, D), lambda i: (i, 0)), pl.BlockSpec((1, D), lambda i: (0, 0))],
        out_specs=pl.BlockSpec((tr, D), lambda i: (i, 0)),
        out_shape=jax.ShapeDtypeStruct((S, D), BF16), name=name, compiler_params=_params("parallel"),
    )(x, g)


def _rmsnorm_bwd(x, dy, g, dres, name):
    S, D = x.shape
    tr = min(512, S)

    def body(x_ref, dy_ref, g_ref, dres_ref, dx_ref, dxb_ref, dg_ref):
        i = pl.program_id(0)
        xv = x_ref[...]
        r = lax.rsqrt(jnp.mean(xv * xv, axis=-1, keepdims=True) + EPS)
        n = xv * r
        dy = dy_ref[...]
        dn = dy * g_ref[...]
        dx = dres_ref[...] + r * (dn - n * jnp.mean(dn * n, axis=-1, keepdims=True))
        dx_ref[...] = dx
        dxb_ref[...] = dx.astype(BF16)
        part = jnp.sum(dy * n, axis=0, keepdims=True)

        @pl.when(i == 0)
        def _():
            dg_ref[...] = part

        @pl.when(i > 0)
        def _():
            dg_ref[...] += part

    row = pl.BlockSpec((tr, D), lambda i: (i, 0))
    vec = pl.BlockSpec((1, D), lambda i: (0, 0))
    return pl.pallas_call(
        body, grid=(S // tr,), in_specs=[row, row, vec, row], out_specs=[row, row, vec],
        out_shape=[jax.ShapeDtypeStruct((S, D), F32), jax.ShapeDtypeStruct((S, D), BF16), jax.ShapeDtypeStruct((1, D), F32)],
        name=name, compiler_params=_params("arbitrary"),
    )(x, dy, g, dres)


def _loss_grad(y, tgt, name):
    S, D = y.shape
    tr = min(512, S)

    def body(y_ref, t_ref, dy_ref, dyb_ref, l_ref):
        i = pl.program_id(0)
        err = y_ref[...] - t_ref[...]
        dy = err * (1.0 / D)
        dy_ref[...] = dy
        dyb_ref[...] = dy.astype(BF16)
        part = jnp.sum(jnp.sum(err * err, axis=1, keepdims=True), axis=0, keepdims=True) * (0.5 / D)

        @pl.when(i == 0)
        def _():
            l_ref[...] = jnp.zeros_like(l_ref)

        l_ref[...] += part

    row = pl.BlockSpec((tr, D), lambda i: (i, 0))
    return pl.pallas_call(
        body, grid=(S // tr,), in_specs=[row, row],
        out_specs=[row, row, pl.BlockSpec((8, 128), lambda i: (0, 0))],
        out_shape=[jax.ShapeDtypeStruct((S, D), F32), jax.ShapeDtypeStruct((S, D), BF16), jax.ShapeDtypeStruct((8, 128), F32)],
        name=name, compiler_params=_params("arbitrary"),
    )(y, tgt)


def _pick(n, cands):
    for c in cands:
        if n % c == 0:
            return c
    return n


def _mm(a, b, mode, name, out_dtype=F32, residual=None, tm=None, tn=None, tk=None, after=None):
    if mode == "nn":
        (M, K), (_, N) = a.shape, b.shape
    elif mode == "nt":
        (M, K), (N, _) = a.shape, b.shape
    else:
        (K, M), (_, N) = a.shape, b.shape
    tm = min(tm, M) if tm else _pick(M, (1024, 512, 256, 128))
    tn = min(tn, N) if tn else _pick(N, (1024, 1152, 1408, 512, 256, 128))
    tk = min(tk, K) if tk else _pick(K, (2048, 1920, 1408, 1024, 512, 256, 128))
    assert M % tm == 0 and N % tn == 0 and K % tk == 0, (name, M, N, K, tm, tn, tk)
    nk = K // tk
    a_spec = pl.BlockSpec((tk, tm), lambda i, j, k: (k, i)) if mode == "tn" else pl.BlockSpec((tm, tk), lambda i, j, k: (i, k))
    b_spec = pl.BlockSpec((tn, tk), lambda i, j, k: (j, k)) if mode == "nt" else pl.BlockSpec((tk, tn), lambda i, j, k: (k, j))
    o_spec = pl.BlockSpec((tm, tn), lambda i, j, k: (i, j))
    dims = {"nn": NN, "nt": NT, "tn": TN}[mode]
    has_res = residual is not None

    has_after = after is not None

    def body(*refs):
        a_ref, b_ref = refs[0], refs[1]
        r_ref = refs[2] if has_res else None
        o_ref = refs[2 + has_res + has_after]
        p = _bdot(a_ref[...], b_ref[...], dims)

        def finish(acc):
            if has_res:
                acc = acc + r_ref[...]
            o_ref[...] = acc.astype(o_ref.dtype)

        if nk == 1:
            finish(p)
        else:
            acc_ref = refs[-1]
            k = pl.program_id(2)

            @pl.when(k == 0)
            def _():
                acc_ref[...] = p

            @pl.when(k > 0)
            def _():
                acc_ref[...] += p

            @pl.when(k == nk - 1)
            def _():
                finish(acc_ref[...])

    ins = [a, b] + ([residual] if has_res else []) + ([after] if has_after else [])
    in_specs = [a_spec, b_spec] + ([o_spec] if has_res else []) + ([pl.BlockSpec(memory_space=pl.ANY)] if has_after else [])
    scratch = [pltpu.VMEM((tm, tn), F32)] if nk > 1 else []
    return pl.pallas_call(
        body, grid=(M // tm, N // tn, nk), in_specs=in_specs, out_specs=o_spec,
        out_shape=jax.ShapeDtypeStruct((M, N), out_dtype), scratch_shapes=scratch, name=name,
        compiler_params=_params("parallel", "parallel", "arbitrary"),
    )(*ins)


def _accumulate(acc_ref, p, k, nk, finish):
    @pl.when(k == 0)
    def _():
        acc_ref[...] = p

    @pl.when(k > 0)
    def _():
        acc_ref[...] += p

    @pl.when(k == nk - 1)
    def _():
        finish(acc_ref[...])


def _swiglu_fwd(hn, wg, wu, name):
    S, K = hn.shape
    F = wg.shape[1]
    tm = _pick(S, (1024, 512))
    tn = _pick(F, (512, 256, 128))

    def body(a_ref, wg_ref, wu_ref, g_ref, u_ref, act_ref):
        a = a_ref[...]
        g = _bdot(a, wg_ref[...], NN)
        u = _bdot(a, wu_ref[...], NN)
        g_ref[...] = g.astype(BF16)
        u_ref[...] = u.astype(BF16)
        act_ref[...] = (g * _sigmoid(g) * u).astype(BF16)

    w_spec = pl.BlockSpec((K, tn), lambda i, j: (0, j))
    o_spec = pl.BlockSpec((tm, tn), lambda i, j: (i, j))
    sh = jax.ShapeDtypeStruct((S, F), BF16)
    return pl.pallas_call(
        body, grid=(S // tm, F // tn), in_specs=[pl.BlockSpec((tm, K), lambda i, j: (i, 0)), w_spec, w_spec],
        out_specs=[o_spec, o_spec, o_spec], out_shape=[sh, sh, sh], name=name,
        compiler_params=_params("parallel", "parallel"),
    )(hn, wg, wu)


def _swiglu_bwd(dx, wd, g, u, name):
    S, K = dx.shape
    F = wd.shape[0]
    tm = _pick(S, (1024, 512))
    tn = _pick(F, (512, 256, 128))

    def body(dx_ref, wd_ref, g_ref, u_ref, dg_ref, du_ref):
        da = _bdot(dx_ref[...], wd_ref[...], NT)
        gv = g_ref[...].astype(F32)
        uv = u_ref[...].astype(F32)
        s = _sigmoid(gv)
        dg_ref[...] = (da * uv * _silu_grad(gv, s)).astype(BF16)
        du_ref[...] = (da * gv * s).astype(BF16)

    o_spec = pl.BlockSpec((tm, tn), lambda i, j: (i, j))
    sh = jax.ShapeDtypeStruct((S, F), BF16)
    return pl.pallas_call(
        body, grid=(S // tm, F // tn),
        in_specs=[pl.BlockSpec((tm, K), lambda i, j: (i, 0)), pl.BlockSpec((tn, K), lambda i, j: (j, 0)), o_spec, o_spec],
        out_specs=[o_spec, o_spec], out_shape=[sh, sh], name=name, compiler_params=_params("parallel", "parallel"),
    )(dx, wd, g, u)


def _mm_nt2(a1, b1, a2, b2, name):
    M, K = a1.shape
    N = b1.shape[0]
    tm = _pick(M, (512,))
    tn = _pick(N, (1024, 512))
    tk = _pick(K, (2816, 1024, 512, 256, 128))
    nk = K // tk

    def body(a1_ref, b1_ref, a2_ref, b2_ref, o_ref, acc_ref):
        def finish(acc):
            o_ref[...] = acc

        p = _bdot(a1_ref[...], b1_ref[...], NT) + _bdot(a2_ref[...], b2_ref[...], NT)
        _accumulate(acc_ref, p, pl.program_id(2), nk, finish)

    a_spec = pl.BlockSpec((tm, tk), lambda i, j, k: (i, k))
    b_spec = pl.BlockSpec((tn, tk), lambda i, j, k: (j, k))
    return pl.pallas_call(
        body, grid=(M // tm, N // tn, nk), in_specs=[a_spec, b_spec, a_spec, b_spec],
        out_specs=pl.BlockSpec((tm, tn), lambda i, j, k: (i, j)), out_shape=jax.ShapeDtypeStruct((M, N), F32),
        scratch_shapes=[pltpu.VMEM((tm, tn), F32)], name=name,
        compiler_params=_params("parallel", "parallel", "arbitrary"),
    )(a1, b1, a2, b2)


XBC_BLK0 = XBC_COL // 128


def _conv_fwd(proj, w, b, name):
    S = proj.shape[0]
    T = min(512, S)

    def body(x_ref, w_ref, b_ref, o_ref, xp_ref):
        xp_ref[pl.ds(0, 8), :] = jnp.zeros((8, 128), F32)
        xp_ref[pl.ds(8, S), :] = x_ref[...]
        wv = w_ref[...]
        bv = b_ref[...]

        def step(c, carry):
            base = pl.multiple_of(c * T, T)
            acc = wv[0:1] * xp_ref[pl.ds(base + 5, T), :]
            for i in range(1, SSD_CONV):
                acc = acc + wv[i:i + 1] * xp_ref[pl.ds(base + 5 + i, T), :]
            acc = bv + acc
            o_ref[pl.ds(base, T), :] = acc * _sigmoid(acc)
            return carry

        lax.fori_loop(0, S // T, step, 0)

    return pl.pallas_call(
        body, grid=(SSD_CONV_CH // 128,),
        in_specs=[pl.BlockSpec((S, 128), lambda j: (0, XBC_BLK0 + j)), pl.BlockSpec((SSD_CONV, 128), lambda j: (0, j)),
                  pl.BlockSpec((1, 128), lambda j: (0, j))],
        out_specs=pl.BlockSpec((S, 128), lambda j: (0, j)),
        out_shape=jax.ShapeDtypeStruct((S, SSD_CONV_CH), F32),
        scratch_shapes=[pltpu.VMEM((S + 8, 128), F32)], name=name, compiler_params=_params("parallel"),
    )(proj, w, b)


def _conv_bwd(dxs, dbm, dcm, proj, w, b, dproj, name):
    S = proj.shape[0]
    T = min(512, S)
    NX, NB = SSD_WIDTH // 128, SSD_GROUPS * SSD_STATE // 128

    def body(dxs_ref, dbm_ref, dcm_ref, x_ref, w_ref, b_ref, dproj_ref, dx_ref, dw_ref, db_ref, xp_ref, dcp_ref):
        j = pl.program_id(0)

        @pl.when(j < NX)
        def _():
            dcp_ref[pl.ds(0, S), :] = dxs_ref[...]

        @pl.when((j >= NX) & (j < NX + NB))
        def _():
            dcp_ref[pl.ds(0, S), :] = dbm_ref[...]

        @pl.when(j >= NX + NB)
        def _():
            dcp_ref[pl.ds(0, S), :] = dcm_ref[...]

        da_ref = dcp_ref
        xp_ref[pl.ds(0, 8), :] = jnp.zeros((8, 128), F32)
        xp_ref[pl.ds(8, S), :] = x_ref[...]
        dcp_ref[pl.ds(S, 8), :] = jnp.zeros((8, 128), F32)
        wv = w_ref[...]
        bv = b_ref[...]

        def step1(c, carry):
            base = pl.multiple_of(c * T, T)
            xs = [xp_ref[pl.ds(base + 5 + i, T), :] for i in range(SSD_CONV)]
            acc = wv[0:1] * xs[0]
            for i in range(1, SSD_CONV):
                acc = acc + wv[i:i + 1] * xs[i]
            acc = bv + acc
            s = _sigmoid(acc)
            dc = da_ref[pl.ds(base, T), :] * _silu_grad(acc, s)
            dcp_ref[pl.ds(base, T), :] = dc
            new = tuple(carry[i] + jnp.sum(xs[i] * dc, axis=0, keepdims=True) for i in range(SSD_CONV))
            return new + (carry[SSD_CONV] + jnp.sum(dc, axis=0, keepdims=True),)

        z = jnp.zeros((1, 128), F32)
        res = lax.fori_loop(0, S // T, step1, (z,) * (SSD_CONV + 1))
        for i in range(SSD_CONV):
            dw_ref[pl.ds(i, 1), :] = res[i]
        db_ref[...] = res[SSD_CONV]

        def step2(c, carry):
            base = pl.multiple_of(c * T, T)
            acc = wv[0:1] * dcp_ref[pl.ds(base + 3, T), :]
            for i in range(1, SSD_CONV):
                acc = acc + wv[i:i + 1] * dcp_ref[pl.ds(base + 3 - i, T), :]
            dx_ref[pl.ds(base, T), :] = acc.astype(dx_ref.dtype)
            return carry

        lax.fori_loop(0, S // T, step2, 0)

    clamp = lambda j, lo, n: jnp.clip(j - lo, 0, n - 1)
    return pl.pallas_call(
        body, grid=(SSD_CONV_CH // 128,),
        in_specs=[pl.BlockSpec((S, 128), lambda j: (0, clamp(j, 0, NX))), pl.BlockSpec((S, 128), lambda j: (0, clamp(j, NX, NB))),
                  pl.BlockSpec((S, 128), lambda j: (0, clamp(j, NX + NB, NB))),
                  pl.BlockSpec((S, 128), lambda j: (0, XBC_BLK0 + j)), pl.BlockSpec((SSD_CONV, 128), lambda j: (0, j)),
                  pl.BlockSpec((1, 128), lambda j: (0, j)), pl.BlockSpec(memory_space=pl.ANY)],
        out_specs=[pl.BlockSpec((S, 128), lambda j: (0, XBC_BLK0 + j)), pl.BlockSpec((SSD_CONV, 128), lambda j: (0, j)),
                   pl.BlockSpec((1, 128), lambda j: (0, j))],
        out_shape=[jax.ShapeDtypeStruct(dproj.shape, dproj.dtype), jax.ShapeDtypeStruct((SSD_CONV, SSD_CONV_CH), F32),
                   jax.ShapeDtypeStruct((1, SSD_CONV_CH), F32)],
        input_output_aliases={6: 0},
        scratch_shapes=[pltpu.VMEM((S + 8, 128), F32), pltpu.VMEM((S + 8, 128), F32)], name=name,
        compiler_params=_params("arbitrary"),
    )(dxs, dbm, dcm, proj, w, b, dproj)


HPG = SSD_HEADS // SSD_GROUPS
GW = HPG * SSD_HEAD_DIM


def _ssd_chunk_terms(dtr, bias, alog, tril, triu):
    pre = dtr + bias
    dt = jnp.maximum(pre, 0.0) + jnp.log(1.0 + jnp.exp(-jnp.abs(pre)))
    a_neg = -jnp.exp(alog)
    a = dt * a_neg
    acum = _xdot(tril, a, NN, exact_first=True)
    acum_t = _xdot(a, triu, TN)
    return pre, dt, a_neg, acum, acum_t


def _head_expanders():
    h64 = lax.broadcasted_iota(jnp.int32, (HPG, GW), 0) == lax.broadcasted_iota(jnp.int32, (HPG, GW), 1) // SSD_HEAD_DIM
    h128 = lax.broadcasted_iota(jnp.int32, (HPG, HPG * CHUNK), 0) == lax.broadcasted_iota(jnp.int32, (HPG, HPG * CHUNK), 1) // CHUNK
    return h64.astype(F32), h128.astype(F32)


def _ssd_fwd(xbc, proj, dtr, dt_bias, a_log, d_rep, gain, name):
    S = xbc.shape[0]
    L = CHUNK
    T = min(512, S)
    CPS = T // L
    NC = S // L

    def body(x_ref, b_ref, c_ref, z_ref, dtr_ref, bias_ref, alog_ref, d_ref, gain_ref, y_ref, yraw_ref, st_ref, state):
        i = pl.program_id(1)

        @pl.when(i == 0)
        def _():
            state[...] = jnp.zeros_like(state)

        row = lax.broadcasted_iota(jnp.int32, (L, L), 0)
        col = lax.broadcasted_iota(jnp.int32, (L, L), 1)
        causal = row >= col
        tril = causal.astype(F32)
        triu = (row <= col).astype(F32)
        low = col < SSD_HEAD_DIM
        e64, e128 = _head_expanders()
        for c in range(CPS):
            rows = pl.ds(c * L, L)
            xv = x_ref[rows, :]
            bm = b_ref[rows, :]
            cm = c_ref[rows, :]
            _, dt, _, acum, acum_t = _ssd_chunk_terms(dtr_ref[rows, :], bias_ref[...], alog_ref[...], tril, triu)
            ac = _xdot(acum, e64, NN)
            ac_sq = _xdot(acum, e128, NN)
            xd = xv * _xdot(dt, e64, NN)
            ac_last = ac[L - 1:L, :]
            sp = state[...]
            st_ref[c] = sp
            yoff = _bdot(cm, sp, NN) * jnp.exp(ac)
            state[...] = sp * jnp.exp(ac_last) + _bdot(bm, xd * jnp.exp(ac_last - ac), TN)
            gmat = _bdot(cm, bm, NT)
            for q in range(HPG // 2):
                pair = slice(q * 128, (q + 1) * 128)
                tile = xd[:, pair]
                y = yoff[:, pair]
                for j, keep in ((2 * q, low), (2 * q + 1, ~low)):
                    lam = jnp.exp(jnp.where(causal, ac_sq[:, j * L:(j + 1) * L] - acum_t[j:j + 1, :], NEG))
                    y = y + _bdot(gmat * lam, jnp.where(keep, tile, 0.0), NN)
                yraw_ref[rows, pair] = y
            zz = z_ref[rows, :]
            u = (yraw_ref[rows, :] + xv * d_ref[...]) * (zz * _sigmoid(zz))
            r = lax.rsqrt(jnp.mean(u * u, axis=-1, keepdims=True) + EPS)
            y_ref[rows, :] = (u * r * gain_ref[...]).astype(y_ref.dtype)

    vec8 = pl.BlockSpec((None, 1, HPG), lambda g, i: (g, 0, 0))
    return pl.pallas_call(
        body, grid=(SSD_GROUPS, S // T),
        in_specs=[pl.BlockSpec((T, GW), lambda g, i: (i, g)),
                  pl.BlockSpec((T, SSD_STATE), lambda g, i: (i, SSD_WIDTH // SSD_STATE + g)),
                  pl.BlockSpec((T, SSD_STATE), lambda g, i: (i, SSD_WIDTH // SSD_STATE + SSD_GROUPS + g)),
                  pl.BlockSpec((T, GW), lambda g, i: (i, Z_COL // GW + g)),
                  pl.BlockSpec((None, T, HPG), lambda g, i: (g, i, 0)),
                  vec8, vec8,
                  pl.BlockSpec((1, GW), lambda g, i: (0, g)), pl.BlockSpec((1, GW), lambda g, i: (0, g))],
        out_specs=[pl.BlockSpec((T, GW), lambda g, i: (i, g)), pl.BlockSpec((T, GW), lambda g, i: (i, g)),
                   pl.BlockSpec((CPS, None, SSD_STATE, GW), lambda g, i: (i, g, 0, 0))],
        out_shape=[jax.ShapeDtypeStruct((S, MIX_WIDTH), BF16), jax.ShapeDtypeStruct((S, SSD_WIDTH), F32),
                   jax.ShapeDtypeStruct((NC, SSD_GROUPS, SSD_STATE, GW), F32)],
        scratch_shapes=[pltpu.VMEM((SSD_STATE, GW), F32)], name=name,
        compiler_params=_params("arbitrary", "arbitrary"),
    )(xbc, xbc, xbc, proj, dtr, dt_bias, a_log, d_rep, gain)


def _ssd_bwd(dy, yraw, xbc, proj, dtr, dt_bias, a_log, d_rep, gain, states, name):
    S = xbc.shape[0]
    L = CHUNK
    T = min(512, S)
    CPS = T // L
    NI = S // T

    def body(dy_ref, yraw_ref, x_ref, b_ref, c_ref, z_ref, dtr_ref, bias_ref, alog_ref, d_ref, gain_ref, st_ref,
             dz_ref, dx_ref, db_ref, dc_ref, ddtr_ref, dbias_ref, dalog_ref, dd_ref, dgain_ref, dstate, dxd_ref):
        i = pl.program_id(1)

        @pl.when(i == 0)
        def _():
            dstate[...] = jnp.zeros_like(dstate)
            dbias_ref[...] = jnp.zeros_like(dbias_ref)
            dalog_ref[...] = jnp.zeros_like(dalog_ref)
            dd_ref[...] = jnp.zeros_like(dd_ref)
            dgain_ref[...] = jnp.zeros_like(dgain_ref)

        row = lax.broadcasted_iota(jnp.int32, (L, L), 0)
        col = lax.broadcasted_iota(jnp.int32, (L, L), 1)
        causal = row >= col
        tril = causal.astype(F32)
        triu = (row <= col).astype(F32)
        low = col < SSD_HEAD_DIM
        e64, e128 = _head_expanders()
        lane8 = lax.broadcasted_iota(jnp.int32, (1, HPG), 1)
        sub8 = lax.broadcasted_iota(jnp.int32, (HPG, 1), 0)
        eye8 = (lax.broadcasted_iota(jnp.int32, (HPG, HPG), 0) == lax.broadcasted_iota(jnp.int32, (HPG, HPG), 1)).astype(F32)
        last_row = (lax.broadcasted_iota(jnp.int32, (L, 1), 0) == L - 1).astype(F32)
        for c in reversed(range(CPS)):
            rows = pl.ds(c * L, L)
            xv = x_ref[rows, :]
            bm = b_ref[rows, :]
            cm = c_ref[rows, :]
            zz = z_ref[rows, :]
            dvec = d_ref[...]
            sz = _sigmoid(zz)
            silu_z = zz * sz
            v = yraw_ref[rows, :] + xv * dvec
            u = v * silu_z
            r = lax.rsqrt(jnp.mean(u * u, axis=-1, keepdims=True) + EPS)
            n = u * r
            do = dy_ref[rows, :]
            dgain_ref[...] += jnp.sum(do * n, axis=0, keepdims=True)
            dn = do * gain_ref[...]
            du = r * (dn - n * jnp.mean(dn * n, axis=-1, keepdims=True))
            dz_ref[rows, :] = (du * v * _silu_grad(zz, sz)).astype(dz_ref.dtype)
            dyv = du * silu_z
            dd_ref[...] += _xdot(jnp.sum(dyv * xv, axis=0, keepdims=True), e64, NT)
            pre, dt, a_neg, acum, acum_t = _ssd_chunk_terms(dtr_ref[rows, :], bias_ref[...], alog_ref[...], tril, triu)
            ac = _xdot(acum, e64, NN)
            ac_sq = _xdot(acum, e128, NN)
            dt_w = _xdot(dt, e64, NN)
            xd = xv * dt_w
            ac_last = ac[L - 1:L, :]
            ea = jnp.exp(ac)
            w = jnp.exp(ac_last - ac)
            ea_last = jnp.exp(ac_last)
            sp = st_ref[c]
            ds = dstate[...]
            dye = dyv * ea
            yoff = _bdot(cm, sp, NN) * ea
            bds = _bdot(bm, ds, NN)
            dcm = _bdot(dye, sp, NT)
            dbm = _bdot(xd * w, ds, NT)
            dstate[...] = ds * ea_last + _bdot(cm, dye, TN)
            w8 = jnp.exp(acum[L - 1:L, :] - acum)
            dw8 = _xdot(xd * bds, e64, NT)
            dac8 = _xdot(dyv * yoff, e64, NT) - dw8 * w8
            tail8 = jnp.sum(dw8 * w8, axis=0, keepdims=True) + jnp.exp(acum[L - 1:L, :]) * _xdot(
                jnp.sum(ds * sp, axis=0, keepdims=True), e64, NT)
            dac8 = dac8 + last_row * tail8
            gmat = _bdot(cm, bm, NT)
            dgmat = jnp.zeros((L, L), F32)
            colsum_t = jnp.zeros((HPG, L), F32)
            for q in range(HPG // 2):
                pair = slice(q * 128, (q + 1) * 128)
                xd_tile = xd[:, pair]
                dy_tile = dyv[:, pair]
                dxd_tile = bds[:, pair] * w[:, pair]
                for j, keep in ((2 * q, low), (2 * q + 1, ~low)):
                    lam = jnp.exp(jnp.where(causal, ac_sq[:, j * L:(j + 1) * L] - acum_t[j:j + 1, :], NEG))
                    mh = gmat * lam
                    dyj = jnp.where(keep, dy_tile, 0.0)
                    dxd_tile = dxd_tile + _bdot(mh, dyj, TN)
                    dm = _bdot(dyj, xd_tile, NT)
                    dgmat = dgmat + dm * lam
                    qm = dm * mh
                    dac8 = dac8 + jnp.sum(qm, axis=1, keepdims=True) * (lane8 == j).astype(F32)
                    colsum_t = colsum_t + (sub8 == j).astype(F32) * jnp.sum(qm, axis=0, keepdims=True)
                dxd_ref[:, pair] = dxd_tile
            dac8 = dac8 - _xdot(colsum_t, eye8, TN)
            dxd = dxd_ref[...]
            dx_ref[rows, :] = dxd * dt_w + dyv * dvec
            dc_ref[rows, :] = dcm + _bdot(dgmat, bm, NN)
            db_ref[rows, :] = dbm + _bdot(dgmat, cm, TN)
            da8 = _xdot(triu, dac8, NN, exact_first=True)
            ddt8 = _xdot(dxd * xv, e64, NT) + da8 * a_neg
            dalog_ref[...] += jnp.sum(da8 * dt, axis=0, keepdims=True) * a_neg
            dpre = ddt8 * _sigmoid(pre)
            ddtr_ref[rows, :] = dpre
            dbias_ref[...] += jnp.sum(dpre, axis=0, keepdims=True)

    rev = lambda i: NI - 1 - i
    vec8 = pl.BlockSpec((None, 1, HPG), lambda g, i: (g, 0, 0))
    grp = pl.BlockSpec((T, GW), lambda g, i: (rev(i), g))
    bspec = pl.BlockSpec((T, SSD_STATE), lambda g, i: (rev(i), SSD_WIDTH // SSD_STATE + g))
    cspec = pl.BlockSpec((T, SSD_STATE), lambda g, i: (rev(i), SSD_WIDTH // SSD_STATE + SSD_GROUPS + g))
    gvec = pl.BlockSpec((1, GW), lambda g, i: (0, g))
    st_spec = pl.BlockSpec((CPS, None, SSD_STATE, GW), lambda g, i: (rev(i), g, 0, 0))
    small = jax.ShapeDtypeStruct((SSD_GROUPS, 1, HPG), F32)
    zspec = pl.BlockSpec((T, GW), lambda g, i: (rev(i), Z_COL // GW + g))
    return pl.pallas_call(
        body, grid=(SSD_GROUPS, NI),
        in_specs=[grp, grp, grp, bspec, cspec, zspec, pl.BlockSpec((None, T, HPG), lambda g, i: (g, rev(i), 0)),
                  vec8, vec8, gvec, gvec, st_spec],
        out_specs=[zspec, grp, pl.BlockSpec((T, SSD_STATE), lambda g, i: (rev(i), g)),
                   pl.BlockSpec((T, SSD_STATE), lambda g, i: (rev(i), g)),
                   pl.BlockSpec((None, T, HPG), lambda g, i: (g, rev(i), 0)), vec8, vec8, vec8, gvec],
        out_shape=[jax.ShapeDtypeStruct((S, IN_PAD), BF16), jax.ShapeDtypeStruct((S, SSD_WIDTH), F32),
                   jax.ShapeDtypeStruct((S, SSD_GROUPS * SSD_STATE), F32), jax.ShapeDtypeStruct((S, SSD_GROUPS * SSD_STATE), F32),
                   jax.ShapeDtypeStruct((SSD_GROUPS, S, HPG), F32), small, small, small,
                   jax.ShapeDtypeStruct((1, SSD_WIDTH), F32)],
        scratch_shapes=[pltpu.VMEM((SSD_STATE, GW), F32), pltpu.VMEM((L, GW), F32)],
        name=name, compiler_params=_params("arbitrary", "arbitrary"),
    )(dy, yraw, xbc, xbc, xbc, proj, dtr, dt_bias, a_log, d_rep, gain, states)


def _swap_halves(t):
    w = t.shape[1]
    lane = lax.broadcasted_iota(jnp.int32, t.shape, 1)
    return jnp.where((lane % 64) < 32, pltpu.roll(t, w - 32, axis=1), pltpu.roll(t, 32, axis=1))


def _widen(tab, w):
    return tab if w == 128 else jnp.concatenate([tab] * (w // 128), axis=1)


def _rope(t, cos, sin_signed):
    return t * cos + _swap_halves(t) * sin_signed


def _rope_t(d, cos, sin_signed):
    return d * cos - _swap_halves(d) * sin_signed


def _group_sum64(v, bd):
    hi = v.astype(BF16)
    lo = (v - hi.astype(F32)).astype(BF16)
    return (lax.dot_general(hi, bd, NN, preferred_element_type=F32)
            + lax.dot_general(lo, bd, NN, preferred_element_type=F32))


AQ_BLK = ATT_COL // ATT_WIDTH


def _att_prep_fwd(proj, qg, kg, cos, sin, bd, name):
    S = proj.shape[0]
    T = min(512, S)
    PB = ATT_SPAN // T
    src = lambda i: jnp.maximum(i - PB, 0)

    def body(q_ref, k_ref, v_ref, qg_ref, kg_ref, cos_ref, sin_ref, bd_ref, qo_ref, ko_ref, vo_ref):
        i = pl.program_id(0)

        @pl.when(i < PB)
        def _():
            ko_ref[...] = jnp.zeros_like(ko_ref)
            vo_ref[...] = jnp.zeros_like(vo_ref)

        @pl.when(i >= PB)
        def _():
            cw = _widen(cos_ref[...], ATT_WIDTH)
            sw = _widen(sin_ref[...], ATT_WIDTH)
            bdv = bd_ref[...]

            def norm_rope(t, gain):
                ss = _group_sum64(t * t, bdv)
                return _rope(t * lax.rsqrt(ss * (1.0 / ATT_HEAD_DIM) + EPS) * gain, cw, sw)

            qo_ref[...] = (norm_rope(q_ref[...], qg_ref[...]) * (ATT_HEAD_DIM ** -0.5)).astype(BF16)
            kt = norm_rope(k_ref[...], kg_ref[...]).astype(BF16)
            vt = v_ref[...].astype(BF16)
            for pr in range(ATT_HEADS // 2):
                ko_ref[pr] = kt[:, pr * 128:(pr + 1) * 128]
                vo_ref[pr] = vt[:, pr * 128:(pr + 1) * 128]

    vec = pl.BlockSpec((1, ATT_WIDTH), lambda i: (0, 0))
    tab = pl.BlockSpec((T, 128), lambda i: (src(i), 0))
    hm = pl.BlockSpec((ATT_HEADS // 2, T, 128), lambda i: (0, i, 0))
    hm_shape = jax.ShapeDtypeStruct((ATT_HEADS // 2, ATT_SPAN + S, 128), BF16)
    return pl.pallas_call(
        body, grid=(PB + S // T,),
        in_specs=[pl.BlockSpec((T, ATT_WIDTH), lambda i: (src(i), AQ_BLK)), pl.BlockSpec((T, ATT_WIDTH), lambda i: (src(i), AQ_BLK + 1)),
                  pl.BlockSpec((T, ATT_WIDTH), lambda i: (src(i), AQ_BLK + 2)), vec, vec, tab, tab,
                  pl.BlockSpec((ATT_WIDTH, ATT_WIDTH), lambda i: (0, 0))],
        out_specs=[pl.BlockSpec((T, ATT_WIDTH), lambda i: (src(i), 0)), hm, hm],
        out_shape=[jax.ShapeDtypeStruct((S, ATT_WIDTH), BF16), hm_shape, hm_shape],
        name=name, compiler_params=_params("arbitrary"),
    )(proj, proj, proj, qg, kg, cos, sin, bd)


def _att_prep_bwd(proj, dq, dk_p, dv_p, qg, kg, cos, sin, bd, dproj, name):
    S = proj.shape[0]
    T = min(512, S)
    NI = S // T
    PB = ATT_SPAN // T
    W = ATT_WIDTH

    def body(q_ref, k_ref, dq_ref, dkp_ref, dvp_ref, qg_ref, kg_ref, cos_ref, sin_ref, bd_ref, dproj_ref,
             do_ref, dqg_ref, dkg_ref, acc_ref):
        i = pl.program_id(0)

        @pl.when(i == 0)
        def _():
            acc_ref[...] = jnp.zeros_like(acc_ref)

        npair = ATT_HEADS // 2
        dk_all = jnp.concatenate([dkp_ref[pr].T for pr in range(npair)], axis=1)
        do_ref[:, 2 * W:3 * W] = jnp.concatenate([dvp_ref[pr].T for pr in range(npair)], axis=1).astype(BF16)
        cw = _widen(cos_ref[...], ATT_WIDTH)
        sw = _widen(sin_ref[...], ATT_WIDTH)
        bdv = bd_ref[...]

        def one(t, d_rot, gain, scale, slot):
            ss = _group_sum64(t * t, bdv)
            r = lax.rsqrt(ss * (1.0 / ATT_HEAD_DIM) + EPS)
            n = t * r
            d_ng = _rope_t(d_rot * scale, cw, sw)
            acc_ref[pl.ds(slot, 1), :] += jnp.sum(d_ng * n, axis=0, keepdims=True)
            dn = d_ng * gain
            return r * (dn - n * (_group_sum64(dn * n, bdv) * (1.0 / ATT_HEAD_DIM)))

        do_ref[:, 0:W] = one(q_ref[...], dq_ref[...], qg_ref[...], ATT_HEAD_DIM ** -0.5, 0).astype(BF16)
        do_ref[:, W:2 * W] = one(k_ref[...], dk_all, kg_ref[...], 1.0, 1).astype(BF16)

        @pl.when(i == NI - 1)
        def _():
            a = acc_ref[...]
            f = a[:, 0:64]
            for h in range(1, ATT_HEADS):
                f = f + a[:, h * 64:(h + 1) * 64]
            dqg_ref[...] = f[0:1]
            dkg_ref[...] = f[1:2]

    vec = pl.BlockSpec((1, ATT_WIDTH), lambda i: (0, 0))
    tab = pl.BlockSpec((T, 128), lambda i: (i, 0))
    row = pl.BlockSpec((T, ATT_WIDTH), lambda i: (i, 0))
    g64 = pl.BlockSpec((1, ATT_HEAD_DIM), lambda i: (0, 0))
    padded = pl.BlockSpec((ATT_HEADS // 2, 128, T), lambda i: (0, 0, i + PB))
    return pl.pallas_call(
        body, grid=(NI,),
        in_specs=[pl.BlockSpec((T, ATT_WIDTH), lambda i: (i, AQ_BLK)), pl.BlockSpec((T, ATT_WIDTH), lambda i: (i, AQ_BLK + 1)),
                  row, padded, padded, vec, vec, tab, tab, pl.BlockSpec((ATT_WIDTH, ATT_WIDTH), lambda i: (0, 0)),
                  pl.BlockSpec(memory_space=pl.ANY)],
        out_specs=[pl.BlockSpec((T, 3 * W), lambda i: (i, ATT_COL // (3 * W))), g64, g64],
        out_shape=[jax.ShapeDtypeStruct(dproj.shape, dproj.dtype), jax.ShapeDtypeStruct((1, ATT_HEAD_DIM), F32),
                   jax.ShapeDtypeStruct((1, ATT_HEAD_DIM), F32)],
        input_output_aliases={10: 0},
        scratch_shapes=[pltpu.VMEM((8, ATT_WIDTH), F32)], name=name, compiler_params=_params("arbitrary"),
    )(proj, proj, dq, dk_p, dv_p, qg, kg, cos, sin, bd, dproj)


def _att_bias():
    qpos = np.arange(CHUNK)[:, None] + ATT_SPAN
    kpos = np.arange(ATT_STRIP)[None, :]
    rel = qpos - kpos
    mult = np.zeros((CHUNK, ATT_STRIP), np.float64)
    for window, dil in DILATED_PAIRS:
        mult += (rel >= 0) & (rel % dil == 0) & (rel // dil <= window // dil)
    return np.where(mult > 0, np.log(np.maximum(mult, 1.0)), NEG).astype(np.float32)


def _att_scores(q, ks, bias, i):
    s = _bdot(q, ks, NT) + bias
    kcol = lax.broadcasted_iota(jnp.int32, (1, ATT_STRIP), 1) + i * CHUNK
    return jnp.where(kcol >= ATT_SPAN, s, NEG)


def _pair_masks():
    low = lax.broadcasted_iota(jnp.int32, (CHUNK, 128), 1) < ATT_HEAD_DIM
    return low, ~low


def _att_fwd(q, kp, vp, bias, y, name):
    S = q.shape[0]
    SP = kp.shape[1]

    TQ = ATT_QB * CHUNK

    def body(q_ref, k_ref, v_ref, bias_ref, y_ref, o_ref):
        i = pl.program_id(1)
        for b in range(ATT_QB):
            blk = i * ATT_QB + b
            strip = pl.ds(pl.multiple_of(blk * CHUNK, CHUNK), ATT_STRIP)
            rows = pl.ds(b * CHUNK, CHUNK)
            qv = q_ref[rows, :]
            ks = k_ref[strip, :]
            vs = v_ref[strip, :]
            outs = []
            for keep in _pair_masks():
                s = _att_scores(jnp.where(keep, qv, jnp.zeros_like(qv)), ks, bias_ref[...], blk)
                m = jnp.max(s, axis=-1, keepdims=True)
                p = jnp.exp(s - m)
                den = jnp.sum(p, axis=-1, keepdims=True)
                outs.append(_bdot(p, vs, NN) / den)
            o_ref[rows, :] = jnp.where(_pair_masks()[0], outs[0], outs[1]).astype(o_ref.dtype)

    kv = pl.BlockSpec((None, SP, 128), lambda hp, i: (hp, 0, 0))
    return pl.pallas_call(
        body, grid=(ATT_HEADS // 2, S // TQ),
        in_specs=[pl.BlockSpec((TQ, 128), lambda hp, i: (i, hp)), kv, kv,
                  pl.BlockSpec((CHUNK, ATT_STRIP), lambda hp, i: (0, 0)), pl.BlockSpec(memory_space=pl.ANY)],
        out_specs=pl.BlockSpec((TQ, 128), lambda hp, i: (i, SSD_WIDTH // 128 + hp)),
        out_shape=jax.ShapeDtypeStruct(y.shape, y.dtype), input_output_aliases={4: 0}, name=name,
        compiler_params=_params("parallel", "arbitrary"),
    )(q, kp, vp, bias, y)


def _att_bwd(q, kp, vp, bias, dy, name):
    S = q.shape[0]
    SP = kp.shape[1]

    def body(q_ref, k_ref, v_ref, bias_ref, do_ref, dq_ref, dk_ref, dv_ref):
        i = pl.program_id(1)

        @pl.when(i == 0)
        def _():
            dk_ref[...] = jnp.zeros_like(dk_ref)
            dv_ref[...] = jnp.zeros_like(dv_ref)

        for b in range(ATT_QB):
            blk = i * ATT_QB + b
            strip = pl.ds(pl.multiple_of(blk * CHUNK, CHUNK), ATT_STRIP)
            rows = pl.ds(b * CHUNK, CHUNK)
            qv = q_ref[rows, :]
            dov = do_ref[rows, :]
            ks = k_ref[strip, :]
            vs = v_ref[strip, :]
            dq = jnp.zeros((CHUNK, 128), F32)
            dk_t = jnp.zeros((128, ATT_STRIP), F32)
            dv_t = jnp.zeros((128, ATT_STRIP), F32)
            for keep in _pair_masks():
                qh = jnp.where(keep, qv, jnp.zeros_like(qv))
                doh = jnp.where(keep, dov, 0.0)
                s = _att_scores(qh, ks, bias_ref[...], blk)
                m = jnp.max(s, axis=-1, keepdims=True)
                p = jnp.exp(s - m)
                p = p / jnp.sum(p, axis=-1, keepdims=True)
                dp = _bdot(doh, vs, NT)
                dsc = p * (dp - jnp.sum(p * dp, axis=-1, keepdims=True))
                dq = dq + jnp.where(keep, _bdot(dsc, ks, NN), 0.0)
                dv_t = dv_t + _bdot(doh, p, TN)
                dk_t = dk_t + _bdot(qh, dsc, TN)
            dq_ref[rows, :] = dq
            dv_ref[:, strip] += dv_t
            dk_ref[:, strip] += dk_t

    TQ = ATT_QB * CHUNK
    kv = pl.BlockSpec((None, SP, 128), lambda hp, i: (hp, 0, 0))
    kv_t = pl.BlockSpec((None, 128, SP), lambda hp, i: (hp, 0, 0))
    pairs = jax.ShapeDtypeStruct((ATT_HEADS // 2, 128, SP), F32)
    return pl.pallas_call(
        body, grid=(ATT_HEADS // 2, S // TQ),
        in_specs=[pl.BlockSpec((TQ, 128), lambda hp, i: (i, hp)), kv, kv,
                  pl.BlockSpec((CHUNK, ATT_STRIP), lambda hp, i: (0, 0)),
                  pl.BlockSpec((TQ, 128), lambda hp, i: (i, SSD_WIDTH // 128 + hp))],
        out_specs=[pl.BlockSpec((TQ, 128), lambda hp, i: (i, hp)), kv_t, kv_t],
        out_shape=[jax.ShapeDtypeStruct((S, ATT_WIDTH), F32), pairs, pairs],
        name=name, compiler_params=_params("parallel", "arbitrary"),
    )(q, kp, vp, bias, dy)


RQ_BLK = RET_COL // RET_QK_WIDTH
RV_BLK = (RET_COL + 2 * RET_QK_WIDTH) // RET_V_WIDTH
RET_PAIR = 2 * RET_QK_DIM
RET_LOG_GAMMA = tuple(math.log1p(-2.0 ** (-5.0 - h)) for h in range(RET_HEADS))


def _ret_decays(h):
    L = CHUNK
    lg = RET_LOG_GAMMA[h]
    row = lax.broadcasted_iota(jnp.int32, (L, L), 0)
    col = lax.broadcasted_iota(jnp.int32, (L, L), 1)
    rel = (row - col).astype(F32)
    dm = jnp.where(rel >= 0, jnp.exp(jnp.maximum(rel, 0.0) * lg), 0.0)
    idx = lax.broadcasted_iota(jnp.int32, (L, 1), 0).astype(F32)
    kte = jnp.exp((L - 1 - idx) * lg)
    qfs = jnp.exp((idx + 1.0) * lg)
    return dm, kte, qfs, math.exp(L * lg)


def _ret_head(t, h):
    tile = t[:, (h // 2) * RET_PAIR:(h // 2 + 1) * RET_PAIR]
    low = lax.broadcasted_iota(jnp.int32, tile.shape, 1) < RET_QK_DIM
    return jnp.where(low if h % 2 == 0 else ~low, tile, 0.0)


def _ret_fwd(proj, cos, sin, gain, y, name):
    S = proj.shape[0]
    L = CHUNK
    T = min(512, S)
    CPS = T // L
    NC = S // L

    def body(q_ref, k_ref, v_ref, g_ref, cos_ref, sin_ref, gain_ref, yin_ref, y_ref, o_ref, st_ref, state):
        i = pl.program_id(0)

        @pl.when(i == 0)
        def _():
            state[...] = jnp.zeros_like(state)

        dec = [_ret_decays(h) for h in range(RET_HEADS)]
        for c in range(CPS):
            rows = pl.ds(c * L, L)
            cw = _widen(cos_ref[rows, :], RET_QK_WIDTH)
            sw = _widen(sin_ref[rows, :], RET_QK_WIDTH)
            qv = _rope(q_ref[rows, :], cw, sw)
            kv = _rope(k_ref[rows, :], cw, sw) * (RET_QK_DIM ** -0.5)
            for h in range(RET_HEADS):
                dm, kte, qfs, cd = dec[h]
                qh, kh = _ret_head(qv, h), _ret_head(kv, h)
                vs = slice(h * RET_V_DIM, (h + 1) * RET_V_DIM)
                vh = v_ref[rows, vs]
                sp = state[h]
                st_ref[c, h] = sp
                o = _bdot(_bdot(qh, kh, NT) * dm, vh, NN) + _bdot(qh * qfs, sp, NN)
                state[h] = cd * sp + _bdot(kh * kte, vh, TN)
                o_ref[rows, vs] = o
                gh = g_ref[rows, vs]
                r = lax.rsqrt(jnp.mean(o * o, axis=-1, keepdims=True) + EPS)
                y_ref[rows, vs] = (o * r * gain_ref[:, vs] * (gh * _sigmoid(gh))).astype(y_ref.dtype)

    tab = pl.BlockSpec((T, 128), lambda i: (i, 0))
    wide = pl.BlockSpec((T, RET_V_WIDTH), lambda i: (i, 0))
    return pl.pallas_call(
        body, grid=(S // T,),
        in_specs=[pl.BlockSpec((T, RET_QK_WIDTH), lambda i: (i, RQ_BLK)), pl.BlockSpec((T, RET_QK_WIDTH), lambda i: (i, RQ_BLK + 1)),
                  pl.BlockSpec((T, RET_V_WIDTH), lambda i: (i, RV_BLK)), pl.BlockSpec((T, RET_V_WIDTH), lambda i: (i, RV_BLK + 1)),
                  tab, tab, pl.BlockSpec((1, RET_V_WIDTH), lambda i: (0, 0)), pl.BlockSpec(memory_space=pl.ANY)],
        out_specs=[pl.BlockSpec((T, RET_V_WIDTH), lambda i: (i, (SSD_WIDTH + ATT_WIDTH) // RET_V_WIDTH)), wide,
                   pl.BlockSpec((CPS, RET_HEADS, RET_PAIR, RET_V_DIM), lambda i: (i, 0, 0, 0))],
        out_shape=[jax.ShapeDtypeStruct(y.shape, y.dtype), jax.ShapeDtypeStruct((S, RET_V_WIDTH), F32),
                   jax.ShapeDtypeStruct((NC, RET_HEADS, RET_PAIR, RET_V_DIM), F32)],
        input_output_aliases={7: 0},
        scratch_shapes=[pltpu.VMEM((RET_HEADS, RET_PAIR, RET_V_DIM), F32)], name=name,
        compiler_params=_params("arbitrary"),
    )(proj, proj, proj, proj, cos, sin, gain, y)


def _ret_bwd(dy, oraw, proj, cos, sin, gain, states, dproj, name):
    S = proj.shape[0]
    L = CHUNK
    T = min(512, S)
    CPS = T // L
    NI = S // T
    QW, VW = RET_QK_WIDTH, RET_V_WIDTH
    V0, G0 = 2 * QW, 2 * QW + VW

    def body(dy_ref, o_ref, q_ref, k_ref, v_ref, g_ref, cos_ref, sin_ref, gain_ref, st_ref, dproj_ref,
             out_ref, dgain_ref, dstate, dqs, dks):
        i = pl.program_id(0)

        @pl.when(i == 0)
        def _():
            dstate[...] = jnp.zeros_like(dstate)
            dgain_ref[...] = jnp.zeros_like(dgain_ref)

        dec = [_ret_decays(h) for h in range(RET_HEADS)]
        for c in reversed(range(CPS)):
            rows = pl.ds(c * L, L)
            cw = _widen(cos_ref[rows, :], RET_QK_WIDTH)
            sw = _widen(sin_ref[rows, :], RET_QK_WIDTH)
            qv = _rope(q_ref[rows, :], cw, sw)
            kv = _rope(k_ref[rows, :], cw, sw) * (RET_QK_DIM ** -0.5)
            for h in range(RET_HEADS):
                dm, kte, qfs, cd = dec[h]
                pair = slice((h // 2) * RET_PAIR, (h // 2 + 1) * RET_PAIR)
                vs = slice(h * RET_V_DIM, (h + 1) * RET_V_DIM)
                qh, kh = _ret_head(qv, h), _ret_head(kv, h)
                vh = v_ref[rows, vs]
                gh = g_ref[rows, vs]
                gn = gain_ref[:, vs]
                o = o_ref[rows, vs]
                dyh = dy_ref[rows, vs]
                sg = _sigmoid(gh)
                silu_g = gh * sg
                r = lax.rsqrt(jnp.mean(o * o, axis=-1, keepdims=True) + EPS)
                n = o * r
                dgain_ref[:, vs] += jnp.sum(dyh * n * silu_g, axis=0, keepdims=True)
                out_ref[rows, G0 + h * RET_V_DIM:G0 + (h + 1) * RET_V_DIM] = (dyh * n * gn * _silu_grad(gh, sg)).astype(out_ref.dtype)
                dn = dyh * gn * silu_g
                do = r * (dn - n * jnp.mean(dn * n, axis=-1, keepdims=True))
                sp = st_ref[c, h]
                ds = dstate[h]
                sc = _bdot(qh, kh, NT) * dm
                dsc = _bdot(do, vh, NT) * dm
                out_ref[rows, V0 + h * RET_V_DIM:V0 + (h + 1) * RET_V_DIM] = (_bdot(sc, do, TN) + _bdot(kh * kte, ds, NN)).astype(out_ref.dtype)
                dqh = _bdot(dsc, kh, NN) + _bdot(do, sp, NT) * qfs
                dkh = _bdot(dsc, qh, TN) + _bdot(vh, ds, NT) * kte
                if h % 2 == 0:
                    dqs[:, pair] = dqh
                    dks[:, pair] = dkh
                else:
                    dqs[:, pair] += dqh
                    dks[:, pair] += dkh
                dstate[h] = cd * ds + _bdot(qh * qfs, do, TN)
            out_ref[rows, 0:QW] = _rope_t(dqs[...], cw, sw).astype(out_ref.dtype)
            out_ref[rows, QW:2 * QW] = _rope_t(dks[...] * (RET_QK_DIM ** -0.5), cw, sw).astype(out_ref.dtype)

    rev = lambda i: NI - 1 - i
    tab = pl.BlockSpec((T, 128), lambda i: (rev(i), 0))
    wide = pl.BlockSpec((T, RET_V_WIDTH), lambda i: (rev(i), 0))
    group = pl.BlockSpec((T, G0 + VW), lambda i: (rev(i), RET_COL // (G0 + VW)))
    gvec = pl.BlockSpec((1, RET_V_WIDTH), lambda i: (0, 0))
    return pl.pallas_call(
        body, grid=(NI,),
        in_specs=[pl.BlockSpec((T, RET_V_WIDTH), lambda i: (rev(i), (SSD_WIDTH + ATT_WIDTH) // RET_V_WIDTH)), wide,
                  pl.BlockSpec((T, RET_QK_WIDTH), lambda i: (rev(i), RQ_BLK)), pl.BlockSpec((T, RET_QK_WIDTH), lambda i: (rev(i), RQ_BLK + 1)),
                  pl.BlockSpec((T, RET_V_WIDTH), lambda i: (rev(i), RV_BLK)), pl.BlockSpec((T, RET_V_WIDTH), lambda i: (rev(i), RV_BLK + 1)),
                  tab, tab, gvec,
                  pl.BlockSpec((CPS, RET_HEADS, RET_PAIR, RET_V_DIM), lambda i: (rev(i), 0, 0, 0)),
                  pl.BlockSpec(memory_space=pl.ANY)],
        out_specs=[group, gvec],
        out_shape=[jax.ShapeDtypeStruct(dproj.shape, dproj.dtype), jax.ShapeDtypeStruct((1, RET_V_WIDTH), F32)],
        input_output_aliases={10: 0},
        scratch_shapes=[pltpu.VMEM((RET_HEADS, RET_PAIR, RET_V_DIM), F32), pltpu.VMEM((L, RET_QK_WIDTH), F32),
                        pltpu.VMEM((L, RET_QK_WIDTH), F32)],
        name=name, compiler_params=_params("arbitrary"),
    )(dy, oraw, proj, proj, proj, proj, cos, sin, gain, states, dproj)


def _adamw_update(g_ref, nb, w_ref, m_ref, v_ref, go_ref, d_ref, mo_ref, vo_ref):
    g = g_ref[0].astype(F32)
    for k in range(1, nb):
        g = g + g_ref[k].astype(F32)
    mn = ADAM_B1 * m_ref[...] + (1.0 - ADAM_B1) * g
    vn = ADAM_B2 * v_ref[...] + (1.0 - ADAM_B2) * (g * g)
    go_ref[...] = g
    mo_ref[...] = mn
    vo_ref[...] = vn
    c1 = 1.0 - ADAM_B1 ** ADAM_STEP
    c2 = 1.0 - ADAM_B2 ** ADAM_STEP
    d_ref[...] = -ADAM_LR * ((mn / c1) / (jnp.sqrt(vn / c2) + ADAM_EPS) + ADAM_WD * w_ref[...])


def _adamw_rows(R, C):
    return _pick(R, tuple(t for t in (512, 256, 128, 64, 32, 16, 8) if t * C <= 256 * 1024))


def _adamw(gblocks, w, m, v, name):
    nb, R, C = gblocks.shape
    tr = _adamw_rows(R, C)

    def body(g_ref, *refs):
        _adamw_update(g_ref, nb, *refs)

    row = pl.BlockSpec((tr, C), lambda i: (i, 0))
    sh = jax.ShapeDtypeStruct((R, C), F32)
    return pl.pallas_call(
        body, grid=(R // tr,), in_specs=[pl.BlockSpec((nb, tr, C), lambda i: (0, i, 0)), row, row, row],
        out_specs=[row, row, row, row], out_shape=[sh, sh, sh, sh], name=name, compiler_params=_params("parallel"),
    )(gblocks, w, m, v)


def _adamw_layers(g0, g1, w, m, v, name):
    nb, R, C = g0.shape
    tr = _adamw_rows(R, C)

    def body(g0_ref, g1_ref, *refs):
        l = pl.program_id(0)

        @pl.when(l == 0)
        def _():
            _adamw_update(g0_ref, nb, *refs)

        @pl.when(l == 1)
        def _():
            _adamw_update(g1_ref, nb, *refs)

    row = pl.BlockSpec((None, tr, C), lambda l, i: (l, i, 0))
    sh = jax.ShapeDtypeStruct((DEPTH, R, C), F32)
    return pl.pallas_call(
        body, grid=(DEPTH, R // tr),
        in_specs=[pl.BlockSpec((nb, tr, C), lambda l, i: (0, i * (1 - l), 0)), pl.BlockSpec((nb, tr, C), lambda l, i: (0, i * l, 0)),
                  row, row, row],
        out_specs=[row, row, row, row], out_shape=[sh, sh, sh, sh], name=name, compiler_params=_params("arbitrary", "arbitrary"),
    )(g0, g1, w, m, v)


def _peers():
    x, y, c = lax.axis_index("x"), lax.axis_index("y"), lax.axis_index("c")
    flips = ((0, 0, 1), (1, 0, 0), (0, 1, 0), (1, 1, 0), (1, 0, 1), (0, 1, 1), (1, 1, 1))
    me = 4 * x + 2 * y + c
    peers = [(x ^ fx, y ^ fy, c ^ fc) for fx, fy, fc in flips]
    return me, peers


def _exchange(arrs, scatter, name):
    n = len(arrs)
    npeer = N_DEV - 1

    def body(*refs):
        ins, outs = refs[:n], refs[n:2 * n]
        send_sems, recv_sems, local_sems = refs[2 * n:]
        me, peers = _peers()
        copies = []
        for a in range(n):
            src_own = ins[a].at[me] if scatter else ins[a]
            own = pltpu.make_async_copy(src_own, outs[a].at[me], local_sems.at[a])
            own.start()
            copies.append(own)
            for k, peer in enumerate(peers):
                src = ins[a].at[4 * peer[0] + 2 * peer[1] + peer[2]] if scatter else ins[a]
                cp = pltpu.make_async_remote_copy(
                    src_ref=src, dst_ref=outs[a].at[me], send_sem=send_sems.at[a * npeer + k],
                    recv_sem=recv_sems.at[a * npeer + k], device_id=peer, device_id_type=pl.DeviceIdType.MESH)
                cp.start()
                copies.append(cp)
        for cp in copies:
            cp.wait()

    out_shape = [jax.ShapeDtypeStruct(((N_DEV,) + a.shape[1:]) if scatter else ((N_DEV,) + a.shape), a.dtype) for a in arrs]
    anyspec = pl.BlockSpec(memory_space=pl.ANY)
    return pl.pallas_call(
        body, in_specs=[anyspec] * n, out_specs=[anyspec] * n, out_shape=out_shape,
        scratch_shapes=[pltpu.SemaphoreType.DMA((n * npeer,)), pltpu.SemaphoreType.DMA((n * npeer,)),
                        pltpu.SemaphoreType.DMA((n,))],
        name=name,
    )(*arrs)


def _dev_index(peer):
    return 4 * peer[0] + 2 * peer[1] + peer[2]


def _push_copies(src_refs, land_refs, send_sems, recv_sems, scatter, as_receiver):
    me, peers = _peers()
    npeer = N_DEV - 1
    copies = []
    for a in range(len(src_refs)):
        for k, peer in enumerate(peers):
            src = src_refs[a].at[_dev_index(peer)] if scatter else src_refs[a]
            slot = _dev_index(peer) if as_receiver else me
            copies.append(pltpu.make_async_remote_copy(
                src_ref=src, dst_ref=land_refs[a].at[slot], send_sem=send_sems.at[a * npeer + k],
                recv_sem=recv_sems.at[a * npeer + k], device_id=peer, device_id_type=pl.DeviceIdType.MESH))
    return copies


def _own_copies(src_refs, land_refs, own_sems, scatter):
    me, _ = _peers()
    return [pltpu.make_async_copy(src_refs[a].at[me] if scatter else src_refs[a], land_refs[a].at[me], own_sems.at[a])
            for a in range(len(src_refs))]


def _push_start(srcs, scatter, name):
    n = len(srcs)
    nsem = n * (N_DEV - 1)

    def body(*refs):
        srcs_r, lands_r = refs[:n], refs[n:2 * n]
        for cp in _push_copies(srcs_r, lands_r, refs[2 * n], refs[2 * n + 1], scatter, False):
            cp.start()
        for cp in _own_copies(srcs_r, lands_r, refs[2 * n + 2], scatter):
            cp.start()
        token = refs[-1]
        token[...] = jnp.zeros_like(token)

    hbm = pl.BlockSpec(memory_space=pltpu.HBM)
    sem = pl.BlockSpec(memory_space=pltpu.SEMAPHORE)
    lands = [lax.empty((N_DEV,) + (s.shape[1:] if scatter else s.shape), s.dtype) for s in srcs]
    arrs = list(srcs) + lands
    return pl.pallas_call(
        body, name=name,
        out_shape=(pltpu.SemaphoreType.DMA((nsem,)), pltpu.SemaphoreType.DMA((nsem,)), pltpu.SemaphoreType.DMA((n,)),
                   *[pltpu.HBM(a.shape, a.dtype) for a in arrs], jax.ShapeDtypeStruct((8, 128), F32)),
        in_specs=[hbm] * (2 * n), out_specs=(sem, sem, sem, *([hbm] * (2 * n)), pl.BlockSpec(memory_space=pltpu.VMEM)),
        input_output_aliases={i: 3 + i for i in range(2 * n)},
        compiler_params=pltpu.CompilerParams(has_side_effects=pltpu.SideEffectType.DATAFLOW_SIDE_EFFECTING),
    )(*[pltpu.with_memory_space_constraint(a, pltpu.HBM) for a in arrs])


def _push_wait(handle, after, scatter, name):
    send_sems, recv_sems, own_sems, *thru, _ = handle
    n = len(thru) // 2

    def body(*refs):
        srcs_r, lands_r = refs[:n], refs[n:2 * n]
        for cp in _push_copies(srcs_r, lands_r, refs[2 * n], refs[2 * n + 1], scatter, True):
            cp.wait_send()
            cp.wait_recv()
        for cp in _own_copies(srcs_r, lands_r, refs[2 * n + 2], scatter):
            cp.wait()

    hbm = pl.BlockSpec(memory_space=pltpu.HBM)
    sem = pl.BlockSpec(memory_space=pltpu.SEMAPHORE)
    outs = pl.pallas_call(
        body, name=name, out_shape=tuple(pltpu.HBM(a.shape, a.dtype) for a in thru),
        in_specs=[hbm] * (2 * n) + [sem, sem, sem, pl.BlockSpec(memory_space=pl.ANY)], out_specs=tuple([hbm] * (2 * n)),
        input_output_aliases={i: i for i in range(2 * n)},
        compiler_params=pltpu.CompilerParams(has_side_effects=pltpu.SideEffectType.DATAFLOW_SIDE_EFFECTING),
    )(*thru, send_sems, recv_sems, own_sems, after)
    return list(outs[n:])


def _relay_copies(src_refs, land_refs, send1, recv1, send2, recv2, as_receiver):
    me, peers = _peers()
    sibling, chips = peers[0], peers[1:4]
    first, second = [], []
    for a in range(len(src_refs)):
        for k, peer in enumerate([sibling] + chips):
            slot = _dev_index(peer) if as_receiver else me
            first.append(pltpu.make_async_remote_copy(
                src_ref=src_refs[a], dst_ref=land_refs[a].at[slot], send_sem=send1.at[4 * a + k], recv_sem=recv1.at[4 * a + k],
                device_id=peer, device_id_type=pl.DeviceIdType.MESH))
        for k, chip in enumerate(chips):
            origin = _dev_index(chip)
            slot = origin ^ 1 if as_receiver else origin
            second.append(pltpu.make_async_remote_copy(
                src_ref=land_refs[a].at[origin], dst_ref=land_refs[a].at[slot], send_sem=send2.at[3 * a + k],
                recv_sem=recv2.at[3 * a + k], device_id=sibling, device_id_type=pl.DeviceIdType.MESH))
    return first, second


def _relay_start(srcs, name):
    n = len(srcs)

    def body(*refs):
        srcs_r, lands_r = refs[:n], refs[n:2 * n]
        for cp in _relay_copies(srcs_r, lands_r, refs[2 * n], refs[2 * n + 1], refs[2 * n], refs[2 * n + 1], False)[0]:
            cp.start()
        for cp in _own_copies(srcs_r, lands_r, refs[2 * n + 2], False):
            cp.start()
        token = refs[-1]
        token[...] = jnp.zeros_like(token)

    hbm = pl.BlockSpec(memory_space=pltpu.HBM)
    sem = pl.BlockSpec(memory_space=pltpu.SEMAPHORE)
    arrs = list(srcs) + [lax.empty((N_DEV,) + s.shape, s.dtype) for s in srcs]
    return pl.pallas_call(
        body, name=name,
        out_shape=(pltpu.SemaphoreType.DMA((4 * n,)), pltpu.SemaphoreType.DMA((4 * n,)), pltpu.SemaphoreType.DMA((n,)),
                   *[pltpu.HBM(a.shape, a.dtype) for a in arrs], jax.ShapeDtypeStruct((8, 128), F32)),
        in_specs=[hbm] * (2 * n), out_specs=(sem, sem, sem, *([hbm] * (2 * n)), pl.BlockSpec(memory_space=pltpu.VMEM)),
        input_output_aliases={i: 3 + i for i in range(2 * n)},
        compiler_params=pltpu.CompilerParams(has_side_effects=pltpu.SideEffectType.DATAFLOW_SIDE_EFFECTING),
    )(*[pltpu.with_memory_space_constraint(a, pltpu.HBM) for a in arrs])


def _relay_forward(handle, after, name):
    _, recv1, _, *thru, _ = handle
    n = len(thru) // 2

    def body(*refs):
        srcs_r, lands_r, recv1_r = refs[:n], refs[n:2 * n], refs[2 * n]
        send2_r, recv2_r = refs[2 * n + 2], refs[2 * n + 3]
        first, second = _relay_copies(srcs_r, lands_r, recv1_r, recv1_r, send2_r, recv2_r, True)
        for a in range(n):
            for k in range(1, 4):
                first[4 * a + k].wait_recv()
        for cp in _relay_copies(srcs_r, lands_r, recv1_r, recv1_r, send2_r, recv2_r, False)[1]:
            cp.start()
        token = refs[-1]
        token[...] = jnp.zeros_like(token)

    hbm = pl.BlockSpec(memory_space=pltpu.HBM)
    sem = pl.BlockSpec(memory_space=pltpu.SEMAPHORE)
    return pl.pallas_call(
        body, name=name,
        out_shape=(pltpu.SemaphoreType.DMA((3 * n,)), pltpu.SemaphoreType.DMA((3 * n,)),
                   *[pltpu.HBM(a.shape, a.dtype) for a in thru], jax.ShapeDtypeStruct((8, 128), F32)),
        in_specs=[hbm] * (2 * n) + [sem, pl.BlockSpec(memory_space=pl.ANY)],
        out_specs=(sem, sem, *([hbm] * (2 * n)), pl.BlockSpec(memory_space=pltpu.VMEM)),
        input_output_aliases={i: 2 + i for i in range(2 * n)},
        compiler_params=pltpu.CompilerParams(has_side_effects=pltpu.SideEffectType.DATAFLOW_SIDE_EFFECTING),
    )(*thru, recv1, after)


def _relay_wait(handle, forwarded, after, name):
    send1, recv1, own_sems, *_ = handle
    send2, recv2, *thru, _ = forwarded
    n = len(thru) // 2

    def body(*refs):
        srcs_r, lands_r = refs[:n], refs[n:2 * n]
        send1_r, recv1_r, own_r, send2_r, recv2_r = refs[2 * n:2 * n + 5]
        first, second = _relay_copies(srcs_r, lands_r, send1_r, recv1_r, send2_r, recv2_r, True)
        for i, cp in enumerate(first):
            cp.wait_send()
            if i % 4 == 0:
                cp.wait_recv()
        for cp in second:
            cp.wait_send()
            cp.wait_recv()
        for cp in _own_copies(srcs_r, lands_r, own_r, False):
            cp.wait()

    hbm = pl.BlockSpec(memory_space=pltpu.HBM)
    sem = pl.BlockSpec(memory_space=pltpu.SEMAPHORE)
    outs = pl.pallas_call(
        body, name=name, out_shape=tuple(pltpu.HBM(a.shape, a.dtype) for a in thru),
        in_specs=[hbm] * (2 * n) + [sem] * 5 + [pl.BlockSpec(memory_space=pl.ANY)], out_specs=tuple([hbm] * (2 * n)),
        input_output_aliases={i: i for i in range(2 * n)},
        compiler_params=pltpu.CompilerParams(has_side_effects=pltpu.SideEffectType.DATAFLOW_SIDE_EFFECTING),
    )(*thru, send1, recv1, own_sems, send2, recv2, after)
    return list(outs[n:])


def _tables(S):
    pos = jnp.arange(S, dtype=F32)
    inv = ROPE_THETA ** (-jnp.arange(0, ATT_HEAD_DIM, 2, dtype=F32) / ATT_HEAD_DIM)
    ang = pos[:, None] * inv[None, :]
    cos, sin = jnp.cos(ang), jnp.sin(ang)
    cos128 = jnp.tile(cos, (1, 4))
    sin128 = jnp.tile(jnp.concatenate([-sin, sin], axis=1), (1, 2))
    lane = np.arange(ATT_WIDTH)
    bd = jnp.asarray((lane[:, None] // 64 == lane[None, :] // 64).astype(np.float32), dtype=BF16)
    return cos128, sin128, bd, jnp.asarray(_att_bias())


def _layer_fwd(l, x, p, tabs, early=None, late=None):
    cos, sin, bd, bias = tabs
    S = x.shape[0]
    row = lambda v: v.reshape(1, -1)
    hn = _rmsnorm_fwd(x, row(p["ln_mix"]), f"norm_mix_fwd{l}")
    if early is not None:
        p.update(early(hn))
    proj = _mm(hn, p["w_in"], "nn", f"in_proj{l}", tn=1920)
    xbc = _conv_fwd(proj, p["conv_w"], row(p["conv_b"]), f"conv_fwd{l}")
    dtr = proj[:, DT_COL:DT_COL + SSD_HEADS].reshape(S, SSD_GROUPS, HPG).transpose(1, 0, 2)
    grp = lambda v: v.reshape(SSD_GROUPS, 1, HPG)
    d_rep = row(jnp.repeat(p["d_skip"], SSD_HEAD_DIM))
    y, yraw, ssd_st = _ssd_fwd(xbc, proj, dtr, grp(p["dt_bias"]), grp(p["a_log"]), d_rep, row(p["ssd_norm"]), f"ssd_fwd{l}")
    qg = row(jnp.tile(p["q_norm"], ATT_HEADS))
    kg = row(jnp.tile(p["k_norm"], ATT_HEADS))
    aq, akp, avp = _att_prep_fwd(proj, qg, kg, cos, sin, bd, f"att_prep_fwd{l}")
    y = _att_fwd(aq, akp, avp, bias, y, f"att_fwd{l}")
    y, oraw, ret_st = _ret_fwd(proj, cos, sin, row(p["ret_norm"]), y, f"ret_fwd{l}")
    if late is not None:
        p.update(late(y))
    x1 = _mm(y, p["w_out"], "nn", f"out_proj{l}", residual=x)
    hn2 = _rmsnorm_fwd(x1, row(p["ln_ffn"]), f"norm_ffn_fwd{l}")
    g, u, act = _swiglu_fwd(hn2, p["w_gate"], p["w_up"], f"swiglu_fwd{l}")
    x2 = _mm(act, p["w_down"], "nn", f"down_proj{l}", residual=x1, tk=2816)
    saved = dict(x=x, hn=hn, proj=proj, xbc=xbc, dtr=dtr, yraw=yraw, ssd_st=ssd_st, aq=aq, akp=akp, avp=avp,
                 oraw=oraw, ret_st=ret_st, y=y, x1=x1, hn2=hn2, g=g, u=u, act=act, d_rep=d_rep, qg=qg, kg=kg)
    return x2, saved


def _layer_bwd(l, dx2, dx2_bf, p, sv, tabs, on_ffn=None, on_all=None):
    cos, sin, bd, bias = tabs
    S = dx2.shape[0]
    row = lambda v: v.reshape(1, -1)
    grp = lambda v: v.reshape(SSD_GROUPS, 1, HPG)
    gr = {}
    dg, du = _swiglu_bwd(dx2_bf, p["w_down"], sv["g"], sv["u"], f"swiglu_bwd{l}")
    gr["w_down"] = _mm(sv["act"], dx2_bf, "tn", f"down_wgrad{l}", out_dtype=BF16, tm=1408, tn=1024, tk=2048)
    dhn2 = _mm_nt2(dg, p["w_gate"], du, p["w_up"], f"ffn_dgrad{l}")
    gr["w_gate"] = _mm(sv["hn2"], dg, "tn", f"gate_wgrad{l}", out_dtype=BF16, tm=512, tn=2816, tk=2048)
    gr["w_up"] = _mm(sv["hn2"], du, "tn", f"up_wgrad{l}", out_dtype=BF16, tm=512, tn=2816, tk=2048)
    ffn_gain = row(p["ln_ffn"]) + (on_ffn(gr)[0, 0] if on_ffn is not None else 0.0)
    dx1, dx1_bf, dln_ffn = _rmsnorm_bwd(sv["x1"], dhn2, ffn_gain, dx2, f"norm_ffn_bwd{l}")
    gr["ln_ffn"] = dln_ffn[0]
    dy = _mm(dx1_bf, p["w_out"], "nt", f"out_dgrad{l}")
    gr["w_out"] = _mm(sv["y"], dx1_bf, "tn", f"out_wgrad{l}", out_dtype=BF16, tm=1024, tn=1024, tk=2048)
    dproj, dxs, dbm, dcm, ddtr, dbias, dalog, dd, dssd_gain = _ssd_bwd(
        dy, sv["yraw"], sv["xbc"], sv["proj"], sv["dtr"], grp(p["dt_bias"]), grp(p["a_log"]), sv["d_rep"],
        row(p["ssd_norm"]), sv["ssd_st"], f"ssd_bwd{l}")
    gr["dt_bias"], gr["a_log"], gr["d_skip"] = dbias.reshape(-1), dalog.reshape(-1), dd.reshape(-1)
    gr["ssd_norm"] = dssd_gain[0]
    dproj, dconv_w, dconv_b = _conv_bwd(dxs, dbm, dcm, sv["proj"], p["conv_w"], row(p["conv_b"]), dproj, f"conv_bwd{l}")
    gr["conv_w"], gr["conv_b"] = dconv_w, dconv_b[0]
    dq, dk_p, dv_p = _att_bwd(sv["aq"], sv["akp"], sv["avp"], bias, dy, f"att_bwd{l}")
    dproj, dqg, dkg = _att_prep_bwd(sv["proj"], dq, dk_p, dv_p, sv["qg"], sv["kg"], cos, sin, bd, dproj, f"att_prep_bwd{l}")
    gr["q_norm"], gr["k_norm"] = dqg[0], dkg[0]
    dproj, dret_gain = _ret_bwd(dy, sv["oraw"], sv["proj"], cos, sin, row(p["ret_norm"]), sv["ret_st"], dproj, f"ret_bwd{l}")
    gr["ret_norm"] = dret_gain[0]
    ddt_cols = ddtr.transpose(1, 0, 2).reshape(S, SSD_HEADS).astype(BF16)
    dproj = lax.dynamic_update_slice(dproj, jnp.pad(ddt_cols, ((0, 0), (0, IN_PAD - DT_COL - SSD_HEADS))), (0, DT_COL))
    gr["w_in"] = _mm(sv["hn"], dproj, "tn", f"in_wgrad{l}", out_dtype=BF16, tm=1024, tn=1920, tk=2048)
    launched = on_all(gr) if on_all is not None else None
    dhn = _mm(dproj, p["w_in"], "nt", f"in_dgrad{l}", tk=1920, after=launched)
    dx0, dx0_bf, dln_mix = _rmsnorm_bwd(sv["x"], dhn, row(p["ln_mix"]), dx1, f"norm_mix_bwd{l}")
    gr["ln_mix"] = dln_mix[0]
    return dx0, dx0_bf, gr


def _local_step(x, tgt, layers, early=None, late=None, on_ffn=None, on_all=None):
    n = len(layers)
    none = [None] * n
    early, late, on_ffn, on_all = early or none, late or none, on_ffn or none, on_all or none
    tabs = _tables(x.shape[0])
    saved, params = [], []
    h = x
    for l in range(n):
        p = dict(layers[l](h) if callable(layers[l]) else layers[l])
        h, sv = _layer_fwd(l, h, p, tabs, early[l], late[l])
        saved.append(sv)
        params.append(p)
    dh, dh_bf, lacc = _loss_grad(h, tgt, "loss_grad")
    grads = [None] * n
    for l in reversed(range(n)):
        dh, dh_bf, grads[l] = _layer_bwd(l, dh, dh_bf, params[l], saved[l], tabs, on_ffn[l], on_all[l])
    return lacc[0, 0], dh, grads


BIG = ("w_in", "w_out", "w_gate", "w_up", "w_down")
SMALL = ("ln_mix", "conv_b", "dt_bias", "a_log", "d_skip", "ssd_norm", "q_norm", "k_norm", "ret_norm", "ln_ffn")
ORDER = ("ln_mix", "w_in", "conv_w", "conv_b", "dt_bias", "a_log", "d_skip", "ssd_norm", "q_norm", "k_norm", "ret_norm",
         "w_out", "ln_ffn", "w_gate", "w_up", "w_down")


COL_SHARDED = ("w_in", "w_gate", "w_up", "conv_w")


IN_GROUPS = ((ORIG_Z_XBC, Z_COL), (ORIG_DT, DT_COL), (ORIG_ATT, ATT_COL), (ORIG_RET, RET_COL))


def _full_weight(k, gathered):
    if k == "w_in":
        cs = gathered.shape[2]
        pieces = []
        for (lo, hi), _ in sorted(IN_GROUPS, key=lambda grp: grp[1]):
            for j in range(N_DEV):
                a, b = max(lo, j * cs), min(hi, (j + 1) * cs)
                if a < b:
                    pieces.append(gathered[j][:, a - j * cs:b - j * cs])
        pieces.append(jnp.zeros((gathered.shape[1], IN_PAD - IN_WIDTH), gathered.dtype))
        return jnp.concatenate(pieces, axis=1)
    if k in COL_SHARDED:
        return gathered.transpose(1, 0, 2).reshape(gathered.shape[1], -1)
    return gathered.reshape(-1, gathered.shape[2])


def _shard_block(k, g):
    if k == "w_in":
        cs = IN_WIDTH // N_DEV
        blocks = []
        for j in range(N_DEV):
            pieces = []
            for (lo, hi), col in IN_GROUPS:
                a, b = max(lo, j * cs), min(hi, (j + 1) * cs)
                if a < b:
                    pieces.append(g[:, col + a - lo:col + b - lo])
            blocks.append(jnp.concatenate(pieces, axis=1))
        return jnp.stack(blocks)
    if k in COL_SHARDED:
        return g.reshape(g.shape[0], N_DEV, -1).transpose(1, 0, 2)
    return g.reshape(N_DEV, -1, g.shape[1])


def kernel(x, ln_mix, w_in, conv_w, conv_b, dt_bias, a_log, d_skip, ssd_norm, q_norm, k_norm, ret_norm, w_out, ln_ffn, w_gate, w_up, w_down, loss_target, m_ln_mix, m_w_in, m_conv_w, m_conv_b, m_dt_bias, m_a_log, m_d_skip, m_ssd_norm, m_q_norm, m_k_norm, m_ret_norm, m_w_out, m_ln_ffn, m_w_gate, m_w_up, m_w_down, v_ln_mix, v_w_in, v_conv_w, v_conv_b, v_dt_bias, v_a_log, v_d_skip, v_ssd_norm, v_q_norm, v_k_norm, v_ret_norm, v_w_out, v_ln_ffn, v_w_gate, v_w_up, v_w_down):
    w = dict(ln_mix=ln_mix, w_in=w_in, conv_w=conv_w, conv_b=conv_b, dt_bias=dt_bias, a_log=a_log, d_skip=d_skip,
             ssd_norm=ssd_norm, q_norm=q_norm, k_norm=k_norm, ret_norm=ret_norm, w_out=w_out, ln_ffn=ln_ffn,
             w_gate=w_gate, w_up=w_up, w_down=w_down)
    m = dict(ln_mix=m_ln_mix, w_in=m_w_in, conv_w=m_conv_w, conv_b=m_conv_b, dt_bias=m_dt_bias, a_log=m_a_log,
             d_skip=m_d_skip, ssd_norm=m_ssd_norm, q_norm=m_q_norm, k_norm=m_k_norm, ret_norm=m_ret_norm, w_out=m_w_out,
             ln_ffn=m_ln_ffn, w_gate=m_w_gate, w_up=m_w_up, w_down=m_w_down)
    v = dict(ln_mix=v_ln_mix, w_in=v_w_in, conv_w=v_conv_w, conv_b=v_conv_b, dt_bias=v_dt_bias, a_log=v_a_log,
             d_skip=v_d_skip, ssd_norm=v_ssd_norm, q_norm=v_q_norm, k_norm=v_k_norm, ret_norm=v_ret_norm, w_out=v_w_out,
             ln_ffn=v_ln_ffn, w_gate=v_w_gate, w_up=v_w_up, w_down=v_w_down)
    me = 4 * lax.axis_index("x") + 2 * lax.axis_index("y") + lax.axis_index("c")

    late_names = ("w_out", "w_gate", "w_up", "w_down")
    waves = {"a": [("w_in", 0), ("conv_w", 0), ("conv_w", 1)], "b": [(k, 0) for k in late_names],
             "c": [("w_in", 1)], "d": [(k, 1) for k in late_names]}
    gather = {}
    behind = 0.0
    for tag, items in waves.items():
        srcs = [w[k][l] if k == "conv_w" else (w[k][l] + behind).astype(BF16) for k, l in items]
        start = _relay_start(srcs, f"gather_{tag}_start") if tag == "a" else _push_start(srcs, False, f"gather_{tag}_start")
        gather[tag] = start
        behind = start[-1][0, 0]
    forwarded = _relay_forward(gather["a"], gather["d"][-1], "gather_a_forward")
    started = forwarded[-1][0, 0]
    full = {}

    def arrive(tag, after):
        if tag == "a":
            lands = _relay_wait(gather["a"], forwarded, after, "gather_a_wait")
        else:
            lands = _push_wait(gather[tag], after, False, f"gather_{tag}_wait")
        for (k, l), g in zip(waves[tag], lands):
            full[k, l] = _full_weight(k, g)

    def layer_weights(l, names):
        return {k: full[k, l] for k in names}

    def small_weights(l):
        return {k: w[k][l] for k in SMALL}

    def layer0(h):
        p = small_weights(0)
        p["ln_mix"] = p["ln_mix"] + started
        return p

    def early0(hn):
        arrive("a", hn)
        return layer_weights(0, ("w_in", "conv_w"))

    def late0(y):
        arrive("b", y)
        return layer_weights(0, late_names)

    def layer1(h):
        arrive("c", h)
        return {**small_weights(1), **layer_weights(1, ("w_in", "conv_w"))}

    def late1(y):
        arrive("d", y)
        return layer_weights(1, late_names)

    groups = {"1": [(k, 1) for k in BIG], "0a": [(k, 0) for k in ("w_down", "w_gate", "w_up")],
              "0b": [(k, 0) for k in ("w_out", "w_in")]}
    scatter = {}

    def push_grads(tag, gr):
        blocks = [_shard_block(k, gr[k]) for k, _ in groups[tag]]
        scatter[tag] = _push_start(blocks, True, f"scatter_{tag}_start")
        return scatter[tag][-1]

    loss_part, gx, grads = _local_step(
        x[0], loss_target[0], [layer0, layer1], early=[early0, None], late=[late0, late1],
        on_ffn=[functools.partial(push_grads, "0a"), None],
        on_all=[functools.partial(push_grads, "0b"), functools.partial(push_grads, "1")])
    loss = lax.psum(loss_part, MESH_AXES)

    out = {}
    recv = {}
    for tag, items in groups.items():
        for item, r in zip(items, _push_wait(scatter[tag], gx, True, f"scatter_{tag}_wait")):
            recv[item] = r
    for k in BIG:
        out[k] = _adamw_layers(recv[k, 0], recv[k, 1], w[k], m[k], v[k], f"adamw_{k}")
    names = SMALL + ("conv_w",)
    sizes = [int(np.prod(grads[0][k].shape)) for k in names]
    packed = jnp.concatenate([jnp.stack([grads[l][k] for l in range(DEPTH)]).reshape(-1) for k in names])
    n_small = packed.shape[0]
    rows_small = -(-n_small // 1024) * 8
    pad = lambda t, fill: jnp.concatenate([t, jnp.full((rows_small * 128 - n_small,), fill, F32)]).reshape(rows_small, 128)
    parts = _exchange([pad(packed, 0.0)], False, "gather_small_grads")[0]
    n_rep = DEPTH * sum(sizes[:-1])
    pack_rep = lambda d, fill: pad(jnp.concatenate([d[k].reshape(-1) for k in SMALL]
                                                   + [jnp.full((n_small - n_rep,), fill, F32)]), fill)
    res = _adamw(parts, pack_rep(w, 1.0), pack_rep(m, 1.0), pack_rep(v, 1.0), "adamw_small")
    res = [t.reshape(-1) for t in res]
    off = 0
    for k, sz in zip(SMALL, sizes[:-1]):
        out[k] = [t[off:off + DEPTH * sz].reshape(w[k].shape) for t in res]
        off += DEPTH * sz
    gconv = res[0][off:off + DEPTH * sizes[-1]].reshape(DEPTH, SSD_CONV, SSD_CONV_CH)
    gconv = lax.dynamic_slice_in_dim(gconv, me * conv_w.shape[2], conv_w.shape[2], axis=2)
    flat = lambda t: t.reshape(8, -1)
    resc = _adamw(flat(gconv)[None], flat(conv_w), flat(m_conv_w), flat(v_conv_w), "adamw_conv_w")
    out["conv_w"] = [t.reshape(conv_w.shape) for t in resc]

    return (loss, gx[None], *[out[k][0] for k in ORDER], *[out[k][1] for k in ORDER],
            *[out[k][2] for k in ORDER], *[out[k][3] for k in ORDER])
```

```python
import functools
import math

import jax
import jax.numpy as jnp
import numpy as np
from jax import lax
from jax.experimental import pallas as pl
from jax.experimental.pallas import tpu as pltpu

F32 = jnp.float32
BF16 = jnp.bfloat16

N_DEV = 8
MESH_AXES = ("x", "y", "c")
D_MODEL = 2048
DEPTH = 2
EPS = 1e-6
ROPE_THETA = 10000.0
SSD_HEADS = 16
SSD_HEAD_DIM = 64
SSD_WIDTH = 1024
SSD_GROUPS = 2
SSD_STATE = 128
SSD_CONV = 4
SSD_CONV_CH = 1536
ATT_HEADS = 8
ATT_HEAD_DIM = 64
ATT_WIDTH = 512
DILATED_PAIRS = ((128, 1), (512, 4), (2048, 16))
RET_HEADS = 4
RET_QK_DIM = 64
RET_V_DIM = 128
RET_QK_WIDTH = 256
RET_V_WIDTH = 512
CHUNK = 128
MIX_WIDTH = 2048
ATT_SPAN = 2048
ATT_STRIP = ATT_SPAN + CHUNK
ATT_QB = 8
IN_WIDTH = 5648
IN_PAD = 5760
RET_COL, ATT_COL, Z_COL, XBC_COL, DT_COL = 0, 1536, 3072, 4096, 5632
ORIG_Z_XBC, ORIG_DT, ORIG_ATT, ORIG_RET = (0, 2560), (2560, 2576), (2576, 4112), (4112, 5648)
D_FF = 5632
ADAM_LR = 0.001
ADAM_B1 = 0.9
ADAM_B2 = 0.999
ADAM_EPS = 1e-08
ADAM_WD = 0.01
ADAM_STEP = 10
NEG = -1e30
VMEM_LIMIT_V7X = 60 * 1024 * 1024

NN = (((1,), (0,)), ((), ()))
NT = (((1,), (1,)), ((), ()))
TN = (((0,), (0,)), ((), ()))


def _bdot(a, b, dims):
    return lax.dot_general(a.astype(BF16), b.astype(BF16), dims, preferred_element_type=F32)


def _xdot(a, b, dims, exact_first=False, pieces=3):
    ones, x = (a, b) if exact_first else (b, a)
    ones = ones.astype(BF16)
    acc, rest = None, x
    for _ in range(pieces):
        piece = rest.astype(BF16)
        rest = rest - piece.astype(F32)
        part = lax.dot_general(*((ones, piece) if exact_first else (piece, ones)), dims, preferred_element_type=F32)
        acc = part if acc is None else acc + part
    return acc


def _params(*sem):
    return pltpu.CompilerParams(dimension_semantics=sem, vmem_limit_bytes=VMEM_LIMIT_V7X)


def _sigmoid(v):
    return 1.0 / (1.0 + jnp.exp(-v))


def _silu_grad(v, s):
    return s * (1.0 + v * (1.0 - s))


def _rmsnorm_fwd(x, g, name):
    S, D = x.shape
    tr = min(512, S)

    def body(x_ref, g_ref, o_ref):
        xv = x_ref[...]
        r = lax.rsqrt(jnp.mean(xv * xv, axis=-1, keepdims=True) + EPS)
        o_ref[...] = (xv * r * g_ref[...]).astype(o_ref.dtype)

    return pl.pallas_call(
        body, grid=(S // tr,),
        in_specs=[pl.BlockSpec((tr, D), lambda i: (i, 0)), pl.BlockSpec((1, D), lambda i: (0, 0))],
        out_specs=pl.BlockSpec((tr, D), lambda i: (i, 0)),
        out_shape=jax.ShapeDtypeStruct((S, D), BF16), name=name, compiler_params=_params("parallel"),
    )(x, g)


def _rmsnorm_bwd(x, dy, g, dres, name):
    S, D = x.shape
    tr = min(512, S)

    def body(x_ref, dy_ref, g_ref, dres_ref, dx_ref, dxb_ref, dg_ref):
        i = pl.program_id(0)
        xv = x_ref[...]
        r = lax.rsqrt(jnp.mean(xv * xv, axis=-1, keepdims=True) + EPS)
        n = xv * r
        dy = dy_ref[...]
        dn = dy * g_ref[...]
        dx = dres_ref[...] + r * (dn - n * jnp.mean(dn * n, axis=-1, keepdims=True))
        dx_ref[...] = dx
        dxb_ref[...] = dx.astype(BF16)
        part = jnp.sum(dy * n, axis=0, keepdims=True)

        @pl.when(i == 0)
        def _():
            dg_ref[...] = part

        @pl.when(i > 0)
        def _():
            dg_ref[...] += part

    row = pl.BlockSpec((tr, D), lambda i: (i, 0))
    vec = pl.BlockSpec((1, D), lambda i: (0, 0))
    return pl.pallas_call(
        body, grid=(S // tr,), in_specs=[row, row, vec, row], out_specs=[row, row, vec],
        out_shape=[jax.ShapeDtypeStruct((S, D), F32), jax.ShapeDtypeStruct((S, D), BF16), jax.ShapeDtypeStruct((1, D), F32)],
        name=name, compiler_params=_params("arbitrary"),
    )(x, dy, g, dres)


def _loss_grad(y, tgt, name):
    S, D = y.shape
    tr = min(512, S)

    def body(y_ref, t_ref, dy_ref, dyb_ref, l_ref):
        i = pl.program_id(0)
        err = y_ref[...] - t_ref[...]
        dy = err * (1.0 / D)
        dy_ref[...] = dy
        dyb_ref[...] = dy.astype(BF16)
        part = jnp.sum(jnp.sum(err * err, axis=1, keepdims=True), axis=0, keepdims=True) * (0.5 / D)

        @pl.when(i == 0)
        def _():
            l_ref[...] = jnp.zeros_like(l_ref)

        l_ref[...] += part

    row = pl.BlockSpec((tr, D), lambda i: (i, 0))
    return pl.pallas_call(
        body, grid=(S // tr,), in_specs=[row, row],
        out_specs=[row, row, pl.BlockSpec((8, 128), lambda i: (0, 0))],
        out_shape=[jax.ShapeDtypeStruct((S, D), F32), jax.ShapeDtypeStruct((S, D), BF16), jax.ShapeDtypeStruct((8, 128), F32)],
        name=name, compiler_params=_params("arbitrary"),
    )(y, tgt)


def _pick(n, cands):
    for c in cands:
        if n % c == 0:
            return c
    return n


def _mm(a, b, mode, name, out_dtype=F32, residual=None, tm=None, tn=None, tk=None, after=None):
    if mode == "nn":
        (M, K), (_, N) = a.shape, b.shape
    elif mode == "nt":
        (M, K), (N, _) = a.shape, b.shape
    else:
        (K, M), (_, N) = a.shape, b.shape
    tm = min(tm, M) if tm else _pick(M, (1024, 512, 256, 128))
    tn = min(tn, N) if tn else _pick(N, (1024, 1152, 1408, 512, 256, 128))
    tk = min(tk, K) if tk else _pick(K, (2048, 1920, 1408, 1024, 512, 256, 128))
    assert M % tm == 0 and N % tn == 0 and K % tk == 0, (name, M, N, K, tm, tn, tk)
    nk = K // tk
    a_spec = pl.BlockSpec((tk, tm), lambda i, j, k: (k, i)) if mode == "tn" else pl.BlockSpec((tm, tk), lambda i, j, k: (i, k))
    b_spec = pl.BlockSpec((tn, tk), lambda i, j, k: (j, k)) if mode == "nt" else pl.BlockSpec((tk, tn), lambda i, j, k: (k, j))
    o_spec = pl.BlockSpec((tm, tn), lambda i, j, k: (i, j))
    dims = {"nn": NN, "nt": NT, "tn": TN}[mode]
    has_res = residual is not None

    has_after = after is not None

    def body(*refs):
        a_ref, b_ref = refs[0], refs[1]
        r_ref = refs[2] if has_res else None
        o_ref = refs[2 + has_res + has_after]
        p = _bdot(a_ref[...], b_ref[...], dims)

        def finish(acc):
            if has_res:
                acc = acc + r_ref[...]
            o_ref[...] = acc.astype(o_ref.dtype)

        if nk == 1:
            finish(p)
        else:
            acc_ref = refs[-1]
            k = pl.program_id(2)

            @pl.when(k == 0)
            def _():
                acc_ref[...] = p

            @pl.when(k > 0)
            def _():
                acc_ref[...] += p

            @pl.when(k == nk - 1)
            def _():
                finish(acc_ref[...])

    ins = [a, b] + ([residual] if has_res else []) + ([after] if has_after else [])
    in_specs = [a_spec, b_spec] + ([o_spec] if has_res else []) + ([pl.BlockSpec(memory_space=pl.ANY)] if has_after else [])
    scratch = [pltpu.VMEM((tm, tn), F32)] if nk > 1 else []
    return pl.pallas_call(
        body, grid=(M // tm, N // tn, nk), in_specs=in_specs, out_specs=o_spec,
        out_shape=jax.ShapeDtypeStruct((M, N), out_dtype), scratch_shapes=scratch, name=name,
        compiler_params=_params("parallel", "parallel", "arbitrary"),
    )(*ins)


def _accumulate(acc_ref, p, k, nk, finish):
    @pl.when(k == 0)
    def _():
        acc_ref[...] = p

    @pl.when(k > 0)
    def _():
        acc_ref[...] += p

    @pl.when(k == nk - 1)
    def _():
        finish(acc_ref[...])


def _swiglu_fwd(hn, wg, wu, name):
    S, K = hn.shape
    F = wg.shape[1]
    tm = _pick(S, (1024, 512))
    tn = _pick(F, (512, 256, 128))

    def body(a_ref, wg_ref, wu_ref, g_ref, u_ref, act_ref):
        a = a_ref[...]
        g = _bdot(a, wg_ref[...], NN)
        u = _bdot(a, wu_ref[...], NN)
        g_ref[...] = g.astype(BF16)
        u_ref[...] = u.astype(BF16)
        act_ref[...] = (g * _sigmoid(g) * u).astype(BF16)

    w_spec = pl.BlockSpec((K, tn), lambda i, j: (0, j))
    o_spec = pl.BlockSpec((tm, tn), lambda i, j: (i, j))
    sh = jax.ShapeDtypeStruct((S, F), BF16)
    return pl.pallas_call(
        body, grid=(S // tm, F // tn), in_specs=[pl.BlockSpec((tm, K), lambda i, j: (i, 0)), w_spec, w_spec],
        out_specs=[o_spec, o_spec, o_spec], out_shape=[sh, sh, sh], name=name,
        compiler_params=_params("parallel", "parallel"),
    )(hn, wg, wu)


def _swiglu_bwd(dx, wd, g, u, name):
    S, K = dx.shape
    F = wd.shape[0]
    tm = _pick(S, (1024, 512))
    tn = _pick(F, (512, 256, 128))

    def body(dx_ref, wd_ref, g_ref, u_ref, dg_ref, du_ref):
        da = _bdot(dx_ref[...], wd_ref[...], NT)
        gv = g_ref[...].astype(F32)
        uv = u_ref[...].astype(F32)
        s = _sigmoid(gv)
        dg_ref[...] = (da * uv * _silu_grad(gv, s)).astype(BF16)
        du_ref[...] = (da * gv * s).astype(BF16)

    o_spec = pl.BlockSpec((tm, tn), lambda i, j: (i, j))
    sh = jax.ShapeDtypeStruct((S, F), BF16)
    return pl.pallas_call(
        body, grid=(S // tm, F // tn),
        in_specs=[pl.BlockSpec((tm, K), lambda i, j: (i, 0)), pl.BlockSpec((tn, K), lambda i, j: (j, 0)), o_spec, o_spec],
        out_specs=[o_spec, o_spec], out_shape=[sh, sh], name=name, compiler_params=_params("parallel", "parallel"),
    )(dx, wd, g, u)


def _mm_nt2(a1, b1, a2, b2, name):
    M, K = a1.shape
    N = b1.shape[0]
    tm = _pick(M, (512,))
    tn = _pick(N, (1024, 512))
    tk = _pick(K, (2816, 1024, 512, 256, 128))
    nk = K // tk

    def body(a1_ref, b1_ref, a2_ref, b2_ref, o_ref, acc_ref):
        def finish(acc):
            o_ref[...] = acc

        p = _bdot(a1_ref[...], b1_ref[...], NT) + _bdot(a2_ref[...], b2_ref[...], NT)
        _accumulate(acc_ref, p, pl.program_id(2), nk, finish)

    a_spec = pl.BlockSpec((tm, tk), lambda i, j, k: (i, k))
    b_spec = pl.BlockSpec((tn, tk), lambda i, j, k: (j, k))
    return pl.pallas_call(
        body, grid=(M // tm, N // tn, nk), in_specs=[a_spec, b_spec, a_spec, b_spec],
        out_specs=pl.BlockSpec((tm, tn), lambda i, j, k: (i, j)), out_shape=jax.ShapeDtypeStruct((M, N), F32),
        scratch_shapes=[pltpu.VMEM((tm, tn), F32)], name=name,
        compiler_params=_params("parallel", "parallel", "arbitrary"),
    )(a1, b1, a2, b2)


XBC_BLK0 = XBC_COL // 128


def _conv_fwd(proj, w, b, name):
    S = proj.shape[0]
    T = min(512, S)

    def body(x_ref, w_ref, b_ref, o_ref, xp_ref):
        xp_ref[pl.ds(0, 8), :] = jnp.zeros((8, 128), F32)
        xp_ref[pl.ds(8, S), :] = x_ref[...]
        wv = w_ref[...]
        bv = b_ref[...]

        def step(c, carry):
            base = pl.multiple_of(c * T, T)
            acc = wv[0:1] * xp_ref[pl.ds(base + 5, T), :]
            for i in range(1, SSD_CONV):
                acc = acc + wv[i:i + 1] * xp_ref[pl.ds(base + 5 + i, T), :]
            acc = bv + acc
            o_ref[pl.ds(base, T), :] = acc * _sigmoid(acc)
            return carry

        lax.fori_loop(0, S // T, step, 0)

    return pl.pallas_call(
        body, grid=(SSD_CONV_CH // 128,),
        in_specs=[pl.BlockSpec((S, 128), lambda j: (0, XBC_BLK0 + j)), pl.BlockSpec((SSD_CONV, 128), lambda j: (0, j)),
                  pl.BlockSpec((1, 128), lambda j: (0, j))],
        out_specs=pl.BlockSpec((S, 128), lambda j: (0, j)),
        out_shape=jax.ShapeDtypeStruct((S, SSD_CONV_CH), F32),
        scratch_shapes=[pltpu.VMEM((S + 8, 128), F32)], name=name, compiler_params=_params("parallel"),
    )(proj, w, b)


def _conv_bwd(dxs, dbm, dcm, proj, w, b, dproj, name):
    S = proj.shape[0]
    T = min(512, S)
    NX, NB = SSD_WIDTH // 128, SSD_GROUPS * SSD_STATE // 128

    def body(dxs_ref, dbm_ref, dcm_ref, x_ref, w_ref, b_ref, dproj_ref, dx_ref, dw_ref, db_ref, xp_ref, dcp_ref):
        j = pl.program_id(0)

        @pl.when(j < NX)
        def _():
            dcp_ref[pl.ds(0, S), :] = dxs_ref[...]

        @pl.when((j >= NX) & (j < NX + NB))
        def _():
            dcp_ref[pl.ds(0, S), :] = dbm_ref[...]

        @pl.when(j >= NX + NB)
        def _():
            dcp_ref[pl.ds(0, S), :] = dcm_ref[...]

        da_ref = dcp_ref
        xp_ref[pl.ds(0, 8), :] = jnp.zeros((8, 128), F32)
        xp_ref[pl.ds(8, S), :] = x_ref[...]
        dcp_ref[pl.ds(S, 8), :] = jnp.zeros((8, 128), F32)
        wv = w_ref[...]
        bv = b_ref[...]

        def step1(c, carry):
            base = pl.multiple_of(c * T, T)
            xs = [xp_ref[pl.ds(base + 5 + i, T), :] for i in range(SSD_CONV)]
            acc = wv[0:1] * xs[0]
            for i in range(1, SSD_CONV):
                acc = acc + wv[i:i + 1] * xs[i]
            acc = bv + acc
            s = _sigmoid(acc)
            dc = da_ref[pl.ds(base, T), :] * _silu_grad(acc, s)
            dcp_ref[pl.ds(base, T), :] = dc
            new = tuple(carry[i] + jnp.sum(xs[i] * dc, axis=0, keepdims=True) for i in range(SSD_CONV))
            return new + (carry[SSD_CONV] + jnp.sum(dc, axis=0, keepdims=True),)

        z = jnp.zeros((1, 128), F32)
        res = lax.fori_loop(0, S // T, step1, (z,) * (SSD_CONV + 1))
        for i in range(SSD_CONV):
            dw_ref[pl.ds(i, 1), :] = res[i]
        db_ref[...] = res[SSD_CONV]

        def step2(c, carry):
            base = pl.multiple_of(c * T, T)
            acc = wv[0:1] * dcp_ref[pl.ds(base + 3, T), :]
            for i in range(1, SSD_CONV):
                acc = acc + wv[i:i + 1] * dcp_ref[pl.ds(base + 3 - i, T), :]
            dx_ref[pl.ds(base, T), :] = acc.astype(dx_ref.dtype)
            return carry

        lax.fori_loop(0, S // T, step2, 0)

    clamp = lambda j, lo, n: jnp.clip(j - lo, 0, n - 1)
    return pl.pallas_call(
        body, grid=(SSD_CONV_CH // 128,),
        in_specs=[pl.BlockSpec((S, 128), lambda j: (0, clamp(j, 0, NX))), pl.BlockSpec((S, 128), lambda j: (0, clamp(j, NX, NB))),
                  pl.BlockSpec((S, 128), lambda j: (0, clamp(j, NX + NB, NB))),
                  pl.BlockSpec((S, 128), lambda j: (0, XBC_BLK0 + j)), pl.BlockSpec((SSD_CONV, 128), lambda j: (0, j)),
                  pl.BlockSpec((1, 128), lambda j: (0, j)), pl.BlockSpec(memory_space=pl.ANY)],
        out_specs=[pl.BlockSpec((S, 128), lambda j: (0, XBC_BLK0 + j)), pl.BlockSpec((SSD_CONV, 128), lambda j: (0, j)),
                   pl.BlockSpec((1, 128), lambda j: (0, j))],
        out_shape=[jax.ShapeDtypeStruct(dproj.shape, dproj.dtype), jax.ShapeDtypeStruct((SSD_CONV, SSD_CONV_CH), F32),
                   jax.ShapeDtypeStruct((1, SSD_CONV_CH), F32)],
        input_output_aliases={6: 0},
        scratch_shapes=[pltpu.VMEM((S + 8, 128), F32), pltpu.VMEM((S + 8, 128), F32)], name=name,
        compiler_params=_params("arbitrary"),
    )(dxs, dbm, dcm, proj, w, b, dproj)


HPG = SSD_HEADS // SSD_GROUPS
GW = HPG * SSD_HEAD_DIM


def _ssd_chunk_terms(dtr, bias, alog, tril, triu):
    pre = dtr + bias
    dt = jnp.maximum(pre, 0.0) + jnp.log(1.0 + jnp.exp(-jnp.abs(pre)))
    a_neg = -jnp.exp(alog)
    a = dt * a_neg
    acum = _xdot(tril, a, NN, exact_first=True)
    acum_t = _xdot(a, triu, TN)
    return pre, dt, a_neg, acum, acum_t


def _head_expanders():
    h64 = lax.broadcasted_iota(jnp.int32, (HPG, GW), 0) == lax.broadcasted_iota(jnp.int32, (HPG, GW), 1) // SSD_HEAD_DIM
    h128 = lax.broadcasted_iota(jnp.int32, (HPG, HPG * CHUNK), 0) == lax.broadcasted_iota(jnp.int32, (HPG, HPG * CHUNK), 1) // CHUNK
    return h64.astype(F32), h128.astype(F32)


def _ssd_fwd(xbc, proj, dtr, dt_bias, a_log, d_rep, gain, name):
    S = xbc.shape[0]
    L = CHUNK
    T = min(512, S)
    CPS = T // L
    NC = S // L

    def body(x_ref, b_ref, c_ref, z_ref, dtr_ref, bias_ref, alog_ref, d_ref, gain_ref, y_ref, yraw_ref, st_ref, state):
        i = pl.program_id(1)

        @pl.when(i == 0)
        def _():
            state[...] = jnp.zeros_like(state)

        row = lax.broadcasted_iota(jnp.int32, (L, L), 0)
        col = lax.broadcasted_iota(jnp.int32, (L, L), 1)
        causal = row >= col
        tril = causal.astype(F32)
        triu = (row <= col).astype(F32)
        low = col < SSD_HEAD_DIM
        e64, e128 = _head_expanders()
        for c in range(CPS):
            rows = pl.ds(c * L, L)
            xv = x_ref[rows, :]
            bm = b_ref[rows, :]
            cm = c_ref[rows, :]
            _, dt, _, acum, acum_t = _ssd_chunk_terms(dtr_ref[rows, :], bias_ref[...], alog_ref[...], tril, triu)
            ac = _xdot(acum, e64, NN)
            ac_sq = _xdot(acum, e128, NN)
            xd = xv * _xdot(dt, e64, NN, pieces=2)
            ac_last = ac[L - 1:L, :]
            sp = state[...]
            st_ref[c] = sp
            yoff = _bdot(cm, sp, NN) * jnp.exp(ac)
            state[...] = sp * jnp.exp(ac_last) + _bdot(bm, xd * jnp.exp(ac_last - ac), TN)
            gmat = _bdot(cm, bm, NT)
            for q in range(HPG // 2):
                pair = slice(q * 128, (q + 1) * 128)
                tile = xd[:, pair]
                y = yoff[:, pair]
                for j, keep in ((2 * q, low), (2 * q + 1, ~low)):
                    lam = jnp.exp(jnp.where(causal, ac_sq[:, j * L:(j + 1) * L] - acum_t[j:j + 1, :], NEG))
                    y = y + _bdot(gmat * lam, jnp.where(keep, tile, 0.0), NN)
                yraw_ref[rows, pair] = y
            zz = z_ref[rows, :]
            u = (yraw_ref[rows, :] + xv * d_ref[...]) * (zz * _sigmoid(zz))
            r = lax.rsqrt(jnp.mean(u * u, axis=-1, keepdims=True) + EPS)
            y_ref[rows, :] = (u * r * gain_ref[...]).astype(y_ref.dtype)

    vec8 = pl.BlockSpec((None, 1, HPG), lambda g, i: (g, 0, 0))
    return pl.pallas_call(
        body, grid=(SSD_GROUPS, S // T),
        in_specs=[pl.BlockSpec((T, GW), lambda g, i: (i, g)),
                  pl.BlockSpec((T, SSD_STATE), lambda g, i: (i, SSD_WIDTH // SSD_STATE + g)),
                  pl.BlockSpec((T, SSD_STATE), lambda g, i: (i, SSD_WIDTH // SSD_STATE + SSD_GROUPS + g)),
                  pl.BlockSpec((T, GW), lambda g, i: (i, Z_COL // GW + g)),
                  pl.BlockSpec((None, T, HPG), lambda g, i: (g, i, 0)),
                  vec8, vec8,
                  pl.BlockSpec((1, GW), lambda g, i: (0, g)), pl.BlockSpec((1, GW), lambda g, i: (0, g))],
        out_specs=[pl.BlockSpec((T, GW), lambda g, i: (i, g)), pl.BlockSpec((T, GW), lambda g, i: (i, g)),
                   pl.BlockSpec((CPS, None, SSD_STATE, GW), lambda g, i: (i, g, 0, 0))],
        out_shape=[jax.ShapeDtypeStruct((S, MIX_WIDTH), BF16), jax.ShapeDtypeStruct((S, SSD_WIDTH), F32),
                   jax.ShapeDtypeStruct((NC, SSD_GROUPS, SSD_STATE, GW), F32)],
        scratch_shapes=[pltpu.VMEM((SSD_STATE, GW), F32)], name=name,
        compiler_params=_params("arbitrary", "arbitrary"),
    )(xbc, xbc, xbc, proj, dtr, dt_bias, a_log, d_rep, gain)


def _ssd_bwd(dy, yraw, xbc, proj, dtr, dt_bias, a_log, d_rep, gain, states, name):
    S = xbc.shape[0]
    L = CHUNK
    T = min(512, S)
    CPS = T // L
    NI = S // T

    def body(dy_ref, yraw_ref, x_ref, b_ref, c_ref, z_ref, dtr_ref, bias_ref, alog_ref, d_ref, gain_ref, st_ref,
             dz_ref, dx_ref, db_ref, dc_ref, ddtr_ref, dbias_ref, dalog_ref, dd_ref, dgain_ref, dstate, dxd_ref):
        i = pl.program_id(1)

        @pl.when(i == 0)
        def _():
            dstate[...] = jnp.zeros_like(dstate)
            dbias_ref[...] = jnp.zeros_like(dbias_ref)
            dalog_ref[...] = jnp.zeros_like(dalog_ref)
            dd_ref[...] = jnp.zeros_like(dd_ref)
            dgain_ref[...] = jnp.zeros_like(dgain_ref)

        row = lax.broadcasted_iota(jnp.int32, (L, L), 0)
        col = lax.broadcasted_iota(jnp.int32, (L, L), 1)
        causal = row >= col
        tril = causal.astype(F32)
        triu = (row <= col).astype(F32)
        low = col < SSD_HEAD_DIM
        e64, e128 = _head_expanders()
        lane8 = lax.broadcasted_iota(jnp.int32, (1, HPG), 1)
        sub8 = lax.broadcasted_iota(jnp.int32, (HPG, 1), 0)
        eye8 = (lax.broadcasted_iota(jnp.int32, (HPG, HPG), 0) == lax.broadcasted_iota(jnp.int32, (HPG, HPG), 1)).astype(F32)
        last_row = (lax.broadcasted_iota(jnp.int32, (L, 1), 0) == L - 1).astype(F32)
        for c in reversed(range(CPS)):
            rows = pl.ds(c * L, L)
            xv = x_ref[rows, :]
            bm = b_ref[rows, :]
            cm = c_ref[rows, :]
            zz = z_ref[rows, :]
            dvec = d_ref[...]
            sz = _sigmoid(zz)
            silu_z = zz * sz
            v = yraw_ref[rows, :] + xv * dvec
            u = v * silu_z
            r = lax.rsqrt(jnp.mean(u * u, axis=-1, keepdims=True) + EPS)
            n = u * r
            do = dy_ref[rows, :]
            dgain_ref[...] += jnp.sum(do * n, axis=0, keepdims=True)
            dn = do * gain_ref[...]
            du = r * (dn - n * jnp.mean(dn * n, axis=-1, keepdims=True))
            dz_ref[rows, :] = (du * v * _silu_grad(zz, sz)).astype(dz_ref.dtype)
            dyv = du * silu_z
            dd_ref[...] += _xdot(jnp.sum(dyv * xv, axis=0, keepdims=True), e64, NT, pieces=2)
            pre, dt, a_neg, acum, acum_t = _ssd_chunk_terms(dtr_ref[rows, :], bias_ref[...], alog_ref[...], tril, triu)
            ac = _xdot(acum, e64, NN)
            ac_sq = _xdot(acum, e128, NN)
            dt_w = _xdot(dt, e64, NN, pieces=2)
            xd = xv * dt_w
            ac_last = ac[L - 1:L, :]
            ea = jnp.exp(ac)
            w = jnp.exp(ac_last - ac)
            ea_last = jnp.exp(ac_last)
            sp = st_ref[c]
            ds = dstate[...]
            dye = dyv * ea
            yoff = _bdot(cm, sp, NN) * ea
            bds = _bdot(bm, ds, NN)
            dcm = _bdot(dye, sp, NT)
            dbm = _bdot(xd * w, ds, NT)
            dstate[...] = ds * ea_last + _bdot(cm, dye, TN)
            w8 = jnp.exp(acum[L - 1:L, :] - acum)
            dw8 = _xdot(xd * bds, e64, NT, pieces=2)
            dac8 = _xdot(dyv * yoff, e64, NT, pieces=2) - dw8 * w8
            tail8 = jnp.sum(dw8 * w8, axis=0, keepdims=True) + jnp.exp(acum[L - 1:L, :]) * _xdot(
                jnp.sum(ds * sp, axis=0, keepdims=True), e64, NT, pieces=2)
            dac8 = dac8 + last_row * tail8
            gmat = _bdot(cm, bm, NT)
            dgmat = jnp.zeros((L, L), F32)
            colsum_t = jnp.zeros((HPG, L), F32)
            for q in range(HPG // 2):
                pair = slice(q * 128, (q + 1) * 128)
                xd_tile = xd[:, pair]
                dy_tile = dyv[:, pair]
                dxd_tile = bds[:, pair] * w[:, pair]
                for j, keep in ((2 * q, low), (2 * q + 1, ~low)):
                    lam = jnp.exp(jnp.where(causal, ac_sq[:, j * L:(j + 1) * L] - acum_t[j:j + 1, :], NEG))
                    mh = gmat * lam
                    dyj = jnp.where(keep, dy_tile, 0.0)
                    dxd_tile = dxd_tile + _bdot(mh, dyj, TN)
                    dm = _bdot(dyj, xd_tile, NT)
                    dgmat = dgmat + dm * lam
                    qm = dm * mh
                    dac8 = dac8 + jnp.sum(qm, axis=1, keepdims=True) * (lane8 == j).astype(F32)
                    colsum_t = colsum_t + (sub8 == j).astype(F32) * jnp.sum(qm, axis=0, keepdims=True)
                dxd_ref[:, pair] = dxd_tile
            dac8 = dac8 - _xdot(colsum_t, eye8, TN)
            dxd = dxd_ref[...]
            dx_ref[rows, :] = dxd * dt_w + dyv * dvec
            dc_ref[rows, :] = dcm + _bdot(dgmat, bm, NN)
            db_ref[rows, :] = dbm + _bdot(dgmat, cm, TN)
            da8 = _xdot(triu, dac8, NN, exact_first=True)
            ddt8 = _xdot(dxd * xv, e64, NT, pieces=2) + da8 * a_neg
            dalog_ref[...] += jnp.sum(da8 * dt, axis=0, keepdims=True) * a_neg
            dpre = ddt8 * _sigmoid(pre)
            ddtr_ref[rows, :] = dpre
            dbias_ref[...] += jnp.sum(dpre, axis=0, keepdims=True)

    rev = lambda i: NI - 1 - i
    vec8 = pl.BlockSpec((None, 1, HPG), lambda g, i: (g, 0, 0))
    grp = pl.BlockSpec((T, GW), lambda g, i: (rev(i), g))
    bspec = pl.BlockSpec((T, SSD_STATE), lambda g, i: (rev(i), SSD_WIDTH // SSD_STATE + g))
    cspec = pl.BlockSpec((T, SSD_STATE), lambda g, i: (rev(i), SSD_WIDTH // SSD_STATE + SSD_GROUPS + g))
    gvec = pl.BlockSpec((1, GW), lambda g, i: (0, g))
    st_spec = pl.BlockSpec((CPS, None, SSD_STATE, GW), lambda g, i: (rev(i), g, 0, 0))
    small = jax.ShapeDtypeStruct((SSD_GROUPS, 1, HPG), F32)
    zspec = pl.BlockSpec((T, GW), lambda g, i: (rev(i), Z_COL // GW + g))
    return pl.pallas_call(
        body, grid=(SSD_GROUPS, NI),
        in_specs=[grp, grp, grp, bspec, cspec, zspec, pl.BlockSpec((None, T, HPG), lambda g, i: (g, rev(i), 0)),
                  vec8, vec8, gvec, gvec, st_spec],
        out_specs=[zspec, grp, pl.BlockSpec((T, SSD_STATE), lambda g, i: (rev(i), g)),
                   pl.BlockSpec((T, SSD_STATE), lambda g, i: (rev(i), g)),
                   pl.BlockSpec((None, T, HPG), lambda g, i: (g, rev(i), 0)), vec8, vec8, vec8, gvec],
        out_shape=[jax.ShapeDtypeStruct((S, IN_PAD), BF16), jax.ShapeDtypeStruct((S, SSD_WIDTH), F32),
                   jax.ShapeDtypeStruct((S, SSD_GROUPS * SSD_STATE), F32), jax.ShapeDtypeStruct((S, SSD_GROUPS * SSD_STATE), F32),
                   jax.ShapeDtypeStruct((SSD_GROUPS, S, HPG), F32), small, small, small,
                   jax.ShapeDtypeStruct((1, SSD_WIDTH), F32)],
        scratch_shapes=[pltpu.VMEM((SSD_STATE, GW), F32), pltpu.VMEM((L, GW), F32)],
        name=name, compiler_params=_params("arbitrary", "arbitrary"),
    )(dy, yraw, xbc, xbc, xbc, proj, dtr, dt_bias, a_log, d_rep, gain, states)


def _swap_halves(t):
    w = t.shape[1]
    lane = lax.broadcasted_iota(jnp.int32, t.shape, 1)
    return jnp.where((lane % 64) < 32, pltpu.roll(t, w - 32, axis=1), pltpu.roll(t, 32, axis=1))


def _widen(tab, w):
    return tab if w == 128 else jnp.concatenate([tab] * (w // 128), axis=1)


def _rope(t, cos, sin_signed):
    return t * cos + _swap_halves(t) * sin_signed


def _rope_t(d, cos, sin_signed):
    return d * cos - _swap_halves(d) * sin_signed


def _group_sum64(v, bd):
    hi = v.astype(BF16)
    lo = (v - hi.astype(F32)).astype(BF16)
    return (lax.dot_general(hi, bd, NN, preferred_element_type=F32)
            + lax.dot_general(lo, bd, NN, preferred_element_type=F32))


AQ_BLK = ATT_COL // ATT_WIDTH


def _att_prep_fwd(proj, qg, kg, cos, sin, bd, name):
    S = proj.shape[0]
    T = min(512, S)
    PB = ATT_SPAN // T
    src = lambda i: jnp.maximum(i - PB, 0)

    def body(q_ref, k_ref, v_ref, qg_ref, kg_ref, cos_ref, sin_ref, bd_ref, qo_ref, ko_ref, vo_ref):
        i = pl.program_id(0)

        @pl.when(i < PB)
        def _():
            ko_ref[...] = jnp.zeros_like(ko_ref)
            vo_ref[...] = jnp.zeros_like(vo_ref)

        @pl.when(i >= PB)
        def _():
            cw = _widen(cos_ref[...], ATT_WIDTH)
            sw = _widen(sin_ref[...], ATT_WIDTH)
            bdv = bd_ref[...]

            def norm_rope(t, gain):
                ss = _group_sum64(t * t, bdv)
                return _rope(t * lax.rsqrt(ss * (1.0 / ATT_HEAD_DIM) + EPS) * gain, cw, sw)

            qo_ref[...] = (norm_rope(q_ref[...], qg_ref[...]) * (ATT_HEAD_DIM ** -0.5)).astype(BF16)
            kt = norm_rope(k_ref[...], kg_ref[...]).astype(BF16)
            vt = v_ref[...].astype(BF16)
            for pr in range(ATT_HEADS // 2):
                ko_ref[pr] = kt[:, pr * 128:(pr + 1) * 128]
                vo_ref[pr] = vt[:, pr * 128:(pr + 1) * 128]

    vec = pl.BlockSpec((1, ATT_WIDTH), lambda i: (0, 0))
    tab = pl.BlockSpec((T, 128), lambda i: (src(i), 0))
    hm = pl.BlockSpec((ATT_HEADS // 2, T, 128), lambda i: (0, i, 0))
    hm_shape = jax.ShapeDtypeStruct((ATT_HEADS // 2, ATT_SPAN + S, 128), BF16)
    return pl.pallas_call(
        body, grid=(PB + S // T,),
        in_specs=[pl.BlockSpec((T, ATT_WIDTH), lambda i: (src(i), AQ_BLK)), pl.BlockSpec((T, ATT_WIDTH), lambda i: (src(i), AQ_BLK + 1)),
                  pl.BlockSpec((T, ATT_WIDTH), lambda i: (src(i), AQ_BLK + 2)), vec, vec, tab, tab,
                  pl.BlockSpec((ATT_WIDTH, ATT_WIDTH), lambda i: (0, 0))],
        out_specs=[pl.BlockSpec((T, ATT_WIDTH), lambda i: (src(i), 0)), hm, hm],
        out_shape=[jax.ShapeDtypeStruct((S, ATT_WIDTH), BF16), hm_shape, hm_shape],
        name=name, compiler_params=_params("arbitrary"),
    )(proj, proj, proj, qg, kg, cos, sin, bd)


def _att_prep_bwd(proj, dq, dk_p, dv_p, qg, kg, cos, sin, bd, dproj, name):
    S = proj.shape[0]
    T = min(512, S)
    NI = S // T
    PB = ATT_SPAN // T
    W = ATT_WIDTH

    def body(q_ref, k_ref, dq_ref, dkp_ref, dvp_ref, qg_ref, kg_ref, cos_ref, sin_ref, bd_ref, dproj_ref,
             do_ref, dqg_ref, dkg_ref, acc_ref):
        i = pl.program_id(0)

        @pl.when(i == 0)
        def _():
            acc_ref[...] = jnp.zeros_like(acc_ref)

        npair = ATT_HEADS // 2
        dk_all = jnp.concatenate([dkp_ref[pr].T for pr in range(npair)], axis=1)
        do_ref[:, 2 * W:3 * W] = jnp.concatenate([dvp_ref[pr].T for pr in range(npair)], axis=1).astype(BF16)
        cw = _widen(cos_ref[...], ATT_WIDTH)
        sw = _widen(sin_ref[...], ATT_WIDTH)
        bdv = bd_ref[...]

        def one(t, d_rot, gain, scale, slot):
            ss = _group_sum64(t * t, bdv)
            r = lax.rsqrt(ss * (1.0 / ATT_HEAD_DIM) + EPS)
            n = t * r
            d_ng = _rope_t(d_rot * scale, cw, sw)
            acc_ref[pl.ds(slot, 1), :] += jnp.sum(d_ng * n, axis=0, keepdims=True)
            dn = d_ng * gain
            return r * (dn - n * (_group_sum64(dn * n, bdv) * (1.0 / ATT_HEAD_DIM)))

        do_ref[:, 0:W] = one(q_ref[...], dq_ref[...], qg_ref[...], ATT_HEAD_DIM ** -0.5, 0).astype(BF16)
        do_ref[:, W:2 * W] = one(k_ref[...], dk_all, kg_ref[...], 1.0, 1).astype(BF16)

        @pl.when(i == NI - 1)
        def _():
            a = acc_ref[...]
            f = a[:, 0:64]
            for h in range(1, ATT_HEADS):
                f = f + a[:, h * 64:(h + 1) * 64]
            dqg_ref[...] = f[0:1]
            dkg_ref[...] = f[1:2]

    vec = pl.BlockSpec((1, ATT_WIDTH), lambda i: (0, 0))
    tab = pl.BlockSpec((T, 128), lambda i: (i, 0))
    row = pl.BlockSpec((T, ATT_WIDTH), lambda i: (i, 0))
    g64 = pl.BlockSpec((1, ATT_HEAD_DIM), lambda i: (0, 0))
    padded = pl.BlockSpec((ATT_HEADS // 2, 128, T), lambda i: (0, 0, i + PB))
    return pl.pallas_call(
        body, grid=(NI,),
        in_specs=[pl.BlockSpec((T, ATT_WIDTH), lambda i: (i, AQ_BLK)), pl.BlockSpec((T, ATT_WIDTH), lambda i: (i, AQ_BLK + 1)),
                  row, padded, padded, vec, vec, tab, tab, pl.BlockSpec((ATT_WIDTH, ATT_WIDTH), lambda i: (0, 0)),
                  pl.BlockSpec(memory_space=pl.ANY)],
        out_specs=[pl.BlockSpec((T, 3 * W), lambda i: (i, ATT_COL // (3 * W))), g64, g64],
        out_shape=[jax.ShapeDtypeStruct(dproj.shape, dproj.dtype), jax.ShapeDtypeStruct((1, ATT_HEAD_DIM), F32),
                   jax.ShapeDtypeStruct((1, ATT_HEAD_DIM), F32)],
        input_output_aliases={10: 0},
        scratch_shapes=[pltpu.VMEM((8, ATT_WIDTH), F32)], name=name, compiler_params=_params("arbitrary"),
    )(proj, proj, dq, dk_p, dv_p, qg, kg, cos, sin, bd, dproj)


def _att_bias():
    qpos = np.arange(CHUNK)[:, None] + ATT_SPAN
    kpos = np.arange(ATT_STRIP)[None, :]
    rel = qpos - kpos
    mult = np.zeros((CHUNK, ATT_STRIP), np.float64)
    for window, dil in DILATED_PAIRS:
        mult += (rel >= 0) & (rel % dil == 0) & (rel // dil <= window // dil)
    return np.where(mult > 0, np.log(np.maximum(mult, 1.0)), NEG).astype(np.float32)


def _att_scores(q, ks, bias, i):
    s = _bdot(q, ks, NT) + bias
    kcol = lax.broadcasted_iota(jnp.int32, (1, ATT_STRIP), 1) + i * CHUNK
    return jnp.where(kcol >= ATT_SPAN, s, NEG)


def _pair_masks():
    low = lax.broadcasted_iota(jnp.int32, (CHUNK, 128), 1) < ATT_HEAD_DIM
    return low, ~low


def _att_fwd(q, kp, vp, bias, y, name):
    S = q.shape[0]
    SP = kp.shape[1]

    QB = min(ATT_QB, S // CHUNK)
    TQ = QB * CHUNK

    def body(q_ref, k_ref, v_ref, bias_ref, y_ref, o_ref):
        i = pl.program_id(1)
        for b in range(QB):
            blk = i * QB + b
            strip = pl.ds(pl.multiple_of(blk * CHUNK, CHUNK), ATT_STRIP)
            rows = pl.ds(b * CHUNK, CHUNK)
            qv = q_ref[rows, :]
            ks = k_ref[strip, :]
            vs = v_ref[strip, :]
            outs = []
            for keep in _pair_masks():
                s = _att_scores(jnp.where(keep, qv, jnp.zeros_like(qv)), ks, bias_ref[...], blk)
                m = jnp.max(s, axis=-1, keepdims=True)
                p = jnp.exp(s - m)
                den = jnp.sum(p, axis=-1, keepdims=True)
                outs.append(_bdot(p, vs, NN) / den)
            o_ref[rows, :] = jnp.where(_pair_masks()[0], outs[0], outs[1]).astype(o_ref.dtype)

    kv = pl.BlockSpec((None, SP, 128), lambda hp, i: (hp, 0, 0))
    return pl.pallas_call(
        body, grid=(ATT_HEADS // 2, S // TQ),
        in_specs=[pl.BlockSpec((TQ, 128), lambda hp, i: (i, hp)), kv, kv,
                  pl.BlockSpec((CHUNK, ATT_STRIP), lambda hp, i: (0, 0)), pl.BlockSpec(memory_space=pl.ANY)],
        out_specs=pl.BlockSpec((TQ, 128), lambda hp, i: (i, SSD_WIDTH // 128 + hp)),
        out_shape=jax.ShapeDtypeStruct(y.shape, y.dtype), input_output_aliases={4: 0}, name=name,
        compiler_params=_params("parallel", "arbitrary"),
    )(q, kp, vp, bias, y)


def _att_bwd(q, kp, vp, bias, dy, name):
    S = q.shape[0]
    SP = kp.shape[1]
    QB = min(ATT_QB, S // CHUNK)

    def body(q_ref, k_ref, v_ref, bias_ref, do_ref, dq_ref, dk_ref, dv_ref):
        i = pl.program_id(1)

        @pl.when(i == 0)
        def _():
            dk_ref[...] = jnp.zeros_like(dk_ref)
            dv_ref[...] = jnp.zeros_like(dv_ref)

        for b in range(QB):
            blk = i * QB + b
            strip = pl.ds(pl.multiple_of(blk * CHUNK, CHUNK), ATT_STRIP)
            rows = pl.ds(b * CHUNK, CHUNK)
            qv = q_ref[rows, :]
            dov = do_ref[rows, :]
            ks = k_ref[strip, :]
            vs = v_ref[strip, :]
            dq = jnp.zeros((CHUNK, 128), F32)
            dk_t = jnp.zeros((128, ATT_STRIP), F32)
            dv_t = jnp.zeros((128, ATT_STRIP), F32)
            for keep in _pair_masks():
                qh = jnp.where(keep, qv, jnp.zeros_like(qv))
                doh = jnp.where(keep, dov, 0.0)
                s = _att_scores(qh, ks, bias_ref[...], blk)
                m = jnp.max(s, axis=-1, keepdims=True)
                p = jnp.exp(s - m)
                p = p / jnp.sum(p, axis=-1, keepdims=True)
                dp = _bdot(doh, vs, NT)
                dsc = p * (dp - jnp.sum(p * dp, axis=-1, keepdims=True))
                dq = dq + jnp.where(keep, _bdot(dsc, ks, NN), 0.0)
                dv_t = dv_t + _bdot(doh, p, TN)
                dk_t = dk_t + _bdot(qh, dsc, TN)
            dq_ref[rows, :] = dq
            dv_ref[:, strip] += dv_t
            dk_ref[:, strip] += dk_t

    TQ = QB * CHUNK
    kv = pl.BlockSpec((None, SP, 128), lambda hp, i: (hp, 0, 0))
    kv_t =pl.BlockSpec((None, 128, SP), lambda hp, i: (hp, 0, 0))
    pairs = jax.ShapeDtypeStruct((ATT_HEADS // 2, 128, SP), F32)
    return pl.pallas_call(
        body, grid=(ATT_HEADS // 2, S // TQ),
        in_specs=[pl.BlockSpec((TQ, 128), lambda hp, i: (i, hp)), kv, kv,
                  pl.BlockSpec((CHUNK, ATT_STRIP), lambda hp, i: (0, 0)),
                  pl.BlockSpec((TQ, 128), lambda hp, i: (i, SSD_WIDTH // 128 + hp))],
        out_specs=[pl.BlockSpec((TQ, 128), lambda hp, i: (i, hp)), kv_t, kv_t],
        out_shape=[jax.ShapeDtypeStruct((S, ATT_WIDTH), F32), pairs, pairs],
        name=name, compiler_params=_params("parallel", "arbitrary"),
    )(q, kp, vp, bias, dy)


RQ_BLK = RET_COL // RET_QK_WIDTH
RV_BLK = (RET_COL + 2 * RET_QK_WIDTH) // RET_V_WIDTH
RET_PAIR = 2 * RET_QK_DIM
RET_LOG_GAMMA = tuple(math.log1p(-2.0 ** (-5.0 - h)) for h in range(RET_HEADS))


def _ret_decays(h):
    L = CHUNK
    lg = RET_LOG_GAMMA[h]
    row = lax.broadcasted_iota(jnp.int32, (L, L), 0)
    col = lax.broadcasted_iota(jnp.int32, (L, L), 1)
    rel = (row - col).astype(F32)
    dm = jnp.where(rel >= 0, jnp.exp(jnp.maximum(rel, 0.0) * lg), 0.0)
    idx = lax.broadcasted_iota(jnp.int32, (L, 1), 0).astype(F32)
    kte = jnp.exp((L - 1 - idx) * lg)
    qfs = jnp.exp((idx + 1.0) * lg)
    return dm, kte, qfs, math.exp(L * lg)


def _ret_head(t, h):
    tile = t[:, (h // 2) * RET_PAIR:(h // 2 + 1) * RET_PAIR]
    low = lax.broadcasted_iota(jnp.int32, tile.shape, 1) < RET_QK_DIM
    return jnp.where(low if h % 2 == 0 else ~low, tile, 0.0)


def _ret_fwd(proj, cos, sin, gain, y, name):
    S = proj.shape[0]
    L = CHUNK
    T = min(512, S)
    CPS = T // L
    NC = S // L

    def body(q_ref, k_ref, v_ref, g_ref, cos_ref, sin_ref, gain_ref, yin_ref, y_ref, o_ref, st_ref, state):
        i = pl.program_id(0)

        @pl.when(i == 0)
        def _():
            state[...] = jnp.zeros_like(state)

        dec = [_ret_decays(h) for h in range(RET_HEADS)]
        for c in range(CPS):
            rows = pl.ds(c * L, L)
            cw = _widen(cos_ref[rows, :], RET_QK_WIDTH)
            sw = _widen(sin_ref[rows, :], RET_QK_WIDTH)
            qv = _rope(q_ref[rows, :], cw, sw)
            kv = _rope(k_ref[rows, :], cw, sw) * (RET_QK_DIM ** -0.5)
            for h in range(RET_HEADS):
                dm, kte, qfs, cd = dec[h]
                qh, kh = _ret_head(qv, h), _ret_head(kv, h)
                vs = slice(h * RET_V_DIM, (h + 1) * RET_V_DIM)
                vh = v_ref[rows, vs]
                sp = state[h]
                st_ref[c, h] = sp
                o = _bdot(_bdot(qh, kh, NT) * dm, vh, NN) + _bdot(qh * qfs, sp, NN)
                state[h] = cd * sp + _bdot(kh * kte, vh, TN)
                o_ref[rows, vs] = o
                gh = g_ref[rows, vs]
                r = lax.rsqrt(jnp.mean(o * o, axis=-1, keepdims=True) + EPS)
                y_ref[rows, vs] = (o * r * gain_ref[:, vs] * (gh * _sigmoid(gh))).astype(y_ref.dtype)

    tab = pl.BlockSpec((T, 128), lambda i: (i, 0))
    wide = pl.BlockSpec((T, RET_V_WIDTH), lambda i: (i, 0))
    return pl.pallas_call(
        body, grid=(S // T,),
        in_specs=[pl.BlockSpec((T, RET_QK_WIDTH), lambda i: (i, RQ_BLK)), pl.BlockSpec((T, RET_QK_WIDTH), lambda i: (i, RQ_BLK + 1)),
                  pl.BlockSpec((T, RET_V_WIDTH), lambda i: (i, RV_BLK)), pl.BlockSpec((T, RET_V_WIDTH), lambda i: (i, RV_BLK + 1)),
                  tab, tab, pl.BlockSpec((1, RET_V_WIDTH), lambda i: (0, 0)), pl.BlockSpec(memory_space=pl.ANY)],
        out_specs=[pl.BlockSpec((T, RET_V_WIDTH), lambda i: (i, (SSD_WIDTH + ATT_WIDTH) // RET_V_WIDTH)), wide,
                   pl.BlockSpec((CPS, RET_HEADS, RET_PAIR, RET_V_DIM), lambda i: (i, 0, 0, 0))],
        out_shape=[jax.ShapeDtypeStruct(y.shape, y.dtype), jax.ShapeDtypeStruct((S, RET_V_WIDTH), F32),
                   jax.ShapeDtypeStruct((NC, RET_HEADS, RET_PAIR, RET_V_DIM), F32)],
        input_output_aliases={7: 0},
        scratch_shapes=[pltpu.VMEM((RET_HEADS, RET_PAIR, RET_V_DIM), F32)], name=name,
        compiler_params=_params("arbitrary"),
    )(proj, proj, proj, proj, cos, sin, gain, y)


def _ret_bwd(dy, oraw, proj, cos, sin, gain, states, dproj, name):
    S = proj.shape[0]
    L = CHUNK
    T = min(512, S)
    CPS = T // L
    NI = S // T
    QW, VW = RET_QK_WIDTH, RET_V_WIDTH
    V0, G0 = 2 * QW, 2 * QW + VW

    def body(dy_ref, o_ref, q_ref, k_ref, v_ref, g_ref, cos_ref, sin_ref, gain_ref, st_ref, dproj_ref,
             out_ref, dgain_ref, dstate, dqs, dks):
        i = pl.program_id(0)

        @pl.when(i == 0)
        def _():
            dstate[...] = jnp.zeros_like(dstate)
            dgain_ref[...] = jnp.zeros_like(dgain_ref)

        dec = [_ret_decays(h) for h in range(RET_HEADS)]
        for c in reversed(range(CPS)):
            rows = pl.ds(c * L, L)
            cw = _widen(cos_ref[rows, :], RET_QK_WIDTH)
            sw = _widen(sin_ref[rows, :], RET_QK_WIDTH)
            qv = _rope(q_ref[rows, :], cw, sw)
            kv = _rope(k_ref[rows, :], cw, sw) * (RET_QK_DIM ** -0.5)
            for h in range(RET_HEADS):
                dm, kte, qfs, cd = dec[h]
                pair = slice((h // 2) * RET_PAIR, (h // 2 + 1) * RET_PAIR)
                vs = slice(h * RET_V_DIM, (h + 1) * RET_V_DIM)
                qh, kh = _ret_head(qv, h), _ret_head(kv, h)
                vh = v_ref[rows, vs]
                gh = g_ref[rows, vs]
                gn = gain_ref[:, vs]
                o = o_ref[rows, vs]
                dyh = dy_ref[rows, vs]
                sg = _sigmoid(gh)
                silu_g = gh * sg
                r = lax.rsqrt(jnp.mean(o * o, axis=-1, keepdims=True) + EPS)
                n = o * r
                dgain_ref[:, vs] += jnp.sum(dyh * n * silu_g, axis=0, keepdims=True)
                out_ref[rows, G0 + h * RET_V_DIM:G0 + (h + 1) * RET_V_DIM] = (dyh * n * gn * _silu_grad(gh, sg)).astype(out_ref.dtype)
                dn = dyh * gn * silu_g
                do = r * (dn - n * jnp.mean(dn * n, axis=-1, keepdims=True))
                sp = st_ref[c, h]
                ds = dstate[h]
                sc = _bdot(qh, kh, NT) * dm
                dsc = _bdot(do, vh, NT) * dm
                out_ref[rows, V0 + h * RET_V_DIM:V0 + (h + 1) * RET_V_DIM] = (_bdot(sc, do, TN) + _bdot(kh * kte, ds, NN)).astype(out_ref.dtype)
                dqh = _bdot(dsc, kh, NN) + _bdot(do, sp, NT) * qfs
                dkh = _bdot(dsc, qh, TN) + _bdot(vh, ds, NT) * kte
                if h % 2 == 0:
                    dqs[:, pair] = dqh
                    dks[:, pair] = dkh
                else:
                    dqs[:, pair] += dqh
                    dks[:, pair] += dkh
                dstate[h] = cd * ds + _bdot(qh * qfs, do, TN)
            out_ref[rows, 0:QW] = _rope_t(dqs[...], cw, sw).astype(out_ref.dtype)
            out_ref[rows, QW:2 * QW] = _rope_t(dks[...] * (RET_QK_DIM ** -0.5), cw, sw).astype(out_ref.dtype)

    rev = lambda i: NI - 1 - i
    tab = pl.BlockSpec((T, 128), lambda i: (rev(i), 0))
    wide = pl.BlockSpec((T, RET_V_WIDTH), lambda i: (rev(i), 0))
    group = pl.BlockSpec((T, G0 + VW), lambda i: (rev(i), RET_COL // (G0 + VW)))
    gvec = pl.BlockSpec((1, RET_V_WIDTH), lambda i: (0, 0))
    return pl.pallas_call(
        body, grid=(NI,),
        in_specs=[pl.BlockSpec((T, RET_V_WIDTH), lambda i: (rev(i), (SSD_WIDTH + ATT_WIDTH) // RET_V_WIDTH)), wide,
                  pl.BlockSpec((T, RET_QK_WIDTH), lambda i: (rev(i), RQ_BLK)), pl.BlockSpec((T, RET_QK_WIDTH), lambda i: (rev(i), RQ_BLK + 1)),
                  pl.BlockSpec((T, RET_V_WIDTH), lambda i: (rev(i), RV_BLK)), pl.BlockSpec((T, RET_V_WIDTH), lambda i: (rev(i), RV_BLK + 1)),
                  tab, tab, gvec,
                  pl.BlockSpec((CPS, RET_HEADS, RET_PAIR, RET_V_DIM), lambda i: (rev(i), 0, 0, 0)),
                  pl.BlockSpec(memory_space=pl.ANY)],
        out_specs=[group, gvec],
        out_shape=[jax.ShapeDtypeStruct(dproj.shape, dproj.dtype), jax.ShapeDtypeStruct((1, RET_V_WIDTH), F32)],
        input_output_aliases={10: 0},
        scratch_shapes=[pltpu.VMEM((RET_HEADS, RET_PAIR, RET_V_DIM), F32), pltpu.VMEM((L, RET_QK_WIDTH), F32),
                        pltpu.VMEM((L, RET_QK_WIDTH), F32)],
        name=name, compiler_params=_params("arbitrary"),
    )(dy, oraw, proj, proj, proj, proj, cos, sin, gain, states, dproj)


def _adamw_update(g_ref, nb, w_ref, m_ref, v_ref, go_ref, d_ref, mo_ref, vo_ref):
    g = g_ref[0].astype(F32)
    for k in range(1, nb):
        g = g + g_ref[k].astype(F32)
    mn = ADAM_B1 * m_ref[...] + (1.0 - ADAM_B1) * g
    vn = ADAM_B2 * v_ref[...] + (1.0 - ADAM_B2) * (g * g)
    go_ref[...] = g
    mo_ref[...] = mn
    vo_ref[...] = vn
    c1 = 1.0 - ADAM_B1 ** ADAM_STEP
    c2 = 1.0 - ADAM_B2 ** ADAM_STEP
    d_ref[...] = -ADAM_LR * ((mn / c1) / (jnp.sqrt(vn / c2) + ADAM_EPS) + ADAM_WD * w_ref[...])


def _adamw_rows(R, C):
    return _pick(R, tuple(t for t in (512, 256, 128, 64, 32, 16, 8) if t * C <= 256 * 1024))


def _adamw(gblocks, w, m, v, name):
    nb, R, C = gblocks.shape
    tr = _adamw_rows(R, C)

    def body(g_ref, *refs):
        _adamw_update(g_ref, nb, *refs)

    row = pl.BlockSpec((tr, C), lambda i: (i, 0))
    sh = jax.ShapeDtypeStruct((R, C), F32)
    return pl.pallas_call(
        body, grid=(R // tr,), in_specs=[pl.BlockSpec((nb, tr, C), lambda i: (0, i, 0)), row, row, row],
        out_specs=[row, row, row, row], out_shape=[sh, sh, sh, sh], name=name, compiler_params=_params("parallel"),
    )(gblocks, w, m, v)


def _adamw_layers(g0, g1, w, m, v, name):
    nb, R, C = g0.shape
    tr = _adamw_rows(R, C)

    def body(g0_ref, g1_ref, *refs):
        l = pl.program_id(0)

        @pl.when(l == 0)
        def _():
            _adamw_update(g0_ref, nb, *refs)

        @pl.when(l == 1)
        def _():
            _adamw_update(g1_ref, nb, *refs)

    row = pl.BlockSpec((None, tr, C), lambda l, i: (l, i, 0))
    sh = jax.ShapeDtypeStruct((DEPTH, R, C), F32)
    return pl.pallas_call(
        body, grid=(DEPTH, R // tr),
        in_specs=[pl.BlockSpec((nb, tr, C), lambda l, i: (0, i * (1 - l), 0)), pl.BlockSpec((nb, tr, C), lambda l, i: (0, i * l, 0)),
                  row, row, row],
        out_specs=[row, row, row, row], out_shape=[sh, sh, sh, sh], name=name, compiler_params=_params("arbitrary", "arbitrary"),
    )(g0, g1, w, m, v)


def _peers():
    x, y, c = lax.axis_index("x"), lax.axis_index("y"), lax.axis_index("c")
    flips = ((0, 0, 1), (1, 0, 0), (0, 1, 0), (1, 1, 0), (1, 0, 1), (0, 1, 1), (1, 1, 1))
    me = 4 * x + 2 * y + c
    peers = [(x ^ fx, y ^ fy, c ^ fc) for fx, fy, fc in flips]
    return me, peers


def _exchange(arrs, scatter, name):
    n = len(arrs)
    npeer = N_DEV - 1

    def body(*refs):
        ins, outs = refs[:n], refs[n:2 * n]
        send_sems, recv_sems, local_sems = refs[2 * n:]
        me, peers = _peers()
        copies = []
        for a in range(n):
            src_own = ins[a].at[me] if scatter else ins[a]
            own = pltpu.make_async_copy(src_own, outs[a].at[me], local_sems.at[a])
            own.start()
            copies.append(own)
            for k, peer in enumerate(peers):
                src = ins[a].at[4 * peer[0] + 2 * peer[1] + peer[2]] if scatter else ins[a]
                cp = pltpu.make_async_remote_copy(
                    src_ref=src, dst_ref=outs[a].at[me], send_sem=send_sems.at[a * npeer + k],
                    recv_sem=recv_sems.at[a * npeer + k], device_id=peer, device_id_type=pl.DeviceIdType.MESH)
                cp.start()
                copies.append(cp)
        for cp in copies:
            cp.wait()

    out_shape = [jax.ShapeDtypeStruct(((N_DEV,) + a.shape[1:]) if scatter else ((N_DEV,) + a.shape), a.dtype) for a in arrs]
    anyspec = pl.BlockSpec(memory_space=pl.ANY)
    return pl.pallas_call(
        body, in_specs=[anyspec] * n, out_specs=[anyspec] * n, out_shape=out_shape,
        scratch_shapes=[pltpu.SemaphoreType.DMA((n * npeer,)), pltpu.SemaphoreType.DMA((n * npeer,)),
                        pltpu.SemaphoreType.DMA((n,))],
        name=name,
    )(*arrs)


def _dev_index(peer):
    return 4 * peer[0] + 2 * peer[1] + peer[2]


def _push_copies(src_refs, land_refs, send_sems, recv_sems, scatter, as_receiver):
    me, peers = _peers()
    npeer = N_DEV - 1
    copies = []
    for a in range(len(src_refs)):
        for k, peer in enumerate(peers):
            src = src_refs[a].at[_dev_index(peer)] if scatter else src_refs[a]
            slot = _dev_index(peer) if as_receiver else me
            copies.append(pltpu.make_async_remote_copy(
                src_ref=src, dst_ref=land_refs[a].at[slot], send_sem=send_sems.at[a * npeer + k],
                recv_sem=recv_sems.at[a * npeer + k], device_id=peer, device_id_type=pl.DeviceIdType.MESH))
    return copies


def _own_copies(src_refs, land_refs, own_sems, scatter):
    me, _ = _peers()
    return [pltpu.make_async_copy(src_refs[a].at[me] if scatter else src_refs[a], land_refs[a].at[me], own_sems.at[a])
            for a in range(len(src_refs))]


def _push_start(srcs, scatter, name):
    n = len(srcs)
    nsem = n * (N_DEV - 1)

    def body(*refs):
        srcs_r, lands_r = refs[:n], refs[n:2 * n]
        for cp in _push_copies(srcs_r, lands_r, refs[2 * n], refs[2 * n + 1], scatter, False):
            cp.start()
        for cp in _own_copies(srcs_r, lands_r, refs[2 * n + 2], scatter):
            cp.start()
        token = refs[-1]
        token[...] = jnp.zeros_like(token)

    hbm = pl.BlockSpec(memory_space=pltpu.HBM)
    sem = pl.BlockSpec(memory_space=pltpu.SEMAPHORE)
    lands = [lax.empty((N_DEV,) + (s.shape[1:] if scatter else s.shape), s.dtype) for s in srcs]
    arrs = list(srcs) + lands
    return pl.pallas_call(
        body, name=name,
        out_shape=(pltpu.SemaphoreType.DMA((nsem,)), pltpu.SemaphoreType.DMA((nsem,)), pltpu.SemaphoreType.DMA((n,)),
                   *[pltpu.HBM(a.shape, a.dtype) for a in arrs], jax.ShapeDtypeStruct((8, 128), F32)),
        in_specs=[hbm] * (2 * n), out_specs=(sem, sem, sem, *([hbm] * (2 * n)), pl.BlockSpec(memory_space=pltpu.VMEM)),
        input_output_aliases={i: 3 + i for i in range(2 * n)},
        compiler_params=pltpu.CompilerParams(has_side_effects=pltpu.SideEffectType.DATAFLOW_SIDE_EFFECTING),
    )(*[pltpu.with_memory_space_constraint(a, pltpu.HBM) for a in arrs])


def _push_wait(handle, after, scatter, name):
    send_sems, recv_sems, own_sems, *thru, _ = handle
    n = len(thru) // 2

    def body(*refs):
        srcs_r, lands_r = refs[:n], refs[n:2 * n]
        for cp in _push_copies(srcs_r, lands_r, refs[2 * n], refs[2 * n + 1], scatter, True):
            cp.wait_send()
            cp.wait_recv()
        for cp in _own_copies(srcs_r, lands_r, refs[2 * n + 2], scatter):
            cp.wait()

    hbm = pl.BlockSpec(memory_space=pltpu.HBM)
    sem = pl.BlockSpec(memory_space=pltpu.SEMAPHORE)
    outs = pl.pallas_call(
        body, name=name, out_shape=tuple(pltpu.HBM(a.shape, a.dtype) for a in thru),
        in_specs=[hbm] * (2 * n) + [sem, sem, sem, pl.BlockSpec(memory_space=pl.ANY)], out_specs=tuple([hbm] * (2 * n)),
        input_output_aliases={i: i for i in range(2 * n)},
        compiler_params=pltpu.CompilerParams(has_side_effects=pltpu.SideEffectType.DATAFLOW_SIDE_EFFECTING),
    )(*thru, send_sems, recv_sems, own_sems, after)
    return list(outs[n:])


def _relay_copies(src_refs, land_refs, send1, recv1, send2, recv2, as_receiver):
    me, peers = _peers()
    sibling, chips = peers[0], peers[1:4]
    first, second = [], []
    for a in range(len(src_refs)):
        for k, peer in enumerate([sibling] + chips):
            slot = _dev_index(peer) if as_receiver else me
            first.append(pltpu.make_async_remote_copy(
                src_ref=src_refs[a], dst_ref=land_refs[a].at[slot], send_sem=send1.at[4 * a + k], recv_sem=recv1.at[4 * a + k],
                device_id=peer, device_id_type=pl.DeviceIdType.MESH))
        for k, chip in enumerate(chips):
            origin = _dev_index(chip)
            slot = origin ^ 1 if as_receiver else origin
            second.append(pltpu.make_async_remote_copy(
                src_ref=land_refs[a].at[origin], dst_ref=land_refs[a].at[slot], send_sem=send2.at[3 * a + k],
                recv_sem=recv2.at[3 * a + k], device_id=sibling, device_id_type=pl.DeviceIdType.MESH))
    return first, second


def _relay_start(srcs, name):
    n = len(srcs)

    def body(*refs):
        srcs_r, lands_r = refs[:n], refs[n:2 * n]
        for cp in _relay_copies(srcs_r, lands_r, refs[2 * n], refs[2 * n + 1], refs[2 * n], refs[2 * n + 1], False)[0]:
            cp.start()
        for cp in _own_copies(srcs_r, lands_r, refs[2 * n + 2], False):
            cp.start()
        token = refs[-1]
        token[...] = jnp.zeros_like(token)

    hbm = pl.BlockSpec(memory_space=pltpu.HBM)
    sem = pl.BlockSpec(memory_space=pltpu.SEMAPHORE)
    arrs = list(srcs) + [lax.empty((N_DEV,) + s.shape, s.dtype) for s in srcs]
    return pl.pallas_call(
        body, name=name,
        out_shape=(pltpu.SemaphoreType.DMA((4 * n,)), pltpu.SemaphoreType.DMA((4 * n,)), pltpu.SemaphoreType.DMA((n,)),
                   *[pltpu.HBM(a.shape, a.dtype) for a in arrs], jax.ShapeDtypeStruct((8, 128), F32)),
        in_specs=[hbm] * (2 * n), out_specs=(sem, sem, sem, *([hbm] * (2 * n)), pl.BlockSpec(memory_space=pltpu.VMEM)),
        input_output_aliases={i: 3 + i for i in range(2 * n)},
        compiler_params=pltpu.CompilerParams(has_side_effects=pltpu.SideEffectType.DATAFLOW_SIDE_EFFECTING),
    )(*[pltpu.with_memory_space_constraint(a, pltpu.HBM) for a in arrs])


def _relay_forward(handle, after, name):
    _, recv1, _, *thru, _ = handle
    n = len(thru) // 2

    def body(*refs):
        srcs_r, lands_r, recv1_r = refs[:n], refs[n:2 * n], refs[2 * n]
        send2_r, recv2_r = refs[2 * n + 2], refs[2 * n + 3]
        first, second = _relay_copies(srcs_r, lands_r, recv1_r, recv1_r, send2_r, recv2_r, True)
        for a in range(n):
            for k in range(1, 4):
                first[4 * a + k].wait_recv()
        for cp in _relay_copies(srcs_r, lands_r, recv1_r, recv1_r, send2_r, recv2_r, False)[1]:
            cp.start()
        token = refs[-1]
        token[...] = jnp.zeros_like(token)

    hbm = pl.BlockSpec(memory_space=pltpu.HBM)
    sem = pl.BlockSpec(memory_space=pltpu.SEMAPHORE)
    return pl.pallas_call(
        body, name=name,
        out_shape=(pltpu.SemaphoreType.DMA((3 * n,)), pltpu.SemaphoreType.DMA((3 * n,)),
                   *[pltpu.HBM(a.shape, a.dtype) for a in thru], jax.ShapeDtypeStruct((8, 128), F32)),
        in_specs=[hbm] * (2 * n) + [sem, pl.BlockSpec(memory_space=pl.ANY)],
        out_specs=(sem, sem, *([hbm] * (2 * n)), pl.BlockSpec(memory_space=pltpu.VMEM)),
        input_output_aliases={i: 2 + i for i in range(2 * n)},
        compiler_params=pltpu.CompilerParams(has_side_effects=pltpu.SideEffectType.DATAFLOW_SIDE_EFFECTING),
    )(*thru, recv1, after)


def _relay_wait(handle, forwarded, after, name):
    send1, recv1, own_sems, *_ = handle
    send2, recv2, *thru, _ = forwarded
    n = len(thru) // 2

    def body(*refs):
        srcs_r, lands_r = refs[:n], refs[n:2 * n]
        send1_r, recv1_r, own_r, send2_r, recv2_r = refs[2 * n:2 * n + 5]
        first, second = _relay_copies(srcs_r, lands_r, send1_r, recv1_r, send2_r, recv2_r, True)
        for i, cp in enumerate(first):
            cp.wait_send()
            if i % 4 == 0:
                cp.wait_recv()
        for cp in second:
            cp.wait_send()
            cp.wait_recv()
        for cp in _own_copies(srcs_r, lands_r, own_r, False):
            cp.wait()

    hbm = pl.BlockSpec(memory_space=pltpu.HBM)
    sem = pl.BlockSpec(memory_space=pltpu.SEMAPHORE)
    outs = pl.pallas_call(
        body, name=name, out_shape=tuple(pltpu.HBM(a.shape, a.dtype) for a in thru),
        in_specs=[hbm] * (2 * n) + [sem] * 5 + [pl.BlockSpec(memory_space=pl.ANY)], out_specs=tuple([hbm] * (2 * n)),
        input_output_aliases={i: i for i in range(2 * n)},
        compiler_params=pltpu.CompilerParams(has_side_effects=pltpu.SideEffectType.DATAFLOW_SIDE_EFFECTING),
    )(*thru, send1, recv1, own_sems, send2, recv2, after)
    return list(outs[n:])


def _tables(S):
    pos = jnp.arange(S, dtype=F32)
    inv = ROPE_THETA ** (-jnp.arange(0, ATT_HEAD_DIM, 2, dtype=F32) / ATT_HEAD_DIM)
    ang = pos[:, None] * inv[None, :]
    cos, sin = jnp.cos(ang), jnp.sin(ang)
    cos128 = jnp.tile(cos, (1, 4))
    sin128 = jnp.tile(jnp.concatenate([-sin, sin], axis=1), (1, 2))
    lane = np.arange(ATT_WIDTH)
    bd = jnp.asarray((lane[:, None] // 64 == lane[None, :] // 64).astype(np.float32), dtype=BF16)
    return cos128, sin128, bd, jnp.asarray(_att_bias())


def _layer_fwd(l, x, p, tabs, early=None, late=None):
    cos, sin, bd, bias = tabs
    S = x.shape[0]
    row = lambda v: v.reshape(1, -1)
    hn = _rmsnorm_fwd(x, row(p["ln_mix"]), f"norm_mix_fwd{l}")
    if early is not None:
        p.update(early(hn))
    proj = _mm(hn, p["w_in"], "nn", f"in_proj{l}", tn=1920)
    xbc = _conv_fwd(proj, p["conv_w"], row(p["conv_b"]), f"conv_fwd{l}")
    dtr = proj[:, DT_COL:DT_COL + SSD_HEADS].reshape(S, SSD_GROUPS, HPG).transpose(1, 0, 2)
    grp = lambda v: v.reshape(SSD_GROUPS, 1, HPG)
    d_rep = row(jnp.repeat(p["d_skip"], SSD_HEAD_DIM))
    y, yraw, ssd_st = _ssd_fwd(xbc, proj, dtr, grp(p["dt_bias"]), grp(p["a_log"]), d_rep, row(p["ssd_norm"]), f"ssd_fwd{l}")
    qg = row(jnp.tile(p["q_norm"], ATT_HEADS))
    kg = row(jnp.tile(p["k_norm"], ATT_HEADS))
    aq, akp, avp = _att_prep_fwd(proj, qg, kg, cos, sin, bd, f"att_prep_fwd{l}")
    y = _att_fwd(aq, akp, avp, bias, y, f"att_fwd{l}")
    y, oraw, ret_st = _ret_fwd(proj, cos, sin, row(p["ret_norm"]), y, f"ret_fwd{l}")
    if late is not None:
        p.update(late(y))
    x1 = _mm(y, p["w_out"], "nn", f"out_proj{l}", residual=x)
    hn2 = _rmsnorm_fwd(x1, row(p["ln_ffn"]), f"norm_ffn_fwd{l}")
    g, u, act = _swiglu_fwd(hn2, p["w_gate"], p["w_up"], f"swiglu_fwd{l}")
    x2 = _mm(act, p["w_down"], "nn", f"down_proj{l}", residual=x1, tk=2816)
    saved = dict(x=x, hn=hn, proj=proj, xbc=xbc, dtr=dtr, yraw=yraw, ssd_st=ssd_st, aq=aq, akp=akp, avp=avp,
                 oraw=oraw, ret_st=ret_st, y=y, x1=x1, hn2=hn2, g=g, u=u, act=act, d_rep=d_rep, qg=qg, kg=kg)
    return x2, saved


def _layer_bwd(l, dx2, dx2_bf, p, sv, tabs, on_ffn=None, on_all=None):
    cos, sin, bd, bias = tabs
    S = dx2.shape[0]
    row = lambda v: v.reshape(1, -1)
    grp = lambda v: v.reshape(SSD_GROUPS, 1, HPG)
    gr = {}
    dg, du = _swiglu_bwd(dx2_bf, p["w_down"], sv["g"], sv["u"], f"swiglu_bwd{l}")
    gr["w_down"] = _mm(sv["act"], dx2_bf, "tn", f"down_wgrad{l}", out_dtype=BF16, tm=1408, tn=1024, tk=2048)
    dhn2 = _mm_nt2(dg, p["w_gate"], du, p["w_up"], f"ffn_dgrad{l}")
    gr["w_gate"] = _mm(sv["hn2"], dg, "tn", f"gate_wgrad{l}", out_dtype=BF16, tm=512, tn=2816, tk=2048)
    gr["w_up"] = _mm(sv["hn2"], du, "tn", f"up_wgrad{l}", out_dtype=BF16, tm=512, tn=2816, tk=2048)
    ffn_gain = row(p["ln_ffn"]) + (on_ffn(gr)[0, 0] if on_ffn is not None else 0.0)
    dx1, dx1_bf, dln_ffn = _rmsnorm_bwd(sv["x1"], dhn2, ffn_gain, dx2, f"norm_ffn_bwd{l}")
    gr["ln_ffn"] = dln_ffn[0]
    dy = _mm(dx1_bf, p["w_out"], "nt", f"out_dgrad{l}")
    gr["w_out"] = _mm(sv["y"], dx1_bf, "tn", f"out_wgrad{l}", out_dtype=BF16, tm=1024, tn=1024, tk=2048)
    dproj, dxs, dbm, dcm, ddtr, dbias, dalog, dd, dssd_gain = _ssd_bwd(
        dy, sv["yraw"], sv["xbc"], sv["proj"], sv["dtr"], grp(p["dt_bias"]), grp(p["a_log"]), sv["d_rep"],
        row(p["ssd_norm"]), sv["ssd_st"], f"ssd_bwd{l}")
    gr["dt_bias"], gr["a_log"], gr["d_skip"] = dbias.reshape(-1), dalog.reshape(-1), dd.reshape(-1)
    gr["ssd_norm"] = dssd_gain[0]
    dproj, dconv_w, dconv_b = _conv_bwd(dxs, dbm, dcm, sv["proj"], p["conv_w"], row(p["conv_b"]), dproj, f"conv_bwd{l}")
    gr["conv_w"], gr["conv_b"] = dconv_w, dconv_b[0]
    dq, dk_p, dv_p = _att_bwd(sv["aq"], sv["akp"], sv["avp"], bias, dy, f"att_bwd{l}")
    dproj, dqg, dkg = _att_prep_bwd(sv["proj"], dq, dk_p, dv_p, sv["qg"], sv["kg"], cos, sin, bd, dproj, f"att_prep_bwd{l}")
    gr["q_norm"], gr["k_norm"] = dqg[0], dkg[0]
    dproj, dret_gain = _ret_bwd(dy, sv["oraw"], sv["proj"], cos, sin, row(p["ret_norm"]), sv["ret_st"], dproj, f"ret_bwd{l}")
    gr["ret_norm"] = dret_gain[0]
    ddt_cols = ddtr.transpose(1, 0, 2).reshape(S, SSD_HEADS).astype(BF16)
    dproj = lax.dynamic_update_slice(dproj, jnp.pad(ddt_cols, ((0, 0), (0, IN_PAD - DT_COL - SSD_HEADS))), (0, DT_COL))
    gr["w_in"] = _mm(sv["hn"], dproj, "tn", f"in_wgrad{l}", out_dtype=BF16, tm=1024, tn=1920, tk=2048)
    launched = on_all(gr) if on_all is not None else None
    dhn = _mm(dproj, p["w_in"], "nt", f"in_dgrad{l}", tk=1920, after=launched)
    dx0, dx0_bf, dln_mix = _rmsnorm_bwd(sv["x"], dhn, row(p["ln_mix"]), dx1, f"norm_mix_bwd{l}")
    gr["ln_mix"] = dln_mix[0]
    return dx0, dx0_bf, gr


def _local_step(x, tgt, layers, early=None, late=None, on_ffn=None, on_all=None):
    n = len(layers)
    none = [None] * n
    early, late, on_ffn, on_all = early or none, late or none, on_ffn or none, on_all or none
    tabs = _tables(x.shape[0])
    saved, params = [], []
    h = x
    for l in range(n):
        p = dict(layers[l](h) if callable(layers[l]) else layers[l])
        h, sv = _layer_fwd(l, h, p, tabs, early[l], late[l])
        saved.append(sv)
        params.append(p)
    dh, dh_bf, lacc = _loss_grad(h, tgt, "loss_grad")
    grads = [None] * n
    for l in reversed(range(n)):
        dh, dh_bf, grads[l] = _layer_bwd(l, dh, dh_bf, params[l], saved[l], tabs, on_ffn[l], on_all[l])
    return lacc[0, 0], dh, grads


BIG = ("w_in", "w_out", "w_gate", "w_up", "w_down")
SMALL = ("ln_mix", "conv_b", "dt_bias", "a_log", "d_skip", "ssd_norm", "q_norm", "k_norm", "ret_norm", "ln_ffn")
ORDER = ("ln_mix", "w_in", "conv_w", "conv_b", "dt_bias", "a_log", "d_skip", "ssd_norm", "q_norm", "k_norm", "ret_norm",
         "w_out", "ln_ffn", "w_gate", "w_up", "w_down")


COL_SHARDED = ("w_in", "w_gate", "w_up", "conv_w")


IN_GROUPS = ((ORIG_Z_XBC, Z_COL), (ORIG_DT, DT_COL), (ORIG_ATT, ATT_COL), (ORIG_RET, RET_COL))


def _full_weight(k, gathered):
    if k == "w_in":
        cs = gathered.shape[2]
        pieces = []
        for (lo, hi), _ in sorted(IN_GROUPS, key=lambda grp: grp[1]):
            for j in range(N_DEV):
                a, b = max(lo, j * cs), min(hi, (j + 1) * cs)
                if a < b:
                    pieces.append(gathered[j][:, a - j * cs:b - j * cs])
        pieces.append(jnp.zeros((gathered.shape[1], IN_PAD - IN_WIDTH), gathered.dtype))
        return jnp.concatenate(pieces, axis=1)
    if k in COL_SHARDED:
        return gathered.transpose(1, 0, 2).reshape(gathered.shape[1], -1)
    return gathered.reshape(-1, gathered.shape[2])


def _shard_block(k, g):
    if k == "w_in":
        cs = IN_WIDTH // N_DEV
        blocks = []
        for j in range(N_DEV):
            pieces = []
            for (lo, hi), col in IN_GROUPS:
                a, b = max(lo, j * cs), min(hi, (j + 1) * cs)
                if a < b:
                    pieces.append(g[:, col + a - lo:col + b - lo])
            blocks.append(jnp.concatenate(pieces, axis=1))
        return jnp.stack(blocks)
    if k in COL_SHARDED:
        return g.reshape(g.shape[0], N_DEV, -1).transpose(1, 0, 2)
    return g.reshape(N_DEV, -1, g.shape[1])


def kernel(x, ln_mix, w_in, conv_w, conv_b, dt_bias, a_log, d_skip, ssd_norm, q_norm, k_norm, ret_norm, w_out, ln_ffn, w_gate, w_up, w_down, loss_target, m_ln_mix, m_w_in, m_conv_w, m_conv_b, m_dt_bias, m_a_log, m_d_skip, m_ssd_norm, m_q_norm, m_k_norm, m_ret_norm, m_w_out, m_ln_ffn, m_w_gate, m_w_up, m_w_down, v_ln_mix, v_w_in, v_conv_w, v_conv_b, v_dt_bias, v_a_log, v_d_skip, v_ssd_norm, v_q_norm, v_k_norm, v_ret_norm, v_w_out, v_ln_ffn, v_w_gate, v_w_up, v_w_down):
    w = dict(ln_mix=ln_mix, w_in=w_in, conv_w=conv_w, conv_b=conv_b, dt_bias=dt_bias, a_log=a_log, d_skip=d_skip,
             ssd_norm=ssd_norm, q_norm=q_norm, k_norm=k_norm, ret_norm=ret_norm, w_out=w_out, ln_ffn=ln_ffn,
             w_gate=w_gate, w_up=w_up, w_down=w_down)
    m = dict(ln_mix=m_ln_mix, w_in=m_w_in, conv_w=m_conv_w, conv_b=m_conv_b, dt_bias=m_dt_bias, a_log=m_a_log,
             d_skip=m_d_skip, ssd_norm=m_ssd_norm, q_norm=m_q_norm, k_norm=m_k_norm, ret_norm=m_ret_norm, w_out=m_w_out,
             ln_ffn=m_ln_ffn, w_gate=m_w_gate, w_up=m_w_up, w_down=m_w_down)
    v = dict(ln_mix=v_ln_mix, w_in=v_w_in, conv_w=v_conv_w, conv_b=v_conv_b, dt_bias=v_dt_bias, a_log=v_a_log,
             d_skip=v_d_skip, ssd_norm=v_ssd_norm, q_norm=v_q_norm, k_norm=v_k_norm, ret_norm=v_ret_norm, w_out=v_w_out,
             ln_ffn=v_ln_ffn, w_gate=v_w_gate, w_up=v_w_up, w_down=v_w_down)
    me = 4 * lax.axis_index("x") + 2 * lax.axis_index("y") + lax.axis_index("c")

    late_names = ("w_out", "w_gate", "w_up", "w_down")
    waves = {"a": [("w_in", 0), ("conv_w", 0), ("conv_w", 1)], "b": [(k, 0) for k in late_names],
             "c": [("w_in", 1)], "d": [(k, 1) for k in late_names]}
    gather = {}
    behind = 0.0
    for tag, items in waves.items():
        srcs = [w[k][l] if k == "conv_w" else (w[k][l] + behind).astype(BF16) for k, l in items]
        start = _relay_start(srcs, f"gather_{tag}_start") if tag == "a" else _push_start(srcs, False, f"gather_{tag}_start")
        gather[tag] = start
        behind = start[-1][0, 0]
    forwarded = _relay_forward(gather["a"], gather["d"][-1], "gather_a_forward")
    started = forwarded[-1][0, 0]
    full = {}

    def arrive(tag, after):
        if tag == "a":
            lands = _relay_wait(gather["a"], forwarded, after, "gather_a_wait")
        else:
            lands = _push_wait(gather[tag], after, False, f"gather_{tag}_wait")
        for (k, l), g in zip(waves[tag], lands):
            full[k, l] = _full_weight(k, g)

    def layer_weights(l, names):
        return {k: full[k, l] for k in names}

    def small_weights(l):
        return {k: w[k][l] for k in SMALL}

    def layer0(h):
        p = small_weights(0)
        p["ln_mix"] = p["ln_mix"] + started
        return p

    def early0(hn):
        arrive("a", hn)
        return layer_weights(0, ("w_in", "conv_w"))

    def late0(y):
        arrive("b", y)
        return layer_weights(0, late_names)

    def layer1(h):
        arrive("c", h)
        return {**small_weights(1), **layer_weights(1, ("w_in", "conv_w"))}

    def late1(y):
        arrive("d", y)
        return layer_weights(1, late_names)

    groups = {"1": [(k, 1) for k in BIG], "0a": [(k, 0) for k in ("w_down", "w_gate", "w_up")],
              "0b": [(k, 0) for k in ("w_out", "w_in")]}
    scatter = {}

    def push_grads(tag, gr):
        blocks = [_shard_block(k, gr[k]) for k, _ in groups[tag]]
        scatter[tag] = _push_start(blocks, True, f"scatter_{tag}_start")
        return scatter[tag][-1]

    loss_part, gx, grads = _local_step(
        x[0], loss_target[0], [layer0, layer1], early=[early0, None], late=[late0, late1],
        on_ffn=[functools.partial(push_grads, "0a"), None],
        on_all=[functools.partial(push_grads, "0b"), functools.partial(push_grads, "1")])
    loss = lax.psum(loss_part, MESH_AXES)

    out = {}
    recv = {}
    for tag, items in groups.items():
        for item, r in zip(items, _push_wait(scatter[tag], gx, True, f"scatter_{tag}_wait")):
            recv[item] = r
    for k in BIG:
        out[k] = _adamw_layers(recv[k, 0], recv[k, 1], w[k], m[k], v[k], f"adamw_{k}")
    names = SMALL + ("conv_w",)
    sizes = [int(np.prod(grads[0][k].shape)) for k in names]
    packed = jnp.concatenate([jnp.stack([grads[l][k] for l in range(DEPTH)]).reshape(-1) for k in names])
    n_small = packed.shape[0]
    rows_small = -(-n_small // 1024) * 8
    pad = lambda t, fill: jnp.concatenate([t, jnp.full((rows_small * 128 - n_small,), fill, F32)]).reshape(rows_small, 128)
    parts = _exchange([pad(packed, 0.0)], False, "gather_small_grads")[0]
    n_rep = DEPTH * sum(sizes[:-1])
    pack_rep = lambda d, fill: pad(jnp.concatenate([d[k].reshape(-1) for k in SMALL]
                                                   + [jnp.full((n_small - n_rep,), fill, F32)]), fill)
    res = _adamw(parts, pack_rep(w, 1.0), pack_rep(m, 1.0), pack_rep(v, 1.0), "adamw_small")
    res = [t.reshape(-1) for t in res]
    off = 0
    for k, sz in zip(SMALL, sizes[:-1]):
        out[k] = [t[off:off + DEPTH * sz].reshape(w[k].shape) for t in res]
        off += DEPTH * sz
    gconv = res[0][off:off + DEPTH * sizes[-1]].reshape(DEPTH, SSD_CONV, SSD_CONV_CH)
    gconv = lax.dynamic_slice_in_dim(gconv, me * conv_w.shape[2], conv_w.shape[2], axis=2)
    flat = lambda t: t.reshape(8, -1)
    resc = _adamw(flat(gconv)[None], flat(conv_w), flat(m_conv_w), flat(v_conv_w), "adamw_conv_w")
    out["conv_w"] = [t.reshape(conv_w.shape) for t in resc]

    return (loss, gx[None], *[out[k][0] for k in ORDER], *[out[k][1] for k in ORDER],
            *[out[k][2] for k in ORDER], *[out[k][3] for k in ORDER])
```

```python
import functools
import math

import jax
import jax.numpy as jnp
import numpy as np
from jax import lax
from jax.experimental import pallas as pl
from jax.experimental.pallas import tpu as pltpu

F32 = jnp.float32
BF16 = jnp.bfloat16

N_DEV = 8
MESH_AXES = ("x", "y", "c")
D_MODEL = 2048
DEPTH = 2
EPS = 1e-6
ROPE_THETA = 10000.0
SSD_HEADS = 16
SSD_HEAD_DIM = 64
SSD_WIDTH = 1024
SSD_GROUPS = 2
SSD_STATE = 128
SSD_CONV = 4
SSD_CONV_CH = 1536
ATT_HEADS = 8
ATT_HEAD_DIM = 64
ATT_WIDTH = 512
DILATED_PAIRS = ((128, 1), (512, 4), (2048, 16))
RET_HEADS = 4
RET_QK_DIM = 64
RET_V_DIM = 128
RET_QK_WIDTH = 256
RET_V_WIDTH = 512
CHUNK = 128
MIX_WIDTH = 2048
ATT_SPAN = 2048
ATT_STRIP = ATT_SPAN + CHUNK
ATT_QB = 8
IN_WIDTH = 5648
IN_PAD = 5760
RET_COL, ATT_COL, Z_COL, XBC_COL, DT_COL = 0, 1536, 3072, 4096, 5632
ORIG_Z_XBC, ORIG_DT, ORIG_ATT, ORIG_RET = (0, 2560), (2560, 2576), (2576, 4112), (4112, 5648)
D_FF = 5632
ADAM_LR = 0.001
ADAM_B1 = 0.9
ADAM_B2 = 0.999
ADAM_EPS = 1e-08
ADAM_WD = 0.01
ADAM_STEP = 10
NEG = -1e30
VMEM_LIMIT_V7X = 60 * 1024 * 1024

NN = (((1,), (0,)), ((), ()))
NT = (((1,), (1,)), ((), ()))
TN = (((0,), (0,)), ((), ()))


def _bdot(a, b, dims):
    return lax.dot_general(a.astype(BF16), b.astype(BF16), dims, preferred_element_type=F32)


def _xdot(a, b, dims, exact_first=False, pieces=3):
    ones, x = (a, b) if exact_first else (b, a)
    ones = ones.astype(BF16)
    acc, rest = None, x
    for _ in range(pieces):
        piece = rest.astype(BF16)
        rest = rest - piece.astype(F32)
        part = lax.dot_general(*((ones, piece) if exact_first else (piece, ones)), dims, preferred_element_type=F32)
        acc = part if acc is None else acc + part
    return acc


def _params(*sem):
    return pltpu.CompilerParams(dimension_semantics=sem, vmem_limit_bytes=VMEM_LIMIT_V7X)


def _sigmoid(v):
    return 1.0 / (1.0 + jnp.exp(-v))


def _silu_grad(v, s):
    return s * (1.0 + v * (1.0 - s))


def _rmsnorm_fwd(x, g, name):
    S, D = x.shape
    tr = min(512, S)

    def body(x_ref, g_ref, o_ref):
        xv = x_ref[...]
        r = lax.rsqrt(jnp.mean(xv * xv, axis=-1, keepdims=True) + EPS)
        o_ref[...] = (xv * r * g_ref[...]).astype(o_ref.dtype)

    return pl.pallas_call(
        body, grid=(S // tr,),
        in_specs=[pl.BlockSpec((tr, D), lambda i: (i, 0)), pl.BlockSpec((1, D), lambda i: (0, 0))],
        out_specs=pl.BlockSpec((tr, D), lambda i: (i, 0)),
        out_shape=jax.ShapeDtypeStruct((S, D), BF16), name=name, compiler_params=_params("parallel"),
    )(x, g)


def _rmsnorm_bwd(x, dy, g, dres, name):
    S, D = x.shape
    tr = min(512, S)

    def body(x_ref, dy_ref, g_ref, dres_ref, dx_ref, dxb_ref, dg_ref):
        i = pl.program_id(0)
        xv = x_ref[...]
        r = lax.rsqrt(jnp.mean(xv * xv, axis=-1, keepdims=True) + EPS)
        n = xv * r
        dy = dy_ref[...]
        dn = dy * g_ref[...]
        dx = dres_ref[...] + r * (dn - n * jnp.mean(dn * n, axis=-1, keepdims=True))
        dx_ref[...] = dx
        dxb_ref[...] = dx.astype(BF16)
        part = jnp.sum(dy * n, axis=0, keepdims=True)

        @pl.when(i == 0)
        def _():
            dg_ref[...] = part

        @pl.when(i > 0)
        def _():
            dg_ref[...] += part

    row = pl.BlockSpec((tr, D), lambda i: (i, 0))
    vec = pl.BlockSpec((1, D), lambda i: (0, 0))
    return pl.pallas_call(
        body, grid=(S // tr,), in_specs=[row, row, vec, row], out_specs=[row, row, vec],
        out_shape=[jax.ShapeDtypeStruct((S, D), F32), jax.ShapeDtypeStruct((S, D), BF16), jax.ShapeDtypeStruct((1, D), F32)],
        name=name, compiler_params=_params("arbitrary"),
    )(x, dy, g, dres)


def _loss_grad(y, tgt, name):
    S, D = y.shape
    tr = min(512, S)

    def body(y_ref, t_ref, dy_ref, dyb_ref, l_ref):
        i = pl.program_id(0)
        err = y_ref[...] - t_ref[...]
        dy = err * (1.0 / D)
        dy_ref[...] = dy
        dyb_ref[...] = dy.astype(BF16)
        part = jnp.sum(jnp.sum(err * err, axis=1, keepdims=True), axis=0, keepdims=True) * (0.5 / D)

        @pl.when(i == 0)
        def _():
            l_ref[...] = jnp.zeros_like(l_ref)

        l_ref[...] += part

    row = pl.BlockSpec((tr, D), lambda i: (i, 0))
    return pl.pallas_call(
        body, grid=(S // tr,), in_specs=[row, row],
        out_specs=[row, row, pl.BlockSpec((8, 128), lambda i: (0, 0))],
        out_shape=[jax.ShapeDtypeStruct((S, D), F32), jax.ShapeDtypeStruct((S, D), BF16), jax.ShapeDtypeStruct((8, 128), F32)],
        name=name, compiler_params=_params("arbitrary"),
    )(y, tgt)


def _pick(n, cands):
    for c in cands:
        if n % c == 0:
            return c
    return n


def _mm(a, b, mode, name, out_dtype=F32, residual=None, tm=None, tn=None, tk=None, after=None):
    if mode == "nn":
        (M, K), (_, N) = a.shape, b.shape
    elif mode == "nt":
        (M, K), (N, _) = a.shape, b.shape
    else:
        (K, M), (_, N) = a.shape, b.shape
    tm = min(tm, M) if tm else _pick(M, (1024, 512, 256, 128))
    tn = min(tn, N) if tn else _pick(N, (1024, 1152, 1408, 512, 256, 128))
    tk = min(tk, K) if tk else _pick(K, (2048, 1920, 1408, 1024, 512, 256, 128))
    assert M % tm == 0 and N % tn == 0 and K % tk == 0, (name, M, N, K, tm, tn, tk)
    nk = K // tk
    a_spec = pl.BlockSpec((tk, tm), lambda i, j, k: (k, i)) if mode == "tn" else pl.BlockSpec((tm, tk), lambda i, j, k: (i, k))
    b_spec = pl.BlockSpec((tn, tk), lambda i, j, k: (j, k)) if mode == "nt" else pl.BlockSpec((tk, tn), lambda i, j, k: (k, j))
    o_spec = pl.BlockSpec((tm, tn), lambda i, j, k: (i, j))
    dims = {"nn": NN, "nt": NT, "tn": TN}[mode]
    has_res = residual is not None

    has_after = after is not None

    def body(*refs):
        a_ref, b_ref = refs[0], refs[1]
        r_ref = refs[2] if has_res else None
        o_ref = refs[2 + has_res + has_after]
        p = _bdot(a_ref[...], b_ref[...], dims)

        def finish(acc):
            if has_res:
                acc = acc + r_ref[...]
            o_ref[...] = acc.astype(o_ref.dtype)

        if nk == 1:
            finish(p)
        else:
            acc_ref = refs[-1]
            k = pl.program_id(2)

            @pl.when(k == 0)
            def _():
                acc_ref[...] = p

            @pl.when(k > 0)
            def _():
                acc_ref[...] += p

            @pl.when(k == nk - 1)
            def _():
                finish(acc_ref[...])

    ins = [a, b] + ([residual] if has_res else []) + ([after] if has_after else [])
    in_specs = [a_spec, b_spec] + ([o_spec] if has_res else []) + ([pl.BlockSpec(memory_space=pl.ANY)] if has_after else [])
    scratch = [pltpu.VMEM((tm, tn), F32)] if nk > 1 else []
    return pl.pallas_call(
        body, grid=(M // tm, N // tn, nk), in_specs=in_specs, out_specs=o_spec,
        out_shape=jax.ShapeDtypeStruct((M, N), out_dtype), scratch_shapes=scratch, name=name,
        compiler_params=_params("parallel", "parallel", "arbitrary"),
    )(*ins)


def _accumulate(acc_ref, p, k, nk, finish):
    @pl.when(k == 0)
    def _():
        acc_ref[...] = p

    @pl.when(k > 0)
    def _():
        acc_ref[...] += p

    @pl.when(k == nk - 1)
    def _():
        finish(acc_ref[...])


def _swiglu_fwd(hn, wg, wu, name):
    S, K = hn.shape
    F = wg.shape[1]
    tm = _pick(S, (1024, 512))
    tn = _pick(F, (512, 256, 128))

    def body(a_ref, wg_ref, wu_ref, g_ref, u_ref, act_ref):
        a = a_ref[...]
        g = _bdot(a, wg_ref[...], NN)
        u = _bdot(a, wu_ref[...], NN)
        g_ref[...] = g.astype(BF16)
        u_ref[...] = u.astype(BF16)
        act_ref[...] = (g * _sigmoid(g) * u).astype(BF16)

    w_spec = pl.BlockSpec((K, tn), lambda i, j: (0, j))
    o_spec = pl.BlockSpec((tm, tn), lambda i, j: (i, j))
    sh = jax.ShapeDtypeStruct((S, F), BF16)
    return pl.pallas_call(
        body, grid=(S // tm, F // tn), in_specs=[pl.BlockSpec((tm, K), lambda i, j: (i, 0)), w_spec, w_spec],
        out_specs=[o_spec, o_spec, o_spec], out_shape=[sh, sh, sh], name=name,
        compiler_params=_params("parallel", "parallel"),
    )(hn, wg, wu)


def _swiglu_bwd(dx, wd, g, u, name):
    S, K = dx.shape
    F = wd.shape[0]
    tm = _pick(S, (1024, 512))
    tn = _pick(F, (512, 256, 128))

    def body(dx_ref, wd_ref, g_ref, u_ref, dg_ref, du_ref):
        da = _bdot(dx_ref[...], wd_ref[...], NT)
        gv = g_ref[...].astype(F32)
        uv = u_ref[...].astype(F32)
        s = _sigmoid(gv)
        dg_ref[...] = (da * uv * _silu_grad(gv, s)).astype(BF16)
        du_ref[...] = (da * gv * s).astype(BF16)

    o_spec = pl.BlockSpec((tm, tn), lambda i, j: (i, j))
    sh = jax.ShapeDtypeStruct((S, F), BF16)
    return pl.pallas_call(
        body, grid=(S // tm, F // tn),
        in_specs=[pl.BlockSpec((tm, K), lambda i, j: (i, 0)), pl.BlockSpec((tn, K), lambda i, j: (j, 0)), o_spec, o_spec],
        out_specs=[o_spec, o_spec], out_shape=[sh, sh], name=name, compiler_params=_params("parallel", "parallel"),
    )(dx, wd, g, u)


def _mm_nt2(a1, b1, a2, b2, name):
    M, K = a1.shape
    N = b1.shape[0]
    tm = _pick(M, (512,))
    tn = _pick(N, (1024, 512))
    tk = _pick(K, (2816, 1024, 512, 256, 128))
    nk = K // tk

    def body(a1_ref, b1_ref, a2_ref, b2_ref, o_ref, acc_ref):
        def finish(acc):
            o_ref[...] = acc

        p = _bdot(a1_ref[...], b1_ref[...], NT) + _bdot(a2_ref[...], b2_ref[...], NT)
        _accumulate(acc_ref, p, pl.program_id(2), nk, finish)

    a_spec = pl.BlockSpec((tm, tk), lambda i, j, k: (i, k))
    b_spec = pl.BlockSpec((tn, tk), lambda i, j, k: (j, k))
    return pl.pallas_call(
        body, grid=(M // tm, N // tn, nk), in_specs=[a_spec, b_spec, a_spec, b_spec],
        out_specs=pl.BlockSpec((tm, tn), lambda i, j, k: (i, j)), out_shape=jax.ShapeDtypeStruct((M, N), F32),
        scratch_shapes=[pltpu.VMEM((tm, tn), F32)], name=name,
        compiler_params=_params("parallel", "parallel", "arbitrary"),
    )(a1, b1, a2, b2)


XBC_BLK0 = XBC_COL // 128


def _conv_fwd(proj, w, b, name):
    S = proj.shape[0]
    T = min(512, S)

    def body(x_ref, w_ref, b_ref, o_ref, xp_ref):
        xp_ref[pl.ds(0, 8), :] = jnp.zeros((8, 128), F32)
        xp_ref[pl.ds(8, S), :] = x_ref[...]
        wv = w_ref[...]
        bv = b_ref[...]

        def step(c, carry):
            base = pl.multiple_of(c * T, T)
            acc = wv[0:1] * xp_ref[pl.ds(base + 5, T), :]
            for i in range(1, SSD_CONV):
                acc = acc + wv[i:i + 1] * xp_ref[pl.ds(base + 5 + i, T), :]
            acc = bv + acc
            o_ref[pl.ds(base, T), :] = acc * _sigmoid(acc)
            return carry

        lax.fori_loop(0, S // T, step, 0)

    return pl.pallas_call(
        body, grid=(SSD_CONV_CH // 128,),
        in_specs=[pl.BlockSpec((S, 128), lambda j: (0, XBC_BLK0 + j)), pl.BlockSpec((SSD_CONV, 128), lambda j: (0, j)),
                  pl.BlockSpec((1, 128), lambda j: (0, j))],
        out_specs=pl.BlockSpec((S, 128), lambda j: (0, j)),
        out_shape=jax.ShapeDtypeStruct((S, SSD_CONV_CH), F32),
        scratch_shapes=[pltpu.VMEM((S + 8, 128), F32)], name=name, compiler_params=_params("parallel"),
    )(proj, w, b)


def _conv_bwd(dxs, dbm, dcm, proj, w, b, dproj, name):
    S = proj.shape[0]
    T = min(512, S)
    NX, NB = SSD_WIDTH // 128, SSD_GROUPS * SSD_STATE // 128

    def body(dxs_ref, dbm_ref, dcm_ref, x_ref, w_ref, b_ref, dproj_ref, dx_ref, dw_ref, db_ref, xp_ref, dcp_ref):
        j = pl.program_id(0)

        @pl.when(j < NX)
        def _():
            dcp_ref[pl.ds(0, S), :] = dxs_ref[...]

        @pl.when((j >= NX) & (j < NX + NB))
        def _():
            dcp_ref[pl.ds(0, S), :] = dbm_ref[...]

        @pl.when(j >= NX + NB)
        def _():
            dcp_ref[pl.ds(0, S), :] = dcm_ref[...]

        da_ref = dcp_ref
        xp_ref[pl.ds(0, 8), :] = jnp.zeros((8, 128), F32)
        xp_ref[pl.ds(8, S), :] = x_ref[...]
        dcp_ref[pl.ds(S, 8), :] = jnp.zeros((8, 128), F32)
        wv = w_ref[...]
        bv = b_ref[...]

        def step1(c, carry):
            base = pl.multiple_of(c * T, T)
            xs = [xp_ref[pl.ds(base + 5 + i, T), :] for i in range(SSD_CONV)]
            acc = wv[0:1] * xs[0]
            for i in range(1, SSD_CONV):
                acc = acc + wv[i:i + 1] * xs[i]
            acc = bv + acc
            s = _sigmoid(acc)
            dc = da_ref[pl.ds(base, T), :] * _silu_grad(acc, s)
            dcp_ref[pl.ds(base, T), :] = dc
            new = tuple(carry[i] + jnp.sum(xs[i] * dc, axis=0, keepdims=True) for i in range(SSD_CONV))
            return new + (carry[SSD_CONV] + jnp.sum(dc, axis=0, keepdims=True),)

        z = jnp.zeros((1, 128), F32)
        res = lax.fori_loop(0, S // T, step1, (z,) * (SSD_CONV + 1))
        for i in range(SSD_CONV):
            dw_ref[pl.ds(i, 1), :] = res[i]
        db_ref[...] = res[SSD_CONV]

        def step2(c, carry):
            base = pl.multiple_of(c * T, T)
            acc = wv[0:1] * dcp_ref[pl.ds(base + 3, T), :]
            for i in range(1, SSD_CONV):
                acc = acc + wv[i:i + 1] * dcp_ref[pl.ds(base + 3 - i, T), :]
            dx_ref[pl.ds(base, T), :] = acc.astype(dx_ref.dtype)
            return carry

        lax.fori_loop(0, S // T, step2, 0)

    clamp = lambda j, lo, n: jnp.clip(j - lo, 0, n - 1)
    return pl.pallas_call(
        body, grid=(SSD_CONV_CH // 128,),
        in_specs=[pl.BlockSpec((S, 128), lambda j: (0, clamp(j, 0, NX))), pl.BlockSpec((S, 128), lambda j: (0, clamp(j, NX, NB))),
                  pl.BlockSpec((S, 128), lambda j: (0, clamp(j, NX + NB, NB))),
                  pl.BlockSpec((S, 128), lambda j: (0, XBC_BLK0 + j)), pl.BlockSpec((SSD_CONV, 128), lambda j: (0, j)),
                  pl.BlockSpec((1, 128), lambda j: (0, j)), pl.BlockSpec(memory_space=pl.ANY)],
        out_specs=[pl.BlockSpec((S, 128), lambda j: (0, XBC_BLK0 + j)), pl.BlockSpec((SSD_CONV, 128), lambda j: (0, j)),
                   pl.BlockSpec((1, 128), lambda j: (0, j))],
        out_shape=[jax.ShapeDtypeStruct(dproj.shape, dproj.dtype), jax.ShapeDtypeStruct((SSD_CONV, SSD_CONV_CH), F32),
                   jax.ShapeDtypeStruct((1, SSD_CONV_CH), F32)],
        input_output_aliases={6: 0},
        scratch_shapes=[pltpu.VMEM((S + 8, 128), F32), pltpu.VMEM((S + 8, 128), F32)], name=name,
        compiler_params=_params("arbitrary"),
    )(dxs, dbm, dcm, proj, w, b, dproj)


HPG = SSD_HEADS // SSD_GROUPS
GW = HPG * SSD_HEAD_DIM


def _ssd_chunk_terms(dtr, bias, alog, tril, triu):
    pre = dtr + bias
    dt = jnp.maximum(pre, 0.0) + jnp.log(1.0 + jnp.exp(-jnp.abs(pre)))
    a_neg = -jnp.exp(alog)
    a = dt * a_neg
    acum = _xdot(tril, a, NN, exact_first=True)
    acum_t = _xdot(a, triu, TN)
    return pre, dt, a_neg, acum, acum_t


def _head_expanders():
    h64 = lax.broadcasted_iota(jnp.int32, (HPG, GW), 0) == lax.broadcasted_iota(jnp.int32, (HPG, GW), 1) // SSD_HEAD_DIM
    h128 = lax.broadcasted_iota(jnp.int32, (HPG, HPG * CHUNK), 0) == lax.broadcasted_iota(jnp.int32, (HPG, HPG * CHUNK), 1) // CHUNK
    return h64.astype(F32), h128.astype(F32)


def _ssd_fwd(xbc, proj, dtr, dt_bias, a_log, d_rep, gain, name):
    S = xbc.shape[0]
    L = CHUNK
    T = min(512, S)
    CPS = T // L
    NC = S // L

    def body(x_ref, b_ref, c_ref, z_ref, dtr_ref, bias_ref, alog_ref, d_ref, gain_ref, y_ref, yraw_ref, st_ref, state):
        i = pl.program_id(1)

        @pl.when(i == 0)
        def _():
            state[...] = jnp.zeros_like(state)

        row = lax.broadcasted_iota(jnp.int32, (L, L), 0)
        col = lax.broadcasted_iota(jnp.int32, (L, L), 1)
        causal = row >= col
        tril = causal.astype(F32)
        triu = (row <= col).astype(F32)
        low = col < SSD_HEAD_DIM
        e64, e128 = _head_expanders()
        for c in range(CPS):
            rows = pl.ds(c * L, L)
            xv = x_ref[rows, :]
            bm = b_ref[rows, :]
            cm = c_ref[rows, :]
            _, dt, _, acum, acum_t = _ssd_chunk_terms(dtr_ref[rows, :], bias_ref[...], alog_ref[...], tril, triu)
            ac = _xdot(acum, e64, NN)
            ac_sq = _xdot(acum, e128, NN)
            xd = xv * _xdot(dt, e64, NN, pieces=2)
            ac_last = ac[L - 1:L, :]
            sp = state[...]
            st_ref[c] = sp
            yoff = _bdot(cm, sp, NN) * jnp.exp(ac)
            state[...] = sp * jnp.exp(ac_last) + _bdot(bm, xd * jnp.exp(ac_last - ac), TN)
            gmat = _bdot(cm, bm, NT)
            for q in range(HPG // 2):
                pair = slice(q * 128, (q + 1) * 128)
                tile = xd[:, pair]
                y = yoff[:, pair]
                for j, keep in ((2 * q, low), (2 * q + 1, ~low)):
                    lam = jnp.exp(jnp.where(causal, ac_sq[:, j * L:(j + 1) * L] - acum_t[j:j + 1, :], NEG))
                    y = y + _bdot(gmat * lam, jnp.where(keep, tile, 0.0), NN)
                yraw_ref[rows, pair] = y
            zz = z_ref[rows, :]
            u = (yraw_ref[rows, :] + xv * d_ref[...]) * (zz * _sigmoid(zz))
            r = lax.rsqrt(jnp.mean(u * u, axis=-1, keepdims=True) + EPS)
            y_ref[rows, :] = (u * r * gain_ref[...]).astype(y_ref.dtype)

    vec8 = pl.BlockSpec((None, 1, HPG), lambda g, i: (g, 0, 0))
    return pl.pallas_call(
        body, grid=(SSD_GROUPS, S // T),
        in_specs=[pl.BlockSpec((T, GW), lambda g, i: (i, g)),
                  pl.BlockSpec((T, SSD_STATE), lambda g, i: (i, SSD_WIDTH // SSD_STATE + g)),
                  pl.BlockSpec((T, SSD_STATE), lambda g, i: (i, SSD_WIDTH // SSD_STATE + SSD_GROUPS + g)),
                  pl.BlockSpec((T, GW), lambda g, i: (i, Z_COL // GW + g)),
                  pl.BlockSpec((None, T, HPG), lambda g, i: (g, i, 0)),
                  vec8, vec8,
                  pl.BlockSpec((1, GW), lambda g, i: (0, g)), pl.BlockSpec((1, GW), lambda g, i: (0, g))],
        out_specs=[pl.BlockSpec((T, GW), lambda g, i: (i, g)), pl.BlockSpec((T, GW), lambda g, i: (i, g)),
                   pl.BlockSpec((CPS, None, SSD_STATE, GW), lambda g, i: (i, g, 0, 0))],
        out_shape=[jax.ShapeDtypeStruct((S, MIX_WIDTH), BF16), jax.ShapeDtypeStruct((S, SSD_WIDTH), F32),
                   jax.ShapeDtypeStruct((NC, SSD_GROUPS, SSD_STATE, GW), F32)],
        scratch_shapes=[pltpu.VMEM((SSD_STATE, GW), F32)], name=name,
        compiler_params=_params("arbitrary", "arbitrary"),
    )(xbc, xbc, xbc, proj, dtr, dt_bias, a_log, d_rep, gain)


def _ssd_bwd(dy, yraw, xbc, proj, dtr, dt_bias, a_log, d_rep, gain, states, name):
    S = xbc.shape[0]
    L = CHUNK
    T = min(512, S)
    CPS = T // L
    NI = S // T

    def body(dy_ref, yraw_ref, x_ref, b_ref, c_ref, z_ref, dtr_ref, bias_ref, alog_ref, d_ref, gain_ref, st_ref,
             dz_ref, dx_ref, db_ref, dc_ref, ddtr_ref, dbias_ref, dalog_ref, dd_ref, dgain_ref, dstate, dxd_ref):
        i = pl.program_id(1)

        @pl.when(i == 0)
        def _():
            dstate[...] = jnp.zeros_like(dstate)
            dbias_ref[...] = jnp.zeros_like(dbias_ref)
            dalog_ref[...] = jnp.zeros_like(dalog_ref)
            dd_ref[...] = jnp.zeros_like(dd_ref)
            dgain_ref[...] = jnp.zeros_like(dgain_ref)

        row = lax.broadcasted_iota(jnp.int32, (L, L), 0)
        col = lax.broadcasted_iota(jnp.int32, (L, L), 1)
        causal = row >= col
        tril = causal.astype(F32)
        triu = (row <= col).astype(F32)
        low = col < SSD_HEAD_DIM
        e64, e128 = _head_expanders()
        lane8 = lax.broadcasted_iota(jnp.int32, (1, HPG), 1)
        sub8 = lax.broadcasted_iota(jnp.int32, (HPG, 1), 0)
        eye8 = (lax.broadcasted_iota(jnp.int32, (HPG, HPG), 0) == lax.broadcasted_iota(jnp.int32, (HPG, HPG), 1)).astype(F32)
        last_row = (lax.broadcasted_iota(jnp.int32, (L, 1), 0) == L - 1).astype(F32)
        for c in reversed(range(CPS)):
            rows = pl.ds(c * L, L)
            xv = x_ref[rows, :]
            bm = b_ref[rows, :]
            cm = c_ref[rows, :]
            zz = z_ref[rows, :]
            dvec = d_ref[...]
            sz = _sigmoid(zz)
            silu_z = zz * sz
            v = yraw_ref[rows, :] + xv * dvec
            u = v * silu_z
            r = lax.rsqrt(jnp.mean(u * u, axis=-1, keepdims=True) + EPS)
            n = u * r
            do = dy_ref[rows, :]
            dgain_ref[...] += jnp.sum(do * n, axis=0, keepdims=True)
            dn = do * gain_ref[...]
            du = r * (dn - n * jnp.mean(dn * n, axis=-1, keepdims=True))
            dz_ref[rows, :] = (du * v * _silu_grad(zz, sz)).astype(dz_ref.dtype)
            dyv = du * silu_z
            dd_ref[...] += _xdot(jnp.sum(dyv * xv, axis=0, keepdims=True), e64, NT, pieces=2)
            pre, dt, a_neg, acum, acum_t = _ssd_chunk_terms(dtr_ref[rows, :], bias_ref[...], alog_ref[...], tril, triu)
            ac = _xdot(acum, e64, NN)
            ac_sq = _xdot(acum, e128, NN)
            dt_w = _xdot(dt, e64, NN, pieces=2)
            xd = xv * dt_w
            ac_last = ac[L - 1:L, :]
            ea = jnp.exp(ac)
            w = jnp.exp(ac_last - ac)
            ea_last = jnp.exp(ac_last)
            sp = st_ref[c]
            ds = dstate[...]
            dye = dyv * ea
            yoff = _bdot(cm, sp, NN) * ea
            bds = _bdot(bm, ds, NN)
            dcm = _bdot(dye, sp, NT)
            dbm = _bdot(xd * w, ds, NT)
            dstate[...] = ds * ea_last + _bdot(cm, dye, TN)
            w8 = jnp.exp(acum[L - 1:L, :] - acum)
            dw8 = _xdot(xd * bds, e64, NT, pieces=2)
            dac8 = _xdot(dyv * yoff, e64, NT, pieces=2) - dw8 * w8
            tail8 = jnp.sum(dw8 * w8, axis=0, keepdims=True) + jnp.exp(acum[L - 1:L, :]) * _xdot(
                jnp.sum(ds * sp, axis=0, keepdims=True), e64, NT, pieces=2)
            dac8 = dac8 + last_row * tail8
            gmat = _bdot(cm, bm, NT)
            dgmat = jnp.zeros((L, L), F32)
            colsum_t = jnp.zeros((HPG, L), F32)
            for q in range(HPG // 2):
                pair = slice(q * 128, (q + 1) * 128)
                xd_tile = xd[:, pair]
                dy_tile = dyv[:, pair]
                dxd_tile = bds[:, pair] * w[:, pair]
                for j, keep in ((2 * q, low), (2 * q + 1, ~low)):
                    lam = jnp.exp(jnp.where(causal, ac_sq[:, j * L:(j + 1) * L] - acum_t[j:j + 1, :], NEG))
                    mh = gmat * lam
                    dyj = jnp.where(keep, dy_tile, 0.0)
                    dxd_tile = dxd_tile + _bdot(mh, dyj, TN)
                    dm = _bdot(dyj, xd_tile, NT)
                    dgmat = dgmat + dm * lam
                    qm = dm * mh
                    dac8 = dac8 + jnp.sum(qm, axis=1, keepdims=True) * (lane8 == j).astype(F32)
                    colsum_t = colsum_t + (sub8 == j).astype(F32) * jnp.sum(qm, axis=0, keepdims=True)
                dxd_ref[:, pair] = dxd_tile
            dac8 = dac8 - _xdot(colsum_t, eye8, TN)
            dxd = dxd_ref[...]
            dx_ref[rows, :] = dxd * dt_w + dyv * dvec
            dc_ref[rows, :] = dcm + _bdot(dgmat, bm, NN)
            db_ref[rows, :] = dbm + _bdot(dgmat, cm, TN)
            da8 = _xdot(triu, dac8, NN, exact_first=True)
            ddt8 = _xdot(dxd * xv, e64, NT, pieces=2) + da8 * a_neg
            dalog_ref[...] += jnp.sum(da8 * dt, axis=0, keepdims=True) * a_neg
            dpre = ddt8 * _sigmoid(pre)
            ddtr_ref[rows, :] = dpre
            dbias_ref[...] += jnp.sum(dpre, axis=0, keepdims=True)

    rev = lambda i: NI - 1 - i
    vec8 = pl.BlockSpec((None, 1, HPG), lambda g, i: (g, 0, 0))
    grp = pl.BlockSpec((T, GW), lambda g, i: (rev(i), g))
    bspec = pl.BlockSpec((T, SSD_STATE), lambda g, i: (rev(i), SSD_WIDTH // SSD_STATE + g))
    cspec = pl.BlockSpec((T, SSD_STATE), lambda g, i: (rev(i), SSD_WIDTH // SSD_STATE + SSD_GROUPS + g))
    gvec = pl.BlockSpec((1, GW), lambda g, i: (0, g))
    st_spec = pl.BlockSpec((CPS, None, SSD_STATE, GW), lambda g, i: (rev(i), g, 0, 0))
    small = jax.ShapeDtypeStruct((SSD_GROUPS, 1, HPG), F32)
    zspec = pl.BlockSpec((T, GW), lambda g, i: (rev(i), Z_COL // GW + g))
    return pl.pallas_call(
        body, grid=(SSD_GROUPS, NI),
        in_specs=[grp, grp, grp, bspec, cspec, zspec, pl.BlockSpec((None, T, HPG), lambda g, i: (g, rev(i), 0)),
                  vec8, vec8, gvec, gvec, st_spec],
        out_specs=[zspec, grp, pl.BlockSpec((T, SSD_STATE), lambda g, i: (rev(i), g)),
                   pl.BlockSpec((T, SSD_STATE), lambda g, i: (rev(i), g)),
                   pl.BlockSpec((None, T, HPG), lambda g, i: (g, rev(i), 0)), vec8, vec8, vec8, gvec],
        out_shape=[jax.ShapeDtypeStruct((S, IN_PAD), BF16), jax.ShapeDtypeStruct((S, SSD_WIDTH), F32),
                   jax.ShapeDtypeStruct((S, SSD_GROUPS * SSD_STATE), F32), jax.ShapeDtypeStruct((S, SSD_GROUPS * SSD_STATE), F32),
                   jax.ShapeDtypeStruct((SSD_GROUPS, S, HPG), F32), small, small, small,
                   jax.ShapeDtypeStruct((1, SSD_WIDTH), F32)],
        scratch_shapes=[pltpu.VMEM((SSD_STATE, GW), F32), pltpu.VMEM((L, GW), F32)],
        name=name, compiler_params=_params("arbitrary", "arbitrary"),
    )(dy, yraw, xbc, xbc, xbc, proj, dtr, dt_bias, a_log, d_rep, gain, states)


def _swap_halves(t):
    w = t.shape[1]
    lane = lax.broadcasted_iota(jnp.int32, t.shape, 1)
    return jnp.where((lane % 64) < 32, pltpu.roll(t, w - 32, axis=1), pltpu.roll(t, 32, axis=1))


def _widen(tab, w):
    return tab if w == 128 else jnp.concatenate([tab] * (w // 128), axis=1)


def _rope(t, cos, sin_signed):
    return t * cos + _swap_halves(t) * sin_signed


def _rope_t(d, cos, sin_signed):
    return d * cos - _swap_halves(d) * sin_signed


def _group_sum64(v, bd):
    hi = v.astype(BF16)
    lo = (v - hi.astype(F32)).astype(BF16)
    return (lax.dot_general(hi, bd, NN, preferred_element_type=F32)
            + lax.dot_general(lo, bd, NN, preferred_element_type=F32))


AQ_BLK = ATT_COL // ATT_WIDTH


def _att_prep_fwd(proj, qg, kg, cos, sin, bd, name):
    S = proj.shape[0]
    T = min(512, S)
    PB = ATT_SPAN // T
    src = lambda i: jnp.maximum(i - PB, 0)

    def body(q_ref, k_ref, v_ref, qg_ref, kg_ref, cos_ref, sin_ref, bd_ref, qo_ref, ko_ref, vo_ref):
        i = pl.program_id(0)

        @pl.when(i < PB)
        def _():
            ko_ref[...] = jnp.zeros_like(ko_ref)
            vo_ref[...] = jnp.zeros_like(vo_ref)

        @pl.when(i >= PB)
        def _():
            cw = _widen(cos_ref[...], ATT_WIDTH)
            sw = _widen(sin_ref[...], ATT_WIDTH)
            bdv = bd_ref[...]

            def norm_rope(t, gain):
                ss = _group_sum64(t * t, bdv)
                return _rope(t * lax.rsqrt(ss * (1.0 / ATT_HEAD_DIM) + EPS) * gain, cw, sw)

            qo_ref[...] = (norm_rope(q_ref[...], qg_ref[...]) * (ATT_HEAD_DIM ** -0.5)).astype(BF16)
            kt = norm_rope(k_ref[...], kg_ref[...]).astype(BF16)
            vt = v_ref[...].astype(BF16)
            for pr in range(ATT_HEADS // 2):
                ko_ref[pr] = kt[:, pr * 128:(pr + 1) * 128]
                vo_ref[pr] = vt[:, pr * 128:(pr + 1) * 128]

    vec = pl.BlockSpec((1, ATT_WIDTH), lambda i: (0, 0))
    tab = pl.BlockSpec((T, 128), lambda i: (src(i), 0))
    hm = pl.BlockSpec((ATT_HEADS // 2, T, 128), lambda i: (0, i, 0))
    hm_shape = jax.ShapeDtypeStruct((ATT_HEADS // 2, ATT_SPAN + S, 128), BF16)
    return pl.pallas_call(
        body, grid=(PB + S // T,),
        in_specs=[pl.BlockSpec((T, ATT_WIDTH), lambda i: (src(i), AQ_BLK)), pl.BlockSpec((T, ATT_WIDTH), lambda i: (src(i), AQ_BLK + 1)),
                  pl.BlockSpec((T, ATT_WIDTH), lambda i: (src(i), AQ_BLK + 2)), vec, vec, tab, tab,
                  pl.BlockSpec((ATT_WIDTH, ATT_WIDTH), lambda i: (0, 0))],
        out_specs=[pl.BlockSpec((T, ATT_WIDTH), lambda i: (src(i), 0)), hm, hm],
        out_shape=[jax.ShapeDtypeStruct((S, ATT_WIDTH), BF16), hm_shape, hm_shape],
        name=name, compiler_params=_params("arbitrary"),
    )(proj, proj, proj, qg, kg, cos, sin, bd)


def _att_prep_bwd(proj, dq, dk_p, dv_p, qg, kg, cos, sin, bd, dproj, name):
    S = proj.shape[0]
    T = min(512, S)
    NI = S // T
    PB = ATT_SPAN // T
    W = ATT_WIDTH

    def body(q_ref, k_ref, dq_ref, dkp_ref, dvp_ref, qg_ref, kg_ref, cos_ref, sin_ref, bd_ref, dproj_ref,
             do_ref, dqg_ref, dkg_ref, acc_ref):
        i = pl.program_id(0)

        @pl.when(i == 0)
        def _():
            acc_ref[...] = jnp.zeros_like(acc_ref)

        npair = ATT_HEADS // 2
        dk_all = jnp.concatenate([dkp_ref[pr].T for pr in range(npair)], axis=1)
        do_ref[:, 2 * W:3 * W] = jnp.concatenate([dvp_ref[pr].T for pr in range(npair)], axis=1).astype(BF16)
        cw = _widen(cos_ref[...], ATT_WIDTH)
        sw = _widen(sin_ref[...], ATT_WIDTH)
        bdv = bd_ref[...]

        def one(t, d_rot, gain, scale, slot):
            ss = _group_sum64(t * t, bdv)
            r = lax.rsqrt(ss * (1.0 / ATT_HEAD_DIM) + EPS)
            n = t * r
            d_ng = _rope_t(d_rot * scale, cw, sw)
            acc_ref[pl.ds(slot, 1), :] += jnp.sum(d_ng * n, axis=0, keepdims=True)
            dn = d_ng * gain
            return r * (dn - n * (_group_sum64(dn * n, bdv) * (1.0 / ATT_HEAD_DIM)))

        do_ref[:, 0:W] = one(q_ref[...], dq_ref[...], qg_ref[...], ATT_HEAD_DIM ** -0.5, 0).astype(BF16)
        do_ref[:, W:2 * W] = one(k_ref[...], dk_all, kg_ref[...], 1.0, 1).astype(BF16)

        @pl.when(i == NI - 1)
        def _():
            a = acc_ref[...]
            f = a[:, 0:64]
            for h in range(1, ATT_HEADS):
                f = f + a[:, h * 64:(h + 1) * 64]
            dqg_ref[...] = f[0:1]
            dkg_ref[...] = f[1:2]

    vec = pl.BlockSpec((1, ATT_WIDTH), lambda i: (0, 0))
    tab = pl.BlockSpec((T, 128), lambda i: (i, 0))
    row = pl.BlockSpec((T, ATT_WIDTH), lambda i: (i, 0))
    g64 = pl.BlockSpec((1, ATT_HEAD_DIM), lambda i: (0, 0))
    padded = pl.BlockSpec((ATT_HEADS // 2, 128, T), lambda i: (0, 0, i + PB))
    return pl.pallas_call(
        body, grid=(NI,),
        in_specs=[pl.BlockSpec((T, ATT_WIDTH), lambda i: (i, AQ_BLK)), pl.BlockSpec((T, ATT_WIDTH), lambda i: (i, AQ_BLK + 1)),
                  row, padded, padded, vec, vec, tab, tab, pl.BlockSpec((ATT_WIDTH, ATT_WIDTH), lambda i: (0, 0)),
                  pl.BlockSpec(memory_space=pl.ANY)],
        out_specs=[pl.BlockSpec((T, 3 * W), lambda i: (i, ATT_COL // (3 * W))), g64, g64],
        out_shape=[jax.ShapeDtypeStruct(dproj.shape, dproj.dtype), jax.ShapeDtypeStruct((1, ATT_HEAD_DIM), F32),
                   jax.ShapeDtypeStruct((1, ATT_HEAD_DIM), F32)],
        input_output_aliases={10: 0},
        scratch_shapes=[pltpu.VMEM((8, ATT_WIDTH), F32)], name=name, compiler_params=_params("arbitrary"),
    )(proj, proj, dq, dk_p, dv_p, qg, kg, cos, sin, bd, dproj)


def _att_bias():
    qpos = np.arange(CHUNK)[:, None] + ATT_SPAN
    kpos = np.arange(ATT_STRIP)[None, :]
    rel = qpos - kpos
    mult = np.zeros((CHUNK, ATT_STRIP), np.float64)
    for window, dil in DILATED_PAIRS:
        mult += (rel >= 0) & (rel % dil == 0) & (rel // dil <= window // dil)
    return np.where(mult > 0, np.log(np.maximum(mult, 1.0)), NEG).astype(np.float32)


def _att_scores(q, ks, bias, i):
    s = _bdot(q, ks, NT) + bias
    kcol = lax.broadcasted_iota(jnp.int32, (1, ATT_STRIP), 1) + i * CHUNK
    return jnp.where(kcol >= ATT_SPAN, s, NEG)


def _pair_masks():
    low = lax.broadcasted_iota(jnp.int32, (CHUNK, 128), 1) < ATT_HEAD_DIM
    return low, ~low


def _att_fwd(q, kp, vp, bias, y, name):
    S = q.shape[0]
    SP = kp.shape[1]

    QB = min(ATT_QB, S // CHUNK)
    TQ = QB * CHUNK

    def body(q_ref, k_ref, v_ref, bias_ref, y_ref, o_ref):
        i = pl.program_id(1)
        for b in range(QB):
            blk = i * QB + b
            strip = pl.ds(pl.multiple_of(blk * CHUNK, CHUNK), ATT_STRIP)
            rows = pl.ds(b * CHUNK, CHUNK)
            qv = q_ref[rows, :]
            ks = k_ref[strip, :]
            vs = v_ref[strip, :]
            outs = []
            for keep in _pair_masks():
                s = _att_scores(jnp.where(keep, qv, jnp.zeros_like(qv)), ks, bias_ref[...], blk)
                m = jnp.max(s, axis=-1, keepdims=True)
                p = jnp.exp(s - m)
                den = jnp.sum(p, axis=-1, keepdims=True)
                outs.append(_bdot(p, vs, NN) / den)
            o_ref[rows, :] = jnp.where(_pair_masks()[0], outs[0], outs[1]).astype(o_ref.dtype)

    kv = pl.BlockSpec((None, SP, 128), lambda hp, i: (hp, 0, 0))
    return pl.pallas_call(
        body, grid=(ATT_HEADS // 2, S // TQ),
        in_specs=[pl.BlockSpec((TQ, 128), lambda hp, i: (i, hp)), kv, kv,
                  pl.BlockSpec((CHUNK, ATT_STRIP), lambda hp, i: (0, 0)), pl.BlockSpec(memory_space=pl.ANY)],
        out_specs=pl.BlockSpec((TQ, 128), lambda hp, i: (i, SSD_WIDTH // 128 + hp)),
        out_shape=jax.ShapeDtypeStruct(y.shape, y.dtype), input_output_aliases={4: 0}, name=name,
        compiler_params=_params("parallel", "arbitrary"),
    )(q, kp, vp, bias, y)


def _att_bwd(q, kp, vp, bias, dy, name):
    S = q.shape[0]
    SP = kp.shape[1]
    QB = min(ATT_QB, S // CHUNK)

    def body(q_ref, k_ref, v_ref, bias_ref, do_ref, dq_ref, dk_ref, dv_ref):
        i = pl.program_id(1)

        @pl.when(i == 0)
        def _():
            dk_ref[...] = jnp.zeros_like(dk_ref)
            dv_ref[...] = jnp.zeros_like(dv_ref)

        for b in range(QB):
            blk = i * QB + b
            strip = pl.ds(pl.multiple_of(blk * CHUNK, CHUNK), ATT_STRIP)
            rows = pl.ds(b * CHUNK, CHUNK)
            qv = q_ref[rows, :]
            dov = do_ref[rows, :]
            ks = k_ref[strip, :]
            vs = v_ref[strip, :]
            dq = jnp.zeros((CHUNK, 128), F32)
            dk_t = jnp.zeros((128, ATT_STRIP), F32)
            dv_t = jnp.zeros((128, ATT_STRIP), F32)
            for keep in _pair_masks():
                qh = jnp.where(keep, qv, jnp.zeros_like(qv))
                doh = jnp.where(keep, dov, 0.0)
                s = _att_scores(qh, ks, bias_ref[...], blk)
                m = jnp.max(s, axis=-1, keepdims=True)
                p = jnp.exp(s - m)
                p = p / jnp.sum(p, axis=-1, keepdims=True)
                dp = _bdot(doh, vs, NT)
                dsc = p * (dp - jnp.sum(p * dp, axis=-1, keepdims=True))
                dq = dq + jnp.where(keep, _bdot(dsc, ks, NN), 0.0)
                dv_t = dv_t + _bdot(doh, p, TN)
                dk_t = dk_t + _bdot(qh, dsc, TN)
            dq_ref[rows, :] = dq
            dv_ref[:, strip] += dv_t
            dk_ref[:, strip] += dk_t

    TQ = QB * CHUNK
    kv = pl.BlockSpec((None, SP, 128), lambda hp, i: (hp, 0, 0))
    kv_t =pl.BlockSpec((None, 128, SP), lambda hp, i: (hp, 0, 0))
    pairs = jax.ShapeDtypeStruct((ATT_HEADS // 2, 128, SP), F32)
    return pl.pallas_call(
        body, grid=(ATT_HEADS // 2, S // TQ),
        in_specs=[pl.BlockSpec((TQ, 128), lambda hp, i: (i, hp)), kv, kv,
                  pl.BlockSpec((CHUNK, ATT_STRIP), lambda hp, i: (0, 0)),
                  pl.BlockSpec((TQ, 128), lambda hp, i: (i, SSD_WIDTH // 128 + hp))],
        out_specs=[pl.BlockSpec((TQ, 128), lambda hp, i: (i, hp)), kv_t, kv_t],
        out_shape=[jax.ShapeDtypeStruct((S, ATT_WIDTH), F32), pairs, pairs],
        name=name, compiler_params=_params("parallel", "arbitrary"),
    )(q, kp, vp, bias, dy)


RQ_BLK = RET_COL // RET_QK_WIDTH
RV_BLK = (RET_COL + 2 * RET_QK_WIDTH) // RET_V_WIDTH
RET_PAIR = 2 * RET_QK_DIM
RET_LOG_GAMMA = tuple(math.log1p(-2.0 ** (-5.0 - h)) for h in range(RET_HEADS))


def _ret_decays(h):
    L = CHUNK
    lg = RET_LOG_GAMMA[h]
    row = lax.broadcasted_iota(jnp.int32, (L, L), 0)
    col = lax.broadcasted_iota(jnp.int32, (L, L), 1)
    rel = (row - col).astype(F32)
    dm = jnp.where(rel >= 0, jnp.exp(jnp.maximum(rel, 0.0) * lg), 0.0)
    idx = lax.broadcasted_iota(jnp.int32, (L, 1), 0).astype(F32)
    kte = jnp.exp((L - 1 - idx) * lg)
    qfs = jnp.exp((idx + 1.0) * lg)
    return dm, kte, qfs, math.exp(L * lg)


def _ret_head(t, h):
    tile = t[:, (h // 2) * RET_PAIR:(h // 2 + 1) * RET_PAIR]
    low = lax.broadcasted_iota(jnp.int32, tile.shape, 1) < RET_QK_DIM
    return jnp.where(low if h % 2 == 0 else ~low, tile, 0.0)


def _ret_fwd(proj, cos, sin, gain, y, name):
    S = proj.shape[0]
    L = CHUNK
    T = min(512, S)
    CPS = T // L
    NC = S // L

    def body(q_ref, k_ref, v_ref, g_ref, cos_ref, sin_ref, gain_ref, yin_ref, y_ref, o_ref, st_ref, state):
        i = pl.program_id(0)

        @pl.when(i == 0)
        def _():
            state[...] = jnp.zeros_like(state)

        dec = [_ret_decays(h) for h in range(RET_HEADS)]
        for c in range(CPS):
            rows = pl.ds(c * L, L)
            cw = _widen(cos_ref[rows, :], RET_QK_WIDTH)
            sw = _widen(sin_ref[rows, :], RET_QK_WIDTH)
            qv = _rope(q_ref[rows, :], cw, sw)
            kv = _rope(k_ref[rows, :], cw, sw) * (RET_QK_DIM ** -0.5)
            for h in range(RET_HEADS):
                dm, kte, qfs, cd = dec[h]
                qh, kh = _ret_head(qv, h), _ret_head(kv, h)
                vs = slice(h * RET_V_DIM, (h + 1) * RET_V_DIM)
                vh = v_ref[rows, vs]
                sp = state[h]
                st_ref[c, h] = sp
                o = _bdot(_bdot(qh, kh, NT) * dm, vh, NN) + _bdot(qh * qfs, sp, NN)
                state[h] = cd * sp + _bdot(kh * kte, vh, TN)
                o_ref[rows, vs] = o
                gh = g_ref[rows, vs]
                r = lax.rsqrt(jnp.mean(o * o, axis=-1, keepdims=True) + EPS)
                y_ref[rows, vs] = (o * r * gain_ref[:, vs] * (gh * _sigmoid(gh))).astype(y_ref.dtype)

    tab = pl.BlockSpec((T, 128), lambda i: (i, 0))
    wide = pl.BlockSpec((T, RET_V_WIDTH), lambda i: (i, 0))
    return pl.pallas_call(
        body, grid=(S // T,),
        in_specs=[pl.BlockSpec((T, RET_QK_WIDTH), lambda i: (i, RQ_BLK)), pl.BlockSpec((T, RET_QK_WIDTH), lambda i: (i, RQ_BLK + 1)),
                  pl.BlockSpec((T, RET_V_WIDTH), lambda i: (i, RV_BLK)), pl.BlockSpec((T, RET_V_WIDTH), lambda i: (i, RV_BLK + 1)),
                  tab, tab, pl.BlockSpec((1, RET_V_WIDTH), lambda i: (0, 0)), pl.BlockSpec(memory_space=pl.ANY)],
        out_specs=[pl.BlockSpec((T, RET_V_WIDTH), lambda i: (i, (SSD_WIDTH + ATT_WIDTH) // RET_V_WIDTH)), wide,
                   pl.BlockSpec((CPS, RET_HEADS, RET_PAIR, RET_V_DIM), lambda i: (i, 0, 0, 0))],
        out_shape=[jax.ShapeDtypeStruct(y.shape, y.dtype), jax.ShapeDtypeStruct((S, RET_V_WIDTH), F32),
                   jax.ShapeDtypeStruct((NC, RET_HEADS, RET_PAIR, RET_V_DIM), F32)],
        input_output_aliases={7: 0},
        scratch_shapes=[pltpu.VMEM((RET_HEADS, RET_PAIR, RET_V_DIM), F32)], name=name,
        compiler_params=_params("arbitrary"),
    )(proj, proj, proj, proj, cos, sin, gain, y)


def _ret_bwd(dy, oraw, proj, cos, sin, gain, states, dproj, name):
    S = proj.shape[0]
    L = CHUNK
    T = min(512, S)
    CPS = T // L
    NI = S // T
    QW, VW = RET_QK_WIDTH, RET_V_WIDTH
    V0, G0 = 2 * QW, 2 * QW + VW

    def body(dy_ref, o_ref, q_ref, k_ref, v_ref, g_ref, cos_ref, sin_ref, gain_ref, st_ref, dproj_ref,
             out_ref, dgain_ref, dstate, dqs, dks):
        i = pl.program_id(0)

        @pl.when(i == 0)
        def _():
            dstate[...] = jnp.zeros_like(dstate)
            dgain_ref[...] = jnp.zeros_like(dgain_ref)

        dec = [_ret_decays(h) for h in range(RET_HEADS)]
        for c in reversed(range(CPS)):
            rows = pl.ds(c * L, L)
            cw = _widen(cos_ref[rows, :], RET_QK_WIDTH)
            sw = _widen(sin_ref[rows, :], RET_QK_WIDTH)
            qv = _rope(q_ref[rows, :], cw, sw)
            kv = _rope(k_ref[rows, :], cw, sw) * (RET_QK_DIM ** -0.5)
            for h in range(RET_HEADS):
                dm, kte, qfs, cd = dec[h]
                pair = slice((h // 2) * RET_PAIR, (h // 2 + 1) * RET_PAIR)
                vs = slice(h * RET_V_DIM, (h + 1) * RET_V_DIM)
                qh, kh = _ret_head(qv, h), _ret_head(kv, h)
                vh = v_ref[rows, vs]
                gh = g_ref[rows, vs]
                gn = gain_ref[:, vs]
                o = o_ref[rows, vs]
                dyh = dy_ref[rows, vs]
                sg = _sigmoid(gh)
                silu_g = gh * sg
                r = lax.rsqrt(jnp.mean(o * o, axis=-1, keepdims=True) + EPS)
                n = o * r
                dgain_ref[:, vs] += jnp.sum(dyh * n * silu_g, axis=0, keepdims=True)
                out_ref[rows, G0 + h * RET_V_DIM:G0 + (h + 1) * RET_V_DIM] = (dyh * n * gn * _silu_grad(gh, sg)).astype(out_ref.dtype)
                dn = dyh * gn * silu_g
                do = r * (dn - n * jnp.mean(dn * n, axis=-1, keepdims=True))
                sp = st_ref[c, h]
                ds = dstate[h]
                sc = _bdot(qh, kh, NT) * dm
                dsc = _bdot(do, vh, NT) * dm
                out_ref[rows, V0 + h * RET_V_DIM:V0 + (h + 1) * RET_V_DIM] = (_bdot(sc, do, TN) + _bdot(kh * kte, ds, NN)).astype(out_ref.dtype)
                dqh = _bdot(dsc, kh, NN) + _bdot(do, sp, NT) * qfs
                dkh = _bdot(dsc, qh, TN) + _bdot(vh, ds, NT) * kte
                if h % 2 == 0:
                    dqs[:, pair] = dqh
                    dks[:, pair] = dkh
                else:
                    dqs[:, pair] += dqh
                    dks[:, pair] += dkh
                dstate[h] = cd * ds + _bdot(qh * qfs, do, TN)
            out_ref[rows, 0:QW] = _rope_t(dqs[...], cw, sw).astype(out_ref.dtype)
            out_ref[rows, QW:2 * QW] = _rope_t(dks[...] * (RET_QK_DIM ** -0.5), cw, sw).astype(out_ref.dtype)

    rev = lambda i: NI - 1 - i
    tab = pl.BlockSpec((T, 128), lambda i: (rev(i), 0))
    wide = pl.BlockSpec((T, RET_V_WIDTH), lambda i: (rev(i), 0))
    group = pl.BlockSpec((T, G0 + VW), lambda i: (rev(i), RET_COL // (G0 + VW)))
    gvec = pl.BlockSpec((1, RET_V_WIDTH), lambda i: (0, 0))
    return pl.pallas_call(
        body, grid=(NI,),
        in_specs=[pl.BlockSpec((T, RET_V_WIDTH), lambda i: (rev(i), (SSD_WIDTH + ATT_WIDTH) // RET_V_WIDTH)), wide,
                  pl.BlockSpec((T, RET_QK_WIDTH), lambda i: (rev(i), RQ_BLK)), pl.BlockSpec((T, RET_QK_WIDTH), lambda i: (rev(i), RQ_BLK + 1)),
                  pl.BlockSpec((T, RET_V_WIDTH), lambda i: (rev(i), RV_BLK)), pl.BlockSpec((T, RET_V_WIDTH), lambda i: (rev(i), RV_BLK + 1)),
                  tab, tab, gvec,
                  pl.BlockSpec((CPS, RET_HEADS, RET_PAIR, RET_V_DIM), lambda i: (rev(i), 0, 0, 0)),
                  pl.BlockSpec(memory_space=pl.ANY)],
        out_specs=[group, gvec],
        out_shape=[jax.ShapeDtypeStruct(dproj.shape, dproj.dtype), jax.ShapeDtypeStruct((1, RET_V_WIDTH), F32)],
        input_output_aliases={10: 0},
        scratch_shapes=[pltpu.VMEM((RET_HEADS, RET_PAIR, RET_V_DIM), F32), pltpu.VMEM((L, RET_QK_WIDTH), F32),
                        pltpu.VMEM((L, RET_QK_WIDTH), F32)],
        name=name, compiler_params=_params("arbitrary"),
    )(dy, oraw, proj, proj, proj, proj, cos, sin, gain, states, dproj)


def _adamw_update(g_ref, nb, w_ref, m_ref, v_ref, go_ref, d_ref, mo_ref, vo_ref):
    g = g_ref[0].astype(F32)
    for k in range(1, nb):
        g = g + g_ref[k].astype(F32)
    mn = ADAM_B1 * m_ref[...] + (1.0 - ADAM_B1) * g
    vn = ADAM_B2 * v_ref[...] + (1.0 - ADAM_B2) * (g * g)
    go_ref[...] = g
    mo_ref[...] = mn
    vo_ref[...] = vn
    c1 = 1.0 - ADAM_B1 ** ADAM_STEP
    c2 = 1.0 - ADAM_B2 ** ADAM_STEP
    d_ref[...] = -ADAM_LR * ((mn / c1) / (jnp.sqrt(vn / c2) + ADAM_EPS) + ADAM_WD * w_ref[...])


def _adamw_rows(R, C):
    return _pick(R, tuple(t for t in (512, 256, 128, 64, 32, 16, 8) if t * C <= 256 * 1024))


def _adamw(gblocks, w, m, v, name):
    nb, R, C = gblocks.shape
    tr = _adamw_rows(R, C)

    def body(g_ref, *refs):
        _adamw_update(g_ref, nb, *refs)

    row = pl.BlockSpec((tr, C), lambda i: (i, 0))
    sh = jax.ShapeDtypeStruct((R, C), F32)
    return pl.pallas_call(
        body, grid=(R // tr,), in_specs=[pl.BlockSpec((nb, tr, C), lambda i: (0, i, 0)), row, row, row],
        out_specs=[row, row, row, row], out_shape=[sh, sh, sh, sh], name=name, compiler_params=_params("parallel"),
    )(gblocks, w, m, v)


def _adamw_layers(g0, g1, w, m, v, name):
    nb, R, C = g0.shape
    tr = _adamw_rows(R, C)

    def body(g0_ref, g1_ref, *refs):
        l = pl.program_id(0)

        @pl.when(l == 0)
        def _():
            _adamw_update(g0_ref, nb, *refs)

        @pl.when(l == 1)
        def _():
            _adamw_update(g1_ref, nb, *refs)

    row = pl.BlockSpec((None, tr, C), lambda l, i: (l, i, 0))
    sh = jax.ShapeDtypeStruct((DEPTH, R, C), F32)
    return pl.pallas_call(
        body, grid=(DEPTH, R // tr),
        in_specs=[pl.BlockSpec((nb, tr, C), lambda l, i: (0, i * (1 - l), 0)), pl.BlockSpec((nb, tr, C), lambda l, i: (0, i * l, 0)),
                  row, row, row],
        out_specs=[row, row, row, row], out_shape=[sh, sh, sh, sh], name=name, compiler_params=_params("arbitrary", "arbitrary"),
    )(g0, g1, w, m, v)


def _peers():
    x, y, c = lax.axis_index("x"), lax.axis_index("y"), lax.axis_index("c")
    flips = ((0, 0, 1), (1, 0, 0), (0, 1, 0), (1, 1, 0), (1, 0, 1), (0, 1, 1), (1, 1, 1))
    me = 4 * x + 2 * y + c
    peers = [(x ^ fx, y ^ fy, c ^ fc) for fx, fy, fc in flips]
    return me, peers


def _exchange(arrs, scatter, name):
    n = len(arrs)
    npeer = N_DEV - 1

    def body(*refs):
        ins, outs = refs[:n], refs[n:2 * n]
        send_sems, recv_sems, local_sems = refs[2 * n:]
        me, peers = _peers()
        copies = []
        for a in range(n):
            src_own = ins[a].at[me] if scatter else ins[a]
            own = pltpu.make_async_copy(src_own, outs[a].at[me], local_sems.at[a])
            own.start()
            copies.append(own)
            for k, peer in enumerate(peers):
                src = ins[a].at[4 * peer[0] + 2 * peer[1] + peer[2]] if scatter else ins[a]
                cp = pltpu.make_async_remote_copy(
                    src_ref=src, dst_ref=outs[a].at[me], send_sem=send_sems.at[a * npeer + k],
                    recv_sem=recv_sems.at[a * npeer + k], device_id=peer, device_id_type=pl.DeviceIdType.MESH)
                cp.start()
                copies.append(cp)
        for cp in copies:
            cp.wait()

    out_shape = [jax.ShapeDtypeStruct(((N_DEV,) + a.shape[1:]) if scatter else ((N_DEV,) + a.shape), a.dtype) for a in arrs]
    anyspec = pl.BlockSpec(memory_space=pl.ANY)
    return pl.pallas_call(
        body, in_specs=[anyspec] * n, out_specs=[anyspec] * n, out_shape=out_shape,
        scratch_shapes=[pltpu.SemaphoreType.DMA((n * npeer,)), pltpu.SemaphoreType.DMA((n * npeer,)),
                        pltpu.SemaphoreType.DMA((n,))],
        name=name,
    )(*arrs)


def _dev_index(peer):
    return 4 * peer[0] + 2 * peer[1] + peer[2]


def _push_copies(src_refs, land_refs, send_sems, recv_sems, scatter, as_receiver):
    me, peers = _peers()
    npeer = N_DEV - 1
    copies = []
    for a in range(len(src_refs)):
        for k, peer in enumerate(peers):
            src = src_refs[a].at[_dev_index(peer)] if scatter else src_refs[a]
            slot = _dev_index(peer) if as_receiver else me
            copies.append(pltpu.make_async_remote_copy(
                src_ref=src, dst_ref=land_refs[a].at[slot], send_sem=send_sems.at[a * npeer + k],
                recv_sem=recv_sems.at[a * npeer + k], device_id=peer, device_id_type=pl.DeviceIdType.MESH))
    return copies


def _own_copies(src_refs, land_refs, own_sems, scatter):
    me, _ = _peers()
    return [pltpu.make_async_copy(src_refs[a].at[me] if scatter else src_refs[a], land_refs[a].at[me], own_sems.at[a])
            for a in range(len(src_refs))]


def _push_start(srcs, scatter, name):
    n = len(srcs)
    nsem = n * (N_DEV - 1)

    def body(*refs):
        srcs_r, lands_r = refs[:n], refs[n:2 * n]
        for cp in _push_copies(srcs_r, lands_r, refs[2 * n], refs[2 * n + 1], scatter, False):
            cp.start()
        for cp in _own_copies(srcs_r, lands_r, refs[2 * n + 2], scatter):
            cp.start()
        token = refs[-1]
        token[...] = jnp.zeros_like(token)

    hbm = pl.BlockSpec(memory_space=pltpu.HBM)
    sem = pl.BlockSpec(memory_space=pltpu.SEMAPHORE)
    lands = [lax.empty((N_DEV,) + (s.shape[1:] if scatter else s.shape), s.dtype) for s in srcs]
    arrs = list(srcs) + lands
    return pl.pallas_call(
        body, name=name,
        out_shape=(pltpu.SemaphoreType.DMA((nsem,)), pltpu.SemaphoreType.DMA((nsem,)), pltpu.SemaphoreType.DMA((n,)),
                   *[pltpu.HBM(a.shape, a.dtype) for a in arrs], jax.ShapeDtypeStruct((8, 128), F32)),
        in_specs=[hbm] * (2 * n), out_specs=(sem, sem, sem, *([hbm] * (2 * n)), pl.BlockSpec(memory_space=pltpu.VMEM)),
        input_output_aliases={i: 3 + i for i in range(2 * n)},
        compiler_params=pltpu.CompilerParams(has_side_effects=pltpu.SideEffectType.DATAFLOW_SIDE_EFFECTING),
    )(*[pltpu.with_memory_space_constraint(a, pltpu.HBM) for a in arrs])


def _push_wait(handle, after, scatter, name):
    send_sems, recv_sems, own_sems, *thru, _ = handle
    n = len(thru) // 2

    def body(*refs):
        srcs_r, lands_r = refs[:n], refs[n:2 * n]
        for cp in _push_copies(srcs_r, lands_r, refs[2 * n], refs[2 * n + 1], scatter, True):
            cp.wait_send()
            cp.wait_recv()
        for cp in _own_copies(srcs_r, lands_r, refs[2 * n + 2], scatter):
            cp.wait()

    hbm = pl.BlockSpec(memory_space=pltpu.HBM)
    sem = pl.BlockSpec(memory_space=pltpu.SEMAPHORE)
    outs = pl.pallas_call(
        body, name=name, out_shape=tuple(pltpu.HBM(a.shape, a.dtype) for a in thru),
        in_specs=[hbm] * (2 * n) + [sem, sem, sem, pl.BlockSpec(memory_space=pl.ANY)], out_specs=tuple([hbm] * (2 * n)),
        input_output_aliases={i: i for i in range(2 * n)},
        compiler_params=pltpu.CompilerParams(has_side_effects=pltpu.SideEffectType.DATAFLOW_SIDE_EFFECTING),
    )(*thru, send_sems, recv_sems, own_sems, after)
    return list(outs[n:])


def _relay_copies(src_refs, land_refs, send1, recv1, send2, recv2, as_receiver):
    me, peers = _peers()
    sibling, chips = peers[0], peers[1:4]
    first, second = [], []
    for a in range(len(src_refs)):
        for k, peer in enumerate([sibling] + chips):
            slot = _dev_index(peer) if as_receiver else me
            first.append(pltpu.make_async_remote_copy(
                src_ref=src_refs[a], dst_ref=land_refs[a].at[slot], send_sem=send1.at[4 * a + k], recv_sem=recv1.at[4 * a + k],
                device_id=peer, device_id_type=pl.DeviceIdType.MESH))
        for k, chip in enumerate(chips):
            origin = _dev_index(chip)
            slot = origin ^ 1 if as_receiver else origin
            second.append(pltpu.make_async_remote_copy(
                src_ref=land_refs[a].at[origin], dst_ref=land_refs[a].at[slot], send_sem=send2.at[3 * a + k],
                recv_sem=recv2.at[3 * a + k], device_id=sibling, device_id_type=pl.DeviceIdType.MESH))
    return first, second


def _relay_start(srcs, name):
    n = len(srcs)

    def body(*refs):
        srcs_r, lands_r = refs[:n], refs[n:2 * n]
        for cp in _relay_copies(srcs_r, lands_r, refs[2 * n], refs[2 * n + 1], refs[2 * n], refs[2 * n + 1], False)[0]:
            cp.start()
        for cp in _own_copies(srcs_r, lands_r, refs[2 * n + 2], False):
            cp.start()
        token = refs[-1]
        token[...] = jnp.zeros_like(token)

    hbm = pl.BlockSpec(memory_space=pltpu.HBM)
    sem = pl.BlockSpec(memory_space=pltpu.SEMAPHORE)
    arrs = list(srcs) + [lax.empty((N_DEV,) + s.shape, s.dtype) for s in srcs]
    return pl.pallas_call(
        body, name=name,
        out_shape=(pltpu.SemaphoreType.DMA((4 * n,)), pltpu.SemaphoreType.DMA((4 * n,)), pltpu.SemaphoreType.DMA((n,)),
                   *[pltpu.HBM(a.shape, a.dtype) for a in arrs], jax.ShapeDtypeStruct((8, 128), F32)),
        in_specs=[hbm] * (2 * n), out_specs=(sem, sem, sem, *([hbm] * (2 * n)), pl.BlockSpec(memory_space=pltpu.VMEM)),
        input_output_aliases={i: 3 + i for i in range(2 * n)},
        compiler_params=pltpu.CompilerParams(has_side_effects=pltpu.SideEffectType.DATAFLOW_SIDE_EFFECTING),
    )(*[pltpu.with_memory_space_constraint(a, pltpu.HBM) for a in arrs])


def _relay_forward(handle, after, name):
    _, recv1, _, *thru, _ = handle
    n = len(thru) // 2

    def body(*refs):
        srcs_r, lands_r, recv1_r = refs[:n], refs[n:2 * n], refs[2 * n]
        send2_r, recv2_r = refs[2 * n + 2], refs[2 * n + 3]
        first, second = _relay_copies(srcs_r, lands_r, recv1_r, recv1_r, send2_r, recv2_r, True)
        for a in range(n):
            for k in range(1, 4):
                first[4 * a + k].wait_recv()
        for cp in _relay_copies(srcs_r, lands_r, recv1_r, recv1_r, send2_r, recv2_r, False)[1]:
            cp.start()
        token = refs[-1]
        token[...] = jnp.zeros_like(token)

    hbm = pl.BlockSpec(memory_space=pltpu.HBM)
    sem = pl.BlockSpec(memory_space=pltpu.SEMAPHORE)
    return pl.pallas_call(
        body, name=name,
        out_shape=(pltpu.SemaphoreType.DMA((3 * n,)), pltpu.SemaphoreType.DMA((3 * n,)),
                   *[pltpu.HBM(a.shape, a.dtype) for a in thru], jax.ShapeDtypeStruct((8, 128), F32)),
        in_specs=[hbm] * (2 * n) + [sem, pl.BlockSpec(memory_space=pl.ANY)],
        out_specs=(sem, sem, *([hbm] * (2 * n)), pl.BlockSpec(memory_space=pltpu.VMEM)),
        input_output_aliases={i: 2 + i for i in range(2 * n)},
        compiler_params=pltpu.CompilerParams(has_side_effects=pltpu.SideEffectType.DATAFLOW_SIDE_EFFECTING),
    )(*thru, recv1, after)


def _relay_wait(handle, forwarded, after, name):
    send1, recv1, own_sems, *_ = handle
    send2, recv2, *thru, _ = forwarded
    n = len(thru) // 2

    def body(*refs):
        srcs_r, lands_r = refs[:n], refs[n:2 * n]
        send1_r, recv1_r, own_r, send2_r, recv2_r = refs[2 * n:2 * n + 5]
        first, second = _relay_copies(srcs_r, lands_r, send1_r, recv1_r, send2_r, recv2_r, True)
        for i, cp in enumerate(first):
            cp.wait_send()
            if i % 4 == 0:
                cp.wait_recv()
        for cp in second:
            cp.wait_send()
            cp.wait_recv()
        for cp in _own_copies(srcs_r, lands_r, own_r, False):
            cp.wait()

    hbm = pl.BlockSpec(memory_space=pltpu.HBM)
    sem = pl.BlockSpec(memory_space=pltpu.SEMAPHORE)
    outs = pl.pallas_call(
        body, name=name, out_shape=tuple(pltpu.HBM(a.shape, a.dtype) for a in thru),
        in_specs=[hbm] * (2 * n) + [sem] * 5 + [pl.BlockSpec(memory_space=pl.ANY)], out_specs=tuple([hbm] * (2 * n)),
        input_output_aliases={i: i for i in range(2 * n)},
        compiler_params=pltpu.CompilerParams(has_side_effects=pltpu.SideEffectType.DATAFLOW_SIDE_EFFECTING),
    )(*thru, send1, recv1, own_sems, send2, recv2, after)
    return list(outs[n:])


def _tables(S):
    pos = jnp.arange(S, dtype=F32)
    inv = ROPE_THETA ** (-jnp.arange(0, ATT_HEAD_DIM, 2, dtype=F32) / ATT_HEAD_DIM)
    ang = pos[:, None] * inv[None, :]
    cos, sin = jnp.cos(ang), jnp.sin(ang)
    cos128 = jnp.tile(cos, (1, 4))
    sin128 = jnp.tile(jnp.concatenate([-sin, sin], axis=1), (1, 2))
    lane = np.arange(ATT_WIDTH)
    bd = jnp.asarray((lane[:, None] // 64 == lane[None, :] // 64).astype(np.float32), dtype=BF16)
    return cos128, sin128, bd, jnp.asarray(_att_bias())


def _layer_fwd(l, x, p, tabs, early=None, late=None):
    cos, sin, bd, bias = tabs
    S = x.shape[0]
    row = lambda v: v.reshape(1, -1)
    hn = _rmsnorm_fwd(x, row(p["ln_mix"]), f"norm_mix_fwd{l}")
    if early is not None:
        p.update(early(hn))
    proj = _mm(hn, p["w_in"], "nn", f"in_proj{l}", tn=1920)
    xbc = _conv_fwd(proj, p["conv_w"], row(p["conv_b"]), f"conv_fwd{l}")
    dtr = proj[:, DT_COL:DT_COL + SSD_HEADS].reshape(S, SSD_GROUPS, HPG).transpose(1, 0, 2)
    grp = lambda v: v.reshape(SSD_GROUPS, 1, HPG)
    d_rep = row(jnp.repeat(p["d_skip"], SSD_HEAD_DIM))
    y, yraw, ssd_st = _ssd_fwd(xbc, proj, dtr, grp(p["dt_bias"]), grp(p["a_log"]), d_rep, row(p["ssd_norm"]), f"ssd_fwd{l}")
    qg = row(jnp.tile(p["q_norm"], ATT_HEADS))
    kg = row(jnp.tile(p["k_norm"], ATT_HEADS))
    aq, akp, avp = _att_prep_fwd(proj, qg, kg, cos, sin, bd, f"att_prep_fwd{l}")
    y = _att_fwd(aq, akp, avp, bias, y, f"att_fwd{l}")
    y, oraw, ret_st = _ret_fwd(proj, cos, sin, row(p["ret_norm"]), y, f"ret_fwd{l}")
    if late is not None:
        p.update(late(y))
    x1 = _mm(y, p["w_out"], "nn", f"out_proj{l}", residual=x)
    hn2 = _rmsnorm_fwd(x1, row(p["ln_ffn"]), f"norm_ffn_fwd{l}")
    g, u, act = _swiglu_fwd(hn2, p["w_gate"], p["w_up"], f"swiglu_fwd{l}")
    x2 = _mm(act, p["w_down"], "nn", f"down_proj{l}", residual=x1, tk=2816)
    saved = dict(x=x, hn=hn, proj=proj, xbc=xbc, dtr=dtr, yraw=yraw, ssd_st=ssd_st, aq=aq, akp=akp, avp=avp,
                 oraw=oraw, ret_st=ret_st, y=y, x1=x1, hn2=hn2, g=g, u=u, act=act, d_rep=d_rep, qg=qg, kg=kg)
    return x2, saved


def _layer_bwd(l, dx2, dx2_bf, p, sv, tabs, on_ffn=None, on_all=None):
    cos, sin, bd, bias = tabs
    S = dx2.shape[0]
    row = lambda v: v.reshape(1, -1)
    grp = lambda v: v.reshape(SSD_GROUPS, 1, HPG)
    gr = {}
    dg, du = _swiglu_bwd(dx2_bf, p["w_down"], sv["g"], sv["u"], f"swiglu_bwd{l}")
    gr["w_down"] = _mm(sv["act"], dx2_bf, "tn", f"down_wgrad{l}", out_dtype=BF16, tm=1408, tn=1024, tk=2048)
    dhn2 = _mm_nt2(dg, p["w_gate"], du, p["w_up"], f"ffn_dgrad{l}")
    gr["w_gate"] = _mm(sv["hn2"], dg, "tn", f"gate_wgrad{l}", out_dtype=BF16, tm=512, tn=2816, tk=2048)
    gr["w_up"] = _mm(sv["hn2"], du, "tn", f"up_wgrad{l}", out_dtype=BF16, tm=512, tn=2816, tk=2048)
    ffn_gain = row(p["ln_ffn"]) + (on_ffn(gr)[0, 0] if on_ffn is not None else 0.0)
    dx1, dx1_bf, dln_ffn = _rmsnorm_bwd(sv["x1"], dhn2, ffn_gain, dx2, f"norm_ffn_bwd{l}")
    gr["ln_ffn"] = dln_ffn[0]
    dy = _mm(dx1_bf, p["w_out"], "nt", f"out_dgrad{l}")
    gr["w_out"] = _mm(sv["y"], dx1_bf, "tn", f"out_wgrad{l}", out_dtype=BF16, tm=1024, tn=1024, tk=2048)
    dproj, dxs, dbm, dcm, ddtr, dbias, dalog, dd, dssd_gain = _ssd_bwd(
        dy, sv["yraw"], sv["xbc"], sv["proj"], sv["dtr"], grp(p["dt_bias"]), grp(p["a_log"]), sv["d_rep"],
        row(p["ssd_norm"]), sv["ssd_st"], f"ssd_bwd{l}")
    gr["dt_bias"], gr["a_log"], gr["d_skip"] = dbias.reshape(-1), dalog.reshape(-1), dd.reshape(-1)
    gr["ssd_norm"] = dssd_gain[0]
    dproj, dconv_w, dconv_b = _conv_bwd(dxs, dbm, dcm, sv["proj"], p["conv_w"], row(p["conv_b"]), dproj, f"conv_bwd{l}")
    gr["conv_w"], gr["conv_b"] = dconv_w, dconv_b[0]
    dq, dk_p, dv_p = _att_bwd(sv["aq"], sv["akp"], sv["avp"], bias, dy, f"att_bwd{l}")
    dproj, dqg, dkg = _att_prep_bwd(sv["proj"], dq, dk_p, dv_p, sv["qg"], sv["kg"], cos, sin, bd, dproj, f"att_prep_bwd{l}")
    gr["q_norm"], gr["k_norm"] = dqg[0], dkg[0]
    dproj, dret_gain = _ret_bwd(dy, sv["oraw"], sv["proj"], cos, sin, row(p["ret_norm"]), sv["ret_st"], dproj, f"ret_bwd{l}")
    gr["ret_norm"] = dret_gain[0]
    ddt_cols = ddtr.transpose(1, 0, 2).reshape(S, SSD_HEADS).astype(BF16)
    dproj = lax.dynamic_update_slice(dproj, jnp.pad(ddt_cols, ((0, 0), (0, IN_PAD - DT_COL - SSD_HEADS))), (0, DT_COL))
    gr["w_in"] = _mm(sv["hn"], dproj, "tn", f"in_wgrad{l}", out_dtype=BF16, tm=1024, tn=1920, tk=2048)
    launched = on_all(gr) if on_all is not None else None
    dhn = _mm(dproj, p["w_in"], "nt", f"in_dgrad{l}", tm=512, tk=IN_PAD, after=launched)
    dx0, dx0_bf, dln_mix = _rmsnorm_bwd(sv["x"], dhn, row(p["ln_mix"]), dx1, f"norm_mix_bwd{l}")
    gr["ln_mix"] = dln_mix[0]
    return dx0, dx0_bf, gr


def _local_step(x, tgt, layers, early=None, late=None, on_ffn=None, on_all=None):
    n = len(layers)
    none = [None] * n
    early, late, on_ffn, on_all = early or none, late or none, on_ffn or none, on_all or none
    tabs = _tables(x.shape[0])
    saved, params = [], []
    h = x
    for l in range(n):
        p = dict(layers[l](h) if callable(layers[l]) else layers[l])
        h, sv = _layer_fwd(l, h, p, tabs, early[l], late[l])
        saved.append(sv)
        params.append(p)
    dh, dh_bf, lacc = _loss_grad(h, tgt, "loss_grad")
    grads = [None] * n
    for l in reversed(range(n)):
        dh, dh_bf, grads[l] = _layer_bwd(l, dh, dh_bf, params[l], saved[l], tabs, on_ffn[l], on_all[l])
    return lacc[0, 0], dh, grads


BIG = ("w_in", "w_out", "w_gate", "w_up", "w_down")
SMALL = ("ln_mix", "conv_b", "dt_bias", "a_log", "d_skip", "ssd_norm", "q_norm", "k_norm", "ret_norm", "ln_ffn")
ORDER = ("ln_mix", "w_in", "conv_w", "conv_b", "dt_bias", "a_log", "d_skip", "ssd_norm", "q_norm", "k_norm", "ret_norm",
         "w_out", "ln_ffn", "w_gate", "w_up", "w_down")


COL_SHARDED = ("w_in", "w_gate", "w_up", "conv_w")


IN_GROUPS = ((ORIG_Z_XBC, Z_COL), (ORIG_DT, DT_COL), (ORIG_ATT, ATT_COL), (ORIG_RET, RET_COL))


def _full_weight(k, gathered):
    if k == "w_in":
        cs = gathered.shape[2]
        pieces = []
        for (lo, hi), _ in sorted(IN_GROUPS, key=lambda grp: grp[1]):
            for j in range(N_DEV):
                a, b = max(lo, j * cs), min(hi, (j + 1) * cs)
                if a < b:
                    pieces.append(gathered[j][:, a - j * cs:b - j * cs])
        pieces.append(jnp.zeros((gathered.shape[1], IN_PAD - IN_WIDTH), gathered.dtype))
        return jnp.concatenate(pieces, axis=1)
    if k in COL_SHARDED:
        return gathered.transpose(1, 0, 2).reshape(gathered.shape[1], -1)
    return gathered.reshape(-1, gathered.shape[2])


def _shard_block(k, g):
    if k == "w_in":
        cs = IN_WIDTH // N_DEV
        blocks = []
        for j in range(N_DEV):
            pieces = []
            for (lo, hi), col in IN_GROUPS:
                a, b = max(lo, j * cs), min(hi, (j + 1) * cs)
                if a < b:
                    pieces.append(g[:, col + a - lo:col + b - lo])
            blocks.append(jnp.concatenate(pieces, axis=1))
        return jnp.stack(blocks)
    if k in COL_SHARDED:
        return g.reshape(g.shape[0], N_DEV, -1).transpose(1, 0, 2)
    return g.reshape(N_DEV, -1, g.shape[1])


def kernel(x, ln_mix, w_in, conv_w, conv_b, dt_bias, a_log, d_skip, ssd_norm, q_norm, k_norm, ret_norm, w_out, ln_ffn, w_gate, w_up, w_down, loss_target, m_ln_mix, m_w_in, m_conv_w, m_conv_b, m_dt_bias, m_a_log, m_d_skip, m_ssd_norm, m_q_norm, m_k_norm, m_ret_norm, m_w_out, m_ln_ffn, m_w_gate, m_w_up, m_w_down, v_ln_mix, v_w_in, v_conv_w, v_conv_b, v_dt_bias, v_a_log, v_d_skip, v_ssd_norm, v_q_norm, v_k_norm, v_ret_norm, v_w_out, v_ln_ffn, v_w_gate, v_w_up, v_w_down):
    w = dict(ln_mix=ln_mix, w_in=w_in, conv_w=conv_w, conv_b=conv_b, dt_bias=dt_bias, a_log=a_log, d_skip=d_skip,
             ssd_norm=ssd_norm, q_norm=q_norm, k_norm=k_norm, ret_norm=ret_norm, w_out=w_out, ln_ffn=ln_ffn,
             w_gate=w_gate, w_up=w_up, w_down=w_down)
    m = dict(ln_mix=m_ln_mix, w_in=m_w_in, conv_w=m_conv_w, conv_b=m_conv_b, dt_bias=m_dt_bias, a_log=m_a_log,
             d_skip=m_d_skip, ssd_norm=m_ssd_norm, q_norm=m_q_norm, k_norm=m_k_norm, ret_norm=m_ret_norm, w_out=m_w_out,
             ln_ffn=m_ln_ffn, w_gate=m_w_gate, w_up=m_w_up, w_down=m_w_down)
    v = dict(ln_mix=v_ln_mix, w_in=v_w_in, conv_w=v_conv_w, conv_b=v_conv_b, dt_bias=v_dt_bias, a_log=v_a_log,
             d_skip=v_d_skip, ssd_norm=v_ssd_norm, q_norm=v_q_norm, k_norm=v_k_norm, ret_norm=v_ret_norm, w_out=v_w_out,
             ln_ffn=v_ln_ffn, w_gate=v_w_gate, w_up=v_w_up, w_down=v_w_down)
    me = 4 * lax.axis_index("x") + 2 * lax.axis_index("y") + lax.axis_index("c")

    late_names = ("w_out", "w_gate", "w_up", "w_down")
    waves = {"a": [("w_in", 0), ("conv_w", 0), ("conv_w", 1)], "b": [(k, 0) for k in late_names],
             "c": [("w_in", 1)], "d": [(k, 1) for k in late_names]}
    gather = {}
    behind = 0.0
    for tag, items in waves.items():
        srcs = [w[k][l] if k == "conv_w" else (w[k][l] + behind).astype(BF16) for k, l in items]
        start = _relay_start(srcs, f"gather_{tag}_start") if tag == "a" else _push_start(srcs, False, f"gather_{tag}_start")
        gather[tag] = start
        behind = start[-1][0, 0]
    started = behind
    full = {}

    def arrive(tag, after):
        if tag == "a":
            forwarded = _relay_forward(gather["a"], after, "gather_a_forward")
            lands = _relay_wait(gather["a"], forwarded, forwarded[-1], "gather_a_wait")
        else:
            lands = _push_wait(gather[tag], after, False, f"gather_{tag}_wait")
        for (k, l), g in zip(waves[tag], lands):
            full[k, l] = _full_weight(k, g)

    def layer_weights(l, names):
        return {k: full[k, l] for k in names}

    def small_weights(l):
        return {k: w[k][l] for k in SMALL}

    def layer0(h):
        p = small_weights(0)
        p["ln_mix"] = p["ln_mix"] + started
        return p

    def early0(hn):
        arrive("a", hn)
        return layer_weights(0, ("w_in", "conv_w"))

    def late0(y):
        arrive("b", y)
        return layer_weights(0, late_names)

    def layer1(h):
        arrive("c", h)
        return {**small_weights(1), **layer_weights(1, ("w_in", "conv_w"))}

    def late1(y):
        arrive("d", y)
        return layer_weights(1, late_names)

    groups = {"1": [(k, 1) for k in BIG], "0a": [(k, 0) for k in ("w_down", "w_gate", "w_up")],
              "0b": [(k, 0) for k in ("w_out", "w_in")]}
    scatter = {}

    def push_grads(tag, gr):
        blocks = [_shard_block(k, gr[k]) for k, _ in groups[tag]]
        scatter[tag] = _push_start(blocks, True, f"scatter_{tag}_start")
        return scatter[tag][-1]

    loss_part, gx, grads = _local_step(
        x[0], loss_target[0], [layer0, layer1], early=[early0, None], late=[late0, late1],
        on_ffn=[functools.partial(push_grads, "0a"), None],
        on_all=[functools.partial(push_grads, "0b"), functools.partial(push_grads, "1")])
    loss = lax.psum(loss_part, MESH_AXES)

    out = {}
    recv = {}
    for tag, items in groups.items():
        for item, r in zip(items, _push_wait(scatter[tag], gx, True, f"scatter_{tag}_wait")):
            recv[item] = r
    for k in BIG:
        out[k] = _adamw_layers(recv[k, 0], recv[k, 1], w[k], m[k], v[k], f"adamw_{k}")
    names = SMALL + ("conv_w",)
    sizes = [int(np.prod(grads[0][k].shape)) for k in names]
    packed = jnp.concatenate([jnp.stack([grads[l][k] for l in range(DEPTH)]).reshape(-1) for k in names])
    n_small = packed.shape[0]
    rows_small = -(-n_small // 1024) * 8
    pad = lambda t, fill: jnp.concatenate([t, jnp.full((rows_small * 128 - n_small,), fill, F32)]).reshape(rows_small, 128)
    parts = _exchange([pad(packed, 0.0)], False, "gather_small_grads")[0]
    n_rep = DEPTH * sum(sizes[:-1])
    pack_rep = lambda d, fill: pad(jnp.concatenate([d[k].reshape(-1) for k in SMALL]
                                                   + [jnp.full((n_small - n_rep,), fill, F32)]), fill)
    res = _adamw(parts, pack_rep(w, 1.0), pack_rep(m, 1.0), pack_rep(v, 1.0), "adamw_small")
    res = [t.reshape(-1) for t in res]
    off = 0
    for k, sz in zip(SMALL, sizes[:-1]):
        out[k] = [t[off:off + DEPTH * sz].reshape(w[k].shape) for t in res]
        off += DEPTH * sz
    gconv = res[0][off:off + DEPTH * sizes[-1]].reshape(DEPTH, SSD_CONV, SSD_CONV_CH)
    gconv = lax.dynamic_slice_in_dim(gconv, me * conv_w.shape[2], conv_w.shape[2], axis=2)
    flat = lambda t: t.reshape(8, -1)
    resc = _adamw(flat(gconv)[None], flat(conv_w), flat(m_conv_w), flat(v_conv_w), "adamw_conv_w")
    out["conv_w"] = [t.reshape(conv_w.shape) for t in resc]

    return (loss, gx[None], *[out[k][0] for k in ORDER], *[out[k][1] for k in ORDER],
            *[out[k][2] for k in ORDER], *[out[k][3] for k in ORDER])
```

```python
import functools
import math

import jax
import jax.numpy as jnp
import numpy as np
from jax import lax
from jax.experimental import pallas as pl
from jax.experimental.pallas import tpu as pltpu

F32 = jnp.float32
BF16 = jnp.bfloat16

N_DEV = 8
MESH_AXES = ("x", "y", "c")
D_MODEL = 2048
DEPTH = 2
EPS = 1e-6
ROPE_THETA = 10000.0
SSD_HEADS = 16
SSD_HEAD_DIM = 64
SSD_WIDTH = 1024
SSD_GROUPS = 2
SSD_STATE = 128
SSD_CONV = 4
SSD_CONV_CH = 1536
ATT_HEADS = 8
ATT_HEAD_DIM = 64
ATT_WIDTH = 512
DILATED_PAIRS = ((128, 1), (512, 4), (2048, 16))
RET_HEADS = 4
RET_QK_DIM = 64
RET_V_DIM = 128
RET_QK_WIDTH = 256
RET_V_WIDTH = 512
CHUNK = 128
MIX_WIDTH = 2048
ATT_SPAN = 2048
ATT_STRIP = ATT_SPAN + CHUNK
ATT_QB = 8
IN_WIDTH = 5648
IN_PAD = 5760
RET_COL, ATT_COL, Z_COL, XBC_COL, DT_COL = 0, 1536, 3072, 4096, 5632
ORIG_Z_XBC, ORIG_DT, ORIG_ATT, ORIG_RET = (0, 2560), (2560, 2576), (2576, 4112), (4112, 5648)
D_FF = 5632
ADAM_LR = 0.001
ADAM_B1 = 0.9
ADAM_B2 = 0.999
ADAM_EPS = 1e-08
ADAM_WD = 0.01
ADAM_STEP = 10
NEG = -1e30
VMEM_LIMIT_V7X = 60 * 1024 * 1024

NN = (((1,), (0,)), ((), ()))
NT = (((1,), (1,)), ((), ()))
TN = (((0,), (0,)), ((), ()))


def _bdot(a, b, dims):
    return lax.dot_general(a.astype(BF16), b.astype(BF16), dims, preferred_element_type=F32)


def _xdot(a, b, dims, exact_first=False, pieces=3):
    ones, x = (a, b) if exact_first else (b, a)
    ones = ones.astype(BF16)
    acc, rest = None, x
    for _ in range(pieces):
        piece = rest.astype(BF16)
        rest = rest - piece.astype(F32)
        part = lax.dot_general(*((ones, piece) if exact_first else (piece, ones)), dims, preferred_element_type=F32)
        acc = part if acc is None else acc + part
    return acc


def _params(*sem):
    return pltpu.CompilerParams(dimension_semantics=sem, vmem_limit_bytes=VMEM_LIMIT_V7X)


def _sigmoid(v):
    return 1.0 / (1.0 + jnp.exp(-v))


def _silu_grad(v, s):
    return s * (1.0 + v * (1.0 - s))


def _rmsnorm_fwd(x, g, name):
    S, D = x.shape
    tr = min(512, S)

    def body(x_ref, g_ref, o_ref):
        xv = x_ref[...]
        r = lax.rsqrt(jnp.mean(xv * xv, axis=-1, keepdims=True) + EPS)
        o_ref[...] = (xv * r * g_ref[...]).astype(o_ref.dtype)

    return pl.pallas_call(
        body, grid=(S // tr,),
        in_specs=[pl.BlockSpec((tr, D), lambda i: (i, 0)), pl.BlockSpec((1, D), lambda i: (0, 0))],
        out_specs=pl.BlockSpec((tr, D), lambda i: (i, 0)),
        out_shape=jax.ShapeDtypeStruct((S, D), BF16), name=name, compiler_params=_params("parallel"),
    )(x, g)


def _rmsnorm_bwd(x, dy, g, dres, name):
    S, D = x.shape
    tr = min(512, S)

    def body(x_ref, dy_ref, g_ref, dres_ref, dx_ref, dxb_ref, dg_ref):
        i = pl.program_id(0)
        xv = x_ref[...]
        r = lax.rsqrt(jnp.mean(xv * xv, axis=-1, keepdims=True) + EPS)
        n = xv * r
        dy = dy_ref[...]
        dn = dy * g_ref[...]
        dx = dres_ref[...] + r * (dn - n * jnp.mean(dn * n, axis=-1, keepdims=True))
        dx_ref[...] = dx
        dxb_ref[...] = dx.astype(BF16)
        part = jnp.sum(dy * n, axis=0, keepdims=True)

        @pl.when(i == 0)
        def _():
            dg_ref[...] = part

        @pl.when(i > 0)
        def _():
            dg_ref[...] += part

    row = pl.BlockSpec((tr, D), lambda i: (i, 0))
    vec = pl.BlockSpec((1, D), lambda i: (0, 0))
    return pl.pallas_call(
        body, grid=(S // tr,), in_specs=[row, row, vec, row], out_specs=[row, row, vec],
        out_shape=[jax.ShapeDtypeStruct((S, D), F32), jax.ShapeDtypeStruct((S, D), BF16), jax.ShapeDtypeStruct((1, D), F32)],
        name=name, compiler_params=_params("arbitrary"),
    )(x, dy, g, dres)


def _loss_grad(y, tgt, name):
    S, D = y.shape
    tr = min(512, S)

    def body(y_ref, t_ref, dy_ref, dyb_ref, l_ref):
        i = pl.program_id(0)
        err = y_ref[...] - t_ref[...]
        dy = err * (1.0 / D)
        dy_ref[...] = dy
        dyb_ref[...] = dy.astype(BF16)
        part = jnp.sum(jnp.sum(err * err, axis=1, keepdims=True), axis=0, keepdims=True) * (0.5 / D)

        @pl.when(i == 0)
        def _():
            l_ref[...] = jnp.zeros_like(l_ref)

        l_ref[...] += part

    row = pl.BlockSpec((tr, D), lambda i: (i, 0))
    return pl.pallas_call(
        body, grid=(S // tr,), in_specs=[row, row],
        out_specs=[row, row, pl.BlockSpec((8, 128), lambda i: (0, 0))],
        out_shape=[jax.ShapeDtypeStruct((S, D), F32), jax.ShapeDtypeStruct((S, D), BF16), jax.ShapeDtypeStruct((8, 128), F32)],
        name=name, compiler_params=_params("arbitrary"),
    )(y, tgt)


def _pick(n, cands):
    for c in cands:
        if n % c == 0:
            return c
    return n


def _mm(a, b, mode, name, out_dtype=F32, residual=None, tm=None, tn=None, tk=None, after=None):
    if mode == "nn":
        (M, K), (_, N) = a.shape, b.shape
    elif mode == "nt":
        (M, K), (N, _) = a.shape, b.shape
    else:
        (K, M), (_, N) = a.shape, b.shape
    tm = min(tm, M) if tm else _pick(M, (1024, 512, 256, 128))
    tn = min(tn, N) if tn else _pick(N, (1024, 1152, 1408, 512, 256, 128))
    tk = min(tk, K) if tk else _pick(K, (2048, 1920, 1408, 1024, 512, 256, 128))
    assert M % tm == 0 and N % tn == 0 and K % tk == 0, (name, M, N, K, tm, tn, tk)
    nk = K // tk
    a_spec = pl.BlockSpec((tk, tm), lambda i, j, k: (k, i)) if mode == "tn" else pl.BlockSpec((tm, tk), lambda i, j, k: (i, k))
    b_spec = pl.BlockSpec((tn, tk), lambda i, j, k: (j, k)) if mode == "nt" else pl.BlockSpec((tk, tn), lambda i, j, k: (k, j))
    o_spec = pl.BlockSpec((tm, tn), lambda i, j, k: (i, j))
    dims = {"nn": NN, "nt": NT, "tn": TN}[mode]
    has_res = residual is not None

    has_after = after is not None

    def body(*refs):
        a_ref, b_ref = refs[0], refs[1]
        r_ref = refs[2] if has_res else None
        o_ref = refs[2 + has_res + has_after]
        p = _bdot(a_ref[...], b_ref[...], dims)

        def finish(acc):
            if has_res:
                acc = acc + r_ref[...]
            o_ref[...] = acc.astype(o_ref.dtype)

        if nk == 1:
            finish(p)
        else:
            acc_ref = refs[-1]
            k = pl.program_id(2)

            @pl.when(k == 0)
            def _():
                acc_ref[...] = p

            @pl.when(k > 0)
            def _():
                acc_ref[...] += p

            @pl.when(k == nk - 1)
            def _():
                finish(acc_ref[...])

    ins = [a, b] + ([residual] if has_res else []) + ([after] if has_after else [])
    in_specs = [a_spec, b_spec] + ([o_spec] if has_res else []) + ([pl.BlockSpec(memory_space=pl.ANY)] if has_after else [])
    scratch = [pltpu.VMEM((tm, tn), F32)] if nk > 1 else []
    return pl.pallas_call(
        body, grid=(M // tm, N // tn, nk), in_specs=in_specs, out_specs=o_spec,
        out_shape=jax.ShapeDtypeStruct((M, N), out_dtype), scratch_shapes=scratch, name=name,
        compiler_params=_params("parallel", "parallel", "arbitrary"),
    )(*ins)


def _accumulate(acc_ref, p, k, nk, finish):
    @pl.when(k == 0)
    def _():
        acc_ref[...] = p

    @pl.when(k > 0)
    def _():
        acc_ref[...] += p

    @pl.when(k == nk - 1)
    def _():
        finish(acc_ref[...])


def _swiglu_fwd(hn, wg, wu, name):
    S, K = hn.shape
    F = wg.shape[1]
    tm = _pick(S, (1024, 512))
    tn = _pick(F, (512, 256, 128))

    def body(a_ref, wg_ref, wu_ref, g_ref, u_ref, act_ref):
        a = a_ref[...]
        g = _bdot(a, wg_ref[...], NN)
        u = _bdot(a, wu_ref[...], NN)
        g_ref[...] = g.astype(BF16)
        u_ref[...] = u.astype(BF16)
        act_ref[...] = (g * _sigmoid(g) * u).astype(BF16)

    w_spec = pl.BlockSpec((K, tn), lambda i, j: (0, j))
    o_spec = pl.BlockSpec((tm, tn), lambda i, j: (i, j))
    sh = jax.ShapeDtypeStruct((S, F), BF16)
    return pl.pallas_call(
        body, grid=(S // tm, F // tn), in_specs=[pl.BlockSpec((tm, K), lambda i, j: (i, 0)), w_spec, w_spec],
        out_specs=[o_spec, o_spec, o_spec], out_shape=[sh, sh, sh], name=name,
        compiler_params=_params("parallel", "parallel"),
    )(hn, wg, wu)


def _swiglu_bwd(dx, wd, g, u, name):
    S, K = dx.shape
    F = wd.shape[0]
    tm = _pick(S, (1024, 512))
    tn = _pick(F, (512, 256, 128))

    def body(dx_ref, wd_ref, g_ref, u_ref, dg_ref, du_ref):
        da = _bdot(dx_ref[...], wd_ref[...], NT)
        gv = g_ref[...].astype(F32)
        uv = u_ref[...].astype(F32)
        s = _sigmoid(gv)
        dg_ref[...] = (da * uv * _silu_grad(gv, s)).astype(BF16)
        du_ref[...] = (da * gv * s).astype(BF16)

    o_spec = pl.BlockSpec((tm, tn), lambda i, j: (i, j))
    sh = jax.ShapeDtypeStruct((S, F), BF16)
    return pl.pallas_call(
        body, grid=(S // tm, F // tn),
        in_specs=[pl.BlockSpec((tm, K), lambda i, j: (i, 0)), pl.BlockSpec((tn, K), lambda i, j: (j, 0)), o_spec, o_spec],
        out_specs=[o_spec, o_spec], out_shape=[sh, sh], name=name, compiler_params=_params("parallel", "parallel"),
    )(dx, wd, g, u)


def _mm_nt2(a1, b1, a2, b2, name):
    M, K = a1.shape
    N = b1.shape[0]
    tm = _pick(M, (512,))
    tn = _pick(N, (1024, 512))
    tk = _pick(K, (2816, 1024, 512, 256, 128))
    nk = K // tk

    def body(a1_ref, b1_ref, a2_ref, b2_ref, o_ref, acc_ref):
        def finish(acc):
            o_ref[...] = acc

        p = _bdot(a1_ref[...], b1_ref[...], NT) + _bdot(a2_ref[...], b2_ref[...], NT)
        _accumulate(acc_ref, p, pl.program_id(2), nk, finish)

    a_spec = pl.BlockSpec((tm, tk), lambda i, j, k: (i, k))
    b_spec = pl.BlockSpec((tn, tk), lambda i, j, k: (j, k))
    return pl.pallas_call(
        body, grid=(M // tm, N // tn, nk), in_specs=[a_spec, b_spec, a_spec, b_spec],
        out_specs=pl.BlockSpec((tm, tn), lambda i, j, k: (i, j)), out_shape=jax.ShapeDtypeStruct((M, N), F32),
        scratch_shapes=[pltpu.VMEM((tm, tn), F32)], name=name,
        compiler_params=_params("parallel", "parallel", "arbitrary"),
    )(a1, b1, a2, b2)


XBC_BLK0 = XBC_COL // 128


def _conv_fwd(proj, w, b, name):
    S = proj.shape[0]
    T = min(512, S)

    def body(x_ref, w_ref, b_ref, o_ref, xp_ref):
        xp_ref[pl.ds(0, 8), :] = jnp.zeros((8, 128), F32)
        xp_ref[pl.ds(8, S), :] = x_ref[...]
        wv = w_ref[...]
        bv = b_ref[...]

        def step(c, carry):
            base = pl.multiple_of(c * T, T)
            acc = wv[0:1] * xp_ref[pl.ds(base + 5, T), :]
            for i in range(1, SSD_CONV):
                acc = acc + wv[i:i + 1] * xp_ref[pl.ds(base + 5 + i, T), :]
            acc = bv + acc
            o_ref[pl.ds(base, T), :] = acc * _sigmoid(acc)
            return carry

        lax.fori_loop(0, S // T, step, 0)

    return pl.pallas_call(
        body, grid=(SSD_CONV_CH // 128,),
        in_specs=[pl.BlockSpec((S, 128), lambda j: (0, XBC_BLK0 + j)), pl.BlockSpec((SSD_CONV, 128), lambda j: (0, j)),
                  pl.BlockSpec((1, 128), lambda j: (0, j))],
        out_specs=pl.BlockSpec((S, 128), lambda j: (0, j)),
        out_shape=jax.ShapeDtypeStruct((S, SSD_CONV_CH), F32),
        scratch_shapes=[pltpu.VMEM((S + 8, 128), F32)], name=name, compiler_params=_params("parallel"),
    )(proj, w, b)


def _conv_bwd(dxs, dbm, dcm, proj, w, b, dproj, name):
    S = proj.shape[0]
    T = min(512, S)
    NX, NB = SSD_WIDTH // 128, SSD_GROUPS * SSD_STATE // 128

    def body(dxs_ref, dbm_ref, dcm_ref, x_ref, w_ref, b_ref, dproj_ref, dx_ref, dw_ref, db_ref, xp_ref, dcp_ref):
        j = pl.program_id(0)

        @pl.when(j < NX)
        def _():
            dcp_ref[pl.ds(0, S), :] = dxs_ref[...]

        @pl.when((j >= NX) & (j < NX + NB))
        def _():
            dcp_ref[pl.ds(0, S), :] = dbm_ref[...]

        @pl.when(j >= NX + NB)
        def _():
            dcp_ref[pl.ds(0, S), :] = dcm_ref[...]

        da_ref = dcp_ref
        xp_ref[pl.ds(0, 8), :] = jnp.zeros((8, 128), F32)
        xp_ref[pl.ds(8, S), :] = x_ref[...]
        dcp_ref[pl.ds(S, 8), :] = jnp.zeros((8, 128), F32)
        wv = w_ref[...]
        bv = b_ref[...]

        def step1(c, carry):
            base = pl.multiple_of(c * T, T)
            xs = [xp_ref[pl.ds(base + 5 + i, T), :] for i in range(SSD_CONV)]
            acc = wv[0:1] * xs[0]
            for i in range(1, SSD_CONV):
                acc = acc + wv[i:i + 1] * xs[i]
            acc = bv + acc
            s = _sigmoid(acc)
            dc = da_ref[pl.ds(base, T), :] * _silu_grad(acc, s)
            dcp_ref[pl.ds(base, T), :] = dc
            new = tuple(carry[i] + jnp.sum(xs[i] * dc, axis=0, keepdims=True) for i in range(SSD_CONV))
            return new + (carry[SSD_CONV] + jnp.sum(dc, axis=0, keepdims=True),)

        z = jnp.zeros((1, 128), F32)
        res = lax.fori_loop(0, S // T, step1, (z,) * (SSD_CONV + 1))
        for i in range(SSD_CONV):
            dw_ref[pl.ds(i, 1), :] = res[i]
        db_ref[...] = res[SSD_CONV]

        def step2(c, carry):
            base = pl.multiple_of(c * T, T)
            acc = wv[0:1] * dcp_ref[pl.ds(base + 3, T), :]
            for i in range(1, SSD_CONV):
                acc = acc + wv[i:i + 1] * dcp_ref[pl.ds(base + 3 - i, T), :]
            dx_ref[pl.ds(base, T), :] = acc.astype(dx_ref.dtype)
            return carry

        lax.fori_loop(0, S // T, step2, 0)

    clamp = lambda j, lo, n: jnp.clip(j - lo, 0, n - 1)
    return pl.pallas_call(
        body, grid=(SSD_CONV_CH // 128,),
        in_specs=[pl.BlockSpec((S, 128), lambda j: (0, clamp(j, 0, NX))), pl.BlockSpec((S, 128), lambda j: (0, clamp(j, NX, NB))),
                  pl.BlockSpec((S, 128), lambda j: (0, clamp(j, NX + NB, NB))),
                  pl.BlockSpec((S, 128), lambda j: (0, XBC_BLK0 + j)), pl.BlockSpec((SSD_CONV, 128), lambda j: (0, j)),
                  pl.BlockSpec((1, 128), lambda j: (0, j)), pl.BlockSpec(memory_space=pl.ANY)],
        out_specs=[pl.BlockSpec((S, 128), lambda j: (0, XBC_BLK0 + j)), pl.BlockSpec((SSD_CONV, 128), lambda j: (0, j)),
                   pl.BlockSpec((1, 128), lambda j: (0, j))],
        out_shape=[jax.ShapeDtypeStruct(dproj.shape, dproj.dtype), jax.ShapeDtypeStruct((SSD_CONV, SSD_CONV_CH), F32),
                   jax.ShapeDtypeStruct((1, SSD_CONV_CH), F32)],
        input_output_aliases={6: 0},
        scratch_shapes=[pltpu.VMEM((S + 8, 128), F32), pltpu.VMEM((S + 8, 128), F32)], name=name,
        compiler_params=_params("arbitrary"),
    )(dxs, dbm, dcm, proj, w, b, dproj)


HPG = SSD_HEADS // SSD_GROUPS
GW = HPG * SSD_HEAD_DIM


def _ssd_chunk_terms(dtr, bias, alog, tril, triu):
    pre = dtr + bias
    dt = jnp.maximum(pre, 0.0) + jnp.log(1.0 + jnp.exp(-jnp.abs(pre)))
    a_neg = -jnp.exp(alog)
    a = dt * a_neg
    acum = _xdot(tril, a, NN, exact_first=True)
    acum_t = _xdot(a, triu, TN)
    return pre, dt, a_neg, acum, acum_t


def _head_expanders():
    h64 = lax.broadcasted_iota(jnp.int32, (HPG, GW), 0) == lax.broadcasted_iota(jnp.int32, (HPG, GW), 1) // SSD_HEAD_DIM
    h128 = lax.broadcasted_iota(jnp.int32, (HPG, HPG * CHUNK), 0) == lax.broadcasted_iota(jnp.int32, (HPG, HPG * CHUNK), 1) // CHUNK
    return h64.astype(F32), h128.astype(F32)


def _ssd_fwd(xbc, proj, dtr, dt_bias, a_log, d_rep, gain, name):
    S = xbc.shape[0]
    L = CHUNK
    T = min(512, S)
    CPS = T // L
    NC = S // L

    def body(x_ref, b_ref, c_ref, z_ref, dtr_ref, bias_ref, alog_ref, d_ref, gain_ref, y_ref, yraw_ref, st_ref, state):
        i = pl.program_id(1)

        @pl.when(i == 0)
        def _():
            state[...] = jnp.zeros_like(state)

        row = lax.broadcasted_iota(jnp.int32, (L, L), 0)
        col = lax.broadcasted_iota(jnp.int32, (L, L), 1)
        causal = row >= col
        tril = causal.astype(F32)
        triu = (row <= col).astype(F32)
        low = col < SSD_HEAD_DIM
        e64, e128 = _head_expanders()
        for c in range(CPS):
            rows = pl.ds(c * L, L)
            xv = x_ref[rows, :]
            bm = b_ref[rows, :]
            cm = c_ref[rows, :]
            _, dt, _, acum, acum_t = _ssd_chunk_terms(dtr_ref[rows, :], bias_ref[...], alog_ref[...], tril, triu)
            ac = _xdot(acum, e64, NN)
            ac_sq = _xdot(acum, e128, NN)
            xd = xv * _xdot(dt, e64, NN, pieces=2)
            ac_last = ac[L - 1:L, :]
            sp = state[...]
            st_ref[c] = sp
            yoff = _bdot(cm, sp, NN) * jnp.exp(ac)
            state[...] = sp * jnp.exp(ac_last) + _bdot(bm, xd * jnp.exp(ac_last - ac), TN)
            gmat = _bdot(cm, bm, NT)
            for q in range(HPG // 2):
                pair = slice(q * 128, (q + 1) * 128)
                tile = xd[:, pair]
                y = yoff[:, pair]
                for j, keep in ((2 * q, low), (2 * q + 1, ~low)):
                    lam = jnp.exp(jnp.where(causal, ac_sq[:, j * L:(j + 1) * L] - acum_t[j:j + 1, :], NEG))
                    y = y + _bdot(gmat * lam, jnp.where(keep, tile, 0.0), NN)
                yraw_ref[rows, pair] = y
            zz = z_ref[rows, :]
            u = (yraw_ref[rows, :] + xv * d_ref[...]) * (zz * _sigmoid(zz))
            r = lax.rsqrt(jnp.mean(u * u, axis=-1, keepdims=True) + EPS)
            y_ref[rows, :] = (u * r * gain_ref[...]).astype(y_ref.dtype)

    vec8 = pl.BlockSpec((None, 1, HPG), lambda g, i: (g, 0, 0))
    return pl.pallas_call(
        body, grid=(SSD_GROUPS, S // T),
        in_specs=[pl.BlockSpec((T, GW), lambda g, i: (i, g)),
                  pl.BlockSpec((T, SSD_STATE), lambda g, i: (i, SSD_WIDTH // SSD_STATE + g)),
                  pl.BlockSpec((T, SSD_STATE), lambda g, i: (i, SSD_WIDTH // SSD_STATE + SSD_GROUPS + g)),
                  pl.BlockSpec((T, GW), lambda g, i: (i, Z_COL // GW + g)),
                  pl.BlockSpec((None, T, HPG), lambda g, i: (g, i, 0)),
                  vec8, vec8,
                  pl.BlockSpec((1, GW), lambda g, i: (0, g)), pl.BlockSpec((1, GW), lambda g, i: (0, g))],
        out_specs=[pl.BlockSpec((T, GW), lambda g, i: (i, g)), pl.BlockSpec((T, GW), lambda g, i: (i, g)),
                   pl.BlockSpec((CPS, None, SSD_STATE, GW), lambda g, i: (i, g, 0, 0))],
        out_shape=[jax.ShapeDtypeStruct((S, MIX_WIDTH), BF16), jax.ShapeDtypeStruct((S, SSD_WIDTH), F32),
                   jax.ShapeDtypeStruct((NC, SSD_GROUPS, SSD_STATE, GW), F32)],
        scratch_shapes=[pltpu.VMEM((SSD_STATE, GW), F32)], name=name,
        compiler_params=_params("arbitrary", "arbitrary"),
    )(xbc, xbc, xbc, proj, dtr, dt_bias, a_log, d_rep, gain)


def _ssd_bwd(dy, yraw, xbc, proj, dtr, dt_bias, a_log, d_rep, gain, states, name):
    S = xbc.shape[0]
    L = CHUNK
    T = min(512, S)
    CPS = T // L
    NI = S // T

    def body(dy_ref, yraw_ref, x_ref, b_ref, c_ref, z_ref, dtr_ref, bias_ref, alog_ref, d_ref, gain_ref, st_ref,
             dz_ref, dx_ref, db_ref, dc_ref, ddtr_ref, dbias_ref, dalog_ref, dd_ref, dgain_ref, dstate, dxd_ref):
        i = pl.program_id(1)

        @pl.when(i == 0)
        def _():
            dstate[...] = jnp.zeros_like(dstate)
            dbias_ref[...] = jnp.zeros_like(dbias_ref)
            dalog_ref[...] = jnp.zeros_like(dalog_ref)
            dd_ref[...] = jnp.zeros_like(dd_ref)
            dgain_ref[...] = jnp.zeros_like(dgain_ref)

        row = lax.broadcasted_iota(jnp.int32, (L, L), 0)
        col = lax.broadcasted_iota(jnp.int32, (L, L), 1)
        causal = row >= col
        tril = causal.astype(F32)
        triu = (row <= col).astype(F32)
        low = col < SSD_HEAD_DIM
        e64, e128 = _head_expanders()
        lane8 = lax.broadcasted_iota(jnp.int32, (1, HPG), 1)
        sub8 = lax.broadcasted_iota(jnp.int32, (HPG, 1), 0)
        eye8 = (lax.broadcasted_iota(jnp.int32, (HPG, HPG), 0) == lax.broadcasted_iota(jnp.int32, (HPG, HPG), 1)).astype(F32)
        last_row = (lax.broadcasted_iota(jnp.int32, (L, 1), 0) == L - 1).astype(F32)
        for c in reversed(range(CPS)):
            rows = pl.ds(c * L, L)
            xv = x_ref[rows, :]
            bm = b_ref[rows, :]
            cm = c_ref[rows, :]
            zz = z_ref[rows, :]
            dvec = d_ref[...]
            sz = _sigmoid(zz)
            silu_z = zz * sz
            v = yraw_ref[rows, :] + xv * dvec
            u = v * silu_z
            r = lax.rsqrt(jnp.mean(u * u, axis=-1, keepdims=True) + EPS)
            n = u * r
            do = dy_ref[rows, :]
            dgain_ref[...] += jnp.sum(do * n, axis=0, keepdims=True)
            dn = do * gain_ref[...]
            du = r * (dn - n * jnp.mean(dn * n, axis=-1, keepdims=True))
            dz_ref[rows, :] = (du * v * _silu_grad(zz, sz)).astype(dz_ref.dtype)
            dyv = du * silu_z
            dd_ref[...] += _xdot(jnp.sum(dyv * xv, axis=0, keepdims=True), e64, NT, pieces=2)
            pre, dt, a_neg, acum, acum_t = _ssd_chunk_terms(dtr_ref[rows, :], bias_ref[...], alog_ref[...], tril, triu)
            ac = _xdot(acum, e64, NN)
            ac_sq = _xdot(acum, e128, NN)
            dt_w = _xdot(dt, e64, NN, pieces=2)
            xd = xv * dt_w
            ac_last = ac[L - 1:L, :]
            ea = jnp.exp(ac)
            w = jnp.exp(ac_last - ac)
            ea_last = jnp.exp(ac_last)
            sp = st_ref[c]
            ds = dstate[...]
            dye = dyv * ea
            yoff = _bdot(cm, sp, NN) * ea
            bds = _bdot(bm, ds, NN)
            dcm = _bdot(dye, sp, NT)
            dbm = _bdot(xd * w, ds, NT)
            dstate[...] = ds * ea_last + _bdot(cm, dye, TN)
            w8 = jnp.exp(acum[L - 1:L, :] - acum)
            dw8 = _xdot(xd * bds, e64, NT, pieces=2)
            dac8 = _xdot(dyv * yoff, e64, NT, pieces=2) - dw8 * w8
            tail8 = jnp.sum(dw8 * w8, axis=0, keepdims=True) + jnp.exp(acum[L - 1:L, :]) * _xdot(
                jnp.sum(ds * sp, axis=0, keepdims=True), e64, NT, pieces=2)
            dac8 = dac8 + last_row * tail8
            gmat = _bdot(cm, bm, NT)
            dgmat = jnp.zeros((L, L), F32)
            colsum_t = jnp.zeros((HPG, L), F32)
            for q in range(HPG // 2):
                pair = slice(q * 128, (q + 1) * 128)
                xd_tile = xd[:, pair]
                dy_tile = dyv[:, pair]
                dxd_tile = bds[:, pair] * w[:, pair]
                for j, keep in ((2 * q, low), (2 * q + 1, ~low)):
                    lam = jnp.exp(jnp.where(causal, ac_sq[:, j * L:(j + 1) * L] - acum_t[j:j + 1, :], NEG))
                    mh = gmat * lam
                    dyj = jnp.where(keep, dy_tile, 0.0)
                    dxd_tile = dxd_tile + _bdot(mh, dyj, TN)
                    dm = _bdot(dyj, xd_tile, NT)
                    dgmat = dgmat + dm * lam
                    qm = dm * mh
                    dac8 = dac8 + jnp.sum(qm, axis=1, keepdims=True) * (lane8 == j).astype(F32)
                    colsum_t = colsum_t + (sub8 == j).astype(F32) * jnp.sum(qm, axis=0, keepdims=True)
                dxd_ref[:, pair] = dxd_tile
            dac8 = dac8 - _xdot(colsum_t, eye8, TN)
            dxd = dxd_ref[...]
            dx_ref[rows, :] = dxd * dt_w + dyv * dvec
            dc_ref[rows, :] = dcm + _bdot(dgmat, bm, NN)
            db_ref[rows, :] = dbm + _bdot(dgmat, cm, TN)
            da8 = _xdot(triu, dac8, NN, exact_first=True)
            ddt8 = _xdot(dxd * xv, e64, NT, pieces=2) + da8 * a_neg
            dalog_ref[...] += jnp.sum(da8 * dt, axis=0, keepdims=True) * a_neg
            dpre = ddt8 * _sigmoid(pre)
            ddtr_ref[rows, :] = dpre
            dbias_ref[...] += jnp.sum(dpre, axis=0, keepdims=True)

    rev = lambda i: NI - 1 - i
    vec8 = pl.BlockSpec((None, 1, HPG), lambda g, i: (g, 0, 0))
    grp = pl.BlockSpec((T, GW), lambda g, i: (rev(i), g))
    bspec = pl.BlockSpec((T, SSD_STATE), lambda g, i: (rev(i), SSD_WIDTH // SSD_STATE + g))
    cspec = pl.BlockSpec((T, SSD_STATE), lambda g, i: (rev(i), SSD_WIDTH // SSD_STATE + SSD_GROUPS + g))
    gvec = pl.BlockSpec((1, GW), lambda g, i: (0, g))
    st_spec = pl.BlockSpec((CPS, None, SSD_STATE, GW), lambda g, i: (rev(i), g, 0, 0))
    small = jax.ShapeDtypeStruct((SSD_GROUPS, 1, HPG), F32)
    zspec = pl.BlockSpec((T, GW), lambda g, i: (rev(i), Z_COL // GW + g))
    return pl.pallas_call(
        body, grid=(SSD_GROUPS, NI),
        in_specs=[grp, grp, grp, bspec, cspec, zspec, pl.BlockSpec((None, T, HPG), lambda g, i: (g, rev(i), 0)),
                  vec8, vec8, gvec, gvec, st_spec],
        out_specs=[zspec, grp, pl.BlockSpec((T, SSD_STATE), lambda g, i: (rev(i), g)),
                   pl.BlockSpec((T, SSD_STATE), lambda g, i: (rev(i), g)),
                   pl.BlockSpec((None, T, HPG), lambda g, i: (g, rev(i), 0)), vec8, vec8, vec8, gvec],
        out_shape=[jax.ShapeDtypeStruct((S, IN_PAD), BF16), jax.ShapeDtypeStruct((S, SSD_WIDTH), F32),
                   jax.ShapeDtypeStruct((S, SSD_GROUPS * SSD_STATE), F32), jax.ShapeDtypeStruct((S, SSD_GROUPS * SSD_STATE), F32),
                   jax.ShapeDtypeStruct((SSD_GROUPS, S, HPG), F32), small, small, small,
                   jax.ShapeDtypeStruct((1, SSD_WIDTH), F32)],
        scratch_shapes=[pltpu.VMEM((SSD_STATE, GW), F32), pltpu.VMEM((L, GW), F32)],
        name=name, compiler_params=_params("arbitrary", "arbitrary"),
    )(dy, yraw, xbc, xbc, xbc, proj, dtr, dt_bias, a_log, d_rep, gain, states)


def _swap_halves(t):
    w = t.shape[1]
    lane = lax.broadcasted_iota(jnp.int32, t.shape, 1)
    return jnp.where((lane % 64) < 32, pltpu.roll(t, w - 32, axis=1), pltpu.roll(t, 32, axis=1))


def _widen(tab, w):
    return tab if w == 128 else jnp.concatenate([tab] * (w // 128), axis=1)


def _rope(t, cos, sin_signed):
    return t * cos + _swap_halves(t) * sin_signed


def _rope_t(d, cos, sin_signed):
    return d * cos - _swap_halves(d) * sin_signed


def _group_sum64(v, bd):
    hi = v.astype(BF16)
    lo = (v - hi.astype(F32)).astype(BF16)
    return (lax.dot_general(hi, bd, NN, preferred_element_type=F32)
            + lax.dot_general(lo, bd, NN, preferred_element_type=F32))


AQ_BLK = ATT_COL // ATT_WIDTH


def _att_prep_fwd(proj, qg, kg, cos, sin, bd, name):
    S = proj.shape[0]
    T = min(512, S)
    PB = ATT_SPAN // T
    src = lambda i: jnp.maximum(i - PB, 0)

    def body(q_ref, k_ref, v_ref, qg_ref, kg_ref, cos_ref, sin_ref, bd_ref, qo_ref, ko_ref, vo_ref):
        i = pl.program_id(0)

        @pl.when(i < PB)
        def _():
            ko_ref[...] = jnp.zeros_like(ko_ref)
            vo_ref[...] = jnp.zeros_like(vo_ref)

        @pl.when(i >= PB)
        def _():
            cw = _widen(cos_ref[...], ATT_WIDTH)
            sw = _widen(sin_ref[...], ATT_WIDTH)
            bdv = bd_ref[...]

            def norm_rope(t, gain):
                ss = _group_sum64(t * t, bdv)
                return _rope(t * lax.rsqrt(ss * (1.0 / ATT_HEAD_DIM) + EPS) * gain, cw, sw)

            qo_ref[...] = (norm_rope(q_ref[...], qg_ref[...]) * (ATT_HEAD_DIM ** -0.5)).astype(BF16)
            kt = norm_rope(k_ref[...], kg_ref[...]).astype(BF16)
            vt = v_ref[...].astype(BF16)
            for pr in range(ATT_HEADS // 2):
                ko_ref[pr] = kt[:, pr * 128:(pr + 1) * 128]
                vo_ref[pr] = vt[:, pr * 128:(pr + 1) * 128]

    vec = pl.BlockSpec((1, ATT_WIDTH), lambda i: (0, 0))
    tab = pl.BlockSpec((T, 128), lambda i: (src(i), 0))
    hm = pl.BlockSpec((ATT_HEADS // 2, T, 128), lambda i: (0, i, 0))
    hm_shape = jax.ShapeDtypeStruct((ATT_HEADS // 2, ATT_SPAN + S, 128), BF16)
    return pl.pallas_call(
        body, grid=(PB + S // T,),
        in_specs=[pl.BlockSpec((T, ATT_WIDTH), lambda i: (src(i), AQ_BLK)), pl.BlockSpec((T, ATT_WIDTH), lambda i: (src(i), AQ_BLK + 1)),
                  pl.BlockSpec((T, ATT_WIDTH), lambda i: (src(i), AQ_BLK + 2)), vec, vec, tab, tab,
                  pl.BlockSpec((ATT_WIDTH, ATT_WIDTH), lambda i: (0, 0))],
        out_specs=[pl.BlockSpec((T, ATT_WIDTH), lambda i: (src(i), 0)), hm, hm],
        out_shape=[jax.ShapeDtypeStruct((S, ATT_WIDTH), BF16), hm_shape, hm_shape],
        name=name, compiler_params=_params("arbitrary"),
    )(proj, proj, proj, qg, kg, cos, sin, bd)


def _att_prep_bwd(proj, dq, dk_p, dv_p, qg, kg, cos, sin, bd, dproj, name):
    S = proj.shape[0]
    T = min(512, S)
    NI = S // T
    PB = ATT_SPAN // T
    W = ATT_WIDTH

    def body(q_ref, k_ref, dq_ref, dkp_ref, dvp_ref, qg_ref, kg_ref, cos_ref, sin_ref, bd_ref, dproj_ref,
             do_ref, dqg_ref, dkg_ref, acc_ref):
        i = pl.program_id(0)

        @pl.when(i == 0)
        def _():
            acc_ref[...] = jnp.zeros_like(acc_ref)

        npair = ATT_HEADS // 2
        dk_all = jnp.concatenate([dkp_ref[pr].T for pr in range(npair)], axis=1)
        do_ref[:, 2 * W:3 * W] = jnp.concatenate([dvp_ref[pr].T for pr in range(npair)], axis=1).astype(BF16)
        cw = _widen(cos_ref[...], ATT_WIDTH)
        sw = _widen(sin_ref[...], ATT_WIDTH)
        bdv = bd_ref[...]

        def one(t, d_rot, gain, scale, slot):
            ss = _group_sum64(t * t, bdv)
            r = lax.rsqrt(ss * (1.0 / ATT_HEAD_DIM) + EPS)
            n = t * r
            d_ng = _rope_t(d_rot * scale, cw, sw)
            acc_ref[pl.ds(slot, 1), :] += jnp.sum(d_ng * n, axis=0, keepdims=True)
            dn = d_ng * gain
            return r * (dn - n * (_group_sum64(dn * n, bdv) * (1.0 / ATT_HEAD_DIM)))

        do_ref[:, 0:W] = one(q_ref[...], dq_ref[...], qg_ref[...], ATT_HEAD_DIM ** -0.5, 0).astype(BF16)
        do_ref[:, W:2 * W] = one(k_ref[...], dk_all, kg_ref[...], 1.0, 1).astype(BF16)

        @pl.when(i == NI - 1)
        def _():
            a = acc_ref[...]
            f = a[:, 0:64]
            for h in range(1, ATT_HEADS):
                f = f + a[:, h * 64:(h + 1) * 64]
            dqg_ref[...] = f[0:1]
            dkg_ref[...] = f[1:2]

    vec = pl.BlockSpec((1, ATT_WIDTH), lambda i: (0, 0))
    tab = pl.BlockSpec((T, 128), lambda i: (i, 0))
    row = pl.BlockSpec((T, ATT_WIDTH), lambda i: (i, 0))
    g64 = pl.BlockSpec((1, ATT_HEAD_DIM), lambda i: (0, 0))
    padded = pl.BlockSpec((ATT_HEADS // 2, 128, T), lambda i: (0, 0, i + PB))
    return pl.pallas_call(
        body, grid=(NI,),
        in_specs=[pl.BlockSpec((T, ATT_WIDTH), lambda i: (i, AQ_BLK)), pl.BlockSpec((T, ATT_WIDTH), lambda i: (i, AQ_BLK + 1)),
                  row, padded, padded, vec, vec, tab, tab, pl.BlockSpec((ATT_WIDTH, ATT_WIDTH), lambda i: (0, 0)),
                  pl.BlockSpec(memory_space=pl.ANY)],
        out_specs=[pl.BlockSpec((T, 3 * W), lambda i: (i, ATT_COL // (3 * W))), g64, g64],
        out_shape=[jax.ShapeDtypeStruct(dproj.shape, dproj.dtype), jax.ShapeDtypeStruct((1, ATT_HEAD_DIM), F32),
                   jax.ShapeDtypeStruct((1, ATT_HEAD_DIM), F32)],
        input_output_aliases={10: 0},
        scratch_shapes=[pltpu.VMEM((8, ATT_WIDTH), F32)], name=name, compiler_params=_params("arbitrary"),
    )(proj, proj, dq, dk_p, dv_p, qg, kg, cos, sin, bd, dproj)


def _att_bias():
    qpos = np.arange(CHUNK)[:, None] + ATT_SPAN
    kpos = np.arange(ATT_STRIP)[None, :]
    rel = qpos - kpos
    mult = np.zeros((CHUNK, ATT_STRIP), np.float64)
    for window, dil in DILATED_PAIRS:
        mult += (rel >= 0) & (rel % dil == 0) & (rel // dil <= window // dil)
    return np.where(mult > 0, np.log(np.maximum(mult, 1.0)), NEG).astype(np.float32)


def _att_scores(q, ks, bias, i):
    s = _bdot(q, ks, NT) + bias
    kcol = lax.broadcasted_iota(jnp.int32, (1, ATT_STRIP), 1) + i * CHUNK
    return jnp.where(kcol >= ATT_SPAN, s, NEG)


def _pair_masks():
    low = lax.broadcasted_iota(jnp.int32, (CHUNK, 128), 1) < ATT_HEAD_DIM
    return low, ~low


def _att_fwd(q, kp, vp, bias, y, name):
    S = q.shape[0]
    SP = kp.shape[1]

    QB = min(ATT_QB, S // CHUNK)
    TQ = QB * CHUNK

    def body(q_ref, k_ref, v_ref, bias_ref, y_ref, o_ref):
        i = pl.program_id(1)
        for b in range(QB):
            blk = i * QB + b
            strip = pl.ds(pl.multiple_of(blk * CHUNK, CHUNK), ATT_STRIP)
            rows = pl.ds(b * CHUNK, CHUNK)
            qv = q_ref[rows, :]
            ks = k_ref[strip, :]
            vs = v_ref[strip, :]
            outs = []
            for keep in _pair_masks():
                s = _att_scores(jnp.where(keep, qv, jnp.zeros_like(qv)), ks, bias_ref[...], blk)
                m = jnp.max(s, axis=-1, keepdims=True)
                p = jnp.exp(s - m)
                den = jnp.sum(p, axis=-1, keepdims=True)
                outs.append(_bdot(p, vs, NN) / den)
            o_ref[rows, :] = jnp.where(_pair_masks()[0], outs[0], outs[1]).astype(o_ref.dtype)

    kv = pl.BlockSpec((None, SP, 128), lambda hp, i: (hp, 0, 0))
    return pl.pallas_call(
        body, grid=(ATT_HEADS // 2, S // TQ),
        in_specs=[pl.BlockSpec((TQ, 128), lambda hp, i: (i, hp)), kv, kv,
                  pl.BlockSpec((CHUNK, ATT_STRIP), lambda hp, i: (0, 0)), pl.BlockSpec(memory_space=pl.ANY)],
        out_specs=pl.BlockSpec((TQ, 128), lambda hp, i: (i, SSD_WIDTH // 128 + hp)),
        out_shape=jax.ShapeDtypeStruct(y.shape, y.dtype), input_output_aliases={4: 0}, name=name,
        compiler_params=_params("parallel", "arbitrary"),
    )(q, kp, vp, bias, y)


def _att_bwd(q, kp, vp, bias, dy, name):
    S = q.shape[0]
    SP = kp.shape[1]
    QB = min(ATT_QB, S // CHUNK)

    def body(q_ref, k_ref, v_ref, bias_ref, do_ref, dq_ref, dk_ref, dv_ref):
        i = pl.program_id(1)

        @pl.when(i == 0)
        def _():
            dk_ref[...] = jnp.zeros_like(dk_ref)
            dv_ref[...] = jnp.zeros_like(dv_ref)

        for b in range(QB):
            blk = i * QB + b
            strip = pl.ds(pl.multiple_of(blk * CHUNK, CHUNK), ATT_STRIP)
            rows = pl.ds(b * CHUNK, CHUNK)
            qv = q_ref[rows, :]
            dov = do_ref[rows, :]
            ks = k_ref[strip, :]
            vs = v_ref[strip, :]
            dq = jnp.zeros((CHUNK, 128), F32)
            dk_t = jnp.zeros((128, ATT_STRIP), F32)
            dv_t = jnp.zeros((128, ATT_STRIP), F32)
            for keep in _pair_masks():
                qh = jnp.where(keep, qv, jnp.zeros_like(qv))
                doh = jnp.where(keep, dov, 0.0)
                s = _att_scores(qh, ks, bias_ref[...], blk)
                m = jnp.max(s, axis=-1, keepdims=True)
                p = jnp.exp(s - m)
                p = p / jnp.sum(p, axis=-1, keepdims=True)
                dp = _bdot(doh, vs, NT)
                dsc = p * (dp - jnp.sum(p * dp, axis=-1, keepdims=True))
                dq = dq + jnp.where(keep, _bdot(dsc, ks, NN), 0.0)
                dv_t = dv_t + _bdot(doh, p, TN)
                dk_t = dk_t + _bdot(qh, dsc, TN)
            dq_ref[rows, :] = dq
            dv_ref[:, strip] += dv_t
            dk_ref[:, strip] += dk_t

    TQ = QB * CHUNK
    kv = pl.BlockSpec((None, SP, 128), lambda hp, i: (hp, 0, 0))
    kv_t =pl.BlockSpec((None, 128, SP), lambda hp, i: (hp, 0, 0))
    pairs = jax.ShapeDtypeStruct((ATT_HEADS // 2, 128, SP), F32)
    return pl.pallas_call(
        body, grid=(ATT_HEADS // 2, S // TQ),
        in_specs=[pl.BlockSpec((TQ, 128), lambda hp, i: (i, hp)), kv, kv,
                  pl.BlockSpec((CHUNK, ATT_STRIP), lambda hp, i: (0, 0)),
                  pl.BlockSpec((TQ, 128), lambda hp, i: (i, SSD_WIDTH // 128 + hp))],
        out_specs=[pl.BlockSpec((TQ, 128), lambda hp, i: (i, hp)), kv_t, kv_t],
        out_shape=[jax.ShapeDtypeStruct((S, ATT_WIDTH), F32), pairs, pairs],
        name=name, compiler_params=_params("parallel", "arbitrary"),
    )(q, kp, vp, bias, dy)


RQ_BLK = RET_COL // RET_QK_WIDTH
RV_BLK = (RET_COL + 2 * RET_QK_WIDTH) // RET_V_WIDTH
RET_PAIR = 2 * RET_QK_DIM
RET_LOG_GAMMA = tuple(math.log1p(-2.0 ** (-5.0 - h)) for h in range(RET_HEADS))


def _ret_decays(h):
    L = CHUNK
    lg = RET_LOG_GAMMA[h]
    row = lax.broadcasted_iota(jnp.int32, (L, L), 0)
    col = lax.broadcasted_iota(jnp.int32, (L, L), 1)
    rel = (row - col).astype(F32)
    dm = jnp.where(rel >= 0, jnp.exp(jnp.maximum(rel, 0.0) * lg), 0.0)
    idx = lax.broadcasted_iota(jnp.int32, (L, 1), 0).astype(F32)
    kte = jnp.exp((L - 1 - idx) * lg)
    qfs = jnp.exp((idx + 1.0) * lg)
    return dm, kte, qfs, math.exp(L * lg)


def _ret_head(t, h):
    tile = t[:, (h // 2) * RET_PAIR:(h // 2 + 1) * RET_PAIR]
    low = lax.broadcasted_iota(jnp.int32, tile.shape, 1) < RET_QK_DIM
    return jnp.where(low if h % 2 == 0 else ~low, tile, 0.0)


def _ret_fwd(proj, cos, sin, gain, y, name):
    S = proj.shape[0]
    L = CHUNK
    T = min(512, S)
    CPS = T // L
    NC = S // L

    def body(q_ref, k_ref, v_ref, g_ref, cos_ref, sin_ref, gain_ref, yin_ref, y_ref, o_ref, st_ref, state):
        i = pl.program_id(0)

        @pl.when(i == 0)
        def _():
            state[...] = jnp.zeros_like(state)

        dec = [_ret_decays(h) for h in range(RET_HEADS)]
        for c in range(CPS):
            rows = pl.ds(c * L, L)
            cw = _widen(cos_ref[rows, :], RET_QK_WIDTH)
            sw = _widen(sin_ref[rows, :], RET_QK_WIDTH)
            qv = _rope(q_ref[rows, :], cw, sw)
            kv = _rope(k_ref[rows, :], cw, sw) * (RET_QK_DIM ** -0.5)
            for h in range(RET_HEADS):
                dm, kte, qfs, cd = dec[h]
                qh, kh = _ret_head(qv, h), _ret_head(kv, h)
                vs = slice(h * RET_V_DIM, (h + 1) * RET_V_DIM)
                vh = v_ref[rows, vs]
                sp = state[h]
                st_ref[c, h] = sp
                o = _bdot(_bdot(qh, kh, NT) * dm, vh, NN) + _bdot(qh * qfs, sp, NN)
                state[h] = cd * sp + _bdot(kh * kte, vh, TN)
                o_ref[rows, vs] = o
                gh = g_ref[rows, vs]
                r = lax.rsqrt(jnp.mean(o * o, axis=-1, keepdims=True) + EPS)
                y_ref[rows, vs] = (o * r * gain_ref[:, vs] * (gh * _sigmoid(gh))).astype(y_ref.dtype)

    tab = pl.BlockSpec((T, 128), lambda i: (i, 0))
    wide = pl.BlockSpec((T, RET_V_WIDTH), lambda i: (i, 0))
    return pl.pallas_call(
        body, grid=(S // T,),
        in_specs=[pl.BlockSpec((T, RET_QK_WIDTH), lambda i: (i, RQ_BLK)), pl.BlockSpec((T, RET_QK_WIDTH), lambda i: (i, RQ_BLK + 1)),
                  pl.BlockSpec((T, RET_V_WIDTH), lambda i: (i, RV_BLK)), pl.BlockSpec((T, RET_V_WIDTH), lambda i: (i, RV_BLK + 1)),
                  tab, tab, pl.BlockSpec((1, RET_V_WIDTH), lambda i: (0, 0)), pl.BlockSpec(memory_space=pl.ANY)],
        out_specs=[pl.BlockSpec((T, RET_V_WIDTH), lambda i: (i, (SSD_WIDTH + ATT_WIDTH) // RET_V_WIDTH)), wide,
                   pl.BlockSpec((CPS, RET_HEADS, RET_PAIR, RET_V_DIM), lambda i: (i, 0, 0, 0))],
        out_shape=[jax.ShapeDtypeStruct(y.shape, y.dtype), jax.ShapeDtypeStruct((S, RET_V_WIDTH), F32),
                   jax.ShapeDtypeStruct((NC, RET_HEADS, RET_PAIR, RET_V_DIM), F32)],
        input_output_aliases={7: 0},
        scratch_shapes=[pltpu.VMEM((RET_HEADS, RET_PAIR, RET_V_DIM), F32)], name=name,
        compiler_params=_params("arbitrary"),
    )(proj, proj, proj, proj, cos, sin, gain, y)


def _ret_bwd(dy, oraw, proj, cos, sin, gain, states, dproj, name):
    S = proj.shape[0]
    L = CHUNK
    T = min(512, S)
    CPS = T // L
    NI = S // T
    QW, VW = RET_QK_WIDTH, RET_V_WIDTH
    V0, G0 = 2 * QW, 2 * QW + VW

    def body(dy_ref, o_ref, q_ref, k_ref, v_ref, g_ref, cos_ref, sin_ref, gain_ref, st_ref, dproj_ref,
             out_ref, dgain_ref, dstate, dqs, dks):
        i = pl.program_id(0)

        @pl.when(i == 0)
        def _():
            dstate[...] = jnp.zeros_like(dstate)
            dgain_ref[...] = jnp.zeros_like(dgain_ref)

        dec = [_ret_decays(h) for h in range(RET_HEADS)]
        for c in reversed(range(CPS)):
            rows = pl.ds(c * L, L)
            cw = _widen(cos_ref[rows, :], RET_QK_WIDTH)
            sw = _widen(sin_ref[rows, :], RET_QK_WIDTH)
            qv = _rope(q_ref[rows, :], cw, sw)
            kv = _rope(k_ref[rows, :], cw, sw) * (RET_QK_DIM ** -0.5)
            for h in range(RET_HEADS):
                dm, kte, qfs, cd = dec[h]
                pair = slice((h // 2) * RET_PAIR, (h // 2 + 1) * RET_PAIR)
                vs = slice(h * RET_V_DIM, (h + 1) * RET_V_DIM)
                qh, kh = _ret_head(qv, h), _ret_head(kv, h)
                vh = v_ref[rows, vs]
                gh = g_ref[rows, vs]
                gn = gain_ref[:, vs]
                o = o_ref[rows, vs]
                dyh = dy_ref[rows, vs]
                sg = _sigmoid(gh)
                silu_g = gh * sg
                r = lax.rsqrt(jnp.mean(o * o, axis=-1, keepdims=True) + EPS)
                n = o * r
                dgain_ref[:, vs] += jnp.sum(dyh * n * silu_g, axis=0, keepdims=True)
                out_ref[rows, G0 + h * RET_V_DIM:G0 + (h + 1) * RET_V_DIM] = (dyh * n * gn * _silu_grad(gh, sg)).astype(out_ref.dtype)
                dn = dyh * gn * silu_g
                do = r * (dn - n * jnp.mean(dn * n, axis=-1, keepdims=True))
                sp = st_ref[c, h]
                ds = dstate[h]
                sc = _bdot(qh, kh, NT) * dm
                dsc = _bdot(do, vh, NT) * dm
                out_ref[rows, V0 + h * RET_V_DIM:V0 + (h + 1) * RET_V_DIM] = (_bdot(sc, do, TN) + _bdot(kh * kte, ds, NN)).astype(out_ref.dtype)
                dqh = _bdot(dsc, kh, NN) + _bdot(do, sp, NT) * qfs
                dkh = _bdot(dsc, qh, TN) + _bdot(vh, ds, NT) * kte
                if h % 2 == 0:
                    dqs[:, pair] = dqh
                    dks[:, pair] = dkh
                else:
                    dqs[:, pair] += dqh
                    dks[:, pair] += dkh
                dstate[h] = cd * ds + _bdot(qh * qfs, do, TN)
            out_ref[rows, 0:QW] = _rope_t(dqs[...], cw, sw).astype(out_ref.dtype)
            out_ref[rows, QW:2 * QW] = _rope_t(dks[...] * (RET_QK_DIM ** -0.5), cw, sw).astype(out_ref.dtype)

    rev = lambda i: NI - 1 - i
    tab = pl.BlockSpec((T, 128), lambda i: (rev(i), 0))
    wide = pl.BlockSpec((T, RET_V_WIDTH), lambda i: (rev(i), 0))
    group = pl.BlockSpec((T, G0 + VW), lambda i: (rev(i), RET_COL // (G0 + VW)))
    gvec = pl.BlockSpec((1, RET_V_WIDTH), lambda i: (0, 0))
    return pl.pallas_call(
        body, grid=(NI,),
        in_specs=[pl.BlockSpec((T, RET_V_WIDTH), lambda i: (rev(i), (SSD_WIDTH + ATT_WIDTH) // RET_V_WIDTH)), wide,
                  pl.BlockSpec((T, RET_QK_WIDTH), lambda i: (rev(i), RQ_BLK)), pl.BlockSpec((T, RET_QK_WIDTH), lambda i: (rev(i), RQ_BLK + 1)),
                  pl.BlockSpec((T, RET_V_WIDTH), lambda i: (rev(i), RV_BLK)), pl.BlockSpec((T, RET_V_WIDTH), lambda i: (rev(i), RV_BLK + 1)),
                  tab, tab, gvec,
                  pl.BlockSpec((CPS, RET_HEADS, RET_PAIR, RET_V_DIM), lambda i: (rev(i), 0, 0, 0)),
                  pl.BlockSpec(memory_space=pl.ANY)],
        out_specs=[group, gvec],
        out_shape=[jax.ShapeDtypeStruct(dproj.shape, dproj.dtype), jax.ShapeDtypeStruct((1, RET_V_WIDTH), F32)],
        input_output_aliases={10: 0},
        scratch_shapes=[pltpu.VMEM((RET_HEADS, RET_PAIR, RET_V_DIM), F32), pltpu.VMEM((L, RET_QK_WIDTH), F32),
                        pltpu.VMEM((L, RET_QK_WIDTH), F32)],
        name=name, compiler_params=_params("arbitrary"),
    )(dy, oraw, proj, proj, proj, proj, cos, sin, gain, states, dproj)


def _adamw_update(g_ref, nb, w_ref, m_ref, v_ref, go_ref, d_ref, mo_ref, vo_ref):
    g = g_ref[0].astype(F32)
    for k in range(1, nb):
        g = g + g_ref[k].astype(F32)
    mn = ADAM_B1 * m_ref[...] + (1.0 - ADAM_B1) * g
    vn = ADAM_B2 * v_ref[...] + (1.0 - ADAM_B2) * (g * g)
    go_ref[...] = g
    mo_ref[...] = mn
    vo_ref[...] = vn
    c1 = 1.0 - ADAM_B1 ** ADAM_STEP
    c2 = 1.0 - ADAM_B2 ** ADAM_STEP
    d_ref[...] = -ADAM_LR * ((mn / c1) / (jnp.sqrt(vn / c2) + ADAM_EPS) + ADAM_WD * w_ref[...])


def _adamw_rows(R, C):
    return _pick(R, tuple(t for t in (512, 256, 128, 64, 32, 16, 8) if t * C <= 384 * 1024))


def _adamw(gblocks, w, m, v, name):
    nb, R, C = gblocks.shape
    tr = _adamw_rows(R, C)

    def body(g_ref, *refs):
        _adamw_update(g_ref, nb, *refs)

    row = pl.BlockSpec((tr, C), lambda i: (i, 0))
    sh = jax.ShapeDtypeStruct((R, C), F32)
    return pl.pallas_call(
        body, grid=(R // tr,), in_specs=[pl.BlockSpec((nb, tr, C), lambda i: (0, i, 0)), row, row, row],
        out_specs=[row, row, row, row], out_shape=[sh, sh, sh, sh], name=name, compiler_params=_params("parallel"),
    )(gblocks, w, m, v)


def _adamw_layers(g0, g1, w, m, v, name):
    nb, R, C = g0.shape
    tr = _adamw_rows(R, C)

    def body(g0_ref, g1_ref, *refs):
        l = pl.program_id(0)

        @pl.when(l == 0)
        def _():
            _adamw_update(g0_ref, nb, *refs)

        @pl.when(l == 1)
        def _():
            _adamw_update(g1_ref, nb, *refs)

    row = pl.BlockSpec((None, tr, C), lambda l, i: (l, i, 0))
    sh = jax.ShapeDtypeStruct((DEPTH, R, C), F32)
    return pl.pallas_call(
        body, grid=(DEPTH, R // tr),
        in_specs=[pl.BlockSpec((nb, tr, C), lambda l, i: (0, i * (1 - l), 0)), pl.BlockSpec((nb, tr, C), lambda l, i: (0, i * l, 0)),
                  row, row, row],
        out_specs=[row, row, row, row], out_shape=[sh, sh, sh, sh], name=name, compiler_params=_params("arbitrary", "arbitrary"),
    )(g0, g1, w, m, v)


def _peers():
    x, y, c = lax.axis_index("x"), lax.axis_index("y"), lax.axis_index("c")
    flips = ((0, 0, 1), (1, 0, 0), (0, 1, 0), (1, 1, 0), (1, 0, 1), (0, 1, 1), (1, 1, 1))
    me = 4 * x + 2 * y + c
    peers = [(x ^ fx, y ^ fy, c ^ fc) for fx, fy, fc in flips]
    return me, peers


def _exchange(arrs, scatter, name):
    n = len(arrs)
    npeer = N_DEV - 1

    def body(*refs):
        ins, outs = refs[:n], refs[n:2 * n]
        send_sems, recv_sems, local_sems = refs[2 * n:]
        me, peers = _peers()
        copies = []
        for a in range(n):
            src_own = ins[a].at[me] if scatter else ins[a]
            own = pltpu.make_async_copy(src_own, outs[a].at[me], local_sems.at[a])
            own.start()
            copies.append(own)
            for k, peer in enumerate(peers):
                src = ins[a].at[4 * peer[0] + 2 * peer[1] + peer[2]] if scatter else ins[a]
                cp = pltpu.make_async_remote_copy(
                    src_ref=src, dst_ref=outs[a].at[me], send_sem=send_sems.at[a * npeer + k],
                    recv_sem=recv_sems.at[a * npeer + k], device_id=peer, device_id_type=pl.DeviceIdType.MESH)
                cp.start()
                copies.append(cp)
        for cp in copies:
            cp.wait()

    out_shape = [jax.ShapeDtypeStruct(((N_DEV,) + a.shape[1:]) if scatter else ((N_DEV,) + a.shape), a.dtype) for a in arrs]
    anyspec = pl.BlockSpec(memory_space=pl.ANY)
    return pl.pallas_call(
        body, in_specs=[anyspec] * n, out_specs=[anyspec] * n, out_shape=out_shape,
        scratch_shapes=[pltpu.SemaphoreType.DMA((n * npeer,)), pltpu.SemaphoreType.DMA((n * npeer,)),
                        pltpu.SemaphoreType.DMA((n,))],
        name=name,
    )(*arrs)


def _dev_index(peer):
    return 4 * peer[0] + 2 * peer[1] + peer[2]


def _push_copies(src_refs, land_refs, send_sems, recv_sems, scatter, as_receiver):
    me, peers = _peers()
    npeer = N_DEV - 1
    copies = []
    for a in range(len(src_refs)):
        for k, peer in enumerate(peers):
            src = src_refs[a].at[_dev_index(peer)] if scatter else src_refs[a]
            slot = _dev_index(peer) if as_receiver else me
            copies.append(pltpu.make_async_remote_copy(
                src_ref=src, dst_ref=land_refs[a].at[slot], send_sem=send_sems.at[a * npeer + k],
                recv_sem=recv_sems.at[a * npeer + k], device_id=peer, device_id_type=pl.DeviceIdType.MESH))
    return copies


def _own_copies(src_refs, land_refs, own_sems, scatter):
    me, _ = _peers()
    return [pltpu.make_async_copy(src_refs[a].at[me] if scatter else src_refs[a], land_refs[a].at[me], own_sems.at[a])
            for a in range(len(src_refs))]


def _push_start(srcs, scatter, name):
    n = len(srcs)
    nsem = n * (N_DEV - 1)

    def body(*refs):
        srcs_r, lands_r = refs[:n], refs[n:2 * n]
        for cp in _push_copies(srcs_r, lands_r, refs[2 * n], refs[2 * n + 1], scatter, False):
            cp.start()
        for cp in _own_copies(srcs_r, lands_r, refs[2 * n + 2], scatter):
            cp.start()
        token = refs[-1]
        token[...] = jnp.zeros_like(token)

    hbm = pl.BlockSpec(memory_space=pltpu.HBM)
    sem = pl.BlockSpec(memory_space=pltpu.SEMAPHORE)
    lands = [lax.empty((N_DEV,) + (s.shape[1:] if scatter else s.shape), s.dtype) for s in srcs]
    arrs = list(srcs) + lands
    return pl.pallas_call(
        body, name=name,
        out_shape=(pltpu.SemaphoreType.DMA((nsem,)), pltpu.SemaphoreType.DMA((nsem,)), pltpu.SemaphoreType.DMA((n,)),
                   *[pltpu.HBM(a.shape, a.dtype) for a in arrs], jax.ShapeDtypeStruct((8, 128), F32)),
        in_specs=[hbm] * (2 * n), out_specs=(sem, sem, sem, *([hbm] * (2 * n)), pl.BlockSpec(memory_space=pltpu.VMEM)),
        input_output_aliases={i: 3 + i for i in range(2 * n)},
        compiler_params=pltpu.CompilerParams(has_side_effects=pltpu.SideEffectType.DATAFLOW_SIDE_EFFECTING),
    )(*[pltpu.with_memory_space_constraint(a, pltpu.HBM) for a in arrs])


def _push_wait(handle, after, scatter, name):
    send_sems, recv_sems, own_sems, *thru, _ = handle
    n = len(thru) // 2

    def body(*refs):
        srcs_r, lands_r = refs[:n], refs[n:2 * n]
        for cp in _push_copies(srcs_r, lands_r, refs[2 * n], refs[2 * n + 1], scatter, True):
            cp.wait_send()
            cp.wait_recv()
        for cp in _own_copies(srcs_r, lands_r, refs[2 * n + 2], scatter):
            cp.wait()

    hbm = pl.BlockSpec(memory_space=pltpu.HBM)
    sem = pl.BlockSpec(memory_space=pltpu.SEMAPHORE)
    outs = pl.pallas_call(
        body, name=name, out_shape=tuple(pltpu.HBM(a.shape, a.dtype) for a in thru),
        in_specs=[hbm] * (2 * n) + [sem, sem, sem, pl.BlockSpec(memory_space=pl.ANY)], out_specs=tuple([hbm] * (2 * n)),
        input_output_aliases={i: i for i in range(2 * n)},
        compiler_params=pltpu.CompilerParams(has_side_effects=pltpu.SideEffectType.DATAFLOW_SIDE_EFFECTING),
    )(*thru, send_sems, recv_sems, own_sems, after)
    return list(outs[n:])


def _relay_copies(src_refs, land_refs, send1, recv1, send2, recv2, as_receiver):
    me, peers = _peers()
    sibling, chips = peers[0], peers[1:4]
    first, second = [], []
    for a in range(len(src_refs)):
        for k, peer in enumerate([sibling] + chips):
            slot = _dev_index(peer) if as_receiver else me
            first.append(pltpu.make_async_remote_copy(
                src_ref=src_refs[a], dst_ref=land_refs[a].at[slot], send_sem=send1.at[4 * a + k], recv_sem=recv1.at[4 * a + k],
                device_id=peer, device_id_type=pl.DeviceIdType.MESH))
        for k, chip in enumerate(chips):
            origin = _dev_index(chip)
            slot = origin ^ 1 if as_receiver else origin
            second.append(pltpu.make_async_remote_copy(
                src_ref=land_refs[a].at[origin], dst_ref=land_refs[a].at[slot], send_sem=send2.at[3 * a + k],
                recv_sem=recv2.at[3 * a + k], device_id=sibling, device_id_type=pl.DeviceIdType.MESH))
    return first, second


def _relay_start(srcs, name):
    n = len(srcs)

    def body(*refs):
        srcs_r, lands_r = refs[:n], refs[n:2 * n]
        for cp in _relay_copies(srcs_r, lands_r, refs[2 * n], refs[2 * n + 1], refs[2 * n], refs[2 * n + 1], False)[0]:
            cp.start()
        for cp in _own_copies(srcs_r, lands_r, refs[2 * n + 2], False):
            cp.start()
        token = refs[-1]
        token[...] = jnp.zeros_like(token)

    hbm = pl.BlockSpec(memory_space=pltpu.HBM)
    sem = pl.BlockSpec(memory_space=pltpu.SEMAPHORE)
    arrs = list(srcs) + [lax.empty((N_DEV,) + s.shape, s.dtype) for s in srcs]
    return pl.pallas_call(
        body, name=name,
        out_shape=(pltpu.SemaphoreType.DMA((4 * n,)), pltpu.SemaphoreType.DMA((4 * n,)), pltpu.SemaphoreType.DMA((n,)),
                   *[pltpu.HBM(a.shape, a.dtype) for a in arrs], jax.ShapeDtypeStruct((8, 128), F32)),
        in_specs=[hbm] * (2 * n), out_specs=(sem, sem, sem, *([hbm] * (2 * n)), pl.BlockSpec(memory_space=pltpu.VMEM)),
        input_output_aliases={i: 3 + i for i in range(2 * n)},
        compiler_params=pltpu.CompilerParams(has_side_effects=pltpu.SideEffectType.DATAFLOW_SIDE_EFFECTING),
    )(*[pltpu.with_memory_space_constraint(a, pltpu.HBM) for a in arrs])


def _relay_forward(handle, after, name):
    _, recv1, _, *thru, _ = handle
    n = len(thru) // 2

    def body(*refs):
        srcs_r, lands_r, recv1_r = refs[:n], refs[n:2 * n], refs[2 * n]
        send2_r, recv2_r = refs[2 * n + 2], refs[2 * n + 3]
        first, second = _relay_copies(srcs_r, lands_r, recv1_r, recv1_r, send2_r, recv2_r, True)
        for a in range(n):
            for k in range(1, 4):
                first[4 * a + k].wait_recv()
        for cp in _relay_copies(srcs_r, lands_r, recv1_r, recv1_r, send2_r, recv2_r, False)[1]:
            cp.start()
        token = refs[-1]
        token[...] = jnp.zeros_like(token)

    hbm = pl.BlockSpec(memory_space=pltpu.HBM)
    sem = pl.BlockSpec(memory_space=pltpu.SEMAPHORE)
    return pl.pallas_call(
        body, name=name,
        out_shape=(pltpu.SemaphoreType.DMA((3 * n,)), pltpu.SemaphoreType.DMA((3 * n,)),
                   *[pltpu.HBM(a.shape, a.dtype) for a in thru], jax.ShapeDtypeStruct((8, 128), F32)),
        in_specs=[hbm] * (2 * n) + [sem, pl.BlockSpec(memory_space=pl.ANY)],
        out_specs=(sem, sem, *([hbm] * (2 * n)), pl.BlockSpec(memory_space=pltpu.VMEM)),
        input_output_aliases={i: 2 + i for i in range(2 * n)},
        compiler_params=pltpu.CompilerParams(has_side_effects=pltpu.SideEffectType.DATAFLOW_SIDE_EFFECTING),
    )(*thru, recv1, after)


def _relay_wait(handle, forwarded, after, name):
    send1, recv1, own_sems, *_ = handle
    send2, recv2, *thru, _ = forwarded
    n = len(thru) // 2

    def body(*refs):
        srcs_r, lands_r = refs[:n], refs[n:2 * n]
        send1_r, recv1_r, own_r, send2_r, recv2_r = refs[2 * n:2 * n + 5]
        first, second = _relay_copies(srcs_r, lands_r, send1_r, recv1_r, send2_r, recv2_r, True)
        for i, cp in enumerate(first):
            cp.wait_send()
            if i % 4 == 0:
                cp.wait_recv()
        for cp in second:
            cp.wait_send()
            cp.wait_recv()
        for cp in _own_copies(srcs_r, lands_r, own_r, False):
            cp.wait()

    hbm = pl.BlockSpec(memory_space=pltpu.HBM)
    sem = pl.BlockSpec(memory_space=pltpu.SEMAPHORE)
    outs = pl.pallas_call(
        body, name=name, out_shape=tuple(pltpu.HBM(a.shape, a.dtype) for a in thru),
        in_specs=[hbm] * (2 * n) + [sem] * 5 + [pl.BlockSpec(memory_space=pl.ANY)], out_specs=tuple([hbm] * (2 * n)),
        input_output_aliases={i: i for i in range(2 * n)},
        compiler_params=pltpu.CompilerParams(has_side_effects=pltpu.SideEffectType.DATAFLOW_SIDE_EFFECTING),
    )(*thru, send1, recv1, own_sems, send2, recv2, after)
    return list(outs[n:])


def _tables(S):
    pos = jnp.arange(S, dtype=F32)
    inv = ROPE_THETA ** (-jnp.arange(0, ATT_HEAD_DIM, 2, dtype=F32) / ATT_HEAD_DIM)
    ang = pos[:, None] * inv[None, :]
    cos, sin = jnp.cos(ang), jnp.sin(ang)
    cos128 = jnp.tile(cos, (1, 4))
    sin128 = jnp.tile(jnp.concatenate([-sin, sin], axis=1), (1, 2))
    lane = np.arange(ATT_WIDTH)
    bd = jnp.asarray((lane[:, None] // 64 == lane[None, :] // 64).astype(np.float32), dtype=BF16)
    return cos128, sin128, bd, jnp.asarray(_att_bias())


def _layer_fwd(l, x, p, tabs, early=None, late=None):
    cos, sin, bd, bias = tabs
    S = x.shape[0]
    row = lambda v: v.reshape(1, -1)
    hn = _rmsnorm_fwd(x, row(p["ln_mix"]), f"norm_mix_fwd{l}")
    if early is not None:
        p.update(early(hn))
    proj = _mm(hn, p["w_in"], "nn", f"in_proj{l}", tn=1920)
    xbc = _conv_fwd(proj, p["conv_w"], row(p["conv_b"]), f"conv_fwd{l}")
    dtr = proj[:, DT_COL:DT_COL + SSD_HEADS].reshape(S, SSD_GROUPS, HPG).transpose(1, 0, 2)
    grp = lambda v: v.reshape(SSD_GROUPS, 1, HPG)
    d_rep = row(jnp.repeat(p["d_skip"], SSD_HEAD_DIM))
    y, yraw, ssd_st = _ssd_fwd(xbc, proj, dtr, grp(p["dt_bias"]), grp(p["a_log"]), d_rep, row(p["ssd_norm"]), f"ssd_fwd{l}")
    qg = row(jnp.tile(p["q_norm"], ATT_HEADS))
    kg = row(jnp.tile(p["k_norm"], ATT_HEADS))
    aq, akp, avp = _att_prep_fwd(proj, qg, kg, cos, sin, bd, f"att_prep_fwd{l}")
    y = _att_fwd(aq, akp, avp, bias, y, f"att_fwd{l}")
    y, oraw, ret_st = _ret_fwd(proj, cos, sin, row(p["ret_norm"]), y, f"ret_fwd{l}")
    if late is not None:
        p.update(late(y))
    x1 = _mm(y, p["w_out"], "nn", f"out_proj{l}", residual=x)
    hn2 = _rmsnorm_fwd(x1, row(p["ln_ffn"]), f"norm_ffn_fwd{l}")
    g, u, act = _swiglu_fwd(hn2, p["w_gate"], p["w_up"], f"swiglu_fwd{l}")
    x2 = _mm(act, p["w_down"], "nn", f"down_proj{l}", residual=x1, tk=2816)
    saved = dict(x=x, hn=hn, proj=proj, xbc=xbc, dtr=dtr, yraw=yraw, ssd_st=ssd_st, aq=aq, akp=akp, avp=avp,
                 oraw=oraw, ret_st=ret_st, y=y, x1=x1, hn2=hn2, g=g, u=u, act=act, d_rep=d_rep, qg=qg, kg=kg)
    return x2, saved


def _layer_bwd(l, dx2, dx2_bf, p, sv, tabs, on_ffn=None, on_all=None):
    cos, sin, bd, bias = tabs
    S = dx2.shape[0]
    row = lambda v: v.reshape(1, -1)
    grp = lambda v: v.reshape(SSD_GROUPS, 1, HPG)
    gr = {}
    dg, du = _swiglu_bwd(dx2_bf, p["w_down"], sv["g"], sv["u"], f"swiglu_bwd{l}")
    gr["w_down"] = _mm(sv["act"], dx2_bf, "tn", f"down_wgrad{l}", out_dtype=BF16, tm=1408, tn=1024, tk=2048)
    dhn2 = _mm_nt2(dg, p["w_gate"], du, p["w_up"], f"ffn_dgrad{l}")
    gr["w_gate"] = _mm(sv["hn2"], dg, "tn", f"gate_wgrad{l}", out_dtype=BF16, tm=512, tn=2816, tk=2048)
    gr["w_up"] = _mm(sv["hn2"], du, "tn", f"up_wgrad{l}", out_dtype=BF16, tm=512, tn=2816, tk=2048)
    ffn_gain = row(p["ln_ffn"]) + (on_ffn(gr)[0, 0] if on_ffn is not None else 0.0)
    dx1, dx1_bf, dln_ffn = _rmsnorm_bwd(sv["x1"], dhn2, ffn_gain, dx2, f"norm_ffn_bwd{l}")
    gr["ln_ffn"] = dln_ffn[0]
    dy = _mm(dx1_bf, p["w_out"], "nt", f"out_dgrad{l}")
    gr["w_out"] = _mm(sv["y"], dx1_bf, "tn", f"out_wgrad{l}", out_dtype=BF16, tm=1024, tn=1024, tk=2048)
    dproj, dxs, dbm, dcm, ddtr, dbias, dalog, dd, dssd_gain = _ssd_bwd(
        dy, sv["yraw"], sv["xbc"], sv["proj"], sv["dtr"], grp(p["dt_bias"]), grp(p["a_log"]), sv["d_rep"],
        row(p["ssd_norm"]), sv["ssd_st"], f"ssd_bwd{l}")
    gr["dt_bias"], gr["a_log"], gr["d_skip"] = dbias.reshape(-1), dalog.reshape(-1), dd.reshape(-1)
    gr["ssd_norm"] = dssd_gain[0]
    dproj, dconv_w, dconv_b = _conv_bwd(dxs, dbm, dcm, sv["proj"], p["conv_w"], row(p["conv_b"]), dproj, f"conv_bwd{l}")
    gr["conv_w"], gr["conv_b"] = dconv_w, dconv_b[0]
    dq, dk_p, dv_p = _att_bwd(sv["aq"], sv["akp"], sv["avp"], bias, dy, f"att_bwd{l}")
    dproj, dqg, dkg = _att_prep_bwd(sv["proj"], dq, dk_p, dv_p, sv["qg"], sv["kg"], cos, sin, bd, dproj, f"att_prep_bwd{l}")
    gr["q_norm"], gr["k_norm"] = dqg[0], dkg[0]
    dproj, dret_gain = _ret_bwd(dy, sv["oraw"], sv["proj"], cos, sin, row(p["ret_norm"]), sv["ret_st"], dproj, f"ret_bwd{l}")
    gr["ret_norm"] = dret_gain[0]
    ddt_cols = ddtr.transpose(1, 0, 2).reshape(S, SSD_HEADS).astype(BF16)
    dproj = lax.dynamic_update_slice(dproj, jnp.pad(ddt_cols, ((0, 0), (0, IN_PAD - DT_COL - SSD_HEADS))), (0, DT_COL))
    gr["w_in"] = _mm(sv["hn"], dproj, "tn", f"in_wgrad{l}", out_dtype=BF16, tm=1024, tn=1920, tk=2048)
    launched = on_all(gr) if on_all is not None else None
    dhn = _mm(dproj, p["w_in"], "nt", f"in_dgrad{l}", tm=512, tk=IN_PAD, after=launched)
    dx0, dx0_bf, dln_mix = _rmsnorm_bwd(sv["x"], dhn, row(p["ln_mix"]), dx1, f"norm_mix_bwd{l}")
    gr["ln_mix"] = dln_mix[0]
    return dx0, dx0_bf, gr


def _local_step(x, tgt, layers, early=None, late=None, on_ffn=None, on_all=None):
    n = len(layers)
    none = [None] * n
    early, late, on_ffn, on_all = early or none, late or none, on_ffn or none, on_all or none
    tabs = _tables(x.shape[0])
    saved, params = [], []
    h = x
    for l in range(n):
        p = dict(layers[l](h) if callable(layers[l]) else layers[l])
        h, sv = _layer_fwd(l, h, p, tabs, early[l], late[l])
        saved.append(sv)
        params.append(p)
    dh, dh_bf, lacc = _loss_grad(h, tgt, "loss_grad")
    grads = [None] * n
    for l in reversed(range(n)):
        dh, dh_bf, grads[l] = _layer_bwd(l, dh, dh_bf, params[l], saved[l], tabs, on_ffn[l], on_all[l])
    return lacc[0, 0], dh, grads


BIG = ("w_in", "w_out", "w_gate", "w_up", "w_down")
SMALL = ("ln_mix", "conv_b", "dt_bias", "a_log", "d_skip", "ssd_norm", "q_norm", "k_norm", "ret_norm", "ln_ffn")
ORDER = ("ln_mix", "w_in", "conv_w", "conv_b", "dt_bias", "a_log", "d_skip", "ssd_norm", "q_norm", "k_norm", "ret_norm",
         "w_out", "ln_ffn", "w_gate", "w_up", "w_down")


COL_SHARDED = ("w_in", "w_gate", "w_up", "conv_w")


IN_GROUPS = ((ORIG_Z_XBC, Z_COL), (ORIG_DT, DT_COL), (ORIG_ATT, ATT_COL), (ORIG_RET, RET_COL))


def _full_weight(k, gathered):
    if k == "w_in":
        cs = gathered.shape[2]
        pieces = []
        for (lo, hi), _ in sorted(IN_GROUPS, key=lambda grp: grp[1]):
            for j in range(N_DEV):
                a, b = max(lo, j * cs), min(hi, (j + 1) * cs)
                if a < b:
                    pieces.append(gathered[j][:, a - j * cs:b - j * cs])
        pieces.append(jnp.zeros((gathered.shape[1], IN_PAD - IN_WIDTH), gathered.dtype))
        return jnp.concatenate(pieces, axis=1)
    if k in COL_SHARDED:
        return gathered.transpose(1, 0, 2).reshape(gathered.shape[1], -1)
    return gathered.reshape(-1, gathered.shape[2])


def _shard_block(k, g):
    if k == "w_in":
        cs = IN_WIDTH // N_DEV
        blocks = []
        for j in range(N_DEV):
            pieces = []
            for (lo, hi), col in IN_GROUPS:
                a, b = max(lo, j * cs), min(hi, (j + 1) * cs)
                if a < b:
                    pieces.append(g[:, col + a - lo:col + b - lo])
            blocks.append(jnp.concatenate(pieces, axis=1))
        return jnp.stack(blocks)
    if k in COL_SHARDED:
        return g.reshape(g.shape[0], N_DEV, -1).transpose(1, 0, 2)
    return g.reshape(N_DEV, -1, g.shape[1])


def kernel(x, ln_mix, w_in, conv_w, conv_b, dt_bias, a_log, d_skip, ssd_norm, q_norm, k_norm, ret_norm, w_out, ln_ffn, w_gate, w_up, w_down, loss_target, m_ln_mix, m_w_in, m_conv_w, m_conv_b, m_dt_bias, m_a_log, m_d_skip, m_ssd_norm, m_q_norm, m_k_norm, m_ret_norm, m_w_out, m_ln_ffn, m_w_gate, m_w_up, m_w_down, v_ln_mix, v_w_in, v_conv_w, v_conv_b, v_dt_bias, v_a_log, v_d_skip, v_ssd_norm, v_q_norm, v_k_norm, v_ret_norm, v_w_out, v_ln_ffn, v_w_gate, v_w_up, v_w_down):
    w = dict(ln_mix=ln_mix, w_in=w_in, conv_w=conv_w, conv_b=conv_b, dt_bias=dt_bias, a_log=a_log, d_skip=d_skip,
             ssd_norm=ssd_norm, q_norm=q_norm, k_norm=k_norm, ret_norm=ret_norm, w_out=w_out, ln_ffn=ln_ffn,
             w_gate=w_gate, w_up=w_up, w_down=w_down)
    m = dict(ln_mix=m_ln_mix, w_in=m_w_in, conv_w=m_conv_w, conv_b=m_conv_b, dt_bias=m_dt_bias, a_log=m_a_log,
             d_skip=m_d_skip, ssd_norm=m_ssd_norm, q_norm=m_q_norm, k_norm=m_k_norm, ret_norm=m_ret_norm, w_out=m_w_out,
             ln_ffn=m_ln_ffn, w_gate=m_w_gate, w_up=m_w_up, w_down=m_w_down)
    v = dict(ln_mix=v_ln_mix, w_in=v_w_in, conv_w=v_conv_w, conv_b=v_conv_b, dt_bias=v_dt_bias, a_log=v_a_log,
             d_skip=v_d_skip, ssd_norm=v_ssd_norm, q_norm=v_q_norm, k_norm=v_k_norm, ret_norm=v_ret_norm, w_out=v_w_out,
             ln_ffn=v_ln_ffn, w_gate=v_w_gate, w_up=v_w_up, w_down=v_w_down)
    me = 4 * lax.axis_index("x") + 2 * lax.axis_index("y") + lax.axis_index("c")

    late_names = ("w_out", "w_gate", "w_up", "w_down")
    waves = {"a": [("w_in", 0), ("conv_w", 0), ("conv_w", 1)], "b": [(k, 0) for k in late_names],
             "c": [("w_in", 1)], "d": [(k, 1) for k in late_names]}
    gather = {}
    behind = 0.0
    for tag, items in waves.items():
        srcs = [w[k][l] if k == "conv_w" else (w[k][l] + behind).astype(BF16) for k, l in items]
        start = _relay_start(srcs, f"gather_{tag}_start") if tag == "a" else _push_start(srcs, False, f"gather_{tag}_start")
        gather[tag] = start
        behind = start[-1][0, 0]
    started = behind
    full = {}

    def arrive(tag, after):
        if tag == "a":
            forwarded = _relay_forward(gather["a"], after, "gather_a_forward")
            lands = _relay_wait(gather["a"], forwarded, forwarded[-1], "gather_a_wait")
        else:
            lands = _push_wait(gather[tag], after, False, f"gather_{tag}_wait")
        for (k, l), g in zip(waves[tag], lands):
            full[k, l] = _full_weight(k, g)

    def layer_weights(l, names):
        return {k: full[k, l] for k in names}

    def small_weights(l):
        return {k: w[k][l] for k in SMALL}

    def layer0(h):
        p = small_weights(0)
        p["ln_mix"] = p["ln_mix"] + started
        return p

    def early0(hn):
        arrive("a", hn)
        return layer_weights(0, ("w_in", "conv_w"))

    def late0(y):
        arrive("b", y)
        return layer_weights(0, late_names)

    def layer1(h):
        arrive("c", h)
        return {**small_weights(1), **layer_weights(1, ("w_in", "conv_w"))}

    def late1(y):
        arrive("d", y)
        return layer_weights(1, late_names)

    groups = {"1": [(k, 1) for k in BIG], "0a": [(k, 0) for k in ("w_down", "w_gate", "w_up")],
              "0b": [(k, 0) for k in ("w_out", "w_in")]}
    scatter = {}

    def push_grads(tag, gr):
        blocks = [_shard_block(k, gr[k]) for k, _ in groups[tag]]
        scatter[tag] = _push_start(blocks, True, f"scatter_{tag}_start")
        return scatter[tag][-1]

    loss_part, gx, grads = _local_step(
        x[0], loss_target[0], [layer0, layer1], early=[early0, None], late=[late0, late1],
        on_ffn=[functools.partial(push_grads, "0a"), None],
        on_all=[functools.partial(push_grads, "0b"), functools.partial(push_grads, "1")])
    loss = lax.psum(loss_part, MESH_AXES)

    out = {}
    recv = {}
    for tag, items in groups.items():
        for item, r in zip(items, _push_wait(scatter[tag], gx, True, f"scatter_{tag}_wait")):
            recv[item] = r
    for k in BIG:
        out[k] = _adamw_layers(recv[k, 0], recv[k, 1], w[k], m[k], v[k], f"adamw_{k}")
    names = SMALL + ("conv_w",)
    sizes = [int(np.prod(grads[0][k].shape)) for k in names]
    packed = jnp.concatenate([jnp.stack([grads[l][k] for l in range(DEPTH)]).reshape(-1) for k in names])
    n_small = packed.shape[0]
    rows_small = -(-n_small // 1024) * 8
    pad = lambda t, fill: jnp.concatenate([t, jnp.full((rows_small * 128 - n_small,), fill, F32)]).reshape(rows_small, 128)
    parts = _exchange([pad(packed, 0.0)], False, "gather_small_grads")[0]
    n_rep = DEPTH * sum(sizes[:-1])
    pack_rep = lambda d, fill: pad(jnp.concatenate([d[k].reshape(-1) for k in SMALL]
                                                   + [jnp.full((n_small - n_rep,), fill, F32)]), fill)
    res = _adamw(parts, pack_rep(w, 1.0), pack_rep(m, 1.0), pack_rep(v, 1.0), "adamw_small")
    res = [t.reshape(-1) for t in res]
    off = 0
    for k, sz in zip(SMALL, sizes[:-1]):
        out[k] = [t[off:off + DEPTH * sz].reshape(w[k].shape) for t in res]
        off += DEPTH * sz
    gconv = res[0][off:off + DEPTH * sizes[-1]].reshape(DEPTH, SSD_CONV, SSD_CONV_CH)
    gconv = lax.dynamic_slice_in_dim(gconv, me * conv_w.shape[2], conv_w.shape[2], axis=2)
    flat = lambda t: t.reshape(8, -1)
    resc = _adamw(flat(gconv)[None], flat(conv_w), flat(m_conv_w), flat(v_conv_w), "adamw_conv_w")
    out["conv_w"] = [t.reshape(conv_w.shape) for t in resc]

    return (loss, gx[None], *[out[k][0] for k in ORDER], *[out[k][1] for k in ORDER],
            *[out[k][2] for k in ORDER], *[out[k][3] for k in ORDER])
```

```python
import functools
import math

import jax
import jax.numpy as jnp
import numpy as np
from jax import lax
from jax.experimental import pallas as pl
from jax.experimental.pallas import tpu as pltpu

F32 = jnp.float32
BF16 = jnp.bfloat16

N_DEV = 8
MESH_AXES = ("x", "y", "c")
D_MODEL = 2048
DEPTH = 2
EPS = 1e-6
ROPE_THETA = 10000.0
SSD_HEADS = 16
SSD_HEAD_DIM = 64
SSD_WIDTH = 1024
SSD_GROUPS = 2
SSD_STATE = 128
SSD_CONV = 4
SSD_CONV_CH = 1536
ATT_HEADS = 8
ATT_HEAD_DIM = 64
ATT_WIDTH = 512
DILATED_PAIRS = ((128, 1), (512, 4), (2048, 16))
RET_HEADS = 4
RET_QK_DIM = 64
RET_V_DIM = 128
RET_QK_WIDTH = 256
RET_V_WIDTH = 512
CHUNK = 128
MIX_WIDTH = 2048
ATT_SPAN = 2048
ATT_STRIP = ATT_SPAN + CHUNK
ATT_QB = 8
IN_WIDTH = 5648
IN_PAD = 5760
RET_COL, ATT_COL, Z_COL, XBC_COL, DT_COL = 0, 1536, 3072, 4096, 5632
ORIG_Z_XBC, ORIG_DT, ORIG_ATT, ORIG_RET = (0, 2560), (2560, 2576), (2576, 4112), (4112, 5648)
D_FF = 5632
ADAM_LR = 0.001
ADAM_B1 = 0.9
ADAM_B2 = 0.999
ADAM_EPS = 1e-08
ADAM_WD = 0.01
ADAM_STEP = 10
NEG = -1e30
VMEM_LIMIT_V7X = 60 * 1024 * 1024

NN = (((1,), (0,)), ((), ()))
NT = (((1,), (1,)), ((), ()))
TN = (((0,), (0,)), ((), ()))


def _bdot(a, b, dims):
    return lax.dot_general(a.astype(BF16), b.astype(BF16), dims, preferred_element_type=F32)


def _xdot(a, b, dims, exact_first=False, pieces=3):
    ones, x = (a, b) if exact_first else (b, a)
    ones = ones.astype(BF16)
    acc, rest = None, x
    for _ in range(pieces):
        piece = rest.astype(BF16)
        rest = rest - piece.astype(F32)
        part = lax.dot_general(*((ones, piece) if exact_first else (piece, ones)), dims, preferred_element_type=F32)
        acc = part if acc is None else acc + part
    return acc


def _params(*sem):
    return pltpu.CompilerParams(dimension_semantics=sem, vmem_limit_bytes=VMEM_LIMIT_V7X)


def _sigmoid(v):
    return 1.0 / (1.0 + jnp.exp(-v))


def _silu_grad(v, s):
    return s * (1.0 + v * (1.0 - s))


def _rmsnorm_fwd(x, g, name):
    S, D = x.shape
    tr = min(512, S)

    def body(x_ref, g_ref, o_ref):
        xv = x_ref[...]
        r = lax.rsqrt(jnp.mean(xv * xv, axis=-1, keepdims=True) + EPS)
        o_ref[...] = (xv * r * g_ref[...]).astype(o_ref.dtype)

    return pl.pallas_call(
        body, grid=(S // tr,),
        in_specs=[pl.BlockSpec((tr, D), lambda i: (i, 0)), pl.BlockSpec((1, D), lambda i: (0, 0))],
        out_specs=pl.BlockSpec((tr, D), lambda i: (i, 0)),
        out_shape=jax.ShapeDtypeStruct((S, D), BF16), name=name, compiler_params=_params("parallel"),
    )(x, g)


def _rmsnorm_bwd(x, dy, g, dres, name):
    S, D = x.shape
    tr = min(512, S)

    def body(x_ref, dy_ref, g_ref, dres_ref, dx_ref, dxb_ref, dg_ref):
        i = pl.program_id(0)
        xv = x_ref[...]
        r = lax.rsqrt(jnp.mean(xv * xv, axis=-1, keepdims=True) + EPS)
        n = xv * r
        dy = dy_ref[...]
        dn = dy * g_ref[...]
        dx = dres_ref[...] + r * (dn - n * jnp.mean(dn * n, axis=-1, keepdims=True))
        dx_ref[...] = dx
        dxb_ref[...] = dx.astype(BF16)
        part = jnp.sum(dy * n, axis=0, keepdims=True)

        @pl.when(i == 0)
        def _():
            dg_ref[...] = part

        @pl.when(i > 0)
        def _():
            dg_ref[...] += part

    row = pl.BlockSpec((tr, D), lambda i: (i, 0))
    vec = pl.BlockSpec((1, D), lambda i: (0, 0))
    return pl.pallas_call(
        body, grid=(S // tr,), in_specs=[row, row, vec, row], out_specs=[row, row, vec],
        out_shape=[jax.ShapeDtypeStruct((S, D), F32), jax.ShapeDtypeStruct((S, D), BF16), jax.ShapeDtypeStruct((1, D), F32)],
        name=name, compiler_params=_params("arbitrary"),
    )(x, dy, g, dres)


def _loss_grad(y, tgt, name):
    S, D = y.shape
    tr = min(512, S)

    def body(y_ref, t_ref, dy_ref, dyb_ref, l_ref):
        i = pl.program_id(0)
        err = y_ref[...] - t_ref[...]
        dy = err * (1.0 / D)
        dy_ref[...] = dy
        dyb_ref[...] = dy.astype(BF16)
        part = jnp.sum(jnp.sum(err * err, axis=1, keepdims=True), axis=0, keepdims=True) * (0.5 / D)

        @pl.when(i == 0)
        def _():
            l_ref[...] = jnp.zeros_like(l_ref)

        l_ref[...] += part

    row = pl.BlockSpec((tr, D), lambda i: (i, 0))
    return pl.pallas_call(
        body, grid=(S // tr,), in_specs=[row, row],
        out_specs=[row, row, pl.BlockSpec((8, 128), lambda i: (0, 0))],
        out_shape=[jax.ShapeDtypeStruct((S, D), F32), jax.ShapeDtypeStruct((S, D), BF16), jax.ShapeDtypeStruct((8, 128), F32)],
        name=name, compiler_params=_params("arbitrary"),
    )(y, tgt)


def _pick(n, cands):
    for c in cands:
        if n % c == 0:
            return c
    return n


def _mm(a, b, mode, name, out_dtype=F32, residual=None, tm=None, tn=None, tk=None, after=None):
    if mode == "nn":
        (M, K), (_, N) = a.shape, b.shape
    elif mode == "nt":
        (M, K), (N, _) = a.shape, b.shape
    else:
        (K, M), (_, N) = a.shape, b.shape
    tm = min(tm, M) if tm else _pick(M, (1024, 512, 256, 128))
    tn = min(tn, N) if tn else _pick(N, (1024, 1152, 1408, 512, 256, 128))
    tk = min(tk, K) if tk else _pick(K, (2048, 1920, 1408, 1024, 512, 256, 128))
    assert M % tm == 0 and N % tn == 0 and K % tk == 0, (name, M, N, K, tm, tn, tk)
    nk = K // tk
    a_spec = pl.BlockSpec((tk, tm), lambda i, j, k: (k, i)) if mode == "tn" else pl.BlockSpec((tm, tk), lambda i, j, k: (i, k))
    b_spec = pl.BlockSpec((tn, tk), lambda i, j, k: (j, k)) if mode == "nt" else pl.BlockSpec((tk, tn), lambda i, j, k: (k, j))
    o_spec = pl.BlockSpec((tm, tn), lambda i, j, k: (i, j))
    dims = {"nn": NN, "nt": NT, "tn": TN}[mode]
    has_res = residual is not None

    has_after = after is not None

    def body(*refs):
        a_ref, b_ref = refs[0], refs[1]
        r_ref = refs[2] if has_res else None
        o_ref = refs[2 + has_res + has_after]
        p = _bdot(a_ref[...], b_ref[...], dims)

        def finish(acc):
            if has_res:
                acc = acc + r_ref[...]
            o_ref[...] = acc.astype(o_ref.dtype)

        if nk == 1:
            finish(p)
        else:
            acc_ref = refs[-1]
            k = pl.program_id(2)

            @pl.when(k == 0)
            def _():
                acc_ref[...] = p

            @pl.when(k > 0)
            def _():
                acc_ref[...] += p

            @pl.when(k == nk - 1)
            def _():
                finish(acc_ref[...])

    ins = [a, b] + ([residual] if has_res else []) + ([after] if has_after else [])
    in_specs = [a_spec, b_spec] + ([o_spec] if has_res else []) + ([pl.BlockSpec(memory_space=pl.ANY)] if has_after else [])
    scratch = [pltpu.VMEM((tm, tn), F32)] if nk > 1 else []
    return pl.pallas_call(
        body, grid=(M // tm, N // tn, nk), in_specs=in_specs, out_specs=o_spec,
        out_shape=jax.ShapeDtypeStruct((M, N), out_dtype), scratch_shapes=scratch, name=name,
        compiler_params=_params("parallel", "parallel", "arbitrary"),
    )(*ins)


def _accumulate(acc_ref, p, k, nk, finish):
    @pl.when(k == 0)
    def _():
        acc_ref[...] = p

    @pl.when(k > 0)
    def _():
        acc_ref[...] += p

    @pl.when(k == nk - 1)
    def _():
        finish(acc_ref[...])


def _swiglu_fwd(hn, wg, wu, name):
    S, K = hn.shape
    F = wg.shape[1]
    tm = _pick(S, (1024, 512))
    tn = _pick(F, (512, 256, 128))

    def body(a_ref, wg_ref, wu_ref, g_ref, u_ref, act_ref):
        a = a_ref[...]
        g = _bdot(a, wg_ref[...], NN)
        u = _bdot(a, wu_ref[...], NN)
        g_ref[...] = g.astype(BF16)
        u_ref[...] = u.astype(BF16)
        act_ref[...] = (g * _sigmoid(g) * u).astype(BF16)

    w_spec = pl.BlockSpec((K, tn), lambda i, j: (0, j))
    o_spec = pl.BlockSpec((tm, tn), lambda i, j: (i, j))
    sh = jax.ShapeDtypeStruct((S, F), BF16)
    return pl.pallas_call(
        body, grid=(S // tm, F // tn), in_specs=[pl.BlockSpec((tm, K), lambda i, j: (i, 0)), w_spec, w_spec],
        out_specs=[o_spec, o_spec, o_spec], out_shape=[sh, sh, sh], name=name,
        compiler_params=_params("parallel", "parallel"),
    )(hn, wg, wu)


def _swiglu_bwd(dx, wd, g, u, name):
    S, K = dx.shape
    F = wd.shape[0]
    tm = _pick(S, (1024, 512))
    tn = _pick(F, (512, 256, 128))

    def body(dx_ref, wd_ref, g_ref, u_ref, dg_ref, du_ref):
        da = _bdot(dx_ref[...], wd_ref[...], NT)
        gv = g_ref[...].astype(F32)
        uv = u_ref[...].astype(F32)
        s = _sigmoid(gv)
        dg_ref[...] = (da * uv * _silu_grad(gv, s)).astype(BF16)
        du_ref[...] = (da * gv * s).astype(BF16)

    o_spec = pl.BlockSpec((tm, tn), lambda i, j: (i, j))
    sh = jax.ShapeDtypeStruct((S, F), BF16)
    return pl.pallas_call(
        body, grid=(S // tm, F // tn),
        in_specs=[pl.BlockSpec((tm, K), lambda i, j: (i, 0)), pl.BlockSpec((tn, K), lambda i, j: (j, 0)), o_spec, o_spec],
        out_specs=[o_spec, o_spec], out_shape=[sh, sh], name=name, compiler_params=_params("parallel", "parallel"),
    )(dx, wd, g, u)


def _mm_nt2(a1, b1, a2, b2, name):
    M, K = a1.shape
    N = b1.shape[0]
    tm = _pick(M, (512,))
    tn = _pick(N, (1024, 512))
    tk = _pick(K, (2816, 1024, 512, 256, 128))
    nk = K // tk

    def body(a1_ref, b1_ref, a2_ref, b2_ref, o_ref, acc_ref):
        def finish(acc):
            o_ref[...] = acc

        p = _bdot(a1_ref[...], b1_ref[...], NT) + _bdot(a2_ref[...], b2_ref[...], NT)
        _accumulate(acc_ref, p, pl.program_id(2), nk, finish)

    a_spec = pl.BlockSpec((tm, tk), lambda i, j, k: (i, k))
    b_spec = pl.BlockSpec((tn, tk), lambda i, j, k: (j, k))
    return pl.pallas_call(
        body, grid=(M // tm, N // tn, nk), in_specs=[a_spec, b_spec, a_spec, b_spec],
        out_specs=pl.BlockSpec((tm, tn), lambda i, j, k: (i, j)), out_shape=jax.ShapeDtypeStruct((M, N), F32),
        scratch_shapes=[pltpu.VMEM((tm, tn), F32)], name=name,
        compiler_params=_params("parallel", "parallel", "arbitrary"),
    )(a1, b1, a2, b2)


XBC_BLK0 = XBC_COL // 128


def _conv_fwd(proj, w, b, name):
    S = proj.shape[0]
    T = min(512, S)

    def body(x_ref, w_ref, b_ref, o_ref, xp_ref):
        xp_ref[pl.ds(0, 8), :] = jnp.zeros((8, 128), F32)
        xp_ref[pl.ds(8, S), :] = x_ref[...]
        wv = w_ref[...]
        bv = b_ref[...]

        def step(c, carry):
            base = pl.multiple_of(c * T, T)
            acc = wv[0:1] * xp_ref[pl.ds(base + 5, T), :]
            for i in range(1, SSD_CONV):
                acc = acc + wv[i:i + 1] * xp_ref[pl.ds(base + 5 + i, T), :]
            acc = bv + acc
            o_ref[pl.ds(base, T), :] = acc * _sigmoid(acc)
            return carry

        lax.fori_loop(0, S // T, step, 0)

    return pl.pallas_call(
        body, grid=(SSD_CONV_CH // 128,),
        in_specs=[pl.BlockSpec((S, 128), lambda j: (0, XBC_BLK0 + j)), pl.BlockSpec((SSD_CONV, 128), lambda j: (0, j)),
                  pl.BlockSpec((1, 128), lambda j: (0, j))],
        out_specs=pl.BlockSpec((S, 128), lambda j: (0, j)),
        out_shape=jax.ShapeDtypeStruct((S, SSD_CONV_CH), F32),
        scratch_shapes=[pltpu.VMEM((S + 8, 128), F32)], name=name, compiler_params=_params("parallel"),
    )(proj, w, b)


def _conv_bwd(dxs, dbm, dcm, proj, w, b, dproj, name):
    S = proj.shape[0]
    T = min(512, S)
    NX, NB = SSD_WIDTH // 128, SSD_GROUPS * SSD_STATE // 128

    def body(dxs_ref, dbm_ref, dcm_ref, x_ref, w_ref, b_ref, dproj_ref, dx_ref, dw_ref, db_ref, xp_ref, dcp_ref):
        j = pl.program_id(0)

        @pl.when(j < NX)
        def _():
            dcp_ref[pl.ds(0, S), :] = dxs_ref[...]

        @pl.when((j >= NX) & (j < NX + NB))
        def _():
            dcp_ref[pl.ds(0, S), :] = dbm_ref[...]

        @pl.when(j >= NX + NB)
        def _():
            dcp_ref[pl.ds(0, S), :] = dcm_ref[...]

        da_ref = dcp_ref
        xp_ref[pl.ds(0, 8), :] = jnp.zeros((8, 128), F32)
        xp_ref[pl.ds(8, S), :] = x_ref[...]
        dcp_ref[pl.ds(S, 8), :] = jnp.zeros((8, 128), F32)
        wv = w_ref[...]
        bv = b_ref[...]

        def step1(c, carry):
            base = pl.multiple_of(c * T, T)
            xs = [xp_ref[pl.ds(base + 5 + i, T), :] for i in range(SSD_CONV)]
            acc = wv[0:1] * xs[0]
            for i in range(1, SSD_CONV):
                acc = acc + wv[i:i + 1] * xs[i]
            acc = bv + acc
            s = _sigmoid(acc)
            dc = da_ref[pl.ds(base, T), :] * _silu_grad(acc, s)
            dcp_ref[pl.ds(base, T), :] = dc
            new = tuple(carry[i] + jnp.sum(xs[i] * dc, axis=0, keepdims=True) for i in range(SSD_CONV))
            return new + (carry[SSD_CONV] + jnp.sum(dc, axis=0, keepdims=True),)

        z = jnp.zeros((1, 128), F32)
        res = lax.fori_loop(0, S // T, step1, (z,) * (SSD_CONV + 1))
        for i in range(SSD_CONV):
            dw_ref[pl.ds(i, 1), :] = res[i]
        db_ref[...] = res[SSD_CONV]

        def step2(c, carry):
            base = pl.multiple_of(c * T, T)
            acc = wv[0:1] * dcp_ref[pl.ds(base + 3, T), :]
            for i in range(1, SSD_CONV):
                acc = acc + wv[i:i + 1] * dcp_ref[pl.ds(base + 3 - i, T), :]
            dx_ref[pl.ds(base, T), :] = acc.astype(dx_ref.dtype)
            return carry

        lax.fori_loop(0, S // T, step2, 0)

    clamp = lambda j, lo, n: jnp.clip(j - lo, 0, n - 1)
    return pl.pallas_call(
        body, grid=(SSD_CONV_CH // 128,),
        in_specs=[pl.BlockSpec((S, 128), lambda j: (0, clamp(j, 0, NX))), pl.BlockSpec((S, 128), lambda j: (0, clamp(j, NX, NB))),
                  pl.BlockSpec((S, 128), lambda j: (0, clamp(j, NX + NB, NB))),
                  pl.BlockSpec((S, 128), lambda j: (0, XBC_BLK0 + j)), pl.BlockSpec((SSD_CONV, 128), lambda j: (0, j)),
                  pl.BlockSpec((1, 128), lambda j: (0, j)), pl.BlockSpec(memory_space=pl.ANY)],
        out_specs=[pl.BlockSpec((S, 128), lambda j: (0, XBC_BLK0 + j)), pl.BlockSpec((SSD_CONV, 128), lambda j: (0, j)),
                   pl.BlockSpec((1, 128), lambda j: (0, j))],
        out_shape=[jax.ShapeDtypeStruct(dproj.shape, dproj.dtype), jax.ShapeDtypeStruct((SSD_CONV, SSD_CONV_CH), F32),
                   jax.ShapeDtypeStruct((1, SSD_CONV_CH), F32)],
        input_output_aliases={6: 0},
        scratch_shapes=[pltpu.VMEM((S + 8, 128), F32), pltpu.VMEM((S + 8, 128), F32)], name=name,
        compiler_params=_params("arbitrary"),
    )(dxs, dbm, dcm, proj, w, b, dproj)


HPG = SSD_HEADS // SSD_GROUPS
GW = HPG * SSD_HEAD_DIM


def _ssd_chunk_terms(dtr, bias, alog, tril, triu):
    pre = dtr + bias
    dt = jnp.maximum(pre, 0.0) + jnp.log(1.0 + jnp.exp(-jnp.abs(pre)))
    a_neg = -jnp.exp(alog)
    a = dt * a_neg
    acum = _xdot(tril, a, NN, exact_first=True)
    acum_t = _xdot(a, triu, TN)
    return pre, dt, a_neg, acum, acum_t


def _head_expanders():
    h64 = lax.broadcasted_iota(jnp.int32, (HPG, GW), 0) == lax.broadcasted_iota(jnp.int32, (HPG, GW), 1) // SSD_HEAD_DIM
    h128 = lax.broadcasted_iota(jnp.int32, (HPG, HPG * CHUNK), 0) == lax.broadcasted_iota(jnp.int32, (HPG, HPG * CHUNK), 1) // CHUNK
    return h64.astype(F32), h128.astype(F32)


def _ssd_fwd(xbc, proj, dtr, dt_bias, a_log, d_rep, gain, name):
    S = xbc.shape[0]
    L = CHUNK
    T = min(512, S)
    CPS = T // L
    NC = S // L

    def body(x_ref, b_ref, c_ref, z_ref, dtr_ref, bias_ref, alog_ref, d_ref, gain_ref, y_ref, yraw_ref, st_ref, state):
        i = pl.program_id(1)

        @pl.when(i == 0)
        def _():
            state[...] = jnp.zeros_like(state)

        row = lax.broadcasted_iota(jnp.int32, (L, L), 0)
        col = lax.broadcasted_iota(jnp.int32, (L, L), 1)
        causal = row >= col
        tril = causal.astype(F32)
        triu = (row <= col).astype(F32)
        low = col < SSD_HEAD_DIM
        e64, e128 = _head_expanders()
        for c in range(CPS):
            rows = pl.ds(c * L, L)
            xv = x_ref[rows, :]
            bm = b_ref[rows, :]
            cm = c_ref[rows, :]
            _, dt, _, acum, acum_t = _ssd_chunk_terms(dtr_ref[rows, :], bias_ref[...], alog_ref[...], tril, triu)
            ac = _xdot(acum, e64, NN)
            ac_sq = _xdot(acum, e128, NN)
            xd = xv * _xdot(dt, e64, NN, pieces=2)
            ac_last = ac[L - 1:L, :]
            sp = state[...]
            st_ref[c] = sp
            yoff = _bdot(cm, sp, NN) * jnp.exp(ac)
            state[...] = sp * jnp.exp(ac_last) + _bdot(bm, xd * jnp.exp(ac_last - ac), TN)
            gmat = _bdot(cm, bm, NT)
            for q in range(HPG // 2):
                pair = slice(q * 128, (q + 1) * 128)
                tile = xd[:, pair]
                y = yoff[:, pair]
                for j, keep in ((2 * q, low), (2 * q + 1, ~low)):
                    lam = jnp.exp(jnp.where(causal, ac_sq[:, j * L:(j + 1) * L] - acum_t[j:j + 1, :], NEG))
                    y = y + _bdot(gmat * lam, jnp.where(keep, tile, 0.0), NN)
                yraw_ref[rows, pair] = y
            zz = z_ref[rows, :]
            u = (yraw_ref[rows, :] + xv * d_ref[...]) * (zz * _sigmoid(zz))
            r = lax.rsqrt(jnp.mean(u * u, axis=-1, keepdims=True) + EPS)
            y_ref[rows, :] = (u * r * gain_ref[...]).astype(y_ref.dtype)

    vec8 = pl.BlockSpec((None, 1, HPG), lambda g, i: (g, 0, 0))
    return pl.pallas_call(
        body, grid=(SSD_GROUPS, S // T),
        in_specs=[pl.BlockSpec((T, GW), lambda g, i: (i, g)),
                  pl.BlockSpec((T, SSD_STATE), lambda g, i: (i, SSD_WIDTH // SSD_STATE + g)),
                  pl.BlockSpec((T, SSD_STATE), lambda g, i: (i, SSD_WIDTH // SSD_STATE + SSD_GROUPS + g)),
                  pl.BlockSpec((T, GW), lambda g, i: (i, Z_COL // GW + g)),
                  pl.BlockSpec((None, T, HPG), lambda g, i: (g, i, 0)),
                  vec8, vec8,
                  pl.BlockSpec((1, GW), lambda g, i: (0, g)), pl.BlockSpec((1, GW), lambda g, i: (0, g))],
        out_specs=[pl.BlockSpec((T, GW), lambda g, i: (i, g)), pl.BlockSpec((T, GW), lambda g, i: (i, g)),
                   pl.BlockSpec((CPS, None, SSD_STATE, GW), lambda g, i: (i, g, 0, 0))],
        out_shape=[jax.ShapeDtypeStruct((S, MIX_WIDTH), BF16), jax.ShapeDtypeStruct((S, SSD_WIDTH), F32),
                   jax.ShapeDtypeStruct((NC, SSD_GROUPS, SSD_STATE, GW), F32)],
        scratch_shapes=[pltpu.VMEM((SSD_STATE, GW), F32)], name=name,
        compiler_params=_params("arbitrary", "arbitrary"),
    )(xbc, xbc, xbc, proj, dtr, dt_bias, a_log, d_rep, gain)


def _ssd_bwd(dy, yraw, xbc, proj, dtr, dt_bias, a_log, d_rep, gain, states, name):
    S = xbc.shape[0]
    L = CHUNK
    T = min(512, S)
    CPS = T // L
    NI = S // T

    def body(dy_ref, yraw_ref, x_ref, b_ref, c_ref, z_ref, dtr_ref, bias_ref, alog_ref, d_ref, gain_ref, st_ref,
             dz_ref, dx_ref, db_ref, dc_ref, ddtr_ref, dbias_ref, dalog_ref, dd_ref, dgain_ref, dstate, dxd_ref):
        i = pl.program_id(1)

        @pl.when(i == 0)
        def _():
            dstate[...] = jnp.zeros_like(dstate)
            dbias_ref[...] = jnp.zeros_like(dbias_ref)
            dalog_ref[...] = jnp.zeros_like(dalog_ref)
            dd_ref[...] = jnp.zeros_like(dd_ref)
            dgain_ref[...] = jnp.zeros_like(dgain_ref)

        row = lax.broadcasted_iota(jnp.int32, (L, L), 0)
        col = lax.broadcasted_iota(jnp.int32, (L, L), 1)
        causal = row >= col
        tril = causal.astype(F32)
        triu = (row <= col).astype(F32)
        low = col < SSD_HEAD_DIM
        e64, e128 = _head_expanders()
        lane8 = lax.broadcasted_iota(jnp.int32, (1, HPG), 1)
        sub8 = lax.broadcasted_iota(jnp.int32, (HPG, 1), 0)
        eye8 = (lax.broadcasted_iota(jnp.int32, (HPG, HPG), 0) == lax.broadcasted_iota(jnp.int32, (HPG, HPG), 1)).astype(F32)
        last_row = (lax.broadcasted_iota(jnp.int32, (L, 1), 0) == L - 1).astype(F32)
        for c in reversed(range(CPS)):
            rows = pl.ds(c * L, L)
            xv = x_ref[rows, :]
            bm = b_ref[rows, :]
            cm = c_ref[rows, :]
            zz = z_ref[rows, :]
            dvec = d_ref[...]
            sz = _sigmoid(zz)
            silu_z = zz * sz
            v = yraw_ref[rows, :] + xv * dvec
            u = v * silu_z
            r = lax.rsqrt(jnp.mean(u * u, axis=-1, keepdims=True) + EPS)
            n = u * r
            do = dy_ref[rows, :]
            dgain_ref[...] += jnp.sum(do * n, axis=0, keepdims=True)
            dn = do * gain_ref[...]
            du = r * (dn - n * jnp.mean(dn * n, axis=-1, keepdims=True))
            dz_ref[rows, :] = (du * v * _silu_grad(zz, sz)).astype(dz_ref.dtype)
            dyv = du * silu_z
            dd_ref[...] += _xdot(jnp.sum(dyv * xv, axis=0, keepdims=True), e64, NT, pieces=2)
            pre, dt, a_neg, acum, acum_t = _ssd_chunk_terms(dtr_ref[rows, :], bias_ref[...], alog_ref[...], tril, triu)
            ac = _xdot(acum, e64, NN)
            ac_sq = _xdot(acum, e128, NN)
            dt_w = _xdot(dt, e64, NN, pieces=2)
            xd = xv * dt_w
            ac_last = ac[L - 1:L, :]
            ea = jnp.exp(ac)
            w = jnp.exp(ac_last - ac)
            ea_last = jnp.exp(ac_last)
            sp = st_ref[c]
            ds = dstate[...]
            dye = dyv * ea
            yoff = _bdot(cm, sp, NN) * ea
            bds = _bdot(bm, ds, NN)
            dcm = _bdot(dye, sp, NT)
            dbm = _bdot(xd * w, ds, NT)
            dstate[...] = ds * ea_last + _bdot(cm, dye, TN)
            w8 = jnp.exp(acum[L - 1:L, :] - acum)
            dw8 = _xdot(xd * bds, e64, NT, pieces=2)
            dac8 = _xdot(dyv * yoff, e64, NT, pieces=2) - dw8 * w8
            tail8 = jnp.sum(dw8 * w8, axis=0, keepdims=True) + jnp.exp(acum[L - 1:L, :]) * _xdot(
                jnp.sum(ds * sp, axis=0, keepdims=True), e64, NT, pieces=2)
            dac8 = dac8 + last_row * tail8
            gmat = _bdot(cm, bm, NT)
            dgmat = jnp.zeros((L, L), F32)
            colsum_t = jnp.zeros((HPG, L), F32)
            for q in range(HPG // 2):
                pair = slice(q * 128, (q + 1) * 128)
                xd_tile = xd[:, pair]
                dy_tile = dyv[:, pair]
                dxd_tile = bds[:, pair] * w[:, pair]
                for j, keep in ((2 * q, low), (2 * q + 1, ~low)):
                    lam = jnp.exp(jnp.where(causal, ac_sq[:, j * L:(j + 1) * L] - acum_t[j:j + 1, :], NEG))
                    mh = gmat * lam
                    dyj = jnp.where(keep, dy_tile, 0.0)
                    dxd_tile = dxd_tile + _bdot(mh, dyj, TN)
                    dm = _bdot(dyj, xd_tile, NT)
                    dgmat = dgmat + dm * lam
                    qm = dm * mh
                    dac8 = dac8 + jnp.sum(qm, axis=1, keepdims=True) * (lane8 == j).astype(F32)
                    colsum_t = colsum_t + (sub8 == j).astype(F32) * jnp.sum(qm, axis=0, keepdims=True)
                dxd_ref[:, pair] = dxd_tile
            dac8 = dac8 - _xdot(colsum_t, eye8, TN)
            dxd = dxd_ref[...]
            dx_ref[rows, :] = dxd * dt_w + dyv * dvec
            dc_ref[rows, :] = dcm + _bdot(dgmat, bm, NN)
            db_ref[rows, :] = dbm + _bdot(dgmat, cm, TN)
            da8 = _xdot(triu, dac8, NN, exact_first=True)
            ddt8 = _xdot(dxd * xv, e64, NT, pieces=2) + da8 * a_neg
            dalog_ref[...] += jnp.sum(da8 * dt, axis=0, keepdims=True) * a_neg
            dpre = ddt8 * _sigmoid(pre)
            ddtr_ref[rows, :] = dpre
            dbias_ref[...] += jnp.sum(dpre, axis=0, keepdims=True)

    rev = lambda i: NI - 1 - i
    vec8 = pl.BlockSpec((None, 1, HPG), lambda g, i: (g, 0, 0))
    grp = pl.BlockSpec((T, GW), lambda g, i: (rev(i), g))
    bspec = pl.BlockSpec((T, SSD_STATE), lambda g, i: (rev(i), SSD_WIDTH // SSD_STATE + g))
    cspec = pl.BlockSpec((T, SSD_STATE), lambda g, i: (rev(i), SSD_WIDTH // SSD_STATE + SSD_GROUPS + g))
    gvec = pl.BlockSpec((1, GW), lambda g, i: (0, g))
    st_spec = pl.BlockSpec((CPS, None, SSD_STATE, GW), lambda g, i: (rev(i), g, 0, 0))
    small = jax.ShapeDtypeStruct((SSD_GROUPS, 1, HPG), F32)
    zspec = pl.BlockSpec((T, GW), lambda g, i: (rev(i), Z_COL // GW + g))
    return pl.pallas_call(
        body, grid=(SSD_GROUPS, NI),
        in_specs=[grp, grp, grp, bspec, cspec, zspec, pl.BlockSpec((None, T, HPG), lambda g, i: (g, rev(i), 0)),
                  vec8, vec8, gvec, gvec, st_spec],
        out_specs=[zspec, grp, pl.BlockSpec((T, SSD_STATE), lambda g, i: (rev(i), g)),
                   pl.BlockSpec((T, SSD_STATE), lambda g, i: (rev(i), g)),
                   pl.BlockSpec((None, T, HPG), lambda g, i: (g, rev(i), 0)), vec8, vec8, vec8, gvec],
        out_shape=[jax.ShapeDtypeStruct((S, IN_PAD), BF16), jax.ShapeDtypeStruct((S, SSD_WIDTH), F32),
                   jax.ShapeDtypeStruct((S, SSD_GROUPS * SSD_STATE), F32), jax.ShapeDtypeStruct((S, SSD_GROUPS * SSD_STATE), F32),
                   jax.ShapeDtypeStruct((SSD_GROUPS, S, HPG), F32), small, small, small,
                   jax.ShapeDtypeStruct((1, SSD_WIDTH), F32)],
        scratch_shapes=[pltpu.VMEM((SSD_STATE, GW), F32), pltpu.VMEM((L, GW), F32)],
        name=name, compiler_params=_params("arbitrary", "arbitrary"),
    )(dy, yraw, xbc, xbc, xbc, proj, dtr, dt_bias, a_log, d_rep, gain, states)


def _swap_halves(t):
    w = t.shape[1]
    lane = lax.broadcasted_iota(jnp.int32, t.shape, 1)
    return jnp.where((lane % 64) < 32, pltpu.roll(t, w - 32, axis=1), pltpu.roll(t, 32, axis=1))


def _widen(tab, w):
    return tab if w == 128 else jnp.concatenate([tab] * (w // 128), axis=1)


def _rope(t, cos, sin_signed):
    return t * cos + _swap_halves(t) * sin_signed


def _rope_t(d, cos, sin_signed):
    return d * cos - _swap_halves(d) * sin_signed


def _group_sum64(v, bd):
    hi = v.astype(BF16)
    lo = (v - hi.astype(F32)).astype(BF16)
    return (lax.dot_general(hi, bd, NN, preferred_element_type=F32)
            + lax.dot_general(lo, bd, NN, preferred_element_type=F32))


AQ_BLK = ATT_COL // ATT_WIDTH


def _att_prep_fwd(proj, qg, kg, cos, sin, bd, name):
    S = proj.shape[0]
    T = min(512, S)
    PB = ATT_SPAN // T
    src = lambda i: jnp.maximum(i - PB, 0)

    def body(q_ref, k_ref, v_ref, qg_ref, kg_ref, cos_ref, sin_ref, bd_ref, qo_ref, ko_ref, vo_ref):
        i = pl.program_id(0)

        @pl.when(i < PB)
        def _():
            ko_ref[...] = jnp.zeros_like(ko_ref)
            vo_ref[...] = jnp.zeros_like(vo_ref)

        @pl.when(i >= PB)
        def _():
            cw = _widen(cos_ref[...], ATT_WIDTH)
            sw = _widen(sin_ref[...], ATT_WIDTH)
            bdv = bd_ref[...]

            def norm_rope(t, gain):
                ss = _group_sum64(t * t, bdv)
                return _rope(t * lax.rsqrt(ss * (1.0 / ATT_HEAD_DIM) + EPS) * gain, cw, sw)

            qo_ref[...] = (norm_rope(q_ref[...], qg_ref[...]) * (ATT_HEAD_DIM ** -0.5)).astype(BF16)
            kt = norm_rope(k_ref[...], kg_ref[...]).astype(BF16)
            vt = v_ref[...].astype(BF16)
            for pr in range(ATT_HEADS // 2):
                ko_ref[pr] = kt[:, pr * 128:(pr + 1) * 128]
                vo_ref[pr] = vt[:, pr * 128:(pr + 1) * 128]

    vec = pl.BlockSpec((1, ATT_WIDTH), lambda i: (0, 0))
    tab = pl.BlockSpec((T, 128), lambda i: (src(i), 0))
    hm = pl.BlockSpec((ATT_HEADS // 2, T, 128), lambda i: (0, i, 0))
    hm_shape = jax.ShapeDtypeStruct((ATT_HEADS // 2, ATT_SPAN + S, 128), BF16)
    return pl.pallas_call(
        body, grid=(PB + S // T,),
        in_specs=[pl.BlockSpec((T, ATT_WIDTH), lambda i: (src(i), AQ_BLK)), pl.BlockSpec((T, ATT_WIDTH), lambda i: (src(i), AQ_BLK + 1)),
                  pl.BlockSpec((T, ATT_WIDTH), lambda i: (src(i), AQ_BLK + 2)), vec, vec, tab, tab,
                  pl.BlockSpec((ATT_WIDTH, ATT_WIDTH), lambda i: (0, 0))],
        out_specs=[pl.BlockSpec((T, ATT_WIDTH), lambda i: (src(i), 0)), hm, hm],
        out_shape=[jax.ShapeDtypeStruct((S, ATT_WIDTH), BF16), hm_shape, hm_shape],
        name=name, compiler_params=_params("arbitrary"),
    )(proj, proj, proj, qg, kg, cos, sin, bd)


def _att_prep_bwd(proj, dq, dk_p, dv_p, qg, kg, cos, sin, bd, dproj, name):
    S = proj.shape[0]
    T = min(512, S)
    NI = S // T
    PB = ATT_SPAN // T
    W = ATT_WIDTH

    def body(q_ref, k_ref, dq_ref, dkp_ref, dvp_ref, qg_ref, kg_ref, cos_ref, sin_ref, bd_ref, dproj_ref,
             do_ref, dqg_ref, dkg_ref, acc_ref):
        i = pl.program_id(0)

        @pl.when(i == 0)
        def _():
            acc_ref[...] = jnp.zeros_like(acc_ref)

        npair = ATT_HEADS // 2
        dk_all = jnp.concatenate([dkp_ref[pr].T for pr in range(npair)], axis=1)
        do_ref[:, 2 * W:3 * W] = jnp.concatenate([dvp_ref[pr].T for pr in range(npair)], axis=1).astype(BF16)
        cw = _widen(cos_ref[...], ATT_WIDTH)
        sw = _widen(sin_ref[...], ATT_WIDTH)
        bdv = bd_ref[...]

        def one(t, d_rot, gain, scale, slot):
            ss = _group_sum64(t * t, bdv)
            r = lax.rsqrt(ss * (1.0 / ATT_HEAD_DIM) + EPS)
            n = t * r
            d_ng = _rope_t(d_rot * scale, cw, sw)
            acc_ref[pl.ds(slot, 1), :] += jnp.sum(d_ng * n, axis=0, keepdims=True)
            dn = d_ng * gain
            return r * (dn - n * (_group_sum64(dn * n, bdv) * (1.0 / ATT_HEAD_DIM)))

        do_ref[:, 0:W] = one(q_ref[...], dq_ref[...], qg_ref[...], ATT_HEAD_DIM ** -0.5, 0).astype(BF16)
        do_ref[:, W:2 * W] = one(k_ref[...], dk_all, kg_ref[...], 1.0, 1).astype(BF16)

        @pl.when(i == NI - 1)
        def _():
            a = acc_ref[...]
            f = a[:, 0:64]
            for h in range(1, ATT_HEADS):
                f = f + a[:, h * 64:(h + 1) * 64]
            dqg_ref[...] = f[0:1]
            dkg_ref[...] = f[1:2]

    vec = pl.BlockSpec((1, ATT_WIDTH), lambda i: (0, 0))
    tab = pl.BlockSpec((T, 128), lambda i: (i, 0))
    row = pl.BlockSpec((T, ATT_WIDTH), lambda i: (i, 0))
    g64 = pl.BlockSpec((1, ATT_HEAD_DIM), lambda i: (0, 0))
    padded = pl.BlockSpec((ATT_HEADS // 2, 128, T), lambda i: (0, 0, i + PB))
    return pl.pallas_call(
        body, grid=(NI,),
        in_specs=[pl.BlockSpec((T, ATT_WIDTH), lambda i: (i, AQ_BLK)), pl.BlockSpec((T, ATT_WIDTH), lambda i: (i, AQ_BLK + 1)),
                  row, padded, padded, vec, vec, tab, tab, pl.BlockSpec((ATT_WIDTH, ATT_WIDTH), lambda i: (0, 0)),
                  pl.BlockSpec(memory_space=pl.ANY)],
        out_specs=[pl.BlockSpec((T, 3 * W), lambda i: (i, ATT_COL // (3 * W))), g64, g64],
        out_shape=[jax.ShapeDtypeStruct(dproj.shape, dproj.dtype), jax.ShapeDtypeStruct((1, ATT_HEAD_DIM), F32),
                   jax.ShapeDtypeStruct((1, ATT_HEAD_DIM), F32)],
        input_output_aliases={10: 0},
        scratch_shapes=[pltpu.VMEM((8, ATT_WIDTH), F32)], name=name, compiler_params=_params("arbitrary"),
    )(proj, proj, dq, dk_p, dv_p, qg, kg, cos, sin, bd, dproj)


def _att_bias():
    qpos = np.arange(CHUNK)[:, None] + ATT_SPAN
    kpos = np.arange(ATT_STRIP)[None, :]
    rel = qpos - kpos
    mult = np.zeros((CHUNK, ATT_STRIP), np.float64)
    for window, dil in DILATED_PAIRS:
        mult += (rel >= 0) & (rel % dil == 0) & (rel // dil <= window // dil)
    return np.where(mult > 0, np.log(np.maximum(mult, 1.0)), NEG).astype(np.float32)


def _att_scores(q, ks, bias, i):
    s = _bdot(q, ks, NT) + bias
    kcol = lax.broadcasted_iota(jnp.int32, (1, ATT_STRIP), 1) + i * CHUNK
    return jnp.where(kcol >= ATT_SPAN, s, NEG)


def _pair_masks():
    low = lax.broadcasted_iota(jnp.int32, (CHUNK, 128), 1) < ATT_HEAD_DIM
    return low, ~low


def _att_fwd(q, kp, vp, bias, y, name):
    S = q.shape[0]
    SP = kp.shape[1]

    QB = min(ATT_QB, S // CHUNK)
    TQ = QB * CHUNK

    def body(q_ref, k_ref, v_ref, bias_ref, y_ref, o_ref):
        i = pl.program_id(1)
        for b in range(QB):
            blk = i * QB + b
            strip = pl.ds(pl.multiple_of(blk * CHUNK, CHUNK), ATT_STRIP)
            rows = pl.ds(b * CHUNK, CHUNK)
            qv = q_ref[rows, :]
            ks = k_ref[strip, :]
            vs = v_ref[strip, :]
            outs = []
            for keep in _pair_masks():
                s = _att_scores(jnp.where(keep, qv, jnp.zeros_like(qv)), ks, bias_ref[...], blk)
                m = jnp.max(s, axis=-1, keepdims=True)
                p = jnp.exp(s - m)
                den = jnp.sum(p, axis=-1, keepdims=True)
                outs.append(_bdot(p, vs, NN) / den)
            o_ref[rows, :] = jnp.where(_pair_masks()[0], outs[0], outs[1]).astype(o_ref.dtype)

    kv = pl.BlockSpec((None, SP, 128), lambda hp, i: (hp, 0, 0))
    return pl.pallas_call(
        body, grid=(ATT_HEADS // 2, S // TQ),
        in_specs=[pl.BlockSpec((TQ, 128), lambda hp, i: (i, hp)), kv, kv,
                  pl.BlockSpec((CHUNK, ATT_STRIP), lambda hp, i: (0, 0)), pl.BlockSpec(memory_space=pl.ANY)],
        out_specs=pl.BlockSpec((TQ, 128), lambda hp, i: (i, SSD_WIDTH // 128 + hp)),
        out_shape=jax.ShapeDtypeStruct(y.shape, y.dtype), input_output_aliases={4: 0}, name=name,
        compiler_params=_params("parallel", "arbitrary"),
    )(q, kp, vp, bias, y)


def _att_bwd(q, kp, vp, bias, dy, name):
    S = q.shape[0]
    SP = kp.shape[1]
    QB = min(ATT_QB, S // CHUNK)

    def body(q_ref, k_ref, v_ref, bias_ref, do_ref, dq_ref, dk_ref, dv_ref):
        i = pl.program_id(1)

        @pl.when(i == 0)
        def _():
            dk_ref[...] = jnp.zeros_like(dk_ref)
            dv_ref[...] = jnp.zeros_like(dv_ref)

        for b in range(QB):
            blk = i * QB + b
            strip = pl.ds(pl.multiple_of(blk * CHUNK, CHUNK), ATT_STRIP)
            rows = pl.ds(b * CHUNK, CHUNK)
            qv = q_ref[rows, :]
            dov = do_ref[rows, :]
            ks = k_ref[strip, :]
            vs = v_ref[strip, :]
            dq = jnp.zeros((CHUNK, 128), F32)
            dk_t = jnp.zeros((128, ATT_STRIP), F32)
            dv_t = jnp.zeros((128, ATT_STRIP), F32)
            for keep in _pair_masks():
                qh = jnp.where(keep, qv, jnp.zeros_like(qv))
                doh = jnp.where(keep, dov, 0.0)
                s = _att_scores(qh, ks, bias_ref[...], blk)
                m = jnp.max(s, axis=-1, keepdims=True)
                p = jnp.exp(s - m)
                p = p / jnp.sum(p, axis=-1, keepdims=True)
                dp = _bdot(doh, vs, NT)
                dsc = p * (dp - jnp.sum(p * dp, axis=-1, keepdims=True))
                dq = dq + jnp.where(keep, _bdot(dsc, ks, NN), 0.0)
                dv_t = dv_t + _bdot(doh, p, TN)
                dk_t = dk_t + _bdot(qh, dsc, TN)
            dq_ref[rows, :] = dq
            dv_ref[:, strip] += dv_t
            dk_ref[:, strip] += dk_t

    TQ = QB * CHUNK
    kv = pl.BlockSpec((None, SP, 128), lambda hp, i: (hp, 0, 0))
    kv_t =pl.BlockSpec((None, 128, SP), lambda hp, i: (hp, 0, 0))
    pairs = jax.ShapeDtypeStruct((ATT_HEADS // 2, 128, SP), F32)
    return pl.pallas_call(
        body, grid=(ATT_HEADS // 2, S // TQ),
        in_specs=[pl.BlockSpec((TQ, 128), lambda hp, i: (i, hp)), kv, kv,
                  pl.BlockSpec((CHUNK, ATT_STRIP), lambda hp, i: (0, 0)),
                  pl.BlockSpec((TQ, 128), lambda hp, i: (i, SSD_WIDTH // 128 + hp))],
        out_specs=[pl.BlockSpec((TQ, 128), lambda hp, i: (i, hp)), kv_t, kv_t],
        out_shape=[jax.ShapeDtypeStruct((S, ATT_WIDTH), F32), pairs, pairs],
        name=name, compiler_params=_params("parallel", "arbitrary"),
    )(q, kp, vp, bias, dy)


RQ_BLK = RET_COL // RET_QK_WIDTH
RV_BLK = (RET_COL + 2 * RET_QK_WIDTH) // RET_V_WIDTH
RET_PAIR = 2 * RET_QK_DIM
RET_LOG_GAMMA = tuple(math.log1p(-2.0 ** (-5.0 - h)) for h in range(RET_HEADS))


def _ret_decays(h):
    L = CHUNK
    lg = RET_LOG_GAMMA[h]
    row = lax.broadcasted_iota(jnp.int32, (L, L), 0)
    col = lax.broadcasted_iota(jnp.int32, (L, L), 1)
    rel = (row - col).astype(F32)
    dm = jnp.where(rel >= 0, jnp.exp(jnp.maximum(rel, 0.0) * lg), 0.0)
    idx = lax.broadcasted_iota(jnp.int32, (L, 1), 0).astype(F32)
    kte = jnp.exp((L - 1 - idx) * lg)
    qfs = jnp.exp((idx + 1.0) * lg)
    return dm, kte, qfs, math.exp(L * lg)


def _ret_head(t, h):
    tile = t[:, (h // 2) * RET_PAIR:(h // 2 + 1) * RET_PAIR]
    low = lax.broadcasted_iota(jnp.int32, tile.shape, 1) < RET_QK_DIM
    return jnp.where(low if h % 2 == 0 else ~low, tile, 0.0)


def _ret_fwd(proj, cos, sin, gain, y, name):
    S = proj.shape[0]
    L = CHUNK
    T = min(512, S)
    CPS = T // L
    NC = S // L

    def body(q_ref, k_ref, v_ref, g_ref, cos_ref, sin_ref, gain_ref, yin_ref, y_ref, o_ref, st_ref, state):
        i = pl.program_id(0)

        @pl.when(i == 0)
        def _():
            state[...] = jnp.zeros_like(state)

        dec = [_ret_decays(h) for h in range(RET_HEADS)]
        for c in range(CPS):
            rows = pl.ds(c * L, L)
            cw = _widen(cos_ref[rows, :], RET_QK_WIDTH)
            sw = _widen(sin_ref[rows, :], RET_QK_WIDTH)
            qv = _rope(q_ref[rows, :], cw, sw)
            kv = _rope(k_ref[rows, :], cw, sw) * (RET_QK_DIM ** -0.5)
            for h in range(RET_HEADS):
                dm, kte, qfs, cd = dec[h]
                qh, kh = _ret_head(qv, h), _ret_head(kv, h)
                vs = slice(h * RET_V_DIM, (h + 1) * RET_V_DIM)
                vh = v_ref[rows, vs]
                sp = state[h]
                st_ref[c, h] = sp
                o = _bdot(_bdot(qh, kh, NT) * dm, vh, NN) + _bdot(qh * qfs, sp, NN)
                state[h] = cd * sp + _bdot(kh * kte, vh, TN)
                o_ref[rows, vs] = o
                gh = g_ref[rows, vs]
                r = lax.rsqrt(jnp.mean(o * o, axis=-1, keepdims=True) + EPS)
                y_ref[rows, vs] = (o * r * gain_ref[:, vs] * (gh * _sigmoid(gh))).astype(y_ref.dtype)

    tab = pl.BlockSpec((T, 128), lambda i: (i, 0))
    wide = pl.BlockSpec((T, RET_V_WIDTH), lambda i: (i, 0))
    return pl.pallas_call(
        body, grid=(S // T,),
        in_specs=[pl.BlockSpec((T, RET_QK_WIDTH), lambda i: (i, RQ_BLK)), pl.BlockSpec((T, RET_QK_WIDTH), lambda i: (i, RQ_BLK + 1)),
                  pl.BlockSpec((T, RET_V_WIDTH), lambda i: (i, RV_BLK)), pl.BlockSpec((T, RET_V_WIDTH), lambda i: (i, RV_BLK + 1)),
                  tab, tab, pl.BlockSpec((1, RET_V_WIDTH), lambda i: (0, 0)), pl.BlockSpec(memory_space=pl.ANY)],
        out_specs=[pl.BlockSpec((T, RET_V_WIDTH), lambda i: (i, (SSD_WIDTH + ATT_WIDTH) // RET_V_WIDTH)), wide,
                   pl.BlockSpec((CPS, RET_HEADS, RET_PAIR, RET_V_DIM), lambda i: (i, 0, 0, 0))],
        out_shape=[jax.ShapeDtypeStruct(y.shape, y.dtype), jax.ShapeDtypeStruct((S, RET_V_WIDTH), F32),
                   jax.ShapeDtypeStruct((NC, RET_HEADS, RET_PAIR, RET_V_DIM), F32)],
        input_output_aliases={7: 0},
        scratch_shapes=[pltpu.VMEM((RET_HEADS, RET_PAIR, RET_V_DIM), F32)], name=name,
        compiler_params=_params("arbitrary"),
    )(proj, proj, proj, proj, cos, sin, gain, y)


def _ret_bwd(dy, oraw, proj, cos, sin, gain, states, dproj, name):
    S = proj.shape[0]
    L = CHUNK
    T = min(512, S)
    CPS = T // L
    NI = S // T
    QW, VW = RET_QK_WIDTH, RET_V_WIDTH
    V0, G0 = 2 * QW, 2 * QW + VW

    def body(dy_ref, o_ref, q_ref, k_ref, v_ref, g_ref, cos_ref, sin_ref, gain_ref, st_ref, dproj_ref,
             out_ref, dgain_ref, dstate, dqs, dks):
        i = pl.program_id(0)

        @pl.when(i == 0)
        def _():
            dstate[...] = jnp.zeros_like(dstate)
            dgain_ref[...] = jnp.zeros_like(dgain_ref)

        dec = [_ret_decays(h) for h in range(RET_HEADS)]
        for c in reversed(range(CPS)):
            rows = pl.ds(c * L, L)
            cw = _widen(cos_ref[rows, :], RET_QK_WIDTH)
            sw = _widen(sin_ref[rows, :], RET_QK_WIDTH)
            qv = _rope(q_ref[rows, :], cw, sw)
            kv = _rope(k_ref[rows, :], cw, sw) * (RET_QK_DIM ** -0.5)
            for h in range(RET_HEADS):
                dm, kte, qfs, cd = dec[h]
                pair = slice((h // 2) * RET_PAIR, (h // 2 + 1) * RET_PAIR)
                vs = slice(h * RET_V_DIM, (h + 1) * RET_V_DIM)
                qh, kh = _ret_head(qv, h), _ret_head(kv, h)
                vh = v_ref[rows, vs]
                gh = g_ref[rows, vs]
                gn = gain_ref[:, vs]
                o = o_ref[rows, vs]
                dyh = dy_ref[rows, vs]
                sg = _sigmoid(gh)
                silu_g = gh * sg
                r = lax.rsqrt(jnp.mean(o * o, axis=-1, keepdims=True) + EPS)
                n = o * r
                dgain_ref[:, vs] += jnp.sum(dyh * n * silu_g, axis=0, keepdims=True)
                out_ref[rows, G0 + h * RET_V_DIM:G0 + (h + 1) * RET_V_DIM] = (dyh * n * gn * _silu_grad(gh, sg)).astype(out_ref.dtype)
                dn = dyh * gn * silu_g
                do = r * (dn - n * jnp.mean(dn * n, axis=-1, keepdims=True))
                sp = st_ref[c, h]
                ds = dstate[h]
                sc = _bdot(qh, kh, NT) * dm
                dsc = _bdot(do, vh, NT) * dm
                out_ref[rows, V0 + h * RET_V_DIM:V0 + (h + 1) * RET_V_DIM] = (_bdot(sc, do, TN) + _bdot(kh * kte, ds, NN)).astype(out_ref.dtype)
                dqh = _bdot(dsc, kh, NN) + _bdot(do, sp, NT) * qfs
                dkh = _bdot(dsc, qh, TN) + _bdot(vh, ds, NT) * kte
                if h % 2 == 0:
                    dqs[:, pair] = dqh
                    dks[:, pair] = dkh
                else:
                    dqs[:, pair] += dqh
                    dks[:, pair] += dkh
                dstate[h] = cd * ds + _bdot(qh * qfs, do, TN)
            out_ref[rows, 0:QW] = _rope_t(dqs[...], cw, sw).astype(out_ref.dtype)
            out_ref[rows, QW:2 * QW] = _rope_t(dks[...] * (RET_QK_DIM ** -0.5), cw, sw).astype(out_ref.dtype)

    rev = lambda i: NI - 1 - i
    tab = pl.BlockSpec((T, 128), lambda i: (rev(i), 0))
    wide = pl.BlockSpec((T, RET_V_WIDTH), lambda i: (rev(i), 0))
    group = pl.BlockSpec((T, G0 + VW), lambda i: (rev(i), RET_COL // (G0 + VW)))
    gvec = pl.BlockSpec((1, RET_V_WIDTH), lambda i: (0, 0))
    return pl.pallas_call(
        body, grid=(NI,),
        in_specs=[pl.BlockSpec((T, RET_V_WIDTH), lambda i: (rev(i), (SSD_WIDTH + ATT_WIDTH) // RET_V_WIDTH)), wide,
                  pl.BlockSpec((T, RET_QK_WIDTH), lambda i: (rev(i), RQ_BLK)), pl.BlockSpec((T, RET_QK_WIDTH), lambda i: (rev(i), RQ_BLK + 1)),
                  pl.BlockSpec((T, RET_V_WIDTH), lambda i: (rev(i), RV_BLK)), pl.BlockSpec((T, RET_V_WIDTH), lambda i: (rev(i), RV_BLK + 1)),
                  tab, tab, gvec,
                  pl.BlockSpec((CPS, RET_HEADS, RET_PAIR, RET_V_DIM), lambda i: (rev(i), 0, 0, 0)),
                  pl.BlockSpec(memory_space=pl.ANY)],
        out_specs=[group, gvec],
        out_shape=[jax.ShapeDtypeStruct(dproj.shape, dproj.dtype), jax.ShapeDtypeStruct((1, RET_V_WIDTH), F32)],
        input_output_aliases={10: 0},
        scratch_shapes=[pltpu.VMEM((RET_HEADS, RET_PAIR, RET_V_DIM), F32), pltpu.VMEM((L, RET_QK_WIDTH), F32),
                        pltpu.VMEM((L, RET_QK_WIDTH), F32)],
        name=name, compiler_params=_params("arbitrary"),
    )(dy, oraw, proj, proj, proj, proj, cos, sin, gain, states, dproj)


def _adamw_update(g_ref, nb, w_ref, m_ref, v_ref, go_ref, d_ref, mo_ref, vo_ref):
    g = g_ref[0].astype(F32)
    for k in range(1, nb):
        g = g + g_ref[k].astype(F32)
    mn = ADAM_B1 * m_ref[...] + (1.0 - ADAM_B1) * g
    vn = ADAM_B2 * v_ref[...] + (1.0 - ADAM_B2) * (g * g)
    go_ref[...] = g
    mo_ref[...] = mn
    vo_ref[...] = vn
    c1 = 1.0 - ADAM_B1 ** ADAM_STEP
    c2 = 1.0 - ADAM_B2 ** ADAM_STEP
    d_ref[...] = -ADAM_LR * ((mn / c1) / (jnp.sqrt(vn / c2) + ADAM_EPS) + ADAM_WD * w_ref[...])


def _adamw_rows(R, C):
    return _pick(R, tuple(t for t in (512, 256, 128, 64, 32, 16, 8) if t * C <= 256 * 1024))


def _adamw(gblocks, w, m, v, name):
    nb, R, C = gblocks.shape
    tr = _adamw_rows(R, C)

    def body(g_ref, *refs):
        _adamw_update(g_ref, nb, *refs)

    row = pl.BlockSpec((tr, C), lambda i: (i, 0))
    sh = jax.ShapeDtypeStruct((R, C), F32)
    return pl.pallas_call(
        body, grid=(R // tr,), in_specs=[pl.BlockSpec((nb, tr, C), lambda i: (0, i, 0)), row, row, row],
        out_specs=[row, row, row, row], out_shape=[sh, sh, sh, sh], name=name, compiler_params=_params("parallel"),
    )(gblocks, w, m, v)


def _adamw_layers(g0, g1, w, m, v, name):
    nb, R, C = g0.shape
    tr = _adamw_rows(R, C)

    def body(g0_ref, g1_ref, *refs):
        l = pl.program_id(0)

        @pl.when(l == 0)
        def _():
            _adamw_update(g0_ref, nb, *refs)

        @pl.when(l == 1)
        def _():
            _adamw_update(g1_ref, nb, *refs)

    row = pl.BlockSpec((None, tr, C), lambda l, i: (l, i, 0))
    sh = jax.ShapeDtypeStruct((DEPTH, R, C), F32)
    return pl.pallas_call(
        body, grid=(DEPTH, R // tr),
        in_specs=[pl.BlockSpec((nb, tr, C), lambda l, i: (0, i * (1 - l), 0)), pl.BlockSpec((nb, tr, C), lambda l, i: (0, i * l, 0)),
                  row, row, row],
        out_specs=[row, row, row, row], out_shape=[sh, sh, sh, sh], name=name, compiler_params=_params("arbitrary", "arbitrary"),
    )(g0, g1, w, m, v)


def _peers():
    x, y, c = lax.axis_index("x"), lax.axis_index("y"), lax.axis_index("c")
    flips = ((0, 0, 1), (1, 0, 0), (0, 1, 0), (1, 1, 0), (1, 0, 1), (0, 1, 1), (1, 1, 1))
    me = 4 * x + 2 * y + c
    peers = [(x ^ fx, y ^ fy, c ^ fc) for fx, fy, fc in flips]
    return me, peers


def _exchange(arrs, scatter, name):
    n = len(arrs)
    npeer = N_DEV - 1

    def body(*refs):
        ins, outs = refs[:n], refs[n:2 * n]
        send_sems, recv_sems, local_sems = refs[2 * n:]
        me, peers = _peers()
        copies = []
        for a in range(n):
            src_own = ins[a].at[me] if scatter else ins[a]
            own = pltpu.make_async_copy(src_own, outs[a].at[me], local_sems.at[a])
            own.start()
            copies.append(own)
            for k, peer in enumerate(peers):
                src = ins[a].at[4 * peer[0] + 2 * peer[1] + peer[2]] if scatter else ins[a]
                cp = pltpu.make_async_remote_copy(
                    src_ref=src, dst_ref=outs[a].at[me], send_sem=send_sems.at[a * npeer + k],
                    recv_sem=recv_sems.at[a * npeer + k], device_id=peer, device_id_type=pl.DeviceIdType.MESH)
                cp.start()
                copies.append(cp)
        for cp in copies:
            cp.wait()

    out_shape = [jax.ShapeDtypeStruct(((N_DEV,) + a.shape[1:]) if scatter else ((N_DEV,) + a.shape), a.dtype) for a in arrs]
    anyspec = pl.BlockSpec(memory_space=pl.ANY)
    return pl.pallas_call(
        body, in_specs=[anyspec] * n, out_specs=[anyspec] * n, out_shape=out_shape,
        scratch_shapes=[pltpu.SemaphoreType.DMA((n * npeer,)), pltpu.SemaphoreType.DMA((n * npeer,)),
                        pltpu.SemaphoreType.DMA((n,))],
        name=name,
    )(*arrs)


def _dev_index(peer):
    return 4 * peer[0] + 2 * peer[1] + peer[2]


def _push_copies(src_refs, land_refs, send_sems, recv_sems, scatter, as_receiver):
    me, peers = _peers()
    npeer = N_DEV - 1
    copies = []
    for a in range(len(src_refs)):
        for k, peer in enumerate(peers):
            src = src_refs[a].at[_dev_index(peer)] if scatter else src_refs[a]
            slot = _dev_index(peer) if as_receiver else me
            copies.append(pltpu.make_async_remote_copy(
                src_ref=src, dst_ref=land_refs[a].at[slot], send_sem=send_sems.at[a * npeer + k],
                recv_sem=recv_sems.at[a * npeer + k], device_id=peer, device_id_type=pl.DeviceIdType.MESH))
    return copies


def _own_copies(src_refs, land_refs, own_sems, scatter):
    me, _ = _peers()
    return [pltpu.make_async_copy(src_refs[a].at[me] if scatter else src_refs[a], land_refs[a].at[me], own_sems.at[a])
            for a in range(len(src_refs))]


def _push_start(srcs, scatter, name):
    n = len(srcs)
    nsem = n * (N_DEV - 1)

    def body(*refs):
        srcs_r, lands_r = refs[:n], refs[2 * n + 3:3 * n + 3]
        for cp in _push_copies(srcs_r, lands_r, refs[n], refs[n + 1], scatter, False):
            cp.start()
        for cp in _own_copies(srcs_r, lands_r, refs[n + 2], scatter):
            cp.start()
        token = refs[-1]
        token[...] = jnp.zeros_like(token)

    hbm = pl.BlockSpec(memory_space=pltpu.HBM)
    sem = pl.BlockSpec(memory_space=pltpu.SEMAPHORE)
    lands = [pltpu.HBM((N_DEV,) + (s.shape[1:] if scatter else s.shape), s.dtype) for s in srcs]
    return pl.pallas_call(
        body, name=name,
        out_shape=(pltpu.SemaphoreType.DMA((nsem,)), pltpu.SemaphoreType.DMA((nsem,)), pltpu.SemaphoreType.DMA((n,)),
                   *[pltpu.HBM(a.shape, a.dtype) for a in srcs], *lands, jax.ShapeDtypeStruct((8, 128), F32)),
        in_specs=[hbm] * n, out_specs=(sem, sem, sem, *([hbm] * (2 * n)), pl.BlockSpec(memory_space=pltpu.VMEM)),
        input_output_aliases={i: 3 + i for i in range(n)},
        compiler_params=pltpu.CompilerParams(has_side_effects=pltpu.SideEffectType.DATAFLOW_SIDE_EFFECTING),
    )(*[pltpu.with_memory_space_constraint(a, pltpu.HBM) for a in srcs])


def _push_wait(handle, after, scatter, name):
    send_sems, recv_sems, own_sems, *thru, _ = handle
    n = len(thru) // 2

    def body(*refs):
        srcs_r, lands_r = refs[:n], refs[n:2 * n]
        for cp in _push_copies(srcs_r, lands_r, refs[2 * n], refs[2 * n + 1], scatter, True):
            cp.wait_send()
            cp.wait_recv()
        for cp in _own_copies(srcs_r, lands_r, refs[2 * n + 2], scatter):
            cp.wait()

    hbm = pl.BlockSpec(memory_space=pltpu.HBM)
    sem = pl.BlockSpec(memory_space=pltpu.SEMAPHORE)
    outs = pl.pallas_call(
        body, name=name, out_shape=tuple(pltpu.HBM(a.shape, a.dtype) for a in thru),
        in_specs=[hbm] * (2 * n) + [sem, sem, sem, pl.BlockSpec(memory_space=pl.ANY)], out_specs=tuple([hbm] * (2 * n)),
        input_output_aliases={i: i for i in range(2 * n)},
        compiler_params=pltpu.CompilerParams(has_side_effects=pltpu.SideEffectType.DATAFLOW_SIDE_EFFECTING),
    )(*thru, send_sems, recv_sems, own_sems, after)
    return list(outs[n:])


def _relay_copies(src_refs, land_refs, send1, recv1, send2, recv2, as_receiver):
    me, peers = _peers()
    sibling, chips = peers[0], peers[1:4]
    first, second = [], []
    for a in range(len(src_refs)):
        for k, peer in enumerate([sibling] + chips):
            slot = _dev_index(peer) if as_receiver else me
            first.append(pltpu.make_async_remote_copy(
                src_ref=src_refs[a], dst_ref=land_refs[a].at[slot], send_sem=send1.at[4 * a + k], recv_sem=recv1.at[4 * a + k],
                device_id=peer, device_id_type=pl.DeviceIdType.MESH))
        for k, chip in enumerate(chips):
            origin = _dev_index(chip)
            slot = origin ^ 1 if as_receiver else origin
            second.append(pltpu.make_async_remote_copy(
                src_ref=land_refs[a].at[origin], dst_ref=land_refs[a].at[slot], send_sem=send2.at[3 * a + k],
                recv_sem=recv2.at[3 * a + k], device_id=sibling, device_id_type=pl.DeviceIdType.MESH))
    return first, second


def _relay_start(srcs, name):
    n = len(srcs)

    def body(*refs):
        srcs_r, lands_r = refs[:n], refs[2 * n + 3:3 * n + 3]
        for cp in _relay_copies(srcs_r, lands_r, refs[n], refs[n + 1], refs[n], refs[n + 1], False)[0]:
            cp.start()
        for cp in _own_copies(srcs_r, lands_r, refs[n + 2], False):
            cp.start()
        token = refs[-1]
        token[...] = jnp.zeros_like(token)

    hbm = pl.BlockSpec(memory_space=pltpu.HBM)
    sem = pl.BlockSpec(memory_space=pltpu.SEMAPHORE)
    return pl.pallas_call(
        body, name=name,
        out_shape=(pltpu.SemaphoreType.DMA((4 * n,)), pltpu.SemaphoreType.DMA((4 * n,)), pltpu.SemaphoreType.DMA((n,)),
                   *[pltpu.HBM(s.shape, s.dtype) for s in srcs], *[pltpu.HBM((N_DEV,) + s.shape, s.dtype) for s in srcs],
                   jax.ShapeDtypeStruct((8, 128), F32)),
        in_specs=[hbm] * n, out_specs=(sem, sem, sem, *([hbm] * (2 * n)), pl.BlockSpec(memory_space=pltpu.VMEM)),
        input_output_aliases={i: 3 + i for i in range(n)},
        compiler_params=pltpu.CompilerParams(has_side_effects=pltpu.SideEffectType.DATAFLOW_SIDE_EFFECTING),
    )(*[pltpu.with_memory_space_constraint(s, pltpu.HBM) for s in srcs])


def _relay_forward(handle, after, name):
    _, recv1, _, *thru, _ = handle
    n = len(thru) // 2

    def body(*refs):
        srcs_r, lands_r, recv1_r = refs[:n], refs[n:2 * n], refs[2 * n]
        send2_r, recv2_r = refs[2 * n + 2], refs[2 * n + 3]
        first, second = _relay_copies(srcs_r, lands_r, recv1_r, recv1_r, send2_r, recv2_r, True)
        for a in range(n):
            for k in range(1, 4):
                first[4 * a + k].wait_recv()
        for cp in _relay_copies(srcs_r, lands_r, recv1_r, recv1_r, send2_r, recv2_r, False)[1]:
            cp.start()
        token = refs[-1]
        token[...] = jnp.zeros_like(token)

    hbm = pl.BlockSpec(memory_space=pltpu.HBM)
    sem = pl.BlockSpec(memory_space=pltpu.SEMAPHORE)
    return pl.pallas_call(
        body, name=name,
        out_shape=(pltpu.SemaphoreType.DMA((3 * n,)), pltpu.SemaphoreType.DMA((3 * n,)),
                   *[pltpu.HBM(a.shape, a.dtype) for a in thru], jax.ShapeDtypeStruct((8, 128), F32)),
        in_specs=[hbm] * (2 * n) + [sem, pl.BlockSpec(memory_space=pl.ANY)],
        out_specs=(sem, sem, *([hbm] * (2 * n)), pl.BlockSpec(memory_space=pltpu.VMEM)),
        input_output_aliases={i: 2 + i for i in range(2 * n)},
        compiler_params=pltpu.CompilerParams(has_side_effects=pltpu.SideEffectType.DATAFLOW_SIDE_EFFECTING),
    )(*thru, recv1, after)


def _relay_wait(handle, forwarded, after, name):
    send1, recv1, own_sems, *_ = handle
    send2, recv2, *thru, _ = forwarded
    n = len(thru) // 2

    def body(*refs):
        srcs_r, lands_r = refs[:n], refs[n:2 * n]
        send1_r, recv1_r, own_r, send2_r, recv2_r = refs[2 * n:2 * n + 5]
        first, second = _relay_copies(srcs_r, lands_r, send1_r, recv1_r, send2_r, recv2_r, True)
        for i, cp in enumerate(first):
            cp.wait_send()
            if i % 4 == 0:
                cp.wait_recv()
        for cp in second:
            cp.wait_send()
            cp.wait_recv()
        for cp in _own_copies(srcs_r, lands_r, own_r, False):
            cp.wait()

    hbm = pl.BlockSpec(memory_space=pltpu.HBM)
    sem = pl.BlockSpec(memory_space=pltpu.SEMAPHORE)
    outs = pl.pallas_call(
        body, name=name, out_shape=tuple(pltpu.HBM(a.shape, a.dtype) for a in thru),
        in_specs=[hbm] * (2 * n) + [sem] * 5 + [pl.BlockSpec(memory_space=pl.ANY)], out_specs=tuple([hbm] * (2 * n)),
        input_output_aliases={i: i for i in range(2 * n)},
        compiler_params=pltpu.CompilerParams(has_side_effects=pltpu.SideEffectType.DATAFLOW_SIDE_EFFECTING),
    )(*thru, send1, recv1, own_sems, send2, recv2, after)
    return list(outs[n:])


def _tables(S):
    pos = jnp.arange(S, dtype=F32)
    inv = ROPE_THETA ** (-jnp.arange(0, ATT_HEAD_DIM, 2, dtype=F32) / ATT_HEAD_DIM)
    ang = pos[:, None] * inv[None, :]
    cos, sin = jnp.cos(ang), jnp.sin(ang)
    cos128 = jnp.tile(cos, (1, 4))
    sin128 = jnp.tile(jnp.concatenate([-sin, sin], axis=1), (1, 2))
    lane = np.arange(ATT_WIDTH)
    bd = jnp.asarray((lane[:, None] // 64 == lane[None, :] // 64).astype(np.float32), dtype=BF16)
    return cos128, sin128, bd, jnp.asarray(_att_bias())


def _layer_fwd(l, x, p, tabs, early=None, late=None):
    cos, sin, bd, bias = tabs
    S = x.shape[0]
    row = lambda v: v.reshape(1, -1)
    hn = _rmsnorm_fwd(x, row(p["ln_mix"]), f"norm_mix_fwd{l}")
    if early is not None:
        p.update(early(hn))
    proj = _mm(hn, p["w_in"], "nn", f"in_proj{l}", tn=1920)
    xbc = _conv_fwd(proj, p["conv_w"], row(p["conv_b"]), f"conv_fwd{l}")
    dtr = proj[:, DT_COL:DT_COL + SSD_HEADS].reshape(S, SSD_GROUPS, HPG).transpose(1, 0, 2)
    grp = lambda v: v.reshape(SSD_GROUPS, 1, HPG)
    d_rep = row(jnp.repeat(p["d_skip"], SSD_HEAD_DIM))
    y, yraw, ssd_st = _ssd_fwd(xbc, proj, dtr, grp(p["dt_bias"]), grp(p["a_log"]), d_rep, row(p["ssd_norm"]), f"ssd_fwd{l}")
    qg = row(jnp.tile(p["q_norm"], ATT_HEADS))
    kg = row(jnp.tile(p["k_norm"], ATT_HEADS))
    aq, akp, avp = _att_prep_fwd(proj, qg, kg, cos, sin, bd, f"att_prep_fwd{l}")
    y = _att_fwd(aq, akp, avp, bias, y, f"att_fwd{l}")
    y, oraw, ret_st = _ret_fwd(proj, cos, sin, row(p["ret_norm"]), y, f"ret_fwd{l}")
    if late is not None:
        p.update(late(y))
    x1 = _mm(y, p["w_out"], "nn", f"out_proj{l}", residual=x)
    hn2 = _rmsnorm_fwd(x1, row(p["ln_ffn"]), f"norm_ffn_fwd{l}")
    g, u, act = _swiglu_fwd(hn2, p["w_gate"], p["w_up"], f"swiglu_fwd{l}")
    x2 = _mm(act, p["w_down"], "nn", f"down_proj{l}", residual=x1, tk=2816)
    saved = dict(x=x, hn=hn, proj=proj, xbc=xbc, dtr=dtr, yraw=yraw, ssd_st=ssd_st, aq=aq, akp=akp, avp=avp,
                 oraw=oraw, ret_st=ret_st, y=y, x1=x1, hn2=hn2, g=g, u=u, act=act, d_rep=d_rep, qg=qg, kg=kg)
    return x2, saved


def _layer_bwd(l, dx2, dx2_bf, p, sv, tabs, on_ffn=None, on_all=None):
    cos, sin, bd, bias = tabs
    S = dx2.shape[0]
    row = lambda v: v.reshape(1, -1)
    grp = lambda v: v.reshape(SSD_GROUPS, 1, HPG)
    gr = {}
    dg, du = _swiglu_bwd(dx2_bf, p["w_down"], sv["g"], sv["u"], f"swiglu_bwd{l}")
    gr["w_down"] = _mm(sv["act"], dx2_bf, "tn", f"down_wgrad{l}", out_dtype=BF16, tm=1408, tn=1024, tk=2048)
    dhn2 = _mm_nt2(dg, p["w_gate"], du, p["w_up"], f"ffn_dgrad{l}")
    gr["w_gate"] = _mm(sv["hn2"], dg, "tn", f"gate_wgrad{l}", out_dtype=BF16, tm=512, tn=2816, tk=2048)
    gr["w_up"] = _mm(sv["hn2"], du, "tn", f"up_wgrad{l}", out_dtype=BF16, tm=512, tn=2816, tk=2048)
    ffn_gain = row(p["ln_ffn"]) + (on_ffn(gr)[0, 0] if on_ffn is not None else 0.0)
    dx1, dx1_bf, dln_ffn = _rmsnorm_bwd(sv["x1"], dhn2, ffn_gain, dx2, f"norm_ffn_bwd{l}")
    gr["ln_ffn"] = dln_ffn[0]
    dy = _mm(dx1_bf, p["w_out"], "nt", f"out_dgrad{l}")
    gr["w_out"] = _mm(sv["y"], dx1_bf, "tn", f"out_wgrad{l}", out_dtype=BF16, tm=1024, tn=1024, tk=2048)
    dproj, dxs, dbm, dcm, ddtr, dbias, dalog, dd, dssd_gain = _ssd_bwd(
        dy, sv["yraw"], sv["xbc"], sv["proj"], sv["dtr"], grp(p["dt_bias"]), grp(p["a_log"]), sv["d_rep"],
        row(p["ssd_norm"]), sv["ssd_st"], f"ssd_bwd{l}")
    gr["dt_bias"], gr["a_log"], gr["d_skip"] = dbias.reshape(-1), dalog.reshape(-1), dd.reshape(-1)
    gr["ssd_norm"] = dssd_gain[0]
    dproj, dconv_w, dconv_b = _conv_bwd(dxs, dbm, dcm, sv["proj"], p["conv_w"], row(p["conv_b"]), dproj, f"conv_bwd{l}")
    gr["conv_w"], gr["conv_b"] = dconv_w, dconv_b[0]
    dq, dk_p, dv_p = _att_bwd(sv["aq"], sv["akp"], sv["avp"], bias, dy, f"att_bwd{l}")
    dproj, dqg, dkg = _att_prep_bwd(sv["proj"], dq, dk_p, dv_p, sv["qg"], sv["kg"], cos, sin, bd, dproj, f"att_prep_bwd{l}")
    gr["q_norm"], gr["k_norm"] = dqg[0], dkg[0]
    dproj, dret_gain = _ret_bwd(dy, sv["oraw"], sv["proj"], cos, sin, row(p["ret_norm"]), sv["ret_st"], dproj, f"ret_bwd{l}")
    gr["ret_norm"] = dret_gain[0]
    ddt_cols = ddtr.transpose(1, 0, 2).reshape(S, SSD_HEADS).astype(BF16)
    dproj = lax.dynamic_update_slice(dproj, jnp.pad(ddt_cols, ((0, 0), (0, IN_PAD - DT_COL - SSD_HEADS))), (0, DT_COL))
    gr["w_in"] = _mm(sv["hn"], dproj, "tn", f"in_wgrad{l}", out_dtype=BF16, tm=1024, tn=1920, tk=2048)
    launched = on_all(gr) if on_all is not None else None
    dhn = _mm(dproj, p["w_in"], "nt", f"in_dgrad{l}", tm=512, tk=IN_PAD, after=launched)
    dx0, dx0_bf, dln_mix = _rmsnorm_bwd(sv["x"], dhn, row(p["ln_mix"]), dx1, f"norm_mix_bwd{l}")
    gr["ln_mix"] = dln_mix[0]
    return dx0, dx0_bf, gr


def _local_step(x, tgt, layers, early=None, late=None, on_ffn=None, on_all=None):
    n = len(layers)
    none = [None] * n
    early, late, on_ffn, on_all = early or none, late or none, on_ffn or none, on_all or none
    tabs = _tables(x.shape[0])
    saved, params = [], []
    h = x
    for l in range(n):
        p = dict(layers[l](h) if callable(layers[l]) else layers[l])
        h, sv = _layer_fwd(l, h, p, tabs, early[l], late[l])
        saved.append(sv)
        params.append(p)
    dh, dh_bf, lacc = _loss_grad(h, tgt, "loss_grad")
    grads = [None] * n
    for l in reversed(range(n)):
        dh, dh_bf, grads[l] = _layer_bwd(l, dh, dh_bf, params[l], saved[l], tabs, on_ffn[l], on_all[l])
    return lacc[0, 0], dh, grads


BIG = ("w_in", "w_out", "w_gate", "w_up", "w_down")
SMALL = ("ln_mix", "conv_b", "dt_bias", "a_log", "d_skip", "ssd_norm", "q_norm", "k_norm", "ret_norm", "ln_ffn")
ORDER = ("ln_mix", "w_in", "conv_w", "conv_b", "dt_bias", "a_log", "d_skip", "ssd_norm", "q_norm", "k_norm", "ret_norm",
         "w_out", "ln_ffn", "w_gate", "w_up", "w_down")


COL_SHARDED = ("w_in", "w_gate", "w_up", "conv_w")


IN_GROUPS = ((ORIG_Z_XBC, Z_COL), (ORIG_DT, DT_COL), (ORIG_ATT, ATT_COL), (ORIG_RET, RET_COL))


def _full_weight(k, gathered):
    if k == "w_in":
        cs = gathered.shape[2]
        pieces = []
        for (lo, hi), _ in sorted(IN_GROUPS, key=lambda grp: grp[1]):
            for j in range(N_DEV):
                a, b = max(lo, j * cs), min(hi, (j + 1) * cs)
                if a < b:
                    pieces.append(gathered[j][:, a - j * cs:b - j * cs])
        pieces.append(jnp.zeros((gathered.shape[1], IN_PAD - IN_WIDTH), gathered.dtype))
        return jnp.concatenate(pieces, axis=1)
    if k in COL_SHARDED:
        return gathered.transpose(1, 0, 2).reshape(gathered.shape[1], -1)
    return gathered.reshape(-1, gathered.shape[2])


def _shard_block(k, g):
    if k == "w_in":
        cs = IN_WIDTH // N_DEV
        blocks = []
        for j in range(N_DEV):
            pieces = []
            for (lo, hi), col in IN_GROUPS:
                a, b = max(lo, j * cs), min(hi, (j + 1) * cs)
                if a < b:
                    pieces.append(g[:, col + a - lo:col + b - lo])
            blocks.append(jnp.concatenate(pieces, axis=1))
        return jnp.stack(blocks)
    if k in COL_SHARDED:
        return g.reshape(g.shape[0], N_DEV, -1).transpose(1, 0, 2)
    return g.reshape(N_DEV, -1, g.shape[1])


def kernel(x, ln_mix, w_in, conv_w, conv_b, dt_bias, a_log, d_skip, ssd_norm, q_norm, k_norm, ret_norm, w_out, ln_ffn, w_gate, w_up, w_down, loss_target, m_ln_mix, m_w_in, m_conv_w, m_conv_b, m_dt_bias, m_a_log, m_d_skip, m_ssd_norm, m_q_norm, m_k_norm, m_ret_norm, m_w_out, m_ln_ffn, m_w_gate, m_w_up, m_w_down, v_ln_mix, v_w_in, v_conv_w, v_conv_b, v_dt_bias, v_a_log, v_d_skip, v_ssd_norm, v_q_norm, v_k_norm, v_ret_norm, v_w_out, v_ln_ffn, v_w_gate, v_w_up, v_w_down):
    w = dict(ln_mix=ln_mix, w_in=w_in, conv_w=conv_w, conv_b=conv_b, dt_bias=dt_bias, a_log=a_log, d_skip=d_skip,
             ssd_norm=ssd_norm, q_norm=q_norm, k_norm=k_norm, ret_norm=ret_norm, w_out=w_out, ln_ffn=ln_ffn,
             w_gate=w_gate, w_up=w_up, w_down=w_down)
    m = dict(ln_mix=m_ln_mix, w_in=m_w_in, conv_w=m_conv_w, conv_b=m_conv_b, dt_bias=m_dt_bias, a_log=m_a_log,
             d_skip=m_d_skip, ssd_norm=m_ssd_norm, q_norm=m_q_norm, k_norm=m_k_norm, ret_norm=m_ret_norm, w_out=m_w_out,
             ln_ffn=m_ln_ffn, w_gate=m_w_gate, w_up=m_w_up, w_down=m_w_down)
    v = dict(ln_mix=v_ln_mix, w_in=v_w_in, conv_w=v_conv_w, conv_b=v_conv_b, dt_bias=v_dt_bias, a_log=v_a_log,
             d_skip=v_d_skip, ssd_norm=v_ssd_norm, q_norm=v_q_norm, k_norm=v_k_norm, ret_norm=v_ret_norm, w_out=v_w_out,
             ln_ffn=v_ln_ffn, w_gate=v_w_gate, w_up=v_w_up, w_down=v_w_down)
    me = 4 * lax.axis_index("x") + 2 * lax.axis_index("y") + lax.axis_index("c")

    late_names = ("w_out", "w_gate", "w_up", "w_down")
    waves = {"a": [("w_in", 0), ("conv_w", 0), ("conv_w", 1)], "b": [(k, 0) for k in late_names],
             "c": [("w_in", 1)], "d": [(k, 1) for k in late_names]}
    gather = {}
    behind = 0.0
    for tag, items in waves.items():
        srcs = [w[k][l] if k == "conv_w" else (w[k][l] + behind).astype(BF16) for k, l in items]
        start = _relay_start(srcs, f"gather_{tag}_start") if tag == "a" else _push_start(srcs, False, f"gather_{tag}_start")
        gather[tag] = start
        behind = start[-1][0, 0]
    started = behind
    full = {}

    def arrive(tag, after):
        if tag == "a":
            forwarded = _relay_forward(gather["a"], after, "gather_a_forward")
            lands = _relay_wait(gather["a"], forwarded, forwarded[-1], "gather_a_wait")
        else:
            lands = _push_wait(gather[tag], after, False, f"gather_{tag}_wait")
        for (k, l), g in zip(waves[tag], lands):
            full[k, l] = _full_weight(k, g)

    def layer_weights(l, names):
        return {k: full[k, l] for k in names}

    def small_weights(l):
        return {k: w[k][l] for k in SMALL}

    def layer0(h):
        p = small_weights(0)
        p["ln_mix"] = p["ln_mix"] + started
        return p

    def early0(hn):
        arrive("a", hn)
        return layer_weights(0, ("w_in", "conv_w"))

    def late0(y):
        arrive("b", y)
        return layer_weights(0, late_names)

    def layer1(h):
        arrive("c", h)
        return {**small_weights(1), **layer_weights(1, ("w_in", "conv_w"))}

    def late1(y):
        arrive("d", y)
        return layer_weights(1, late_names)

    groups = {"1": [(k, 1) for k in BIG], "0a": [(k, 0) for k in ("w_down", "w_gate", "w_up")],
              "0b": [(k, 0) for k in ("w_out", "w_in")]}
    scatter = {}

    def push_grads(tag, gr):
        blocks = [_shard_block(k, gr[k]) for k, _ in groups[tag]]
        scatter[tag] = _push_start(blocks, True, f"scatter_{tag}_start")
        return scatter[tag][-1]

    loss_part, gx, grads = _local_step(
        x[0], loss_target[0], [layer0, layer1], early=[early0, None], late=[late0, late1],
        on_ffn=[functools.partial(push_grads, "0a"), None],
        on_all=[functools.partial(push_grads, "0b"), functools.partial(push_grads, "1")])
    loss = lax.psum(loss_part, MESH_AXES)

    out = {}
    recv = {}
    for tag, items in groups.items():
        for item, r in zip(items, _push_wait(scatter[tag], gx, True, f"scatter_{tag}_wait")):
            recv[item] = r
    for k in BIG:
        out[k] = _adamw_layers(recv[k, 0], recv[k, 1], w[k], m[k], v[k], f"adamw_{k}")
    names = SMALL + ("conv_w",)
    sizes = [int(np.prod(grads[0][k].shape)) for k in names]
    packed = jnp.concatenate([jnp.stack([grads[l][k] for l in range(DEPTH)]).reshape(-1) for k in names])
    n_small = packed.shape[0]
    rows_small = -(-n_small // 1024) * 8
    pad = lambda t, fill: jnp.concatenate([t, jnp.full((rows_small * 128 - n_small,), fill, F32)]).reshape(rows_small, 128)
    parts = _exchange([pad(packed, 0.0)], False, "gather_small_grads")[0]
    n_rep = DEPTH * sum(sizes[:-1])
    pack_rep = lambda d, fill: pad(jnp.concatenate([d[k].reshape(-1) for k in SMALL]
                                                   + [jnp.full((n_small - n_rep,), fill, F32)]), fill)
    res = _adamw(parts, pack_rep(w, 1.0), pack_rep(m, 1.0), pack_rep(v, 1.0), "adamw_small")
    res = [t.reshape(-1) for t in res]
    off = 0
    for k, sz in zip(SMALL, sizes[:-1]):
        out[k] = [t[off:off + DEPTH * sz].reshape(w[k].shape) for t in res]
        off += DEPTH * sz
    gconv = res[0][off:off + DEPTH * sizes[-1]].reshape(DEPTH, SSD_CONV, SSD_CONV_CH)
    gconv = lax.dynamic_slice_in_dim(gconv, me * conv_w.shape[2], conv_w.shape[2], axis=2)
    flat = lambda t: t.reshape(8, -1)
    resc = _adamw(flat(gconv)[None], flat(conv_w), flat(m_conv_w), flat(v_conv_w), "adamw_conv_w")
    out["conv_w"] = [t.reshape(conv_w.shape) for t in resc]

    return (loss, gx[None], *[out[k][0] for k in ORDER], *[out[k][1] for k in ORDER],
            *[out[k][2] for k in ORDER], *[out[k][3] for k in ORDER])
```

```python
import functools
import math

import jax
import jax.numpy as jnp
import numpy as np
from jax import lax
from jax.experimental import pallas as pl
from jax.experimental.pallas import tpu as pltpu

F32 = jnp.float32
BF16 = jnp.bfloat16

N_DEV = 8
MESH_AXES = ("x", "y", "c")
D_MODEL = 2048
DEPTH = 2
EPS = 1e-6
ROPE_THETA = 10000.0
SSD_HEADS = 16
SSD_HEAD_DIM = 64
SSD_WIDTH = 1024
SSD_GROUPS = 2
SSD_STATE = 128
SSD_CONV = 4
SSD_CONV_CH = 1536
ATT_HEADS = 8
ATT_HEAD_DIM = 64
ATT_WIDTH = 512
DILATED_PAIRS = ((128, 1), (512, 4), (2048, 16))
RET_HEADS = 4
RET_QK_DIM = 64
RET_V_DIM = 128
RET_QK_WIDTH = 256
RET_V_WIDTH = 512
CHUNK = 128
MIX_WIDTH = 2048
ATT_SPAN = 2048
ATT_STRIP = ATT_SPAN + CHUNK
ATT_QB = 8
IN_WIDTH = 5648
IN_PAD = 5760
RET_COL, ATT_COL, Z_COL, XBC_COL, DT_COL = 0, 1536, 3072, 4096, 5632
ORIG_Z_XBC, ORIG_DT, ORIG_ATT, ORIG_RET = (0, 2560), (2560, 2576), (2576, 4112), (4112, 5648)
D_FF = 5632
ADAM_LR = 0.001
ADAM_B1 = 0.9
ADAM_B2 = 0.999
ADAM_EPS = 1e-08
ADAM_WD = 0.01
ADAM_STEP = 10
NEG = -1e30
VMEM_LIMIT_V7X = 60 * 1024 * 1024

NN = (((1,), (0,)), ((), ()))
NT = (((1,), (1,)), ((), ()))
TN = (((0,), (0,)), ((), ()))


def _bdot(a, b, dims):
    return lax.dot_general(a.astype(BF16), b.astype(BF16), dims, preferred_element_type=F32)


def _xdot(a, b, dims, exact_first=False, pieces=3):
    ones, x = (a, b) if exact_first else (b, a)
    ones = ones.astype(BF16)
    acc, rest = None, x
    for _ in range(pieces):
        piece = rest.astype(BF16)
        rest = rest - piece.astype(F32)
        part = lax.dot_general(*((ones, piece) if exact_first else (piece, ones)), dims, preferred_element_type=F32)
        acc = part if acc is None else acc + part
    return acc


def _params(*sem):
    return pltpu.CompilerParams(dimension_semantics=sem, vmem_limit_bytes=VMEM_LIMIT_V7X)


def _sigmoid(v):
    return 1.0 / (1.0 + jnp.exp(-v))


def _silu_grad(v, s):
    return s * (1.0 + v * (1.0 - s))


def _rmsnorm_fwd(x, g, name):
    S, D = x.shape
    tr = min(512, S)

    def body(x_ref, g_ref, o_ref):
        xv = x_ref[...]
        r = lax.rsqrt(jnp.mean(xv * xv, axis=-1, keepdims=True) + EPS)
        o_ref[...] = (xv * r * g_ref[...]).astype(o_ref.dtype)

    return pl.pallas_call(
        body, grid=(S // tr,),
        in_specs=[pl.BlockSpec((tr, D), lambda i: (i, 0)), pl.BlockSpec((1, D), lambda i: (0, 0))],
        out_specs=pl.BlockSpec((tr, D), lambda i: (i, 0)),
        out_shape=jax.ShapeDtypeStruct((S, D), BF16), name=name, compiler_params=_params("parallel"),
    )(x, g)


def _rmsnorm_bwd(x, dy, g, dres, name):
    S, D = x.shape
    tr = min(512, S)

    def body(x_ref, dy_ref, g_ref, dres_ref, dx_ref, dxb_ref, dg_ref):
        i = pl.program_id(0)
        xv = x_ref[...]
        r = lax.rsqrt(jnp.mean(xv * xv, axis=-1, keepdims=True) + EPS)
        n = xv * r
        dy = dy_ref[...]
        dn = dy * g_ref[...]
        dx = dres_ref[...] + r * (dn - n * jnp.mean(dn * n, axis=-1, keepdims=True))
        dx_ref[...] = dx
        dxb_ref[...] = dx.astype(BF16)
        part = jnp.sum(dy * n, axis=0, keepdims=True)

        @pl.when(i == 0)
        def _():
            dg_ref[...] = part

        @pl.when(i > 0)
        def _():
            dg_ref[...] += part

    row = pl.BlockSpec((tr, D), lambda i: (i, 0))
    vec = pl.BlockSpec((1, D), lambda i: (0, 0))
    return pl.pallas_call(
        body, grid=(S // tr,), in_specs=[row, row, vec, row], out_specs=[row, row, vec],
        out_shape=[jax.ShapeDtypeStruct((S, D), F32), jax.ShapeDtypeStruct((S, D), BF16), jax.ShapeDtypeStruct((1, D), F32)],
        name=name, compiler_params=_params("arbitrary"),
    )(x, dy, g, dres)


def _loss_grad(y, tgt, name):
    S, D = y.shape
    tr = min(512, S)

    def body(y_ref, t_ref, dy_ref, dyb_ref, l_ref):
        i = pl.program_id(0)
        err = y_ref[...] - t_ref[...]
        dy = err * (1.0 / D)
        dy_ref[...] = dy
        dyb_ref[...] = dy.astype(BF16)
        part = jnp.sum(jnp.sum(err * err, axis=1, keepdims=True), axis=0, keepdims=True) * (0.5 / D)

        @pl.when(i == 0)
        def _():
            l_ref[...] = jnp.zeros_like(l_ref)

        l_ref[...] += part

    row = pl.BlockSpec((tr, D), lambda i: (i, 0))
    return pl.pallas_call(
        body, grid=(S // tr,), in_specs=[row, row],
        out_specs=[row, row, pl.BlockSpec((8, 128), lambda i: (0, 0))],
        out_shape=[jax.ShapeDtypeStruct((S, D), F32), jax.ShapeDtypeStruct((S, D), BF16), jax.ShapeDtypeStruct((8, 128), F32)],
        name=name, compiler_params=_params("arbitrary"),
    )(y, tgt)


def _pick(n, cands):
    for c in cands:
        if n % c == 0:
            return c
    return n


def _mm(a, b, mode, name, out_dtype=F32, residual=None, tm=None, tn=None, tk=None, after=None):
    if mode == "nn":
        (M, K), (_, N) = a.shape, b.shape
    elif mode == "nt":
        (M, K), (N, _) = a.shape, b.shape
    else:
        (K, M), (_, N) = a.shape, b.shape
    tm = min(tm, M) if tm else _pick(M, (1024, 512, 256, 128))
    tn = min(tn, N) if tn else _pick(N, (1024, 1152, 1408, 512, 256, 128))
    tk = min(tk, K) if tk else _pick(K, (2048, 1920, 1408, 1024, 512, 256, 128))
    assert M % tm == 0 and N % tn == 0 and K % tk == 0, (name, M, N, K, tm, tn, tk)
    nk = K // tk
    a_spec = pl.BlockSpec((tk, tm), lambda i, j, k: (k, i)) if mode == "tn" else pl.BlockSpec((tm, tk), lambda i, j, k: (i, k))
    b_spec = pl.BlockSpec((tn, tk), lambda i, j, k: (j, k)) if mode == "nt" else pl.BlockSpec((tk, tn), lambda i, j, k: (k, j))
    o_spec = pl.BlockSpec((tm, tn), lambda i, j, k: (i, j))
    dims = {"nn": NN, "nt": NT, "tn": TN}[mode]
    has_res = residual is not None

    has_after = after is not None

    def body(*refs):
        a_ref, b_ref = refs[0], refs[1]
        r_ref = refs[2] if has_res else None
        o_ref = refs[2 + has_res + has_after]
        p = _bdot(a_ref[...], b_ref[...], dims)

        def finish(acc):
            if has_res:
                acc = acc + r_ref[...]
            o_ref[...] = acc.astype(o_ref.dtype)

        if nk == 1:
            finish(p)
        else:
            acc_ref = refs[-1]
            k = pl.program_id(2)

            @pl.when(k == 0)
            def _():
                acc_ref[...] = p

            @pl.when(k > 0)
            def _():
                acc_ref[...] += p

            @pl.when(k == nk - 1)
            def _():
                finish(acc_ref[...])

    ins = [a, b] + ([residual] if has_res else []) + ([after] if has_after else [])
    in_specs = [a_spec, b_spec] + ([o_spec] if has_res else []) + ([pl.BlockSpec(memory_space=pl.ANY)] if has_after else [])
    scratch = [pltpu.VMEM((tm, tn), F32)] if nk > 1 else []
    return pl.pallas_call(
        body, grid=(M // tm, N // tn, nk), in_specs=in_specs, out_specs=o_spec,
        out_shape=jax.ShapeDtypeStruct((M, N), out_dtype), scratch_shapes=scratch, name=name,
        compiler_params=_params("parallel", "parallel", "arbitrary"),
    )(*ins)


def _accumulate(acc_ref, p, k, nk, finish):
    @pl.when(k == 0)
    def _():
        acc_ref[...] = p

    @pl.when(k > 0)
    def _():
        acc_ref[...] += p

    @pl.when(k == nk - 1)
    def _():
        finish(acc_ref[...])


def _swiglu_fwd(hn, wg, wu, name):
    S, K = hn.shape
    F = wg.shape[1]
    tm = _pick(S, (1024, 512))
    tn = _pick(F, (512, 256, 128))

    def body(a_ref, wg_ref, wu_ref, g_ref, u_ref, act_ref):
        a = a_ref[...]
        g = _bdot(a, wg_ref[...], NN)
        u = _bdot(a, wu_ref[...], NN)
        g_ref[...] = g.astype(BF16)
        u_ref[...] = u.astype(BF16)
        act_ref[...] = (g * _sigmoid(g) * u).astype(BF16)

    w_spec = pl.BlockSpec((K, tn), lambda i, j: (0, j))
    o_spec = pl.BlockSpec((tm, tn), lambda i, j: (i, j))
    sh = jax.ShapeDtypeStruct((S, F), BF16)
    return pl.pallas_call(
        body, grid=(S // tm, F // tn), in_specs=[pl.BlockSpec((tm, K), lambda i, j: (i, 0)), w_spec, w_spec],
        out_specs=[o_spec, o_spec, o_spec], out_shape=[sh, sh, sh], name=name,
        compiler_params=_params("parallel", "parallel"),
    )(hn, wg, wu)


def _swiglu_bwd(dx, wd, g, u, name):
    S, K = dx.shape
    F = wd.shape[0]
    tm = _pick(S, (1024, 512))
    tn = _pick(F, (512, 256, 128))

    def body(dx_ref, wd_ref, g_ref, u_ref, dg_ref, du_ref):
        da = _bdot(dx_ref[...], wd_ref[...], NT)
        gv = g_ref[...].astype(F32)
        uv = u_ref[...].astype(F32)
        s = _sigmoid(gv)
        dg_ref[...] = (da * uv * _silu_grad(gv, s)).astype(BF16)
        du_ref[...] = (da * gv * s).astype(BF16)

    o_spec = pl.BlockSpec((tm, tn), lambda i, j: (i, j))
    sh = jax.ShapeDtypeStruct((S, F), BF16)
    return pl.pallas_call(
        body, grid=(S // tm, F // tn),
        in_specs=[pl.BlockSpec((tm, K), lambda i, j: (i, 0)), pl.BlockSpec((tn, K), lambda i, j: (j, 0)), o_spec, o_spec],
        out_specs=[o_spec, o_spec], out_shape=[sh, sh], name=name, compiler_params=_params("parallel", "parallel"),
    )(dx, wd, g, u)


def _mm_nt2(a1, b1, a2, b2, name):
    M, K = a1.shape
    N = b1.shape[0]
    tm = _pick(M, (512,))
    tn = _pick(N, (1024, 512))
    tk = _pick(K, (2816, 1024, 512, 256, 128))
    nk = K // tk

    def body(a1_ref, b1_ref, a2_ref, b2_ref, o_ref, acc_ref):
        def finish(acc):
            o_ref[...] = acc

        p = _bdot(a1_ref[...], b1_ref[...], NT) + _bdot(a2_ref[...], b2_ref[...], NT)
        _accumulate(acc_ref, p, pl.program_id(2), nk, finish)

    a_spec = pl.BlockSpec((tm, tk), lambda i, j, k: (i, k))
    b_spec = pl.BlockSpec((tn, tk), lambda i, j, k: (j, k))
    return pl.pallas_call(
        body, grid=(M // tm, N // tn, nk), in_specs=[a_spec, b_spec, a_spec, b_spec],
        out_specs=pl.BlockSpec((tm, tn), lambda i, j, k: (i, j)), out_shape=jax.ShapeDtypeStruct((M, N), F32),
        scratch_shapes=[pltpu.VMEM((tm, tn), F32)], name=name,
        compiler_params=_params("parallel", "parallel", "arbitrary"),
    )(a1, b1, a2, b2)


XBC_BLK0 = XBC_COL // 128


def _conv_fwd(proj, w, b, name):
    S = proj.shape[0]
    T = min(512, S)

    def body(x_ref, w_ref, b_ref, o_ref, xp_ref):
        xp_ref[pl.ds(0, 8), :] = jnp.zeros((8, 128), F32)
        xp_ref[pl.ds(8, S), :] = x_ref[...]
        wv = w_ref[...]
        bv = b_ref[...]

        def step(c, carry):
            base = pl.multiple_of(c * T, T)
            acc = wv[0:1] * xp_ref[pl.ds(base + 5, T), :]
            for i in range(1, SSD_CONV):
                acc = acc + wv[i:i + 1] * xp_ref[pl.ds(base + 5 + i, T), :]
            acc = bv + acc
            o_ref[pl.ds(base, T), :] = acc * _sigmoid(acc)
            return carry

        lax.fori_loop(0, S // T, step, 0)

    return pl.pallas_call(
        body, grid=(SSD_CONV_CH // 128,),
        in_specs=[pl.BlockSpec((S, 128), lambda j: (0, XBC_BLK0 + j)), pl.BlockSpec((SSD_CONV, 128), lambda j: (0, j)),
                  pl.BlockSpec((1, 128), lambda j: (0, j))],
        out_specs=pl.BlockSpec((S, 128), lambda j: (0, j)),
        out_shape=jax.ShapeDtypeStruct((S, SSD_CONV_CH), F32),
        scratch_shapes=[pltpu.VMEM((S + 8, 128), F32)], name=name, compiler_params=_params("parallel"),
    )(proj, w, b)


def _conv_bwd(dxs, dbm, dcm, proj, w, b, dproj, name):
    S = proj.shape[0]
    T = min(512, S)
    NX, NB = SSD_WIDTH // 128, SSD_GROUPS * SSD_STATE // 128

    def body(dxs_ref, dbm_ref, dcm_ref, x_ref, w_ref, b_ref, dproj_ref, dx_ref, dw_ref, db_ref, xp_ref, dcp_ref):
        j = pl.program_id(0)

        @pl.when(j < NX)
        def _():
            dcp_ref[pl.ds(0, S), :] = dxs_ref[...]

        @pl.when((j >= NX) & (j < NX + NB))
        def _():
            dcp_ref[pl.ds(0, S), :] = dbm_ref[...]

        @pl.when(j >= NX + NB)
        def _():
            dcp_ref[pl.ds(0, S), :] = dcm_ref[...]

        da_ref = dcp_ref
        xp_ref[pl.ds(0, 8), :] = jnp.zeros((8, 128), F32)
        xp_ref[pl.ds(8, S), :] = x_ref[...]
        dcp_ref[pl.ds(S, 8), :] = jnp.zeros((8, 128), F32)
        wv = w_ref[...]
        bv = b_ref[...]

        def step1(c, carry):
            base = pl.multiple_of(c * T, T)
            xs = [xp_ref[pl.ds(base + 5 + i, T), :] for i in range(SSD_CONV)]
            acc = wv[0:1] * xs[0]
            for i in range(1, SSD_CONV):
                acc = acc + wv[i:i + 1] * xs[i]
            acc = bv + acc
            s = _sigmoid(acc)
            dc = da_ref[pl.ds(base, T), :] * _silu_grad(acc, s)
            dcp_ref[pl.ds(base, T), :] = dc
            new = tuple(carry[i] + jnp.sum(xs[i] * dc, axis=0, keepdims=True) for i in range(SSD_CONV))
            return new + (carry[SSD_CONV] + jnp.sum(dc, axis=0, keepdims=True),)

        z = jnp.zeros((1, 128), F32)
        res = lax.fori_loop(0, S // T, step1, (z,) * (SSD_CONV + 1))
        for i in range(SSD_CONV):
            dw_ref[pl.ds(i, 1), :] = res[i]
        db_ref[...] = res[SSD_CONV]

        def step2(c, carry):
            base = pl.multiple_of(c * T, T)
            acc = wv[0:1] * dcp_ref[pl.ds(base + 3, T), :]
            for i in range(1, SSD_CONV):
                acc = acc + wv[i:i + 1] * dcp_ref[pl.ds(base + 3 - i, T), :]
            dx_ref[pl.ds(base, T), :] = acc.astype(dx_ref.dtype)
            return carry

        lax.fori_loop(0, S // T, step2, 0)

    clamp = lambda j, lo, n: jnp.clip(j - lo, 0, n - 1)
    return pl.pallas_call(
        body, grid=(SSD_CONV_CH // 128,),
        in_specs=[pl.BlockSpec((S, 128), lambda j: (0, clamp(j, 0, NX))), pl.BlockSpec((S, 128), lambda j: (0, clamp(j, NX, NB))),
                  pl.BlockSpec((S, 128), lambda j: (0, clamp(j, NX + NB, NB))),
                  pl.BlockSpec((S, 128), lambda j: (0, XBC_BLK0 + j)), pl.BlockSpec((SSD_CONV, 128), lambda j: (0, j)),
                  pl.BlockSpec((1, 128), lambda j: (0, j)), pl.BlockSpec(memory_space=pl.ANY)],
        out_specs=[pl.BlockSpec((S, 128), lambda j: (0, XBC_BLK0 + j)), pl.BlockSpec((SSD_CONV, 128), lambda j: (0, j)),
                   pl.BlockSpec((1, 128), lambda j: (0, j))],
        out_shape=[jax.ShapeDtypeStruct(dproj.shape, dproj.dtype), jax.ShapeDtypeStruct((SSD_CONV, SSD_CONV_CH), F32),
                   jax.ShapeDtypeStruct((1, SSD_CONV_CH), F32)],
        input_output_aliases={6: 0},
        scratch_shapes=[pltpu.VMEM((S + 8, 128), F32), pltpu.VMEM((S + 8, 128), F32)], name=name,
        compiler_params=_params("arbitrary"),
    )(dxs, dbm, dcm, proj, w, b, dproj)


HPG = SSD_HEADS // SSD_GROUPS
GW = HPG * SSD_HEAD_DIM


def _ssd_chunk_terms(dtr, bias, alog, tril, triu):
    pre = dtr + bias
    dt = jnp.maximum(pre, 0.0) + jnp.log(1.0 + jnp.exp(-jnp.abs(pre)))
    a_neg = -jnp.exp(alog)
    a = dt * a_neg
    acum = _xdot(tril, a, NN, exact_first=True)
    acum_t = _xdot(a, triu, TN)
    return pre, dt, a_neg, acum, acum_t


def _head_expanders():
    h64 = lax.broadcasted_iota(jnp.int32, (HPG, GW), 0) == lax.broadcasted_iota(jnp.int32, (HPG, GW), 1) // SSD_HEAD_DIM
    h128 = lax.broadcasted_iota(jnp.int32, (HPG, HPG * CHUNK), 0) == lax.broadcasted_iota(jnp.int32, (HPG, HPG * CHUNK), 1) // CHUNK
    return h64.astype(F32), h128.astype(F32)


def _ssd_fwd(xbc, proj, dtr, dt_bias, a_log, d_rep, gain, name):
    S = xbc.shape[0]
    L = CHUNK
    T = min(512, S)
    CPS = T // L
    NC = S // L

    def body(x_ref, b_ref, c_ref, z_ref, dtr_ref, bias_ref, alog_ref, d_ref, gain_ref, y_ref, yraw_ref, st_ref, state):
        i = pl.program_id(1)

        @pl.when(i == 0)
        def _():
            state[...] = jnp.zeros_like(state)

        row = lax.broadcasted_iota(jnp.int32, (L, L), 0)
        col = lax.broadcasted_iota(jnp.int32, (L, L), 1)
        causal = row >= col
        tril = causal.astype(F32)
        triu = (row <= col).astype(F32)
        low = col < SSD_HEAD_DIM
        e64, e128 = _head_expanders()
        for c in range(CPS):
            rows = pl.ds(c * L, L)
            xv = x_ref[rows, :]
            bm = b_ref[rows, :]
            cm = c_ref[rows, :]
            _, dt, _, acum, acum_t = _ssd_chunk_terms(dtr_ref[rows, :], bias_ref[...], alog_ref[...], tril, triu)
            ac = _xdot(acum, e64, NN)
            ac_sq = _xdot(acum, e128, NN)
            xd = xv * _xdot(dt, e64, NN, pieces=2)
            ac_last = ac[L - 1:L, :]
            sp = state[...]
            st_ref[c] = sp
            yoff = _bdot(cm, sp, NN) * jnp.exp(ac)
            state[...] = sp * jnp.exp(ac_last) + _bdot(bm, xd * jnp.exp(ac_last - ac), TN)
            gmat = _bdot(cm, bm, NT)
            for q in range(HPG // 2):
                pair = slice(q * 128, (q + 1) * 128)
                tile = xd[:, pair]
                y = yoff[:, pair]
                for j, keep in ((2 * q, low), (2 * q + 1, ~low)):
                    lam = jnp.exp(jnp.where(causal, ac_sq[:, j * L:(j + 1) * L] - acum_t[j:j + 1, :], NEG))
                    y = y + _bdot(gmat * lam, jnp.where(keep, tile, 0.0), NN)
                yraw_ref[rows, pair] = y
            zz = z_ref[rows, :]
            u = (yraw_ref[rows, :] + xv * d_ref[...]) * (zz * _sigmoid(zz))
            r = lax.rsqrt(jnp.mean(u * u, axis=-1, keepdims=True) + EPS)
            y_ref[rows, :] = (u * r * gain_ref[...]).astype(y_ref.dtype)

    vec8 = pl.BlockSpec((None, 1, HPG), lambda g, i: (g, 0, 0))
    return pl.pallas_call(
        body, grid=(SSD_GROUPS, S // T),
        in_specs=[pl.BlockSpec((T, GW), lambda g, i: (i, g)),
                  pl.BlockSpec((T, SSD_STATE), lambda g, i: (i, SSD_WIDTH // SSD_STATE + g)),
                  pl.BlockSpec((T, SSD_STATE), lambda g, i: (i, SSD_WIDTH // SSD_STATE + SSD_GROUPS + g)),
                  pl.BlockSpec((T, GW), lambda g, i: (i, Z_COL // GW + g)),
                  pl.BlockSpec((None, T, HPG), lambda g, i: (g, i, 0)),
                  vec8, vec8,
                  pl.BlockSpec((1, GW), lambda g, i: (0, g)), pl.BlockSpec((1, GW), lambda g, i: (0, g))],
        out_specs=[pl.BlockSpec((T, GW), lambda g, i: (i, g)), pl.BlockSpec((T, GW), lambda g, i: (i, g)),
                   pl.BlockSpec((CPS, None, SSD_STATE, GW), lambda g, i: (i, g, 0, 0))],
        out_shape=[jax.ShapeDtypeStruct((S, MIX_WIDTH), BF16), jax.ShapeDtypeStruct((S, SSD_WIDTH), F32),
                   jax.ShapeDtypeStruct((NC, SSD_GROUPS, SSD_STATE, GW), F32)],
        scratch_shapes=[pltpu.VMEM((SSD_STATE, GW), F32)], name=name,
        compiler_params=_params("arbitrary", "arbitrary"),
    )(xbc, xbc, xbc, proj, dtr, dt_bias, a_log, d_rep, gain)


def _ssd_bwd(dy, yraw, xbc, proj, dtr, dt_bias, a_log, d_rep, gain, states, name):
    S = xbc.shape[0]
    L = CHUNK
    T = min(512, S)
    CPS = T // L
    NI = S // T

    def body(dy_ref, yraw_ref, x_ref, b_ref, c_ref, z_ref, dtr_ref, bias_ref, alog_ref, d_ref, gain_ref, st_ref,
             dz_ref, dx_ref, db_ref, dc_ref, ddtr_ref, dbias_ref, dalog_ref, dd_ref, dgain_ref, dstate, dxd_ref):
        i = pl.program_id(1)

        @pl.when(i == 0)
        def _():
            dstate[...] = jnp.zeros_like(dstate)
            dbias_ref[...] = jnp.zeros_like(dbias_ref)
            dalog_ref[...] = jnp.zeros_like(dalog_ref)
            dd_ref[...] = jnp.zeros_like(dd_ref)
            dgain_ref[...] = jnp.zeros_like(dgain_ref)

        row = lax.broadcasted_iota(jnp.int32, (L, L), 0)
        col = lax.broadcasted_iota(jnp.int32, (L, L), 1)
        causal = row >= col
        tril = causal.astype(F32)
        triu = (row <= col).astype(F32)
        low = col < SSD_HEAD_DIM
        e64, e128 = _head_expanders()
        lane8 = lax.broadcasted_iota(jnp.int32, (1, HPG), 1)
        sub8 = lax.broadcasted_iota(jnp.int32, (HPG, 1), 0)
        eye8 = (lax.broadcasted_iota(jnp.int32, (HPG, HPG), 0) == lax.broadcasted_iota(jnp.int32, (HPG, HPG), 1)).astype(F32)
        last_row = (lax.broadcasted_iota(jnp.int32, (L, 1), 0) == L - 1).astype(F32)
        for c in reversed(range(CPS)):
            rows = pl.ds(c * L, L)
            xv = x_ref[rows, :]
            bm = b_ref[rows, :]
            cm = c_ref[rows, :]
            zz = z_ref[rows, :]
            dvec = d_ref[...]
            sz = _sigmoid(zz)
            silu_z = zz * sz
            v = yraw_ref[rows, :] + xv * dvec
            u = v * silu_z
            r = lax.rsqrt(jnp.mean(u * u, axis=-1, keepdims=True) + EPS)
            n = u * r
            do = dy_ref[rows, :]
            dgain_ref[...] += jnp.sum(do * n, axis=0, keepdims=True)
            dn = do * gain_ref[...]
            du = r * (dn - n * jnp.mean(dn * n, axis=-1, keepdims=True))
            dz_ref[rows, :] = (du * v * _silu_grad(zz, sz)).astype(dz_ref.dtype)
            dyv = du * silu_z
            dd_ref[...] += _xdot(jnp.sum(dyv * xv, axis=0, keepdims=True), e64, NT, pieces=2)
            pre, dt, a_neg, acum, acum_t = _ssd_chunk_terms(dtr_ref[rows, :], bias_ref[...], alog_ref[...], tril, triu)
            ac = _xdot(acum, e64, NN)
            ac_sq = _xdot(acum, e128, NN)
            dt_w = _xdot(dt, e64, NN, pieces=2)
            xd = xv * dt_w
            ac_last = ac[L - 1:L, :]
            ea = jnp.exp(ac)
            w = jnp.exp(ac_last - ac)
            ea_last = jnp.exp(ac_last)
            sp = st_ref[c]
            ds = dstate[...]
            dye = dyv * ea
            yoff = _bdot(cm, sp, NN) * ea
            bds = _bdot(bm, ds, NN)
            dcm = _bdot(dye, sp, NT)
            dbm = _bdot(xd * w, ds, NT)
            dstate[...] = ds * ea_last + _bdot(cm, dye, TN)
            w8 = jnp.exp(acum[L - 1:L, :] - acum)
            dw8 = _xdot(xd * bds, e64, NT, pieces=2)
            dac8 = _xdot(dyv * yoff, e64, NT, pieces=2) - dw8 * w8
            tail8 = jnp.sum(dw8 * w8, axis=0, keepdims=True) + jnp.exp(acum[L - 1:L, :]) * _xdot(
                jnp.sum(ds * sp, axis=0, keepdims=True), e64, NT, pieces=2)
            dac8 = dac8 + last_row * tail8
            gmat = _bdot(cm, bm, NT)
            dgmat = jnp.zeros((L, L), F32)
            colsum_t = jnp.zeros((HPG, L), F32)
            for q in range(HPG // 2):
                pair = slice(q * 128, (q + 1) * 128)
                xd_tile = xd[:, pair]
                dy_tile = dyv[:, pair]
                dxd_tile = bds[:, pair] * w[:, pair]
                for j, keep in ((2 * q, low), (2 * q + 1, ~low)):
                    lam = jnp.exp(jnp.where(causal, ac_sq[:, j * L:(j + 1) * L] - acum_t[j:j + 1, :], NEG))
                    mh = gmat * lam
                    dyj = jnp.where(keep, dy_tile, 0.0)
                    dxd_tile = dxd_tile + _bdot(mh, dyj, TN)
                    dm = _bdot(dyj, xd_tile, NT)
                    dgmat = dgmat + dm * lam
                    qm = dm * mh
                    dac8 = dac8 + jnp.sum(qm, axis=1, keepdims=True) * (lane8 == j).astype(F32)
                    colsum_t = colsum_t + (sub8 == j).astype(F32) * jnp.sum(qm, axis=0, keepdims=True)
                dxd_ref[:, pair] = dxd_tile
            dac8 = dac8 - _xdot(colsum_t, eye8, TN)
            dxd = dxd_ref[...]
            dx_ref[rows, :] = dxd * dt_w + dyv * dvec
            dc_ref[rows, :] = dcm + _bdot(dgmat, bm, NN)
            db_ref[rows, :] = dbm + _bdot(dgmat, cm, TN)
            da8 = _xdot(triu, dac8, NN, exact_first=True)
            ddt8 = _xdot(dxd * xv, e64, NT, pieces=2) + da8 * a_neg
            dalog_ref[...] += jnp.sum(da8 * dt, axis=0, keepdims=True) * a_neg
            dpre = ddt8 * _sigmoid(pre)
            ddtr_ref[rows, :] = dpre
            dbias_ref[...] += jnp.sum(dpre, axis=0, keepdims=True)

    rev = lambda i: NI - 1 - i
    vec8 = pl.BlockSpec((None, 1, HPG), lambda g, i: (g, 0, 0))
    grp = pl.BlockSpec((T, GW), lambda g, i: (rev(i), g))
    bspec = pl.BlockSpec((T, SSD_STATE), lambda g, i: (rev(i), SSD_WIDTH // SSD_STATE + g))
    cspec = pl.BlockSpec((T, SSD_STATE), lambda g, i: (rev(i), SSD_WIDTH // SSD_STATE + SSD_GROUPS + g))
    gvec = pl.BlockSpec((1, GW), lambda g, i: (0, g))
    st_spec = pl.BlockSpec((CPS, None, SSD_STATE, GW), lambda g, i: (rev(i), g, 0, 0))
    small = jax.ShapeDtypeStruct((SSD_GROUPS, 1, HPG), F32)
    zspec = pl.BlockSpec((T, GW), lambda g, i: (rev(i), Z_COL // GW + g))
    return pl.pallas_call(
        body, grid=(SSD_GROUPS, NI),
        in_specs=[grp, grp, grp, bspec, cspec, zspec, pl.BlockSpec((None, T, HPG), lambda g, i: (g, rev(i), 0)),
                  vec8, vec8, gvec, gvec, st_spec],
        out_specs=[zspec, grp, pl.BlockSpec((T, SSD_STATE), lambda g, i: (rev(i), g)),
                   pl.BlockSpec((T, SSD_STATE), lambda g, i: (rev(i), g)),
                   pl.BlockSpec((None, T, HPG), lambda g, i: (g, rev(i), 0)), vec8, vec8, vec8, gvec],
        out_shape=[jax.ShapeDtypeStruct((S, IN_PAD), BF16), jax.ShapeDtypeStruct((S, SSD_WIDTH), F32),
                   jax.ShapeDtypeStruct((S, SSD_GROUPS * SSD_STATE), F32), jax.ShapeDtypeStruct((S, SSD_GROUPS * SSD_STATE), F32),
                   jax.ShapeDtypeStruct((SSD_GROUPS, S, HPG), F32), small, small, small,
                   jax.ShapeDtypeStruct((1, SSD_WIDTH), F32)],
        scratch_shapes=[pltpu.VMEM((SSD_STATE, GW), F32), pltpu.VMEM((L, GW), F32)],
        name=name, compiler_params=_params("arbitrary", "arbitrary"),
    )(dy, yraw, xbc, xbc, xbc, proj, dtr, dt_bias, a_log, d_rep, gain, states)


def _swap_halves(t):
    w = t.shape[1]
    lane = lax.broadcasted_iota(jnp.int32, t.shape, 1)
    return jnp.where((lane % 64) < 32, pltpu.roll(t, w - 32, axis=1), pltpu.roll(t, 32, axis=1))


def _widen(tab, w):
    return tab if w == 128 else jnp.concatenate([tab] * (w // 128), axis=1)


def _rope(t, cos, sin_signed):
    return t * cos + _swap_halves(t) * sin_signed


def _rope_t(d, cos, sin_signed):
    return d * cos - _swap_halves(d) * sin_signed


def _group_sum64(v, bd):
    hi = v.astype(BF16)
    lo = (v - hi.astype(F32)).astype(BF16)
    return (lax.dot_general(hi, bd, NN, preferred_element_type=F32)
            + lax.dot_general(lo, bd, NN, preferred_element_type=F32))


AQ_BLK = ATT_COL // ATT_WIDTH


def _att_prep_fwd(proj, qg, kg, cos, sin, bd, name):
    S = proj.shape[0]
    T = min(512, S)
    PB = ATT_SPAN // T
    src = lambda i: jnp.maximum(i - PB, 0)

    def body(q_ref, k_ref, v_ref, qg_ref, kg_ref, cos_ref, sin_ref, bd_ref, qo_ref, ko_ref, vo_ref):
        i = pl.program_id(0)

        @pl.when(i < PB)
        def _():
            ko_ref[...] = jnp.zeros_like(ko_ref)
            vo_ref[...] = jnp.zeros_like(vo_ref)

        @pl.when(i >= PB)
        def _():
            cw = _widen(cos_ref[...], ATT_WIDTH)
            sw = _widen(sin_ref[...], ATT_WIDTH)
            bdv = bd_ref[...]

            def norm_rope(t, gain):
                ss = _group_sum64(t * t, bdv)
                return _rope(t * lax.rsqrt(ss * (1.0 / ATT_HEAD_DIM) + EPS) * gain, cw, sw)

            qo_ref[...] = (norm_rope(q_ref[...], qg_ref[...]) * (ATT_HEAD_DIM ** -0.5)).astype(BF16)
            kt = norm_rope(k_ref[...], kg_ref[...]).astype(BF16)
            vt = v_ref[...].astype(BF16)
            for pr in range(ATT_HEADS // 2):
                ko_ref[pr] = kt[:, pr * 128:(pr + 1) * 128]
                vo_ref[pr] = vt[:, pr * 128:(pr + 1) * 128]

    vec = pl.BlockSpec((1, ATT_WIDTH), lambda i: (0, 0))
    tab = pl.BlockSpec((T, 128), lambda i: (src(i), 0))
    hm = pl.BlockSpec((ATT_HEADS // 2, T, 128), lambda i: (0, i, 0))
    hm_shape = jax.ShapeDtypeStruct((ATT_HEADS // 2, ATT_SPAN + S, 128), BF16)
    return pl.pallas_call(
        body, grid=(PB + S // T,),
        in_specs=[pl.BlockSpec((T, ATT_WIDTH), lambda i: (src(i), AQ_BLK)), pl.BlockSpec((T, ATT_WIDTH), lambda i: (src(i), AQ_BLK + 1)),
                  pl.BlockSpec((T, ATT_WIDTH), lambda i: (src(i), AQ_BLK + 2)), vec, vec, tab, tab,
                  pl.BlockSpec((ATT_WIDTH, ATT_WIDTH), lambda i: (0, 0))],
        out_specs=[pl.BlockSpec((T, ATT_WIDTH), lambda i: (src(i), 0)), hm, hm],
        out_shape=[jax.ShapeDtypeStruct((S, ATT_WIDTH), BF16), hm_shape, hm_shape],
        name=name, compiler_params=_params("arbitrary"),
    )(proj, proj, proj, qg, kg, cos, sin, bd)


def _att_prep_bwd(proj, dq, dk_p, dv_p, qg, kg, cos, sin, bd, dproj, name):
    S = proj.shape[0]
    T = min(512, S)
    NI = S // T
    PB = ATT_SPAN // T
    W = ATT_WIDTH

    def body(q_ref, k_ref, dq_ref, dkp_ref, dvp_ref, qg_ref, kg_ref, cos_ref, sin_ref, bd_ref, dproj_ref,
             do_ref, dqg_ref, dkg_ref, acc_ref):
        i = pl.program_id(0)

        @pl.when(i == 0)
        def _():
            acc_ref[...] = jnp.zeros_like(acc_ref)

        npair = ATT_HEADS // 2
        dk_all = jnp.concatenate([dkp_ref[pr].T for pr in range(npair)], axis=1)
        do_ref[:, 2 * W:3 * W] = jnp.concatenate([dvp_ref[pr].T for pr in range(npair)], axis=1).astype(BF16)
        cw = _widen(cos_ref[...], ATT_WIDTH)
        sw = _widen(sin_ref[...], ATT_WIDTH)
        bdv = bd_ref[...]

        def one(t, d_rot, gain, scale, slot):
            ss = _group_sum64(t * t, bdv)
            r = lax.rsqrt(ss * (1.0 / ATT_HEAD_DIM) + EPS)
            n = t * r
            d_ng = _rope_t(d_rot * scale, cw, sw)
            acc_ref[pl.ds(slot, 1), :] += jnp.sum(d_ng * n, axis=0, keepdims=True)
            dn = d_ng * gain
            return r * (dn - n * (_group_sum64(dn * n, bdv) * (1.0 / ATT_HEAD_DIM)))

        do_ref[:, 0:W] = one(q_ref[...], dq_ref[...], qg_ref[...], ATT_HEAD_DIM ** -0.5, 0).astype(BF16)
        do_ref[:, W:2 * W] = one(k_ref[...], dk_all, kg_ref[...], 1.0, 1).astype(BF16)

        @pl.when(i == NI - 1)
        def _():
            a = acc_ref[...]
            f = a[:, 0:64]
            for h in range(1, ATT_HEADS):
                f = f + a[:, h * 64:(h + 1) * 64]
            dqg_ref[...] = f[0:1]
            dkg_ref[...] = f[1:2]

    vec = pl.BlockSpec((1, ATT_WIDTH), lambda i: (0, 0))
    tab = pl.BlockSpec((T, 128), lambda i: (i, 0))
    row = pl.BlockSpec((T, ATT_WIDTH), lambda i: (i, 0))
    g64 = pl.BlockSpec((1, ATT_HEAD_DIM), lambda i: (0, 0))
    padded = pl.BlockSpec((ATT_HEADS // 2, 128, T), lambda i: (0, 0, i + PB))
    return pl.pallas_call(
        body, grid=(NI,),
        in_specs=[pl.BlockSpec((T, ATT_WIDTH), lambda i: (i, AQ_BLK)), pl.BlockSpec((T, ATT_WIDTH), lambda i: (i, AQ_BLK + 1)),
                  row, padded, padded, vec, vec, tab, tab, pl.BlockSpec((ATT_WIDTH, ATT_WIDTH), lambda i: (0, 0)),
                  pl.BlockSpec(memory_space=pl.ANY)],
        out_specs=[pl.BlockSpec((T, 3 * W), lambda i: (i, ATT_COL // (3 * W))), g64, g64],
        out_shape=[jax.ShapeDtypeStruct(dproj.shape, dproj.dtype), jax.ShapeDtypeStruct((1, ATT_HEAD_DIM), F32),
                   jax.ShapeDtypeStruct((1, ATT_HEAD_DIM), F32)],
        input_output_aliases={10: 0},
        scratch_shapes=[pltpu.VMEM((8, ATT_WIDTH), F32)], name=name, compiler_params=_params("arbitrary"),
    )(proj, proj, dq, dk_p, dv_p, qg, kg, cos, sin, bd, dproj)


def _att_bias():
    qpos = np.arange(CHUNK)[:, None] + ATT_SPAN
    kpos = np.arange(ATT_STRIP)[None, :]
    rel = qpos - kpos
    mult = np.zeros((CHUNK, ATT_STRIP), np.float64)
    for window, dil in DILATED_PAIRS:
        mult += (rel >= 0) & (rel % dil == 0) & (rel // dil <= window // dil)
    return np.where(mult > 0, np.log(np.maximum(mult, 1.0)), NEG).astype(np.float32)


def _att_scores(q, ks, bias, i):
    s = _bdot(q, ks, NT) + bias
    kcol = lax.broadcasted_iota(jnp.int32, (1, ATT_STRIP), 1) + i * CHUNK
    return jnp.where(kcol >= ATT_SPAN, s, NEG)


def _pair_masks():
    low = lax.broadcasted_iota(jnp.int32, (CHUNK, 128), 1) < ATT_HEAD_DIM
    return low, ~low


def _att_fwd(q, kp, vp, bias, y, name):
    S = q.shape[0]
    SP = kp.shape[1]

    QB = min(ATT_QB, S // CHUNK)
    TQ = QB * CHUNK

    def body(q_ref, k_ref, v_ref, bias_ref, y_ref, o_ref):
        i = pl.program_id(1)
        for b in range(QB):
            blk = i * QB + b
            strip = pl.ds(pl.multiple_of(blk * CHUNK, CHUNK), ATT_STRIP)
            rows = pl.ds(b * CHUNK, CHUNK)
            qv = q_ref[rows, :]
            ks = k_ref[strip, :]
            vs = v_ref[strip, :]
            outs = []
            for keep in _pair_masks():
                s = _att_scores(jnp.where(keep, qv, jnp.zeros_like(qv)), ks, bias_ref[...], blk)
                m = jnp.max(s, axis=-1, keepdims=True)
                p = jnp.exp(s - m)
                den = jnp.sum(p, axis=-1, keepdims=True)
                outs.append(_bdot(p, vs, NN) / den)
            o_ref[rows, :] = jnp.where(_pair_masks()[0], outs[0], outs[1]).astype(o_ref.dtype)

    kv = pl.BlockSpec((None, SP, 128), lambda hp, i: (hp, 0, 0))
    return pl.pallas_call(
        body, grid=(ATT_HEADS // 2, S // TQ),
        in_specs=[pl.BlockSpec((TQ, 128), lambda hp, i: (i, hp)), kv, kv,
                  pl.BlockSpec((CHUNK, ATT_STRIP), lambda hp, i: (0, 0)), pl.BlockSpec(memory_space=pl.ANY)],
        out_specs=pl.BlockSpec((TQ, 128), lambda hp, i: (i, SSD_WIDTH // 128 + hp)),
        out_shape=jax.ShapeDtypeStruct(y.shape, y.dtype), input_output_aliases={4: 0}, name=name,
        compiler_params=_params("parallel", "arbitrary"),
    )(q, kp, vp, bias, y)


def _att_bwd(q, kp, vp, bias, dy, name):
    S = q.shape[0]
    SP = kp.shape[1]
    QB = min(ATT_QB, S // CHUNK)

    def body(q_ref, k_ref, v_ref, bias_ref, do_ref, dq_ref, dk_ref, dv_ref):
        i = pl.program_id(1)

        @pl.when(i == 0)
        def _():
            dk_ref[...] = jnp.zeros_like(dk_ref)
            dv_ref[...] = jnp.zeros_like(dv_ref)

        for b in range(QB):
            blk = i * QB + b
            strip = pl.ds(pl.multiple_of(blk * CHUNK, CHUNK), ATT_STRIP)
            rows = pl.ds(b * CHUNK, CHUNK)
            qv = q_ref[rows, :]
            dov = do_ref[rows, :]
            ks = k_ref[strip, :]
            vs = v_ref[strip, :]
            dq = jnp.zeros((CHUNK, 128), F32)
            dk_t = jnp.zeros((128, ATT_STRIP), F32)
            dv_t = jnp.zeros((128, ATT_STRIP), F32)
            for keep in _pair_masks():
                qh = jnp.where(keep, qv, jnp.zeros_like(qv))
                doh = jnp.where(keep, dov, 0.0)
                s = _att_scores(qh, ks, bias_ref[...], blk)
                m = jnp.max(s, axis=-1, keepdims=True)
                p = jnp.exp(s - m)
                p = p / jnp.sum(p, axis=-1, keepdims=True)
                dp = _bdot(doh, vs, NT)
                dsc = p * (dp - jnp.sum(p * dp, axis=-1, keepdims=True))
                dq = dq + jnp.where(keep, _bdot(dsc, ks, NN), 0.0)
                dv_t = dv_t + _bdot(doh, p, TN)
                dk_t = dk_t + _bdot(qh, dsc, TN)
            dq_ref[rows, :] = dq
            dv_ref[:, strip] += dv_t
            dk_ref[:, strip] += dk_t

    TQ = QB * CHUNK
    kv = pl.BlockSpec((None, SP, 128), lambda hp, i: (hp, 0, 0))
    kv_t =pl.BlockSpec((None, 128, SP), lambda hp, i: (hp, 0, 0))
    pairs = jax.ShapeDtypeStruct((ATT_HEADS // 2, 128, SP), F32)
    return pl.pallas_call(
        body, grid=(ATT_HEADS // 2, S // TQ),
        in_specs=[pl.BlockSpec((TQ, 128), lambda hp, i: (i, hp)), kv, kv,
                  pl.BlockSpec((CHUNK, ATT_STRIP), lambda hp, i: (0, 0)),
                  pl.BlockSpec((TQ, 128), lambda hp, i: (i, SSD_WIDTH // 128 + hp))],
        out_specs=[pl.BlockSpec((TQ, 128), lambda hp, i: (i, hp)), kv_t, kv_t],
        out_shape=[jax.ShapeDtypeStruct((S, ATT_WIDTH), F32), pairs, pairs],
        name=name, compiler_params=_params("parallel", "arbitrary"),
    )(q, kp, vp, bias, dy)


RQ_BLK = RET_COL // RET_QK_WIDTH
RV_BLK = (RET_COL + 2 * RET_QK_WIDTH) // RET_V_WIDTH
RET_PAIR = 2 * RET_QK_DIM
RET_LOG_GAMMA = tuple(math.log1p(-2.0 ** (-5.0 - h)) for h in range(RET_HEADS))


def _ret_decays(h):
    L = CHUNK
    lg = RET_LOG_GAMMA[h]
    row = lax.broadcasted_iota(jnp.int32, (L, L), 0)
    col = lax.broadcasted_iota(jnp.int32, (L, L), 1)
    rel = (row - col).astype(F32)
    dm = jnp.where(rel >= 0, jnp.exp(jnp.maximum(rel, 0.0) * lg), 0.0)
    idx = lax.broadcasted_iota(jnp.int32, (L, 1), 0).astype(F32)
    kte = jnp.exp((L - 1 - idx) * lg)
    qfs = jnp.exp((idx + 1.0) * lg)
    return dm, kte, qfs, math.exp(L * lg)


def _ret_head(t, h):
    tile = t[:, (h // 2) * RET_PAIR:(h // 2 + 1) * RET_PAIR]
    low = lax.broadcasted_iota(jnp.int32, tile.shape, 1) < RET_QK_DIM
    return jnp.where(low if h % 2 == 0 else ~low, tile, 0.0)


def _ret_fwd(proj, cos, sin, gain, y, name):
    S = proj.shape[0]
    L = CHUNK
    T = min(512, S)
    CPS = T // L
    NC = S // L

    def body(q_ref, k_ref, v_ref, g_ref, cos_ref, sin_ref, gain_ref, yin_ref, y_ref, o_ref, st_ref, state):
        i = pl.program_id(0)

        @pl.when(i == 0)
        def _():
            state[...] = jnp.zeros_like(state)

        dec = [_ret_decays(h) for h in range(RET_HEADS)]
        for c in range(CPS):
            rows = pl.ds(c * L, L)
            cw = _widen(cos_ref[rows, :], RET_QK_WIDTH)
            sw = _widen(sin_ref[rows, :], RET_QK_WIDTH)
            qv = _rope(q_ref[rows, :], cw, sw)
            kv = _rope(k_ref[rows, :], cw, sw) * (RET_QK_DIM ** -0.5)
            for h in range(RET_HEADS):
                dm, kte, qfs, cd = dec[h]
                qh, kh = _ret_head(qv, h), _ret_head(kv, h)
                vs = slice(h * RET_V_DIM, (h + 1) * RET_V_DIM)
                vh = v_ref[rows, vs]
                sp = state[h]
                st_ref[c, h] = sp
                o = _bdot(_bdot(qh, kh, NT) * dm, vh, NN) + _bdot(qh * qfs, sp, NN)
                state[h] = cd * sp + _bdot(kh * kte, vh, TN)
                o_ref[rows, vs] = o
                gh = g_ref[rows, vs]
                r = lax.rsqrt(jnp.mean(o * o, axis=-1, keepdims=True) + EPS)
                y_ref[rows, vs] = (o * r * gain_ref[:, vs] * (gh * _sigmoid(gh))).astype(y_ref.dtype)

    tab = pl.BlockSpec((T, 128), lambda i: (i, 0))
    wide = pl.BlockSpec((T, RET_V_WIDTH), lambda i: (i, 0))
    return pl.pallas_call(
        body, grid=(S // T,),
        in_specs=[pl.BlockSpec((T, RET_QK_WIDTH), lambda i: (i, RQ_BLK)), pl.BlockSpec((T, RET_QK_WIDTH), lambda i: (i, RQ_BLK + 1)),
                  pl.BlockSpec((T, RET_V_WIDTH), lambda i: (i, RV_BLK)), pl.BlockSpec((T, RET_V_WIDTH), lambda i: (i, RV_BLK + 1)),
                  tab, tab, pl.BlockSpec((1, RET_V_WIDTH), lambda i: (0, 0)), pl.BlockSpec(memory_space=pl.ANY)],
        out_specs=[pl.BlockSpec((T, RET_V_WIDTH), lambda i: (i, (SSD_WIDTH + ATT_WIDTH) // RET_V_WIDTH)), wide,
                   pl.BlockSpec((CPS, RET_HEADS, RET_PAIR, RET_V_DIM), lambda i: (i, 0, 0, 0))],
        out_shape=[jax.ShapeDtypeStruct(y.shape, y.dtype), jax.ShapeDtypeStruct((S, RET_V_WIDTH), F32),
                   jax.ShapeDtypeStruct((NC, RET_HEADS, RET_PAIR, RET_V_DIM), F32)],
        input_output_aliases={7: 0},
        scratch_shapes=[pltpu.VMEM((RET_HEADS, RET_PAIR, RET_V_DIM), F32)], name=name,
        compiler_params=_params("arbitrary"),
    )(proj, proj, proj, proj, cos, sin, gain, y)


def _ret_bwd(dy, oraw, proj, cos, sin, gain, states, dproj, name):
    S = proj.shape[0]
    L = CHUNK
    T = min(512, S)
    CPS = T // L
    NI = S // T
    QW, VW = RET_QK_WIDTH, RET_V_WIDTH
    V0, G0 = 2 * QW, 2 * QW + VW

    def body(dy_ref, o_ref, q_ref, k_ref, v_ref, g_ref, cos_ref, sin_ref, gain_ref, st_ref, dproj_ref,
             out_ref, dgain_ref, dstate, dqs, dks):
        i = pl.program_id(0)

        @pl.when(i == 0)
        def _():
            dstate[...] = jnp.zeros_like(dstate)
            dgain_ref[...] = jnp.zeros_like(dgain_ref)

        dec = [_ret_decays(h) for h in range(RET_HEADS)]
        for c in reversed(range(CPS)):
            rows = pl.ds(c * L, L)
            cw = _widen(cos_ref[rows, :], RET_QK_WIDTH)
            sw = _widen(sin_ref[rows, :], RET_QK_WIDTH)
            qv = _rope(q_ref[rows, :], cw, sw)
            kv = _rope(k_ref[rows, :], cw, sw) * (RET_QK_DIM ** -0.5)
            for h in range(RET_HEADS):
                dm, kte, qfs, cd = dec[h]
                pair = slice((h // 2) * RET_PAIR, (h // 2 + 1) * RET_PAIR)
                vs = slice(h * RET_V_DIM, (h + 1) * RET_V_DIM)
                qh, kh = _ret_head(qv, h), _ret_head(kv, h)
                vh = v_ref[rows, vs]
                gh = g_ref[rows, vs]
                gn = gain_ref[:, vs]
                o = o_ref[rows, vs]
                dyh = dy_ref[rows, vs]
                sg = _sigmoid(gh)
                silu_g = gh * sg
                r = lax.rsqrt(jnp.mean(o * o, axis=-1, keepdims=True) + EPS)
                n = o * r
                dgain_ref[:, vs] += jnp.sum(dyh * n * silu_g, axis=0, keepdims=True)
                out_ref[rows, G0 + h * RET_V_DIM:G0 + (h + 1) * RET_V_DIM] = (dyh * n * gn * _silu_grad(gh, sg)).astype(out_ref.dtype)
                dn = dyh * gn * silu_g
                do = r * (dn - n * jnp.mean(dn * n, axis=-1, keepdims=True))
                sp = st_ref[c, h]
                ds = dstate[h]
                sc = _bdot(qh, kh, NT) * dm
                dsc = _bdot(do, vh, NT) * dm
                out_ref[rows, V0 + h * RET_V_DIM:V0 + (h + 1) * RET_V_DIM] = (_bdot(sc, do, TN) + _bdot(kh * kte, ds, NN)).astype(out_ref.dtype)
                dqh = _bdot(dsc, kh, NN) + _bdot(do, sp, NT) * qfs
                dkh = _bdot(dsc, qh, TN) + _bdot(vh, ds, NT) * kte
                if h % 2 == 0:
                    dqs[:, pair] = dqh
                    dks[:, pair] = dkh
                else:
                    dqs[:, pair] += dqh
                    dks[:, pair] += dkh
                dstate[h] = cd * ds + _bdot(qh * qfs, do, TN)
            out_ref[rows, 0:QW] = _rope_t(dqs[...], cw, sw).astype(out_ref.dtype)
            out_ref[rows, QW:2 * QW] = _rope_t(dks[...] * (RET_QK_DIM ** -0.5), cw, sw).astype(out_ref.dtype)

    rev = lambda i: NI - 1 - i
    tab = pl.BlockSpec((T, 128), lambda i: (rev(i), 0))
    wide = pl.BlockSpec((T, RET_V_WIDTH), lambda i: (rev(i), 0))
    group = pl.BlockSpec((T, G0 + VW), lambda i: (rev(i), RET_COL // (G0 + VW)))
    gvec = pl.BlockSpec((1, RET_V_WIDTH), lambda i: (0, 0))
    return pl.pallas_call(
        body, grid=(NI,),
        in_specs=[pl.BlockSpec((T, RET_V_WIDTH), lambda i: (rev(i), (SSD_WIDTH + ATT_WIDTH) // RET_V_WIDTH)), wide,
                  pl.BlockSpec((T, RET_QK_WIDTH), lambda i: (rev(i), RQ_BLK)), pl.BlockSpec((T, RET_QK_WIDTH), lambda i: (rev(i), RQ_BLK + 1)),
                  pl.BlockSpec((T, RET_V_WIDTH), lambda i: (rev(i), RV_BLK)), pl.BlockSpec((T, RET_V_WIDTH), lambda i: (rev(i), RV_BLK + 1)),
                  tab, tab, gvec,
                  pl.BlockSpec((CPS, RET_HEADS, RET_PAIR, RET_V_DIM), lambda i: (rev(i), 0, 0, 0)),
                  pl.BlockSpec(memory_space=pl.ANY)],
        out_specs=[group, gvec],
        out_shape=[jax.ShapeDtypeStruct(dproj.shape, dproj.dtype), jax.ShapeDtypeStruct((1, RET_V_WIDTH), F32)],
        input_output_aliases={10: 0},
        scratch_shapes=[pltpu.VMEM((RET_HEADS, RET_PAIR, RET_V_DIM), F32), pltpu.VMEM((L, RET_QK_WIDTH), F32),
                        pltpu.VMEM((L, RET_QK_WIDTH), F32)],
        name=name, compiler_params=_params("arbitrary"),
    )(dy, oraw, proj, proj, proj, proj, cos, sin, gain, states, dproj)


def _adamw_update(g_ref, nb, w_ref, m_ref, v_ref, go_ref, d_ref, mo_ref, vo_ref):
    g = g_ref[0].astype(F32)
    for k in range(1, nb):
        g = g + g_ref[k].astype(F32)
    mn = ADAM_B1 * m_ref[...] + (1.0 - ADAM_B1) * g
    vn = ADAM_B2 * v_ref[...] + (1.0 - ADAM_B2) * (g * g)
    go_ref[...] = g
    mo_ref[...] = mn
    vo_ref[...] = vn
    c1 = 1.0 - ADAM_B1 ** ADAM_STEP
    c2 = 1.0 - ADAM_B2 ** ADAM_STEP
    d_ref[...] = -ADAM_LR * ((mn / c1) / (jnp.sqrt(vn / c2) + ADAM_EPS) + ADAM_WD * w_ref[...])


def _adamw_rows(R, C):
    return _pick(R, tuple(t for t in (512, 256, 128, 64, 32, 16, 8) if t * C <= 256 * 1024))


def _adamw(gblocks, w, m, v, name):
    nb, R, C = gblocks.shape
    tr = _adamw_rows(R, C)

    def body(g_ref, *refs):
        _adamw_update(g_ref, nb, *refs)

    row = pl.BlockSpec((tr, C), lambda i: (i, 0))
    sh = jax.ShapeDtypeStruct((R, C), F32)
    return pl.pallas_call(
        body, grid=(R // tr,), in_specs=[pl.BlockSpec((nb, tr, C), lambda i: (0, i, 0)), row, row, row],
        out_specs=[row, row, row, row], out_shape=[sh, sh, sh, sh], name=name, compiler_params=_params("parallel"),
    )(gblocks, w, m, v)


def _adamw_layers(g0, g1, w, m, v, name):
    nb, R, C = g0.shape
    tr = _adamw_rows(R, C)

    def body(g0_ref, g1_ref, *refs):
        l = pl.program_id(0)

        @pl.when(l == 0)
        def _():
            _adamw_update(g0_ref, nb, *refs)

        @pl.when(l == 1)
        def _():
            _adamw_update(g1_ref, nb, *refs)

    row = pl.BlockSpec((tr, C), lambda l, i: (l * (R // tr) + i, 0))
    sh = jax.ShapeDtypeStruct((DEPTH * R, C), F32)
    flat = lambda t: t.reshape(DEPTH * R, C)
    outs = pl.pallas_call(
        body, grid=(DEPTH, R // tr),
        in_specs=[pl.BlockSpec((nb, tr, C), lambda l, i: (0, i * (1 - l), 0)), pl.BlockSpec((nb, tr, C), lambda l, i: (0, i * l, 0)),
                  row, row, row],
        out_specs=[row, row, row, row], out_shape=[sh, sh, sh, sh], name=name, compiler_params=_params("arbitrary", "arbitrary"),
    )(g0, g1, flat(w), flat(m), flat(v))
    return [t.reshape(DEPTH, R, C) for t in outs]


def _peers():
    x, y, c = lax.axis_index("x"), lax.axis_index("y"), lax.axis_index("c")
    flips = ((0, 0, 1), (1, 0, 0), (0, 1, 0), (1, 1, 0), (1, 0, 1), (0, 1, 1), (1, 1, 1))
    me = 4 * x + 2 * y + c
    peers = [(x ^ fx, y ^ fy, c ^ fc) for fx, fy, fc in flips]
    return me, peers


def _exchange(arrs, scatter, name):
    n = len(arrs)
    npeer = N_DEV - 1

    def body(*refs):
        ins, outs = refs[:n], refs[n:2 * n]
        send_sems, recv_sems, local_sems = refs[2 * n:]
        me, peers = _peers()
        copies = []
        for a in range(n):
            src_own = ins[a].at[me] if scatter else ins[a]
            own = pltpu.make_async_copy(src_own, outs[a].at[me], local_sems.at[a])
            own.start()
            copies.append(own)
            for k, peer in enumerate(peers):
                src = ins[a].at[4 * peer[0] + 2 * peer[1] + peer[2]] if scatter else ins[a]
                cp = pltpu.make_async_remote_copy(
                    src_ref=src, dst_ref=outs[a].at[me], send_sem=send_sems.at[a * npeer + k],
                    recv_sem=recv_sems.at[a * npeer + k], device_id=peer, device_id_type=pl.DeviceIdType.MESH)
                cp.start()
                copies.append(cp)
        for cp in copies:
            cp.wait()

    out_shape = [jax.ShapeDtypeStruct(((N_DEV,) + a.shape[1:]) if scatter else ((N_DEV,) + a.shape), a.dtype) for a in arrs]
    anyspec = pl.BlockSpec(memory_space=pl.ANY)
    return pl.pallas_call(
        body, in_specs=[anyspec] * n, out_specs=[anyspec] * n, out_shape=out_shape,
        scratch_shapes=[pltpu.SemaphoreType.DMA((n * npeer,)), pltpu.SemaphoreType.DMA((n * npeer,)),
                        pltpu.SemaphoreType.DMA((n,))],
        name=name,
    )(*arrs)


def _dev_index(peer):
    return 4 * peer[0] + 2 * peer[1] + peer[2]


def _push_copies(src_refs, land_refs, send_sems, recv_sems, scatter, as_receiver):
    me, peers = _peers()
    npeer = N_DEV - 1
    copies = []
    for a in range(len(src_refs)):
        for k, peer in enumerate(peers):
            src = src_refs[a].at[_dev_index(peer)] if scatter else src_refs[a]
            slot = _dev_index(peer) if as_receiver else me
            copies.append(pltpu.make_async_remote_copy(
                src_ref=src, dst_ref=land_refs[a].at[slot], send_sem=send_sems.at[a * npeer + k],
                recv_sem=recv_sems.at[a * npeer + k], device_id=peer, device_id_type=pl.DeviceIdType.MESH))
    return copies


def _own_copies(src_refs, land_refs, own_sems, scatter):
    me, _ = _peers()
    return [pltpu.make_async_copy(src_refs[a].at[me] if scatter else src_refs[a], land_refs[a].at[me], own_sems.at[a])
            for a in range(len(src_refs))]


def _push_start(srcs, scatter, name):
    n = len(srcs)
    nsem = n * (N_DEV - 1)

    def body(*refs):
        srcs_r, lands_r = refs[:n], refs[n:2 * n]
        for cp in _push_copies(srcs_r, lands_r, refs[2 * n], refs[2 * n + 1], scatter, False):
            cp.start()
        for cp in _own_copies(srcs_r, lands_r, refs[2 * n + 2], scatter):
            cp.start()
        token = refs[-1]
        token[...] = jnp.zeros_like(token)

    hbm = pl.BlockSpec(memory_space=pltpu.HBM)
    sem = pl.BlockSpec(memory_space=pltpu.SEMAPHORE)
    lands = [lax.empty((N_DEV,) + (s.shape[1:] if scatter else s.shape), s.dtype) for s in srcs]
    arrs = list(srcs) + lands
    return pl.pallas_call(
        body, name=name,
        out_shape=(pltpu.SemaphoreType.DMA((nsem,)), pltpu.SemaphoreType.DMA((nsem,)), pltpu.SemaphoreType.DMA((n,)),
                   *[pltpu.HBM(a.shape, a.dtype) for a in arrs], jax.ShapeDtypeStruct((8, 128), F32)),
        in_specs=[hbm] * (2 * n), out_specs=(sem, sem, sem, *([hbm] * (2 * n)), pl.BlockSpec(memory_space=pltpu.VMEM)),
        input_output_aliases={i: 3 + i for i in range(2 * n)},
        compiler_params=pltpu.CompilerParams(has_side_effects=pltpu.SideEffectType.DATAFLOW_SIDE_EFFECTING),
    )(*[pltpu.with_memory_space_constraint(a, pltpu.HBM) for a in arrs])


def _push_wait(handle, after, scatter, name):
    send_sems, recv_sems, own_sems, *thru, _ = handle
    n = len(thru) // 2

    def body(*refs):
        srcs_r, lands_r = refs[:n], refs[n:2 * n]
        for cp in _push_copies(srcs_r, lands_r, refs[2 * n], refs[2 * n + 1], scatter, True):
            cp.wait_send()
            cp.wait_recv()
        for cp in _own_copies(srcs_r, lands_r, refs[2 * n + 2], scatter):
            cp.wait()

    hbm = pl.BlockSpec(memory_space=pltpu.HBM)
    sem = pl.BlockSpec(memory_space=pltpu.SEMAPHORE)
    outs = pl.pallas_call(
        body, name=name, out_shape=tuple(pltpu.HBM(a.shape, a.dtype) for a in thru),
        in_specs=[hbm] * (2 * n) + [sem, sem, sem, pl.BlockSpec(memory_space=pl.ANY)], out_specs=tuple([hbm] * (2 * n)),
        input_output_aliases={i: i for i in range(2 * n)},
        compiler_params=pltpu.CompilerParams(has_side_effects=pltpu.SideEffectType.DATAFLOW_SIDE_EFFECTING),
    )(*thru, send_sems, recv_sems, own_sems, after)
    return list(outs[n:])


def _relay_copies(src_refs, land_refs, send1, recv1, send2, recv2, as_receiver):
    me, peers = _peers()
    sibling, chips = peers[0], peers[1:4]
    first, second = [], []
    for a in range(len(src_refs)):
        for k, peer in enumerate([sibling] + chips):
            slot = _dev_index(peer) if as_receiver else me
            first.append(pltpu.make_async_remote_copy(
                src_ref=src_refs[a], dst_ref=land_refs[a].at[slot], send_sem=send1.at[4 * a + k], recv_sem=recv1.at[4 * a + k],
                device_id=peer, device_id_type=pl.DeviceIdType.MESH))
        for k, chip in enumerate(chips):
            origin = _dev_index(chip)
            slot = origin ^ 1 if as_receiver else origin
            second.append(pltpu.make_async_remote_copy(
                src_ref=land_refs[a].at[origin], dst_ref=land_refs[a].at[slot], send_sem=send2.at[3 * a + k],
                recv_sem=recv2.at[3 * a + k], device_id=sibling, device_id_type=pl.DeviceIdType.MESH))
    return first, second


def _relay_start(srcs, name):
    n = len(srcs)

    def body(*refs):
        srcs_r, lands_r = refs[:n], refs[n:2 * n]
        for cp in _relay_copies(srcs_r, lands_r, refs[2 * n], refs[2 * n + 1], refs[2 * n], refs[2 * n + 1], False)[0]:
            cp.start()
        for cp in _own_copies(srcs_r, lands_r, refs[2 * n + 2], False):
            cp.start()
        token = refs[-1]
        token[...] = jnp.zeros_like(token)

    hbm = pl.BlockSpec(memory_space=pltpu.HBM)
    sem = pl.BlockSpec(memory_space=pltpu.SEMAPHORE)
    arrs = list(srcs) + [lax.empty((N_DEV,) + s.shape, s.dtype) for s in srcs]
    return pl.pallas_call(
        body, name=name,
        out_shape=(pltpu.SemaphoreType.DMA((4 * n,)), pltpu.SemaphoreType.DMA((4 * n,)), pltpu.SemaphoreType.DMA((n,)),
                   *[pltpu.HBM(a.shape, a.dtype) for a in arrs], jax.ShapeDtypeStruct((8, 128), F32)),
        in_specs=[hbm] * (2 * n), out_specs=(sem, sem, sem, *([hbm] * (2 * n)), pl.BlockSpec(memory_space=pltpu.VMEM)),
        input_output_aliases={i: 3 + i for i in range(2 * n)},
        compiler_params=pltpu.CompilerParams(has_side_effects=pltpu.SideEffectType.DATAFLOW_SIDE_EFFECTING),
    )(*[pltpu.with_memory_space_constraint(a, pltpu.HBM) for a in arrs])


def _relay_forward(handle, after, name):
    _, recv1, _, *thru, _ = handle
    n = len(thru) // 2

    def body(*refs):
        srcs_r, lands_r, recv1_r = refs[:n], refs[n:2 * n], refs[2 * n]
        send2_r, recv2_r = refs[2 * n + 2], refs[2 * n + 3]
        first, second = _relay_copies(srcs_r, lands_r, recv1_r, recv1_r, send2_r, recv2_r, True)
        for a in range(n):
            for k in range(1, 4):
                first[4 * a + k].wait_recv()
        for cp in _relay_copies(srcs_r, lands_r, recv1_r, recv1_r, send2_r, recv2_r, False)[1]:
            cp.start()
        token = refs[-1]
        token[...] = jnp.zeros_like(token)

    hbm = pl.BlockSpec(memory_space=pltpu.HBM)
    sem = pl.BlockSpec(memory_space=pltpu.SEMAPHORE)
    return pl.pallas_call(
        body, name=name,
        out_shape=(pltpu.SemaphoreType.DMA((3 * n,)), pltpu.SemaphoreType.DMA((3 * n,)),
                   *[pltpu.HBM(a.shape, a.dtype) for a in thru], jax.ShapeDtypeStruct((8, 128), F32)),
        in_specs=[hbm] * (2 * n) + [sem, pl.BlockSpec(memory_space=pl.ANY)],
        out_specs=(sem, sem, *([hbm] * (2 * n)), pl.BlockSpec(memory_space=pltpu.VMEM)),
        input_output_aliases={i: 2 + i for i in range(2 * n)},
        compiler_params=pltpu.CompilerParams(has_side_effects=pltpu.SideEffectType.DATAFLOW_SIDE_EFFECTING),
    )(*thru, recv1, after)


def _relay_wait(handle, forwarded, after, name):
    send1, recv1, own_sems, *_ = handle
    send2, recv2, *thru, _ = forwarded
    n = len(thru) // 2

    def body(*refs):
        srcs_r, lands_r = refs[:n], refs[n:2 * n]
        send1_r, recv1_r, own_r, send2_r, recv2_r = refs[2 * n:2 * n + 5]
        first, second = _relay_copies(srcs_r, lands_r, send1_r, recv1_r, send2_r, recv2_r, True)
        for i, cp in enumerate(first):
            cp.wait_send()
            if i % 4 == 0:
                cp.wait_recv()
        for cp in second:
            cp.wait_send()
            cp.wait_recv()
        for cp in _own_copies(srcs_r, lands_r, own_r, False):
            cp.wait()

    hbm = pl.BlockSpec(memory_space=pltpu.HBM)
    sem = pl.BlockSpec(memory_space=pltpu.SEMAPHORE)
    outs = pl.pallas_call(
        body, name=name, out_shape=tuple(pltpu.HBM(a.shape, a.dtype) for a in thru),
        in_specs=[hbm] * (2 * n) + [sem] * 5 + [pl.BlockSpec(memory_space=pl.ANY)], out_specs=tuple([hbm] * (2 * n)),
        input_output_aliases={i: i for i in range(2 * n)},
        compiler_params=pltpu.CompilerParams(has_side_effects=pltpu.SideEffectType.DATAFLOW_SIDE_EFFECTING),
    )(*thru, send1, recv1, own_sems, send2, recv2, after)
    return list(outs[n:])


def _tables(S):
    pos = jnp.arange(S, dtype=F32)
    inv = ROPE_THETA ** (-jnp.arange(0, ATT_HEAD_DIM, 2, dtype=F32) / ATT_HEAD_DIM)
    ang = pos[:, None] * inv[None, :]
    cos, sin = jnp.cos(ang), jnp.sin(ang)
    cos128 = jnp.tile(cos, (1, 4))
    sin128 = jnp.tile(jnp.concatenate([-sin, sin], axis=1), (1, 2))
    lane = np.arange(ATT_WIDTH)
    bd = jnp.asarray((lane[:, None] // 64 == lane[None, :] // 64).astype(np.float32), dtype=BF16)
    return cos128, sin128, bd, jnp.asarray(_att_bias())


def _layer_fwd(l, x, p, tabs, early=None, late=None):
    cos, sin, bd, bias = tabs
    S = x.shape[0]
    row = lambda v: v.reshape(1, -1)
    hn = _rmsnorm_fwd(x, row(p["ln_mix"]), f"norm_mix_fwd{l}")
    if early is not None:
        p.update(early(hn))
    proj = _mm(hn, p["w_in"], "nn", f"in_proj{l}", tn=1920)
    xbc = _conv_fwd(proj, p["conv_w"], row(p["conv_b"]), f"conv_fwd{l}")
    dtr = proj[:, DT_COL:DT_COL + SSD_HEADS].reshape(S, SSD_GROUPS, HPG).transpose(1, 0, 2)
    grp = lambda v: v.reshape(SSD_GROUPS, 1, HPG)
    d_rep = row(jnp.repeat(p["d_skip"], SSD_HEAD_DIM))
    y, yraw, ssd_st = _ssd_fwd(xbc, proj, dtr, grp(p["dt_bias"]), grp(p["a_log"]), d_rep, row(p["ssd_norm"]), f"ssd_fwd{l}")
    qg = row(jnp.tile(p["q_norm"], ATT_HEADS))
    kg = row(jnp.tile(p["k_norm"], ATT_HEADS))
    aq, akp, avp = _att_prep_fwd(proj, qg, kg, cos, sin, bd, f"att_prep_fwd{l}")
    y = _att_fwd(aq, akp, avp, bias, y, f"att_fwd{l}")
    y, oraw, ret_st = _ret_fwd(proj, cos, sin, row(p["ret_norm"]), y, f"ret_fwd{l}")
    if late is not None:
        p.update(late(y))
    x1 = _mm(y, p["w_out"], "nn", f"out_proj{l}", residual=x)
    hn2 = _rmsnorm_fwd(x1, row(p["ln_ffn"]), f"norm_ffn_fwd{l}")
    g, u, act = _swiglu_fwd(hn2, p["w_gate"], p["w_up"], f"swiglu_fwd{l}")
    x2 = _mm(act, p["w_down"], "nn", f"down_proj{l}", residual=x1, tk=2816)
    saved = dict(x=x, hn=hn, proj=proj, xbc=xbc, dtr=dtr, yraw=yraw, ssd_st=ssd_st, aq=aq, akp=akp, avp=avp,
                 oraw=oraw, ret_st=ret_st, y=y, x1=x1, hn2=hn2, g=g, u=u, act=act, d_rep=d_rep, qg=qg, kg=kg)
    return x2, saved


def _layer_bwd(l, dx2, dx2_bf, p, sv, tabs, on_ffn=None, on_all=None):
    cos, sin, bd, bias = tabs
    S = dx2.shape[0]
    row = lambda v: v.reshape(1, -1)
    grp = lambda v: v.reshape(SSD_GROUPS, 1, HPG)
    gr = {}
    dg, du = _swiglu_bwd(dx2_bf, p["w_down"], sv["g"], sv["u"], f"swiglu_bwd{l}")
    gr["w_down"] = _mm(sv["act"], dx2_bf, "tn", f"down_wgrad{l}", out_dtype=BF16, tm=1408, tn=1024, tk=2048)
    dhn2 = _mm_nt2(dg, p["w_gate"], du, p["w_up"], f"ffn_dgrad{l}")
    gr["w_gate"] = _mm(sv["hn2"], dg, "tn", f"gate_wgrad{l}", out_dtype=BF16, tm=512, tn=2816, tk=2048)
    gr["w_up"] = _mm(sv["hn2"], du, "tn", f"up_wgrad{l}", out_dtype=BF16, tm=512, tn=2816, tk=2048)
    ffn_gain = row(p["ln_ffn"]) + (on_ffn(gr)[0, 0] if on_ffn is not None else 0.0)
    dx1, dx1_bf, dln_ffn = _rmsnorm_bwd(sv["x1"], dhn2, ffn_gain, dx2, f"norm_ffn_bwd{l}")
    gr["ln_ffn"] = dln_ffn[0]
    dy = _mm(dx1_bf, p["w_out"], "nt", f"out_dgrad{l}")
    gr["w_out"] = _mm(sv["y"], dx1_bf, "tn", f"out_wgrad{l}", out_dtype=BF16, tm=1024, tn=1024, tk=2048)
    dproj, dxs, dbm, dcm, ddtr, dbias, dalog, dd, dssd_gain = _ssd_bwd(
        dy, sv["yraw"], sv["xbc"], sv["proj"], sv["dtr"], grp(p["dt_bias"]), grp(p["a_log"]), sv["d_rep"],
        row(p["ssd_norm"]), sv["ssd_st"], f"ssd_bwd{l}")
    gr["dt_bias"], gr["a_log"], gr["d_skip"] = dbias.reshape(-1), dalog.reshape(-1), dd.reshape(-1)
    gr["ssd_norm"] = dssd_gain[0]
    dproj, dconv_w, dconv_b = _conv_bwd(dxs, dbm, dcm, sv["proj"], p["conv_w"], row(p["conv_b"]), dproj, f"conv_bwd{l}")
    gr["conv_w"], gr["conv_b"] = dconv_w, dconv_b[0]
    dq, dk_p, dv_p = _att_bwd(sv["aq"], sv["akp"], sv["avp"], bias, dy, f"att_bwd{l}")
    dproj, dqg, dkg = _att_prep_bwd(sv["proj"], dq, dk_p, dv_p, sv["qg"], sv["kg"], cos, sin, bd, dproj, f"att_prep_bwd{l}")
    gr["q_norm"], gr["k_norm"] = dqg[0], dkg[0]
    dproj, dret_gain = _ret_bwd(dy, sv["oraw"], sv["proj"], cos, sin, row(p["ret_norm"]), sv["ret_st"], dproj, f"ret_bwd{l}")
    gr["ret_norm"] = dret_gain[0]
    ddt_cols = ddtr.transpose(1, 0, 2).reshape(S, SSD_HEADS).astype(BF16)
    dproj = lax.dynamic_update_slice(dproj, jnp.pad(ddt_cols, ((0, 0), (0, IN_PAD - DT_COL - SSD_HEADS))), (0, DT_COL))
    gr["w_in"] = _mm(sv["hn"], dproj, "tn", f"in_wgrad{l}", out_dtype=BF16, tm=1024, tn=1920, tk=2048)
    launched = on_all(gr) if on_all is not None else None
    dhn = _mm(dproj, p["w_in"], "nt", f"in_dgrad{l}", tm=512, tk=IN_PAD, after=launched)
    dx0, dx0_bf, dln_mix = _rmsnorm_bwd(sv["x"], dhn, row(p["ln_mix"]), dx1, f"norm_mix_bwd{l}")
    gr["ln_mix"] = dln_mix[0]
    return dx0, dx0_bf, gr


def _local_step(x, tgt, layers, early=None, late=None, on_ffn=None, on_all=None):
    n = len(layers)
    none = [None] * n
    early, late, on_ffn, on_all = early or none, late or none, on_ffn or none, on_all or none
    tabs = _tables(x.shape[0])
    saved, params = [], []
    h = x
    for l in range(n):
        p = dict(layers[l](h) if callable(layers[l]) else layers[l])
        h, sv = _layer_fwd(l, h, p, tabs, early[l], late[l])
        saved.append(sv)
        params.append(p)
    dh, dh_bf, lacc = _loss_grad(h, tgt, "loss_grad")
    grads = [None] * n
    for l in reversed(range(n)):
        dh, dh_bf, grads[l] = _layer_bwd(l, dh, dh_bf, params[l], saved[l], tabs, on_ffn[l], on_all[l])
    return lacc[0, 0], dh, grads


BIG = ("w_in", "w_out", "w_gate", "w_up", "w_down")
SMALL = ("ln_mix", "conv_b", "dt_bias", "a_log", "d_skip", "ssd_norm", "q_norm", "k_norm", "ret_norm", "ln_ffn")
ORDER = ("ln_mix", "w_in", "conv_w", "conv_b", "dt_bias", "a_log", "d_skip", "ssd_norm", "q_norm", "k_norm", "ret_norm",
         "w_out", "ln_ffn", "w_gate", "w_up", "w_down")


COL_SHARDED = ("w_in", "w_gate", "w_up", "conv_w")


IN_GROUPS = ((ORIG_Z_XBC, Z_COL), (ORIG_DT, DT_COL), (ORIG_ATT, ATT_COL), (ORIG_RET, RET_COL))


def _full_weight(k, gathered):
    if k == "w_in":
        cs = gathered.shape[2]
        pieces = []
        for (lo, hi), _ in sorted(IN_GROUPS, key=lambda grp: grp[1]):
            for j in range(N_DEV):
                a, b = max(lo, j * cs), min(hi, (j + 1) * cs)
                if a < b:
                    pieces.append(gathered[j][:, a - j * cs:b - j * cs])
        pieces.append(jnp.zeros((gathered.shape[1], IN_PAD - IN_WIDTH), gathered.dtype))
        return jnp.concatenate(pieces, axis=1)
    if k in COL_SHARDED:
        return gathered.transpose(1, 0, 2).reshape(gathered.shape[1], -1)
    return gathered.reshape(-1, gathered.shape[2])


def _shard_block(k, g):
    if k == "w_in":
        cs = IN_WIDTH // N_DEV
        blocks = []
        for j in range(N_DEV):
            pieces = []
            for (lo, hi), col in IN_GROUPS:
                a, b = max(lo, j * cs), min(hi, (j + 1) * cs)
                if a < b:
                    pieces.append(g[:, col + a - lo:col + b - lo])
            blocks.append(jnp.concatenate(pieces, axis=1))
        return jnp.stack(blocks)
    if k in COL_SHARDED:
        return g.reshape(g.shape[0], N_DEV, -1).transpose(1, 0, 2)
    return g.reshape(N_DEV, -1, g.shape[1])


def kernel(x, ln_mix, w_in, conv_w, conv_b, dt_bias, a_log, d_skip, ssd_norm, q_norm, k_norm, ret_norm, w_out, ln_ffn, w_gate, w_up, w_down, loss_target, m_ln_mix, m_w_in, m_conv_w, m_conv_b, m_dt_bias, m_a_log, m_d_skip, m_ssd_norm, m_q_norm, m_k_norm, m_ret_norm, m_w_out, m_ln_ffn, m_w_gate, m_w_up, m_w_down, v_ln_mix, v_w_in, v_conv_w, v_conv_b, v_dt_bias, v_a_log, v_d_skip, v_ssd_norm, v_q_norm, v_k_norm, v_ret_norm, v_w_out, v_ln_ffn, v_w_gate, v_w_up, v_w_down):
    w = dict(ln_mix=ln_mix, w_in=w_in, conv_w=conv_w, conv_b=conv_b, dt_bias=dt_bias, a_log=a_log, d_skip=d_skip,
             ssd_norm=ssd_norm, q_norm=q_norm, k_norm=k_norm, ret_norm=ret_norm, w_out=w_out, ln_ffn=ln_ffn,
             w_gate=w_gate, w_up=w_up, w_down=w_down)
    m = dict(ln_mix=m_ln_mix, w_in=m_w_in, conv_w=m_conv_w, conv_b=m_conv_b, dt_bias=m_dt_bias, a_log=m_a_log,
             d_skip=m_d_skip, ssd_norm=m_ssd_norm, q_norm=m_q_norm, k_norm=m_k_norm, ret_norm=m_ret_norm, w_out=m_w_out,
             ln_ffn=m_ln_ffn, w_gate=m_w_gate, w_up=m_w_up, w_down=m_w_down)
    v = dict(ln_mix=v_ln_mix, w_in=v_w_in, conv_w=v_conv_w, conv_b=v_conv_b, dt_bias=v_dt_bias, a_log=v_a_log,
             d_skip=v_d_skip, ssd_norm=v_ssd_norm, q_norm=v_q_norm, k_norm=v_k_norm, ret_norm=v_ret_norm, w_out=v_w_out,
             ln_ffn=v_ln_ffn, w_gate=v_w_gate, w_up=v_w_up, w_down=v_w_down)
    me = 4 * lax.axis_index("x") + 2 * lax.axis_index("y") + lax.axis_index("c")

    late_names = ("w_out", "w_gate", "w_up", "w_down")
    waves = {"a": [("w_in", 0), ("conv_w", 0), ("conv_w", 1)], "b": [(k, 0) for k in late_names],
             "c": [("w_in", 1)], "d": [(k, 1) for k in late_names]}
    gather = {}
    behind = 0.0
    for tag, items in waves.items():
        srcs = [w[k][l] if k == "conv_w" else (w[k][l] + behind).astype(BF16) for k, l in items]
        start = _relay_start(srcs, f"gather_{tag}_start") if tag == "a" else _push_start(srcs, False, f"gather_{tag}_start")
        gather[tag] = start
        behind = start[-1][0, 0]
    started = behind
    full = {}

    def arrive(tag, after):
        if tag == "a":
            forwarded = _relay_forward(gather["a"], after, "gather_a_forward")
            lands = _relay_wait(gather["a"], forwarded, forwarded[-1], "gather_a_wait")
        else:
            lands = _push_wait(gather[tag], after, False, f"gather_{tag}_wait")
        for (k, l), g in zip(waves[tag], lands):
            full[k, l] = _full_weight(k, g)

    def layer_weights(l, names):
        return {k: full[k, l] for k in names}

    def small_weights(l):
        return {k: w[k][l] for k in SMALL}

    def layer0(h):
        p = small_weights(0)
        p["ln_mix"] = p["ln_mix"] + started
        return p

    def early0(hn):
        arrive("a", hn)
        return layer_weights(0, ("w_in", "conv_w"))

    def late0(y):
        arrive("b", y)
        return layer_weights(0, late_names)

    def layer1(h):
        arrive("c", h)
        return {**small_weights(1), **layer_weights(1, ("w_in", "conv_w"))}

    def late1(y):
        arrive("d", y)
        return layer_weights(1, late_names)

    groups = {"1": [(k, 1) for k in BIG], "0a": [(k, 0) for k in ("w_down", "w_gate", "w_up")],
              "0b": [(k, 0) for k in ("w_out", "w_in")]}
    scatter = {}

    def push_grads(tag, gr):
        blocks = [_shard_block(k, gr[k]) for k, _ in groups[tag]]
        scatter[tag] = _push_start(blocks, True, f"scatter_{tag}_start")
        return scatter[tag][-1]

    loss_part, gx, grads = _local_step(
        x[0], loss_target[0], [layer0, layer1], early=[early0, None], late=[late0, late1],
        on_ffn=[functools.partial(push_grads, "0a"), None],
        on_all=[functools.partial(push_grads, "0b"), functools.partial(push_grads, "1")])
    loss = lax.psum(loss_part, MESH_AXES)

    out = {}
    recv = {}
    for tag, items in groups.items():
        for item, r in zip(items, _push_wait(scatter[tag], gx, True, f"scatter_{tag}_wait")):
            recv[item] = r
    for k in BIG:
        out[k] = _adamw_layers(recv[k, 0], recv[k, 1], w[k], m[k], v[k], f"adamw_{k}")
    names = SMALL + ("conv_w",)
    sizes = [int(np.prod(grads[0][k].shape)) for k in names]
    packed = jnp.concatenate([jnp.stack([grads[l][k] for l in range(DEPTH)]).reshape(-1) for k in names])
    n_small = packed.shape[0]
    rows_small = -(-n_small // 1024) * 8
    pad = lambda t, fill: jnp.concatenate([t, jnp.full((rows_small * 128 - n_small,), fill, F32)]).reshape(rows_small, 128)
    parts = _exchange([pad(packed, 0.0)], False, "gather_small_grads")[0]
    n_rep = DEPTH * sum(sizes[:-1])
    pack_rep = lambda d, fill: pad(jnp.concatenate([d[k].reshape(-1) for k in SMALL]
                                                   + [jnp.full((n_small - n_rep,), fill, F32)]), fill)
    res = _adamw(parts, pack_rep(w, 1.0), pack_rep(m, 1.0), pack_rep(v, 1.0), "adamw_small")
    res = [t.reshape(-1) for t in res]
    off = 0
    for k, sz in zip(SMALL, sizes[:-1]):
        out[k] = [t[off:off + DEPTH * sz].reshape(w[k].shape) for t in res]
        off += DEPTH * sz
    gconv = res[0][off:off + DEPTH * sizes[-1]].reshape(DEPTH, SSD_CONV, SSD_CONV_CH)
    gconv = lax.dynamic_slice_in_dim(gconv, me * conv_w.shape[2], conv_w.shape[2], axis=2)
    flat = lambda t: t.reshape(8, -1)
    resc = _adamw(flat(gconv)[None], flat(conv_w), flat(m_conv_w), flat(v_conv_w), "adamw_conv_w")
    out["conv_w"] = [t.reshape(conv_w.shape) for t in resc]

    return (loss, gx[None], *[out[k][0] for k in ORDER], *[out[k][1] for k in ORDER],
            *[out[k][2] for k in ORDER], *[out[k][3] for k in ORDER])
```
